```python
import math
import jax, jax.numpy as jnp
from jax import lax
import numpy as np

D_MODEL = 1024
BATCH = 8
SEQ = 4096
DEPTH = 2

CHUNK = 64
Q_BLOCK = 128
D_MIX = D_MODEL
SSD_WIDTH = D_MIX // 2
SSD_HEAD_DIM = 64
SSD_HEADS = SSD_WIDTH // SSD_HEAD_DIM
SSD_GROUPS = 2
SSD_HPG = SSD_HEADS // SSD_GROUPS
SSD_STATE = 128
SSD_CONV = 4
SSD_CONV_DIM = SSD_WIDTH + 2 * SSD_GROUPS * SSD_STATE
SSD_IN = SSD_WIDTH + SSD_CONV_DIM + SSD_HEADS
FOX_WIDTH = D_MIX // 4
FOX_HEAD_DIM = 64
FOX_HEADS = FOX_WIDTH // FOX_HEAD_DIM
FOX_IN = 3 * FOX_WIDTH + FOX_HEADS
SCONV_WIDTH = D_MIX - SSD_WIDTH - FOX_WIDTH
SCONV_K = 3
SCONV_IN = 3 * SCONV_WIDTH
D_IN_PROJ = SSD_IN + FOX_IN + SCONV_IN
D_FF = 2816
ALPHA = (2 * DEPTH) ** 0.25
BETA = (8 * DEPTH) ** -0.25
LN_EPS = 1e-5
RMS_EPS = 1e-5
N_SUB = 3

kernel_name = "hybrid_ssd_fox_shortconv_macaron_deepnorm_adaln"


def layer_norm(x, g, b):
    xf = x.astype(jnp.float32)
    mu = jnp.mean(xf, axis=-1, keepdims=True)
    var = jnp.mean(jnp.square(xf - mu), axis=-1, keepdims=True)
    return ((xf - mu) * lax.rsqrt(var + LN_EPS) * g + b).astype(x.dtype)


def causal_depthwise_conv(x, w, b=None):
    k_w, ch = w.shape
    y = lax.conv_general_dilated(
        x, w[:, None, :], window_strides=(1,), padding=[(k_w - 1, 0)],
        dimension_numbers=("NWC", "WIO", "NWC"), feature_group_count=ch)
    return y if b is None else y + b


def swiglu_ffn(h, w_in, w_out):
    gate, up = jnp.split(h @ w_in, 2, axis=-1)
    return (jax.nn.silu(gate) * up) @ w_out


def segsum(a):
    t = a.shape[-1]
    cs = jnp.cumsum(a, axis=-1)
    diff = cs[..., :, None] - cs[..., None, :]
    mask = jnp.tril(jnp.ones((t, t), dtype=bool))
    return jnp.where(mask, diff, -jnp.inf)


def ssd_chunked_scan(xdt, a, bm, cm):
    b, seq, g, e, p = xdt.shape
    n = bm.shape[-1]
    nc = seq // CHUNK
    xdt = xdt.reshape(b, nc, CHUNK, g, e, p)
    bm = bm.reshape(b, nc, CHUNK, g, n)
    cm = cm.reshape(b, nc, CHUNK, g, n)
    a = jnp.transpose(a.reshape(b, nc, CHUNK, g, e), (0, 3, 4, 1, 2))
    a_cs = jnp.cumsum(a, axis=-1)
    l_mat = jnp.exp(segsum(a))
    cb = jnp.einsum("bclgn,bcsgn->bgcls", cm, bm)
    y_diag = jnp.einsum("bgcls,bgecls,bcsgep->bclgep", cb, l_mat, xdt)
    decay_states = jnp.exp(a_cs[..., -1:] - a_cs)
    states = jnp.einsum("bclgn,bgecl,bclgep->bcgepn", bm, decay_states, xdt)
    chunk_a = jnp.pad(a_cs[..., -1], ((0, 0), (0, 0), (0, 0), (1, 0)))
    decay_chunk = jnp.exp(segsum(chunk_a))
    states = jnp.concatenate([jnp.zeros_like(states[:, :1]), states], axis=1)
    states_in = jnp.einsum("bgezc,bcgepn->bzgepn", decay_chunk, states)[:, :-1]
    y_off = jnp.einsum("bclgn,bcgepn,bgecl->bclgep", cm, states_in, jnp.exp(a_cs))
    return (y_diag + y_off).reshape(b, seq, g, e, p)


def mamba2_group(proj, conv_w, conv_b, dt_bias, a_log, d_skip, norm_g):
    b, seq, _ = proj.shape
    z, xbc, dt_raw = jnp.split(proj, [SSD_WIDTH, SSD_WIDTH + SSD_CONV_DIM], axis=-1)
    xbc = jax.nn.silu(causal_depthwise_conv(xbc, conv_w, conv_b))
    xs, bm, cm = jnp.split(xbc, [SSD_WIDTH, SSD_WIDTH + SSD_GROUPS * SSD_STATE], axis=-1)
    xs = xs.astype(jnp.float32).reshape(b, seq, SSD_GROUPS, SSD_HPG, SSD_HEAD_DIM)
    bm = bm.astype(jnp.float32).reshape(b, seq, SSD_GROUPS, SSD_STATE)
    cm = cm.astype(jnp.float32).reshape(b, seq, SSD_GROUPS, SSD_STATE)
    dt = jax.nn.softplus(dt_raw.astype(jnp.float32) + dt_bias.astype(jnp.float32))
    dt = dt.reshape(b, seq, SSD_GROUPS, SSD_HPG)
    a_head = -jnp.exp(a_log.astype(jnp.float32)).reshape(SSD_GROUPS, SSD_HPG)
    y = ssd_chunked_scan(xs * dt[..., None], dt * a_head, bm, cm)
    y = y + d_skip.astype(jnp.float32).reshape(SSD_GROUPS, SSD_HPG)[:, :, None] * xs
    y = y.reshape(b, seq, SSD_WIDTH) * jax.nn.silu(z.astype(jnp.float32))
    yg = y.reshape(b, seq, SSD_GROUPS, SSD_WIDTH // SSD_GROUPS)
    yg = yg * lax.rsqrt(jnp.mean(jnp.square(yg), axis=-1, keepdims=True) + RMS_EPS)
    return (yg.reshape(b, seq, SSD_WIDTH) * norm_g).astype(proj.dtype)


def fox_group(proj, f_bias):
    b, seq, _ = proj.shape
    q, k, v, f = jnp.split(proj, [FOX_WIDTH, 2 * FOX_WIDTH, 3 * FOX_WIDTH], axis=-1)
    q = q.reshape(b, seq, FOX_HEADS, FOX_HEAD_DIM)
    k = k.reshape(b, seq, FOX_HEADS, FOX_HEAD_DIM)
    v = v.reshape(b, seq, FOX_HEADS, FOX_HEAD_DIM)
    log_f = jax.nn.log_sigmoid(f.astype(jnp.float32) + f_bias.astype(jnp.float32))
    cum_f = jnp.transpose(jnp.cumsum(log_f, axis=1), (0, 2, 1))
    scale = FOX_HEAD_DIM ** -0.5
    outs = []
    for i in range(seq // Q_BLOCK):
        q0, end = i * Q_BLOCK, (i + 1) * Q_BLOCK
        s = jnp.einsum("bqhd,bkhd->bhqk", q[:, q0:end], k[:, :end]).astype(jnp.float32) * scale
        s = s + cum_f[:, :, q0:end, None] - cum_f[:, :, None, :end]
        mask = jnp.arange(q0, end)[:, None] >= jnp.arange(end)[None, :]
        prob = jax.nn.softmax(jnp.where(mask, s, -jnp.inf), axis=-1)
        outs.append(jnp.einsum("bhqk,bkhd->bqhd", prob.astype(v.dtype), v[:, :end]))
    return jnp.concatenate(outs, axis=1).reshape(b, seq, FOX_WIDTH)


def shortconv_group(proj, conv_w):
    bg, cg, xin = jnp.split(proj, 3, axis=-1)
    return bg * causal_depthwise_conv(cg * xin, conv_w)


def modulate(x, shift, scale):
    return x * (1.0 + scale) + shift


def _fwd_setup_inputs(seed: int = 0) -> dict:
    key = jax.random.key(seed)
    ks = jax.random.split(key, 24)

    def nrm(k, shape, s):
        return jax.random.normal(k, shape, jnp.float32) * s

    dt0 = jnp.exp(jax.random.uniform(ks[12], (DEPTH, SSD_HEADS), jnp.float32,
                                     minval=math.log(1e-3), maxval=math.log(1e-1)))
    return {
        "x": nrm(ks[0], (BATCH, SEQ, D_MODEL), 1.0),
        "c": nrm(ks[1], (BATCH, D_MODEL), 1.0),
        "ln_in_g": 1.0 + nrm(ks[2], (D_MODEL,), 0.01),
        "ln_in_b": nrm(ks[3], (D_MODEL,), 0.01),
        "ada_w": nrm(ks[4], (DEPTH, D_MODEL, N_SUB * 3 * D_MODEL), 0.5 * D_MODEL ** -0.5),
        "ada_b": nrm(ks[5], (DEPTH, N_SUB * 3 * D_MODEL), 0.01),
        "ffn1_w_in": nrm(ks[6], (DEPTH, D_MODEL, 2 * D_FF), D_MODEL ** -0.5),
        "ffn1_w_out": nrm(ks[7], (DEPTH, D_FF, D_MODEL), BETA * D_FF ** -0.5),
        "mix_w_in": nrm(ks[8], (DEPTH, D_MODEL, D_IN_PROJ), D_MODEL ** -0.5),
        "mix_w_out": nrm(ks[9], (DEPTH, D_MIX, D_MODEL), BETA * D_MIX ** -0.5),
        "ssd_conv_w": nrm(ks[10], (DEPTH, SSD_CONV, SSD_CONV_DIM), SSD_CONV ** -0.5),
        "ssd_conv_b": nrm(ks[11], (DEPTH, SSD_CONV_DIM), 0.01),
        "ssd_dt_bias": dt0 + jnp.log(-jnp.expm1(-dt0)),
        "ssd_a_log": jnp.log(jax.random.uniform(ks[13], (DEPTH, SSD_HEADS), jnp.float32,
                                                 minval=1.0, maxval=16.0)),
        "ssd_d": 1.0 + nrm(ks[14], (DEPTH, SSD_HEADS), 0.01),
        "ssd_norm_g": 1.0 + nrm(ks[15], (DEPTH, SSD_WIDTH), 0.01),
        "fox_f_bias": jax.random.uniform(ks[16], (DEPTH, FOX_HEADS), jnp.float32,
                                         minval=1.0, maxval=5.0),
        "sconv_w": nrm(ks[17], (DEPTH, SCONV_K, SCONV_WIDTH), SCONV_K ** -0.5),
        "ffn2_w_in": nrm(ks[18], (DEPTH, D_MODEL, 2 * D_FF), D_MODEL ** -0.5),
        "ffn2_w_out": nrm(ks[19], (DEPTH, D_FF, D_MODEL), BETA * D_FF ** -0.5),
        "ln_g": 1.0 + nrm(ks[20], (DEPTH, N_SUB, D_MODEL), 0.01),
        "ln_b": nrm(ks[21], (DEPTH, N_SUB, D_MODEL), 0.01),
    }


def _fwd_reference(x, c, ln_in_g, ln_in_b, ada_w, ada_b, ffn1_w_in, ffn1_w_out,
              mix_w_in, mix_w_out, ssd_conv_w, ssd_conv_b, ssd_dt_bias, ssd_a_log,
              ssd_d, ssd_norm_g, fox_f_bias, sconv_w, ffn2_w_in, ffn2_w_out,
              ln_g, ln_b):
    b = x.shape[0]
    x = layer_norm(x, ln_in_g, ln_in_b)
    c_act = jax.nn.silu(c)
    for l in range(DEPTH):
        mod = (c_act @ ada_w[l] + ada_b[l]).reshape(b, N_SUB, 3, 1, D_MODEL)

        h = modulate(x, mod[:, 0, 0], mod[:, 0, 1])
        y = swiglu_ffn(h, ffn1_w_in[l], ffn1_w_out[l])
        x = layer_norm(ALPHA * x + 0.5 * mod[:, 0, 2] * y, ln_g[l, 0], ln_b[l, 0])

        h = modulate(x, mod[:, 1, 0], mod[:, 1, 1])
        proj = h @ mix_w_in[l]
        p_ssd, p_fox, p_sc = jnp.split(proj, [SSD_IN, SSD_IN + FOX_IN], axis=-1)
        y_ssd = mamba2_group(p_ssd, ssd_conv_w[l], ssd_conv_b[l], ssd_dt_bias[l],
                             ssd_a_log[l], ssd_d[l], ssd_norm_g[l])
        y_fox = fox_group(p_fox, fox_f_bias[l])
        y_sc = shortconv_group(p_sc, sconv_w[l])
        y = jnp.concatenate([y_ssd, y_fox, y_sc], axis=-1) @ mix_w_out[l]
        x = layer_norm(ALPHA * x + mod[:, 1, 2] * y, ln_g[l, 1], ln_b[l, 1])

        h = modulate(x, mod[:, 2, 0], mod[:, 2, 1])
        y = swiglu_ffn(h, ffn2_w_in[l], ffn2_w_out[l])
        x = layer_norm(ALPHA * x + 0.5 * mod[:, 2, 2] * y, ln_g[l, 2], ln_b[l, 2])
    return x


import jax as _jax
import jax.numpy as _jnp

TWIN_FORMAT = 'train_step'
FWD_PARAMS = ['x', 'c', 'ln_in_g', 'ln_in_b', 'ada_w', 'ada_b', 'ffn1_w_in', 'ffn1_w_out', 'mix_w_in', 'mix_w_out', 'ssd_conv_w', 'ssd_conv_b', 'ssd_dt_bias', 'ssd_a_log', 'ssd_d', 'ssd_norm_g', 'fox_f_bias', 'sconv_w', 'ffn2_w_in', 'ffn2_w_out', 'ln_g', 'ln_b']
TWIN_WEIGHTS = ['ln_in_g', 'ln_in_b', 'ada_w', 'ada_b', 'ffn1_w_in', 'ffn1_w_out', 'mix_w_in', 'mix_w_out', 'ssd_conv_w', 'ssd_conv_b', 'ssd_dt_bias', 'ssd_a_log', 'ssd_d', 'ssd_norm_g', 'fox_f_bias', 'sconv_w', 'ffn2_w_in', 'ffn2_w_out', 'ln_g', 'ln_b']
TWIN_DIFF_INPUT = 'x'
TWIN_INPUTS = ['x', 'c', 'ln_in_g', 'ln_in_b', 'ada_w', 'ada_b', 'ffn1_w_in', 'ffn1_w_out', 'mix_w_in', 'mix_w_out', 'ssd_conv_w', 'ssd_conv_b', 'ssd_dt_bias', 'ssd_a_log', 'ssd_d', 'ssd_norm_g', 'fox_f_bias', 'sconv_w', 'ffn2_w_in', 'ffn2_w_out', 'ln_g', 'ln_b', 'loss_target', 'm_ln_in_g', 'm_ln_in_b', 'm_ada_w', 'm_ada_b', 'm_ffn1_w_in', 'm_ffn1_w_out', 'm_mix_w_in', 'm_mix_w_out', 'm_ssd_conv_w', 'm_ssd_conv_b', 'm_ssd_dt_bias', 'm_ssd_a_log', 'm_ssd_d', 'm_ssd_norm_g', 'm_fox_f_bias', 'm_sconv_w', 'm_ffn2_w_in', 'm_ffn2_w_out', 'm_ln_g', 'm_ln_b', 'v_ln_in_g', 'v_ln_in_b', 'v_ada_w', 'v_ada_b', 'v_ffn1_w_in', 'v_ffn1_w_out', 'v_mix_w_in', 'v_mix_w_out', 'v_ssd_conv_w', 'v_ssd_conv_b', 'v_ssd_dt_bias', 'v_ssd_a_log', 'v_ssd_d', 'v_ssd_norm_g', 'v_fox_f_bias', 'v_sconv_w', 'v_ffn2_w_in', 'v_ffn2_w_out', 'v_ln_g', 'v_ln_b']
TWIN_OUTPUTS = ['loss', 'grad_x', 'grad_ln_in_g', 'grad_ln_in_b', 'grad_ada_w', 'grad_ada_b', 'grad_ffn1_w_in', 'grad_ffn1_w_out', 'grad_mix_w_in', 'grad_mix_w_out', 'grad_ssd_conv_w', 'grad_ssd_conv_b', 'grad_ssd_dt_bias', 'grad_ssd_a_log', 'grad_ssd_d', 'grad_ssd_norm_g', 'grad_fox_f_bias', 'grad_sconv_w', 'grad_ffn2_w_in', 'grad_ffn2_w_out', 'grad_ln_g', 'grad_ln_b', 'delta_ln_in_g', 'delta_ln_in_b', 'delta_ada_w', 'delta_ada_b', 'delta_ffn1_w_in', 'delta_ffn1_w_out', 'delta_mix_w_in', 'delta_mix_w_out', 'delta_ssd_conv_w', 'delta_ssd_conv_b', 'delta_ssd_dt_bias', 'delta_ssd_a_log', 'delta_ssd_d', 'delta_ssd_norm_g', 'delta_fox_f_bias', 'delta_sconv_w', 'delta_ffn2_w_in', 'delta_ffn2_w_out', 'delta_ln_g', 'delta_ln_b', 'new_m_ln_in_g', 'new_m_ln_in_b', 'new_m_ada_w', 'new_m_ada_b', 'new_m_ffn1_w_in', 'new_m_ffn1_w_out', 'new_m_mix_w_in', 'new_m_mix_w_out', 'new_m_ssd_conv_w', 'new_m_ssd_conv_b', 'new_m_ssd_dt_bias', 'new_m_ssd_a_log', 'new_m_ssd_d', 'new_m_ssd_norm_g', 'new_m_fox_f_bias', 'new_m_sconv_w', 'new_m_ffn2_w_in', 'new_m_ffn2_w_out', 'new_m_ln_g', 'new_m_ln_b', 'new_v_ln_in_g', 'new_v_ln_in_b', 'new_v_ada_w', 'new_v_ada_b', 'new_v_ffn1_w_in', 'new_v_ffn1_w_out', 'new_v_mix_w_in', 'new_v_mix_w_out', 'new_v_ssd_conv_w', 'new_v_ssd_conv_b', 'new_v_ssd_dt_bias', 'new_v_ssd_a_log', 'new_v_ssd_d', 'new_v_ssd_norm_g', 'new_v_fox_f_bias', 'new_v_sconv_w', 'new_v_ffn2_w_in', 'new_v_ffn2_w_out', 'new_v_ln_g', 'new_v_ln_b']
TWIN_LEAF_KINDS = {'loss': 'loss', 'grad_x': 'grad_x', 'grad_ln_in_g': 'grad_w', 'grad_ln_in_b': 'grad_w', 'grad_ada_w': 'grad_w', 'grad_ada_b': 'grad_w', 'grad_ffn1_w_in': 'grad_w', 'grad_ffn1_w_out': 'grad_w', 'grad_mix_w_in': 'grad_w', 'grad_mix_w_out': 'grad_w', 'grad_ssd_conv_w': 'grad_w', 'grad_ssd_conv_b': 'grad_w', 'grad_ssd_dt_bias': 'grad_w', 'grad_ssd_a_log': 'grad_w', 'grad_ssd_d': 'grad_w', 'grad_ssd_norm_g': 'grad_w', 'grad_fox_f_bias': 'grad_w', 'grad_sconv_w': 'grad_w', 'grad_ffn2_w_in': 'grad_w', 'grad_ffn2_w_out': 'grad_w', 'grad_ln_g': 'grad_w', 'grad_ln_b': 'grad_w', 'delta_ln_in_g': 'delta_w', 'delta_ln_in_b': 'delta_w', 'delta_ada_w': 'delta_w', 'delta_ada_b': 'delta_w', 'delta_ffn1_w_in': 'delta_w', 'delta_ffn1_w_out': 'delta_w', 'delta_mix_w_in': 'delta_w', 'delta_mix_w_out': 'delta_w', 'delta_ssd_conv_w': 'delta_w', 'delta_ssd_conv_b': 'delta_w', 'delta_ssd_dt_bias': 'delta_w', 'delta_ssd_a_log': 'delta_w', 'delta_ssd_d': 'delta_w', 'delta_ssd_norm_g': 'delta_w', 'delta_fox_f_bias': 'delta_w', 'delta_sconv_w': 'delta_w', 'delta_ffn2_w_in': 'delta_w', 'delta_ffn2_w_out': 'delta_w', 'delta_ln_g': 'delta_w', 'delta_ln_b': 'delta_w', 'new_m_ln_in_g': 'new_m', 'new_m_ln_in_b': 'new_m', 'new_m_ada_w': 'new_m', 'new_m_ada_b': 'new_m', 'new_m_ffn1_w_in': 'new_m', 'new_m_ffn1_w_out': 'new_m', 'new_m_mix_w_in': 'new_m', 'new_m_mix_w_out': 'new_m', 'new_m_ssd_conv_w': 'new_m', 'new_m_ssd_conv_b': 'new_m', 'new_m_ssd_dt_bias': 'new_m', 'new_m_ssd_a_log': 'new_m', 'new_m_ssd_d': 'new_m', 'new_m_ssd_norm_g': 'new_m', 'new_m_fox_f_bias': 'new_m', 'new_m_sconv_w': 'new_m', 'new_m_ffn2_w_in': 'new_m', 'new_m_ffn2_w_out': 'new_m', 'new_m_ln_g': 'new_m', 'new_m_ln_b': 'new_m', 'new_v_ln_in_g': 'new_v', 'new_v_ln_in_b': 'new_v', 'new_v_ada_w': 'new_v', 'new_v_ada_b': 'new_v', 'new_v_ffn1_w_in': 'new_v', 'new_v_ffn1_w_out': 'new_v', 'new_v_mix_w_in': 'new_v', 'new_v_mix_w_out': 'new_v', 'new_v_ssd_conv_w': 'new_v', 'new_v_ssd_conv_b': 'new_v', 'new_v_ssd_dt_bias': 'new_v', 'new_v_ssd_a_log': 'new_v', 'new_v_ssd_d': 'new_v', 'new_v_ssd_norm_g': 'new_v', 'new_v_fox_f_bias': 'new_v', 'new_v_sconv_w': 'new_v', 'new_v_ffn2_w_in': 'new_v', 'new_v_ffn2_w_out': 'new_v', 'new_v_ln_g': 'new_v', 'new_v_ln_b': 'new_v'}


def _forward(args):
    return _fwd_reference(*[args[k] for k in FWD_PARAMS])


def _output_shape():
    def fwd():
        inp = _fwd_setup_inputs(0)
        return _fwd_reference(*[inp[k] for k in FWD_PARAMS])
    out = _jax.eval_shape(fwd)
    return out.shape, out.dtype

N_MICROBATCH = 1
ADAM_LR = 0.001
ADAM_B1 = 0.9
ADAM_B2 = 0.999
ADAM_EPS = 1e-08
ADAM_WD = 0.01
ADAM_STEP = 10
PER_EXAMPLE_BATCH_AXIS = {'x': 0, 'c': 0, 'loss_target': 0}
SHARED_INPUTS = []
_WEIGHT_DTYPES = {'ln_in_g': _jnp.float32, 'ln_in_b': _jnp.float32, 'ada_w': _jnp.float32, 'ada_b': _jnp.float32, 'ffn1_w_in': _jnp.float32, 'ffn1_w_out': _jnp.float32, 'mix_w_in': _jnp.float32, 'mix_w_out': _jnp.float32, 'ssd_conv_w': _jnp.float32, 'ssd_conv_b': _jnp.float32, 'ssd_dt_bias': _jnp.float32, 'ssd_a_log': _jnp.float32, 'ssd_d': _jnp.float32, 'ssd_norm_g': _jnp.float32, 'fox_f_bias': _jnp.float32, 'sconv_w': _jnp.float32, 'ffn2_w_in': _jnp.float32, 'ffn2_w_out': _jnp.float32, 'ln_g': _jnp.float32, 'ln_b': _jnp.float32}
MOMENT_SCALE = {'ln_in_g': 5.315630e-01, 'ln_in_b': 3.065104e-01, 'ada_w': 1.649643e-02, 'ada_b': 2.779431e-02, 'ffn1_w_in': 4.227035e-03, 'ffn1_w_out': 1.377850e-02, 'mix_w_in': 1.698943e-02, 'mix_w_out': 3.698984e-02, 'ssd_conv_w': 1.412467e-02, 'ssd_conv_b': 1.753622e-02, 'ssd_dt_bias': 3.668995e-02, 'ssd_a_log': 4.782889e-02, 'ssd_d': 1.485606e-01, 'ssd_norm_g': 1.860852e-02, 'fox_f_bias': 3.005786e-02, 'sconv_w': 2.433024e-02, 'ffn2_w_in': 4.132918e-03, 'ffn2_w_out': 1.350984e-02, 'ln_g': 1.308711e+01, 'ln_b': 4.490232e-01}


def _to_microbatches(a, axis):
    t = _jnp.moveaxis(a, axis, 0)
    t = t.reshape((N_MICROBATCH, t.shape[0] // N_MICROBATCH) + t.shape[1:])
    return _jnp.moveaxis(t, 1, axis + 1)


def setup_inputs(seed: int = 0) -> dict:
    inp = _fwd_setup_inputs(seed)
    key = _jax.random.fold_in(_jax.random.key(seed), 7919)
    shape, _ = _output_shape()
    out = dict(inp)
    out["loss_target"] = _jax.random.normal(_jax.random.fold_in(key, 0), shape, _jnp.float32)
    for i, name in enumerate(TWIN_WEIGHTS):
        w = inp[name].astype(_jnp.float32)
        if MOMENT_SCALE is None:
            s = _jnp.sqrt(_jnp.mean(_jnp.square(w)) + 1e-30)
        else:
            s = MOMENT_SCALE[name]
        km, kv = _jax.random.split(_jax.random.fold_in(key, i + 1))
        out[name] = w
        out["m_" + name] = s * _jax.random.normal(km, w.shape, _jnp.float32)
        out["v_" + name] = (s * s) * _jax.random.uniform(kv, w.shape, _jnp.float32, 0.5, 1.5)
    if N_MICROBATCH > 1:
        for name, axis in PER_EXAMPLE_BATCH_AXIS.items():
            out[name] = _to_microbatches(out[name], axis)
    return {'x': out['x'], 'c': out['c'], 'ln_in_g': out['ln_in_g'], 'ln_in_b': out['ln_in_b'], 'ada_w': out['ada_w'], 'ada_b': out['ada_b'], 'ffn1_w_in': out['ffn1_w_in'], 'ffn1_w_out': out['ffn1_w_out'], 'mix_w_in': out['mix_w_in'], 'mix_w_out': out['mix_w_out'], 'ssd_conv_w': out['ssd_conv_w'], 'ssd_conv_b': out['ssd_conv_b'], 'ssd_dt_bias': out['ssd_dt_bias'], 'ssd_a_log': out['ssd_a_log'], 'ssd_d': out['ssd_d'], 'ssd_norm_g': out['ssd_norm_g'], 'fox_f_bias': out['fox_f_bias'], 'sconv_w': out['sconv_w'], 'ffn2_w_in': out['ffn2_w_in'], 'ffn2_w_out': out['ffn2_w_out'], 'ln_g': out['ln_g'], 'ln_b': out['ln_b'], 'loss_target': out['loss_target'], 'm_ln_in_g': out['m_ln_in_g'], 'm_ln_in_b': out['m_ln_in_b'], 'm_ada_w': out['m_ada_w'], 'm_ada_b': out['m_ada_b'], 'm_ffn1_w_in': out['m_ffn1_w_in'], 'm_ffn1_w_out': out['m_ffn1_w_out'], 'm_mix_w_in': out['m_mix_w_in'], 'm_mix_w_out': out['m_mix_w_out'], 'm_ssd_conv_w': out['m_ssd_conv_w'], 'm_ssd_conv_b': out['m_ssd_conv_b'], 'm_ssd_dt_bias': out['m_ssd_dt_bias'], 'm_ssd_a_log': out['m_ssd_a_log'], 'm_ssd_d': out['m_ssd_d'], 'm_ssd_norm_g': out['m_ssd_norm_g'], 'm_fox_f_bias': out['m_fox_f_bias'], 'm_sconv_w': out['m_sconv_w'], 'm_ffn2_w_in': out['m_ffn2_w_in'], 'm_ffn2_w_out': out['m_ffn2_w_out'], 'm_ln_g': out['m_ln_g'], 'm_ln_b': out['m_ln_b'], 'v_ln_in_g': out['v_ln_in_g'], 'v_ln_in_b': out['v_ln_in_b'], 'v_ada_w': out['v_ada_w'], 'v_ada_b': out['v_ada_b'], 'v_ffn1_w_in': out['v_ffn1_w_in'], 'v_ffn1_w_out': out['v_ffn1_w_out'], 'v_mix_w_in': out['v_mix_w_in'], 'v_mix_w_out': out['v_mix_w_out'], 'v_ssd_conv_w': out['v_ssd_conv_w'], 'v_ssd_conv_b': out['v_ssd_conv_b'], 'v_ssd_dt_bias': out['v_ssd_dt_bias'], 'v_ssd_a_log': out['v_ssd_a_log'], 'v_ssd_d': out['v_ssd_d'], 'v_ssd_norm_g': out['v_ssd_norm_g'], 'v_fox_f_bias': out['v_fox_f_bias'], 'v_sconv_w': out['v_sconv_w'], 'v_ffn2_w_in': out['v_ffn2_w_in'], 'v_ffn2_w_out': out['v_ffn2_w_out'], 'v_ln_g': out['v_ln_g'], 'v_ln_b': out['v_ln_b']}


def _loss(weights, diff, rest, loss_target):
    with _jax.named_scope("forward"):
        args = {**rest, TWIN_DIFF_INPUT: diff, **{k: w.astype(_WEIGHT_DTYPES[k]) for k, w in weights.items()}}
        y = _forward(args)
    with _jax.named_scope("loss_head"):
        err = _jnp.square(y.astype(_jnp.float32) - loss_target)
        return 0.5 * _jnp.sum(_jnp.mean(err, axis=-1)) if err.ndim else 0.5 * err


def _adamw(w, g, m, v):
    m = ADAM_B1 * m + (1.0 - ADAM_B1) * g
    v = ADAM_B2 * v + (1.0 - ADAM_B2) * _jnp.square(g)
    m_hat = m / (1.0 - ADAM_B1 ** ADAM_STEP)
    v_hat = v / (1.0 - ADAM_B2 ** ADAM_STEP)
    delta = -ADAM_LR * (m_hat / (_jnp.sqrt(v_hat) + ADAM_EPS) + ADAM_WD * w)
    return delta, m, v


def reference(x, c, ln_in_g, ln_in_b, ada_w, ada_b, ffn1_w_in, ffn1_w_out, mix_w_in, mix_w_out, ssd_conv_w, ssd_conv_b, ssd_dt_bias, ssd_a_log, ssd_d, ssd_norm_g, fox_f_bias, sconv_w, ffn2_w_in, ffn2_w_out, ln_g, ln_b, loss_target, m_ln_in_g, m_ln_in_b, m_ada_w, m_ada_b, m_ffn1_w_in, m_ffn1_w_out, m_mix_w_in, m_mix_w_out, m_ssd_conv_w, m_ssd_conv_b, m_ssd_dt_bias, m_ssd_a_log, m_ssd_d, m_ssd_norm_g, m_fox_f_bias, m_sconv_w, m_ffn2_w_in, m_ffn2_w_out, m_ln_g, m_ln_b, v_ln_in_g, v_ln_in_b, v_ada_w, v_ada_b, v_ffn1_w_in, v_ffn1_w_out, v_mix_w_in, v_mix_w_out, v_ssd_conv_w, v_ssd_conv_b, v_ssd_dt_bias, v_ssd_a_log, v_ssd_d, v_ssd_norm_g, v_fox_f_bias, v_sconv_w, v_ffn2_w_in, v_ffn2_w_out, v_ln_g, v_ln_b):
    given = dict(x=x, c=c, ln_in_g=ln_in_g, ln_in_b=ln_in_b, ada_w=ada_w, ada_b=ada_b, ffn1_w_in=ffn1_w_in, ffn1_w_out=ffn1_w_out, mix_w_in=mix_w_in, mix_w_out=mix_w_out, ssd_conv_w=ssd_conv_w, ssd_conv_b=ssd_conv_b, ssd_dt_bias=ssd_dt_bias, ssd_a_log=ssd_a_log, ssd_d=ssd_d, ssd_norm_g=ssd_norm_g, fox_f_bias=fox_f_bias, sconv_w=sconv_w, ffn2_w_in=ffn2_w_in, ffn2_w_out=ffn2_w_out, ln_g=ln_g, ln_b=ln_b, loss_target=loss_target, m_ln_in_g=m_ln_in_g, m_ln_in_b=m_ln_in_b, m_ada_w=m_ada_w, m_ada_b=m_ada_b, m_ffn1_w_in=m_ffn1_w_in, m_ffn1_w_out=m_ffn1_w_out, m_mix_w_in=m_mix_w_in, m_mix_w_out=m_mix_w_out, m_ssd_conv_w=m_ssd_conv_w, m_ssd_conv_b=m_ssd_conv_b, m_ssd_dt_bias=m_ssd_dt_bias, m_ssd_a_log=m_ssd_a_log, m_ssd_d=m_ssd_d, m_ssd_norm_g=m_ssd_norm_g, m_fox_f_bias=m_fox_f_bias, m_sconv_w=m_sconv_w, m_ffn2_w_in=m_ffn2_w_in, m_ffn2_w_out=m_ffn2_w_out, m_ln_g=m_ln_g, m_ln_b=m_ln_b, v_ln_in_g=v_ln_in_g, v_ln_in_b=v_ln_in_b, v_ada_w=v_ada_w, v_ada_b=v_ada_b, v_ffn1_w_in=v_ffn1_w_in, v_ffn1_w_out=v_ffn1_w_out, v_mix_w_in=v_mix_w_in, v_mix_w_out=v_mix_w_out, v_ssd_conv_w=v_ssd_conv_w, v_ssd_conv_b=v_ssd_conv_b, v_ssd_dt_bias=v_ssd_dt_bias, v_ssd_a_log=v_ssd_a_log, v_ssd_d=v_ssd_d, v_ssd_norm_g=v_ssd_norm_g, v_fox_f_bias=v_fox_f_bias, v_sconv_w=v_sconv_w, v_ffn2_w_in=v_ffn2_w_in, v_ffn2_w_out=v_ffn2_w_out, v_ln_g=v_ln_g, v_ln_b=v_ln_b)
    weights = {n: given[n] for n in TWIN_WEIGHTS}
    shared = {n: given[n] for n in SHARED_INPUTS}
    per_example = {n: given[n] for n in ['x', 'c']}
    grad_fn = _jax.value_and_grad(_loss, argnums=(0, 1))

    def one_microbatch(ex, loss_target):
        ex = dict(ex)
        diff = ex.pop(TWIN_DIFF_INPUT)
        return grad_fn(weights, diff, {**shared, **ex}, loss_target)

    if N_MICROBATCH == 1:
        loss, (grad_w, grad_x) = one_microbatch(per_example, given["loss_target"])
    else:
        def body(carry, xs):
            loss_sum, grad_sum = carry
            l_k, (gw_k, gx_k) = one_microbatch(xs[0], xs[1])
            with _jax.named_scope("update"):
                return (loss_sum + l_k, _jax.tree.map(_jnp.add, grad_sum, gw_k)), gx_k

        init = (_jnp.zeros((), _jnp.float32), _jax.tree.map(_jnp.zeros_like, weights))
        (loss, grad_w), grad_x = _jax.lax.scan(body, init, (per_example, given["loss_target"]))
    with _jax.named_scope("update"):
        delta_w, new_m, new_v = {}, {}, {}
        for n in TWIN_WEIGHTS:
            delta_w[n], new_m[n], new_v[n] = _adamw(weights[n], grad_w[n], given["m_" + n], given["v_" + n])
    return (loss, grad_x, *[grad_w[n] for n in TWIN_WEIGHTS], *[delta_w[n] for n in TWIN_WEIGHTS],
            *[new_m[n] for n in TWIN_WEIGHTS], *[new_v[n] for n in TWIN_WEIGHTS])
```

```python
import functools

import jax
import jax.numpy as jnp
from jax import lax
from jax.experimental import pallas as pl
from jax.experimental.pallas import tpu as pltpu

f32, bf16 = jnp.float32, jnp.bfloat16

D = 1024
F = 2816
DEPTH = 2
N_DEV = 8
SSD_W, SSD_HD, SSD_H, SSD_G, SSD_N, SSD_K = 512, 64, 8, 2, 128, 4
FOX_W, FOX_HD, FOX_H = 256, 64, 4
SC_W, SC_K = 256, 3
ALPHA = (2 * DEPTH) ** 0.25
LN_EPS = 1e-5
RMS_EPS = 1e-5
P_XBC, P_Z, P_Q, P_K, P_V, P_SB, P_SC, P_SX, P_SM = 0, 1024, 1536, 1792, 2048, 2304, 2560, 2816, 3072
P_W = 3200
SM_DT, SM_F = 0, 8
ADAM_LR, ADAM_B1, ADAM_B2, ADAM_EPS, ADAM_WD, ADAM_STEP = 0.001, 0.9, 0.999, 1e-08, 0.01, 10

VMEM_LIMIT = 56 * 1024 * 1024


def _cp(sem=None):
    return pltpu.CompilerParams(dimension_semantics=sem, vmem_limit_bytes=VMEM_LIMIT)


def _const_spec(shape):
    nd = len(shape)
    return pl.BlockSpec(shape, lambda *_: (0,) * nd, pipeline_mode=pl.Buffered(1))


def _sigmoid(x):
    return 1.0 / (1.0 + jnp.exp(-x))


def _ln_fwd(u, g, b):
    mu = jnp.mean(u, -1, keepdims=True)
    xc = u - mu
    rstd = lax.rsqrt(jnp.mean(xc * xc, -1, keepdims=True) + LN_EPS)
    xhat = xc * rstd
    return xhat * g + b, xhat, rstd


def _ln_bwd(dout, xhat, rstd, g):
    dxh = dout * g
    m1 = jnp.mean(dxh, -1, keepdims=True)
    m2 = jnp.mean(dxh * xhat, -1, keepdims=True)
    du = rstd * (dxh - m1 - xhat * m2)
    return du, jnp.sum(dout * xhat, 0, keepdims=True), jnp.sum(dout, 0, keepdims=True)


def _dot(a, b):
    return jnp.dot(a, b, preferred_element_type=f32)


def _dot_nt(a, b):
    return lax.dot_general(a, b, (((1,), (1,)), ((), ())), preferred_element_type=f32)


def _dot_tn(a, b):
    return lax.dot_general(a, b, (((0,), (0,)), ((), ())), preferred_element_type=f32)


def _dot_hi(a, b):
    return jnp.dot(a, b, preferred_element_type=f32, precision=lax.Precision.HIGHEST)


def _shift_down(cur, prev, s):
    if s == 0:
        return cur
    row = lax.broadcasted_iota(jnp.int32, cur.shape, 0)
    return jnp.where(row < s, pltpu.roll(prev, s, 0), pltpu.roll(cur, s, 0))


def _shift_up(cur, nxt, s):
    if s == 0:
        return cur
    t = cur.shape[0]
    row = lax.broadcasted_iota(jnp.int32, cur.shape, 0)
    return jnp.where(row >= t - s, pltpu.roll(nxt, t - s, 0), pltpu.roll(cur, t - s, 0))


def _ln_in_fwd(x, gb, *, tt=512):
    L = x.shape[0]

    def body(x_ref, gb_ref, o_ref):
        o_ref[...] = _ln_fwd(x_ref[...], gb_ref[0:1, :], gb_ref[1:2, :])[0]

    return pl.pallas_call(
        body, name="ln_in_fwd", grid=(L // tt,),
        in_specs=[pl.BlockSpec((tt, D), lambda i: (i, 0)), _const_spec((8, D))],
        out_specs=pl.BlockSpec((tt, D), lambda i: (i, 0)),
        out_shape=jax.ShapeDtypeStruct((L, D), f32), compiler_params=_cp(("parallel",)))(x, gb)


def _ln_in_bwd(x, dy, gb, *, tt=512):
    L = x.shape[0]

    def body(x_ref, dy_ref, gb_ref, dx_ref, acc_ref):
        @pl.when(pl.program_id(0) == 0)
        def _():
            acc_ref[...] = jnp.zeros_like(acc_ref)
        _, xhat, rstd = _ln_fwd(x_ref[...], gb_ref[0:1, :], gb_ref[1:2, :])
        du, dg, db = _ln_bwd(dy_ref[...], xhat, rstd, gb_ref[0:1, :])
        dx_ref[...] = du
        acc_ref[0:1, :] += dg
        acc_ref[1:2, :] += db

    return pl.pallas_call(
        body, name="ln_in_bwd", grid=(L // tt,),
        in_specs=[pl.BlockSpec((tt, D), lambda i: (i, 0)), pl.BlockSpec((tt, D), lambda i: (i, 0)), _const_spec((8, D))],
        out_specs=[pl.BlockSpec((tt, D), lambda i: (i, 0)), pl.BlockSpec((8, D), lambda i: (0, 0))],
        out_shape=[jax.ShapeDtypeStruct((L, D), f32), jax.ShapeDtypeStruct((8, D), f32)],
        compiler_params=_cp(("arbitrary",)))(x, dy, gb)


def _loss_head(y, tgt, *, tt=512):
    L = y.shape[0]

    def body(y_ref, t_ref, dy_ref, acc_ref):
        @pl.when(pl.program_id(0) == 0)
        def _():
            acc_ref[...] = jnp.zeros_like(acc_ref)
        e = y_ref[...] - t_ref[...]
        dy_ref[...] = e * (1.0 / D)
        acc_ref[...] += 0.5 * jnp.sum(jnp.mean(e * e, -1, keepdims=True))

    return pl.pallas_call(
        body, name="loss_head", grid=(L // tt,),
        in_specs=[pl.BlockSpec((tt, D), lambda i: (i, 0)), pl.BlockSpec((tt, D), lambda i: (i, 0))],
        out_specs=[pl.BlockSpec((tt, D), lambda i: (i, 0)), pl.BlockSpec((8, 128), lambda i: (0, 0))],
        out_shape=[jax.ShapeDtypeStruct((L, D), f32), jax.ShapeDtypeStruct((8, 128), f32)],
        compiler_params=_cp(("arbitrary",)))(y, tgt)


FFN_CH = 4
FS = F // FFN_CH


def _ffn_fwd(x, mv, w_in, w_out, *, tt=256):
    L = x.shape[0]

    def body(x_ref, mv_ref, wi_ref, wo_ref, o_ref):
        x = x_ref[...]
        h = (x * (1.0 + mv_ref[1:2, :]) + mv_ref[0:1, :]).astype(bf16)
        y = jnp.zeros((tt, D), f32)
        for c in range(FFN_CH):
            g = _dot(h, wi_ref[c])
            u = _dot(h, wi_ref[c + FFN_CH])
            act = (g * _sigmoid(g) * u).astype(bf16)
            y = y + _dot(act, wo_ref[c * FS:(c + 1) * FS, :])
        uu = ALPHA * x + (0.5 * mv_ref[2:3, :]) * y
        o_ref[...] = _ln_fwd(uu, mv_ref[3:4, :], mv_ref[4:5, :])[0]

    return pl.pallas_call(
        body, name="ffn_fwd", grid=(L // tt,),
        in_specs=[pl.BlockSpec((tt, D), lambda i: (i, 0)), _const_spec((8, D)),
                  _const_spec((2 * FFN_CH, D, FS)), _const_spec((F, D))],
        out_specs=pl.BlockSpec((tt, D), lambda i: (i, 0)),
        out_shape=jax.ShapeDtypeStruct((L, D), f32), compiler_params=_cp(("parallel",)))(x, mv, w_in, w_out)


def _ffn_bwd(x, dxo, mv, w_in, w_out, *, tt=256):
    L = x.shape[0]

    def body(x_ref, dxo_ref, mv_ref, wi_ref, wo_ref, dx_ref, h_ref, da_ref, act_ref, dy_ref, acc_ref, a_scr):
        @pl.when(pl.program_id(0) == 0)
        def _():
            acc_ref[...] = jnp.zeros_like(acc_ref)
        x = x_ref[...]
        scale1 = 1.0 + mv_ref[1:2, :]
        h = (x * scale1 + mv_ref[0:1, :]).astype(bf16)
        h_ref[...] = h
        y = jnp.zeros((tt, D), f32)
        for c in range(FFN_CH):
            g = _dot(h, wi_ref[c])
            u = _dot(h, wi_ref[c + FFN_CH])
            a_scr[c] = g
            a_scr[c + FFN_CH] = u
            act = (g * _sigmoid(g) * u).astype(bf16)
            act_ref[c] = act
            y = y + _dot(act, wo_ref[c * FS:(c + 1) * FS, :])
        hg = 0.5 * mv_ref[2:3, :]
        _, xhat, rstd = _ln_fwd(ALPHA * x + hg * y, mv_ref[3:4, :], mv_ref[4:5, :])
        du, dlg, dlb = _ln_bwd(dxo_ref[...], xhat, rstd, mv_ref[3:4, :])
        acc_ref[3:4, :] += dlg
        acc_ref[4:5, :] += dlb
        acc_ref[2:3, :] += jnp.sum(0.5 * y * du, 0, keepdims=True)
        dyb = (hg * du).astype(bf16)
        dy_ref[...] = dyb
        dh = jnp.zeros((tt, D), f32)
        for c in range(FFN_CH):
            g = a_scr[c]
            u = a_scr[c + FFN_CH]
            dact = _dot_nt(dyb, wo_ref[c * FS:(c + 1) * FS, :])
            s = _sigmoid(g)
            dg = (dact * u * (s * (1.0 + g * (1.0 - s)))).astype(bf16)
            dup = (dact * (g * s)).astype(bf16)
            da_ref[c] = dg
            da_ref[c + FFN_CH] = dup
            dh = dh + _dot_nt(dg, wi_ref[c])
            dh = dh + _dot_nt(dup, wi_ref[c + FFN_CH])
        dx_ref[...] = ALPHA * du + dh * scale1
        acc_ref[0:1, :] += jnp.sum(dh, 0, keepdims=True)
        acc_ref[1:2, :] += jnp.sum(dh * x, 0, keepdims=True)

    tok = lambda w: pl.BlockSpec((tt, w), lambda i: (i, 0))
    by_chunk = lambda n: pl.BlockSpec((n, tt, FS), lambda i: (0, i, 0))
    return pl.pallas_call(
        body, name="ffn_bwd", grid=(L // tt,),
        in_specs=[tok(D), tok(D), _const_spec((8, D)), _const_spec((2 * FFN_CH, D, FS)), _const_spec((F, D))],
        out_specs=[tok(D), tok(D), by_chunk(2 * FFN_CH), by_chunk(FFN_CH), tok(D), pl.BlockSpec((8, D), lambda i: (0, 0))],
        out_shape=[jax.ShapeDtypeStruct((L, D), f32), jax.ShapeDtypeStruct((L, D), bf16),
                   jax.ShapeDtypeStruct((2 * FFN_CH, L, FS), bf16), jax.ShapeDtypeStruct((FFN_CH, L, FS), bf16),
                   jax.ShapeDtypeStruct((L, D), bf16), jax.ShapeDtypeStruct((8, D), f32)],
        scratch_shapes=[pltpu.VMEM((2 * FFN_CH, tt, FS), f32)],
        compiler_params=_cp(("arbitrary",)))(x, dxo, mv, w_in, w_out)


def _matmul_tn(a, b, *, tn, tk, name):
    ga, gb = a.ndim == 3, b.ndim == 3
    G = a.shape[0] if ga else (b.shape[0] if gb else 1)
    K, M = a.shape[-2:]
    N = b.shape[-1]
    nk = K // tk

    def body(a_ref, b_ref, o_ref, acc):
        k = pl.program_id(2)
        p = _dot_tn(a_ref[...], b_ref[...])

        @pl.when(k == 0)
        def _():
            acc[...] = p

        @pl.when(k > 0)
        def _():
            acc[...] += p

        @pl.when(k == nk - 1)
        def _():
            o_ref[...] = acc[...].astype(bf16)

    a_spec = (pl.BlockSpec((None, tk, M), lambda g, j, k: (g, k, 0)) if ga
              else pl.BlockSpec((tk, M), lambda g, j, k: (k, 0)))
    b_spec = (pl.BlockSpec((None, tk, tn), lambda g, j, k: (g, k, j)) if gb
              else pl.BlockSpec((tk, tn), lambda g, j, k: (k, j)))
    if ga or gb:
        o_spec, o_shape = pl.BlockSpec((None, M, tn), lambda g, j, k: (g, 0, j)), (G, M, N)
    else:
        o_spec, o_shape = pl.BlockSpec((M, tn), lambda g, j, k: (0, j)), (M, N)
    return pl.pallas_call(
        body, name=name, grid=(G, N // tn, nk), in_specs=[a_spec, b_spec], out_specs=o_spec,
        out_shape=jax.ShapeDtypeStruct(o_shape, bf16), scratch_shapes=[pltpu.VMEM((M, tn), f32)],
        compiler_params=_cp(("parallel", "parallel", "arbitrary")))(a, b)


def _inproj_fwd(x, mv, w, *, tt=512):
    L = x.shape[0]

    def body(x_ref, mv_ref, w_ref, o_ref):
        h = (x_ref[...] * (1.0 + mv_ref[1:2, :]) + mv_ref[0:1, :]).astype(bf16)
        o_ref[...] = _dot(h, w_ref[...])

    return pl.pallas_call(
        body, name="inproj_fwd", grid=(L // tt,),
        in_specs=[pl.BlockSpec((tt, D), lambda i: (i, 0)), _const_spec((8, D)), _const_spec((D, P_W))],
        out_specs=pl.BlockSpec((tt, P_W), lambda i: (i, 0)),
        out_shape=jax.ShapeDtypeStruct((L, P_W), f32), compiler_params=_cp(("parallel",)))(x, mv, w)


def _inproj_bwd(x, dx_part, dproj, mv, w, *, tt=512):
    L = x.shape[0]

    def body(x_ref, dxp_ref, dp_ref, mv_ref, w_ref, dx_ref, h_ref, acc_ref):
        @pl.when(pl.program_id(0) == 0)
        def _():
            acc_ref[...] = jnp.zeros_like(acc_ref)
        x = x_ref[...]
        scale1 = 1.0 + mv_ref[1:2, :]
        h_ref[...] = (x * scale1 + mv_ref[0:1, :]).astype(bf16)
        dh = _dot_nt(dp_ref[...], w_ref[...])
        dx_ref[...] = dxp_ref[...] + dh * scale1
        acc_ref[0:1, :] += jnp.sum(dh, 0, keepdims=True)
        acc_ref[1:2, :] += jnp.sum(dh * x, 0, keepdims=True)

    tok = lambda w_: pl.BlockSpec((tt, w_), lambda i: (i, 0))
    return pl.pallas_call(
        body, name="inproj_bwd", grid=(L // tt,),
        in_specs=[tok(D), tok(D), tok(P_W), _const_spec((8, D)), _const_spec((D, P_W))],
        out_specs=[tok(D), tok(D), pl.BlockSpec((8, D), lambda i: (0, 0))],
        out_shape=[jax.ShapeDtypeStruct((L, D), f32), jax.ShapeDtypeStruct((L, D), bf16),
                   jax.ShapeDtypeStruct((8, D), f32)],
        compiler_params=_cp(("arbitrary",)))(x, dx_part, dproj, mv, w)


def _outproj_fwd(x, ycat, mv, w, *, tt=512):
    L = x.shape[0]

    def body(x_ref, y_ref, mv_ref, w_ref, o_ref):
        y = _dot(y_ref[...], w_ref[...])
        uu = ALPHA * x_ref[...] + mv_ref[2:3, :] * y
        o_ref[...] = _ln_fwd(uu, mv_ref[3:4, :], mv_ref[4:5, :])[0]

    tok = lambda w_: pl.BlockSpec((tt, w_), lambda i: (i, 0))
    return pl.pallas_call(
        body, name="outproj_fwd", grid=(L // tt,),
        in_specs=[tok(D), tok(D), _const_spec((8, D)), _const_spec((D, D))],
        out_specs=tok(D),
        out_shape=jax.ShapeDtypeStruct((L, D), f32), compiler_params=_cp(("parallel",)))(x, ycat, mv, w)


def _outproj_bwd(x, ycat, dxo, mv, w, *, tt=512):
    L = x.shape[0]

    def body(x_ref, y_ref, dxo_ref, mv_ref, w_ref, dx_ref, dy_ref, dyc_ref, acc_ref):
        @pl.when(pl.program_id(0) == 0)
        def _():
            acc_ref[...] = jnp.zeros_like(acc_ref)
        y = _dot(y_ref[...], w_ref[...])
        gate = mv_ref[2:3, :]
        _, xhat, rstd = _ln_fwd(ALPHA * x_ref[...] + gate * y, mv_ref[3:4, :], mv_ref[4:5, :])
        du, dlg, dlb = _ln_bwd(dxo_ref[...], xhat, rstd, mv_ref[3:4, :])
        acc_ref[3:4, :] += dlg
        acc_ref[4:5, :] += dlb
        acc_ref[2:3, :] += jnp.sum(y * du, 0, keepdims=True)
        dx_ref[...] = ALPHA * du
        dyb = (gate * du).astype(bf16)
        dy_ref[...] = dyb
        dyc_ref[...] = _dot_nt(dyb, w_ref[...])

    tok = lambda w_: pl.BlockSpec((tt, w_), lambda i: (i, 0))
    return pl.pallas_call(
        body, name="outproj_bwd", grid=(L // tt,),
        in_specs=[tok(D), tok(D), tok(D), _const_spec((8, D)), _const_spec((D, D))],
        out_specs=[tok(D), tok(D), tok(D), pl.BlockSpec((8, D), lambda i: (0, 0))],
        out_shape=[jax.ShapeDtypeStruct((L, D), f32), jax.ShapeDtypeStruct((L, D), bf16),
                   jax.ShapeDtypeStruct((L, D), f32), jax.ShapeDtypeStruct((8, D), f32)],
        compiler_params=_cp(("arbitrary",)))(x, ycat, dxo, mv, w)


def _sconv_fwd(proj, w, *, tt=512):
    L = proj.shape[0]
    cb = SC_W

    def body(b_ref, c_ref, x_ref, cp_ref, xp_ref, w_ref, o_ref):
        first = jnp.where(pl.program_id(0) > 0, 1.0, 0.0)
        u = c_ref[...] * x_ref[...]
        up = cp_ref[...] * xp_ref[...] * first
        v = w_ref[2:3, :] * u + w_ref[1:2, :] * _shift_down(u, up, 1) + w_ref[0:1, :] * _shift_down(u, up, 2)
        o_ref[...] = (b_ref[...] * v).astype(bf16)

    cur = lambda col: pl.BlockSpec((tt, cb), lambda i: (i, col // cb))
    prev = lambda col: pl.BlockSpec((tt, cb), lambda i: (jnp.maximum(i - 1, 0), col // cb))
    return pl.pallas_call(
        body, name="sconv_fwd", grid=(L // tt,),
        in_specs=[cur(P_SB), cur(P_SC), cur(P_SX), prev(P_SC), prev(P_SX), _const_spec((8, cb))],
        out_specs=pl.BlockSpec((tt, cb), lambda i: (i, 0)),
        out_shape=jax.ShapeDtypeStruct((L, cb), bf16), compiler_params=_cp(("parallel",)))(proj, proj, proj, proj, proj, w)


def _sconv_bwd(proj, dycat, w, *, tt=512):
    L = proj.shape[0]
    cb = SC_W
    n = L // tt

    def body(b_ref, c_ref, x_ref, cp_ref, xp_ref, bn_ref, dy_ref, dyn_ref, w_ref, db_ref, dc_ref, dx_ref, acc_ref):
        i = pl.program_id(0)

        @pl.when(i == 0)
        def _():
            acc_ref[...] = jnp.zeros_like(acc_ref)
        first = jnp.where(i > 0, 1.0, 0.0)
        last = jnp.where(i < n - 1, 1.0, 0.0)
        cg, xin, bg = c_ref[...], x_ref[...], b_ref[...]
        u = cg * xin
        up = cp_ref[...] * xp_ref[...] * first
        u1, u2 = _shift_down(u, up, 1), _shift_down(u, up, 2)
        v = w_ref[2:3, :] * u + w_ref[1:2, :] * u1 + w_ref[0:1, :] * u2
        dy = dy_ref[...]
        db_ref[...] = (dy * v).astype(bf16)
        dv = dy * bg
        dvn = dyn_ref[...] * bn_ref[...] * last
        du = w_ref[2:3, :] * dv + w_ref[1:2, :] * _shift_up(dv, dvn, 1) + w_ref[0:1, :] * _shift_up(dv, dvn, 2)
        acc_ref[2:3, :] += jnp.sum(dv * u, 0, keepdims=True)
        acc_ref[1:2, :] += jnp.sum(dv * u1, 0, keepdims=True)
        acc_ref[0:1, :] += jnp.sum(dv * u2, 0, keepdims=True)
        dc_ref[...] = (du * xin).astype(bf16)
        dx_ref[...] = (du * cg).astype(bf16)

    cur = lambda col: pl.BlockSpec((tt, cb), lambda i: (i, col // cb))
    prev = lambda col: pl.BlockSpec((tt, cb), lambda i: (jnp.maximum(i - 1, 0), col // cb))
    nxt = lambda col: pl.BlockSpec((tt, cb), lambda i: (jnp.minimum(i + 1, n - 1), col // cb))
    ycol = SSD_W + FOX_W
    out = pl.BlockSpec((tt, cb), lambda i: (i, 0))
    return pl.pallas_call(
        body, name="sconv_bwd", grid=(n,),
        in_specs=[cur(P_SB), cur(P_SC), cur(P_SX), prev(P_SC), prev(P_SX), nxt(P_SB), cur(ycol), nxt(ycol),
                  _const_spec((8, cb))],
        out_specs=[out, out, out, pl.BlockSpec((8, cb), lambda i: (0, 0))],
        out_shape=[jax.ShapeDtypeStruct((L, cb), bf16)] * 3 + [jax.ShapeDtypeStruct((8, cb), f32)],
        compiler_params=_cp(("arbitrary",)))(proj, proj, proj, proj, proj, proj, dycat, dycat, w)


def _log1pexp(x):
    return jnp.log(1.0 + jnp.exp(-jnp.abs(x)))


def _fox_gate_fwd(proj, hp, *, tt=256):
    L = proj.shape[0]

    def body(sm_ref, hp_ref, cum_ref, cumt_ref, carry):
        @pl.when(pl.program_id(0) == 0)
        def _():
            carry[...] = jnp.zeros_like(carry)
        xx = sm_ref[...] + hp_ref[3:4, :]
        logf = jnp.minimum(xx, 0.0) - _log1pexp(xx)
        r = lax.broadcasted_iota(jnp.int32, (tt, tt), 0)
        c = lax.broadcasted_iota(jnp.int32, (tt, tt), 1)
        cum = _dot_hi(jnp.where(r >= c, 1.0, 0.0), logf) + carry[0:1, :]
        cum_ref[...] = cum
        cumt_ref[...] = cum.T
        carry[0:1, :] = cum[tt - 1:tt, :]

    return pl.pallas_call(
        body, name="fox_gate_fwd", grid=(L // tt,),
        in_specs=[pl.BlockSpec((tt, 128), lambda i: (i, P_SM // 128)), _const_spec((8, 128))],
        out_specs=[pl.BlockSpec((tt, 128), lambda i: (i, 0)), pl.BlockSpec((128, tt), lambda i: (0, i))],
        out_shape=[jax.ShapeDtypeStruct((L, 128), f32), jax.ShapeDtypeStruct((128, L), f32)],
        scratch_shapes=[pltpu.VMEM((8, 128), f32)],
        compiler_params=_cp(("arbitrary",)))(proj, hp)


def _fox_gate_bwd(dcumt, proj, ddt, hp, *, tt=256):
    L = proj.shape[0]
    n = L // tt

    def body(dct_ref, sm_ref, ddt_ref, hp_ref, dsm_ref, acc_ref, carry):
        @pl.when(pl.program_id(0) == 0)
        def _():
            carry[...] = jnp.zeros_like(carry)
            acc_ref[...] = jnp.zeros_like(acc_ref)
        dc = dct_ref[...].T
        r = lax.broadcasted_iota(jnp.int32, (tt, tt), 0)
        c = lax.broadcasted_iota(jnp.int32, (tt, tt), 1)
        dl = _dot_hi(jnp.where(r <= c, 1.0, 0.0), dc) + carry[0:1, :]
        carry[0:1, :] += jnp.sum(dc, 0, keepdims=True)
        xx = sm_ref[...] + hp_ref[3:4, :]
        lane = lax.broadcasted_iota(jnp.int32, (tt, 128), 1)
        dlogit = jnp.where((lane >= SM_F) & (lane < SM_F + FOX_H), dl * _sigmoid(-xx), 0.0)
        acc_ref[3:4, :] += jnp.sum(dlogit, 0, keepdims=True)
        dsm_ref[...] = (dlogit + ddt_ref[...]).astype(bf16)

    return pl.pallas_call(
        body, name="fox_gate_bwd", grid=(n,),
        in_specs=[pl.BlockSpec((128, tt), lambda i: (0, n - 1 - i)),
                  pl.BlockSpec((tt, 128), lambda i: (n - 1 - i, P_SM // 128)),
                  pl.BlockSpec((tt, 128), lambda i: (n - 1 - i, 0)), _const_spec((8, 128))],
        out_specs=[pl.BlockSpec((tt, 128), lambda i: (n - 1 - i, 0)), pl.BlockSpec((8, 128), lambda i: (0, 0))],
        out_shape=[jax.ShapeDtypeStruct((L, 128), bf16), jax.ShapeDtypeStruct((8, 128), f32)],
        scratch_shapes=[pltpu.VMEM((8, 128), f32)],
        compiler_params=_cp(("arbitrary",)))(dcumt, proj, ddt, hp)


NEG = -1e30
FOX_SCALE = FOX_HD ** -0.5


def _fox_fwd(proj, cum, cumt, *, tq=256):
    L = proj.shape[0]

    def body(q_ref, k_ref, v_ref, cq_ref, ct_ref, o_ref, lse_ref):
        i = pl.program_id(0)
        row = lax.broadcasted_iota(jnp.int32, (tq, tq), 0)
        col = lax.broadcasted_iota(jnp.int32, (tq, tq), 1)
        lane = lax.broadcasted_iota(jnp.int32, (tq, 128), 1)
        lse_all = jnp.zeros((tq, 128), f32)
        for h in range(FOX_H):
            hs = slice(h * FOX_HD, (h + 1) * FOX_HD)
            qh = (q_ref[:, hs] * FOX_SCALE).astype(bf16)
            cq = cq_ref[:, SM_F + h:SM_F + h + 1]

            def step(j, carry, hs=hs, qh=qh, cq=cq, h=h):
                m, l, acc = carry
                r0 = pl.multiple_of(j * tq, tq)
                kj = k_ref[pl.ds(r0, tq), hs].astype(bf16)
                vj = v_ref[pl.ds(r0, tq), hs].astype(bf16)
                ck = ct_ref[SM_F + h:SM_F + h + 1, pl.ds(r0, tq)]
                s = _dot_nt(qh, kj) + (cq - ck)
                s = jnp.where(row + i * tq >= col + j * tq, s, NEG)
                m_new = jnp.maximum(m, jnp.max(s, -1, keepdims=True))
                p = jnp.exp(s - m_new)
                al = jnp.exp(m - m_new)
                return m_new, al * l + jnp.sum(p, -1, keepdims=True), al * acc + _dot(p.astype(bf16), vj)

            m, l, acc = lax.fori_loop(
                0, i + 1, step,
                (jnp.full((tq, 1), NEG, f32), jnp.zeros((tq, 1), f32), jnp.zeros((tq, FOX_HD), f32)))
            o_ref[:, hs] = acc / l
            lse_all = jnp.where(lane == h, m + jnp.log(l), lse_all)
        lse_ref[...] = lse_all

    return pl.pallas_call(
        body, name="fox_fwd", grid=(L // tq,),
        in_specs=[pl.BlockSpec((tq, FOX_W), lambda i: (i, P_Q // FOX_W)),
                  pl.BlockSpec((L, FOX_W), lambda i: (0, P_K // FOX_W), pipeline_mode=pl.Buffered(1)),
                  pl.BlockSpec((L, FOX_W), lambda i: (0, P_V // FOX_W), pipeline_mode=pl.Buffered(1)),
                  pl.BlockSpec((tq, 128), lambda i: (i, 0)), _const_spec((128, L))],
        out_specs=[pl.BlockSpec((tq, FOX_W), lambda i: (i, 0)), pl.BlockSpec((tq, 128), lambda i: (i, 0))],
        out_shape=[jax.ShapeDtypeStruct((L, FOX_W), f32), jax.ShapeDtypeStruct((L, 128), f32)],
        compiler_params=_cp(("parallel",)))(proj, proj, proj, cum, cumt)


def _fox_delta(proj, dycat, lse, cum, cumt, *, tq=256):
    L = proj.shape[0]

    def body(q_ref, k_ref, v_ref, do_ref, lse_ref, cq_ref, ct_ref, dl_ref):
        i = pl.program_id(0)
        row = lax.broadcasted_iota(jnp.int32, (tq, tq), 0)
        col = lax.broadcasted_iota(jnp.int32, (tq, tq), 1)
        lane = lax.broadcasted_iota(jnp.int32, (tq, 128), 1)
        out = jnp.zeros((tq, 128), f32)
        for h in range(FOX_H):
            hs = slice(h * FOX_HD, (h + 1) * FOX_HD)
            qh = (q_ref[:, hs] * FOX_SCALE).astype(bf16)
            dob = do_ref[:, hs].astype(bf16)
            cq = cq_ref[:, SM_F + h:SM_F + h + 1]
            lse_h = lse_ref[:, h:h + 1]

            def step(j, acc, hs=hs, qh=qh, dob=dob, cq=cq, lse_h=lse_h, h=h):
                r0 = pl.multiple_of(j * tq, tq)
                kj = k_ref[pl.ds(r0, tq), hs].astype(bf16)
                vj = v_ref[pl.ds(r0, tq), hs].astype(bf16)
                ck = ct_ref[SM_F + h:SM_F + h + 1, pl.ds(r0, tq)]
                s = _dot_nt(qh, kj) + (cq - ck)
                s = jnp.where(row + i * tq >= col + j * tq, s, NEG)
                return acc + jnp.sum(jnp.exp(s - lse_h) * _dot_nt(dob, vj), -1, keepdims=True)

            dl = lax.fori_loop(0, i + 1, step, jnp.zeros((tq, 1), f32))
            out = jnp.where(lane == h, dl, out)
        dl_ref[...] = out

    return pl.pallas_call(
        body, name="fox_delta", grid=(L // tq,),
        in_specs=[pl.BlockSpec((tq, FOX_W), lambda i: (i, P_Q // FOX_W)),
                  pl.BlockSpec((L, FOX_W), lambda i: (0, P_K // FOX_W), pipeline_mode=pl.Buffered(1)),
                  pl.BlockSpec((L, FOX_W), lambda i: (0, P_V // FOX_W), pipeline_mode=pl.Buffered(1)),
                  pl.BlockSpec((tq, FOX_W), lambda i: (i, SSD_W // FOX_W)),
                  pl.BlockSpec((tq, 128), lambda i: (i, 0)), pl.BlockSpec((tq, 128), lambda i: (i, 0)),
                  _const_spec((128, L))],
        out_specs=pl.BlockSpec((tq, 128), lambda i: (i, 0)),
        out_shape=jax.ShapeDtypeStruct((L, 128), f32),
        compiler_params=_cp(("parallel",)))(proj, proj, proj, dycat, lse, cum, cumt)


def _fox_bwd(proj, dycat, delta, lse, cum, cumt, *, tq=256):
    L = proj.shape[0]
    nq = L // tq

    def body(q_ref, k_ref, v_ref, do_ref, dl_ref, lse_ref, cum_ref, ct_ref, dq_ref, dk_ref, dv_ref, dct_ref):
        j = pl.program_id(0)

        @pl.when(j == 0)
        def _():
            dq_ref[...] = jnp.zeros_like(dq_ref)
        row = lax.broadcasted_iota(jnp.int32, (tq, tq), 0)
        col = lax.broadcasted_iota(jnp.int32, (tq, tq), 1)
        sub = lax.broadcasted_iota(jnp.int32, (128, tq), 0)
        dct = jnp.zeros((128, tq), f32)
        for h in range(FOX_H):
            hs = slice(h * FOX_HD, (h + 1) * FOX_HD)
            kj = k_ref[:, hs].astype(bf16)
            vj = v_ref[:, hs].astype(bf16)
            ck = ct_ref[SM_F + h:SM_F + h + 1, :]

            def step(i, carry, hs=hs, kj=kj, vj=vj, ck=ck, h=h):
                dk, dv, dc = carry
                r0 = pl.multiple_of(i * tq, tq)
                qi = (q_ref[pl.ds(r0, tq), hs] * FOX_SCALE).astype(bf16)
                doi = do_ref[pl.ds(r0, tq), hs]
                delta = dl_ref[pl.ds(r0, tq), h:h + 1]
                lse_i = lse_ref[pl.ds(r0, tq), h:h + 1]
                cq = cum_ref[pl.ds(r0, tq), SM_F + h:SM_F + h + 1]
                s = _dot_nt(qi, kj) + (cq - ck)
                s = jnp.where(row + i * tq >= col + j * tq, s, NEG)
                p = jnp.exp(s - lse_i)
                dob = doi.astype(bf16)
                dv = dv + _dot_tn(p.astype(bf16), dob)
                ds = p * (_dot_nt(dob, vj) - delta)
                dsb = ds.astype(bf16)
                dq_ref[pl.ds(r0, tq), hs] += _dot(dsb, kj) * FOX_SCALE
                dk = dk + _dot_tn(dsb, qi)
                dc = dc - jnp.sum(ds, 0, keepdims=True)
                return dk, dv, dc

            dk, dv, dc = lax.fori_loop(
                j, nq, step,
                (jnp.zeros((tq, FOX_HD), f32), jnp.zeros((tq, FOX_HD), f32), jnp.zeros((1, tq), f32)))
            dk_ref[:, hs] = dk
            dv_ref[:, hs] = dv
            dct = jnp.where(sub == SM_F + h, dc, dct)
        dct_ref[...] = dct

    full = lambda w_, col: pl.BlockSpec((L, w_), lambda j: (0, col // w_), pipeline_mode=pl.Buffered(1))
    blk = lambda col: pl.BlockSpec((tq, FOX_W), lambda j: (j, col // FOX_W))
    return pl.pallas_call(
        body, name="fox_bwd", grid=(nq,),
        in_specs=[full(FOX_W, P_Q), blk(P_K), blk(P_V), full(FOX_W, SSD_W), full(128, 0), full(128, 0), full(128, 0),
                  pl.BlockSpec((128, tq), lambda j: (0, j))],
        out_specs=[pl.BlockSpec((L, FOX_W), lambda j: (0, 0)), pl.BlockSpec((tq, FOX_W), lambda j: (j, 0)),
                   pl.BlockSpec((tq, FOX_W), lambda j: (j, 0)), pl.BlockSpec((128, tq), lambda j: (0, j))],
        out_shape=[jax.ShapeDtypeStruct((L, FOX_W), f32)] * 3 + [jax.ShapeDtypeStruct((128, L), f32)],
        compiler_params=_cp(("arbitrary",)))(proj, proj, proj, dycat, delta, lse, cum, cumt)


SSD_GW = SSD_W // SSD_G
SSD_HPG = SSD_H // SSD_G


def _ssd_pre(x, xprev, sm, cp_ref, hp_ref, tc):
    pre = (cp_ref[4:5, :] + cp_ref[3:4, :] * x + cp_ref[2:3, :] * _shift_down(x, xprev, 1)
           + cp_ref[1:2, :] * _shift_down(x, xprev, 2) + cp_ref[0:1, :] * _shift_down(x, xprev, 3))
    sig = _sigmoid(pre)
    raw = sm + hp_ref[0:1, :]
    dt = jnp.maximum(raw, 0.0) + _log1pexp(raw)
    a_neg = -jnp.exp(hp_ref[1:2, :])
    r = lax.broadcasted_iota(jnp.int32, (tc, tc), 0)
    c = lax.broadcasted_iota(jnp.int32, (tc, tc), 1)
    cs = _dot_hi(jnp.where(r >= c, 1.0, 0.0), dt * a_neg)
    return pre, sig, raw, dt, a_neg, cs, cs.T, r >= c


def _ssd_fwd(proj, cp, hp, ng, *, tc=256):
    L = proj.shape[0]
    nc = L // tc

    def body(xc_ref, xp_ref, z_ref, sm_ref, cp_ref, hp_ref, ng_ref, y_ref, ypre_ref, sin_ref, s_scr):
        i = pl.program_id(0)

        @pl.when(i == 0)
        def _():
            s_scr[...] = jnp.zeros_like(s_scr)
        x = xc_ref[...]
        xprev = xp_ref[...] * jnp.where(i > 0, 1.0, 0.0)
        pre, sig, _, dt, _, cs, cst, tril = _ssd_pre(x, xprev, sm_ref[...], cp_ref, hp_ref, tc)
        xbc = pre * sig
        sin_ref[...] = s_scr[...]
        for g in range(SSD_G):
            bg = xbc[:, SSD_W + g * SSD_N:SSD_W + (g + 1) * SSD_N]
            cg = xbc[:, SSD_W + SSD_G * SSD_N + g * SSD_N:SSD_W + SSD_G * SSD_N + (g + 1) * SSD_N].astype(bf16)
            cb = _dot_nt(cg, bg.astype(bf16))
            for e in range(SSD_HPG):
                h = g * SSD_HPG + e
                hs = slice(h * SSD_HD, (h + 1) * SSD_HD)
                xs = xbc[:, hs]
                csc = cs[:, h:h + 1]
                lm = jnp.where(tril, jnp.exp(jnp.minimum(csc - cst[h:h + 1, :], 0.0)), 0.0)
                xdt = (xs * dt[:, h:h + 1]).astype(bf16)
                s_h = s_scr[:, hs]
                y = _dot((cb * lm).astype(bf16), xdt) + jnp.exp(csc) * _dot(cg, s_h.astype(bf16))
                ypre_ref[:, hs] = y + hp_ref[2:3, h:h + 1] * xs
                cl = cs[tc - 1:tc, h:h + 1]
                bd = (bg * jnp.exp(cl - csc)).astype(bf16)
                s_scr[:, hs] = jnp.exp(cl) * s_h + _dot_tn(bd, xdt)
        z = z_ref[...]
        yz = ypre_ref[...] * (z * _sigmoid(z))
        for g in range(SSD_G):
            gs = slice(g * SSD_GW, (g + 1) * SSD_GW)
            yg = yz[:, gs]
            r = lax.rsqrt(jnp.mean(yg * yg, -1, keepdims=True) + RMS_EPS)
            y_ref[:, gs] = (yg * r * ng_ref[0:1, gs]).astype(bf16)

    return pl.pallas_call(
        body, name="ssd_fwd", grid=(nc,),
        in_specs=[pl.BlockSpec((tc, 1024), lambda i: (i, 0)),
                  pl.BlockSpec((tc, 1024), lambda i: (jnp.maximum(i - 1, 0), 0)),
                  pl.BlockSpec((tc, SSD_W), lambda i: (i, P_Z // SSD_W)),
                  pl.BlockSpec((tc, 128), lambda i: (i, P_SM // 128)),
                  _const_spec((8, 1024)), _const_spec((8, 128)), _const_spec((8, SSD_W))],
        out_specs=[pl.BlockSpec((tc, SSD_W), lambda i: (i, 0)), pl.BlockSpec((tc, SSD_W), lambda i: (i, 0)),
                   pl.BlockSpec((SSD_N, SSD_W), lambda i: (i, 0))],
        out_shape=[jax.ShapeDtypeStruct((L, SSD_W), bf16), jax.ShapeDtypeStruct((L, SSD_W), f32),
                   jax.ShapeDtypeStruct((nc * SSD_N, SSD_W), f32)],
        scratch_shapes=[pltpu.VMEM((SSD_N, SSD_W), f32)],
        compiler_params=_cp(("arbitrary",)))(proj, proj, proj, proj, cp, hp, ng)


def _ssd_bwd(proj, dycat, ypre, sin, cp, hp, ng, *, tc=256):
    L = proj.shape[0]
    nc = L // tc

    def body(xc_ref, xp_ref, z_ref, sm_ref, cp_ref, hp_ref, ng_ref, sin_ref, ypre_ref, dy_ref,
             dxbc_ref, dz_ref, ddt_ref, acc1_ref, acc2_ref, ds_scr, dnext_scr, dxbc_scr):
        i = pl.program_id(0)
        c_idx = nc - 1 - i

        @pl.when(i == 0)
        def _():
            ds_scr[...] = jnp.zeros_like(ds_scr)
            dnext_scr[...] = jnp.zeros_like(dnext_scr)
            acc1_ref[...] = jnp.zeros_like(acc1_ref)
            acc2_ref[...] = jnp.zeros_like(acc2_ref)
        x = xc_ref[...]
        xprev = xp_ref[...] * jnp.where(c_idx > 0, 1.0, 0.0)
        pre, sig, raw, dt, a_neg, cs, cst, tril = _ssd_pre(x, xprev, sm_ref[...], cp_ref, hp_ref, tc)
        xbc = pre * sig
        z = z_ref[...]
        sz = _sigmoid(z)
        silz = z * sz
        yall = ypre_ref[...]
        yz = yall * silz
        dy = dy_ref[...]
        dyz_parts = []
        for g in range(SSD_G):
            gs = slice(g * SSD_GW, (g + 1) * SSD_GW)
            yg, dyg = yz[:, gs], dy[:, gs]
            r = lax.rsqrt(jnp.mean(yg * yg, -1, keepdims=True) + RMS_EPS)
            acc1_ref[5:6, gs] += jnp.sum(dyg * yg * r, 0, keepdims=True)
            dyn = dyg * ng_ref[0:1, gs]
            dyz_parts.append(r * (dyn - yg * (r * r) * jnp.mean(dyn * yg, -1, keepdims=True)))
        dyz = jnp.concatenate(dyz_parts, axis=1)
        dz_ref[...] = (dyz * yall * (sz * (1.0 + z * (1.0 - sz)))).astype(bf16)
        dyall = dyz * silz

        lane1 = lax.broadcasted_iota(jnp.int32, (1, 128), 1)
        sub = lax.broadcasted_iota(jnp.int32, (128, tc), 0)
        rowc = lax.broadcasted_iota(jnp.int32, (tc, 1), 0)
        dcs = jnp.zeros((tc, 128), f32)
        dcsr = jnp.zeros((128, tc), f32)
        ddt = jnp.zeros((tc, 128), f32)
        dd_row = jnp.zeros((1, 128), f32)
        for g in range(SSD_G):
            b0 = SSD_W + g * SSD_N
            c0 = SSD_W + SSD_G * SSD_N + g * SSD_N
            bg = xbc[:, b0:b0 + SSD_N]
            bgb = bg.astype(bf16)
            cgb = xbc[:, c0:c0 + SSD_N].astype(bf16)
            cb = _dot_nt(cgb, bgb)
            dbg = jnp.zeros((tc, SSD_N), f32)
            dcg = jnp.zeros((tc, SSD_N), f32)
            for e in range(SSD_HPG):
                h = g * SSD_HPG + e
                hs = slice(h * SSD_HD, (h + 1) * SSD_HD)
                oh = jnp.where(lane1 == h, 1.0, 0.0)
                xs = xbc[:, hs]
                dth = dt[:, h:h + 1]
                csc = cs[:, h:h + 1]
                lm = jnp.where(tril, jnp.exp(jnp.minimum(csc - cst[h:h + 1, :], 0.0)), 0.0)
                m = cb * lm
                xdt = (xs * dth).astype(bf16)
                s_h = sin_ref[:, hs]
                s_hb = s_h.astype(bf16)
                dyh = dyall[:, hs]
                dyb = dyh.astype(bf16)
                dd_row = dd_row + oh * jnp.sum(dyh * xs)
                dxs = hp_ref[2:3, h:h + 1] * dyh
                ecs = jnp.exp(csc)
                cs_prod = _dot(cgb, s_hb)
                dcsb = (ecs * dyh).astype(bf16)
                dcg = dcg + _dot_nt(dcsb, s_hb)
                ds_in = _dot_tn(cgb, dcsb)
                dcs_h = jnp.sum(dyh * ecs * cs_prod, -1, keepdims=True)
                dm = _dot_nt(dyb, xdt)
                w = dm * m
                dcs_h = dcs_h + jnp.sum(w, -1, keepdims=True)
                dcsr = jnp.where(sub == h, jnp.sum(w, 0, keepdims=True), dcsr)
                dcbb = (dm * lm).astype(bf16)
                dcg = dcg + _dot(dcbb, bgb)
                dbg = dbg + _dot_tn(dcbb, cgb)
                dxdt = _dot_tn(m.astype(bf16), dyb)
                dsn = ds_scr[:, hs]
                dsnb = dsn.astype(bf16)
                cl = cs[tc - 1:tc, h:h + 1]
                dec = jnp.exp(cl - csc)
                dxdt = dxdt + _dot((bg * dec).astype(bf16), dsnb)
                dbd = _dot_nt(xdt, dsnb)
                dbg = dbg + dbd * dec
                gdec = jnp.sum(dbd * bg, -1, keepdims=True) * dec
                ecl = jnp.exp(cl)
                dcl = jnp.sum(gdec) + jnp.sum(dsn * s_h) * ecl
                ds_scr[:, hs] = ecl * dsn + ds_in
                dcs_h = dcs_h - gdec + jnp.where(rowc == tc - 1, dcl, 0.0)
                dcs = dcs + dcs_h * oh
                dxbc_scr[:, hs] = dxs + dxdt * dth
                ddt = ddt + jnp.sum(dxdt * xs, -1, keepdims=True) * oh
            dxbc_scr[:, b0:b0 + SSD_N] = dbg
            dxbc_scr[:, c0:c0 + SSD_N] = dcg
        dcs = dcs - dcsr.T
        r_i = lax.broadcasted_iota(jnp.int32, (tc, tc), 0)
        c_i = lax.broadcasted_iota(jnp.int32, (tc, tc), 1)
        da = _dot_hi(jnp.where(r_i <= c_i, 1.0, 0.0), dcs)
        ddt = ddt + da * a_neg
        acc2_ref[1:2, :] += jnp.sum(da * dt, 0, keepdims=True) * a_neg
        lane = lax.broadcasted_iota(jnp.int32, (tc, 128), 1)
        ddraw = jnp.where(lane < SSD_H, ddt * _sigmoid(raw), 0.0)
        acc2_ref[0:1, :] += jnp.sum(ddraw, 0, keepdims=True)
        acc2_ref[2:3, :] += dd_row
        ddt_ref[...] = ddraw
        dpre = dxbc_scr[...] * (sig * (1.0 + pre * (1.0 - sig)))
        acc1_ref[4:5, :] += jnp.sum(dpre, 0, keepdims=True)
        for k in range(SSD_K):
            acc1_ref[k:k + 1, :] += jnp.sum(dpre * _shift_down(x, xprev, SSD_K - 1 - k), 0, keepdims=True)
        dnext = dnext_scr[...]
        dxbc_ref[...] = (cp_ref[3:4, :] * dpre + cp_ref[2:3, :] * _shift_up(dpre, dnext, 1)
                         + cp_ref[1:2, :] * _shift_up(dpre, dnext, 2)
                         + cp_ref[0:1, :] * _shift_up(dpre, dnext, 3)).astype(bf16)
        dnext_scr[...] = dpre

    rev = lambda w_, col: pl.BlockSpec((tc, w_), lambda i: (nc - 1 - i, col // w_))
    return pl.pallas_call(
        body, name="ssd_bwd", grid=(nc,),
        in_specs=[rev(1024, 0), pl.BlockSpec((tc, 1024), lambda i: (jnp.maximum(nc - 2 - i, 0), 0)),
                  rev(SSD_W, P_Z), rev(128, P_SM),
                  _const_spec((8, 1024)), _const_spec((8, 128)), _const_spec((8, SSD_W)),
                  pl.BlockSpec((SSD_N, SSD_W), lambda i: (nc - 1 - i, 0)), rev(SSD_W, 0), rev(SSD_W, 0)],
        out_specs=[rev(1024, 0), rev(SSD_W, 0), rev(128, 0),
                   pl.BlockSpec((8, 1024), lambda i: (0, 0)), pl.BlockSpec((8, 128), lambda i: (0, 0))],
        out_shape=[jax.ShapeDtypeStruct((L, 1024), bf16), jax.ShapeDtypeStruct((L, SSD_W), bf16),
                   jax.ShapeDtypeStruct((L, 128), f32), jax.ShapeDtypeStruct((8, 1024), f32),
                   jax.ShapeDtypeStruct((8, 128), f32)],
        scratch_shapes=[pltpu.VMEM((SSD_N, SSD_W), f32), pltpu.VMEM((tc, 1024), f32), pltpu.VMEM((tc, 1024), f32)],
        compiler_params=_cp(("arbitrary",)))(proj, proj, proj, proj, cp, hp, ng, sin, ypre, dycat)


def _pack_cols(w):
    pad = jnp.zeros(w.shape[:-1] + (P_W - P_SM - SSD_H - FOX_H,), w.dtype)
    return jnp.concatenate([w[..., 512:1536], w[..., 0:512], w[..., 1544:2312], w[..., 2316:3084],
                            w[..., 1536:1544], w[..., 2312:2316], pad], axis=-1)


def _unpack_cols(g):
    return jnp.concatenate([g[..., 1024:1536], g[..., 0:1024], g[..., 3072:3080], g[..., 1536:2304],
                            g[..., 3080:3084], g[..., 2304:3072]], axis=-1)


def _rows8(*rows):
    width = max(r.shape[-1] for r in rows)
    out = [jnp.pad(r.astype(f32), (0, width - r.shape[-1])) for r in rows]
    out += [jnp.zeros((width,), f32)] * (8 - len(out))
    return jnp.stack(out)


def _local_step(x, tgt, p):
    gb_in = _rows8(p["ln_in_g"], p["ln_in_b"])
    x0 = _ln_in_fwd(x, gb_in)
    saved = []
    for l in range(DEPTH):
        mv = [_rows8(p["mod"][l, 3 * j], p["mod"][l, 3 * j + 1], p["mod"][l, 3 * j + 2], p["ln_g"][l, j], p["ln_b"][l, j])
              for j in range(3)]
        cp = _rows8(*[p["ssd_conv_w"][l, k] for k in range(SSD_K)], p["ssd_conv_b"][l])
        hp = _rows8(jnp.pad(p["ssd_dt_bias"][l], (0, 120)), jnp.pad(p["ssd_a_log"][l], (0, 120)),
                    jnp.pad(p["ssd_d"][l], (0, 120)), jnp.pad(p["fox_f_bias"][l], (SM_F, 128 - SM_F - FOX_H)))
        ng = _rows8(p["ssd_norm_g"][l])
        scw = _rows8(*[p["sconv_w"][l, k] for k in range(SC_K)])
        x1 = _ffn_fwd(x0, mv[0], p["ffn1_w_in"][l], p["ffn1_w_out"][l])
        proj = _inproj_fwd(x1, mv[1], p["mix_w_in"][l])
        y_ssd, ypre, sin = _ssd_fwd(proj, cp, hp, ng)
        cum, cumt = _fox_gate_fwd(proj, hp)
        o, lse = _fox_fwd(proj, cum, cumt)
        y_sc = _sconv_fwd(proj, scw)
        ycat = jnp.concatenate([y_ssd, o.astype(bf16), y_sc], axis=1)
        x2 = _outproj_fwd(x1, ycat, mv[1], p["mix_w_out"][l])
        x3 = _ffn_fwd(x2, mv[2], p["ffn2_w_in"][l], p["ffn2_w_out"][l])
        saved.append((x0, x1, x2, mv, cp, hp, ng, scw, proj, ypre, sin, cum, cumt, o, lse, ycat))
        x0 = x3
    dx, loss_acc = _loss_head(x0, tgt)

    g = {k: [None] * DEPTH for k in (
        "ffn1_w_in", "ffn1_w_out", "ffn2_w_in", "ffn2_w_out", "mix_w_in", "mix_w_out", "mod", "ln_g", "ln_b",
        "ssd_conv_w", "ssd_conv_b", "ssd_dt_bias", "ssd_a_log", "ssd_d", "ssd_norm_g", "fox_f_bias", "sconv_w")}
    for l in reversed(range(DEPTH)):
        x0, x1, x2, mv, cp, hp, ng, scw, proj, ypre, sin, cum, cumt, o, lse, ycat = saved[l]
        dx, h, da, act, dyb, a2 = _ffn_bwd(x2, dx, mv[2], p["ffn2_w_in"][l], p["ffn2_w_out"][l])
        g["ffn2_w_in"][l] = _matmul_tn(h, da, tn=FS, tk=512, name="dw_ffn_in")
        g["ffn2_w_out"][l] = _matmul_tn(act, dyb, tn=D, tk=512, name="dw_ffn_out")
        dxp, dyb, dycat, a1 = _outproj_bwd(x1, ycat, dx, mv[1], p["mix_w_out"][l])
        g["mix_w_out"][l] = _matmul_tn(ycat, dyb, tn=D // 2, tk=512, name="dw_mix_out")
        dxbc, dz, ddt, acc1, acc2 = _ssd_bwd(proj, dycat, ypre, sin, cp, hp, ng)
        delta = _fox_delta(proj, dycat, lse, cum, cumt)
        dq, dk, dv, dcumt = _fox_bwd(proj, dycat, delta, lse, cum, cumt)
        dsm, accf = _fox_gate_bwd(dcumt, proj, ddt, hp)
        dsb, dsc, dsx, accs = _sconv_bwd(proj, dycat, scw)
        dproj = jnp.concatenate([dxbc, dz, dq.astype(bf16), dk.astype(bf16), dv.astype(bf16), dsb, dsc, dsx, dsm], axis=1)
        dx, h, a1b = _inproj_bwd(x1, dxp, dproj, mv[1], p["mix_w_in"][l])
        g["mix_w_in"][l] = _matmul_tn(h, dproj, tn=P_W // 5, tk=512, name="dw_mix_in")
        dx, h, da, act, dyb, a0 = _ffn_bwd(x0, dx, mv[0], p["ffn1_w_in"][l], p["ffn1_w_out"][l])
        g["ffn1_w_in"][l] = _matmul_tn(h, da, tn=FS, tk=512, name="dw_ffn_in")
        g["ffn1_w_out"][l] = _matmul_tn(act, dyb, tn=D, tk=512, name="dw_ffn_out")
        g["mod"][l] = jnp.concatenate([a0[0:3], a1b[0:2], a1[2:3], a2[0:3]], axis=0)
        g["ln_g"][l] = jnp.stack([a0[3], a1[3], a2[3]])
        g["ln_b"][l] = jnp.stack([a0[4], a1[4], a2[4]])
        g["ssd_conv_w"][l] = acc1[0:SSD_K]
        g["ssd_conv_b"][l] = acc1[4]
        g["ssd_norm_g"][l] = acc1[5, :SSD_W]
        g["ssd_dt_bias"][l] = acc2[0, :SSD_H]
        g["ssd_a_log"][l] = acc2[1, :SSD_H]
        g["ssd_d"][l] = acc2[2, :SSD_H]
        g["fox_f_bias"][l] = accf[3, SM_F:SM_F + FOX_H]
        g["sconv_w"][l] = accs[0:SC_K]
    grad_x, a_in = _ln_in_bwd(x, dx, gb_in)
    g = {k: (v if k in BIG else jnp.stack(v)) for k, v in g.items()}
    g["ln_in_g"], g["ln_in_b"] = a_in[0], a_in[1]
    return loss_acc[0, 0], grad_x, g


MESH = pl.DeviceIdType.MESH
ANY = pl.BlockSpec(memory_space=pl.ANY)


def _all_gather(shards, *, in_vmem, name):
    n_arr = len(shards)

    def body(*refs):
        x_refs, out_refs = refs[:n_arr], refs[n_arr:2 * n_arr]
        send_sems, recv_sems, local_sems = refs[2 * n_arr:]
        x, y, c = lax.axis_index("x"), lax.axis_index("y"), lax.axis_index("c")
        me, sibling = (x, y, c), (x, y, 1 - c)
        chips = [(1 - x, y), (x, 1 - y), (1 - x, 1 - y)]

        def copy(a, k, block, to, src=None):
            px, py, pc = block
            slot = out_refs[a].at[4 * px + 2 * py + pc]
            return pltpu.make_async_remote_copy(
                src_ref=slot if src is None else src, dst_ref=slot,
                send_sem=send_sems.at[7 * a + k], recv_sem=recv_sems.at[7 * a + k], device_id=to, device_id_type=MESH)

        mine, first, passed = [], [], []
        for a in range(n_arr):
            mine.append(pltpu.make_async_copy(x_refs[a], out_refs[a].at[4 * x + 2 * y + c], local_sems.at[a]))
            mine[-1].start()
            first.append(copy(a, 0, me, sibling, src=x_refs[a]))
            first += [copy(a, 1 + j, me, (*chip, c), src=x_refs[a]) for j, chip in enumerate(chips)]
        for cp in first:
            cp.start()
        for j, chip in enumerate(chips):
            for a in range(n_arr):
                copy(a, 1 + j, (*chip, c), me).wait_recv()
                passed.append(copy(a, 4 + j, (*chip, c), sibling))
                passed[-1].start()
        for a in range(n_arr):
            copy(a, 0, sibling, me).wait_recv()
            for j, chip in enumerate(chips):
                copy(a, 4 + j, (*chip, 1 - c), me).wait_recv()
        for cp in first + passed:
            cp.wait_send()
        for cp in mine:
            cp.wait()

    spec = pl.BlockSpec(memory_space=pltpu.VMEM) if in_vmem else ANY
    return pl.pallas_call(
        body, name=name, out_shape=[jax.ShapeDtypeStruct((N_DEV,) + s.shape, s.dtype) for s in shards],
        in_specs=[spec] * n_arr, out_specs=[spec] * n_arr,
        scratch_shapes=[pltpu.SemaphoreType.DMA((7 * n_arr,)), pltpu.SemaphoreType.DMA((7 * n_arr,)),
                        pltpu.SemaphoreType.DMA((n_arr,))],
    )(*shards)


def _swap_sibling(sends):
    n_arr = len(sends)

    def body(*refs):
        s_refs, o_refs = refs[:n_arr], refs[n_arr:2 * n_arr]
        send_sems, recv_sems = refs[2 * n_arr:]
        x, y, c = lax.axis_index("x"), lax.axis_index("y"), lax.axis_index("c")
        cps = [pltpu.make_async_remote_copy(
            src_ref=s_refs[a].at[:, 1 - c], dst_ref=o_refs[a], send_sem=send_sems.at[a], recv_sem=recv_sems.at[a],
            device_id=(x, y, 1 - c), device_id_type=MESH) for a in range(n_arr)]
        for cp in cps:
            cp.start()
        for cp in cps:
            cp.wait_recv()
        for cp in cps:
            cp.wait_send()

    return pl.pallas_call(
        body, name="swap_sibling",
        out_shape=[jax.ShapeDtypeStruct((s.shape[0],) + s.shape[2:], s.dtype) for s in sends],
        in_specs=[ANY] * n_arr, out_specs=[ANY] * n_arr,
        scratch_shapes=[pltpu.SemaphoreType.DMA((n_arr,)), pltpu.SemaphoreType.DMA((n_arr,))])(*sends)


def _exchange_chips(bufs):
    n_arr = len(bufs)

    def body(*refs):
        b_refs, o_refs = refs[:n_arr], refs[n_arr:2 * n_arr]
        send_sems, recv_sems, local_sems = refs[2 * n_arr:]
        x, y, c = lax.axis_index("x"), lax.axis_index("y"), lax.axis_index("c")
        mine = 2 * x + y
        peers = [(x, 1 - y), (1 - x, y), (1 - x, 1 - y)]

        def copy(a, k, src_slot, dst_slot):
            px, py = peers[k]
            return pltpu.make_async_remote_copy(
                src_ref=b_refs[a].at[src_slot], dst_ref=o_refs[a].at[dst_slot],
                send_sem=send_sems.at[3 * a + k], recv_sem=recv_sems.at[3 * a + k],
                device_id=(px, py, c), device_id_type=MESH)

        own = [pltpu.make_async_copy(b_refs[a].at[mine], o_refs[a].at[mine], local_sems.at[a]) for a in range(n_arr)]
        sent = [copy(a, k, 2 * px + py, mine) for a in range(n_arr) for k, (px, py) in enumerate(peers)]
        for cp in own + sent:
            cp.start()
        for a in range(n_arr):
            for k, (px, py) in enumerate(peers):
                copy(a, k, mine, 2 * px + py).wait_recv()
        for cp in sent:
            cp.wait_send()
        for cp in own:
            cp.wait()

    return pl.pallas_call(
        body, name="exchange_chips", out_shape=[jax.ShapeDtypeStruct(b.shape, b.dtype) for b in bufs],
        in_specs=[ANY] * n_arr, out_specs=[ANY] * n_arr,
        scratch_shapes=[pltpu.SemaphoreType.DMA((3 * n_arr,)), pltpu.SemaphoreType.DMA((3 * n_arr,)),
                        pltpu.SemaphoreType.DMA((n_arr,))])(*bufs)


def _row_tile(r, c):
    if r * c <= 512 * 1024:
        return r
    t = r
    while t * c > 512 * 1024 and t % 2 == 0 and (t // 2) % 16 == 0:
        t //= 2
    return t


def _add_own_half(send, recv, c):
    nb, _, r, n = send.shape
    tr = _row_tile(r, n)

    def body(c_ref, s_ref, r_ref, o_ref):
        o_ref[...] = (s_ref[...].astype(f32) + r_ref[...].astype(f32)).astype(o_ref.dtype)

    return pl.pallas_call(
        body, name="add_own_half",
        grid_spec=pltpu.PrefetchScalarGridSpec(
            num_scalar_prefetch=1, grid=(nb, r // tr),
            in_specs=[pl.BlockSpec((None, None, tr, n), lambda j, i, cr: (j, cr[0], i, 0)),
                      pl.BlockSpec((None, tr, n), lambda j, i, cr: (j, i, 0))],
            out_specs=pl.BlockSpec((None, tr, n), lambda j, i, cr: (j, i, 0))),
        out_shape=jax.ShapeDtypeStruct((nb, r, n), bf16),
        compiler_params=_cp(("parallel", "parallel")))(jnp.reshape(c, (1,)).astype(jnp.int32), send, recv)


def _sum_slots(buf, *, tr, name):
    nb, r, n = buf.shape

    def body(b_ref, o_ref):
        acc = b_ref[0].astype(f32)
        for k in range(1, nb):
            acc = acc + b_ref[k].astype(f32)
        o_ref[...] = acc

    return pl.pallas_call(
        body, name=name, grid=(r // tr,),
        in_specs=[pl.BlockSpec((nb, tr, n), lambda i: (0, i, 0))],
        out_specs=pl.BlockSpec((tr, n), lambda i: (i, 0)),
        out_shape=jax.ShapeDtypeStruct((r, n), f32), compiler_params=_cp(("parallel",)))(buf)


def _ada_fwd(c_all, ada_w, ada_b_cols):
    n = ada_w.shape[-1]

    def body(c_ref, w_ref, b_ref, o_ref):
        cv = c_ref[...]
        ca = (cv * _sigmoid(cv)).astype(bf16)
        o_ref[...] = _dot(ca, w_ref[...].astype(bf16)) + b_ref[...]

    return pl.pallas_call(
        body, name="ada_fwd", grid=(DEPTH,),
        in_specs=[_const_spec((N_DEV, D)), pl.BlockSpec((None, D, n), lambda l: (l, 0, 0)),
                  pl.BlockSpec((None, 1, n), lambda l: (l, 0, 0))],
        out_specs=pl.BlockSpec((None, N_DEV, n), lambda l: (l, 0, 0)),
        out_shape=jax.ShapeDtypeStruct((DEPTH, N_DEV, n), f32), compiler_params=_cp(("parallel",)))(c_all, ada_w, ada_b_cols)


def _ada_bwd(c_all, dmod_cols):
    n = dmod_cols.shape[-1]

    def body(c_ref, d_ref, o_ref):
        cv = c_ref[...]
        ca = (cv * _sigmoid(cv)).astype(bf16)
        o_ref[...] = _dot_tn(ca, d_ref[...].astype(bf16))

    return pl.pallas_call(
        body, name="ada_bwd", grid=(DEPTH,),
        in_specs=[_const_spec((N_DEV, D)), pl.BlockSpec((None, N_DEV, n), lambda l: (l, 0, 0))],
        out_specs=pl.BlockSpec((None, D, n), lambda l: (l, 0, 0)),
        out_shape=jax.ShapeDtypeStruct((DEPTH, D, n), f32), compiler_params=_cp(("parallel",)))(c_all, dmod_cols)


def _adamw(w, g, m, v, *, tr, name):
    r, n = w.shape

    def body(w_ref, g_ref, m_ref, v_ref, d_ref, mo_ref, vo_ref):
        g_ = g_ref[...]
        m_ = ADAM_B1 * m_ref[...] + (1.0 - ADAM_B1) * g_
        v_ = ADAM_B2 * v_ref[...] + (1.0 - ADAM_B2) * jnp.square(g_)
        m_hat = m_ / (1.0 - ADAM_B1 ** ADAM_STEP)
        v_hat = v_ / (1.0 - ADAM_B2 ** ADAM_STEP)
        d_ref[...] = -ADAM_LR * (m_hat / (jnp.sqrt(v_hat) + ADAM_EPS) + ADAM_WD * w_ref[...])
        mo_ref[...] = m_
        vo_ref[...] = v_

    blk = pl.BlockSpec((tr, n), lambda i: (i, 0))
    return pl.pallas_call(
        body, name=name, grid=(r // tr,), in_specs=[blk] * 4, out_specs=[blk] * 3,
        out_shape=[jax.ShapeDtypeStruct((r, n), f32)] * 3, compiler_params=_cp(("parallel",)))(w, g, m, v)


WEIGHTS = ["ln_in_g", "ln_in_b", "ada_w", "ada_b", "ffn1_w_in", "ffn1_w_out", "mix_w_in", "mix_w_out", "ssd_conv_w",
           "ssd_conv_b", "ssd_dt_bias", "ssd_a_log", "ssd_d", "ssd_norm_g", "fox_f_bias", "sconv_w", "ffn2_w_in",
           "ffn2_w_out", "ln_g", "ln_b"]
BIG = ["ffn1_w_in", "ffn1_w_out", "ffn2_w_in", "ffn2_w_out", "mix_w_in", "mix_w_out"]
COL_SHARDED = ("ffn1_w_in", "ffn2_w_in")
SMALL_SHARDED = {"ssd_conv_w": 128, "sconv_w": 32, "ln_g": 128, "ln_b": 128}
ADAM_TR = {"ada_w": 256, "ffn1_w_in": 512, "ffn2_w_in": 512, "ffn1_w_out": 352, "ffn2_w_out": 352, "mix_w_in": 64,
           "mix_w_out": 256}


def _pad_rows(v, mult=128):
    v = v.reshape(-1)
    return jnp.pad(v, (0, (-v.shape[0]) % mult))


def _pack_rows(parts, row_mult=8):
    flat = [_pad_rows(p.astype(f32)) for p in parts]
    offs, o = [], 0
    for f in flat:
        offs.append(o)
        o += f.shape[0] // 128
    buf = jnp.concatenate(flat).reshape(-1, 128)
    return jnp.pad(buf, ((0, (-buf.shape[0]) % row_mult), (0, 0))), offs


def _take(buf, off, shape):
    n = 1
    for s in shape:
        n *= s
    rows = -(-n // 128)
    lead = buf.shape[:-2]
    flat = buf[..., off:off + rows, :].reshape(lead + (rows * 128,))
    return flat[..., :n].reshape(lead + tuple(shape))


def _reduce_scatter_layer(g_layer, ci):
    sends = [t.reshape((4, 2) + t.shape[-2:]) for t in g_layer]
    got = _swap_sibling(sends)
    chip_sums = [_add_own_half(s, r, ci) for s, r in zip(sends, got)]
    got = _exchange_chips(chip_sums)
    return [_sum_slots(t, tr=_row_tile(*t.shape[-2:]), name="sum_chips") for t in got]


def kernel(*args):
    names = (["x", "c"] + WEIGHTS + ["loss_target"] + ["m_" + n for n in WEIGHTS] + ["v_" + n for n in WEIGHTS])
    assert len(args) == len(names)
    a = dict(zip(names, args))
    xi, yi, ci = lax.axis_index("x"), lax.axis_index("y"), lax.axis_index("c")
    me = 4 * xi + 2 * yi + ci

    small_in = [a["c"], a["ln_g"], a["ln_b"], a["ssd_conv_w"], a["sconv_w"]]
    buf, offs = _pack_rows(small_in)
    got, = _all_gather([buf], in_vmem=True, name="gather_small")
    c_all = _take(got, offs[0], (D,))
    full = {}
    for k, n in enumerate(["ln_g", "ln_b", "ssd_conv_w", "sconv_w"]):
        sh = a[n].shape
        t = _take(got, offs[k + 1], sh)
        full[n] = jnp.transpose(t, (1, 2, 0, 3)).reshape(sh[0], sh[1], N_DEV * sh[2])

    ncol = a["ada_w"].shape[-1]
    ada_b_cols = lax.dynamic_slice_in_dim(a["ada_b"], me * ncol, ncol, axis=1)[:, None, :]
    mod_cols = _ada_fwd(c_all, a["ada_w"], ada_b_cols)
    got, = _all_gather([mod_cols.reshape(DEPTH * N_DEV, ncol)], in_vmem=True, name="gather_mod")
    got = got.reshape(N_DEV, DEPTH, N_DEV, ncol)
    mod = lax.dynamic_index_in_dim(got, me, axis=2, keepdims=False)
    mod = jnp.transpose(mod, (1, 0, 2)).reshape(DEPTH, 9, D)

    shards = {n: a[n] for n in BIG}
    shards["mix_w_in"] = _pack_cols(a["mix_w_in"])
    p = {n: [] for n in BIG}
    for l in range(DEPTH):
        got = _all_gather([shards[n][l].astype(bf16) for n in BIG], in_vmem=False, name="gather_weights")
        for n, t in zip(BIG, got):
            p[n].append(t if n in COL_SHARDED else t.reshape(-1, t.shape[-1]))
    for n in ("ln_in_g", "ln_in_b", "ssd_conv_b", "ssd_dt_bias", "ssd_a_log", "ssd_d", "ssd_norm_g", "fox_f_bias"):
        p[n] = a[n]
    p.update(full)
    p["mod"] = mod

    loss_local, grad_x, g = _local_step(a["x"][0], a["loss_target"][0], p)
    loss = lax.psum(loss_local, ("x", "y", "c"))

    small_names = ["mod", "ln_in_g", "ln_in_b", "ssd_conv_b", "ssd_dt_bias", "ssd_a_log", "ssd_d", "ssd_norm_g",
                   "fox_f_bias", "ln_g", "ln_b", "ssd_conv_w", "sconv_w"]
    buf, offs = _pack_rows([g[n] for n in small_names])
    got, = _all_gather([buf], in_vmem=True, name="gather_small_grads")
    tot = _sum_slots(got, tr=buf.shape[0], name="sum_small_grads")
    grads = {}
    for k, n in enumerate(small_names[1:], start=1):
        t = _take(tot, offs[k], g[n].shape)
        if n in SMALL_SHARDED:
            w_ = SMALL_SHARDED[n]
            t = lax.dynamic_slice_in_dim(t, me * w_, w_, axis=2)
        grads[n] = t
    grads["ada_b"] = _take(tot, offs[0], (DEPTH, 9 * D))
    dmod_all = _take(got, offs[0], (DEPTH, 9 * D))
    dmod_cols = jnp.transpose(lax.dynamic_slice_in_dim(dmod_all, me * ncol, ncol, axis=2), (1, 0, 2))
    grads["ada_w"] = _ada_bwd(c_all, dmod_cols)

    per_layer = []
    for l in reversed(range(DEPTH)):
        g_layer = [g[n][l].reshape((N_DEV,) + shards[n].shape[1:]) for n in BIG]
        per_layer.append(_reduce_scatter_layer(g_layer, ci))
    for k, n in enumerate(BIG):
        grads[n] = jnp.stack([per_layer[1][k], per_layer[0][k]])
    grads["mix_w_in"] = _unpack_cols(grads["mix_w_in"])

    delta, new_m, new_v = {}, {}, {}
    small_params = [n for n in WEIGHTS if n not in ADAM_TR]
    packs = [_pack_rows([src[pre + n] for n in small_params])[0]
             for src, pre in ((a, ""), (grads, ""), (a, "m_"), (a, "v_"))]
    _, offs = _pack_rows([a[n] for n in small_params])
    outs = _adamw(*packs, tr=packs[0].shape[0], name="adamw_small")
    for k, n in enumerate(small_params):
        delta[n], new_m[n], new_v[n] = (_take(t, offs[k], a[n].shape) for t in outs)
    for n, tr in ADAM_TR.items():
        sh = a[n].shape
        two = lambda t: t.reshape(-1, sh[-1])
        outs = _adamw(two(a[n]), two(grads[n]), two(a["m_" + n]), two(a["v_" + n]), tr=tr, name="adamw_" + n)
        delta[n], new_m[n], new_v[n] = (t.reshape(sh) for t in outs)

    return (loss, grad_x[None], *[grads[n] for n in WEIGHTS], *[delta[n] for n in WEIGHTS],
            *[new_m[n] for n in WEIGHTS], *[new_v[n] for n in WEIGHTS])
```

```python
import functools

import jax
import jax.numpy as jnp
from jax import lax
from jax.experimental import pallas as pl
from jax.experimental.pallas import tpu as pltpu

f32, bf16 = jnp.float32, jnp.bfloat16

D = 1024
F = 2816
DEPTH = 2
N_DEV = 8
SSD_W, SSD_HD, SSD_H, SSD_G, SSD_N, SSD_K = 512, 64, 8, 2, 128, 4
FOX_W, FOX_HD, FOX_H = 256, 64, 4
SC_W, SC_K = 256, 3
ALPHA = (2 * DEPTH) ** 0.25
LN_EPS = 1e-5
RMS_EPS = 1e-5
P_XBC, P_Z, P_Q, P_K, P_V, P_SB, P_SC, P_SX, P_SM = 0, 1024, 1536, 1792, 2048, 2304, 2560, 2816, 3072
P_W = 3200
SM_DT, SM_F = 0, 8
ADAM_LR, ADAM_B1, ADAM_B2, ADAM_EPS, ADAM_WD, ADAM_STEP = 0.001, 0.9, 0.999, 1e-08, 0.01, 10

VMEM_LIMIT = 56 * 1024 * 1024


def _cp(sem=None):
    return pltpu.CompilerParams(dimension_semantics=sem, vmem_limit_bytes=VMEM_LIMIT)


def _const_spec(shape):
    nd = len(shape)
    return pl.BlockSpec(shape, lambda *_: (0,) * nd, pipeline_mode=pl.Buffered(1))


def _sigmoid(x):
    return 1.0 / (1.0 + jnp.exp(-x))


def _ln_fwd(u, g, b):
    mu = jnp.mean(u, -1, keepdims=True)
    xc = u - mu
    rstd = lax.rsqrt(jnp.mean(xc * xc, -1, keepdims=True) + LN_EPS)
    xhat = xc * rstd
    return xhat * g + b, xhat, rstd


def _ln_bwd(dout, xhat, rstd, g):
    dxh = dout * g
    m1 = jnp.mean(dxh, -1, keepdims=True)
    m2 = jnp.mean(dxh * xhat, -1, keepdims=True)
    du = rstd * (dxh - m1 - xhat * m2)
    return du, jnp.sum(dout * xhat, 0, keepdims=True), jnp.sum(dout, 0, keepdims=True)


def _dot(a, b):
    return jnp.dot(a, b, preferred_element_type=f32)


def _dot_nt(a, b):
    return lax.dot_general(a, b, (((1,), (1,)), ((), ())), preferred_element_type=f32)


def _dot_tn(a, b):
    return lax.dot_general(a, b, (((0,), (0,)), ((), ())), preferred_element_type=f32)


def _dot_hi(a, b):
    return jnp.dot(a, b, preferred_element_type=f32, precision=lax.Precision.HIGHEST)


def _shift_down(cur, prev, s):
    if s == 0:
        return cur
    row = lax.broadcasted_iota(jnp.int32, cur.shape, 0)
    return jnp.where(row < s, pltpu.roll(prev, s, 0), pltpu.roll(cur, s, 0))


def _shift_up(cur, nxt, s):
    if s == 0:
        return cur
    t = cur.shape[0]
    row = lax.broadcasted_iota(jnp.int32, cur.shape, 0)
    return jnp.where(row >= t - s, pltpu.roll(nxt, t - s, 0), pltpu.roll(cur, t - s, 0))


def _ln_in_fwd(x, gb, *, tt=512):
    L = x.shape[0]

    def body(x_ref, gb_ref, o_ref):
        o_ref[...] = _ln_fwd(x_ref[...], gb_ref[0:1, :], gb_ref[1:2, :])[0]

    return pl.pallas_call(
        body, name="ln_in_fwd", grid=(L // tt,),
        in_specs=[pl.BlockSpec((tt, D), lambda i: (i, 0)), _const_spec((8, D))],
        out_specs=pl.BlockSpec((tt, D), lambda i: (i, 0)),
        out_shape=jax.ShapeDtypeStruct((L, D), f32), compiler_params=_cp(("parallel",)))(x, gb)


def _ln_in_bwd(x, dy, gb, *, tt=512):
    L = x.shape[0]

    def body(x_ref, dy_ref, gb_ref, dx_ref, acc_ref):
        @pl.when(pl.program_id(0) == 0)
        def _():
            acc_ref[...] = jnp.zeros_like(acc_ref)
        _, xhat, rstd = _ln_fwd(x_ref[...], gb_ref[0:1, :], gb_ref[1:2, :])
        du, dg, db = _ln_bwd(dy_ref[...], xhat, rstd, gb_ref[0:1, :])
        dx_ref[...] = du
        acc_ref[0:1, :] += dg
        acc_ref[1:2, :] += db

    return pl.pallas_call(
        body, name="ln_in_bwd", grid=(L // tt,),
        in_specs=[pl.BlockSpec((tt, D), lambda i: (i, 0)), pl.BlockSpec((tt, D), lambda i: (i, 0)), _const_spec((8, D))],
        out_specs=[pl.BlockSpec((tt, D), lambda i: (i, 0)), pl.BlockSpec((8, D), lambda i: (0, 0))],
        out_shape=[jax.ShapeDtypeStruct((L, D), f32), jax.ShapeDtypeStruct((8, D), f32)],
        compiler_params=_cp(("arbitrary",)))(x, dy, gb)


def _loss_head(y, tgt, *, tt=512):
    L = y.shape[0]

    def body(y_ref, t_ref, dy_ref, acc_ref):
        @pl.when(pl.program_id(0) == 0)
        def _():
            acc_ref[...] = jnp.zeros_like(acc_ref)
        e = y_ref[...] - t_ref[...]
        dy_ref[...] = e * (1.0 / D)
        acc_ref[...] += 0.5 * jnp.sum(jnp.mean(e * e, -1, keepdims=True))

    return pl.pallas_call(
        body, name="loss_head", grid=(L // tt,),
        in_specs=[pl.BlockSpec((tt, D), lambda i: (i, 0)), pl.BlockSpec((tt, D), lambda i: (i, 0))],
        out_specs=[pl.BlockSpec((tt, D), lambda i: (i, 0)), pl.BlockSpec((8, 128), lambda i: (0, 0))],
        out_shape=[jax.ShapeDtypeStruct((L, D), f32), jax.ShapeDtypeStruct((8, 128), f32)],
        compiler_params=_cp(("arbitrary",)))(y, tgt)


FFN_CH = 4
FS = F // FFN_CH


def _ffn_fwd(x, mv, w_in, w_out, *, tt=256):
    L = x.shape[0]

    def body(x_ref, mv_ref, wi_ref, wo_ref, o_ref):
        x = x_ref[...]
        h = (x * (1.0 + mv_ref[1:2, :]) + mv_ref[0:1, :]).astype(bf16)
        y = jnp.zeros((tt, D), f32)
        for c in range(FFN_CH):
            g = _dot(h, wi_ref[c])
            u = _dot(h, wi_ref[c + FFN_CH])
            act = (g * _sigmoid(g) * u).astype(bf16)
            y = y + _dot(act, wo_ref[c * FS:(c + 1) * FS, :])
        uu = ALPHA * x + (0.5 * mv_ref[2:3, :]) * y
        o_ref[...] = _ln_fwd(uu, mv_ref[3:4, :], mv_ref[4:5, :])[0]

    return pl.pallas_call(
        body, name="ffn_fwd", grid=(L // tt,),
        in_specs=[pl.BlockSpec((tt, D), lambda i: (i, 0)), _const_spec((8, D)),
                  _const_spec((2 * FFN_CH, D, FS)), _const_spec((F, D))],
        out_specs=pl.BlockSpec((tt, D), lambda i: (i, 0)),
        out_shape=jax.ShapeDtypeStruct((L, D), f32), compiler_params=_cp(("parallel",)))(x, mv, w_in, w_out)


def _ffn_bwd(x, dxo, mv, w_in, w_out, *, tt=256):
    L = x.shape[0]

    def body(x_ref, dxo_ref, mv_ref, wi_ref, wo_ref, dx_ref, h_ref, da_ref, act_ref, dy_ref, acc_ref, a_scr):
        @pl.when(pl.program_id(0) == 0)
        def _():
            acc_ref[...] = jnp.zeros_like(acc_ref)
        x = x_ref[...]
        scale1 = 1.0 + mv_ref[1:2, :]
        h = (x * scale1 + mv_ref[0:1, :]).astype(bf16)
        h_ref[...] = h
        y = jnp.zeros((tt, D), f32)
        for c in range(FFN_CH):
            g = _dot(h, wi_ref[c])
            u = _dot(h, wi_ref[c + FFN_CH])
            a_scr[c] = g
            a_scr[c + FFN_CH] = u
            act = (g * _sigmoid(g) * u).astype(bf16)
            act_ref[c] = act
            y = y + _dot(act, wo_ref[c * FS:(c + 1) * FS, :])
        hg = 0.5 * mv_ref[2:3, :]
        _, xhat, rstd = _ln_fwd(ALPHA * x + hg * y, mv_ref[3:4, :], mv_ref[4:5, :])
        du, dlg, dlb = _ln_bwd(dxo_ref[...], xhat, rstd, mv_ref[3:4, :])
        acc_ref[3:4, :] += dlg
        acc_ref[4:5, :] += dlb
        acc_ref[2:3, :] += jnp.sum(0.5 * y * du, 0, keepdims=True)
        dyb = (hg * du).astype(bf16)
        dy_ref[...] = dyb
        dh = jnp.zeros((tt, D), f32)
        for c in range(FFN_CH):
            g = a_scr[c]
            u = a_scr[c + FFN_CH]
            dact = _dot_nt(dyb, wo_ref[c * FS:(c + 1) * FS, :])
            s = _sigmoid(g)
            dg = (dact * u * (s * (1.0 + g * (1.0 - s)))).astype(bf16)
            dup = (dact * (g * s)).astype(bf16)
            da_ref[c] = dg
            da_ref[c + FFN_CH] = dup
            dh = dh + _dot_nt(dg, wi_ref[c])
            dh = dh + _dot_nt(dup, wi_ref[c + FFN_CH])
        dx_ref[...] = ALPHA * du + dh * scale1
        acc_ref[0:1, :] += jnp.sum(dh, 0, keepdims=True)
        acc_ref[1:2, :] += jnp.sum(dh * x, 0, keepdims=True)

    tok = lambda w: pl.BlockSpec((tt, w), lambda i: (i, 0))
    by_chunk = lambda n: pl.BlockSpec((n, tt, FS), lambda i: (0, i, 0))
    return pl.pallas_call(
        body, name="ffn_bwd", grid=(L // tt,),
        in_specs=[tok(D), tok(D), _const_spec((8, D)), _const_spec((2 * FFN_CH, D, FS)), _const_spec((F, D))],
        out_specs=[tok(D), tok(D), by_chunk(2 * FFN_CH), by_chunk(FFN_CH), tok(D), pl.BlockSpec((8, D), lambda i: (0, 0))],
        out_shape=[jax.ShapeDtypeStruct((L, D), f32), jax.ShapeDtypeStruct((L, D), bf16),
                   jax.ShapeDtypeStruct((2 * FFN_CH, L, FS), bf16), jax.ShapeDtypeStruct((FFN_CH, L, FS), bf16),
                   jax.ShapeDtypeStruct((L, D), bf16), jax.ShapeDtypeStruct((8, D), f32)],
        scratch_shapes=[pltpu.VMEM((2 * FFN_CH, tt, FS), f32)],
        compiler_params=_cp(("arbitrary",)))(x, dxo, mv, w_in, w_out)


def _matmul_tn(a, b, *, tn, tk, name):
    ga, gb = a.ndim == 3, b.ndim == 3
    G = a.shape[0] if ga else (b.shape[0] if gb else 1)
    K, M = a.shape[-2:]
    N = b.shape[-1]
    nk = K // tk

    def body(a_ref, b_ref, o_ref, acc):
        k = pl.program_id(2)
        p = _dot_tn(a_ref[...], b_ref[...])

        @pl.when(k == 0)
        def _():
            acc[...] = p

        @pl.when(k > 0)
        def _():
            acc[...] += p

        @pl.when(k == nk - 1)
        def _():
            o_ref[...] = acc[...].astype(bf16)

    a_spec = (pl.BlockSpec((None, tk, M), lambda g, j, k: (g, k, 0)) if ga
              else pl.BlockSpec((tk, M), lambda g, j, k: (k, 0)))
    b_spec = (pl.BlockSpec((None, tk, tn), lambda g, j, k: (g, k, j)) if gb
              else pl.BlockSpec((tk, tn), lambda g, j, k: (k, j)))
    if ga or gb:
        o_spec, o_shape = pl.BlockSpec((None, M, tn), lambda g, j, k: (g, 0, j)), (G, M, N)
    else:
        o_spec, o_shape = pl.BlockSpec((M, tn), lambda g, j, k: (0, j)), (M, N)
    return pl.pallas_call(
        body, name=name, grid=(G, N // tn, nk), in_specs=[a_spec, b_spec], out_specs=o_spec,
        out_shape=jax.ShapeDtypeStruct(o_shape, bf16), scratch_shapes=[pltpu.VMEM((M, tn), f32)],
        compiler_params=_cp(("parallel", "parallel", "arbitrary")))(a, b)


def _inproj_fwd(x, mv, w, *, tt=512):
    L = x.shape[0]

    def body(x_ref, mv_ref, w_ref, o_ref):
        h = (x_ref[...] * (1.0 + mv_ref[1:2, :]) + mv_ref[0:1, :]).astype(bf16)
        o_ref[...] = _dot(h, w_ref[...])

    return pl.pallas_call(
        body, name="inproj_fwd", grid=(L // tt,),
        in_specs=[pl.BlockSpec((tt, D), lambda i: (i, 0)), _const_spec((8, D)), _const_spec((D, P_W))],
        out_specs=pl.BlockSpec((tt, P_W), lambda i: (i, 0)),
        out_shape=jax.ShapeDtypeStruct((L, P_W), f32), compiler_params=_cp(("parallel",)))(x, mv, w)


def _inproj_bwd(x, dx_part, dproj, mv, w, *, tt=512):
    L = x.shape[0]

    def body(x_ref, dxp_ref, dp_ref, mv_ref, w_ref, dx_ref, h_ref, acc_ref):
        @pl.when(pl.program_id(0) == 0)
        def _():
            acc_ref[...] = jnp.zeros_like(acc_ref)
        x = x_ref[...]
        scale1 = 1.0 + mv_ref[1:2, :]
        h_ref[...] = (x * scale1 + mv_ref[0:1, :]).astype(bf16)
        dh = _dot_nt(dp_ref[...], w_ref[...])
        dx_ref[...] = dxp_ref[...] + dh * scale1
        acc_ref[0:1, :] += jnp.sum(dh, 0, keepdims=True)
        acc_ref[1:2, :] += jnp.sum(dh * x, 0, keepdims=True)

    tok = lambda w_: pl.BlockSpec((tt, w_), lambda i: (i, 0))
    return pl.pallas_call(
        body, name="inproj_bwd", grid=(L // tt,),
        in_specs=[tok(D), tok(D), tok(P_W), _const_spec((8, D)), _const_spec((D, P_W))],
        out_specs=[tok(D), tok(D), pl.BlockSpec((8, D), lambda i: (0, 0))],
        out_shape=[jax.ShapeDtypeStruct((L, D), f32), jax.ShapeDtypeStruct((L, D), bf16),
                   jax.ShapeDtypeStruct((8, D), f32)],
        compiler_params=_cp(("arbitrary",)))(x, dx_part, dproj, mv, w)


def _outproj_fwd(x, ycat, mv, w, *, tt=512):
    L = x.shape[0]

    def body(x_ref, y_ref, mv_ref, w_ref, o_ref):
        y = _dot(y_ref[...], w_ref[...])
        uu = ALPHA * x_ref[...] + mv_ref[2:3, :] * y
        o_ref[...] = _ln_fwd(uu, mv_ref[3:4, :], mv_ref[4:5, :])[0]

    tok = lambda w_: pl.BlockSpec((tt, w_), lambda i: (i, 0))
    return pl.pallas_call(
        body, name="outproj_fwd", grid=(L // tt,),
        in_specs=[tok(D), tok(D), _const_spec((8, D)), _const_spec((D, D))],
        out_specs=tok(D),
        out_shape=jax.ShapeDtypeStruct((L, D), f32), compiler_params=_cp(("parallel",)))(x, ycat, mv, w)


def _outproj_bwd(x, ycat, dxo, mv, w, *, tt=512):
    L = x.shape[0]

    def body(x_ref, y_ref, dxo_ref, mv_ref, w_ref, dx_ref, dy_ref, dyc_ref, acc_ref):
        @pl.when(pl.program_id(0) == 0)
        def _():
            acc_ref[...] = jnp.zeros_like(acc_ref)
        y = _dot(y_ref[...], w_ref[...])
        gate = mv_ref[2:3, :]
        _, xhat, rstd = _ln_fwd(ALPHA * x_ref[...] + gate * y, mv_ref[3:4, :], mv_ref[4:5, :])
        du, dlg, dlb = _ln_bwd(dxo_ref[...], xhat, rstd, mv_ref[3:4, :])
        acc_ref[3:4, :] += dlg
        acc_ref[4:5, :] += dlb
        acc_ref[2:3, :] += jnp.sum(y * du, 0, keepdims=True)
        dx_ref[...] = ALPHA * du
        dyb = (gate * du).astype(bf16)
        dy_ref[...] = dyb
        dyc_ref[...] = _dot_nt(dyb, w_ref[...])

    tok = lambda w_: pl.BlockSpec((tt, w_), lambda i: (i, 0))
    return pl.pallas_call(
        body, name="outproj_bwd", grid=(L // tt,),
        in_specs=[tok(D), tok(D), tok(D), _const_spec((8, D)), _const_spec((D, D))],
        out_specs=[tok(D), tok(D), tok(D), pl.BlockSpec((8, D), lambda i: (0, 0))],
        out_shape=[jax.ShapeDtypeStruct((L, D), f32), jax.ShapeDtypeStruct((L, D), bf16),
                   jax.ShapeDtypeStruct((L, D), f32), jax.ShapeDtypeStruct((8, D), f32)],
        compiler_params=_cp(("arbitrary",)))(x, ycat, dxo, mv, w)


def _sconv_fwd(proj, w, *, tt=512):
    L = proj.shape[0]
    cb = SC_W

    def body(b_ref, c_ref, x_ref, cp_ref, xp_ref, w_ref, o_ref):
        first = jnp.where(pl.program_id(0) > 0, 1.0, 0.0)
        u = c_ref[...] * x_ref[...]
        up = cp_ref[...] * xp_ref[...] * first
        v = w_ref[2:3, :] * u + w_ref[1:2, :] * _shift_down(u, up, 1) + w_ref[0:1, :] * _shift_down(u, up, 2)
        o_ref[...] = (b_ref[...] * v).astype(bf16)

    cur = lambda col: pl.BlockSpec((tt, cb), lambda i: (i, col // cb))
    prev = lambda col: pl.BlockSpec((tt, cb), lambda i: (jnp.maximum(i - 1, 0), col // cb))
    return pl.pallas_call(
        body, name="sconv_fwd", grid=(L // tt,),
        in_specs=[cur(P_SB), cur(P_SC), cur(P_SX), prev(P_SC), prev(P_SX), _const_spec((8, cb))],
        out_specs=pl.BlockSpec((tt, cb), lambda i: (i, 0)),
        out_shape=jax.ShapeDtypeStruct((L, cb), bf16), compiler_params=_cp(("parallel",)))(proj, proj, proj, proj, proj, w)


def _sconv_bwd(proj, dycat, w, *, tt=512):
    L = proj.shape[0]
    cb = SC_W
    n = L // tt

    def body(b_ref, c_ref, x_ref, cp_ref, xp_ref, bn_ref, dy_ref, dyn_ref, w_ref, db_ref, dc_ref, dx_ref, acc_ref):
        i = pl.program_id(0)

        @pl.when(i == 0)
        def _():
            acc_ref[...] = jnp.zeros_like(acc_ref)
        first = jnp.where(i > 0, 1.0, 0.0)
        last = jnp.where(i < n - 1, 1.0, 0.0)
        cg, xin, bg = c_ref[...], x_ref[...], b_ref[...]
        u = cg * xin
        up = cp_ref[...] * xp_ref[...] * first
        u1, u2 = _shift_down(u, up, 1), _shift_down(u, up, 2)
        v = w_ref[2:3, :] * u + w_ref[1:2, :] * u1 + w_ref[0:1, :] * u2
        dy = dy_ref[...]
        db_ref[...] = (dy * v).astype(bf16)
        dv = dy * bg
        dvn = dyn_ref[...] * bn_ref[...] * last
        du = w_ref[2:3, :] * dv + w_ref[1:2, :] * _shift_up(dv, dvn, 1) + w_ref[0:1, :] * _shift_up(dv, dvn, 2)
        acc_ref[2:3, :] += jnp.sum(dv * u, 0, keepdims=True)
        acc_ref[1:2, :] += jnp.sum(dv * u1, 0, keepdims=True)
        acc_ref[0:1, :] += jnp.sum(dv * u2, 0, keepdims=True)
        dc_ref[...] = (du * xin).astype(bf16)
        dx_ref[...] = (du * cg).astype(bf16)

    cur = lambda col: pl.BlockSpec((tt, cb), lambda i: (i, col // cb))
    prev = lambda col: pl.BlockSpec((tt, cb), lambda i: (jnp.maximum(i - 1, 0), col // cb))
    nxt = lambda col: pl.BlockSpec((tt, cb), lambda i: (jnp.minimum(i + 1, n - 1), col // cb))
    ycol = SSD_W + FOX_W
    out = pl.BlockSpec((tt, cb), lambda i: (i, 0))
    return pl.pallas_call(
        body, name="sconv_bwd", grid=(n,),
        in_specs=[cur(P_SB), cur(P_SC), cur(P_SX), prev(P_SC), prev(P_SX), nxt(P_SB), cur(ycol), nxt(ycol),
                  _const_spec((8, cb))],
        out_specs=[out, out, out, pl.BlockSpec((8, cb), lambda i: (0, 0))],
        out_shape=[jax.ShapeDtypeStruct((L, cb), bf16)] * 3 + [jax.ShapeDtypeStruct((8, cb), f32)],
        compiler_params=_cp(("arbitrary",)))(proj, proj, proj, proj, proj, proj, dycat, dycat, w)


def _log1pexp(x):
    return jnp.log(1.0 + jnp.exp(-jnp.abs(x)))


def _fox_gate_fwd(proj, hp, *, tt=256):
    L = proj.shape[0]

    def body(sm_ref, hp_ref, cum_ref, cumt_ref, carry):
        @pl.when(pl.program_id(0) == 0)
        def _():
            carry[...] = jnp.zeros_like(carry)
        xx = sm_ref[...] + hp_ref[3:4, :]
        logf = jnp.minimum(xx, 0.0) - _log1pexp(xx)
        r = lax.broadcasted_iota(jnp.int32, (tt, tt), 0)
        c = lax.broadcasted_iota(jnp.int32, (tt, tt), 1)
        cum = _dot_hi(jnp.where(r >= c, 1.0, 0.0), logf) + carry[0:1, :]
        cum_ref[...] = cum
        cumt_ref[...] = cum.T
        carry[0:1, :] = cum[tt - 1:tt, :]

    return pl.pallas_call(
        body, name="fox_gate_fwd", grid=(L // tt,),
        in_specs=[pl.BlockSpec((tt, 128), lambda i: (i, P_SM // 128)), _const_spec((8, 128))],
        out_specs=[pl.BlockSpec((tt, 128), lambda i: (i, 0)), pl.BlockSpec((128, tt), lambda i: (0, i))],
        out_shape=[jax.ShapeDtypeStruct((L, 128), f32), jax.ShapeDtypeStruct((128, L), f32)],
        scratch_shapes=[pltpu.VMEM((8, 128), f32)],
        compiler_params=_cp(("arbitrary",)))(proj, hp)


def _fox_gate_bwd(dcumt, proj, ddt, hp, *, tt=256):
    L = proj.shape[0]
    n = L // tt

    def body(dct_ref, sm_ref, ddt_ref, hp_ref, dsm_ref, acc_ref, carry):
        @pl.when(pl.program_id(0) == 0)
        def _():
            carry[...] = jnp.zeros_like(carry)
            acc_ref[...] = jnp.zeros_like(acc_ref)
        dc = dct_ref[...].T
        r = lax.broadcasted_iota(jnp.int32, (tt, tt), 0)
        c = lax.broadcasted_iota(jnp.int32, (tt, tt), 1)
        dl = _dot_hi(jnp.where(r <= c, 1.0, 0.0), dc) + carry[0:1, :]
        carry[0:1, :] += jnp.sum(dc, 0, keepdims=True)
        xx = sm_ref[...] + hp_ref[3:4, :]
        lane = lax.broadcasted_iota(jnp.int32, (tt, 128), 1)
        dlogit = jnp.where((lane >= SM_F) & (lane < SM_F + FOX_H), dl * _sigmoid(-xx), 0.0)
        acc_ref[3:4, :] += jnp.sum(dlogit, 0, keepdims=True)
        dsm_ref[...] = (dlogit + ddt_ref[...]).astype(bf16)

    return pl.pallas_call(
        body, name="fox_gate_bwd", grid=(n,),
        in_specs=[pl.BlockSpec((128, tt), lambda i: (0, n - 1 - i)),
                  pl.BlockSpec((tt, 128), lambda i: (n - 1 - i, P_SM // 128)),
                  pl.BlockSpec((tt, 128), lambda i: (n - 1 - i, 0)), _const_spec((8, 128))],
        out_specs=[pl.BlockSpec((tt, 128), lambda i: (n - 1 - i, 0)), pl.BlockSpec((8, 128), lambda i: (0, 0))],
        out_shape=[jax.ShapeDtypeStruct((L, 128), bf16), jax.ShapeDtypeStruct((8, 128), f32)],
        scratch_shapes=[pltpu.VMEM((8, 128), f32)],
        compiler_params=_cp(("arbitrary",)))(dcumt, proj, ddt, hp)


NEG = -1e30
FOX_SCALE = FOX_HD ** -0.5


def _fox_fwd(proj, cum, cumt, *, tq=256):
    L = proj.shape[0]
    heads = [slice(h * FOX_HD, (h + 1) * FOX_HD) for h in range(FOX_H)]

    def body(q_ref, k_ref, v_ref, cq_ref, ct_ref, o_ref, oa_ref, lse_ref):
        i = pl.program_id(0)
        row = lax.broadcasted_iota(jnp.int32, (tq, tq), 0)
        col = lax.broadcasted_iota(jnp.int32, (tq, tq), 1)
        diag_bias = jnp.where(row >= col, 0.0, NEG)
        qs = [(q_ref[:, hs] * FOX_SCALE).astype(bf16) for hs in heads]
        cqs = [cq_ref[:, SM_F + h:SM_F + h + 1] for h in range(FOX_H)]

        def block(r0, carry, bias):
            out = []
            for h, hs in enumerate(heads):
                m, l, acc, acc_lo = carry[h]
                kj = k_ref[pl.ds(r0, tq), hs].astype(bf16)
                vj = v_ref[pl.ds(r0, tq), hs].astype(bf16)
                s = _dot_nt(qs[h], kj) + (cqs[h] - ct_ref[SM_F + h:SM_F + h + 1, pl.ds(r0, tq)])
                if bias is not None:
                    s = s + bias
                m_new = jnp.maximum(m, jnp.max(s, -1, keepdims=True))
                p = jnp.exp(s - m_new)
                al = jnp.exp(m - m_new)
                pb = p.astype(bf16)
                p_lo = (p - pb.astype(f32)).astype(bf16)
                out.append((m_new, al * l + jnp.sum(p, -1, keepdims=True),
                            al * acc + _dot(pb, vj), al * acc_lo + _dot(p_lo, vj)))
            return tuple(out)

        init = tuple((jnp.full((tq, 1), NEG, f32), jnp.zeros((tq, 1), f32), jnp.zeros((tq, FOX_HD), f32),
                      jnp.zeros((tq, FOX_HD), f32)) for _ in heads)
        carry = lax.fori_loop(0, i, lambda j, c: block(pl.multiple_of(j * tq, tq), c, None), init)
        carry = block(pl.multiple_of(i * tq, tq), carry, diag_bias)
        lane = lax.broadcasted_iota(jnp.int32, (tq, 128), 1)
        lse_all = jnp.zeros((tq, 128), f32)
        for h, hs in enumerate(heads):
            m, l, acc, acc_lo = carry[h]
            inv = 1.0 / l
            o_ref[:, hs] = acc * inv
            oa_ref[:, hs] = (acc + acc_lo) * inv
            lse_all = jnp.where(lane == h, m + jnp.log(l), lse_all)
        lse_ref[...] = lse_all

    return pl.pallas_call(
        body, name="fox_fwd", grid=(L // tq,),
        in_specs=[pl.BlockSpec((tq, FOX_W), lambda i: (i, P_Q // FOX_W)),
                  pl.BlockSpec((L, FOX_W), lambda i: (0, P_K // FOX_W), pipeline_mode=pl.Buffered(1)),
                  pl.BlockSpec((L, FOX_W), lambda i: (0, P_V // FOX_W), pipeline_mode=pl.Buffered(1)),
                  pl.BlockSpec((tq, 128), lambda i: (i, 0)), _const_spec((128, L))],
        out_specs=[pl.BlockSpec((tq, FOX_W), lambda i: (i, 0)), pl.BlockSpec((tq, FOX_W), lambda i: (i, 0)),
                   pl.BlockSpec((tq, 128), lambda i: (i, 0))],
        out_shape=[jax.ShapeDtypeStruct((L, FOX_W), f32), jax.ShapeDtypeStruct((L, FOX_W), f32),
                   jax.ShapeDtypeStruct((L, 128), f32)],
        compiler_params=_cp(("parallel",)))(proj, proj, proj, cum, cumt)


def _fox_bwd(proj, dycat, o_acc, lse, cum, cumt, *, tq=256):
    L = proj.shape[0]
    nq = L // tq
    heads = [slice(h * FOX_HD, (h + 1) * FOX_HD) for h in range(FOX_H)]

    def body(q_ref, k_ref, v_ref, do_ref, oa_ref, lse_ref, cum_ref, ct_ref, dq_ref, dk_ref, dv_ref, dct_ref):
        j = pl.program_id(0)

        @pl.when(j == 0)
        def _():
            dq_ref[...] = jnp.zeros_like(dq_ref)
        row = lax.broadcasted_iota(jnp.int32, (tq, tq), 0)
        col = lax.broadcasted_iota(jnp.int32, (tq, tq), 1)
        diag_bias = jnp.where(row >= col, 0.0, NEG)
        kjs = [k_ref[:, hs].astype(bf16) for hs in heads]
        vjs = [v_ref[:, hs].astype(bf16) for hs in heads]
        cks = [ct_ref[SM_F + h:SM_F + h + 1, :] for h in range(FOX_H)]

        def block(r0, carry, bias):
            out = []
            for h, hs in enumerate(heads):
                dk, dv, dc = carry[h]
                qi = (q_ref[pl.ds(r0, tq), hs] * FOX_SCALE).astype(bf16)
                dob = do_ref[pl.ds(r0, tq), hs].astype(bf16)
                delta = jnp.sum(dob.astype(f32) * oa_ref[pl.ds(r0, tq), hs], -1, keepdims=True)
                s = _dot_nt(qi, kjs[h]) + (cum_ref[pl.ds(r0, tq), SM_F + h:SM_F + h + 1] - cks[h])
                if bias is not None:
                    s = s + bias
                p = jnp.exp(s - lse_ref[pl.ds(r0, tq), h:h + 1])
                ds = p * (_dot_nt(dob, vjs[h]) - delta)
                dsb = ds.astype(bf16)
                dq_ref[pl.ds(r0, tq), hs] += _dot(dsb, kjs[h]) * FOX_SCALE
                out.append((dk + _dot_tn(dsb, qi), dv + _dot_tn(p.astype(bf16), dob),
                            dc - jnp.sum(ds, 0, keepdims=True)))
            return tuple(out)

        init = tuple((jnp.zeros((tq, FOX_HD), f32), jnp.zeros((tq, FOX_HD), f32), jnp.zeros((1, tq), f32))
                     for _ in heads)
        carry = block(pl.multiple_of(j * tq, tq), init, diag_bias)
        carry = lax.fori_loop(j + 1, nq, lambda i, c: block(pl.multiple_of(i * tq, tq), c, None), carry)
        sub = lax.broadcasted_iota(jnp.int32, (128, tq), 0)
        dct = jnp.zeros((128, tq), f32)
        for h, hs in enumerate(heads):
            dk, dv, dc = carry[h]
            dk_ref[:, hs] = dk
            dv_ref[:, hs] = dv
            dct = jnp.where(sub == SM_F + h, dc, dct)
        dct_ref[...] = dct

    full = lambda w_, col: pl.BlockSpec((L, w_), lambda j: (0, col // w_), pipeline_mode=pl.Buffered(1))
    blk = lambda col: pl.BlockSpec((tq, FOX_W), lambda j: (j, col // FOX_W))
    return pl.pallas_call(
        body, name="fox_bwd", grid=(nq,),
        in_specs=[full(FOX_W, P_Q), blk(P_K), blk(P_V), full(FOX_W, SSD_W), full(FOX_W, 0), full(128, 0), full(128, 0),
                  pl.BlockSpec((128, tq), lambda j: (0, j))],
        out_specs=[pl.BlockSpec((L, FOX_W), lambda j: (0, 0)), pl.BlockSpec((tq, FOX_W), lambda j: (j, 0)),
                   pl.BlockSpec((tq, FOX_W), lambda j: (j, 0)), pl.BlockSpec((128, tq), lambda j: (0, j))],
        out_shape=[jax.ShapeDtypeStruct((L, FOX_W), f32)] * 3 + [jax.ShapeDtypeStruct((128, L), f32)],
        compiler_params=_cp(("arbitrary",)))(proj, proj, proj, dycat, o_acc, lse, cum, cumt)


SSD_GW = SSD_W // SSD_G
SSD_HPG = SSD_H // SSD_G


def _ssd_pre(x, xprev, sm, cp_ref, hp_ref, tc):
    pre = (cp_ref[4:5, :] + cp_ref[3:4, :] * x + cp_ref[2:3, :] * _shift_down(x, xprev, 1)
           + cp_ref[1:2, :] * _shift_down(x, xprev, 2) + cp_ref[0:1, :] * _shift_down(x, xprev, 3))
    sig = _sigmoid(pre)
    raw = sm + hp_ref[0:1, :]
    dt = jnp.maximum(raw, 0.0) + _log1pexp(raw)
    a_neg = -jnp.exp(hp_ref[1:2, :])
    r = lax.broadcasted_iota(jnp.int32, (tc, tc), 0)
    c = lax.broadcasted_iota(jnp.int32, (tc, tc), 1)
    cs = _dot_hi(jnp.where(r >= c, 1.0, 0.0), dt * a_neg)
    return pre, sig, raw, dt, a_neg, cs, cs.T, r >= c


def _ssd_fwd(proj, cp, hp, ng, *, tc=256):
    L = proj.shape[0]
    nc = L // tc

    def body(xc_ref, xp_ref, z_ref, sm_ref, cp_ref, hp_ref, ng_ref, y_ref, ypre_ref, sin_ref, s_scr):
        i = pl.program_id(0)

        @pl.when(i == 0)
        def _():
            s_scr[...] = jnp.zeros_like(s_scr)
        x = xc_ref[...]
        xprev = xp_ref[...] * jnp.where(i > 0, 1.0, 0.0)
        pre, sig, _, dt, _, cs, cst, tril = _ssd_pre(x, xprev, sm_ref[...], cp_ref, hp_ref, tc)
        xbc = pre * sig
        sin_ref[...] = s_scr[...]
        for g in range(SSD_G):
            bg = xbc[:, SSD_W + g * SSD_N:SSD_W + (g + 1) * SSD_N]
            cg = xbc[:, SSD_W + SSD_G * SSD_N + g * SSD_N:SSD_W + SSD_G * SSD_N + (g + 1) * SSD_N].astype(bf16)
            cb = _dot_nt(cg, bg.astype(bf16))
            for e in range(SSD_HPG):
                h = g * SSD_HPG + e
                hs = slice(h * SSD_HD, (h + 1) * SSD_HD)
                xs = xbc[:, hs]
                csc = cs[:, h:h + 1]
                lm = jnp.where(tril, jnp.exp(jnp.minimum(csc - cst[h:h + 1, :], 0.0)), 0.0)
                xdt = (xs * dt[:, h:h + 1]).astype(bf16)
                s_h = s_scr[:, hs]
                y = _dot((cb * lm).astype(bf16), xdt) + jnp.exp(csc) * _dot(cg, s_h.astype(bf16))
                ypre_ref[:, hs] = y + hp_ref[2:3, h:h + 1] * xs
                cl = cs[tc - 1:tc, h:h + 1]
                bd = (bg * jnp.exp(cl - csc)).astype(bf16)
                s_scr[:, hs] = jnp.exp(cl) * s_h + _dot_tn(bd, xdt)
        z = z_ref[...]
        yz = ypre_ref[...] * (z * _sigmoid(z))
        for g in range(SSD_G):
            gs = slice(g * SSD_GW, (g + 1) * SSD_GW)
            yg = yz[:, gs]
            r = lax.rsqrt(jnp.mean(yg * yg, -1, keepdims=True) + RMS_EPS)
            y_ref[:, gs] = (yg * r * ng_ref[0:1, gs]).astype(bf16)

    return pl.pallas_call(
        body, name="ssd_fwd", grid=(nc,),
        in_specs=[pl.BlockSpec((tc, 1024), lambda i: (i, 0)),
                  pl.BlockSpec((tc, 1024), lambda i: (jnp.maximum(i - 1, 0), 0)),
                  pl.BlockSpec((tc, SSD_W), lambda i: (i, P_Z // SSD_W)),
                  pl.BlockSpec((tc, 128), lambda i: (i, P_SM // 128)),
                  _const_spec((8, 1024)), _const_spec((8, 128)), _const_spec((8, SSD_W))],
        out_specs=[pl.BlockSpec((tc, SSD_W), lambda i: (i, 0)), pl.BlockSpec((tc, SSD_W), lambda i: (i, 0)),
                   pl.BlockSpec((SSD_N, SSD_W), lambda i: (i, 0))],
        out_shape=[jax.ShapeDtypeStruct((L, SSD_W), bf16), jax.ShapeDtypeStruct((L, SSD_W), f32),
                   jax.ShapeDtypeStruct((nc * SSD_N, SSD_W), f32)],
        scratch_shapes=[pltpu.VMEM((SSD_N, SSD_W), f32)],
        compiler_params=_cp(("arbitrary",)))(proj, proj, proj, proj, cp, hp, ng)


def _ssd_bwd(proj, dycat, ypre, sin, cp, hp, ng, *, tc=256):
    L = proj.shape[0]
    nc = L // tc

    def body(xc_ref, xp_ref, z_ref, sm_ref, cp_ref, hp_ref, ng_ref, sin_ref, ypre_ref, dy_ref,
             dxbc_ref, dz_ref, ddt_ref, acc1_ref, acc2_ref, ds_scr, dnext_scr, dxbc_scr):
        i = pl.program_id(0)
        c_idx = nc - 1 - i

        @pl.when(i == 0)
        def _():
            ds_scr[...] = jnp.zeros_like(ds_scr)
            dnext_scr[...] = jnp.zeros_like(dnext_scr)
            acc1_ref[...] = jnp.zeros_like(acc1_ref)
            acc2_ref[...] = jnp.zeros_like(acc2_ref)
        x = xc_ref[...]
        xprev = xp_ref[...] * jnp.where(c_idx > 0, 1.0, 0.0)
        pre, sig, raw, dt, a_neg, cs, cst, tril = _ssd_pre(x, xprev, sm_ref[...], cp_ref, hp_ref, tc)
        xbc = pre * sig
        z = z_ref[...]
        sz = _sigmoid(z)
        silz = z * sz
        yall = ypre_ref[...]
        yz = yall * silz
        dy = dy_ref[...]
        dyz_parts = []
        for g in range(SSD_G):
            gs = slice(g * SSD_GW, (g + 1) * SSD_GW)
            yg, dyg = yz[:, gs], dy[:, gs]
            r = lax.rsqrt(jnp.mean(yg * yg, -1, keepdims=True) + RMS_EPS)
            acc1_ref[5:6, gs] += jnp.sum(dyg * yg * r, 0, keepdims=True)
            dyn = dyg * ng_ref[0:1, gs]
            dyz_parts.append(r * (dyn - yg * (r * r) * jnp.mean(dyn * yg, -1, keepdims=True)))
        dyz = jnp.concatenate(dyz_parts, axis=1)
        dz_ref[...] = (dyz * yall * (sz * (1.0 + z * (1.0 - sz)))).astype(bf16)
        dyall = dyz * silz

        lane1 = lax.broadcasted_iota(jnp.int32, (1, 128), 1)
        sub = lax.broadcasted_iota(jnp.int32, (128, tc), 0)
        rowc = lax.broadcasted_iota(jnp.int32, (tc, 1), 0)
        dcs = jnp.zeros((tc, 128), f32)
        dcsr = jnp.zeros((128, tc), f32)
        ddt = jnp.zeros((tc, 128), f32)
        dd_row = jnp.zeros((1, 128), f32)
        for g in range(SSD_G):
            b0 = SSD_W + g * SSD_N
            c0 = SSD_W + SSD_G * SSD_N + g * SSD_N
            bg = xbc[:, b0:b0 + SSD_N]
            bgb = bg.astype(bf16)
            cgb = xbc[:, c0:c0 + SSD_N].astype(bf16)
            cb = _dot_nt(cgb, bgb)
            dbg = jnp.zeros((tc, SSD_N), f32)
            dcg = jnp.zeros((tc, SSD_N), f32)
            for e in range(SSD_HPG):
                h = g * SSD_HPG + e
                hs = slice(h * SSD_HD, (h + 1) * SSD_HD)
                oh = jnp.where(lane1 == h, 1.0, 0.0)
                xs = xbc[:, hs]
                dth = dt[:, h:h + 1]
                csc = cs[:, h:h + 1]
                lm = jnp.where(tril, jnp.exp(jnp.minimum(csc - cst[h:h + 1, :], 0.0)), 0.0)
                m = cb * lm
                xdt = (xs * dth).astype(bf16)
                s_h = sin_ref[:, hs]
                s_hb = s_h.astype(bf16)
                dyh = dyall[:, hs]
                dyb = dyh.astype(bf16)
                dd_row = dd_row + oh * jnp.sum(dyh * xs)
                dxs = hp_ref[2:3, h:h + 1] * dyh
                ecs = jnp.exp(csc)
                cs_prod = _dot(cgb, s_hb)
                dcsb = (ecs * dyh).astype(bf16)
                dcg = dcg + _dot_nt(dcsb, s_hb)
                ds_in = _dot_tn(cgb, dcsb)
                dcs_h = jnp.sum(dyh * ecs * cs_prod, -1, keepdims=True)
                dm = _dot_nt(dyb, xdt)
                w = dm * m
                dcs_h = dcs_h + jnp.sum(w, -1, keepdims=True)
                dcsr = jnp.where(sub == h, jnp.sum(w, 0, keepdims=True), dcsr)
                dcbb = (dm * lm).astype(bf16)
                dcg = dcg + _dot(dcbb, bgb)
                dbg = dbg + _dot_tn(dcbb, cgb)
                dxdt = _dot_tn(m.astype(bf16), dyb)
                dsn = ds_scr[:, hs]
                dsnb = dsn.astype(bf16)
                cl = cs[tc - 1:tc, h:h + 1]
                dec = jnp.exp(cl - csc)
                dxdt = dxdt + _dot((bg * dec).astype(bf16), dsnb)
                dbd = _dot_nt(xdt, dsnb)
                dbg = dbg + dbd * dec
                gdec = jnp.sum(dbd * bg, -1, keepdims=True) * dec
                ecl = jnp.exp(cl)
                dcl = jnp.sum(gdec) + jnp.sum(dsn * s_h) * ecl
                ds_scr[:, hs] = ecl * dsn + ds_in
                dcs_h = dcs_h - gdec + jnp.where(rowc == tc - 1, dcl, 0.0)
                dcs = dcs + dcs_h * oh
                dxbc_scr[:, hs] = dxs + dxdt * dth
                ddt = ddt + jnp.sum(dxdt * xs, -1, keepdims=True) * oh
            dxbc_scr[:, b0:b0 + SSD_N] = dbg
            dxbc_scr[:, c0:c0 + SSD_N] = dcg
        dcs = dcs - dcsr.T
        r_i = lax.broadcasted_iota(jnp.int32, (tc, tc), 0)
        c_i = lax.broadcasted_iota(jnp.int32, (tc, tc), 1)
        da = _dot_hi(jnp.where(r_i <= c_i, 1.0, 0.0), dcs)
        ddt = ddt + da * a_neg
        acc2_ref[1:2, :] += jnp.sum(da * dt, 0, keepdims=True) * a_neg
        lane = lax.broadcasted_iota(jnp.int32, (tc, 128), 1)
        ddraw = jnp.where(lane < SSD_H, ddt * _sigmoid(raw), 0.0)
        acc2_ref[0:1, :] += jnp.sum(ddraw, 0, keepdims=True)
        acc2_ref[2:3, :] += dd_row
        ddt_ref[...] = ddraw
        dpre = dxbc_scr[...] * (sig * (1.0 + pre * (1.0 - sig)))
        acc1_ref[4:5, :] += jnp.sum(dpre, 0, keepdims=True)
        for k in range(SSD_K):
            acc1_ref[k:k + 1, :] += jnp.sum(dpre * _shift_down(x, xprev, SSD_K - 1 - k), 0, keepdims=True)
        dnext = dnext_scr[...]
        dxbc_ref[...] = (cp_ref[3:4, :] * dpre + cp_ref[2:3, :] * _shift_up(dpre, dnext, 1)
                         + cp_ref[1:2, :] * _shift_up(dpre, dnext, 2)
                         + cp_ref[0:1, :] * _shift_up(dpre, dnext, 3)).astype(bf16)
        dnext_scr[...] = dpre

    rev = lambda w_, col: pl.BlockSpec((tc, w_), lambda i: (nc - 1 - i, col // w_))
    return pl.pallas_call(
        body, name="ssd_bwd", grid=(nc,),
        in_specs=[rev(1024, 0), pl.BlockSpec((tc, 1024), lambda i: (jnp.maximum(nc - 2 - i, 0), 0)),
                  rev(SSD_W, P_Z), rev(128, P_SM),
                  _const_spec((8, 1024)), _const_spec((8, 128)), _const_spec((8, SSD_W)),
                  pl.BlockSpec((SSD_N, SSD_W), lambda i: (nc - 1 - i, 0)), rev(SSD_W, 0), rev(SSD_W, 0)],
        out_specs=[rev(1024, 0), rev(SSD_W, 0), rev(128, 0),
                   pl.BlockSpec((8, 1024), lambda i: (0, 0)), pl.BlockSpec((8, 128), lambda i: (0, 0))],
        out_shape=[jax.ShapeDtypeStruct((L, 1024), bf16), jax.ShapeDtypeStruct((L, SSD_W), bf16),
                   jax.ShapeDtypeStruct((L, 128), f32), jax.ShapeDtypeStruct((8, 1024), f32),
                   jax.ShapeDtypeStruct((8, 128), f32)],
        scratch_shapes=[pltpu.VMEM((SSD_N, SSD_W), f32), pltpu.VMEM((tc, 1024), f32), pltpu.VMEM((tc, 1024), f32)],
        compiler_params=_cp(("arbitrary",)))(proj, proj, proj, proj, cp, hp, ng, sin, ypre, dycat)


def _pack_cols(w):
    pad = jnp.zeros(w.shape[:-1] + (P_W - P_SM - SSD_H - FOX_H,), w.dtype)
    return jnp.concatenate([w[..., 512:1536], w[..., 0:512], w[..., 1544:2312], w[..., 2316:3084],
                            w[..., 1536:1544], w[..., 2312:2316], pad], axis=-1)


def _unpack_cols(g):
    return jnp.concatenate([g[..., 1024:1536], g[..., 0:1024], g[..., 3072:3080], g[..., 1536:2304],
                            g[..., 3080:3084], g[..., 2304:3072]], axis=-1)


def _rows8(*rows):
    width = max(r.shape[-1] for r in rows)
    out = [jnp.pad(r.astype(f32), (0, width - r.shape[-1])) for r in rows]
    out += [jnp.zeros((width,), f32)] * (8 - len(out))
    return jnp.stack(out)


def _local_step(x, tgt, p):
    gb_in = _rows8(p["ln_in_g"], p["ln_in_b"])
    x0 = _ln_in_fwd(x, gb_in)
    saved = []
    for l in range(DEPTH):
        mv = [_rows8(p["mod"][l, 3 * j], p["mod"][l, 3 * j + 1], p["mod"][l, 3 * j + 2], p["ln_g"][l, j], p["ln_b"][l, j])
              for j in range(3)]
        cp = _rows8(*[p["ssd_conv_w"][l, k] for k in range(SSD_K)], p["ssd_conv_b"][l])
        hp = _rows8(jnp.pad(p["ssd_dt_bias"][l], (0, 120)), jnp.pad(p["ssd_a_log"][l], (0, 120)),
                    jnp.pad(p["ssd_d"][l], (0, 120)), jnp.pad(p["fox_f_bias"][l], (SM_F, 128 - SM_F - FOX_H)))
        ng = _rows8(p["ssd_norm_g"][l])
        scw = _rows8(*[p["sconv_w"][l, k] for k in range(SC_K)])
        x1 = _ffn_fwd(x0, mv[0], p["ffn1_w_in"][l], p["ffn1_w_out"][l])
        proj = _inproj_fwd(x1, mv[1], p["mix_w_in"][l])
        y_ssd, ypre, sin = _ssd_fwd(proj, cp, hp, ng)
        cum, cumt = _fox_gate_fwd(proj, hp)
        o, o_acc, lse = _fox_fwd(proj, cum, cumt)
        y_sc = _sconv_fwd(proj, scw)
        ycat = jnp.concatenate([y_ssd, o.astype(bf16), y_sc], axis=1)
        x2 = _outproj_fwd(x1, ycat, mv[1], p["mix_w_out"][l])
        x3 = _ffn_fwd(x2, mv[2], p["ffn2_w_in"][l], p["ffn2_w_out"][l])
        saved.append((x0, x1, x2, mv, cp, hp, ng, scw, proj, ypre, sin, cum, cumt, o_acc, lse, ycat))
        x0 = x3
    dx, loss_acc = _loss_head(x0, tgt)

    g = {k: [None] * DEPTH for k in (
        "ffn1_w_in", "ffn1_w_out", "ffn2_w_in", "ffn2_w_out", "mix_w_in", "mix_w_out", "mod", "ln_g", "ln_b",
        "ssd_conv_w", "ssd_conv_b", "ssd_dt_bias", "ssd_a_log", "ssd_d", "ssd_norm_g", "fox_f_bias", "sconv_w")}
    for l in reversed(range(DEPTH)):
        x0, x1, x2, mv, cp, hp, ng, scw, proj, ypre, sin, cum, cumt, o_acc, lse, ycat = saved[l]
        dx, h, da, act, dyb, a2 = _ffn_bwd(x2, dx, mv[2], p["ffn2_w_in"][l], p["ffn2_w_out"][l])
        g["ffn2_w_in"][l] = _matmul_tn(h, da, tn=FS, tk=512, name="dw_ffn_in")
        g["ffn2_w_out"][l] = _matmul_tn(act, dyb, tn=D, tk=512, name="dw_ffn_out")
        dxp, dyb, dycat, a1 = _outproj_bwd(x1, ycat, dx, mv[1], p["mix_w_out"][l])
        g["mix_w_out"][l] = _matmul_tn(ycat, dyb, tn=D // 2, tk=512, name="dw_mix_out")
        dxbc, dz, ddt, acc1, acc2 = _ssd_bwd(proj, dycat, ypre, sin, cp, hp, ng)
        dq, dk, dv, dcumt = _fox_bwd(proj, dycat, o_acc, lse, cum, cumt)
        dsm, accf = _fox_gate_bwd(dcumt, proj, ddt, hp)
        dsb, dsc, dsx, accs = _sconv_bwd(proj, dycat, scw)
        dproj = jnp.concatenate([dxbc, dz, dq.astype(bf16), dk.astype(bf16), dv.astype(bf16), dsb, dsc, dsx, dsm], axis=1)
        dx, h, a1b = _inproj_bwd(x1, dxp, dproj, mv[1], p["mix_w_in"][l])
        g["mix_w_in"][l] = _matmul_tn(h, dproj, tn=P_W // 5, tk=512, name="dw_mix_in")
        dx, h, da, act, dyb, a0 = _ffn_bwd(x0, dx, mv[0], p["ffn1_w_in"][l], p["ffn1_w_out"][l])
        g["ffn1_w_in"][l] = _matmul_tn(h, da, tn=FS, tk=512, name="dw_ffn_in")
        g["ffn1_w_out"][l] = _matmul_tn(act, dyb, tn=D, tk=512, name="dw_ffn_out")
        g["mod"][l] = jnp.concatenate([a0[0:3], a1b[0:2], a1[2:3], a2[0:3]], axis=0)
        g["ln_g"][l] = jnp.stack([a0[3], a1[3], a2[3]])
        g["ln_b"][l] = jnp.stack([a0[4], a1[4], a2[4]])
        g["ssd_conv_w"][l] = acc1[0:SSD_K]
        g["ssd_conv_b"][l] = acc1[4]
        g["ssd_norm_g"][l] = acc1[5, :SSD_W]
        g["ssd_dt_bias"][l] = acc2[0, :SSD_H]
        g["ssd_a_log"][l] = acc2[1, :SSD_H]
        g["ssd_d"][l] = acc2[2, :SSD_H]
        g["fox_f_bias"][l] = accf[3, SM_F:SM_F + FOX_H]
        g["sconv_w"][l] = accs[0:SC_K]
    grad_x, a_in = _ln_in_bwd(x, dx, gb_in)
    g = {k: (v if k in BIG else jnp.stack(v)) for k, v in g.items()}
    g["ln_in_g"], g["ln_in_b"] = a_in[0], a_in[1]
    return loss_acc[0, 0], grad_x, g


MESH = pl.DeviceIdType.MESH
ANY = pl.BlockSpec(memory_space=pl.ANY)


def _all_gather(shards, *, in_vmem, name):
    n_arr = len(shards)

    def body(*refs):
        x_refs, out_refs = refs[:n_arr], refs[n_arr:2 * n_arr]
        send_sems, recv_sems, local_sems = refs[2 * n_arr:]
        x, y, c = lax.axis_index("x"), lax.axis_index("y"), lax.axis_index("c")
        me, sibling = (x, y, c), (x, y, 1 - c)
        chips = [(1 - x, y), (x, 1 - y), (1 - x, 1 - y)]

        def copy(a, k, block, to, src=None):
            px, py, pc = block
            slot = out_refs[a].at[4 * px + 2 * py + pc]
            return pltpu.make_async_remote_copy(
                src_ref=slot if src is None else src, dst_ref=slot,
                send_sem=send_sems.at[7 * a + k], recv_sem=recv_sems.at[7 * a + k], device_id=to, device_id_type=MESH)

        mine, first, passed = [], [], []
        for a in range(n_arr):
            mine.append(pltpu.make_async_copy(x_refs[a], out_refs[a].at[4 * x + 2 * y + c], local_sems.at[a]))
            mine[-1].start()
            first.append(copy(a, 0, me, sibling, src=x_refs[a]))
            first += [copy(a, 1 + j, me, (*chip, c), src=x_refs[a]) for j, chip in enumerate(chips)]
        for cp in first:
            cp.start()
        for j, chip in enumerate(chips):
            for a in range(n_arr):
                copy(a, 1 + j, (*chip, c), me).wait_recv()
                passed.append(copy(a, 4 + j, (*chip, c), sibling))
                passed[-1].start()
        for a in range(n_arr):
            copy(a, 0, sibling, me).wait_recv()
            for j, chip in enumerate(chips):
                copy(a, 4 + j, (*chip, 1 - c), me).wait_recv()
        for cp in first + passed:
            cp.wait_send()
        for cp in mine:
            cp.wait()

    spec = pl.BlockSpec(memory_space=pltpu.VMEM) if in_vmem else ANY
    return pl.pallas_call(
        body, name=name, out_shape=[jax.ShapeDtypeStruct((N_DEV,) + s.shape, s.dtype) for s in shards],
        in_specs=[spec] * n_arr, out_specs=[spec] * n_arr,
        scratch_shapes=[pltpu.SemaphoreType.DMA((7 * n_arr,)), pltpu.SemaphoreType.DMA((7 * n_arr,)),
                        pltpu.SemaphoreType.DMA((n_arr,))],
    )(*shards)


def _swap_sibling(sends):
    n_arr = len(sends)

    def body(*refs):
        s_refs, o_refs = refs[:n_arr], refs[n_arr:2 * n_arr]
        send_sems, recv_sems = refs[2 * n_arr:]
        x, y, c = lax.axis_index("x"), lax.axis_index("y"), lax.axis_index("c")
        cps = [pltpu.make_async_remote_copy(
            src_ref=s_refs[a].at[:, 1 - c], dst_ref=o_refs[a], send_sem=send_sems.at[a], recv_sem=recv_sems.at[a],
            device_id=(x, y, 1 - c), device_id_type=MESH) for a in range(n_arr)]
        for cp in cps:
            cp.start()
        for cp in cps:
            cp.wait_recv()
        for cp in cps:
            cp.wait_send()

    return pl.pallas_call(
        body, name="swap_sibling",
        out_shape=[jax.ShapeDtypeStruct((s.shape[0],) + s.shape[2:], s.dtype) for s in sends],
        in_specs=[ANY] * n_arr, out_specs=[ANY] * n_arr,
        scratch_shapes=[pltpu.SemaphoreType.DMA((n_arr,)), pltpu.SemaphoreType.DMA((n_arr,))])(*sends)


def _exchange_chips(bufs):
    n_arr = len(bufs)

    def body(*refs):
        b_refs, o_refs = refs[:n_arr], refs[n_arr:2 * n_arr]
        send_sems, recv_sems, local_sems = refs[2 * n_arr:]
        x, y, c = lax.axis_index("x"), lax.axis_index("y"), lax.axis_index("c")
        mine = 2 * x + y
        peers = [(x, 1 - y), (1 - x, y), (1 - x, 1 - y)]

        def copy(a, k, src_slot, dst_slot):
            px, py = peers[k]
            return pltpu.make_async_remote_copy(
                src_ref=b_refs[a].at[src_slot], dst_ref=o_refs[a].at[dst_slot],
                send_sem=send_sems.at[3 * a + k], recv_sem=recv_sems.at[3 * a + k],
                device_id=(px, py, c), device_id_type=MESH)

        own = [pltpu.make_async_copy(b_refs[a].at[mine], o_refs[a].at[mine], local_sems.at[a]) for a in range(n_arr)]
        sent = [copy(a, k, 2 * px + py, mine) for a in range(n_arr) for k, (px, py) in enumerate(peers)]
        for cp in own + sent:
            cp.start()
        for a in range(n_arr):
            for k, (px, py) in enumerate(peers):
                copy(a, k, mine, 2 * px + py).wait_recv()
        for cp in sent:
            cp.wait_send()
        for cp in own:
            cp.wait()

    return pl.pallas_call(
        body, name="exchange_chips", out_shape=[jax.ShapeDtypeStruct(b.shape, b.dtype) for b in bufs],
        in_specs=[ANY] * n_arr, out_specs=[ANY] * n_arr,
        scratch_shapes=[pltpu.SemaphoreType.DMA((3 * n_arr,)), pltpu.SemaphoreType.DMA((3 * n_arr,)),
                        pltpu.SemaphoreType.DMA((n_arr,))])(*bufs)


def _row_tile(r, c):
    if r * c <= 512 * 1024:
        return r
    t = r
    while t * c > 512 * 1024 and t % 2 == 0 and (t // 2) % 16 == 0:
        t //= 2
    return t


def _add_own_half(send, recv, c):
    nb, _, r, n = send.shape
    tr = _row_tile(r, n)

    def body(c_ref, s_ref, r_ref, o_ref):
        o_ref[...] = (s_ref[...].astype(f32) + r_ref[...].astype(f32)).astype(o_ref.dtype)

    return pl.pallas_call(
        body, name="add_own_half",
        grid_spec=pltpu.PrefetchScalarGridSpec(
            num_scalar_prefetch=1, grid=(nb, r // tr),
            in_specs=[pl.BlockSpec((None, None, tr, n), lambda j, i, cr: (j, cr[0], i, 0)),
                      pl.BlockSpec((None, tr, n), lambda j, i, cr: (j, i, 0))],
            out_specs=pl.BlockSpec((None, tr, n), lambda j, i, cr: (j, i, 0))),
        out_shape=jax.ShapeDtypeStruct((nb, r, n), bf16),
        compiler_params=_cp(("parallel", "parallel")))(jnp.reshape(c, (1,)).astype(jnp.int32), send, recv)


def _sum_slots(buf, *, tr, name):
    nb, r, n = buf.shape

    def body(b_ref, o_ref):
        acc = b_ref[0].astype(f32)
        for k in range(1, nb):
            acc = acc + b_ref[k].astype(f32)
        o_ref[...] = acc

    return pl.pallas_call(
        body, name=name, grid=(r // tr,),
        in_specs=[pl.BlockSpec((nb, tr, n), lambda i: (0, i, 0))],
        out_specs=pl.BlockSpec((tr, n), lambda i: (i, 0)),
        out_shape=jax.ShapeDtypeStruct((r, n), f32), compiler_params=_cp(("parallel",)))(buf)


def _ada_fwd(c_all, ada_w, ada_b_cols):
    n = ada_w.shape[-1]

    def body(c_ref, w_ref, b_ref, o_ref):
        cv = c_ref[...]
        ca = (cv * _sigmoid(cv)).astype(bf16)
        o_ref[...] = _dot(ca, w_ref[...].astype(bf16)) + b_ref[...]

    return pl.pallas_call(
        body, name="ada_fwd", grid=(DEPTH,),
        in_specs=[_const_spec((N_DEV, D)), pl.BlockSpec((None, D, n), lambda l: (l, 0, 0)),
                  pl.BlockSpec((None, 1, n), lambda l: (l, 0, 0))],
        out_specs=pl.BlockSpec((None, N_DEV, n), lambda l: (l, 0, 0)),
        out_shape=jax.ShapeDtypeStruct((DEPTH, N_DEV, n), f32), compiler_params=_cp(("parallel",)))(c_all, ada_w, ada_b_cols)


def _ada_bwd(c_all, dmod_cols):
    n = dmod_cols.shape[-1]

    def body(c_ref, d_ref, o_ref):
        cv = c_ref[...]
        ca = (cv * _sigmoid(cv)).astype(bf16)
        o_ref[...] = _dot_tn(ca, d_ref[...].astype(bf16))

    return pl.pallas_call(
        body, name="ada_bwd", grid=(DEPTH,),
        in_specs=[_const_spec((N_DEV, D)), pl.BlockSpec((None, N_DEV, n), lambda l: (l, 0, 0))],
        out_specs=pl.BlockSpec((None, D, n), lambda l: (l, 0, 0)),
        out_shape=jax.ShapeDtypeStruct((DEPTH, D, n), f32), compiler_params=_cp(("parallel",)))(c_all, dmod_cols)


def _adamw(w, g, m, v, *, tr, name):
    r, n = w.shape

    def body(w_ref, g_ref, m_ref, v_ref, d_ref, mo_ref, vo_ref):
        g_ = g_ref[...]
        m_ = ADAM_B1 * m_ref[...] + (1.0 - ADAM_B1) * g_
        v_ = ADAM_B2 * v_ref[...] + (1.0 - ADAM_B2) * jnp.square(g_)
        m_hat = m_ / (1.0 - ADAM_B1 ** ADAM_STEP)
        v_hat = v_ / (1.0 - ADAM_B2 ** ADAM_STEP)
        d_ref[...] = -ADAM_LR * (m_hat / (jnp.sqrt(v_hat) + ADAM_EPS) + ADAM_WD * w_ref[...])
        mo_ref[...] = m_
        vo_ref[...] = v_

    blk = pl.BlockSpec((tr, n), lambda i: (i, 0))
    return pl.pallas_call(
        body, name=name, grid=(r // tr,), in_specs=[blk] * 4, out_specs=[blk] * 3,
        out_shape=[jax.ShapeDtypeStruct((r, n), f32)] * 3, compiler_params=_cp(("parallel",)))(w, g, m, v)


WEIGHTS = ["ln_in_g", "ln_in_b", "ada_w", "ada_b", "ffn1_w_in", "ffn1_w_out", "mix_w_in", "mix_w_out", "ssd_conv_w",
           "ssd_conv_b", "ssd_dt_bias", "ssd_a_log", "ssd_d", "ssd_norm_g", "fox_f_bias", "sconv_w", "ffn2_w_in",
           "ffn2_w_out", "ln_g", "ln_b"]
BIG = ["ffn1_w_in", "ffn1_w_out", "ffn2_w_in", "ffn2_w_out", "mix_w_in", "mix_w_out"]
COL_SHARDED = ("ffn1_w_in", "ffn2_w_in")
SMALL_SHARDED = {"ssd_conv_w": 128, "sconv_w": 32, "ln_g": 128, "ln_b": 128}
ADAM_TR = {"ada_w": 256, "ffn1_w_in": 512, "ffn2_w_in": 512, "ffn1_w_out": 352, "ffn2_w_out": 352, "mix_w_in": 64,
           "mix_w_out": 256}


def _pad_rows(v, mult=128):
    v = v.reshape(-1)
    return jnp.pad(v, (0, (-v.shape[0]) % mult))


def _pack_rows(parts, row_mult=8):
    flat = [_pad_rows(p.astype(f32)) for p in parts]
    offs, o = [], 0
    for f in flat:
        offs.append(o)
        o += f.shape[0] // 128
    buf = jnp.concatenate(flat).reshape(-1, 128)
    return jnp.pad(buf, ((0, (-buf.shape[0]) % row_mult), (0, 0))), offs


def _take(buf, off, shape):
    n = 1
    for s in shape:
        n *= s
    rows = -(-n // 128)
    lead = buf.shape[:-2]
    flat = buf[..., off:off + rows, :].reshape(lead + (rows * 128,))
    return flat[..., :n].reshape(lead + tuple(shape))


def _reduce_scatter_layer(g_layer, ci):
    sends = [t.reshape((4, 2) + t.shape[-2:]) for t in g_layer]
    got = _swap_sibling(sends)
    chip_sums = [_add_own_half(s, r, ci) for s, r in zip(sends, got)]
    got = _exchange_chips(chip_sums)
    return [_sum_slots(t, tr=_row_tile(*t.shape[-2:]), name="sum_chips") for t in got]


def kernel(*args):
    names = (["x", "c"] + WEIGHTS + ["loss_target"] + ["m_" + n for n in WEIGHTS] + ["v_" + n for n in WEIGHTS])
    assert len(args) == len(names)
    a = dict(zip(names, args))
    xi, yi, ci = lax.axis_index("x"), lax.axis_index("y"), lax.axis_index("c")
    me = 4 * xi + 2 * yi + ci

    small_in = [a["c"], a["ln_g"], a["ln_b"], a["ssd_conv_w"], a["sconv_w"]]
    buf, offs = _pack_rows(small_in)
    got, = _all_gather([buf], in_vmem=True, name="gather_small")
    c_all = _take(got, offs[0], (D,))
    full = {}
    for k, n in enumerate(["ln_g", "ln_b", "ssd_conv_w", "sconv_w"]):
        sh = a[n].shape
        t = _take(got, offs[k + 1], sh)
        full[n] = jnp.transpose(t, (1, 2, 0, 3)).reshape(sh[0], sh[1], N_DEV * sh[2])

    ncol = a["ada_w"].shape[-1]
    ada_b_cols = lax.dynamic_slice_in_dim(a["ada_b"], me * ncol, ncol, axis=1)[:, None, :]
    mod_cols = _ada_fwd(c_all, a["ada_w"], ada_b_cols)
    got, = _all_gather([mod_cols.reshape(DEPTH * N_DEV, ncol)], in_vmem=True, name="gather_mod")
    got = got.reshape(N_DEV, DEPTH, N_DEV, ncol)
    mod = lax.dynamic_index_in_dim(got, me, axis=2, keepdims=False)
    mod = jnp.transpose(mod, (1, 0, 2)).reshape(DEPTH, 9, D)

    shards = {n: a[n] for n in BIG}
    shards["mix_w_in"] = _pack_cols(a["mix_w_in"])
    p = {n: [] for n in BIG}
    for l in range(DEPTH):
        got = _all_gather([shards[n][l].astype(bf16) for n in BIG], in_vmem=False, name="gather_weights")
        for n, t in zip(BIG, got):
            p[n].append(t if n in COL_SHARDED else t.reshape(-1, t.shape[-1]))
    for n in ("ln_in_g", "ln_in_b", "ssd_conv_b", "ssd_dt_bias", "ssd_a_log", "ssd_d", "ssd_norm_g", "fox_f_bias"):
        p[n] = a[n]
    p.update(full)
    p["mod"] = mod

    loss_local, grad_x, g = _local_step(a["x"][0], a["loss_target"][0], p)
    loss = lax.psum(loss_local, ("x", "y", "c"))

    small_names = ["mod", "ln_in_g", "ln_in_b", "ssd_conv_b", "ssd_dt_bias", "ssd_a_log", "ssd_d", "ssd_norm_g",
                   "fox_f_bias", "ln_g", "ln_b", "ssd_conv_w", "sconv_w"]
    buf, offs = _pack_rows([g[n] for n in small_names])
    got, = _all_gather([buf], in_vmem=True, name="gather_small_grads")
    tot = _sum_slots(got, tr=buf.shape[0], name="sum_small_grads")
    grads = {}
    for k, n in enumerate(small_names[1:], start=1):
        t = _take(tot, offs[k], g[n].shape)
        if n in SMALL_SHARDED:
            w_ = SMALL_SHARDED[n]
            t = lax.dynamic_slice_in_dim(t, me * w_, w_, axis=2)
        grads[n] = t
    grads["ada_b"] = _take(tot, offs[0], (DEPTH, 9 * D))
    dmod_all = _take(got, offs[0], (DEPTH, 9 * D))
    dmod_cols = jnp.transpose(lax.dynamic_slice_in_dim(dmod_all, me * ncol, ncol, axis=2), (1, 0, 2))
    grads["ada_w"] = _ada_bwd(c_all, dmod_cols)

    per_layer = []
    for l in reversed(range(DEPTH)):
        g_layer = [g[n][l].reshape((N_DEV,) + shards[n].shape[1:]) for n in BIG]
        per_layer.append(_reduce_scatter_layer(g_layer, ci))
    for k, n in enumerate(BIG):
        grads[n] = jnp.stack([per_layer[1][k], per_layer[0][k]])
    grads["mix_w_in"] = _unpack_cols(grads["mix_w_in"])

    delta, new_m, new_v = {}, {}, {}
    small_params = [n for n in WEIGHTS if n not in ADAM_TR]
    packs = [_pack_rows([src[pre + n] for n in small_params])[0]
             for src, pre in ((a, ""), (grads, ""), (a, "m_"), (a, "v_"))]
    _, offs = _pack_rows([a[n] for n in small_params])
    outs = _adamw(*packs, tr=packs[0].shape[0], name="adamw_small")
    for k, n in enumerate(small_params):
        delta[n], new_m[n], new_v[n] = (_take(t, offs[k], a[n].shape) for t in outs)
    for n, tr in ADAM_TR.items():
        sh = a[n].shape
        two = lambda t: t.reshape(-1, sh[-1])
        outs = _adamw(two(a[n]), two(grads[n]), two(a["m_" + n]), two(a["v_" + n]), tr=tr, name="adamw_" + n)
        delta[n], new_m[n], new_v[n] = (t.reshape(sh) for t in outs)

    return (loss, grad_x[None], *[grads[n] for n in WEIGHTS], *[delta[n] for n in WEIGHTS],
            *[new_m[n] for n in WEIGHTS], *[new_v[n] for n in WEIGHTS])
```

```python
import functools

import jax
import jax.numpy as jnp
from jax import lax
from jax.experimental import pallas as pl
from jax.experimental.pallas import tpu as pltpu

f32, bf16 = jnp.float32, jnp.bfloat16

D = 1024
F = 2816
DEPTH = 2
N_DEV = 8
SSD_W, SSD_HD, SSD_H, SSD_G, SSD_N, SSD_K = 512, 64, 8, 2, 128, 4
FOX_W, FOX_HD, FOX_H = 256, 64, 4
SC_W, SC_K = 256, 3
ALPHA = (2 * DEPTH) ** 0.25
LN_EPS = 1e-5
RMS_EPS = 1e-5
P_XBC, P_Z, P_Q, P_K, P_V, P_SB, P_SC, P_SX, P_SM = 0, 1024, 1536, 1792, 2048, 2304, 2560, 2816, 3072
P_W = 3200
SM_DT, SM_F = 0, 8
ADAM_LR, ADAM_B1, ADAM_B2, ADAM_EPS, ADAM_WD, ADAM_STEP = 0.001, 0.9, 0.999, 1e-08, 0.01, 10

VMEM_LIMIT = 56 * 1024 * 1024


def _cp(sem=None):
    return pltpu.CompilerParams(dimension_semantics=sem, vmem_limit_bytes=VMEM_LIMIT)


def _const_spec(shape):
    nd = len(shape)
    return pl.BlockSpec(shape, lambda *_: (0,) * nd, pipeline_mode=pl.Buffered(1))


def _sigmoid(x):
    return 1.0 / (1.0 + jnp.exp(-x))


def _ln_fwd(u, g, b):
    mu = jnp.mean(u, -1, keepdims=True)
    xc = u - mu
    rstd = lax.rsqrt(jnp.mean(xc * xc, -1, keepdims=True) + LN_EPS)
    xhat = xc * rstd
    return xhat * g + b, xhat, rstd


def _ln_bwd(dout, xhat, rstd, g):
    dxh = dout * g
    m1 = jnp.mean(dxh, -1, keepdims=True)
    m2 = jnp.mean(dxh * xhat, -1, keepdims=True)
    du = rstd * (dxh - m1 - xhat * m2)
    return du, jnp.sum(dout * xhat, 0, keepdims=True), jnp.sum(dout, 0, keepdims=True)


def _dot(a, b):
    return jnp.dot(a, b, preferred_element_type=f32)


def _dot_nt(a, b):
    return lax.dot_general(a, b, (((1,), (1,)), ((), ())), preferred_element_type=f32)


def _dot_tn(a, b):
    return lax.dot_general(a, b, (((0,), (0,)), ((), ())), preferred_element_type=f32)


def _dot_hi(a, b):
    return jnp.dot(a, b, preferred_element_type=f32, precision=lax.Precision.HIGHEST)


def _shift_down(cur, prev, s):
    if s == 0:
        return cur
    row = lax.broadcasted_iota(jnp.int32, cur.shape, 0)
    return jnp.where(row < s, pltpu.roll(prev, s, 0), pltpu.roll(cur, s, 0))


def _shift_up(cur, nxt, s):
    if s == 0:
        return cur
    t = cur.shape[0]
    row = lax.broadcasted_iota(jnp.int32, cur.shape, 0)
    return jnp.where(row >= t - s, pltpu.roll(nxt, t - s, 0), pltpu.roll(cur, t - s, 0))


def _ln_in_fwd(x, gb, *, tt=512):
    L = x.shape[0]

    def body(x_ref, gb_ref, o_ref):
        o_ref[...] = _ln_fwd(x_ref[...], gb_ref[0:1, :], gb_ref[1:2, :])[0]

    return pl.pallas_call(
        body, name="ln_in_fwd", grid=(L // tt,),
        in_specs=[pl.BlockSpec((tt, D), lambda i: (i, 0)), _const_spec((8, D))],
        out_specs=pl.BlockSpec((tt, D), lambda i: (i, 0)),
        out_shape=jax.ShapeDtypeStruct((L, D), f32), compiler_params=_cp(("parallel",)))(x, gb)


def _ln_in_bwd(x, dy, gb, *, tt=512):
    L = x.shape[0]

    def body(x_ref, dy_ref, gb_ref, dx_ref, acc_ref):
        @pl.when(pl.program_id(0) == 0)
        def _():
            acc_ref[...] = jnp.zeros_like(acc_ref)
        _, xhat, rstd = _ln_fwd(x_ref[...], gb_ref[0:1, :], gb_ref[1:2, :])
        du, dg, db = _ln_bwd(dy_ref[...], xhat, rstd, gb_ref[0:1, :])
        dx_ref[...] = du
        acc_ref[0:1, :] += dg
        acc_ref[1:2, :] += db

    return pl.pallas_call(
        body, name="ln_in_bwd", grid=(L // tt,),
        in_specs=[pl.BlockSpec((tt, D), lambda i: (i, 0)), pl.BlockSpec((tt, D), lambda i: (i, 0)), _const_spec((8, D))],
        out_specs=[pl.BlockSpec((tt, D), lambda i: (i, 0)), pl.BlockSpec((8, D), lambda i: (0, 0))],
        out_shape=[jax.ShapeDtypeStruct((L, D), f32), jax.ShapeDtypeStruct((8, D), f32)],
        compiler_params=_cp(("arbitrary",)))(x, dy, gb)


def _loss_head(y, tgt, *, tt=512):
    L = y.shape[0]

    def body(y_ref, t_ref, dy_ref, acc_ref):
        @pl.when(pl.program_id(0) == 0)
        def _():
            acc_ref[...] = jnp.zeros_like(acc_ref)
        e = y_ref[...] - t_ref[...]
        dy_ref[...] = e * (1.0 / D)
        acc_ref[...] += 0.5 * jnp.sum(jnp.mean(e * e, -1, keepdims=True))

    return pl.pallas_call(
        body, name="loss_head", grid=(L // tt,),
        in_specs=[pl.BlockSpec((tt, D), lambda i: (i, 0)), pl.BlockSpec((tt, D), lambda i: (i, 0))],
        out_specs=[pl.BlockSpec((tt, D), lambda i: (i, 0)), pl.BlockSpec((8, 128), lambda i: (0, 0))],
        out_shape=[jax.ShapeDtypeStruct((L, D), f32), jax.ShapeDtypeStruct((8, 128), f32)],
        compiler_params=_cp(("arbitrary",)))(y, tgt)


FFN_CH = 4
FS = F // FFN_CH


def _ffn_fwd(x, mv, w_in, w_out, *, tt=256):
    L = x.shape[0]

    def body(x_ref, mv_ref, wi_ref, wo_ref, o_ref):
        x = x_ref[...]
        h = (x * (1.0 + mv_ref[1:2, :]) + mv_ref[0:1, :]).astype(bf16)
        y = jnp.zeros((tt, D), f32)
        for c in range(FFN_CH):
            g = _dot(h, wi_ref[c])
            u = _dot(h, wi_ref[c + FFN_CH])
            act = (g * _sigmoid(g) * u).astype(bf16)
            y = y + _dot(act, wo_ref[c * FS:(c + 1) * FS, :])
        uu = ALPHA * x + (0.5 * mv_ref[2:3, :]) * y
        o_ref[...] = _ln_fwd(uu, mv_ref[3:4, :], mv_ref[4:5, :])[0]

    return pl.pallas_call(
        body, name="ffn_fwd", grid=(L // tt,),
        in_specs=[pl.BlockSpec((tt, D), lambda i: (i, 0)), _const_spec((8, D)),
                  _const_spec((2 * FFN_CH, D, FS)), _const_spec((F, D))],
        out_specs=pl.BlockSpec((tt, D), lambda i: (i, 0)),
        out_shape=jax.ShapeDtypeStruct((L, D), f32), compiler_params=_cp(("parallel",)))(x, mv, w_in, w_out)


def _ffn_bwd(x, dxo, mv, w_in, w_out, *, tt=256):
    L = x.shape[0]

    def body(x_ref, dxo_ref, mv_ref, wi_ref, wo_ref, dx_ref, h_ref, da_ref, act_ref, dy_ref, acc_ref, a_scr):
        @pl.when(pl.program_id(0) == 0)
        def _():
            acc_ref[...] = jnp.zeros_like(acc_ref)
        x = x_ref[...]
        scale1 = 1.0 + mv_ref[1:2, :]
        h = (x * scale1 + mv_ref[0:1, :]).astype(bf16)
        h_ref[...] = h
        y = jnp.zeros((tt, D), f32)
        for c in range(FFN_CH):
            g = _dot(h, wi_ref[c])
            u = _dot(h, wi_ref[c + FFN_CH])
            a_scr[c] = g
            a_scr[c + FFN_CH] = u
            act = (g * _sigmoid(g) * u).astype(bf16)
            act_ref[c] = act
            y = y + _dot(act, wo_ref[c * FS:(c + 1) * FS, :])
        hg = 0.5 * mv_ref[2:3, :]
        _, xhat, rstd = _ln_fwd(ALPHA * x + hg * y, mv_ref[3:4, :], mv_ref[4:5, :])
        du, dlg, dlb = _ln_bwd(dxo_ref[...], xhat, rstd, mv_ref[3:4, :])
        acc_ref[3:4, :] += dlg
        acc_ref[4:5, :] += dlb
        acc_ref[2:3, :] += jnp.sum(0.5 * y * du, 0, keepdims=True)
        dyb = (hg * du).astype(bf16)
        dy_ref[...] = dyb
        dh = jnp.zeros((tt, D), f32)
        for c in range(FFN_CH):
            g = a_scr[c]
            u = a_scr[c + FFN_CH]
            dact = _dot_nt(dyb, wo_ref[c * FS:(c + 1) * FS, :])
            s = _sigmoid(g)
            dg = (dact * u * (s * (1.0 + g * (1.0 - s)))).astype(bf16)
            dup = (dact * (g * s)).astype(bf16)
            da_ref[c] = dg
            da_ref[c + FFN_CH] = dup
            dh = dh + _dot_nt(dg, wi_ref[c])
            dh = dh + _dot_nt(dup, wi_ref[c + FFN_CH])
        dx_ref[...] = ALPHA * du + dh * scale1
        acc_ref[0:1, :] += jnp.sum(dh, 0, keepdims=True)
        acc_ref[1:2, :] += jnp.sum(dh * x, 0, keepdims=True)

    tok = lambda w: pl.BlockSpec((tt, w), lambda i: (i, 0))
    by_chunk = lambda n: pl.BlockSpec((n, tt, FS), lambda i: (0, i, 0))
    return pl.pallas_call(
        body, name="ffn_bwd", grid=(L // tt,),
        in_specs=[tok(D), tok(D), _const_spec((8, D)), _const_spec((2 * FFN_CH, D, FS)), _const_spec((F, D))],
        out_specs=[tok(D), tok(D), by_chunk(2 * FFN_CH), by_chunk(FFN_CH), tok(D), pl.BlockSpec((8, D), lambda i: (0, 0))],
        out_shape=[jax.ShapeDtypeStruct((L, D), f32), jax.ShapeDtypeStruct((L, D), bf16),
                   jax.ShapeDtypeStruct((2 * FFN_CH, L, FS), bf16), jax.ShapeDtypeStruct((FFN_CH, L, FS), bf16),
                   jax.ShapeDtypeStruct((L, D), bf16), jax.ShapeDtypeStruct((8, D), f32)],
        scratch_shapes=[pltpu.VMEM((2 * FFN_CH, tt, FS), f32)],
        compiler_params=_cp(("arbitrary",)))(x, dxo, mv, w_in, w_out)


def _matmul_tn(a, b, *, tn, tk, name):
    ga, gb = a.ndim == 3, b.ndim == 3
    G = a.shape[0] if ga else (b.shape[0] if gb else 1)
    K, M = a.shape[-2:]
    N = b.shape[-1]
    nk = K // tk

    def body(a_ref, b_ref, o_ref, acc):
        k = pl.program_id(2)
        p = _dot_tn(a_ref[...], b_ref[...])

        @pl.when(k == 0)
        def _():
            acc[...] = p

        @pl.when(k > 0)
        def _():
            acc[...] += p

        @pl.when(k == nk - 1)
        def _():
            o_ref[...] = acc[...].astype(bf16)

    a_spec = (pl.BlockSpec((None, tk, M), lambda g, j, k: (g, k, 0)) if ga
              else pl.BlockSpec((tk, M), lambda g, j, k: (k, 0)))
    b_spec = (pl.BlockSpec((None, tk, tn), lambda g, j, k: (g, k, j)) if gb
              else pl.BlockSpec((tk, tn), lambda g, j, k: (k, j)))
    if ga or gb:
        o_spec, o_shape = pl.BlockSpec((None, M, tn), lambda g, j, k: (g, 0, j)), (G, M, N)
    else:
        o_spec, o_shape = pl.BlockSpec((M, tn), lambda g, j, k: (0, j)), (M, N)
    return pl.pallas_call(
        body, name=name, grid=(G, N // tn, nk), in_specs=[a_spec, b_spec], out_specs=o_spec,
        out_shape=jax.ShapeDtypeStruct(o_shape, bf16), scratch_shapes=[pltpu.VMEM((M, tn), f32)],
        compiler_params=_cp(("parallel", "parallel", "arbitrary")))(a, b)


def _inproj_fwd(x, mv, w, *, tt=512):
    L = x.shape[0]

    def body(x_ref, mv_ref, w_ref, o_ref):
        h = (x_ref[...] * (1.0 + mv_ref[1:2, :]) + mv_ref[0:1, :]).astype(bf16)
        o_ref[...] = _dot(h, w_ref[...])

    return pl.pallas_call(
        body, name="inproj_fwd", grid=(L // tt,),
        in_specs=[pl.BlockSpec((tt, D), lambda i: (i, 0)), _const_spec((8, D)), _const_spec((D, P_W))],
        out_specs=pl.BlockSpec((tt, P_W), lambda i: (i, 0)),
        out_shape=jax.ShapeDtypeStruct((L, P_W), f32), compiler_params=_cp(("parallel",)))(x, mv, w)


def _inproj_bwd(x, dx_part, dproj, mv, w, *, tt=512):
    L = x.shape[0]

    def body(x_ref, dxp_ref, dp_ref, mv_ref, w_ref, dx_ref, h_ref, acc_ref):
        @pl.when(pl.program_id(0) == 0)
        def _():
            acc_ref[...] = jnp.zeros_like(acc_ref)
        x = x_ref[...]
        scale1 = 1.0 + mv_ref[1:2, :]
        h_ref[...] = (x * scale1 + mv_ref[0:1, :]).astype(bf16)
        dh = _dot_nt(dp_ref[...], w_ref[...])
        dx_ref[...] = dxp_ref[...] + dh * scale1
        acc_ref[0:1, :] += jnp.sum(dh, 0, keepdims=True)
        acc_ref[1:2, :] += jnp.sum(dh * x, 0, keepdims=True)

    tok = lambda w_: pl.BlockSpec((tt, w_), lambda i: (i, 0))
    return pl.pallas_call(
        body, name="inproj_bwd", grid=(L // tt,),
        in_specs=[tok(D), tok(D), tok(P_W), _const_spec((8, D)), _const_spec((D, P_W))],
        out_specs=[tok(D), tok(D), pl.BlockSpec((8, D), lambda i: (0, 0))],
        out_shape=[jax.ShapeDtypeStruct((L, D), f32), jax.ShapeDtypeStruct((L, D), bf16),
                   jax.ShapeDtypeStruct((8, D), f32)],
        compiler_params=_cp(("arbitrary",)))(x, dx_part, dproj, mv, w)


def _outproj_fwd(x, ycat, mv, w, *, tt=512):
    L = x.shape[0]

    def body(x_ref, y_ref, mv_ref, w_ref, o_ref):
        y = _dot(y_ref[...], w_ref[...])
        uu = ALPHA * x_ref[...] + mv_ref[2:3, :] * y
        o_ref[...] = _ln_fwd(uu, mv_ref[3:4, :], mv_ref[4:5, :])[0]

    tok = lambda w_: pl.BlockSpec((tt, w_), lambda i: (i, 0))
    return pl.pallas_call(
        body, name="outproj_fwd", grid=(L // tt,),
        in_specs=[tok(D), tok(D), _const_spec((8, D)), _const_spec((D, D))],
        out_specs=tok(D),
        out_shape=jax.ShapeDtypeStruct((L, D), f32), compiler_params=_cp(("parallel",)))(x, ycat, mv, w)


def _outproj_bwd(x, ycat, dxo, mv, w, *, tt=512):
    L = x.shape[0]

    def body(x_ref, y_ref, dxo_ref, mv_ref, w_ref, dx_ref, dy_ref, dyc_ref, acc_ref):
        @pl.when(pl.program_id(0) == 0)
        def _():
            acc_ref[...] = jnp.zeros_like(acc_ref)
        y = _dot(y_ref[...], w_ref[...])
        gate = mv_ref[2:3, :]
        _, xhat, rstd = _ln_fwd(ALPHA * x_ref[...] + gate * y, mv_ref[3:4, :], mv_ref[4:5, :])
        du, dlg, dlb = _ln_bwd(dxo_ref[...], xhat, rstd, mv_ref[3:4, :])
        acc_ref[3:4, :] += dlg
        acc_ref[4:5, :] += dlb
        acc_ref[2:3, :] += jnp.sum(y * du, 0, keepdims=True)
        dx_ref[...] = ALPHA * du
        dyb = (gate * du).astype(bf16)
        dy_ref[...] = dyb
        dyc_ref[...] = _dot_nt(dyb, w_ref[...])

    tok = lambda w_: pl.BlockSpec((tt, w_), lambda i: (i, 0))
    return pl.pallas_call(
        body, name="outproj_bwd", grid=(L // tt,),
        in_specs=[tok(D), tok(D), tok(D), _const_spec((8, D)), _const_spec((D, D))],
        out_specs=[tok(D), tok(D), tok(D), pl.BlockSpec((8, D), lambda i: (0, 0))],
        out_shape=[jax.ShapeDtypeStruct((L, D), f32), jax.ShapeDtypeStruct((L, D), bf16),
                   jax.ShapeDtypeStruct((L, D), f32), jax.ShapeDtypeStruct((8, D), f32)],
        compiler_params=_cp(("arbitrary",)))(x, ycat, dxo, mv, w)


def _sconv_fwd(proj, w, *, tt=512):
    L = proj.shape[0]
    cb = SC_W

    def body(b_ref, c_ref, x_ref, cp_ref, xp_ref, w_ref, o_ref):
        first = jnp.where(pl.program_id(0) > 0, 1.0, 0.0)
        u = c_ref[...] * x_ref[...]
        up = cp_ref[...] * xp_ref[...] * first
        v = w_ref[2:3, :] * u + w_ref[1:2, :] * _shift_down(u, up, 1) + w_ref[0:1, :] * _shift_down(u, up, 2)
        o_ref[...] = (b_ref[...] * v).astype(bf16)

    cur = lambda col: pl.BlockSpec((tt, cb), lambda i: (i, col // cb))
    prev = lambda col: pl.BlockSpec((tt, cb), lambda i: (jnp.maximum(i - 1, 0), col // cb))
    return pl.pallas_call(
        body, name="sconv_fwd", grid=(L // tt,),
        in_specs=[cur(P_SB), cur(P_SC), cur(P_SX), prev(P_SC), prev(P_SX), _const_spec((8, cb))],
        out_specs=pl.BlockSpec((tt, cb), lambda i: (i, 0)),
        out_shape=jax.ShapeDtypeStruct((L, cb), bf16), compiler_params=_cp(("parallel",)))(proj, proj, proj, proj, proj, w)


def _sconv_bwd(proj, dycat, w, *, tt=512):
    L = proj.shape[0]
    cb = SC_W
    n = L // tt

    def body(b_ref, c_ref, x_ref, cp_ref, xp_ref, bn_ref, dy_ref, dyn_ref, w_ref, db_ref, dc_ref, dx_ref, acc_ref):
        i = pl.program_id(0)

        @pl.when(i == 0)
        def _():
            acc_ref[...] = jnp.zeros_like(acc_ref)
        first = jnp.where(i > 0, 1.0, 0.0)
        last = jnp.where(i < n - 1, 1.0, 0.0)
        cg, xin, bg = c_ref[...], x_ref[...], b_ref[...]
        u = cg * xin
        up = cp_ref[...] * xp_ref[...] * first
        u1, u2 = _shift_down(u, up, 1), _shift_down(u, up, 2)
        v = w_ref[2:3, :] * u + w_ref[1:2, :] * u1 + w_ref[0:1, :] * u2
        dy = dy_ref[...]
        db_ref[...] = (dy * v).astype(bf16)
        dv = dy * bg
        dvn = dyn_ref[...] * bn_ref[...] * last
        du = w_ref[2:3, :] * dv + w_ref[1:2, :] * _shift_up(dv, dvn, 1) + w_ref[0:1, :] * _shift_up(dv, dvn, 2)
        acc_ref[2:3, :] += jnp.sum(dv * u, 0, keepdims=True)
        acc_ref[1:2, :] += jnp.sum(dv * u1, 0, keepdims=True)
        acc_ref[0:1, :] += jnp.sum(dv * u2, 0, keepdims=True)
        dc_ref[...] = (du * xin).astype(bf16)
        dx_ref[...] = (du * cg).astype(bf16)

    cur = lambda col: pl.BlockSpec((tt, cb), lambda i: (i, col // cb))
    prev = lambda col: pl.BlockSpec((tt, cb), lambda i: (jnp.maximum(i - 1, 0), col // cb))
    nxt = lambda col: pl.BlockSpec((tt, cb), lambda i: (jnp.minimum(i + 1, n - 1), col // cb))
    ycol = SSD_W + FOX_W
    out = pl.BlockSpec((tt, cb), lambda i: (i, 0))
    return pl.pallas_call(
        body, name="sconv_bwd", grid=(n,),
        in_specs=[cur(P_SB), cur(P_SC), cur(P_SX), prev(P_SC), prev(P_SX), nxt(P_SB), cur(ycol), nxt(ycol),
                  _const_spec((8, cb))],
        out_specs=[out, out, out, pl.BlockSpec((8, cb), lambda i: (0, 0))],
        out_shape=[jax.ShapeDtypeStruct((L, cb), bf16)] * 3 + [jax.ShapeDtypeStruct((8, cb), f32)],
        compiler_params=_cp(("arbitrary",)))(proj, proj, proj, proj, proj, proj, dycat, dycat, w)


def _log1pexp(x):
    return jnp.log(1.0 + jnp.exp(-jnp.abs(x)))


def _fox_gate_fwd(proj, hp, *, tt=256):
    L = proj.shape[0]

    def body(sm_ref, hp_ref, cum_ref, cumt_ref, carry):
        @pl.when(pl.program_id(0) == 0)
        def _():
            carry[...] = jnp.zeros_like(carry)
        xx = sm_ref[...] + hp_ref[3:4, :]
        logf = jnp.minimum(xx, 0.0) - _log1pexp(xx)
        r = lax.broadcasted_iota(jnp.int32, (tt, tt), 0)
        c = lax.broadcasted_iota(jnp.int32, (tt, tt), 1)
        cum = _dot_hi(jnp.where(r >= c, 1.0, 0.0), logf) + carry[0:1, :]
        cum_ref[...] = cum
        cumt_ref[...] = cum.T
        carry[0:1, :] = cum[tt - 1:tt, :]

    return pl.pallas_call(
        body, name="fox_gate_fwd", grid=(L // tt,),
        in_specs=[pl.BlockSpec((tt, 128), lambda i: (i, P_SM // 128)), _const_spec((8, 128))],
        out_specs=[pl.BlockSpec((tt, 128), lambda i: (i, 0)), pl.BlockSpec((128, tt), lambda i: (0, i))],
        out_shape=[jax.ShapeDtypeStruct((L, 128), f32), jax.ShapeDtypeStruct((128, L), f32)],
        scratch_shapes=[pltpu.VMEM((8, 128), f32)],
        compiler_params=_cp(("arbitrary",)))(proj, hp)


def _fox_gate_bwd(dcumt, proj, ddt, hp, *, tt=256):
    L = proj.shape[0]
    n = L // tt

    def body(dct_ref, sm_ref, ddt_ref, hp_ref, dsm_ref, acc_ref, carry):
        @pl.when(pl.program_id(0) == 0)
        def _():
            carry[...] = jnp.zeros_like(carry)
            acc_ref[...] = jnp.zeros_like(acc_ref)
        dc = dct_ref[...].T
        r = lax.broadcasted_iota(jnp.int32, (tt, tt), 0)
        c = lax.broadcasted_iota(jnp.int32, (tt, tt), 1)
        dl = _dot_hi(jnp.where(r <= c, 1.0, 0.0), dc) + carry[0:1, :]
        carry[0:1, :] += jnp.sum(dc, 0, keepdims=True)
        xx = sm_ref[...] + hp_ref[3:4, :]
        lane = lax.broadcasted_iota(jnp.int32, (tt, 128), 1)
        dlogit = jnp.where((lane >= SM_F) & (lane < SM_F + FOX_H), dl * _sigmoid(-xx), 0.0)
        acc_ref[3:4, :] += jnp.sum(dlogit, 0, keepdims=True)
        dsm_ref[...] = (dlogit + ddt_ref[...]).astype(bf16)

    return pl.pallas_call(
        body, name="fox_gate_bwd", grid=(n,),
        in_specs=[pl.BlockSpec((128, tt), lambda i: (0, n - 1 - i)),
                  pl.BlockSpec((tt, 128), lambda i: (n - 1 - i, P_SM // 128)),
                  pl.BlockSpec((tt, 128), lambda i: (n - 1 - i, 0)), _const_spec((8, 128))],
        out_specs=[pl.BlockSpec((tt, 128), lambda i: (n - 1 - i, 0)), pl.BlockSpec((8, 128), lambda i: (0, 0))],
        out_shape=[jax.ShapeDtypeStruct((L, 128), bf16), jax.ShapeDtypeStruct((8, 128), f32)],
        scratch_shapes=[pltpu.VMEM((8, 128), f32)],
        compiler_params=_cp(("arbitrary",)))(dcumt, proj, ddt, hp)


NEG = -1e30
FOX_SCALE = FOX_HD ** -0.5


def _fox_fwd(proj, cum, cumt, *, tq=256):
    L = proj.shape[0]
    heads = [slice(h * FOX_HD, (h + 1) * FOX_HD) for h in range(FOX_H)]

    def body(q_ref, k_ref, v_ref, cq_ref, ct_ref, o_ref, oa_ref, lse_ref):
        i = pl.program_id(0)
        row = lax.broadcasted_iota(jnp.int32, (tq, tq), 0)
        col = lax.broadcasted_iota(jnp.int32, (tq, tq), 1)
        diag_bias = jnp.where(row >= col, 0.0, NEG)
        qs = [(q_ref[:, hs] * FOX_SCALE).astype(bf16) for hs in heads]
        cqs = [cq_ref[:, SM_F + h:SM_F + h + 1] for h in range(FOX_H)]

        def block(r0, carry, bias):
            out = []
            for h, hs in enumerate(heads):
                m, l, acc, acc_lo = carry[h]
                kj = k_ref[pl.ds(r0, tq), hs].astype(bf16)
                vj = v_ref[pl.ds(r0, tq), hs].astype(bf16)
                s = _dot_nt(qs[h], kj) + (cqs[h] - ct_ref[SM_F + h:SM_F + h + 1, pl.ds(r0, tq)])
                if bias is not None:
                    s = s + bias
                m_new = jnp.maximum(m, jnp.max(s, -1, keepdims=True))
                p = jnp.exp(s - m_new)
                al = jnp.exp(m - m_new)
                pb = p.astype(bf16)
                p_lo = (p - pb.astype(f32)).astype(bf16)
                out.append((m_new, al * l + jnp.sum(p, -1, keepdims=True),
                            al * acc + _dot(pb, vj), al * acc_lo + _dot(p_lo, vj)))
            return tuple(out)

        init = tuple((jnp.full((tq, 1), NEG, f32), jnp.zeros((tq, 1), f32), jnp.zeros((tq, FOX_HD), f32),
                      jnp.zeros((tq, FOX_HD), f32)) for _ in heads)
        carry = lax.fori_loop(0, i, lambda j, c: block(pl.multiple_of(j * tq, tq), c, None), init)
        carry = block(pl.multiple_of(i * tq, tq), carry, diag_bias)
        lane = lax.broadcasted_iota(jnp.int32, (tq, 128), 1)
        lse_all = jnp.zeros((tq, 128), f32)
        for h, hs in enumerate(heads):
            m, l, acc, acc_lo = carry[h]
            inv = 1.0 / l
            o_ref[:, hs] = acc * inv
            oa_ref[:, hs] = (acc + acc_lo) * inv
            lse_all = jnp.where(lane == h, m + jnp.log(l), lse_all)
        lse_ref[...] = lse_all

    return pl.pallas_call(
        body, name="fox_fwd", grid=(L // tq,),
        in_specs=[pl.BlockSpec((tq, FOX_W), lambda i: (i, P_Q // FOX_W)),
                  pl.BlockSpec((L, FOX_W), lambda i: (0, P_K // FOX_W), pipeline_mode=pl.Buffered(1)),
                  pl.BlockSpec((L, FOX_W), lambda i: (0, P_V // FOX_W), pipeline_mode=pl.Buffered(1)),
                  pl.BlockSpec((tq, 128), lambda i: (i, 0)), _const_spec((128, L))],
        out_specs=[pl.BlockSpec((tq, FOX_W), lambda i: (i, 0)), pl.BlockSpec((tq, FOX_W), lambda i: (i, 0)),
                   pl.BlockSpec((tq, 128), lambda i: (i, 0))],
        out_shape=[jax.ShapeDtypeStruct((L, FOX_W), f32), jax.ShapeDtypeStruct((L, FOX_W), f32),
                   jax.ShapeDtypeStruct((L, 128), f32)],
        compiler_params=_cp(("parallel",)))(proj, proj, proj, cum, cumt)


def _fox_bwd(proj, dycat, o_acc, lse, cum, cumt, *, tq=256):
    L = proj.shape[0]
    nq = L // tq
    heads = [slice(h * FOX_HD, (h + 1) * FOX_HD) for h in range(FOX_H)]

    def body(q_ref, k_ref, v_ref, do_ref, oa_ref, lse_ref, cum_ref, ct_ref, dq_ref, dk_ref, dv_ref, dct_ref):
        j = pl.program_id(0)

        @pl.when(j == 0)
        def _():
            dq_ref[...] = jnp.zeros_like(dq_ref)
        row = lax.broadcasted_iota(jnp.int32, (tq, tq), 0)
        col = lax.broadcasted_iota(jnp.int32, (tq, tq), 1)
        diag_bias = jnp.where(row >= col, 0.0, NEG)
        kjs = [k_ref[:, hs].astype(bf16) for hs in heads]
        vjs = [v_ref[:, hs].astype(bf16) for hs in heads]
        cks = [ct_ref[SM_F + h:SM_F + h + 1, :] for h in range(FOX_H)]

        def block(r0, carry, bias):
            out = []
            for h, hs in enumerate(heads):
                dk, dv, dc = carry[h]
                qi = (q_ref[pl.ds(r0, tq), hs] * FOX_SCALE).astype(bf16)
                dob = do_ref[pl.ds(r0, tq), hs].astype(bf16)
                delta = jnp.sum(dob.astype(f32) * oa_ref[pl.ds(r0, tq), hs], -1, keepdims=True)
                s = _dot_nt(qi, kjs[h]) + (cum_ref[pl.ds(r0, tq), SM_F + h:SM_F + h + 1] - cks[h])
                if bias is not None:
                    s = s + bias
                p = jnp.exp(s - lse_ref[pl.ds(r0, tq), h:h + 1])
                ds = p * (_dot_nt(dob, vjs[h]) - delta)
                dsb = ds.astype(bf16)
                dq_ref[pl.ds(r0, tq), hs] += _dot(dsb, kjs[h]) * FOX_SCALE
                out.append((dk + _dot_tn(dsb, qi), dv + _dot_tn(p.astype(bf16), dob),
                            dc - jnp.sum(ds, 0, keepdims=True)))
            return tuple(out)

        init = tuple((jnp.zeros((tq, FOX_HD), f32), jnp.zeros((tq, FOX_HD), f32), jnp.zeros((1, tq), f32))
                     for _ in heads)
        carry = block(pl.multiple_of(j * tq, tq), init, diag_bias)
        carry = lax.fori_loop(j + 1, nq, lambda i, c: block(pl.multiple_of(i * tq, tq), c, None), carry)
        sub = lax.broadcasted_iota(jnp.int32, (128, tq), 0)
        dct = jnp.zeros((128, tq), f32)
        for h, hs in enumerate(heads):
            dk, dv, dc = carry[h]
            dk_ref[:, hs] = dk
            dv_ref[:, hs] = dv
            dct = jnp.where(sub == SM_F + h, dc, dct)
        dct_ref[...] = dct

    full = lambda w_, col: pl.BlockSpec((L, w_), lambda j: (0, col // w_), pipeline_mode=pl.Buffered(1))
    blk = lambda col: pl.BlockSpec((tq, FOX_W), lambda j: (j, col // FOX_W))
    return pl.pallas_call(
        body, name="fox_bwd", grid=(nq,),
        in_specs=[full(FOX_W, P_Q), blk(P_K), blk(P_V), full(FOX_W, SSD_W), full(FOX_W, 0), full(128, 0), full(128, 0),
                  pl.BlockSpec((128, tq), lambda j: (0, j))],
        out_specs=[pl.BlockSpec((L, FOX_W), lambda j: (0, 0)), pl.BlockSpec((tq, FOX_W), lambda j: (j, 0)),
                   pl.BlockSpec((tq, FOX_W), lambda j: (j, 0)), pl.BlockSpec((128, tq), lambda j: (0, j))],
        out_shape=[jax.ShapeDtypeStruct((L, FOX_W), f32)] * 3 + [jax.ShapeDtypeStruct((128, L), f32)],
        compiler_params=_cp(("arbitrary",)))(proj, proj, proj, dycat, o_acc, lse, cum, cumt)


HL = 128
AW = FOX_H * HL


def _np_place(rows, cols, pairs, dtype):
    import numpy as np
    m = np.zeros((rows, cols), np.float32)
    for r, c in pairs:
        m[r, c] = 1.0
    return jnp.asarray(m, dtype)


def _fox_consts():
    data = [(h * FOX_HD + d, h * HL + d) for h in range(FOX_H) for d in range(FOX_HD)]
    return dict(
        pq=_np_place(FOX_W, AW, data, bf16),
        pqt=_np_place(AW, FOX_W, [(c, r) for r, c in data], bf16),
        cum_a=[_np_place(128, AW, [(SM_F + h, h * HL + 64 + r) for h in range(FOX_H)], bf16) for r in range(3)],
        head_a=[_np_place(128, AW, [(h, h * HL + 64 + r) for h in range(FOX_H)], bf16) for r in range(3)],
        head_b=[_np_place(128, AW, [(h, h * HL + 67 + r) for h in range(FOX_H)], bf16) for r in range(3)],
        group=_np_place(FOX_W, 128, [(h * FOX_HD + d, h) for h in range(FOX_H) for d in range(FOX_HD)], f32),
        col_a=_np_place(AW, 128, [(h * HL + 64, SM_F + h) for h in range(FOX_H)], f32))


def _split3(x):
    hi = x.astype(bf16)
    r1 = x - hi.astype(f32)
    mid = r1.astype(bf16)
    return hi, mid, (r1 - mid.astype(f32)).astype(bf16)


def _slot_ones(tt, first):
    lane = lax.broadcasted_iota(jnp.int32, (tt, AW), 1) % HL
    return jnp.where((lane >= first) & (lane < first + 3), 1.0, 0.0)


def _fox_prep(proj, hp, cst, *, tt=256):
    L = proj.shape[0]

    def body(q_ref, k_ref, v_ref, sm_ref, hp_ref, pq_ref, c0_ref, c1_ref, c2_ref, qa_ref, ka_ref, va_ref, carry):
        @pl.when(pl.program_id(0) == 0)
        def _():
            carry[...] = jnp.zeros_like(carry)
        xx = sm_ref[...] + hp_ref[3:4, :]
        logf = jnp.minimum(xx, 0.0) - _log1pexp(xx)
        r = lax.broadcasted_iota(jnp.int32, (tt, tt), 0)
        c = lax.broadcasted_iota(jnp.int32, (tt, tt), 1)
        cum = _dot_hi(jnp.where(r >= c, 1.0, 0.0), logf) + carry[0:1, :]
        carry[0:1, :] = cum[tt - 1:tt, :]
        parts = _split3(-cum)
        pq = pq_ref[...]
        a_ones, b_ones = _slot_ones(tt, 64), _slot_ones(tt, 67)
        qa_ref[...] = (_dot((q_ref[...] * FOX_SCALE).astype(bf16), pq) + a_ones).astype(bf16)
        ka = _dot(k_ref[...].astype(bf16), pq) + b_ones
        for part, c_ref in zip(parts, (c0_ref, c1_ref, c2_ref)):
            ka = ka + _dot(part, c_ref[...])
        ka_ref[...] = ka.astype(bf16)
        va_ref[...] = (_dot(v_ref[...].astype(bf16), pq) + a_ones).astype(bf16)

    col = lambda c_: pl.BlockSpec((tt, FOX_W), lambda i: (i, c_ // FOX_W))
    out = pl.BlockSpec((tt, AW), lambda i: (i, 0))
    return pl.pallas_call(
        body, name="fox_prep", grid=(L // tt,),
        in_specs=[col(P_Q), col(P_K), col(P_V), pl.BlockSpec((tt, 128), lambda i: (i, P_SM // 128)),
                  _const_spec((8, 128)), _const_spec((FOX_W, AW))] + [_const_spec((128, AW))] * 3,
        out_specs=[out, out, out], out_shape=[jax.ShapeDtypeStruct((L, AW), bf16)] * 3,
        scratch_shapes=[pltpu.VMEM((8, 128), f32)],
        compiler_params=_cp(("arbitrary",)))(proj, proj, proj, proj, hp, cst["pq"], *cst["cum_a"])


def _fox_attn_fwd(qa, ka, va, *, tq=256):
    L = qa.shape[0]

    def body(qa_ref, ka_ref, va_ref, o_ref, oa_ref, lse_ref):
        i = pl.program_id(0)
        row = lax.broadcasted_iota(jnp.int32, (tq, tq), 0)
        col = lax.broadcasted_iota(jnp.int32, (tq, tq), 1)
        diag_bias = jnp.where(row >= col, 0.0, NEG)

        def block(r0, carry, bias):
            out = []
            for h in range(FOX_H):
                hl = slice(h * HL, (h + 1) * HL)
                m, acc = carry[h]
                s = _dot_nt(qa_ref[:, hl], ka_ref[pl.ds(r0, tq), hl])
                if bias is not None:
                    s = s + bias
                m_new = jnp.maximum(m, jnp.max(s, -1, keepdims=True))
                p = jnp.exp(s - m_new)
                al = jnp.exp(m - m_new)
                pb = p.astype(bf16)
                p_lo = (p - pb.astype(f32)).astype(bf16)
                vj = va_ref[pl.ds(r0, tq), hl]
                out.append((m_new, al * acc + (_dot(pb, vj) + _dot(p_lo, vj))))
            return tuple(out)

        init = tuple((jnp.full((tq, 1), NEG, f32), jnp.zeros((tq, HL), f32)) for _ in range(FOX_H))
        carry = lax.fori_loop(0, i, lambda j, c: block(pl.multiple_of(j * tq, tq), c, None), init)
        carry = block(pl.multiple_of(i * tq, tq), carry, diag_bias)
        lane = lax.broadcasted_iota(jnp.int32, (tq, 128), 1)
        lse_all = jnp.zeros((tq, 128), f32)
        for h in range(FOX_H):
            hs = slice(h * FOX_HD, (h + 1) * FOX_HD)
            m, acc = carry[h]
            l = acc[:, FOX_HD:FOX_HD + 1]
            o = acc[:, :FOX_HD] * (1.0 / l)
            o_ref[:, hs] = o.astype(bf16)
            oa_ref[:, hs] = o
            lse_all = jnp.where(lane == h, m + jnp.log(l), lse_all)
        lse_ref[...] = lse_all

    full = pl.BlockSpec((L, AW), lambda i: (0, 0), pipeline_mode=pl.Buffered(1))
    return pl.pallas_call(
        body, name="fox_fwd", grid=(L // tq,),
        in_specs=[pl.BlockSpec((tq, AW), lambda i: (i, 0)), full, full],
        out_specs=[pl.BlockSpec((tq, FOX_W), lambda i: (i, 0)), pl.BlockSpec((tq, FOX_W), lambda i: (i, 0)),
                   pl.BlockSpec((tq, 128), lambda i: (i, 0))],
        out_shape=[jax.ShapeDtypeStruct((L, FOX_W), bf16), jax.ShapeDtypeStruct((L, FOX_W), f32),
                   jax.ShapeDtypeStruct((L, 128), f32)],
        compiler_params=_cp(("parallel",)))(qa, ka, va)


def _fox_bprep(qa, dycat, o_acc, lse, cst, *, tt=256):
    L = qa.shape[0]

    def body(qa_ref, do_ref, oa_ref, lse_ref, pq_ref, g_ref, a0, a1, a2, b0, b1, b2, qb_ref, doa_ref, qbt_ref, doat_ref):
        dob = do_ref[...].astype(bf16)
        delta = _dot_hi(dob.astype(f32) * oa_ref[...], g_ref[...])
        doa = _dot(dob, pq_ref[...])
        for part, ref in zip(_split3(-delta), (a0, a1, a2)):
            doa = doa + _dot(part, ref[...])
        qb = qa_ref[...].astype(f32)
        for part, ref in zip(_split3(-lse_ref[...]), (b0, b1, b2)):
            qb = qb + _dot(part, ref[...])
        doa, qb = doa.astype(bf16), qb.astype(bf16)
        doa_ref[...] = doa
        qb_ref[...] = qb
        doat_ref[...] = doa.T
        qbt_ref[...] = qb.T

    tok = lambda w_: pl.BlockSpec((tt, w_), lambda i: (i, 0))
    tr = pl.BlockSpec((AW, tt), lambda i: (0, i))
    return pl.pallas_call(
        body, name="fox_bprep", grid=(L // tt,),
        in_specs=[tok(AW), pl.BlockSpec((tt, FOX_W), lambda i: (i, SSD_W // FOX_W)), tok(FOX_W), tok(128),
                  _const_spec((FOX_W, AW)), _const_spec((FOX_W, 128))] + [_const_spec((128, AW))] * 6,
        out_specs=[tok(AW), tok(AW), tr, tr],
        out_shape=[jax.ShapeDtypeStruct((L, AW), bf16)] * 2 + [jax.ShapeDtypeStruct((AW, L), bf16)] * 2,
        compiler_params=_cp(("parallel",)))(qa, dycat, o_acc, lse, cst["pq"], cst["group"], *cst["head_a"], *cst["head_b"])


def _fox_attn_bwd(ka, va, qb, doa, qbt, doat, *, tq=256):
    L = ka.shape[0]
    nq = L // tq

    def body(ka_ref, va_ref, qb_ref, doa_ref, qbt_ref, doat_ref, dq_ref, dkt_ref, dvt_ref):
        j = pl.program_id(0)

        @pl.when(j == 0)
        def _():
            dq_ref[...] = jnp.zeros_like(dq_ref)
        row = lax.broadcasted_iota(jnp.int32, (tq, tq), 0)
        col = lax.broadcasted_iota(jnp.int32, (tq, tq), 1)
        diag_bias = jnp.where(row >= col, 0.0, NEG)

        def block(r0, carry, bias):
            out = []
            for h in range(FOX_H):
                hl = slice(h * HL, (h + 1) * HL)
                dkt, dvt, dsum = carry[h]
                kj = ka_ref[:, hl]
                s = _dot_nt(qb_ref[pl.ds(r0, tq), hl], kj)
                if bias is not None:
                    s = s + bias
                p = jnp.exp(s)
                ds = p * _dot_nt(doa_ref[pl.ds(r0, tq), hl], va_ref[:, hl])
                dsb = ds.astype(bf16)
                dq_ref[pl.ds(r0, tq), hl] += _dot(dsb, kj)
                out.append((dkt + _dot(qbt_ref[hl, pl.ds(r0, tq)], dsb),
                            dvt + _dot(doat_ref[hl, pl.ds(r0, tq)], p.astype(bf16)),
                            dsum + jnp.sum(ds, 0, keepdims=True)))
            return tuple(out)

        init = tuple((jnp.zeros((HL, tq), f32), jnp.zeros((HL, tq), f32), jnp.zeros((1, tq), f32))
                     for _ in range(FOX_H))
        carry = block(pl.multiple_of(j * tq, tq), init, diag_bias)
        carry = lax.fori_loop(j + 1, nq, lambda i, c: block(pl.multiple_of(i * tq, tq), c, None), carry)
        for h in range(FOX_H):
            hl = slice(h * HL, (h + 1) * HL)
            dkt_ref[hl, :] = carry[h][0]
            dvt_ref[hl, :] = carry[h][1]
            dkt_ref[h * HL + FOX_HD:h * HL + FOX_HD + 1, :] = carry[h][2]

    full = lambda shape: pl.BlockSpec(shape, lambda j: (0, 0), pipeline_mode=pl.Buffered(1))
    blk = pl.BlockSpec((tq, AW), lambda j: (j, 0))
    trb = pl.BlockSpec((AW, tq), lambda j: (0, j))
    return pl.pallas_call(
        body, name="fox_bwd", grid=(nq,),
        in_specs=[blk, blk, full((L, AW)), full((L, AW)), full((AW, L)), full((AW, L))],
        out_specs=[pl.BlockSpec((L, AW), lambda j: (0, 0)), trb, trb],
        out_shape=[jax.ShapeDtypeStruct((L, AW), f32), jax.ShapeDtypeStruct((AW, L), f32),
                   jax.ShapeDtypeStruct((AW, L), f32)],
        compiler_params=_cp(("arbitrary",)))(ka, va, qb, doa, qbt, doat)


def _fox_post(dq, dkt, dvt, proj, ddt, hp, cst, *, tt=256):
    L = proj.shape[0]
    n = L // tt

    def body(dq_ref, dkt_ref, dvt_ref, sm_ref, ddt_ref, hp_ref, pqt_ref, ca_ref,
             dqo_ref, dko_ref, dvo_ref, dsm_ref, acc_ref, carry):
        @pl.when(pl.program_id(0) == 0)
        def _():
            carry[...] = jnp.zeros_like(carry)
            acc_ref[...] = jnp.zeros_like(acc_ref)
        pqt = pqt_ref[...]
        dk_full = dkt_ref[...].T
        dqo_ref[...] = _dot((dq_ref[...] * FOX_SCALE).astype(bf16), pqt).astype(bf16)
        dko_ref[...] = _dot(dk_full.astype(bf16), pqt).astype(bf16)
        dvo_ref[...] = _dot(dvt_ref[...].T.astype(bf16), pqt).astype(bf16)
        dc = -_dot_hi(dk_full, ca_ref[...])
        r = lax.broadcasted_iota(jnp.int32, (tt, tt), 0)
        c = lax.broadcasted_iota(jnp.int32, (tt, tt), 1)
        dl = _dot_hi(jnp.where(r <= c, 1.0, 0.0), dc) + carry[0:1, :]
        carry[0:1, :] += jnp.sum(dc, 0, keepdims=True)
        xx = sm_ref[...] + hp_ref[3:4, :]
        lane = lax.broadcasted_iota(jnp.int32, (tt, 128), 1)
        dlogit = jnp.where((lane >= SM_F) & (lane < SM_F + FOX_H), dl * _sigmoid(-xx), 0.0)
        acc_ref[3:4, :] += jnp.sum(dlogit, 0, keepdims=True)
        dsm_ref[...] = (dlogit + ddt_ref[...]).astype(bf16)

    rev = lambda w_: pl.BlockSpec((tt, w_), lambda i: (n - 1 - i, 0))
    revt = pl.BlockSpec((AW, tt), lambda i: (0, n - 1 - i))
    return pl.pallas_call(
        body, name="fox_post", grid=(n,),
        in_specs=[rev(AW), revt, revt, pl.BlockSpec((tt, 128), lambda i: (n - 1 - i, P_SM // 128)), rev(128),
                  _const_spec((8, 128)), _const_spec((AW, FOX_W)), _const_spec((AW, 128))],
        out_specs=[rev(FOX_W), rev(FOX_W), rev(FOX_W), rev(128), pl.BlockSpec((8, 128), lambda i: (0, 0))],
        out_shape=[jax.ShapeDtypeStruct((L, FOX_W), bf16)] * 3 + [jax.ShapeDtypeStruct((L, 128), bf16),
                                                                   jax.ShapeDtypeStruct((8, 128), f32)],
        scratch_shapes=[pltpu.VMEM((8, 128), f32)],
        compiler_params=_cp(("arbitrary",)))(dq, dkt, dvt, proj, ddt, hp, cst["pqt"], cst["col_a"])


SSD_GW = SSD_W // SSD_G
SSD_HPG = SSD_H // SSD_G


def _ssd_pre(x, xprev, sm, cp_ref, hp_ref, tc):
    pre = (cp_ref[4:5, :] + cp_ref[3:4, :] * x + cp_ref[2:3, :] * _shift_down(x, xprev, 1)
           + cp_ref[1:2, :] * _shift_down(x, xprev, 2) + cp_ref[0:1, :] * _shift_down(x, xprev, 3))
    sig = _sigmoid(pre)
    raw = sm + hp_ref[0:1, :]
    dt = jnp.maximum(raw, 0.0) + _log1pexp(raw)
    a_neg = -jnp.exp(hp_ref[1:2, :])
    r = lax.broadcasted_iota(jnp.int32, (tc, tc), 0)
    c = lax.broadcasted_iota(jnp.int32, (tc, tc), 1)
    cs = _dot_hi(jnp.where(r >= c, 1.0, 0.0), dt * a_neg)
    return pre, sig, raw, dt, a_neg, cs, cs.T, r >= c


def _ssd_fwd(proj, cp, hp, ng, *, tc=256):
    L = proj.shape[0]
    nc = L // tc

    def body(xc_ref, xp_ref, z_ref, sm_ref, cp_ref, hp_ref, ng_ref, y_ref, ypre_ref, sin_ref, s_scr):
        i = pl.program_id(0)

        @pl.when(i == 0)
        def _():
            s_scr[...] = jnp.zeros_like(s_scr)
        x = xc_ref[...]
        xprev = xp_ref[...] * jnp.where(i > 0, 1.0, 0.0)
        pre, sig, _, dt, _, cs, cst, tril = _ssd_pre(x, xprev, sm_ref[...], cp_ref, hp_ref, tc)
        xbc = pre * sig
        sin_ref[...] = s_scr[...]
        for g in range(SSD_G):
            bg = xbc[:, SSD_W + g * SSD_N:SSD_W + (g + 1) * SSD_N]
            cg = xbc[:, SSD_W + SSD_G * SSD_N + g * SSD_N:SSD_W + SSD_G * SSD_N + (g + 1) * SSD_N].astype(bf16)
            cb = _dot_nt(cg, bg.astype(bf16))
            for e in range(SSD_HPG):
                h = g * SSD_HPG + e
                hs = slice(h * SSD_HD, (h + 1) * SSD_HD)
                xs = xbc[:, hs]
                csc = cs[:, h:h + 1]
                lm = jnp.where(tril, jnp.exp(jnp.minimum(csc - cst[h:h + 1, :], 0.0)), 0.0)
                xdt = (xs * dt[:, h:h + 1]).astype(bf16)
                s_h = s_scr[:, hs]
                y = _dot((cb * lm).astype(bf16), xdt) + jnp.exp(csc) * _dot(cg, s_h.astype(bf16))
                ypre_ref[:, hs] = y + hp_ref[2:3, h:h + 1] * xs
                cl = cs[tc - 1:tc, h:h + 1]
                bd = (bg * jnp.exp(cl - csc)).astype(bf16)
                s_scr[:, hs] = jnp.exp(cl) * s_h + _dot_tn(bd, xdt)
        z = z_ref[...]
        yz = ypre_ref[...] * (z * _sigmoid(z))
        for g in range(SSD_G):
            gs = slice(g * SSD_GW, (g + 1) * SSD_GW)
            yg = yz[:, gs]
            r = lax.rsqrt(jnp.mean(yg * yg, -1, keepdims=True) + RMS_EPS)
            y_ref[:, gs] = (yg * r * ng_ref[0:1, gs]).astype(bf16)

    return pl.pallas_call(
        body, name="ssd_fwd", grid=(nc,),
        in_specs=[pl.BlockSpec((tc, 1024), lambda i: (i, 0)),
                  pl.BlockSpec((tc, 1024), lambda i: (jnp.maximum(i - 1, 0), 0)),
                  pl.BlockSpec((tc, SSD_W), lambda i: (i, P_Z // SSD_W)),
                  pl.BlockSpec((tc, 128), lambda i: (i, P_SM // 128)),
                  _const_spec((8, 1024)), _const_spec((8, 128)), _const_spec((8, SSD_W))],
        out_specs=[pl.BlockSpec((tc, SSD_W), lambda i: (i, 0)), pl.BlockSpec((tc, SSD_W), lambda i: (i, 0)),
                   pl.BlockSpec((SSD_N, SSD_W), lambda i: (i, 0))],
        out_shape=[jax.ShapeDtypeStruct((L, SSD_W), bf16), jax.ShapeDtypeStruct((L, SSD_W), f32),
                   jax.ShapeDtypeStruct((nc * SSD_N, SSD_W), f32)],
        scratch_shapes=[pltpu.VMEM((SSD_N, SSD_W), f32)],
        compiler_params=_cp(("arbitrary",)))(proj, proj, proj, proj, cp, hp, ng)


def _ssd_bwd(proj, dycat, ypre, sin, cp, hp, ng, *, tc=256):
    L = proj.shape[0]
    nc = L // tc

    def body(xc_ref, xp_ref, z_ref, sm_ref, cp_ref, hp_ref, ng_ref, sin_ref, ypre_ref, dy_ref,
             dxbc_ref, dz_ref, ddt_ref, acc1_ref, acc2_ref, ds_scr, dnext_scr, dxbc_scr):
        i = pl.program_id(0)
        c_idx = nc - 1 - i

        @pl.when(i == 0)
        def _():
            ds_scr[...] = jnp.zeros_like(ds_scr)
            dnext_scr[...] = jnp.zeros_like(dnext_scr)
            acc1_ref[...] = jnp.zeros_like(acc1_ref)
            acc2_ref[...] = jnp.zeros_like(acc2_ref)
        x = xc_ref[...]
        xprev = xp_ref[...] * jnp.where(c_idx > 0, 1.0, 0.0)
        pre, sig, raw, dt, a_neg, cs, cst, tril = _ssd_pre(x, xprev, sm_ref[...], cp_ref, hp_ref, tc)
        xbc = pre * sig
        z = z_ref[...]
        sz = _sigmoid(z)
        silz = z * sz
        yall = ypre_ref[...]
        yz = yall * silz
        dy = dy_ref[...]
        dyz_parts = []
        for g in range(SSD_G):
            gs = slice(g * SSD_GW, (g + 1) * SSD_GW)
            yg, dyg = yz[:, gs], dy[:, gs]
            r = lax.rsqrt(jnp.mean(yg * yg, -1, keepdims=True) + RMS_EPS)
            acc1_ref[5:6, gs] += jnp.sum(dyg * yg * r, 0, keepdims=True)
            dyn = dyg * ng_ref[0:1, gs]
            dyz_parts.append(r * (dyn - yg * (r * r) * jnp.mean(dyn * yg, -1, keepdims=True)))
        dyz = jnp.concatenate(dyz_parts, axis=1)
        dz_ref[...] = (dyz * yall * (sz * (1.0 + z * (1.0 - sz)))).astype(bf16)
        dyall = dyz * silz

        lane1 = lax.broadcasted_iota(jnp.int32, (1, 128), 1)
        sub = lax.broadcasted_iota(jnp.int32, (128, tc), 0)
        rowc = lax.broadcasted_iota(jnp.int32, (tc, 1), 0)
        dcs = jnp.zeros((tc, 128), f32)
        dcsr = jnp.zeros((128, tc), f32)
        ddt = jnp.zeros((tc, 128), f32)
        dd_row = jnp.zeros((1, 128), f32)
        for g in range(SSD_G):
            b0 = SSD_W + g * SSD_N
            c0 = SSD_W + SSD_G * SSD_N + g * SSD_N
            bg = xbc[:, b0:b0 + SSD_N]
            bgb = bg.astype(bf16)
            cgb = xbc[:, c0:c0 + SSD_N].astype(bf16)
            cb = _dot_nt(cgb, bgb)
            dbg = jnp.zeros((tc, SSD_N), f32)
            dcg = jnp.zeros((tc, SSD_N), f32)
            for e in range(SSD_HPG):
                h = g * SSD_HPG + e
                hs = slice(h * SSD_HD, (h + 1) * SSD_HD)
                oh = jnp.where(lane1 == h, 1.0, 0.0)
                xs = xbc[:, hs]
                dth = dt[:, h:h + 1]
                csc = cs[:, h:h + 1]
                lm = jnp.where(tril, jnp.exp(jnp.minimum(csc - cst[h:h + 1, :], 0.0)), 0.0)
                m = cb * lm
                xdt = (xs * dth).astype(bf16)
                s_h = sin_ref[:, hs]
                s_hb = s_h.astype(bf16)
                dyh = dyall[:, hs]
                dyb = dyh.astype(bf16)
                dd_row = dd_row + oh * jnp.sum(dyh * xs)
                dxs = hp_ref[2:3, h:h + 1] * dyh
                ecs = jnp.exp(csc)
                cs_prod = _dot(cgb, s_hb)
                dcsb = (ecs * dyh).astype(bf16)
                dcg = dcg + _dot_nt(dcsb, s_hb)
                ds_in = _dot_tn(cgb, dcsb)
                dcs_h = jnp.sum(dyh * ecs * cs_prod, -1, keepdims=True)
                dm = _dot_nt(dyb, xdt)
                w = dm * m
                dcs_h = dcs_h + jnp.sum(w, -1, keepdims=True)
                dcsr = jnp.where(sub == h, jnp.sum(w, 0, keepdims=True), dcsr)
                dcbb = (dm * lm).astype(bf16)
                dcg = dcg + _dot(dcbb, bgb)
                dbg = dbg + _dot_tn(dcbb, cgb)
                dxdt = _dot_tn(m.astype(bf16), dyb)
                dsn = ds_scr[:, hs]
                dsnb = dsn.astype(bf16)
                cl = cs[tc - 1:tc, h:h + 1]
                dec = jnp.exp(cl - csc)
                dxdt = dxdt + _dot((bg * dec).astype(bf16), dsnb)
                dbd = _dot_nt(xdt, dsnb)
                dbg = dbg + dbd * dec
                gdec = jnp.sum(dbd * bg, -1, keepdims=True) * dec
                ecl = jnp.exp(cl)
                dcl = jnp.sum(gdec) + jnp.sum(dsn * s_h) * ecl
                ds_scr[:, hs] = ecl * dsn + ds_in
                dcs_h = dcs_h - gdec + jnp.where(rowc == tc - 1, dcl, 0.0)
                dcs = dcs + dcs_h * oh
                dxbc_scr[:, hs] = dxs + dxdt * dth
                ddt = ddt + jnp.sum(dxdt * xs, -1, keepdims=True) * oh
            dxbc_scr[:, b0:b0 + SSD_N] = dbg
            dxbc_scr[:, c0:c0 + SSD_N] = dcg
        dcs = dcs - dcsr.T
        r_i = lax.broadcasted_iota(jnp.int32, (tc, tc), 0)
        c_i = lax.broadcasted_iota(jnp.int32, (tc, tc), 1)
        da = _dot_hi(jnp.where(r_i <= c_i, 1.0, 0.0), dcs)
        ddt = ddt + da * a_neg
        acc2_ref[1:2, :] += jnp.sum(da * dt, 0, keepdims=True) * a_neg
        lane = lax.broadcasted_iota(jnp.int32, (tc, 128), 1)
        ddraw = jnp.where(lane < SSD_H, ddt * _sigmoid(raw), 0.0)
        acc2_ref[0:1, :] += jnp.sum(ddraw, 0, keepdims=True)
        acc2_ref[2:3, :] += dd_row
        ddt_ref[...] = ddraw
        dpre = dxbc_scr[...] * (sig * (1.0 + pre * (1.0 - sig)))
        acc1_ref[4:5, :] += jnp.sum(dpre, 0, keepdims=True)
        for k in range(SSD_K):
            acc1_ref[k:k + 1, :] += jnp.sum(dpre * _shift_down(x, xprev, SSD_K - 1 - k), 0, keepdims=True)
        dnext = dnext_scr[...]
        dxbc_ref[...] = (cp_ref[3:4, :] * dpre + cp_ref[2:3, :] * _shift_up(dpre, dnext, 1)
                         + cp_ref[1:2, :] * _shift_up(dpre, dnext, 2)
                         + cp_ref[0:1, :] * _shift_up(dpre, dnext, 3)).astype(bf16)
        dnext_scr[...] = dpre

    rev = lambda w_, col: pl.BlockSpec((tc, w_), lambda i: (nc - 1 - i, col // w_))
    return pl.pallas_call(
        body, name="ssd_bwd", grid=(nc,),
        in_specs=[rev(1024, 0), pl.BlockSpec((tc, 1024), lambda i: (jnp.maximum(nc - 2 - i, 0), 0)),
                  rev(SSD_W, P_Z), rev(128, P_SM),
                  _const_spec((8, 1024)), _const_spec((8, 128)), _const_spec((8, SSD_W)),
                  pl.BlockSpec((SSD_N, SSD_W), lambda i: (nc - 1 - i, 0)), rev(SSD_W, 0), rev(SSD_W, 0)],
        out_specs=[rev(1024, 0), rev(SSD_W, 0), rev(128, 0),
                   pl.BlockSpec((8, 1024), lambda i: (0, 0)), pl.BlockSpec((8, 128), lambda i: (0, 0))],
        out_shape=[jax.ShapeDtypeStruct((L, 1024), bf16), jax.ShapeDtypeStruct((L, SSD_W), bf16),
                   jax.ShapeDtypeStruct((L, 128), f32), jax.ShapeDtypeStruct((8, 1024), f32),
                   jax.ShapeDtypeStruct((8, 128), f32)],
        scratch_shapes=[pltpu.VMEM((SSD_N, SSD_W), f32), pltpu.VMEM((tc, 1024), f32), pltpu.VMEM((tc, 1024), f32)],
        compiler_params=_cp(("arbitrary",)))(proj, proj, proj, proj, cp, hp, ng, sin, ypre, dycat)


def _pack_cols(w):
    pad = jnp.zeros(w.shape[:-1] + (P_W - P_SM - SSD_H - FOX_H,), w.dtype)
    return jnp.concatenate([w[..., 512:1536], w[..., 0:512], w[..., 1544:2312], w[..., 2316:3084],
                            w[..., 1536:1544], w[..., 2312:2316], pad], axis=-1)


def _unpack_cols(g):
    return jnp.concatenate([g[..., 1024:1536], g[..., 0:1024], g[..., 3072:3080], g[..., 1536:2304],
                            g[..., 3080:3084], g[..., 2304:3072]], axis=-1)


def _rows8(*rows):
    width = max(r.shape[-1] for r in rows)
    out = [jnp.pad(r.astype(f32), (0, width - r.shape[-1])) for r in rows]
    out += [jnp.zeros((width,), f32)] * (8 - len(out))
    return jnp.stack(out)


def _local_step(x, tgt, p):
    gb_in = _rows8(p["ln_in_g"], p["ln_in_b"])
    cst = _fox_consts()
    x0 = _ln_in_fwd(x, gb_in)
    saved = []
    for l in range(DEPTH):
        mv = [_rows8(p["mod"][l, 3 * j], p["mod"][l, 3 * j + 1], p["mod"][l, 3 * j + 2], p["ln_g"][l, j], p["ln_b"][l, j])
              for j in range(3)]
        cp = _rows8(*[p["ssd_conv_w"][l, k] for k in range(SSD_K)], p["ssd_conv_b"][l])
        hp = _rows8(jnp.pad(p["ssd_dt_bias"][l], (0, 120)), jnp.pad(p["ssd_a_log"][l], (0, 120)),
                    jnp.pad(p["ssd_d"][l], (0, 120)), jnp.pad(p["fox_f_bias"][l], (SM_F, 128 - SM_F - FOX_H)))
        ng = _rows8(p["ssd_norm_g"][l])
        scw = _rows8(*[p["sconv_w"][l, k] for k in range(SC_K)])
        x1 = _ffn_fwd(x0, mv[0], p["ffn1_w_in"][l], p["ffn1_w_out"][l])
        proj = _inproj_fwd(x1, mv[1], p["mix_w_in"][l])
        y_ssd, ypre, sin = _ssd_fwd(proj, cp, hp, ng)
        qa, ka, va = _fox_prep(proj, hp, cst)
        o, o_acc, lse = _fox_attn_fwd(qa, ka, va)
        y_sc = _sconv_fwd(proj, scw)
        ycat = jnp.concatenate([y_ssd, o, y_sc], axis=1)
        x2 = _outproj_fwd(x1, ycat, mv[1], p["mix_w_out"][l])
        x3 = _ffn_fwd(x2, mv[2], p["ffn2_w_in"][l], p["ffn2_w_out"][l])
        saved.append((x0, x1, x2, mv, cp, hp, ng, scw, proj, ypre, sin, qa, ka, va, o_acc, lse, ycat))
        x0 = x3
    dx, loss_acc = _loss_head(x0, tgt)

    g = {k: [None] * DEPTH for k in (
        "ffn1_w_in", "ffn1_w_out", "ffn2_w_in", "ffn2_w_out", "mix_w_in", "mix_w_out", "mod", "ln_g", "ln_b",
        "ssd_conv_w", "ssd_conv_b", "ssd_dt_bias", "ssd_a_log", "ssd_d", "ssd_norm_g", "fox_f_bias", "sconv_w")}
    for l in reversed(range(DEPTH)):
        x0, x1, x2, mv, cp, hp, ng, scw, proj, ypre, sin, qa, ka, va, o_acc, lse, ycat = saved[l]
        dx, h, da, act, dyb, a2 = _ffn_bwd(x2, dx, mv[2], p["ffn2_w_in"][l], p["ffn2_w_out"][l])
        g["ffn2_w_in"][l] = _matmul_tn(h, da, tn=FS, tk=512, name="dw_ffn_in")
        g["ffn2_w_out"][l] = _matmul_tn(act, dyb, tn=D, tk=512, name="dw_ffn_out")
        dxp, dyb, dycat, a1 = _outproj_bwd(x1, ycat, dx, mv[1], p["mix_w_out"][l])
        g["mix_w_out"][l] = _matmul_tn(ycat, dyb, tn=D // 2, tk=512, name="dw_mix_out")
        dxbc, dz, ddt, acc1, acc2 = _ssd_bwd(proj, dycat, ypre, sin, cp, hp, ng)
        qb, doa, qbt, doat = _fox_bprep(qa, dycat, o_acc, lse, cst)
        dq, dkt, dvt = _fox_attn_bwd(ka, va, qb, doa, qbt, doat)
        dq, dk, dv, dsm, accf = _fox_post(dq, dkt, dvt, proj, ddt, hp, cst)
        dsb, dsc, dsx, accs = _sconv_bwd(proj, dycat, scw)
        dproj = jnp.concatenate([dxbc, dz, dq, dk, dv, dsb, dsc, dsx, dsm], axis=1)
        dx, h, a1b = _inproj_bwd(x1, dxp, dproj, mv[1], p["mix_w_in"][l])
        g["mix_w_in"][l] = _matmul_tn(h, dproj, tn=P_W // 5, tk=512, name="dw_mix_in")
        dx, h, da, act, dyb, a0 = _ffn_bwd(x0, dx, mv[0], p["ffn1_w_in"][l], p["ffn1_w_out"][l])
        g["ffn1_w_in"][l] = _matmul_tn(h, da, tn=FS, tk=512, name="dw_ffn_in")
        g["ffn1_w_out"][l] = _matmul_tn(act, dyb, tn=D, tk=512, name="dw_ffn_out")
        g["mod"][l] = jnp.concatenate([a0[0:3], a1b[0:2], a1[2:3], a2[0:3]], axis=0)
        g["ln_g"][l] = jnp.stack([a0[3], a1[3], a2[3]])
        g["ln_b"][l] = jnp.stack([a0[4], a1[4], a2[4]])
        g["ssd_conv_w"][l] = acc1[0:SSD_K]
        g["ssd_conv_b"][l] = acc1[4]
        g["ssd_norm_g"][l] = acc1[5, :SSD_W]
        g["ssd_dt_bias"][l] = acc2[0, :SSD_H]
        g["ssd_a_log"][l] = acc2[1, :SSD_H]
        g["ssd_d"][l] = acc2[2, :SSD_H]
        g["fox_f_bias"][l] = accf[3, SM_F:SM_F + FOX_H]
        g["sconv_w"][l] = accs[0:SC_K]
    grad_x, a_in = _ln_in_bwd(x, dx, gb_in)
    g = {k: (v if k in BIG else jnp.stack(v)) for k, v in g.items()}
    g["ln_in_g"], g["ln_in_b"] = a_in[0], a_in[1]
    return loss_acc[0, 0], grad_x, g


MESH = pl.DeviceIdType.MESH
ANY = pl.BlockSpec(memory_space=pl.ANY)


def _all_gather(shards, *, in_vmem, name):
    n_arr = len(shards)

    def body(*refs):
        x_refs, out_refs = refs[:n_arr], refs[n_arr:2 * n_arr]
        send_sems, recv_sems, local_sems = refs[2 * n_arr:]
        x, y, c = lax.axis_index("x"), lax.axis_index("y"), lax.axis_index("c")
        me, sibling = (x, y, c), (x, y, 1 - c)
        chips = [(1 - x, y), (x, 1 - y), (1 - x, 1 - y)]

        def copy(a, k, block, to, src=None):
            px, py, pc = block
            slot = out_refs[a].at[4 * px + 2 * py + pc]
            return pltpu.make_async_remote_copy(
                src_ref=slot if src is None else src, dst_ref=slot,
                send_sem=send_sems.at[7 * a + k], recv_sem=recv_sems.at[7 * a + k], device_id=to, device_id_type=MESH)

        mine, first, passed = [], [], []
        for a in range(n_arr):
            mine.append(pltpu.make_async_copy(x_refs[a], out_refs[a].at[4 * x + 2 * y + c], local_sems.at[a]))
            mine[-1].start()
            first.append(copy(a, 0, me, sibling, src=x_refs[a]))
            first += [copy(a, 1 + j, me, (*chip, c), src=x_refs[a]) for j, chip in enumerate(chips)]
        for cp in first:
            cp.start()
        for j, chip in enumerate(chips):
            for a in range(n_arr):
                copy(a, 1 + j, (*chip, c), me).wait_recv()
                passed.append(copy(a, 4 + j, (*chip, c), sibling))
                passed[-1].start()
        for a in range(n_arr):
            copy(a, 0, sibling, me).wait_recv()
            for j, chip in enumerate(chips):
                copy(a, 4 + j, (*chip, 1 - c), me).wait_recv()
        for cp in first + passed:
            cp.wait_send()
        for cp in mine:
            cp.wait()

    spec = pl.BlockSpec(memory_space=pltpu.VMEM) if in_vmem else ANY
    return pl.pallas_call(
        body, name=name, out_shape=[jax.ShapeDtypeStruct((N_DEV,) + s.shape, s.dtype) for s in shards],
        in_specs=[spec] * n_arr, out_specs=[spec] * n_arr,
        scratch_shapes=[pltpu.SemaphoreType.DMA((7 * n_arr,)), pltpu.SemaphoreType.DMA((7 * n_arr,)),
                        pltpu.SemaphoreType.DMA((n_arr,))],
    )(*shards)


def _swap_sibling(sends):
    n_arr = len(sends)

    def body(*refs):
        s_refs, o_refs = refs[:n_arr], refs[n_arr:2 * n_arr]
        send_sems, recv_sems = refs[2 * n_arr:]
        x, y, c = lax.axis_index("x"), lax.axis_index("y"), lax.axis_index("c")
        cps = [pltpu.make_async_remote_copy(
            src_ref=s_refs[a].at[:, 1 - c], dst_ref=o_refs[a], send_sem=send_sems.at[a], recv_sem=recv_sems.at[a],
            device_id=(x, y, 1 - c), device_id_type=MESH) for a in range(n_arr)]
        for cp in cps:
            cp.start()
        for cp in cps:
            cp.wait_recv()
        for cp in cps:
            cp.wait_send()

    return pl.pallas_call(
        body, name="swap_sibling",
        out_shape=[jax.ShapeDtypeStruct((s.shape[0],) + s.shape[2:], s.dtype) for s in sends],
        in_specs=[ANY] * n_arr, out_specs=[ANY] * n_arr,
        scratch_shapes=[pltpu.SemaphoreType.DMA((n_arr,)), pltpu.SemaphoreType.DMA((n_arr,))])(*sends)


def _exchange_chips(bufs):
    n_arr = len(bufs)

    def body(*refs):
        b_refs, o_refs = refs[:n_arr], refs[n_arr:2 * n_arr]
        send_sems, recv_sems, local_sems = refs[2 * n_arr:]
        x, y, c = lax.axis_index("x"), lax.axis_index("y"), lax.axis_index("c")
        mine = 2 * x + y
        peers = [(x, 1 - y), (1 - x, y), (1 - x, 1 - y)]

        def copy(a, k, src_slot, dst_slot):
            px, py = peers[k]
            return pltpu.make_async_remote_copy(
                src_ref=b_refs[a].at[src_slot], dst_ref=o_refs[a].at[dst_slot],
                send_sem=send_sems.at[3 * a + k], recv_sem=recv_sems.at[3 * a + k],
                device_id=(px, py, c), device_id_type=MESH)

        own = [pltpu.make_async_copy(b_refs[a].at[mine], o_refs[a].at[mine], local_sems.at[a]) for a in range(n_arr)]
        sent = [copy(a, k, 2 * px + py, mine) for a in range(n_arr) for k, (px, py) in enumerate(peers)]
        for cp in own + sent:
            cp.start()
        for a in range(n_arr):
            for k, (px, py) in enumerate(peers):
                copy(a, k, mine, 2 * px + py).wait_recv()
        for cp in sent:
            cp.wait_send()
        for cp in own:
            cp.wait()

    return pl.pallas_call(
        body, name="exchange_chips", out_shape=[jax.ShapeDtypeStruct(b.shape, b.dtype) for b in bufs],
        in_specs=[ANY] * n_arr, out_specs=[ANY] * n_arr,
        scratch_shapes=[pltpu.SemaphoreType.DMA((3 * n_arr,)), pltpu.SemaphoreType.DMA((3 * n_arr,)),
                        pltpu.SemaphoreType.DMA((n_arr,))])(*bufs)


def _row_tile(r, c):
    if r * c <= 512 * 1024:
        return r
    t = r
    while t * c > 512 * 1024 and t % 2 == 0 and (t // 2) % 16 == 0:
        t //= 2
    return t


def _add_own_half(send, recv, c):
    nb, _, r, n = send.shape
    tr = _row_tile(r, n)

    def body(c_ref, s_ref, r_ref, o_ref):
        o_ref[...] = (s_ref[...].astype(f32) + r_ref[...].astype(f32)).astype(o_ref.dtype)

    return pl.pallas_call(
        body, name="add_own_half",
        grid_spec=pltpu.PrefetchScalarGridSpec(
            num_scalar_prefetch=1, grid=(nb, r // tr),
            in_specs=[pl.BlockSpec((None, None, tr, n), lambda j, i, cr: (j, cr[0], i, 0)),
                      pl.BlockSpec((None, tr, n), lambda j, i, cr: (j, i, 0))],
            out_specs=pl.BlockSpec((None, tr, n), lambda j, i, cr: (j, i, 0))),
        out_shape=jax.ShapeDtypeStruct((nb, r, n), bf16),
        compiler_params=_cp(("parallel", "parallel")))(jnp.reshape(c, (1,)).astype(jnp.int32), send, recv)


def _sum_slots(buf, *, tr, name):
    nb, r, n = buf.shape

    def body(b_ref, o_ref):
        acc = b_ref[0].astype(f32)
        for k in range(1, nb):
            acc = acc + b_ref[k].astype(f32)
        o_ref[...] = acc

    return pl.pallas_call(
        body, name=name, grid=(r // tr,),
        in_specs=[pl.BlockSpec((nb, tr, n), lambda i: (0, i, 0))],
        out_specs=pl.BlockSpec((tr, n), lambda i: (i, 0)),
        out_shape=jax.ShapeDtypeStruct((r, n), f32), compiler_params=_cp(("parallel",)))(buf)


def _ada_fwd(c_all, ada_w, ada_b_cols):
    n = ada_w.shape[-1]

    def body(c_ref, w_ref, b_ref, o_ref):
        cv = c_ref[...]
        ca = (cv * _sigmoid(cv)).astype(bf16)
        o_ref[...] = _dot(ca, w_ref[...].astype(bf16)) + b_ref[...]

    return pl.pallas_call(
        body, name="ada_fwd", grid=(DEPTH,),
        in_specs=[_const_spec((N_DEV, D)), pl.BlockSpec((None, D, n), lambda l: (l, 0, 0)),
                  pl.BlockSpec((None, 1, n), lambda l: (l, 0, 0))],
        out_specs=pl.BlockSpec((None, N_DEV, n), lambda l: (l, 0, 0)),
        out_shape=jax.ShapeDtypeStruct((DEPTH, N_DEV, n), f32), compiler_params=_cp(("parallel",)))(c_all, ada_w, ada_b_cols)


def _ada_bwd(c_all, dmod_cols):
    n = dmod_cols.shape[-1]

    def body(c_ref, d_ref, o_ref):
        cv = c_ref[...]
        ca = (cv * _sigmoid(cv)).astype(bf16)
        o_ref[...] = _dot_tn(ca, d_ref[...].astype(bf16))

    return pl.pallas_call(
        body, name="ada_bwd", grid=(DEPTH,),
        in_specs=[_const_spec((N_DEV, D)), pl.BlockSpec((None, N_DEV, n), lambda l: (l, 0, 0))],
        out_specs=pl.BlockSpec((None, D, n), lambda l: (l, 0, 0)),
        out_shape=jax.ShapeDtypeStruct((DEPTH, D, n), f32), compiler_params=_cp(("parallel",)))(c_all, dmod_cols)


def _adamw(w, g, m, v, *, tr, name):
    r, n = w.shape

    def body(w_ref, g_ref, m_ref, v_ref, d_ref, mo_ref, vo_ref):
        g_ = g_ref[...]
        m_ = ADAM_B1 * m_ref[...] + (1.0 - ADAM_B1) * g_
        v_ = ADAM_B2 * v_ref[...] + (1.0 - ADAM_B2) * jnp.square(g_)
        m_hat = m_ / (1.0 - ADAM_B1 ** ADAM_STEP)
        v_hat = v_ / (1.0 - ADAM_B2 ** ADAM_STEP)
        d_ref[...] = -ADAM_LR * (m_hat / (jnp.sqrt(v_hat) + ADAM_EPS) + ADAM_WD * w_ref[...])
        mo_ref[...] = m_
        vo_ref[...] = v_

    blk = pl.BlockSpec((tr, n), lambda i: (i, 0))
    return pl.pallas_call(
        body, name=name, grid=(r // tr,), in_specs=[blk] * 4, out_specs=[blk] * 3,
        out_shape=[jax.ShapeDtypeStruct((r, n), f32)] * 3, compiler_params=_cp(("parallel",)))(w, g, m, v)


WEIGHTS = ["ln_in_g", "ln_in_b", "ada_w", "ada_b", "ffn1_w_in", "ffn1_w_out", "mix_w_in", "mix_w_out", "ssd_conv_w",
           "ssd_conv_b", "ssd_dt_bias", "ssd_a_log", "ssd_d", "ssd_norm_g", "fox_f_bias", "sconv_w", "ffn2_w_in",
           "ffn2_w_out", "ln_g", "ln_b"]
BIG = ["ffn1_w_in", "ffn1_w_out", "ffn2_w_in", "ffn2_w_out", "mix_w_in", "mix_w_out"]
COL_SHARDED = ("ffn1_w_in", "ffn2_w_in")
SMALL_SHARDED = {"ssd_conv_w": 128, "sconv_w": 32, "ln_g": 128, "ln_b": 128}
ADAM_TR = {"ada_w": 256, "ffn1_w_in": 512, "ffn2_w_in": 512, "ffn1_w_out": 352, "ffn2_w_out": 352, "mix_w_in": 64,
           "mix_w_out": 256}


def _pad_rows(v, mult=128):
    v = v.reshape(-1)
    return jnp.pad(v, (0, (-v.shape[0]) % mult))


def _pack_rows(parts, row_mult=8):
    flat = [_pad_rows(p.astype(f32)) for p in parts]
    offs, o = [], 0
    for f in flat:
        offs.append(o)
        o += f.shape[0] // 128
    buf = jnp.concatenate(flat).reshape(-1, 128)
    return jnp.pad(buf, ((0, (-buf.shape[0]) % row_mult), (0, 0))), offs


def _take(buf, off, shape):
    n = 1
    for s in shape:
        n *= s
    rows = -(-n // 128)
    lead = buf.shape[:-2]
    flat = buf[..., off:off + rows, :].reshape(lead + (rows * 128,))
    return flat[..., :n].reshape(lead + tuple(shape))


def _reduce_scatter_layer(g_layer, ci):
    sends = [t.reshape((4, 2) + t.shape[-2:]) for t in g_layer]
    got = _swap_sibling(sends)
    chip_sums = [_add_own_half(s, r, ci) for s, r in zip(sends, got)]
    got = _exchange_chips(chip_sums)
    return [_sum_slots(t, tr=_row_tile(*t.shape[-2:]), name="sum_chips") for t in got]


def kernel(*args):
    names = (["x", "c"] + WEIGHTS + ["loss_target"] + ["m_" + n for n in WEIGHTS] + ["v_" + n for n in WEIGHTS])
    assert len(args) == len(names)
    a = dict(zip(names, args))
    xi, yi, ci = lax.axis_index("x"), lax.axis_index("y"), lax.axis_index("c")
    me = 4 * xi + 2 * yi + ci

    small_in = [a["c"], a["ln_g"], a["ln_b"], a["ssd_conv_w"], a["sconv_w"]]
    buf, offs = _pack_rows(small_in)
    got, = _all_gather([buf], in_vmem=True, name="gather_small")
    c_all = _take(got, offs[0], (D,))
    full = {}
    for k, n in enumerate(["ln_g", "ln_b", "ssd_conv_w", "sconv_w"]):
        sh = a[n].shape
        t = _take(got, offs[k + 1], sh)
        full[n] = jnp.transpose(t, (1, 2, 0, 3)).reshape(sh[0], sh[1], N_DEV * sh[2])

    ncol = a["ada_w"].shape[-1]
    ada_b_cols = lax.dynamic_slice_in_dim(a["ada_b"], me * ncol, ncol, axis=1)[:, None, :]
    mod_cols = _ada_fwd(c_all, a["ada_w"], ada_b_cols)
    got, = _all_gather([mod_cols.reshape(DEPTH * N_DEV, ncol)], in_vmem=True, name="gather_mod")
    got = got.reshape(N_DEV, DEPTH, N_DEV, ncol)
    mod = lax.dynamic_index_in_dim(got, me, axis=2, keepdims=False)
    mod = jnp.transpose(mod, (1, 0, 2)).reshape(DEPTH, 9, D)

    shards = {n: a[n] for n in BIG}
    shards["mix_w_in"] = _pack_cols(a["mix_w_in"])
    p = {n: [] for n in BIG}
    for l in range(DEPTH):
        got = _all_gather([shards[n][l].astype(bf16) for n in BIG], in_vmem=False, name="gather_weights")
        for n, t in zip(BIG, got):
            p[n].append(t if n in COL_SHARDED else t.reshape(-1, t.shape[-1]))
    for n in ("ln_in_g", "ln_in_b", "ssd_conv_b", "ssd_dt_bias", "ssd_a_log", "ssd_d", "ssd_norm_g", "fox_f_bias"):
        p[n] = a[n]
    p.update(full)
    p["mod"] = mod

    loss_local, grad_x, g = _local_step(a["x"][0], a["loss_target"][0], p)
    loss = lax.psum(loss_local, ("x", "y", "c"))

    small_names = ["mod", "ln_in_g", "ln_in_b", "ssd_conv_b", "ssd_dt_bias", "ssd_a_log", "ssd_d", "ssd_norm_g",
                   "fox_f_bias", "ln_g", "ln_b", "ssd_conv_w", "sconv_w"]
    buf, offs = _pack_rows([g[n] for n in small_names])
    got, = _all_gather([buf], in_vmem=True, name="gather_small_grads")
    tot = _sum_slots(got, tr=buf.shape[0], name="sum_small_grads")
    grads = {}
    for k, n in enumerate(small_names[1:], start=1):
        t = _take(tot, offs[k], g[n].shape)
        if n in SMALL_SHARDED:
            w_ = SMALL_SHARDED[n]
            t = lax.dynamic_slice_in_dim(t, me * w_, w_, axis=2)
        grads[n] = t
    grads["ada_b"] = _take(tot, offs[0], (DEPTH, 9 * D))
    dmod_all = _take(got, offs[0], (DEPTH, 9 * D))
    dmod_cols = jnp.transpose(lax.dynamic_slice_in_dim(dmod_all, me * ncol, ncol, axis=2), (1, 0, 2))
    grads["ada_w"] = _ada_bwd(c_all, dmod_cols)

    per_layer = []
    for l in reversed(range(DEPTH)):
        g_layer = [g[n][l].reshape((N_DEV,) + shards[n].shape[1:]) for n in BIG]
        per_layer.append(_reduce_scatter_layer(g_layer, ci))
    for k, n in enumerate(BIG):
        grads[n] = jnp.stack([per_layer[1][k], per_layer[0][k]])
    grads["mix_w_in"] = _unpack_cols(grads["mix_w_in"])

    delta, new_m, new_v = {}, {}, {}
    small_params = [n for n in WEIGHTS if n not in ADAM_TR]
    packs = [_pack_rows([src[pre + n] for n in small_params])[0]
             for src, pre in ((a, ""), (grads, ""), (a, "m_"), (a, "v_"))]
    _, offs = _pack_rows([a[n] for n in small_params])
    outs = _adamw(*packs, tr=packs[0].shape[0], name="adamw_small")
    for k, n in enumerate(small_params):
        delta[n], new_m[n], new_v[n] = (_take(t, offs[k], a[n].shape) for t in outs)
    for n, tr in ADAM_TR.items():
        sh = a[n].shape
        two = lambda t: t.reshape(-1, sh[-1])
        outs = _adamw(two(a[n]), two(grads[n]), two(a["m_" + n]), two(a["v_" + n]), tr=tr, name="adamw_" + n)
        delta[n], new_m[n], new_v[n] = (t.reshape(sh) for t in outs)

    return (loss, grad_x[None], *[grads[n] for n in WEIGHTS], *[delta[n] for n in WEIGHTS],
            *[new_m[n] for n in WEIGHTS], *[new_v[n] for n in WEIGHTS])
```

```python
import functools

import jax
import jax.numpy as jnp
from jax import lax
from jax.experimental import pallas as pl
from jax.experimental.pallas import tpu as pltpu

f32, bf16 = jnp.float32, jnp.bfloat16

D = 1024
F = 2816
DEPTH = 2
N_DEV = 8
SSD_W, SSD_HD, SSD_H, SSD_G, SSD_N, SSD_K = 512, 64, 8, 2, 128, 4
FOX_W, FOX_HD, FOX_H = 256, 64, 4
SC_W, SC_K = 256, 3
ALPHA = (2 * DEPTH) ** 0.25
LN_EPS = 1e-5
RMS_EPS = 1e-5
P_XBC, P_Z, P_Q, P_K, P_V, P_SB, P_SC, P_SX, P_SM = 0, 1024, 1536, 1792, 2048, 2304, 2560, 2816, 3072
P_W = 3200
SM_DT, SM_F = 0, 8
ADAM_LR, ADAM_B1, ADAM_B2, ADAM_EPS, ADAM_WD, ADAM_STEP = 0.001, 0.9, 0.999, 1e-08, 0.01, 10

VMEM_LIMIT = 56 * 1024 * 1024


def _cp(sem=None):
    return pltpu.CompilerParams(dimension_semantics=sem, vmem_limit_bytes=VMEM_LIMIT)


def _const_spec(shape):
    nd = len(shape)
    return pl.BlockSpec(shape, lambda *_: (0,) * nd, pipeline_mode=pl.Buffered(1))


def _sigmoid(x):
    return 1.0 / (1.0 + jnp.exp(-x))


def _ln_fwd(u, g, b):
    mu = jnp.mean(u, -1, keepdims=True)
    xc = u - mu
    rstd = lax.rsqrt(jnp.mean(xc * xc, -1, keepdims=True) + LN_EPS)
    xhat = xc * rstd
    return xhat * g + b, xhat, rstd


def _ln_bwd(dout, xhat, rstd, g):
    dxh = dout * g
    m1 = jnp.mean(dxh, -1, keepdims=True)
    m2 = jnp.mean(dxh * xhat, -1, keepdims=True)
    du = rstd * (dxh - m1 - xhat * m2)
    return du, jnp.sum(dout * xhat, 0, keepdims=True), jnp.sum(dout, 0, keepdims=True)


def _dot(a, b):
    return jnp.dot(a, b, preferred_element_type=f32)


def _dot_nt(a, b):
    return lax.dot_general(a, b, (((1,), (1,)), ((), ())), preferred_element_type=f32)


def _dot_tn(a, b):
    return lax.dot_general(a, b, (((0,), (0,)), ((), ())), preferred_element_type=f32)


def _dot_hi(a, b):
    return jnp.dot(a, b, preferred_element_type=f32, precision=lax.Precision.HIGHEST)


def _shift_down(cur, prev, s):
    if s == 0:
        return cur
    row = lax.broadcasted_iota(jnp.int32, cur.shape, 0)
    return jnp.where(row < s, pltpu.roll(prev, s, 0), pltpu.roll(cur, s, 0))


def _shift_up(cur, nxt, s):
    if s == 0:
        return cur
    t = cur.shape[0]
    row = lax.broadcasted_iota(jnp.int32, cur.shape, 0)
    return jnp.where(row >= t - s, pltpu.roll(nxt, t - s, 0), pltpu.roll(cur, t - s, 0))


def _ln_in_fwd(x, gb, *, tt=512):
    L = x.shape[0]

    def body(x_ref, gb_ref, o_ref):
        o_ref[...] = _ln_fwd(x_ref[...], gb_ref[0:1, :], gb_ref[1:2, :])[0]

    return pl.pallas_call(
        body, name="ln_in_fwd", grid=(L // tt,),
        in_specs=[pl.BlockSpec((tt, D), lambda i: (i, 0)), _const_spec((8, D))],
        out_specs=pl.BlockSpec((tt, D), lambda i: (i, 0)),
        out_shape=jax.ShapeDtypeStruct((L, D), f32), compiler_params=_cp(("parallel",)))(x, gb)


def _ln_in_bwd(x, dy, gb, *, tt=512):
    L = x.shape[0]

    def body(x_ref, dy_ref, gb_ref, dx_ref, acc_ref):
        @pl.when(pl.program_id(0) == 0)
        def _():
            acc_ref[...] = jnp.zeros_like(acc_ref)
        _, xhat, rstd = _ln_fwd(x_ref[...], gb_ref[0:1, :], gb_ref[1:2, :])
        du, dg, db = _ln_bwd(dy_ref[...], xhat, rstd, gb_ref[0:1, :])
        dx_ref[...] = du
        acc_ref[0:1, :] += dg
        acc_ref[1:2, :] += db

    return pl.pallas_call(
        body, name="ln_in_bwd", grid=(L // tt,),
        in_specs=[pl.BlockSpec((tt, D), lambda i: (i, 0)), pl.BlockSpec((tt, D), lambda i: (i, 0)), _const_spec((8, D))],
        out_specs=[pl.BlockSpec((tt, D), lambda i: (i, 0)), pl.BlockSpec((8, D), lambda i: (0, 0))],
        out_shape=[jax.ShapeDtypeStruct((L, D), f32), jax.ShapeDtypeStruct((8, D), f32)],
        compiler_params=_cp(("arbitrary",)))(x, dy, gb)


def _loss_head(y, tgt, *, tt=512):
    L = y.shape[0]

    def body(y_ref, t_ref, dy_ref, acc_ref):
        @pl.when(pl.program_id(0) == 0)
        def _():
            acc_ref[...] = jnp.zeros_like(acc_ref)
        e = y_ref[...] - t_ref[...]
        dy_ref[...] = e * (1.0 / D)
        acc_ref[...] += 0.5 * jnp.sum(jnp.mean(e * e, -1, keepdims=True))

    return pl.pallas_call(
        body, name="loss_head", grid=(L // tt,),
        in_specs=[pl.BlockSpec((tt, D), lambda i: (i, 0)), pl.BlockSpec((tt, D), lambda i: (i, 0))],
        out_specs=[pl.BlockSpec((tt, D), lambda i: (i, 0)), pl.BlockSpec((8, 128), lambda i: (0, 0))],
        out_shape=[jax.ShapeDtypeStruct((L, D), f32), jax.ShapeDtypeStruct((8, 128), f32)],
        compiler_params=_cp(("arbitrary",)))(y, tgt)


FFN_CH = 4
FS = F // FFN_CH


def _ffn_fwd(x, mv, w_in, w_out, *, tt=256):
    L = x.shape[0]

    def body(x_ref, mv_ref, wi_ref, wo_ref, o_ref):
        x = x_ref[...]
        h = (x * (1.0 + mv_ref[1:2, :]) + mv_ref[0:1, :]).astype(bf16)
        y = jnp.zeros((tt, D), f32)
        for c in range(FFN_CH):
            g = _dot(h, wi_ref[c])
            u = _dot(h, wi_ref[c + FFN_CH])
            act = (g * _sigmoid(g) * u).astype(bf16)
            y = y + _dot(act, wo_ref[c * FS:(c + 1) * FS, :])
        uu = ALPHA * x + (0.5 * mv_ref[2:3, :]) * y
        o_ref[...] = _ln_fwd(uu, mv_ref[3:4, :], mv_ref[4:5, :])[0]

    return pl.pallas_call(
        body, name="ffn_fwd", grid=(L // tt,),
        in_specs=[pl.BlockSpec((tt, D), lambda i: (i, 0)), _const_spec((8, D)),
                  _const_spec((2 * FFN_CH, D, FS)), _const_spec((F, D))],
        out_specs=pl.BlockSpec((tt, D), lambda i: (i, 0)),
        out_shape=jax.ShapeDtypeStruct((L, D), f32), compiler_params=_cp(("parallel",)))(x, mv, w_in, w_out)


def _ffn_bwd(x, dxo, mv, w_in, w_out, *, tt=256):
    L = x.shape[0]

    def body(x_ref, dxo_ref, mv_ref, wi_ref, wo_ref, dx_ref, h_ref, da_ref, act_ref, dy_ref, acc_ref, a_scr):
        @pl.when(pl.program_id(0) == 0)
        def _():
            acc_ref[...] = jnp.zeros_like(acc_ref)
        x = x_ref[...]
        scale1 = 1.0 + mv_ref[1:2, :]
        h = (x * scale1 + mv_ref[0:1, :]).astype(bf16)
        h_ref[...] = h
        y = jnp.zeros((tt, D), f32)
        for c in range(FFN_CH):
            g = _dot(h, wi_ref[c])
            u = _dot(h, wi_ref[c + FFN_CH])
            a_scr[c] = g
            a_scr[c + FFN_CH] = u
            act = (g * _sigmoid(g) * u).astype(bf16)
            act_ref[c] = act
            y = y + _dot(act, wo_ref[c * FS:(c + 1) * FS, :])
        hg = 0.5 * mv_ref[2:3, :]
        _, xhat, rstd = _ln_fwd(ALPHA * x + hg * y, mv_ref[3:4, :], mv_ref[4:5, :])
        du, dlg, dlb = _ln_bwd(dxo_ref[...], xhat, rstd, mv_ref[3:4, :])
        acc_ref[3:4, :] += dlg
        acc_ref[4:5, :] += dlb
        acc_ref[2:3, :] += jnp.sum(0.5 * y * du, 0, keepdims=True)
        dyb = (hg * du).astype(bf16)
        dy_ref[...] = dyb
        dh = jnp.zeros((tt, D), f32)
        for c in range(FFN_CH):
            g = a_scr[c]
            u = a_scr[c + FFN_CH]
            dact = _dot_nt(dyb, wo_ref[c * FS:(c + 1) * FS, :])
            s = _sigmoid(g)
            dg = (dact * u * (s * (1.0 + g * (1.0 - s)))).astype(bf16)
            dup = (dact * (g * s)).astype(bf16)
            da_ref[c] = dg
            da_ref[c + FFN_CH] = dup
            dh = dh + _dot_nt(dg, wi_ref[c])
            dh = dh + _dot_nt(dup, wi_ref[c + FFN_CH])
        dx_ref[...] = ALPHA * du + dh * scale1
        acc_ref[0:1, :] += jnp.sum(dh, 0, keepdims=True)
        acc_ref[1:2, :] += jnp.sum(dh * x, 0, keepdims=True)

    tok = lambda w: pl.BlockSpec((tt, w), lambda i: (i, 0))
    by_chunk = lambda n: pl.BlockSpec((n, tt, FS), lambda i: (0, i, 0))
    return pl.pallas_call(
        body, name="ffn_bwd", grid=(L // tt,),
        in_specs=[tok(D), tok(D), _const_spec((8, D)), _const_spec((2 * FFN_CH, D, FS)), _const_spec((F, D))],
        out_specs=[tok(D), tok(D), by_chunk(2 * FFN_CH), by_chunk(FFN_CH), tok(D), pl.BlockSpec((8, D), lambda i: (0, 0))],
        out_shape=[jax.ShapeDtypeStruct((L, D), f32), jax.ShapeDtypeStruct((L, D), bf16),
                   jax.ShapeDtypeStruct((2 * FFN_CH, L, FS), bf16), jax.ShapeDtypeStruct((FFN_CH, L, FS), bf16),
                   jax.ShapeDtypeStruct((L, D), bf16), jax.ShapeDtypeStruct((8, D), f32)],
        scratch_shapes=[pltpu.VMEM((2 * FFN_CH, tt, FS), f32)],
        compiler_params=_cp(("arbitrary",)))(x, dxo, mv, w_in, w_out)


def _matmul_tn(a, b, *, tn, tk, name):
    ga, gb = a.ndim == 3, b.ndim == 3
    G = a.shape[0] if ga else (b.shape[0] if gb else 1)
    K, M = a.shape[-2:]
    N = b.shape[-1]
    nk = K // tk

    def body(a_ref, b_ref, o_ref, acc):
        k = pl.program_id(2)
        p = _dot_tn(a_ref[...], b_ref[...])

        @pl.when(k == 0)
        def _():
            acc[...] = p

        @pl.when(k > 0)
        def _():
            acc[...] += p

        @pl.when(k == nk - 1)
        def _():
            o_ref[...] = acc[...].astype(bf16)

    a_spec = (pl.BlockSpec((None, tk, M), lambda g, j, k: (g, k, 0)) if ga
              else pl.BlockSpec((tk, M), lambda g, j, k: (k, 0)))
    b_spec = (pl.BlockSpec((None, tk, tn), lambda g, j, k: (g, k, j)) if gb
              else pl.BlockSpec((tk, tn), lambda g, j, k: (k, j)))
    if ga or gb:
        o_spec, o_shape = pl.BlockSpec((None, M, tn), lambda g, j, k: (g, 0, j)), (G, M, N)
    else:
        o_spec, o_shape = pl.BlockSpec((M, tn), lambda g, j, k: (0, j)), (M, N)
    return pl.pallas_call(
        body, name=name, grid=(G, N // tn, nk), in_specs=[a_spec, b_spec], out_specs=o_spec,
        out_shape=jax.ShapeDtypeStruct(o_shape, bf16), scratch_shapes=[pltpu.VMEM((M, tn), f32)],
        compiler_params=_cp(("parallel", "parallel", "arbitrary")))(a, b)


def _inproj_fwd(x, mv, w, *, tt=512):
    L = x.shape[0]

    def body(x_ref, mv_ref, w_ref, o_ref):
        h = (x_ref[...] * (1.0 + mv_ref[1:2, :]) + mv_ref[0:1, :]).astype(bf16)
        o_ref[...] = _dot(h, w_ref[...])

    return pl.pallas_call(
        body, name="inproj_fwd", grid=(L // tt,),
        in_specs=[pl.BlockSpec((tt, D), lambda i: (i, 0)), _const_spec((8, D)), _const_spec((D, P_W))],
        out_specs=pl.BlockSpec((tt, P_W), lambda i: (i, 0)),
        out_shape=jax.ShapeDtypeStruct((L, P_W), f32), compiler_params=_cp(("parallel",)))(x, mv, w)


def _inproj_bwd(x, dx_part, dproj, mv, w, *, tt=512):
    L = x.shape[0]

    def body(x_ref, dxp_ref, dp_ref, mv_ref, w_ref, dx_ref, h_ref, acc_ref):
        @pl.when(pl.program_id(0) == 0)
        def _():
            acc_ref[...] = jnp.zeros_like(acc_ref)
        x = x_ref[...]
        scale1 = 1.0 + mv_ref[1:2, :]
        h_ref[...] = (x * scale1 + mv_ref[0:1, :]).astype(bf16)
        dh = _dot_nt(dp_ref[...], w_ref[...])
        dx_ref[...] = dxp_ref[...] + dh * scale1
        acc_ref[0:1, :] += jnp.sum(dh, 0, keepdims=True)
        acc_ref[1:2, :] += jnp.sum(dh * x, 0, keepdims=True)

    tok = lambda w_: pl.BlockSpec((tt, w_), lambda i: (i, 0))
    return pl.pallas_call(
        body, name="inproj_bwd", grid=(L // tt,),
        in_specs=[tok(D), tok(D), tok(P_W), _const_spec((8, D)), _const_spec((D, P_W))],
        out_specs=[tok(D), tok(D), pl.BlockSpec((8, D), lambda i: (0, 0))],
        out_shape=[jax.ShapeDtypeStruct((L, D), f32), jax.ShapeDtypeStruct((L, D), bf16),
                   jax.ShapeDtypeStruct((8, D), f32)],
        compiler_params=_cp(("arbitrary",)))(x, dx_part, dproj, mv, w)


def _outproj_fwd(x, ycat, mv, w, *, tt=512):
    L = x.shape[0]

    def body(x_ref, y_ref, mv_ref, w_ref, o_ref):
        y = _dot(y_ref[...], w_ref[...])
        uu = ALPHA * x_ref[...] + mv_ref[2:3, :] * y
        o_ref[...] = _ln_fwd(uu, mv_ref[3:4, :], mv_ref[4:5, :])[0]

    tok = lambda w_: pl.BlockSpec((tt, w_), lambda i: (i, 0))
    return pl.pallas_call(
        body, name="outproj_fwd", grid=(L // tt,),
        in_specs=[tok(D), tok(D), _const_spec((8, D)), _const_spec((D, D))],
        out_specs=tok(D),
        out_shape=jax.ShapeDtypeStruct((L, D), f32), compiler_params=_cp(("parallel",)))(x, ycat, mv, w)


def _outproj_bwd(x, ycat, dxo, mv, w, *, tt=512):
    L = x.shape[0]

    def body(x_ref, y_ref, dxo_ref, mv_ref, w_ref, dx_ref, dy_ref, dyc_ref, acc_ref):
        @pl.when(pl.program_id(0) == 0)
        def _():
            acc_ref[...] = jnp.zeros_like(acc_ref)
        y = _dot(y_ref[...], w_ref[...])
        gate = mv_ref[2:3, :]
        _, xhat, rstd = _ln_fwd(ALPHA * x_ref[...] + gate * y, mv_ref[3:4, :], mv_ref[4:5, :])
        du, dlg, dlb = _ln_bwd(dxo_ref[...], xhat, rstd, mv_ref[3:4, :])
        acc_ref[3:4, :] += dlg
        acc_ref[4:5, :] += dlb
        acc_ref[2:3, :] += jnp.sum(y * du, 0, keepdims=True)
        dx_ref[...] = ALPHA * du
        dyb = (gate * du).astype(bf16)
        dy_ref[...] = dyb
        dyc_ref[...] = _dot_nt(dyb, w_ref[...])

    tok = lambda w_: pl.BlockSpec((tt, w_), lambda i: (i, 0))
    return pl.pallas_call(
        body, name="outproj_bwd", grid=(L // tt,),
        in_specs=[tok(D), tok(D), tok(D), _const_spec((8, D)), _const_spec((D, D))],
        out_specs=[tok(D), tok(D), tok(D), pl.BlockSpec((8, D), lambda i: (0, 0))],
        out_shape=[jax.ShapeDtypeStruct((L, D), f32), jax.ShapeDtypeStruct((L, D), bf16),
                   jax.ShapeDtypeStruct((L, D), f32), jax.ShapeDtypeStruct((8, D), f32)],
        compiler_params=_cp(("arbitrary",)))(x, ycat, dxo, mv, w)


def _sconv_fwd(proj, w, *, tt=512):
    L = proj.shape[0]
    cb = SC_W

    def body(b_ref, c_ref, x_ref, cp_ref, xp_ref, w_ref, o_ref):
        first = jnp.where(pl.program_id(0) > 0, 1.0, 0.0)
        u = c_ref[...] * x_ref[...]
        up = cp_ref[...] * xp_ref[...] * first
        v = w_ref[2:3, :] * u + w_ref[1:2, :] * _shift_down(u, up, 1) + w_ref[0:1, :] * _shift_down(u, up, 2)
        o_ref[...] = (b_ref[...] * v).astype(bf16)

    cur = lambda col: pl.BlockSpec((tt, cb), lambda i: (i, col // cb))
    prev = lambda col: pl.BlockSpec((tt, cb), lambda i: (jnp.maximum(i - 1, 0), col // cb))
    return pl.pallas_call(
        body, name="sconv_fwd", grid=(L // tt,),
        in_specs=[cur(P_SB), cur(P_SC), cur(P_SX), prev(P_SC), prev(P_SX), _const_spec((8, cb))],
        out_specs=pl.BlockSpec((tt, cb), lambda i: (i, 0)),
        out_shape=jax.ShapeDtypeStruct((L, cb), bf16), compiler_params=_cp(("parallel",)))(proj, proj, proj, proj, proj, w)


def _sconv_bwd(proj, dycat, w, *, tt=512):
    L = proj.shape[0]
    cb = SC_W
    n = L // tt

    def body(b_ref, c_ref, x_ref, cp_ref, xp_ref, bn_ref, dy_ref, dyn_ref, w_ref, db_ref, dc_ref, dx_ref, acc_ref):
        i = pl.program_id(0)

        @pl.when(i == 0)
        def _():
            acc_ref[...] = jnp.zeros_like(acc_ref)
        first = jnp.where(i > 0, 1.0, 0.0)
        last = jnp.where(i < n - 1, 1.0, 0.0)
        cg, xin, bg = c_ref[...], x_ref[...], b_ref[...]
        u = cg * xin
        up = cp_ref[...] * xp_ref[...] * first
        u1, u2 = _shift_down(u, up, 1), _shift_down(u, up, 2)
        v = w_ref[2:3, :] * u + w_ref[1:2, :] * u1 + w_ref[0:1, :] * u2
        dy = dy_ref[...]
        db_ref[...] = (dy * v).astype(bf16)
        dv = dy * bg
        dvn = dyn_ref[...] * bn_ref[...] * last
        du = w_ref[2:3, :] * dv + w_ref[1:2, :] * _shift_up(dv, dvn, 1) + w_ref[0:1, :] * _shift_up(dv, dvn, 2)
        acc_ref[2:3, :] += jnp.sum(dv * u, 0, keepdims=True)
        acc_ref[1:2, :] += jnp.sum(dv * u1, 0, keepdims=True)
        acc_ref[0:1, :] += jnp.sum(dv * u2, 0, keepdims=True)
        dc_ref[...] = (du * xin).astype(bf16)
        dx_ref[...] = (du * cg).astype(bf16)

    cur = lambda col: pl.BlockSpec((tt, cb), lambda i: (i, col // cb))
    prev = lambda col: pl.BlockSpec((tt, cb), lambda i: (jnp.maximum(i - 1, 0), col // cb))
    nxt = lambda col: pl.BlockSpec((tt, cb), lambda i: (jnp.minimum(i + 1, n - 1), col // cb))
    ycol = SSD_W + FOX_W
    out = pl.BlockSpec((tt, cb), lambda i: (i, 0))
    return pl.pallas_call(
        body, name="sconv_bwd", grid=(n,),
        in_specs=[cur(P_SB), cur(P_SC), cur(P_SX), prev(P_SC), prev(P_SX), nxt(P_SB), cur(ycol), nxt(ycol),
                  _const_spec((8, cb))],
        out_specs=[out, out, out, pl.BlockSpec((8, cb), lambda i: (0, 0))],
        out_shape=[jax.ShapeDtypeStruct((L, cb), bf16)] * 3 + [jax.ShapeDtypeStruct((8, cb), f32)],
        compiler_params=_cp(("arbitrary",)))(proj, proj, proj, proj, proj, proj, dycat, dycat, w)


def _log1pexp(x):
    return jnp.log(1.0 + jnp.exp(-jnp.abs(x)))


def _fox_gate_fwd(proj, hp, *, tt=256):
    L = proj.shape[0]

    def body(sm_ref, hp_ref, cum_ref, cumt_ref, carry):
        @pl.when(pl.program_id(0) == 0)
        def _():
            carry[...] = jnp.zeros_like(carry)
        xx = sm_ref[...] + hp_ref[3:4, :]
        logf = jnp.minimum(xx, 0.0) - _log1pexp(xx)
        r = lax.broadcasted_iota(jnp.int32, (tt, tt), 0)
        c = lax.broadcasted_iota(jnp.int32, (tt, tt), 1)
        cum = _dot_hi(jnp.where(r >= c, 1.0, 0.0), logf) + carry[0:1, :]
        cum_ref[...] = cum
        cumt_ref[...] = cum.T
        carry[0:1, :] = cum[tt - 1:tt, :]

    return pl.pallas_call(
        body, name="fox_gate_fwd", grid=(L // tt,),
        in_specs=[pl.BlockSpec((tt, 128), lambda i: (i, P_SM // 128)), _const_spec((8, 128))],
        out_specs=[pl.BlockSpec((tt, 128), lambda i: (i, 0)), pl.BlockSpec((128, tt), lambda i: (0, i))],
        out_shape=[jax.ShapeDtypeStruct((L, 128), f32), jax.ShapeDtypeStruct((128, L), f32)],
        scratch_shapes=[pltpu.VMEM((8, 128), f32)],
        compiler_params=_cp(("arbitrary",)))(proj, hp)


def _fox_gate_bwd(dcumt, proj, ddt, hp, *, tt=256):
    L = proj.shape[0]
    n = L // tt

    def body(dct_ref, sm_ref, ddt_ref, hp_ref, dsm_ref, acc_ref, carry):
        @pl.when(pl.program_id(0) == 0)
        def _():
            carry[...] = jnp.zeros_like(carry)
            acc_ref[...] = jnp.zeros_like(acc_ref)
        dc = dct_ref[...].T
        r = lax.broadcasted_iota(jnp.int32, (tt, tt), 0)
        c = lax.broadcasted_iota(jnp.int32, (tt, tt), 1)
        dl = _dot_hi(jnp.where(r <= c, 1.0, 0.0), dc) + carry[0:1, :]
        carry[0:1, :] += jnp.sum(dc, 0, keepdims=True)
        xx = sm_ref[...] + hp_ref[3:4, :]
        lane = lax.broadcasted_iota(jnp.int32, (tt, 128), 1)
        dlogit = jnp.where((lane >= SM_F) & (lane < SM_F + FOX_H), dl * _sigmoid(-xx), 0.0)
        acc_ref[3:4, :] += jnp.sum(dlogit, 0, keepdims=True)
        dsm_ref[...] = (dlogit + ddt_ref[...]).astype(bf16)

    return pl.pallas_call(
        body, name="fox_gate_bwd", grid=(n,),
        in_specs=[pl.BlockSpec((128, tt), lambda i: (0, n - 1 - i)),
                  pl.BlockSpec((tt, 128), lambda i: (n - 1 - i, P_SM // 128)),
                  pl.BlockSpec((tt, 128), lambda i: (n - 1 - i, 0)), _const_spec((8, 128))],
        out_specs=[pl.BlockSpec((tt, 128), lambda i: (n - 1 - i, 0)), pl.BlockSpec((8, 128), lambda i: (0, 0))],
        out_shape=[jax.ShapeDtypeStruct((L, 128), bf16), jax.ShapeDtypeStruct((8, 128), f32)],
        scratch_shapes=[pltpu.VMEM((8, 128), f32)],
        compiler_params=_cp(("arbitrary",)))(dcumt, proj, ddt, hp)


NEG = -1e30
FOX_SCALE = FOX_HD ** -0.5


def _fox_fwd(proj, cum, cumt, *, tq=256):
    L = proj.shape[0]
    heads = [slice(h * FOX_HD, (h + 1) * FOX_HD) for h in range(FOX_H)]

    def body(q_ref, k_ref, v_ref, cq_ref, ct_ref, o_ref, oa_ref, lse_ref):
        i = pl.program_id(0)
        row = lax.broadcasted_iota(jnp.int32, (tq, tq), 0)
        col = lax.broadcasted_iota(jnp.int32, (tq, tq), 1)
        diag_bias = jnp.where(row >= col, 0.0, NEG)
        qs = [(q_ref[:, hs] * FOX_SCALE).astype(bf16) for hs in heads]
        cqs = [cq_ref[:, SM_F + h:SM_F + h + 1] for h in range(FOX_H)]

        def block(r0, carry, bias):
            out = []
            for h, hs in enumerate(heads):
                m, l, acc, acc_lo = carry[h]
                kj = k_ref[pl.ds(r0, tq), hs].astype(bf16)
                vj = v_ref[pl.ds(r0, tq), hs].astype(bf16)
                s = _dot_nt(qs[h], kj) + (cqs[h] - ct_ref[SM_F + h:SM_F + h + 1, pl.ds(r0, tq)])
                if bias is not None:
                    s = s + bias
                m_new = jnp.maximum(m, jnp.max(s, -1, keepdims=True))
                p = jnp.exp(s - m_new)
                al = jnp.exp(m - m_new)
                pb = p.astype(bf16)
                p_lo = (p - pb.astype(f32)).astype(bf16)
                out.append((m_new, al * l + jnp.sum(p, -1, keepdims=True),
                            al * acc + _dot(pb, vj), al * acc_lo + _dot(p_lo, vj)))
            return tuple(out)

        init = tuple((jnp.full((tq, 1), NEG, f32), jnp.zeros((tq, 1), f32), jnp.zeros((tq, FOX_HD), f32),
                      jnp.zeros((tq, FOX_HD), f32)) for _ in heads)
        carry = lax.fori_loop(0, i, lambda j, c: block(pl.multiple_of(j * tq, tq), c, None), init)
        carry = block(pl.multiple_of(i * tq, tq), carry, diag_bias)
        lane = lax.broadcasted_iota(jnp.int32, (tq, 128), 1)
        lse_all = jnp.zeros((tq, 128), f32)
        for h, hs in enumerate(heads):
            m, l, acc, acc_lo = carry[h]
            inv = 1.0 / l
            o_ref[:, hs] = acc * inv
            oa_ref[:, hs] = (acc + acc_lo) * inv
            lse_all = jnp.where(lane == h, m + jnp.log(l), lse_all)
        lse_ref[...] = lse_all

    return pl.pallas_call(
        body, name="fox_fwd", grid=(L // tq,),
        in_specs=[pl.BlockSpec((tq, FOX_W), lambda i: (i, P_Q // FOX_W)),
                  pl.BlockSpec((L, FOX_W), lambda i: (0, P_K // FOX_W), pipeline_mode=pl.Buffered(1)),
                  pl.BlockSpec((L, FOX_W), lambda i: (0, P_V // FOX_W), pipeline_mode=pl.Buffered(1)),
                  pl.BlockSpec((tq, 128), lambda i: (i, 0)), _const_spec((128, L))],
        out_specs=[pl.BlockSpec((tq, FOX_W), lambda i: (i, 0)), pl.BlockSpec((tq, FOX_W), lambda i: (i, 0)),
                   pl.BlockSpec((tq, 128), lambda i: (i, 0))],
        out_shape=[jax.ShapeDtypeStruct((L, FOX_W), f32), jax.ShapeDtypeStruct((L, FOX_W), f32),
                   jax.ShapeDtypeStruct((L, 128), f32)],
        compiler_params=_cp(("parallel",)))(proj, proj, proj, cum, cumt)


def _fox_bwd(proj, dycat, o_acc, lse, cum, cumt, *, tq=256):
    L = proj.shape[0]
    nq = L // tq
    heads = [slice(h * FOX_HD, (h + 1) * FOX_HD) for h in range(FOX_H)]

    def body(q_ref, k_ref, v_ref, do_ref, oa_ref, lse_ref, cum_ref, ct_ref, dq_ref, dk_ref, dv_ref, dct_ref):
        j = pl.program_id(0)

        @pl.when(j == 0)
        def _():
            dq_ref[...] = jnp.zeros_like(dq_ref)
        row = lax.broadcasted_iota(jnp.int32, (tq, tq), 0)
        col = lax.broadcasted_iota(jnp.int32, (tq, tq), 1)
        diag_bias = jnp.where(row >= col, 0.0, NEG)
        kjs = [k_ref[:, hs].astype(bf16) for hs in heads]
        vjs = [v_ref[:, hs].astype(bf16) for hs in heads]
        cks = [ct_ref[SM_F + h:SM_F + h + 1, :] for h in range(FOX_H)]

        def block(r0, carry, bias):
            out = []
            for h, hs in enumerate(heads):
                dk, dv, dc = carry[h]
                qi = (q_ref[pl.ds(r0, tq), hs] * FOX_SCALE).astype(bf16)
                dob = do_ref[pl.ds(r0, tq), hs].astype(bf16)
                delta = jnp.sum(dob.astype(f32) * oa_ref[pl.ds(r0, tq), hs], -1, keepdims=True)
                s = _dot_nt(qi, kjs[h]) + (cum_ref[pl.ds(r0, tq), SM_F + h:SM_F + h + 1] - cks[h])
                if bias is not None:
                    s = s + bias
                p = jnp.exp(s - lse_ref[pl.ds(r0, tq), h:h + 1])
                ds = p * (_dot_nt(dob, vjs[h]) - delta)
                dsb = ds.astype(bf16)
                dq_ref[pl.ds(r0, tq), hs] += _dot(dsb, kjs[h]) * FOX_SCALE
                out.append((dk + _dot_tn(dsb, qi), dv + _dot_tn(p.astype(bf16), dob),
                            dc - jnp.sum(ds, 0, keepdims=True)))
            return tuple(out)

        init = tuple((jnp.zeros((tq, FOX_HD), f32), jnp.zeros((tq, FOX_HD), f32), jnp.zeros((1, tq), f32))
                     for _ in heads)
        carry = block(pl.multiple_of(j * tq, tq), init, diag_bias)
        carry = lax.fori_loop(j + 1, nq, lambda i, c: block(pl.multiple_of(i * tq, tq), c, None), carry)
        sub = lax.broadcasted_iota(jnp.int32, (128, tq), 0)
        dct = jnp.zeros((128, tq), f32)
        for h, hs in enumerate(heads):
            dk, dv, dc = carry[h]
            dk_ref[:, hs] = dk
            dv_ref[:, hs] = dv
            dct = jnp.where(sub == SM_F + h, dc, dct)
        dct_ref[...] = dct

    full = lambda w_, col: pl.BlockSpec((L, w_), lambda j: (0, col // w_), pipeline_mode=pl.Buffered(1))
    blk = lambda col: pl.BlockSpec((tq, FOX_W), lambda j: (j, col // FOX_W))
    return pl.pallas_call(
        body, name="fox_bwd", grid=(nq,),
        in_specs=[full(FOX_W, P_Q), blk(P_K), blk(P_V), full(FOX_W, SSD_W), full(FOX_W, 0), full(128, 0), full(128, 0),
                  pl.BlockSpec((128, tq), lambda j: (0, j))],
        out_specs=[pl.BlockSpec((L, FOX_W), lambda j: (0, 0)), pl.BlockSpec((tq, FOX_W), lambda j: (j, 0)),
                   pl.BlockSpec((tq, FOX_W), lambda j: (j, 0)), pl.BlockSpec((128, tq), lambda j: (0, j))],
        out_shape=[jax.ShapeDtypeStruct((L, FOX_W), f32)] * 3 + [jax.ShapeDtypeStruct((128, L), f32)],
        compiler_params=_cp(("arbitrary",)))(proj, proj, proj, dycat, o_acc, lse, cum, cumt)


HL = 128
AW = FOX_H * HL


def _np_place(rows, cols, pairs, dtype):
    import numpy as np
    m = np.zeros((rows, cols), np.float32)
    for r, c in pairs:
        m[r, c] = 1.0
    return jnp.asarray(m, dtype)


def _fox_consts():
    data = [(h * FOX_HD + d, h * HL + d) for h in range(FOX_H) for d in range(FOX_HD)]
    return dict(
        pq=_np_place(FOX_W, AW, data, bf16),
        pqt=_np_place(AW, FOX_W, [(c, r) for r, c in data], bf16),
        cum_a=[_np_place(128, AW, [(SM_F + h, h * HL + 64 + r) for h in range(FOX_H)], bf16) for r in range(3)],
        head_a=[_np_place(128, AW, [(h, h * HL + 64 + r) for h in range(FOX_H)], bf16) for r in range(3)],
        head_b=[_np_place(128, AW, [(h, h * HL + 67 + r) for h in range(FOX_H)], bf16) for r in range(3)],
        group=_np_place(FOX_W, 128, [(h * FOX_HD + d, h) for h in range(FOX_H) for d in range(FOX_HD)], f32),
        col_a=_np_place(AW, 128, [(h * HL + 64, SM_F + h) for h in range(FOX_H)], f32))


def _split3(x):
    hi = x.astype(bf16)
    r1 = x - hi.astype(f32)
    mid = r1.astype(bf16)
    return hi, mid, (r1 - mid.astype(f32)).astype(bf16)


def _slot_ones(tt, first):
    lane = lax.broadcasted_iota(jnp.int32, (tt, AW), 1) % HL
    return jnp.where((lane >= first) & (lane < first + 3), 1.0, 0.0)


def _fox_prep(proj, hp, cst, *, tt=256):
    L = proj.shape[0]

    def body(q_ref, k_ref, v_ref, sm_ref, hp_ref, pq_ref, c0_ref, c1_ref, c2_ref, qa_ref, ka_ref, va_ref, carry):
        @pl.when(pl.program_id(0) == 0)
        def _():
            carry[...] = jnp.zeros_like(carry)
        xx = sm_ref[...] + hp_ref[3:4, :]
        logf = jnp.minimum(xx, 0.0) - _log1pexp(xx)
        r = lax.broadcasted_iota(jnp.int32, (tt, tt), 0)
        c = lax.broadcasted_iota(jnp.int32, (tt, tt), 1)
        cum = _dot_hi(jnp.where(r >= c, 1.0, 0.0), logf) + carry[0:1, :]
        carry[0:1, :] = cum[tt - 1:tt, :]
        parts = _split3(-cum)
        pq = pq_ref[...]
        a_ones, b_ones = _slot_ones(tt, 64), _slot_ones(tt, 67)
        qa_ref[...] = (_dot((q_ref[...] * FOX_SCALE).astype(bf16), pq) + a_ones).astype(bf16)
        ka = _dot(k_ref[...].astype(bf16), pq) + b_ones
        for part, c_ref in zip(parts, (c0_ref, c1_ref, c2_ref)):
            ka = ka + _dot(part, c_ref[...])
        ka_ref[...] = ka.astype(bf16)
        va_ref[...] = (_dot(v_ref[...].astype(bf16), pq) + a_ones).astype(bf16)

    col = lambda c_: pl.BlockSpec((tt, FOX_W), lambda i: (i, c_ // FOX_W))
    out = pl.BlockSpec((tt, AW), lambda i: (i, 0))
    return pl.pallas_call(
        body, name="fox_prep", grid=(L // tt,),
        in_specs=[col(P_Q), col(P_K), col(P_V), pl.BlockSpec((tt, 128), lambda i: (i, P_SM // 128)),
                  _const_spec((8, 128)), _const_spec((FOX_W, AW))] + [_const_spec((128, AW))] * 3,
        out_specs=[out, out, out], out_shape=[jax.ShapeDtypeStruct((L, AW), bf16)] * 3,
        scratch_shapes=[pltpu.VMEM((8, 128), f32)],
        compiler_params=_cp(("arbitrary",)))(proj, proj, proj, proj, hp, cst["pq"], *cst["cum_a"])


def _fox_attn_fwd(qa, ka, va, *, tq=256):
    L = qa.shape[0]

    def body(qa_ref, ka_ref, va_ref, o_ref, oa_ref, lse_ref):
        i = pl.program_id(0)
        row = lax.broadcasted_iota(jnp.int32, (tq, tq), 0)
        col = lax.broadcasted_iota(jnp.int32, (tq, tq), 1)
        diag_bias = jnp.where(row >= col, 0.0, NEG)

        def block(r0, carry, bias):
            out = []
            for h in range(FOX_H):
                hl = slice(h * HL, (h + 1) * HL)
                m, acc = carry[h]
                s = _dot_nt(qa_ref[:, hl], ka_ref[pl.ds(r0, tq), hl])
                if bias is not None:
                    s = s + bias
                m_new = jnp.maximum(m, jnp.max(s, -1, keepdims=True))
                p = jnp.exp(s - m_new)
                al = jnp.exp(m - m_new)
                pb = p.astype(bf16)
                p_lo = (p - pb.astype(f32)).astype(bf16)
                vj = va_ref[pl.ds(r0, tq), hl]
                out.append((m_new, al * acc + (_dot(pb, vj) + _dot(p_lo, vj))))
            return tuple(out)

        init = tuple((jnp.full((tq, 1), NEG, f32), jnp.zeros((tq, HL), f32)) for _ in range(FOX_H))
        carry = lax.fori_loop(0, i, lambda j, c: block(pl.multiple_of(j * tq, tq), c, None), init)
        carry = block(pl.multiple_of(i * tq, tq), carry, diag_bias)
        lane = lax.broadcasted_iota(jnp.int32, (tq, 128), 1)
        lse_all = jnp.zeros((tq, 128), f32)
        for h in range(FOX_H):
            hs = slice(h * FOX_HD, (h + 1) * FOX_HD)
            m, acc = carry[h]
            l = acc[:, FOX_HD:FOX_HD + 1]
            o = acc[:, :FOX_HD] * (1.0 / l)
            o_ref[:, hs] = o.astype(bf16)
            oa_ref[:, hs] = o
            lse_all = jnp.where(lane == h, m + jnp.log(l), lse_all)
        lse_ref[...] = lse_all

    full = pl.BlockSpec((L, AW), lambda i: (0, 0), pipeline_mode=pl.Buffered(1))
    return pl.pallas_call(
        body, name="fox_fwd", grid=(L // tq,),
        in_specs=[pl.BlockSpec((tq, AW), lambda i: (i, 0)), full, full],
        out_specs=[pl.BlockSpec((tq, FOX_W), lambda i: (i, 0)), pl.BlockSpec((tq, FOX_W), lambda i: (i, 0)),
                   pl.BlockSpec((tq, 128), lambda i: (i, 0))],
        out_shape=[jax.ShapeDtypeStruct((L, FOX_W), bf16), jax.ShapeDtypeStruct((L, FOX_W), f32),
                   jax.ShapeDtypeStruct((L, 128), f32)],
        compiler_params=_cp(("parallel",)))(qa, ka, va)


def _fox_bprep(qa, dycat, o_acc, lse, cst, *, tt=256):
    L = qa.shape[0]

    def body(qa_ref, do_ref, oa_ref, lse_ref, pq_ref, g_ref, a0, a1, a2, b0, b1, b2, qb_ref, doa_ref, qbt_ref, doat_ref):
        dob = do_ref[...].astype(bf16)
        delta = _dot_hi(dob.astype(f32) * oa_ref[...], g_ref[...])
        doa = _dot(dob, pq_ref[...])
        for part, ref in zip(_split3(-delta), (a0, a1, a2)):
            doa = doa + _dot(part, ref[...])
        qb = qa_ref[...].astype(f32)
        for part, ref in zip(_split3(-lse_ref[...]), (b0, b1, b2)):
            qb = qb + _dot(part, ref[...])
        doa, qb = doa.astype(bf16), qb.astype(bf16)
        doa_ref[...] = doa
        qb_ref[...] = qb
        doat_ref[...] = doa.T
        qbt_ref[...] = qb.T

    tok = lambda w_: pl.BlockSpec((tt, w_), lambda i: (i, 0))
    tr = pl.BlockSpec((AW, tt), lambda i: (0, i))
    return pl.pallas_call(
        body, name="fox_bprep", grid=(L // tt,),
        in_specs=[tok(AW), pl.BlockSpec((tt, FOX_W), lambda i: (i, SSD_W // FOX_W)), tok(FOX_W), tok(128),
                  _const_spec((FOX_W, AW)), _const_spec((FOX_W, 128))] + [_const_spec((128, AW))] * 6,
        out_specs=[tok(AW), tok(AW), tr, tr],
        out_shape=[jax.ShapeDtypeStruct((L, AW), bf16)] * 2 + [jax.ShapeDtypeStruct((AW, L), bf16)] * 2,
        compiler_params=_cp(("parallel",)))(qa, dycat, o_acc, lse, cst["pq"], cst["group"], *cst["head_a"], *cst["head_b"])


def _fox_attn_bwd(ka, va, qb, doa, qbt, doat, *, tq=256):
    L = ka.shape[0]
    nq = L // tq

    def body(ka_ref, va_ref, qb_ref, doa_ref, qbt_ref, doat_ref, dq_ref, dkt_ref, dvt_ref):
        j = pl.program_id(0)

        @pl.when(j == 0)
        def _():
            dq_ref[...] = jnp.zeros_like(dq_ref)
        row = lax.broadcasted_iota(jnp.int32, (tq, tq), 0)
        col = lax.broadcasted_iota(jnp.int32, (tq, tq), 1)
        diag_bias = jnp.where(row >= col, 0.0, NEG)

        def block(r0, carry, bias):
            out = []
            for h in range(FOX_H):
                hl = slice(h * HL, (h + 1) * HL)
                dkt, dvt, dsum = carry[h]
                kj = ka_ref[:, hl]
                s = _dot_nt(qb_ref[pl.ds(r0, tq), hl], kj)
                if bias is not None:
                    s = s + bias
                p = jnp.exp(s)
                ds = p * _dot_nt(doa_ref[pl.ds(r0, tq), hl], va_ref[:, hl])
                dsb = ds.astype(bf16)
                dq_ref[pl.ds(r0, tq), hl] += _dot(dsb, kj)
                out.append((dkt + _dot(qbt_ref[hl, pl.ds(r0, tq)], dsb),
                            dvt + _dot(doat_ref[hl, pl.ds(r0, tq)], p.astype(bf16)),
                            dsum + jnp.sum(ds, 0, keepdims=True)))
            return tuple(out)

        init = tuple((jnp.zeros((HL, tq), f32), jnp.zeros((HL, tq), f32), jnp.zeros((1, tq), f32))
                     for _ in range(FOX_H))
        carry = block(pl.multiple_of(j * tq, tq), init, diag_bias)
        carry = lax.fori_loop(j + 1, nq, lambda i, c: block(pl.multiple_of(i * tq, tq), c, None), carry)
        for h in range(FOX_H):
            hl = slice(h * HL, (h + 1) * HL)
            dkt_ref[hl, :] = carry[h][0]
            dvt_ref[hl, :] = carry[h][1]
            dkt_ref[h * HL + FOX_HD:h * HL + FOX_HD + 1, :] = carry[h][2]

    full = lambda shape: pl.BlockSpec(shape, lambda j: (0, 0), pipeline_mode=pl.Buffered(1))
    blk = pl.BlockSpec((tq, AW), lambda j: (j, 0))
    trb = pl.BlockSpec((AW, tq), lambda j: (0, j))
    return pl.pallas_call(
        body, name="fox_bwd", grid=(nq,),
        in_specs=[blk, blk, full((L, AW)), full((L, AW)), full((AW, L)), full((AW, L))],
        out_specs=[pl.BlockSpec((L, AW), lambda j: (0, 0)), trb, trb],
        out_shape=[jax.ShapeDtypeStruct((L, AW), f32), jax.ShapeDtypeStruct((AW, L), f32),
                   jax.ShapeDtypeStruct((AW, L), f32)],
        compiler_params=_cp(("arbitrary",)))(ka, va, qb, doa, qbt, doat)


def _fox_post(dq, dkt, dvt, proj, ddt, hp, cst, *, tt=256):
    L = proj.shape[0]
    n = L // tt

    def body(dq_ref, dkt_ref, dvt_ref, sm_ref, ddt_ref, hp_ref, pqt_ref, ca_ref,
             dqo_ref, dko_ref, dvo_ref, dsm_ref, acc_ref, carry):
        @pl.when(pl.program_id(0) == 0)
        def _():
            carry[...] = jnp.zeros_like(carry)
            acc_ref[...] = jnp.zeros_like(acc_ref)
        pqt = pqt_ref[...]
        dk_full = dkt_ref[...].T
        dqo_ref[...] = _dot((dq_ref[...] * FOX_SCALE).astype(bf16), pqt).astype(bf16)
        dko_ref[...] = _dot(dk_full.astype(bf16), pqt).astype(bf16)
        dvo_ref[...] = _dot(dvt_ref[...].T.astype(bf16), pqt).astype(bf16)
        dc = -_dot_hi(dk_full, ca_ref[...])
        r = lax.broadcasted_iota(jnp.int32, (tt, tt), 0)
        c = lax.broadcasted_iota(jnp.int32, (tt, tt), 1)
        dl = _dot_hi(jnp.where(r <= c, 1.0, 0.0), dc) + carry[0:1, :]
        carry[0:1, :] += jnp.sum(dc, 0, keepdims=True)
        xx = sm_ref[...] + hp_ref[3:4, :]
        lane = lax.broadcasted_iota(jnp.int32, (tt, 128), 1)
        dlogit = jnp.where((lane >= SM_F) & (lane < SM_F + FOX_H), dl * _sigmoid(-xx), 0.0)
        acc_ref[3:4, :] += jnp.sum(dlogit, 0, keepdims=True)
        dsm_ref[...] = (dlogit + ddt_ref[...]).astype(bf16)

    rev = lambda w_: pl.BlockSpec((tt, w_), lambda i: (n - 1 - i, 0))
    revt = pl.BlockSpec((AW, tt), lambda i: (0, n - 1 - i))
    return pl.pallas_call(
        body, name="fox_post", grid=(n,),
        in_specs=[rev(AW), revt, revt, pl.BlockSpec((tt, 128), lambda i: (n - 1 - i, P_SM // 128)), rev(128),
                  _const_spec((8, 128)), _const_spec((AW, FOX_W)), _const_spec((AW, 128))],
        out_specs=[rev(FOX_W), rev(FOX_W), rev(FOX_W), rev(128), pl.BlockSpec((8, 128), lambda i: (0, 0))],
        out_shape=[jax.ShapeDtypeStruct((L, FOX_W), bf16)] * 3 + [jax.ShapeDtypeStruct((L, 128), bf16),
                                                                   jax.ShapeDtypeStruct((8, 128), f32)],
        scratch_shapes=[pltpu.VMEM((8, 128), f32)],
        compiler_params=_cp(("arbitrary",)))(dq, dkt, dvt, proj, ddt, hp, cst["pqt"], cst["col_a"])


SSD_GW = SSD_W // SSD_G
SSD_HPG = SSD_H // SSD_G


def _ssd_pre(x, xprev, sm, cp_ref, hp_ref, tc):
    pre = (cp_ref[4:5, :] + cp_ref[3:4, :] * x + cp_ref[2:3, :] * _shift_down(x, xprev, 1)
           + cp_ref[1:2, :] * _shift_down(x, xprev, 2) + cp_ref[0:1, :] * _shift_down(x, xprev, 3))
    sig = _sigmoid(pre)
    raw = sm + hp_ref[0:1, :]
    dt = jnp.maximum(raw, 0.0) + _log1pexp(raw)
    a_neg = -jnp.exp(hp_ref[1:2, :])
    r = lax.broadcasted_iota(jnp.int32, (tc, tc), 0)
    c = lax.broadcasted_iota(jnp.int32, (tc, tc), 1)
    cs = _dot_hi(jnp.where(r >= c, 1.0, 0.0), dt * a_neg)
    return pre, sig, raw, dt, a_neg, cs, cs.T, r >= c


def _ssd_fwd(proj, cp, hp, ng, *, tc=256):
    L = proj.shape[0]
    nc = L // tc

    def body(xc_ref, xp_ref, z_ref, sm_ref, cp_ref, hp_ref, ng_ref, y_ref, ypre_ref, sin_ref, s_scr):
        i = pl.program_id(0)

        @pl.when(i == 0)
        def _():
            s_scr[...] = jnp.zeros_like(s_scr)
        x = xc_ref[...]
        xprev = xp_ref[...] * jnp.where(i > 0, 1.0, 0.0)
        pre, sig, _, dt, _, cs, cst, tril = _ssd_pre(x, xprev, sm_ref[...], cp_ref, hp_ref, tc)
        xbc = pre * sig
        sin_ref[...] = s_scr[...]
        for g in range(SSD_G):
            bg = xbc[:, SSD_W + g * SSD_N:SSD_W + (g + 1) * SSD_N]
            cg = xbc[:, SSD_W + SSD_G * SSD_N + g * SSD_N:SSD_W + SSD_G * SSD_N + (g + 1) * SSD_N].astype(bf16)
            cb = _dot_nt(cg, bg.astype(bf16))
            for e in range(SSD_HPG):
                h = g * SSD_HPG + e
                hs = slice(h * SSD_HD, (h + 1) * SSD_HD)
                xs = xbc[:, hs]
                csc = cs[:, h:h + 1]
                lm = jnp.where(tril, jnp.exp(jnp.minimum(csc - cst[h:h + 1, :], 0.0)), 0.0)
                xdt = (xs * dt[:, h:h + 1]).astype(bf16)
                s_h = s_scr[:, hs]
                y = _dot((cb * lm).astype(bf16), xdt) + jnp.exp(csc) * _dot(cg, s_h.astype(bf16))
                ypre_ref[:, hs] = y + hp_ref[2:3, h:h + 1] * xs
                cl = cs[tc - 1:tc, h:h + 1]
                bd = (bg * jnp.exp(cl - csc)).astype(bf16)
                s_scr[:, hs] = jnp.exp(cl) * s_h + _dot_tn(bd, xdt)
        z = z_ref[...]
        yz = ypre_ref[...] * (z * _sigmoid(z))
        for g in range(SSD_G):
            gs = slice(g * SSD_GW, (g + 1) * SSD_GW)
            yg = yz[:, gs]
            r = lax.rsqrt(jnp.mean(yg * yg, -1, keepdims=True) + RMS_EPS)
            y_ref[:, gs] = (yg * r * ng_ref[0:1, gs]).astype(bf16)

    return pl.pallas_call(
        body, name="ssd_fwd", grid=(nc,),
        in_specs=[pl.BlockSpec((tc, 1024), lambda i: (i, 0)),
                  pl.BlockSpec((tc, 1024), lambda i: (jnp.maximum(i - 1, 0), 0)),
                  pl.BlockSpec((tc, SSD_W), lambda i: (i, P_Z // SSD_W)),
                  pl.BlockSpec((tc, 128), lambda i: (i, P_SM // 128)),
                  _const_spec((8, 1024)), _const_spec((8, 128)), _const_spec((8, SSD_W))],
        out_specs=[pl.BlockSpec((tc, SSD_W), lambda i: (i, 0)), pl.BlockSpec((tc, SSD_W), lambda i: (i, 0)),
                   pl.BlockSpec((SSD_N, SSD_W), lambda i: (i, 0))],
        out_shape=[jax.ShapeDtypeStruct((L, SSD_W), bf16), jax.ShapeDtypeStruct((L, SSD_W), f32),
                   jax.ShapeDtypeStruct((nc * SSD_N, SSD_W), f32)],
        scratch_shapes=[pltpu.VMEM((SSD_N, SSD_W), f32)],
        compiler_params=_cp(("arbitrary",)))(proj, proj, proj, proj, cp, hp, ng)


def _ssd_bwd(proj, dycat, ypre, sin, cp, hp, ng, *, tc=256):
    L = proj.shape[0]
    nc = L // tc

    def body(xc_ref, xp_ref, z_ref, sm_ref, cp_ref, hp_ref, ng_ref, sin_ref, ypre_ref, dy_ref,
             dxbc_ref, dz_ref, ddt_ref, acc1_ref, acc2_ref, ds_scr, dnext_scr, dxbc_scr):
        i = pl.program_id(0)
        c_idx = nc - 1 - i

        @pl.when(i == 0)
        def _():
            ds_scr[...] = jnp.zeros_like(ds_scr)
            dnext_scr[...] = jnp.zeros_like(dnext_scr)
            acc1_ref[...] = jnp.zeros_like(acc1_ref)
            acc2_ref[...] = jnp.zeros_like(acc2_ref)
        x = xc_ref[...]
        xprev = xp_ref[...] * jnp.where(c_idx > 0, 1.0, 0.0)
        pre, sig, raw, dt, a_neg, cs, cst, tril = _ssd_pre(x, xprev, sm_ref[...], cp_ref, hp_ref, tc)
        xbc = pre * sig
        z = z_ref[...]
        sz = _sigmoid(z)
        silz = z * sz
        yall = ypre_ref[...]
        yz = yall * silz
        dy = dy_ref[...]
        dyz_parts = []
        for g in range(SSD_G):
            gs = slice(g * SSD_GW, (g + 1) * SSD_GW)
            yg, dyg = yz[:, gs], dy[:, gs]
            r = lax.rsqrt(jnp.mean(yg * yg, -1, keepdims=True) + RMS_EPS)
            acc1_ref[5:6, gs] += jnp.sum(dyg * yg * r, 0, keepdims=True)
            dyn = dyg * ng_ref[0:1, gs]
            dyz_parts.append(r * (dyn - yg * (r * r) * jnp.mean(dyn * yg, -1, keepdims=True)))
        dyz = jnp.concatenate(dyz_parts, axis=1)
        dz_ref[...] = (dyz * yall * (sz * (1.0 + z * (1.0 - sz)))).astype(bf16)
        dyall = dyz * silz

        lane1 = lax.broadcasted_iota(jnp.int32, (1, 128), 1)
        sub = lax.broadcasted_iota(jnp.int32, (128, tc), 0)
        rowc = lax.broadcasted_iota(jnp.int32, (tc, 1), 0)
        dcs = jnp.zeros((tc, 128), f32)
        dcsr = jnp.zeros((128, tc), f32)
        ddt = jnp.zeros((tc, 128), f32)
        dd_row = jnp.zeros((1, 128), f32)
        for g in range(SSD_G):
            b0 = SSD_W + g * SSD_N
            c0 = SSD_W + SSD_G * SSD_N + g * SSD_N
            bg = xbc[:, b0:b0 + SSD_N]
            bgb = bg.astype(bf16)
            cgb = xbc[:, c0:c0 + SSD_N].astype(bf16)
            cb = _dot_nt(cgb, bgb)
            dbg = jnp.zeros((tc, SSD_N), f32)
            dcg = jnp.zeros((tc, SSD_N), f32)
            for e in range(SSD_HPG):
                h = g * SSD_HPG + e
                hs = slice(h * SSD_HD, (h + 1) * SSD_HD)
                oh = jnp.where(lane1 == h, 1.0, 0.0)
                xs = xbc[:, hs]
                dth = dt[:, h:h + 1]
                csc = cs[:, h:h + 1]
                lm = jnp.where(tril, jnp.exp(jnp.minimum(csc - cst[h:h + 1, :], 0.0)), 0.0)
                m = cb * lm
                xdt = (xs * dth).astype(bf16)
                s_h = sin_ref[:, hs]
                s_hb = s_h.astype(bf16)
                dyh = dyall[:, hs]
                dyb = dyh.astype(bf16)
                dd_row = dd_row + oh * jnp.sum(dyh * xs)
                dxs = hp_ref[2:3, h:h + 1] * dyh
                ecs = jnp.exp(csc)
                cs_prod = _dot(cgb, s_hb)
                dcsb = (ecs * dyh).astype(bf16)
                dcg = dcg + _dot_nt(dcsb, s_hb)
                ds_in = _dot_tn(cgb, dcsb)
                dcs_h = jnp.sum(dyh * ecs * cs_prod, -1, keepdims=True)
                dm = _dot_nt(dyb, xdt)
                w = dm * m
                dcs_h = dcs_h + jnp.sum(w, -1, keepdims=True)
                dcsr = jnp.where(sub == h, jnp.sum(w, 0, keepdims=True), dcsr)
                dcbb = (dm * lm).astype(bf16)
                dcg = dcg + _dot(dcbb, bgb)
                dbg = dbg + _dot_tn(dcbb, cgb)
                dxdt = _dot_tn(m.astype(bf16), dyb)
                dsn = ds_scr[:, hs]
                dsnb = dsn.astype(bf16)
                cl = cs[tc - 1:tc, h:h + 1]
                dec = jnp.exp(cl - csc)
                dxdt = dxdt + _dot((bg * dec).astype(bf16), dsnb)
                dbd = _dot_nt(xdt, dsnb)
                dbg = dbg + dbd * dec
                gdec = jnp.sum(dbd * bg, -1, keepdims=True) * dec
                ecl = jnp.exp(cl)
                dcl = jnp.sum(gdec) + jnp.sum(dsn * s_h) * ecl
                ds_scr[:, hs] = ecl * dsn + ds_in
                dcs_h = dcs_h - gdec + jnp.where(rowc == tc - 1, dcl, 0.0)
                dcs = dcs + dcs_h * oh
                dxbc_scr[:, hs] = dxs + dxdt * dth
                ddt = ddt + jnp.sum(dxdt * xs, -1, keepdims=True) * oh
            dxbc_scr[:, b0:b0 + SSD_N] = dbg
            dxbc_scr[:, c0:c0 + SSD_N] = dcg
        dcs = dcs - dcsr.T
        r_i = lax.broadcasted_iota(jnp.int32, (tc, tc), 0)
        c_i = lax.broadcasted_iota(jnp.int32, (tc, tc), 1)
        da = _dot_hi(jnp.where(r_i <= c_i, 1.0, 0.0), dcs)
        ddt = ddt + da * a_neg
        acc2_ref[1:2, :] += jnp.sum(da * dt, 0, keepdims=True) * a_neg
        lane = lax.broadcasted_iota(jnp.int32, (tc, 128), 1)
        ddraw = jnp.where(lane < SSD_H, ddt * _sigmoid(raw), 0.0)
        acc2_ref[0:1, :] += jnp.sum(ddraw, 0, keepdims=True)
        acc2_ref[2:3, :] += dd_row
        ddt_ref[...] = ddraw
        dpre = dxbc_scr[...] * (sig * (1.0 + pre * (1.0 - sig)))
        acc1_ref[4:5, :] += jnp.sum(dpre, 0, keepdims=True)
        for k in range(SSD_K):
            acc1_ref[k:k + 1, :] += jnp.sum(dpre * _shift_down(x, xprev, SSD_K - 1 - k), 0, keepdims=True)
        dnext = dnext_scr[...]
        dxbc_ref[...] = (cp_ref[3:4, :] * dpre + cp_ref[2:3, :] * _shift_up(dpre, dnext, 1)
                         + cp_ref[1:2, :] * _shift_up(dpre, dnext, 2)
                         + cp_ref[0:1, :] * _shift_up(dpre, dnext, 3)).astype(bf16)
        dnext_scr[...] = dpre

    rev = lambda w_, col: pl.BlockSpec((tc, w_), lambda i: (nc - 1 - i, col // w_))
    return pl.pallas_call(
        body, name="ssd_bwd", grid=(nc,),
        in_specs=[rev(1024, 0), pl.BlockSpec((tc, 1024), lambda i: (jnp.maximum(nc - 2 - i, 0), 0)),
                  rev(SSD_W, P_Z), rev(128, P_SM),
                  _const_spec((8, 1024)), _const_spec((8, 128)), _const_spec((8, SSD_W)),
                  pl.BlockSpec((SSD_N, SSD_W), lambda i: (nc - 1 - i, 0)), rev(SSD_W, 0), rev(SSD_W, 0)],
        out_specs=[rev(1024, 0), rev(SSD_W, 0), rev(128, 0),
                   pl.BlockSpec((8, 1024), lambda i: (0, 0)), pl.BlockSpec((8, 128), lambda i: (0, 0))],
        out_shape=[jax.ShapeDtypeStruct((L, 1024), bf16), jax.ShapeDtypeStruct((L, SSD_W), bf16),
                   jax.ShapeDtypeStruct((L, 128), f32), jax.ShapeDtypeStruct((8, 1024), f32),
                   jax.ShapeDtypeStruct((8, 128), f32)],
        scratch_shapes=[pltpu.VMEM((SSD_N, SSD_W), f32), pltpu.VMEM((tc, 1024), f32), pltpu.VMEM((tc, 1024), f32)],
        compiler_params=_cp(("arbitrary",)))(proj, proj, proj, proj, cp, hp, ng, sin, ypre, dycat)


def _pack_cols(w):
    pad = jnp.zeros(w.shape[:-1] + (P_W - P_SM - SSD_H - FOX_H,), w.dtype)
    return jnp.concatenate([w[..., 512:1536], w[..., 0:512], w[..., 1544:2312], w[..., 2316:3084],
                            w[..., 1536:1544], w[..., 2312:2316], pad], axis=-1)


def _unpack_cols(g):
    return jnp.concatenate([g[..., 1024:1536], g[..., 0:1024], g[..., 3072:3080], g[..., 1536:2304],
                            g[..., 3080:3084], g[..., 2304:3072]], axis=-1)


def _rows8(*rows):
    width = max(r.shape[-1] for r in rows)
    out = [jnp.pad(r.astype(f32), (0, width - r.shape[-1])) for r in rows]
    out += [jnp.zeros((width,), f32)] * (8 - len(out))
    return jnp.stack(out)


def _local_step(x, tgt, p, weights_of, grads_done):
    gb_in = _rows8(p["ln_in_g"], p["ln_in_b"])
    cst = _fox_consts()
    x0 = _ln_in_fwd(x, gb_in)
    saved = []
    for l in range(DEPTH):
        mv = [_rows8(p["mod"][l, 3 * j], p["mod"][l, 3 * j + 1], p["mod"][l, 3 * j + 2], p["ln_g"][l, j], p["ln_b"][l, j])
              for j in range(3)]
        cp = _rows8(*[p["ssd_conv_w"][l, k] for k in range(SSD_K)], p["ssd_conv_b"][l])
        hp = _rows8(jnp.pad(p["ssd_dt_bias"][l], (0, 120)), jnp.pad(p["ssd_a_log"][l], (0, 120)),
                    jnp.pad(p["ssd_d"][l], (0, 120)), jnp.pad(p["fox_f_bias"][l], (SM_F, 128 - SM_F - FOX_H)))
        ng = _rows8(p["ssd_norm_g"][l])
        scw = _rows8(*[p["sconv_w"][l, k] for k in range(SC_K)])
        w = weights_of(l, x0)
        x1 = _ffn_fwd(x0, mv[0], w["ffn1_w_in"], w["ffn1_w_out"])
        proj = _inproj_fwd(x1, mv[1], w["mix_w_in"])
        y_ssd, ypre, sin = _ssd_fwd(proj, cp, hp, ng)
        qa, ka, va = _fox_prep(proj, hp, cst)
        o, o_acc, lse = _fox_attn_fwd(qa, ka, va)
        y_sc = _sconv_fwd(proj, scw)
        ycat = jnp.concatenate([y_ssd, o, y_sc], axis=1)
        x2 = _outproj_fwd(x1, ycat, mv[1], w["mix_w_out"])
        x3 = _ffn_fwd(x2, mv[2], w["ffn2_w_in"], w["ffn2_w_out"])
        saved.append((x0, x1, x2, mv, cp, hp, ng, scw, proj, ypre, sin, qa, ka, va, o_acc, lse, ycat, w))
        x0 = x3
    dx, loss_acc = _loss_head(x0, tgt)

    g = {k: [None] * DEPTH for k in (
        "mod", "ln_g", "ln_b", "ssd_conv_w", "ssd_conv_b", "ssd_dt_bias", "ssd_a_log", "ssd_d", "ssd_norm_g",
        "fox_f_bias", "sconv_w")}
    token = None
    for l in reversed(range(DEPTH)):
        x0, x1, x2, mv, cp, hp, ng, scw, proj, ypre, sin, qa, ka, va, o_acc, lse, ycat, w = saved[l]
        gw = {}
        mv2 = mv[2] if token is None else mv[2] + token[0, 0]
        dx, h, da, act, dyb, a2 = _ffn_bwd(x2, dx, mv2, w["ffn2_w_in"], w["ffn2_w_out"])
        gw["ffn2_w_in"] = _matmul_tn(h, da, tn=FS, tk=512, name="dw_ffn_in")
        gw["ffn2_w_out"] = _matmul_tn(act, dyb, tn=D, tk=512, name="dw_ffn_out")
        dxp, dyb, dycat, a1 = _outproj_bwd(x1, ycat, dx, mv[1], w["mix_w_out"])
        gw["mix_w_out"] = _matmul_tn(ycat, dyb, tn=D // 2, tk=512, name="dw_mix_out")
        dxbc, dz, ddt, acc1, acc2 = _ssd_bwd(proj, dycat, ypre, sin, cp, hp, ng)
        qb, doa, qbt, doat = _fox_bprep(qa, dycat, o_acc, lse, cst)
        dq, dkt, dvt = _fox_attn_bwd(ka, va, qb, doa, qbt, doat)
        dq, dk, dv, dsm, accf = _fox_post(dq, dkt, dvt, proj, ddt, hp, cst)
        dsb, dsc, dsx, accs = _sconv_bwd(proj, dycat, scw)
        dproj = jnp.concatenate([dxbc, dz, dq, dk, dv, dsb, dsc, dsx, dsm], axis=1)
        dx, h, a1b = _inproj_bwd(x1, dxp, dproj, mv[1], w["mix_w_in"])
        gw["mix_w_in"] = _matmul_tn(h, dproj, tn=P_W // 5, tk=512, name="dw_mix_in")
        dx, h, da, act, dyb, a0 = _ffn_bwd(x0, dx, mv[0], w["ffn1_w_in"], w["ffn1_w_out"])
        gw["ffn1_w_in"] = _matmul_tn(h, da, tn=FS, tk=512, name="dw_ffn_in")
        gw["ffn1_w_out"] = _matmul_tn(act, dyb, tn=D, tk=512, name="dw_ffn_out")
        token = grads_done(l, gw)
        g["mod"][l] = jnp.concatenate([a0[0:3], a1b[0:2], a1[2:3], a2[0:3]], axis=0)
        g["ln_g"][l] = jnp.stack([a0[3], a1[3], a2[3]])
        g["ln_b"][l] = jnp.stack([a0[4], a1[4], a2[4]])
        g["ssd_conv_w"][l] = acc1[0:SSD_K]
        g["ssd_conv_b"][l] = acc1[4]
        g["ssd_norm_g"][l] = acc1[5, :SSD_W]
        g["ssd_dt_bias"][l] = acc2[0, :SSD_H]
        g["ssd_a_log"][l] = acc2[1, :SSD_H]
        g["ssd_d"][l] = acc2[2, :SSD_H]
        g["fox_f_bias"][l] = accf[3, SM_F:SM_F + FOX_H]
        g["sconv_w"][l] = accs[0:SC_K]
    grad_x, a_in = _ln_in_bwd(x, dx, gb_in)
    g = {k: jnp.stack(v) for k, v in g.items()}
    g["ln_in_g"], g["ln_in_b"] = a_in[0], a_in[1]
    return loss_acc[0, 0], grad_x, g


MESH = pl.DeviceIdType.MESH
ANY = pl.BlockSpec(memory_space=pl.ANY)


def _all_gather(shards, *, in_vmem, name):
    n_arr = len(shards)

    def body(*refs):
        x_refs, out_refs = refs[:n_arr], refs[n_arr:2 * n_arr]
        send_sems, recv_sems, local_sems = refs[2 * n_arr:]
        x, y, c = lax.axis_index("x"), lax.axis_index("y"), lax.axis_index("c")
        me, sibling = (x, y, c), (x, y, 1 - c)
        chips = [(1 - x, y), (x, 1 - y), (1 - x, 1 - y)]

        def copy(a, k, block, to, src=None):
            px, py, pc = block
            slot = out_refs[a].at[4 * px + 2 * py + pc]
            return pltpu.make_async_remote_copy(
                src_ref=slot if src is None else src, dst_ref=slot,
                send_sem=send_sems.at[7 * a + k], recv_sem=recv_sems.at[7 * a + k], device_id=to, device_id_type=MESH)

        mine, first, passed = [], [], []
        for a in range(n_arr):
            mine.append(pltpu.make_async_copy(x_refs[a], out_refs[a].at[4 * x + 2 * y + c], local_sems.at[a]))
            mine[-1].start()
            first.append(copy(a, 0, me, sibling, src=x_refs[a]))
            first += [copy(a, 1 + j, me, (*chip, c), src=x_refs[a]) for j, chip in enumerate(chips)]
        for cp in first:
            cp.start()
        for j, chip in enumerate(chips):
            for a in range(n_arr):
                copy(a, 1 + j, (*chip, c), me).wait_recv()
                passed.append(copy(a, 4 + j, (*chip, c), sibling))
                passed[-1].start()
        for a in range(n_arr):
            copy(a, 0, sibling, me).wait_recv()
            for j, chip in enumerate(chips):
                copy(a, 4 + j, (*chip, 1 - c), me).wait_recv()
        for cp in first + passed:
            cp.wait_send()
        for cp in mine:
            cp.wait()

    spec = pl.BlockSpec(memory_space=pltpu.VMEM) if in_vmem else ANY
    return pl.pallas_call(
        body, name=name, out_shape=[jax.ShapeDtypeStruct((N_DEV,) + s.shape, s.dtype) for s in shards],
        in_specs=[spec] * n_arr, out_specs=[spec] * n_arr,
        scratch_shapes=[pltpu.SemaphoreType.DMA((7 * n_arr,)), pltpu.SemaphoreType.DMA((7 * n_arr,)),
                        pltpu.SemaphoreType.DMA((n_arr,))],
    )(*shards)


def _swap_sibling(sends):
    n_arr = len(sends)

    def body(*refs):
        s_refs, o_refs = refs[:n_arr], refs[n_arr:2 * n_arr]
        send_sems, recv_sems = refs[2 * n_arr:]
        x, y, c = lax.axis_index("x"), lax.axis_index("y"), lax.axis_index("c")
        cps = [pltpu.make_async_remote_copy(
            src_ref=s_refs[a].at[:, 1 - c], dst_ref=o_refs[a], send_sem=send_sems.at[a], recv_sem=recv_sems.at[a],
            device_id=(x, y, 1 - c), device_id_type=MESH) for a in range(n_arr)]
        for cp in cps:
            cp.start()
        for cp in cps:
            cp.wait_recv()
        for cp in cps:
            cp.wait_send()

    return pl.pallas_call(
        body, name="swap_sibling",
        out_shape=[jax.ShapeDtypeStruct((s.shape[0],) + s.shape[2:], s.dtype) for s in sends],
        in_specs=[ANY] * n_arr, out_specs=[ANY] * n_arr,
        scratch_shapes=[pltpu.SemaphoreType.DMA((n_arr,)), pltpu.SemaphoreType.DMA((n_arr,))])(*sends)


def _exchange_chips(bufs):
    n_arr = len(bufs)

    def body(*refs):
        b_refs, o_refs = refs[:n_arr], refs[n_arr:2 * n_arr]
        send_sems, recv_sems, local_sems = refs[2 * n_arr:]
        x, y, c = lax.axis_index("x"), lax.axis_index("y"), lax.axis_index("c")
        mine = 2 * x + y
        peers = [(x, 1 - y), (1 - x, y), (1 - x, 1 - y)]

        def copy(a, k, src_slot, dst_slot):
            px, py = peers[k]
            return pltpu.make_async_remote_copy(
                src_ref=b_refs[a].at[src_slot], dst_ref=o_refs[a].at[dst_slot],
                send_sem=send_sems.at[3 * a + k], recv_sem=recv_sems.at[3 * a + k],
                device_id=(px, py, c), device_id_type=MESH)

        own = [pltpu.make_async_copy(b_refs[a].at[mine], o_refs[a].at[mine], local_sems.at[a]) for a in range(n_arr)]
        sent = [copy(a, k, 2 * px + py, mine) for a in range(n_arr) for k, (px, py) in enumerate(peers)]
        for cp in own + sent:
            cp.start()
        for a in range(n_arr):
            for k, (px, py) in enumerate(peers):
                copy(a, k, mine, 2 * px + py).wait_recv()
        for cp in sent:
            cp.wait_send()
        for cp in own:
            cp.wait()

    return pl.pallas_call(
        body, name="exchange_chips", out_shape=[jax.ShapeDtypeStruct(b.shape, b.dtype) for b in bufs],
        in_specs=[ANY] * n_arr, out_specs=[ANY] * n_arr,
        scratch_shapes=[pltpu.SemaphoreType.DMA((3 * n_arr,)), pltpu.SemaphoreType.DMA((3 * n_arr,)),
                        pltpu.SemaphoreType.DMA((n_arr,))])(*bufs)


HBM = pl.BlockSpec(memory_space=pltpu.HBM)
SEM = pl.BlockSpec(memory_space=pltpu.SEMAPHORE)
EFFECT = pltpu.SideEffectType.DATAFLOW_SIDE_EFFECTING


def _spread_copies(srcs, lands, send_sems, recv_sems, local_sems, scatter):
    x, y, c = lax.axis_index("x"), lax.axis_index("y"), lax.axis_index("c")
    me = 4 * x + 2 * y + c
    local, remote = [], []
    for a in range(len(srcs)):
        own = srcs[a].at[me] if scatter else srcs[a]
        local.append(pltpu.make_async_copy(own, lands[a].at[me], local_sems.at[a]))
        for r in range(1, N_DEV):
            px, py, pc = (1 - x if r & 4 else x), (1 - y if r & 2 else y), (1 - c if r & 1 else c)
            peer = 4 * px + 2 * py + pc
            k = (N_DEV - 1) * a + r - 1
            mk = functools.partial(pltpu.make_async_remote_copy, send_sem=send_sems.at[k], recv_sem=recv_sems.at[k],
                                   device_id=(px, py, pc), device_id_type=MESH)
            remote.append((mk(src_ref=srcs[a].at[peer] if scatter else srcs[a], dst_ref=lands[a].at[me]),
                           mk(src_ref=own, dst_ref=lands[a].at[peer])))
    return local, remote


def _spread_start(srcs, *, scatter, name):
    n = len(srcs)
    lands = [jax.ShapeDtypeStruct((N_DEV,) + s.shape[-2:], s.dtype) for s in srcs]

    def body(*refs):
        src, land = refs[:n], refs[n:2 * n]
        send_sems, recv_sems, local_sems = refs[2 * n:2 * n + 3]
        token = refs[-1]
        local, remote = _spread_copies(src, land, send_sems, recv_sems, local_sems, scatter)
        for cp in local:
            cp.start()
        for cp, _ in remote:
            cp.start()
        token[...] = jnp.zeros_like(token)

    nsem = (N_DEV - 1) * n
    out = pl.pallas_call(
        body, name=name,
        out_shape=(pltpu.SemaphoreType.DMA((nsem,)), pltpu.SemaphoreType.DMA((nsem,)), pltpu.SemaphoreType.DMA((n,)),
                   *[pltpu.HBM(s.shape, s.dtype) for s in srcs], *[pltpu.HBM(s.shape, s.dtype) for s in lands],
                   jax.ShapeDtypeStruct((8, 128), f32)),
        in_specs=[HBM] * (2 * n),
        out_specs=(SEM, SEM, SEM, *[HBM] * (2 * n), pl.BlockSpec(memory_space=pltpu.VMEM)),
        input_output_aliases={i: 3 + i for i in range(2 * n)},
        compiler_params=pltpu.CompilerParams(has_side_effects=EFFECT),
    )(*[pltpu.with_memory_space_constraint(s, pltpu.HBM) for s in srcs],
      *[pltpu.with_memory_space_constraint(lax.empty(s.shape, s.dtype), pltpu.HBM) for s in lands])
    return out[:-1], out[-1]


def _spread_wait(state, after, *, scatter, name):
    n = (len(state) - 3) // 2
    sems, thru = state[:3], state[3:]

    def body(*refs):
        src, land = refs[:n], refs[n:2 * n]
        send_sems, recv_sems, local_sems = refs[2 * n:2 * n + 3]
        local, remote = _spread_copies(src, land, send_sems, recv_sems, local_sems, scatter)
        for sent, received in remote:
            sent.wait_send()
            received.wait_recv()
        for cp in local:
            cp.wait()

    out = pl.pallas_call(
        body, name=name,
        out_shape=tuple(pltpu.HBM(t.shape, t.dtype) for t in thru),
        in_specs=[HBM] * (2 * n) + [SEM] * 3 + [ANY], out_specs=tuple([HBM] * (2 * n)),
        input_output_aliases={i: i for i in range(2 * n)},
        compiler_params=pltpu.CompilerParams(has_side_effects=EFFECT),
    )(*thru, *sems, after)
    return list(out[n:])


def _row_tile(r, c):
    if r * c <= 512 * 1024:
        return r
    t = r
    while t * c > 512 * 1024 and t % 2 == 0 and (t // 2) % 16 == 0:
        t //= 2
    return t


def _add_own_half(send, recv, c):
    nb, _, r, n = send.shape
    tr = _row_tile(r, n)

    def body(c_ref, s_ref, r_ref, o_ref):
        o_ref[...] = (s_ref[...].astype(f32) + r_ref[...].astype(f32)).astype(o_ref.dtype)

    return pl.pallas_call(
        body, name="add_own_half",
        grid_spec=pltpu.PrefetchScalarGridSpec(
            num_scalar_prefetch=1, grid=(nb, r // tr),
            in_specs=[pl.BlockSpec((None, None, tr, n), lambda j, i, cr: (j, cr[0], i, 0)),
                      pl.BlockSpec((None, tr, n), lambda j, i, cr: (j, i, 0))],
            out_specs=pl.BlockSpec((None, tr, n), lambda j, i, cr: (j, i, 0))),
        out_shape=jax.ShapeDtypeStruct((nb, r, n), bf16),
        compiler_params=_cp(("parallel", "parallel")))(jnp.reshape(c, (1,)).astype(jnp.int32), send, recv)


def _sum_slots(buf, *, tr, name):
    nb, r, n = buf.shape

    def body(b_ref, o_ref):
        acc = b_ref[0].astype(f32)
        for k in range(1, nb):
            acc = acc + b_ref[k].astype(f32)
        o_ref[...] = acc

    return pl.pallas_call(
        body, name=name, grid=(r // tr,),
        in_specs=[pl.BlockSpec((nb, tr, n), lambda i: (0, i, 0))],
        out_specs=pl.BlockSpec((tr, n), lambda i: (i, 0)),
        out_shape=jax.ShapeDtypeStruct((r, n), f32), compiler_params=_cp(("parallel",)))(buf)


def _ada_fwd(c_all, ada_w, ada_b_cols):
    n = ada_w.shape[-1]

    def body(c_ref, w_ref, b_ref, o_ref):
        cv = c_ref[...]
        ca = (cv * _sigmoid(cv)).astype(bf16)
        o_ref[...] = _dot(ca, w_ref[...].astype(bf16)) + b_ref[...]

    return pl.pallas_call(
        body, name="ada_fwd", grid=(DEPTH,),
        in_specs=[_const_spec((N_DEV, D)), pl.BlockSpec((None, D, n), lambda l: (l, 0, 0)),
                  pl.BlockSpec((None, 1, n), lambda l: (l, 0, 0))],
        out_specs=pl.BlockSpec((None, N_DEV, n), lambda l: (l, 0, 0)),
        out_shape=jax.ShapeDtypeStruct((DEPTH, N_DEV, n), f32), compiler_params=_cp(("parallel",)))(c_all, ada_w, ada_b_cols)


def _ada_bwd(c_all, dmod_cols):
    n = dmod_cols.shape[-1]

    def body(c_ref, d_ref, o_ref):
        cv = c_ref[...]
        ca = (cv * _sigmoid(cv)).astype(bf16)
        o_ref[...] = _dot_tn(ca, d_ref[...].astype(bf16))

    return pl.pallas_call(
        body, name="ada_bwd", grid=(DEPTH,),
        in_specs=[_const_spec((N_DEV, D)), pl.BlockSpec((None, N_DEV, n), lambda l: (l, 0, 0))],
        out_specs=pl.BlockSpec((None, D, n), lambda l: (l, 0, 0)),
        out_shape=jax.ShapeDtypeStruct((DEPTH, D, n), f32), compiler_params=_cp(("parallel",)))(c_all, dmod_cols)


def _adamw(w, g, m, v, *, tr, name):
    r, n = w.shape

    def body(w_ref, g_ref, m_ref, v_ref, d_ref, mo_ref, vo_ref):
        g_ = g_ref[...]
        m_ = ADAM_B1 * m_ref[...] + (1.0 - ADAM_B1) * g_
        v_ = ADAM_B2 * v_ref[...] + (1.0 - ADAM_B2) * jnp.square(g_)
        m_hat = m_ / (1.0 - ADAM_B1 ** ADAM_STEP)
        v_hat = v_ / (1.0 - ADAM_B2 ** ADAM_STEP)
        d_ref[...] = -ADAM_LR * (m_hat / (jnp.sqrt(v_hat) + ADAM_EPS) + ADAM_WD * w_ref[...])
        mo_ref[...] = m_
        vo_ref[...] = v_

    blk = pl.BlockSpec((tr, n), lambda i: (i, 0))
    return pl.pallas_call(
        body, name=name, grid=(r // tr,), in_specs=[blk] * 4, out_specs=[blk] * 3,
        out_shape=[jax.ShapeDtypeStruct((r, n), f32)] * 3, compiler_params=_cp(("parallel",)))(w, g, m, v)


WEIGHTS = ["ln_in_g", "ln_in_b", "ada_w", "ada_b", "ffn1_w_in", "ffn1_w_out", "mix_w_in", "mix_w_out", "ssd_conv_w",
           "ssd_conv_b", "ssd_dt_bias", "ssd_a_log", "ssd_d", "ssd_norm_g", "fox_f_bias", "sconv_w", "ffn2_w_in",
           "ffn2_w_out", "ln_g", "ln_b"]
BIG = ["ffn1_w_in", "ffn1_w_out", "ffn2_w_in", "ffn2_w_out", "mix_w_in", "mix_w_out"]
COL_SHARDED = ("ffn1_w_in", "ffn2_w_in")
SMALL_SHARDED = {"ssd_conv_w": 128, "sconv_w": 32, "ln_g": 128, "ln_b": 128}
ADAM_TR = {"ada_w": 256, "ffn1_w_in": 512, "ffn2_w_in": 512, "ffn1_w_out": 352, "ffn2_w_out": 352, "mix_w_in": 64,
           "mix_w_out": 256}


def _pad_rows(v, mult=128):
    v = v.reshape(-1)
    return jnp.pad(v, (0, (-v.shape[0]) % mult))


def _pack_rows(parts, row_mult=8):
    flat = [_pad_rows(p.astype(f32)) for p in parts]
    offs, o = [], 0
    for f in flat:
        offs.append(o)
        o += f.shape[0] // 128
    buf = jnp.concatenate(flat).reshape(-1, 128)
    return jnp.pad(buf, ((0, (-buf.shape[0]) % row_mult), (0, 0))), offs


def _take(buf, off, shape):
    n = 1
    for s in shape:
        n *= s
    rows = -(-n // 128)
    lead = buf.shape[:-2]
    flat = buf[..., off:off + rows, :].reshape(lead + (rows * 128,))
    return flat[..., :n].reshape(lead + tuple(shape))


def _reduce_scatter_layer(g_layer, ci):
    sends = [t.reshape((4, 2) + t.shape[-2:]) for t in g_layer]
    got = _swap_sibling(sends)
    chip_sums = [_add_own_half(s, r, ci) for s, r in zip(sends, got)]
    got = _exchange_chips(chip_sums)
    return [_sum_slots(t, tr=_row_tile(*t.shape[-2:]), name="sum_chips") for t in got]


def kernel(*args):
    names = (["x", "c"] + WEIGHTS + ["loss_target"] + ["m_" + n for n in WEIGHTS] + ["v_" + n for n in WEIGHTS])
    assert len(args) == len(names)
    a = dict(zip(names, args))
    xi, yi, ci = lax.axis_index("x"), lax.axis_index("y"), lax.axis_index("c")
    me = 4 * xi + 2 * yi + ci

    small_in = [a["c"], a["ln_g"], a["ln_b"], a["ssd_conv_w"], a["sconv_w"]]
    buf, offs = _pack_rows(small_in)
    got, = _all_gather([buf], in_vmem=True, name="gather_small")
    c_all = _take(got, offs[0], (D,))
    full = {}
    for k, n in enumerate(["ln_g", "ln_b", "ssd_conv_w", "sconv_w"]):
        sh = a[n].shape
        t = _take(got, offs[k + 1], sh)
        full[n] = jnp.transpose(t, (1, 2, 0, 3)).reshape(sh[0], sh[1], N_DEV * sh[2])

    ncol = a["ada_w"].shape[-1]
    ada_b_cols = lax.dynamic_slice_in_dim(a["ada_b"], me * ncol, ncol, axis=1)[:, None, :]
    mod_cols = _ada_fwd(c_all, a["ada_w"], ada_b_cols)
    got, = _all_gather([mod_cols.reshape(DEPTH * N_DEV, ncol)], in_vmem=True, name="gather_mod")
    got = got.reshape(N_DEV, DEPTH, N_DEV, ncol)
    mod = lax.dynamic_index_in_dim(got, me, axis=2, keepdims=False)
    mod = jnp.transpose(mod, (1, 0, 2)).reshape(DEPTH, 9, D)

    shards = {n: a[n] for n in BIG}
    shards["mix_w_in"] = _pack_cols(a["mix_w_in"])
    def as_weights(got):
        return {n: (t if n in COL_SHARDED else t.reshape(-1, t.shape[-1])) for n, t in zip(BIG, got)}

    w0 = as_weights(_all_gather([shards[n][0].astype(bf16) for n in BIG], in_vmem=False, name="gather_weights"))
    w1_state, w1_token = _spread_start([shards[n][1].astype(bf16) for n in BIG], scatter=False, name="weights1_start")

    def weights_of(l, after):
        if l == 0:
            return w0
        return as_weights(_spread_wait(w1_state, after, scatter=False, name="weights1_wait"))

    rs = {}

    def grads_done(l, gw):
        g_layer = [gw[n].reshape((N_DEV,) + shards[n].shape[1:]) for n in BIG]
        if l == 0:
            rs[0] = _reduce_scatter_layer(g_layer, ci)
            return None
        rs[1], token = _spread_start(g_layer, scatter=True, name="grads1_start")
        return token

    p = {n: a[n] for n in ("ln_in_g", "ln_in_b", "ssd_conv_b", "ssd_dt_bias", "ssd_a_log", "ssd_d", "ssd_norm_g",
                           "fox_f_bias")}
    p.update(full)
    p["mod"] = mod + w1_token[0, 0]

    loss_local, grad_x, g = _local_step(a["x"][0], a["loss_target"][0], p, weights_of, grads_done)
    loss = lax.psum(loss_local, ("x", "y", "c"))

    small_names = ["mod", "ln_in_g", "ln_in_b", "ssd_conv_b", "ssd_dt_bias", "ssd_a_log", "ssd_d", "ssd_norm_g",
                   "fox_f_bias", "ln_g", "ln_b", "ssd_conv_w", "sconv_w"]
    buf, offs = _pack_rows([g[n] for n in small_names])
    got, = _all_gather([buf], in_vmem=True, name="gather_small_grads")
    tot = _sum_slots(got, tr=buf.shape[0], name="sum_small_grads")
    grads = {}
    for k, n in enumerate(small_names[1:], start=1):
        t = _take(tot, offs[k], g[n].shape)
        if n in SMALL_SHARDED:
            w_ = SMALL_SHARDED[n]
            t = lax.dynamic_slice_in_dim(t, me * w_, w_, axis=2)
        grads[n] = t
    grads["ada_b"] = _take(tot, offs[0], (DEPTH, 9 * D))
    dmod_all = _take(got, offs[0], (DEPTH, 9 * D))
    dmod_cols = jnp.transpose(lax.dynamic_slice_in_dim(dmod_all, me * ncol, ncol, axis=2), (1, 0, 2))
    grads["ada_w"] = _ada_bwd(c_all, dmod_cols)

    got1 = _spread_wait(rs[1], grad_x, scatter=True, name="grads1_wait")
    sum1 = [_sum_slots(t, tr=_row_tile(t.shape[-2], N_DEV // 2 * t.shape[-1]), name="sum_devices") for t in got1]
    for k, n in enumerate(BIG):
        grads[n] = jnp.stack([rs[0][k], sum1[k]])
    grads["mix_w_in"] = _unpack_cols(grads["mix_w_in"])

    delta, new_m, new_v = {}, {}, {}
    small_params = [n for n in WEIGHTS if n not in ADAM_TR]
    packs = [_pack_rows([src[pre + n] for n in small_params])[0]
             for src, pre in ((a, ""), (grads, ""), (a, "m_"), (a, "v_"))]
    _, offs = _pack_rows([a[n] for n in small_params])
    outs = _adamw(*packs, tr=packs[0].shape[0], name="adamw_small")
    for k, n in enumerate(small_params):
        delta[n], new_m[n], new_v[n] = (_take(t, offs[k], a[n].shape) for t in outs)
    for n, tr in ADAM_TR.items():
        sh = a[n].shape
        two = lambda t: t.reshape(-1, sh[-1])
        outs = _adamw(two(a[n]), two(grads[n]), two(a["m_" + n]), two(a["v_" + n]), tr=tr, name="adamw_" + n)
        delta[n], new_m[n], new_v[n] = (t.reshape(sh) for t in outs)

    return (loss, grad_x[None], *[grads[n] for n in WEIGHTS], *[delta[n] for n in WEIGHTS],
            *[new_m[n] for n in WEIGHTS], *[new_v[n] for n in WEIGHTS])
```

```python
import functools

import jax
import jax.numpy as jnp
from jax import lax
from jax.experimental import pallas as pl
from jax.experimental.pallas import tpu as pltpu

f32, bf16 = jnp.float32, jnp.bfloat16

D = 1024
F = 2816
DEPTH = 2
N_DEV = 8
SSD_W, SSD_HD, SSD_H, SSD_G, SSD_N, SSD_K = 512, 64, 8, 2, 128, 4
FOX_W, FOX_HD, FOX_H = 256, 64, 4
SC_W, SC_K = 256, 3
ALPHA = (2 * DEPTH) ** 0.25
LN_EPS = 1e-5
RMS_EPS = 1e-5
P_XBC, P_Z, P_Q, P_K, P_V, P_SB, P_SC, P_SX, P_SM = 0, 1024, 1536, 1792, 2048, 2304, 2560, 2816, 3072
P_W = 3200
SM_DT, SM_F = 0, 8
ADAM_LR, ADAM_B1, ADAM_B2, ADAM_EPS, ADAM_WD, ADAM_STEP = 0.001, 0.9, 0.999, 1e-08, 0.01, 10

VMEM_LIMIT = 56 * 1024 * 1024


def _cp(sem=None):
    return pltpu.CompilerParams(dimension_semantics=sem, vmem_limit_bytes=VMEM_LIMIT)


def _const_spec(shape):
    nd = len(shape)
    return pl.BlockSpec(shape, lambda *_: (0,) * nd, pipeline_mode=pl.Buffered(1))


def _sigmoid(x):
    return 1.0 / (1.0 + jnp.exp(-x))


def _ln_fwd(u, g, b):
    mu = jnp.mean(u, -1, keepdims=True)
    xc = u - mu
    rstd = lax.rsqrt(jnp.mean(xc * xc, -1, keepdims=True) + LN_EPS)
    xhat = xc * rstd
    return xhat * g + b, xhat, rstd


def _ln_bwd(dout, xhat, rstd, g):
    dxh = dout * g
    m1 = jnp.mean(dxh, -1, keepdims=True)
    m2 = jnp.mean(dxh * xhat, -1, keepdims=True)
    du = rstd * (dxh - m1 - xhat * m2)
    return du, jnp.sum(dout * xhat, 0, keepdims=True), jnp.sum(dout, 0, keepdims=True)


def _dot(a, b):
    return jnp.dot(a, b, preferred_element_type=f32)


def _dot_nt(a, b):
    return lax.dot_general(a, b, (((1,), (1,)), ((), ())), preferred_element_type=f32)


def _dot_tn(a, b):
    return lax.dot_general(a, b, (((0,), (0,)), ((), ())), preferred_element_type=f32)


def _dot_hi(a, b):
    return jnp.dot(a, b, preferred_element_type=f32, precision=lax.Precision.HIGHEST)


def _shift_down(cur, prev, s):
    if s == 0:
        return cur
    row = lax.broadcasted_iota(jnp.int32, cur.shape, 0)
    return jnp.where(row < s, pltpu.roll(prev, s, 0), pltpu.roll(cur, s, 0))


def _shift_up(cur, nxt, s):
    if s == 0:
        return cur
    t = cur.shape[0]
    row = lax.broadcasted_iota(jnp.int32, cur.shape, 0)
    return jnp.where(row >= t - s, pltpu.roll(nxt, t - s, 0), pltpu.roll(cur, t - s, 0))


def _ln_in_fwd(x, gb, *, tt=512):
    L = x.shape[0]

    def body(x_ref, gb_ref, o_ref):
        o_ref[...] = _ln_fwd(x_ref[...], gb_ref[0:1, :], gb_ref[1:2, :])[0]

    return pl.pallas_call(
        body, name="ln_in_fwd", grid=(L // tt,),
        in_specs=[pl.BlockSpec((tt, D), lambda i: (i, 0)), _const_spec((8, D))],
        out_specs=pl.BlockSpec((tt, D), lambda i: (i, 0)),
        out_shape=jax.ShapeDtypeStruct((L, D), f32), compiler_params=_cp(("parallel",)))(x, gb)


def _ln_in_bwd(x, dy, gb, *, tt=512):
    L = x.shape[0]

    def body(x_ref, dy_ref, gb_ref, dx_ref, acc_ref):
        @pl.when(pl.program_id(0) == 0)
        def _():
            acc_ref[...] = jnp.zeros_like(acc_ref)
        _, xhat, rstd = _ln_fwd(x_ref[...], gb_ref[0:1, :], gb_ref[1:2, :])
        du, dg, db = _ln_bwd(dy_ref[...], xhat, rstd, gb_ref[0:1, :])
        dx_ref[...] = du
        acc_ref[0:1, :] += dg
        acc_ref[1:2, :] += db

    return pl.pallas_call(
        body, name="ln_in_bwd", grid=(L // tt,),
        in_specs=[pl.BlockSpec((tt, D), lambda i: (i, 0)), pl.BlockSpec((tt, D), lambda i: (i, 0)), _const_spec((8, D))],
        out_specs=[pl.BlockSpec((tt, D), lambda i: (i, 0)), pl.BlockSpec((8, D), lambda i: (0, 0))],
        out_shape=[jax.ShapeDtypeStruct((L, D), f32), jax.ShapeDtypeStruct((8, D), f32)],
        compiler_params=_cp(("arbitrary",)))(x, dy, gb)


def _loss_head(y, tgt, *, tt=512):
    L = y.shape[0]

    def body(y_ref, t_ref, dy_ref, acc_ref):
        @pl.when(pl.program_id(0) == 0)
        def _():
            acc_ref[...] = jnp.zeros_like(acc_ref)
        e = y_ref[...] - t_ref[...]
        dy_ref[...] = e * (1.0 / D)
        acc_ref[...] += 0.5 * jnp.sum(jnp.mean(e * e, -1, keepdims=True))

    return pl.pallas_call(
        body, name="loss_head", grid=(L // tt,),
        in_specs=[pl.BlockSpec((tt, D), lambda i: (i, 0)), pl.BlockSpec((tt, D), lambda i: (i, 0))],
        out_specs=[pl.BlockSpec((tt, D), lambda i: (i, 0)), pl.BlockSpec((8, 128), lambda i: (0, 0))],
        out_shape=[jax.ShapeDtypeStruct((L, D), f32), jax.ShapeDtypeStruct((8, 128), f32)],
        compiler_params=_cp(("arbitrary",)))(y, tgt)


FFN_CH = 4
FS = F // FFN_CH


def _ffn_fwd(x, mv, w_in, w_out, *, tt=256):
    L = x.shape[0]

    def body(x_ref, mv_ref, wi_ref, wo_ref, o_ref):
        x = x_ref[...]
        h = (x * (1.0 + mv_ref[1:2, :]) + mv_ref[0:1, :]).astype(bf16)
        y = jnp.zeros((tt, D), f32)
        for c in range(FFN_CH):
            g = _dot(h, wi_ref[c])
            u = _dot(h, wi_ref[c + FFN_CH])
            act = (g * _sigmoid(g) * u).astype(bf16)
            y = y + _dot(act, wo_ref[c * FS:(c + 1) * FS, :])
        uu = ALPHA * x + (0.5 * mv_ref[2:3, :]) * y
        o_ref[...] = _ln_fwd(uu, mv_ref[3:4, :], mv_ref[4:5, :])[0]

    return pl.pallas_call(
        body, name="ffn_fwd", grid=(L // tt,),
        in_specs=[pl.BlockSpec((tt, D), lambda i: (i, 0)), _const_spec((8, D)),
                  _const_spec((2 * FFN_CH, D, FS)), _const_spec((F, D))],
        out_specs=pl.BlockSpec((tt, D), lambda i: (i, 0)),
        out_shape=jax.ShapeDtypeStruct((L, D), f32), compiler_params=_cp(("parallel",)))(x, mv, w_in, w_out)


def _ffn_bwd(x, dxo, mv, w_in, w_out, *, tt=256):
    L = x.shape[0]

    def body(x_ref, dxo_ref, mv_ref, wi_ref, wo_ref, dx_ref, h_ref, da_ref, act_ref, dy_ref, acc_ref, a_scr):
        @pl.when(pl.program_id(0) == 0)
        def _():
            acc_ref[...] = jnp.zeros_like(acc_ref)
        x = x_ref[...]
        scale1 = 1.0 + mv_ref[1:2, :]
        h = (x * scale1 + mv_ref[0:1, :]).astype(bf16)
        h_ref[...] = h
        y = jnp.zeros((tt, D), f32)
        for c in range(FFN_CH):
            g = _dot(h, wi_ref[c])
            u = _dot(h, wi_ref[c + FFN_CH])
            a_scr[c] = g
            a_scr[c + FFN_CH] = u
            act = (g * _sigmoid(g) * u).astype(bf16)
            act_ref[c] = act
            y = y + _dot(act, wo_ref[c * FS:(c + 1) * FS, :])
        hg = 0.5 * mv_ref[2:3, :]
        _, xhat, rstd = _ln_fwd(ALPHA * x + hg * y, mv_ref[3:4, :], mv_ref[4:5, :])
        du, dlg, dlb = _ln_bwd(dxo_ref[...], xhat, rstd, mv_ref[3:4, :])
        acc_ref[3:4, :] += dlg
        acc_ref[4:5, :] += dlb
        acc_ref[2:3, :] += jnp.sum(0.5 * y * du, 0, keepdims=True)
        dyb = (hg * du).astype(bf16)
        dy_ref[...] = dyb
        dh = jnp.zeros((tt, D), f32)
        for c in range(FFN_CH):
            g = a_scr[c]
            u = a_scr[c + FFN_CH]
            dact = _dot_nt(dyb, wo_ref[c * FS:(c + 1) * FS, :])
            s = _sigmoid(g)
            dg = (dact * u * (s * (1.0 + g * (1.0 - s)))).astype(bf16)
            dup = (dact * (g * s)).astype(bf16)
            da_ref[c] = dg
            da_ref[c + FFN_CH] = dup
            dh = dh + _dot_nt(dg, wi_ref[c])
            dh = dh + _dot_nt(dup, wi_ref[c + FFN_CH])
        dx_ref[...] = ALPHA * du + dh * scale1
        acc_ref[0:1, :] += jnp.sum(dh, 0, keepdims=True)
        acc_ref[1:2, :] += jnp.sum(dh * x, 0, keepdims=True)

    tok = lambda w: pl.BlockSpec((tt, w), lambda i: (i, 0))
    by_chunk = lambda n: pl.BlockSpec((n, tt, FS), lambda i: (0, i, 0))
    return pl.pallas_call(
        body, name="ffn_bwd", grid=(L // tt,),
        in_specs=[tok(D), tok(D), _const_spec((8, D)), _const_spec((2 * FFN_CH, D, FS)), _const_spec((F, D))],
        out_specs=[tok(D), tok(D), by_chunk(2 * FFN_CH), by_chunk(FFN_CH), tok(D), pl.BlockSpec((8, D), lambda i: (0, 0))],
        out_shape=[jax.ShapeDtypeStruct((L, D), f32), jax.ShapeDtypeStruct((L, D), bf16),
                   jax.ShapeDtypeStruct((2 * FFN_CH, L, FS), bf16), jax.ShapeDtypeStruct((FFN_CH, L, FS), bf16),
                   jax.ShapeDtypeStruct((L, D), bf16), jax.ShapeDtypeStruct((8, D), f32)],
        scratch_shapes=[pltpu.VMEM((2 * FFN_CH, tt, FS), f32)],
        compiler_params=_cp(("arbitrary",)))(x, dxo, mv, w_in, w_out)


def _matmul_tn(a, b, *, tn, tk, name):
    ga, gb = a.ndim == 3, b.ndim == 3
    G = a.shape[0] if ga else (b.shape[0] if gb else 1)
    K, M = a.shape[-2:]
    N = b.shape[-1]
    nk = K // tk

    def body(a_ref, b_ref, o_ref, acc):
        k = pl.program_id(2)
        p = _dot_tn(a_ref[...], b_ref[...])

        @pl.when(k == 0)
        def _():
            acc[...] = p

        @pl.when(k > 0)
        def _():
            acc[...] += p

        @pl.when(k == nk - 1)
        def _():
            o_ref[...] = acc[...].astype(bf16)

    a_spec = (pl.BlockSpec((None, tk, M), lambda g, j, k: (g, k, 0)) if ga
              else pl.BlockSpec((tk, M), lambda g, j, k: (k, 0)))
    b_spec = (pl.BlockSpec((None, tk, tn), lambda g, j, k: (g, k, j)) if gb
              else pl.BlockSpec((tk, tn), lambda g, j, k: (k, j)))
    if ga or gb:
        o_spec, o_shape = pl.BlockSpec((None, M, tn), lambda g, j, k: (g, 0, j)), (G, M, N)
    else:
        o_spec, o_shape = pl.BlockSpec((M, tn), lambda g, j, k: (0, j)), (M, N)
    return pl.pallas_call(
        body, name=name, grid=(G, N // tn, nk), in_specs=[a_spec, b_spec], out_specs=o_spec,
        out_shape=jax.ShapeDtypeStruct(o_shape, bf16), scratch_shapes=[pltpu.VMEM((M, tn), f32)],
        compiler_params=_cp(("parallel", "parallel", "arbitrary")))(a, b)


def _inproj_fwd(x, mv, w, *, tt=512):
    L = x.shape[0]

    def body(x_ref, mv_ref, w_ref, o_ref):
        h = (x_ref[...] * (1.0 + mv_ref[1:2, :]) + mv_ref[0:1, :]).astype(bf16)
        o_ref[...] = _dot(h, w_ref[...])

    return pl.pallas_call(
        body, name="inproj_fwd", grid=(L // tt,),
        in_specs=[pl.BlockSpec((tt, D), lambda i: (i, 0)), _const_spec((8, D)), _const_spec((D, P_W))],
        out_specs=pl.BlockSpec((tt, P_W), lambda i: (i, 0)),
        out_shape=jax.ShapeDtypeStruct((L, P_W), f32), compiler_params=_cp(("parallel",)))(x, mv, w)


def _inproj_bwd(x, dx_part, dproj, mv, w, *, tt=512):
    L = x.shape[0]

    def body(x_ref, dxp_ref, dp_ref, mv_ref, w_ref, dx_ref, h_ref, acc_ref):
        @pl.when(pl.program_id(0) == 0)
        def _():
            acc_ref[...] = jnp.zeros_like(acc_ref)
        x = x_ref[...]
        scale1 = 1.0 + mv_ref[1:2, :]
        h_ref[...] = (x * scale1 + mv_ref[0:1, :]).astype(bf16)
        dh = _dot_nt(dp_ref[...], w_ref[...])
        dx_ref[...] = dxp_ref[...] + dh * scale1
        acc_ref[0:1, :] += jnp.sum(dh, 0, keepdims=True)
        acc_ref[1:2, :] += jnp.sum(dh * x, 0, keepdims=True)

    tok = lambda w_: pl.BlockSpec((tt, w_), lambda i: (i, 0))
    return pl.pallas_call(
        body, name="inproj_bwd", grid=(L // tt,),
        in_specs=[tok(D), tok(D), tok(P_W), _const_spec((8, D)), _const_spec((D, P_W))],
        out_specs=[tok(D), tok(D), pl.BlockSpec((8, D), lambda i: (0, 0))],
        out_shape=[jax.ShapeDtypeStruct((L, D), f32), jax.ShapeDtypeStruct((L, D), bf16),
                   jax.ShapeDtypeStruct((8, D), f32)],
        compiler_params=_cp(("arbitrary",)))(x, dx_part, dproj, mv, w)


def _outproj_fwd(x, ycat, mv, w, *, tt=512):
    L = x.shape[0]

    def body(x_ref, y_ref, mv_ref, w_ref, o_ref):
        y = _dot(y_ref[...], w_ref[...])
        uu = ALPHA * x_ref[...] + mv_ref[2:3, :] * y
        o_ref[...] = _ln_fwd(uu, mv_ref[3:4, :], mv_ref[4:5, :])[0]

    tok = lambda w_: pl.BlockSpec((tt, w_), lambda i: (i, 0))
    return pl.pallas_call(
        body, name="outproj_fwd", grid=(L // tt,),
        in_specs=[tok(D), tok(D), _const_spec((8, D)), _const_spec((D, D))],
        out_specs=tok(D),
        out_shape=jax.ShapeDtypeStruct((L, D), f32), compiler_params=_cp(("parallel",)))(x, ycat, mv, w)


def _outproj_bwd(x, ycat, dxo, mv, w, *, tt=512):
    L = x.shape[0]

    def body(x_ref, y_ref, dxo_ref, mv_ref, w_ref, dx_ref, dy_ref, dyc_ref, acc_ref):
        @pl.when(pl.program_id(0) == 0)
        def _():
            acc_ref[...] = jnp.zeros_like(acc_ref)
        y = _dot(y_ref[...], w_ref[...])
        gate = mv_ref[2:3, :]
        _, xhat, rstd = _ln_fwd(ALPHA * x_ref[...] + gate * y, mv_ref[3:4, :], mv_ref[4:5, :])
        du, dlg, dlb = _ln_bwd(dxo_ref[...], xhat, rstd, mv_ref[3:4, :])
        acc_ref[3:4, :] += dlg
        acc_ref[4:5, :] += dlb
        acc_ref[2:3, :] += jnp.sum(y * du, 0, keepdims=True)
        dx_ref[...] = ALPHA * du
        dyb = (gate * du).astype(bf16)
        dy_ref[...] = dyb
        dyc_ref[...] = _dot_nt(dyb, w_ref[...])

    tok = lambda w_: pl.BlockSpec((tt, w_), lambda i: (i, 0))
    return pl.pallas_call(
        body, name="outproj_bwd", grid=(L // tt,),
        in_specs=[tok(D), tok(D), tok(D), _const_spec((8, D)), _const_spec((D, D))],
        out_specs=[tok(D), tok(D), tok(D), pl.BlockSpec((8, D), lambda i: (0, 0))],
        out_shape=[jax.ShapeDtypeStruct((L, D), f32), jax.ShapeDtypeStruct((L, D), bf16),
                   jax.ShapeDtypeStruct((L, D), f32), jax.ShapeDtypeStruct((8, D), f32)],
        compiler_params=_cp(("arbitrary",)))(x, ycat, dxo, mv, w)


def _sconv_fwd(proj, w, *, tt=512):
    L = proj.shape[0]
    cb = SC_W

    def body(b_ref, c_ref, x_ref, cp_ref, xp_ref, w_ref, o_ref):
        first = jnp.where(pl.program_id(0) > 0, 1.0, 0.0)
        u = c_ref[...] * x_ref[...]
        up = cp_ref[...] * xp_ref[...] * first
        v = w_ref[2:3, :] * u + w_ref[1:2, :] * _shift_down(u, up, 1) + w_ref[0:1, :] * _shift_down(u, up, 2)
        o_ref[...] = (b_ref[...] * v).astype(bf16)

    cur = lambda col: pl.BlockSpec((tt, cb), lambda i: (i, col // cb))
    prev = lambda col: pl.BlockSpec((tt, cb), lambda i: (jnp.maximum(i - 1, 0), col // cb))
    return pl.pallas_call(
        body, name="sconv_fwd", grid=(L // tt,),
        in_specs=[cur(P_SB), cur(P_SC), cur(P_SX), prev(P_SC), prev(P_SX), _const_spec((8, cb))],
        out_specs=pl.BlockSpec((tt, cb), lambda i: (i, 0)),
        out_shape=jax.ShapeDtypeStruct((L, cb), bf16), compiler_params=_cp(("parallel",)))(proj, proj, proj, proj, proj, w)


def _sconv_bwd(proj, dycat, w, *, tt=512):
    L = proj.shape[0]
    cb = SC_W
    n = L // tt

    def body(b_ref, c_ref, x_ref, cp_ref, xp_ref, bn_ref, dy_ref, dyn_ref, w_ref, db_ref, dc_ref, dx_ref, acc_ref):
        i = pl.program_id(0)

        @pl.when(i == 0)
        def _():
            acc_ref[...] = jnp.zeros_like(acc_ref)
        first = jnp.where(i > 0, 1.0, 0.0)
        last = jnp.where(i < n - 1, 1.0, 0.0)
        cg, xin, bg = c_ref[...], x_ref[...], b_ref[...]
        u = cg * xin
        up = cp_ref[...] * xp_ref[...] * first
        u1, u2 = _shift_down(u, up, 1), _shift_down(u, up, 2)
        v = w_ref[2:3, :] * u + w_ref[1:2, :] * u1 + w_ref[0:1, :] * u2
        dy = dy_ref[...]
        db_ref[...] = (dy * v).astype(bf16)
        dv = dy * bg
        dvn = dyn_ref[...] * bn_ref[...] * last
        du = w_ref[2:3, :] * dv + w_ref[1:2, :] * _shift_up(dv, dvn, 1) + w_ref[0:1, :] * _shift_up(dv, dvn, 2)
        acc_ref[2:3, :] += jnp.sum(dv * u, 0, keepdims=True)
        acc_ref[1:2, :] += jnp.sum(dv * u1, 0, keepdims=True)
        acc_ref[0:1, :] += jnp.sum(dv * u2, 0, keepdims=True)
        dc_ref[...] = (du * xin).astype(bf16)
        dx_ref[...] = (du * cg).astype(bf16)

    cur = lambda col: pl.BlockSpec((tt, cb), lambda i: (i, col // cb))
    prev = lambda col: pl.BlockSpec((tt, cb), lambda i: (jnp.maximum(i - 1, 0), col // cb))
    nxt = lambda col: pl.BlockSpec((tt, cb), lambda i: (jnp.minimum(i + 1, n - 1), col // cb))
    ycol = SSD_W + FOX_W
    out = pl.BlockSpec((tt, cb), lambda i: (i, 0))
    return pl.pallas_call(
        body, name="sconv_bwd", grid=(n,),
        in_specs=[cur(P_SB), cur(P_SC), cur(P_SX), prev(P_SC), prev(P_SX), nxt(P_SB), cur(ycol), nxt(ycol),
                  _const_spec((8, cb))],
        out_specs=[out, out, out, pl.BlockSpec((8, cb), lambda i: (0, 0))],
        out_shape=[jax.ShapeDtypeStruct((L, cb), bf16)] * 3 + [jax.ShapeDtypeStruct((8, cb), f32)],
        compiler_params=_cp(("arbitrary",)))(proj, proj, proj, proj, proj, proj, dycat, dycat, w)


def _log1pexp(x):
    return jnp.log(1.0 + jnp.exp(-jnp.abs(x)))


def _fox_gate_fwd(proj, hp, *, tt=256):
    L = proj.shape[0]

    def body(sm_ref, hp_ref, cum_ref, cumt_ref, carry):
        @pl.when(pl.program_id(0) == 0)
        def _():
            carry[...] = jnp.zeros_like(carry)
        xx = sm_ref[...] + hp_ref[3:4, :]
        logf = jnp.minimum(xx, 0.0) - _log1pexp(xx)
        r = lax.broadcasted_iota(jnp.int32, (tt, tt), 0)
        c = lax.broadcasted_iota(jnp.int32, (tt, tt), 1)
        cum = _dot_hi(jnp.where(r >= c, 1.0, 0.0), logf) + carry[0:1, :]
        cum_ref[...] = cum
        cumt_ref[...] = cum.T
        carry[0:1, :] = cum[tt - 1:tt, :]

    return pl.pallas_call(
        body, name="fox_gate_fwd", grid=(L // tt,),
        in_specs=[pl.BlockSpec((tt, 128), lambda i: (i, P_SM // 128)), _const_spec((8, 128))],
        out_specs=[pl.BlockSpec((tt, 128), lambda i: (i, 0)), pl.BlockSpec((128, tt), lambda i: (0, i))],
        out_shape=[jax.ShapeDtypeStruct((L, 128), f32), jax.ShapeDtypeStruct((128, L), f32)],
        scratch_shapes=[pltpu.VMEM((8, 128), f32)],
        compiler_params=_cp(("arbitrary",)))(proj, hp)


def _fox_gate_bwd(dcumt, proj, ddt, hp, *, tt=256):
    L = proj.shape[0]
    n = L // tt

    def body(dct_ref, sm_ref, ddt_ref, hp_ref, dsm_ref, acc_ref, carry):
        @pl.when(pl.program_id(0) == 0)
        def _():
            carry[...] = jnp.zeros_like(carry)
            acc_ref[...] = jnp.zeros_like(acc_ref)
        dc = dct_ref[...].T
        r = lax.broadcasted_iota(jnp.int32, (tt, tt), 0)
        c = lax.broadcasted_iota(jnp.int32, (tt, tt), 1)
        dl = _dot_hi(jnp.where(r <= c, 1.0, 0.0), dc) + carry[0:1, :]
        carry[0:1, :] += jnp.sum(dc, 0, keepdims=True)
        xx = sm_ref[...] + hp_ref[3:4, :]
        lane = lax.broadcasted_iota(jnp.int32, (tt, 128), 1)
        dlogit = jnp.where((lane >= SM_F) & (lane < SM_F + FOX_H), dl * _sigmoid(-xx), 0.0)
        acc_ref[3:4, :] += jnp.sum(dlogit, 0, keepdims=True)
        dsm_ref[...] = (dlogit + ddt_ref[...]).astype(bf16)

    return pl.pallas_call(
        body, name="fox_gate_bwd", grid=(n,),
        in_specs=[pl.BlockSpec((128, tt), lambda i: (0, n - 1 - i)),
                  pl.BlockSpec((tt, 128), lambda i: (n - 1 - i, P_SM // 128)),
                  pl.BlockSpec((tt, 128), lambda i: (n - 1 - i, 0)), _const_spec((8, 128))],
        out_specs=[pl.BlockSpec((tt, 128), lambda i: (n - 1 - i, 0)), pl.BlockSpec((8, 128), lambda i: (0, 0))],
        out_shape=[jax.ShapeDtypeStruct((L, 128), bf16), jax.ShapeDtypeStruct((8, 128), f32)],
        scratch_shapes=[pltpu.VMEM((8, 128), f32)],
        compiler_params=_cp(("arbitrary",)))(dcumt, proj, ddt, hp)


NEG = -1e30
FOX_SCALE = FOX_HD ** -0.5


def _fox_fwd(proj, cum, cumt, *, tq=256):
    L = proj.shape[0]
    heads = [slice(h * FOX_HD, (h + 1) * FOX_HD) for h in range(FOX_H)]

    def body(q_ref, k_ref, v_ref, cq_ref, ct_ref, o_ref, oa_ref, lse_ref):
        i = pl.program_id(0)
        row = lax.broadcasted_iota(jnp.int32, (tq, tq), 0)
        col = lax.broadcasted_iota(jnp.int32, (tq, tq), 1)
        diag_bias = jnp.where(row >= col, 0.0, NEG)
        qs = [(q_ref[:, hs] * FOX_SCALE).astype(bf16) for hs in heads]
        cqs = [cq_ref[:, SM_F + h:SM_F + h + 1] for h in range(FOX_H)]

        def block(r0, carry, bias):
            out = []
            for h, hs in enumerate(heads):
                m, l, acc, acc_lo = carry[h]
                kj = k_ref[pl.ds(r0, tq), hs].astype(bf16)
                vj = v_ref[pl.ds(r0, tq), hs].astype(bf16)
                s = _dot_nt(qs[h], kj) + (cqs[h] - ct_ref[SM_F + h:SM_F + h + 1, pl.ds(r0, tq)])
                if bias is not None:
                    s = s + bias
                m_new = jnp.maximum(m, jnp.max(s, -1, keepdims=True))
                p = jnp.exp(s - m_new)
                al = jnp.exp(m - m_new)
                pb = p.astype(bf16)
                p_lo = (p - pb.astype(f32)).astype(bf16)
                out.append((m_new, al * l + jnp.sum(p, -1, keepdims=True),
                            al * acc + _dot(pb, vj), al * acc_lo + _dot(p_lo, vj)))
            return tuple(out)

        init = tuple((jnp.full((tq, 1), NEG, f32), jnp.zeros((tq, 1), f32), jnp.zeros((tq, FOX_HD), f32),
                      jnp.zeros((tq, FOX_HD), f32)) for _ in heads)
        carry = lax.fori_loop(0, i, lambda j, c: block(pl.multiple_of(j * tq, tq), c, None), init)
        carry = block(pl.multiple_of(i * tq, tq), carry, diag_bias)
        lane = lax.broadcasted_iota(jnp.int32, (tq, 128), 1)
        lse_all = jnp.zeros((tq, 128), f32)
        for h, hs in enumerate(heads):
            m, l, acc, acc_lo = carry[h]
            inv = 1.0 / l
            o_ref[:, hs] = acc * inv
            oa_ref[:, hs] = (acc + acc_lo) * inv
            lse_all = jnp.where(lane == h, m + jnp.log(l), lse_all)
        lse_ref[...] = lse_all

    return pl.pallas_call(
        body, name="fox_fwd", grid=(L // tq,),
        in_specs=[pl.BlockSpec((tq, FOX_W), lambda i: (i, P_Q // FOX_W)),
                  pl.BlockSpec((L, FOX_W), lambda i: (0, P_K // FOX_W), pipeline_mode=pl.Buffered(1)),
                  pl.BlockSpec((L, FOX_W), lambda i: (0, P_V // FOX_W), pipeline_mode=pl.Buffered(1)),
                  pl.BlockSpec((tq, 128), lambda i: (i, 0)), _const_spec((128, L))],
        out_specs=[pl.BlockSpec((tq, FOX_W), lambda i: (i, 0)), pl.BlockSpec((tq, FOX_W), lambda i: (i, 0)),
                   pl.BlockSpec((tq, 128), lambda i: (i, 0))],
        out_shape=[jax.ShapeDtypeStruct((L, FOX_W), f32), jax.ShapeDtypeStruct((L, FOX_W), f32),
                   jax.ShapeDtypeStruct((L, 128), f32)],
        compiler_params=_cp(("parallel",)))(proj, proj, proj, cum, cumt)


def _fox_bwd(proj, dycat, o_acc, lse, cum, cumt, *, tq=256):
    L = proj.shape[0]
    nq = L // tq
    heads = [slice(h * FOX_HD, (h + 1) * FOX_HD) for h in range(FOX_H)]

    def body(q_ref, k_ref, v_ref, do_ref, oa_ref, lse_ref, cum_ref, ct_ref, dq_ref, dk_ref, dv_ref, dct_ref):
        j = pl.program_id(0)

        @pl.when(j == 0)
        def _():
            dq_ref[...] = jnp.zeros_like(dq_ref)
        row = lax.broadcasted_iota(jnp.int32, (tq, tq), 0)
        col = lax.broadcasted_iota(jnp.int32, (tq, tq), 1)
        diag_bias = jnp.where(row >= col, 0.0, NEG)
        kjs = [k_ref[:, hs].astype(bf16) for hs in heads]
        vjs = [v_ref[:, hs].astype(bf16) for hs in heads]
        cks = [ct_ref[SM_F + h:SM_F + h + 1, :] for h in range(FOX_H)]

        def block(r0, carry, bias):
            out = []
            for h, hs in enumerate(heads):
                dk, dv, dc = carry[h]
                qi = (q_ref[pl.ds(r0, tq), hs] * FOX_SCALE).astype(bf16)
                dob = do_ref[pl.ds(r0, tq), hs].astype(bf16)
                delta = jnp.sum(dob.astype(f32) * oa_ref[pl.ds(r0, tq), hs], -1, keepdims=True)
                s = _dot_nt(qi, kjs[h]) + (cum_ref[pl.ds(r0, tq), SM_F + h:SM_F + h + 1] - cks[h])
                if bias is not None:
                    s = s + bias
                p = jnp.exp(s - lse_ref[pl.ds(r0, tq), h:h + 1])
                ds = p * (_dot_nt(dob, vjs[h]) - delta)
                dsb = ds.astype(bf16)
                dq_ref[pl.ds(r0, tq), hs] += _dot(dsb, kjs[h]) * FOX_SCALE
                out.append((dk + _dot_tn(dsb, qi), dv + _dot_tn(p.astype(bf16), dob),
                            dc - jnp.sum(ds, 0, keepdims=True)))
            return tuple(out)

        init = tuple((jnp.zeros((tq, FOX_HD), f32), jnp.zeros((tq, FOX_HD), f32), jnp.zeros((1, tq), f32))
                     for _ in heads)
        carry = block(pl.multiple_of(j * tq, tq), init, diag_bias)
        carry = lax.fori_loop(j + 1, nq, lambda i, c: block(pl.multiple_of(i * tq, tq), c, None), carry)
        sub = lax.broadcasted_iota(jnp.int32, (128, tq), 0)
        dct = jnp.zeros((128, tq), f32)
        for h, hs in enumerate(heads):
            dk, dv, dc = carry[h]
            dk_ref[:, hs] = dk
            dv_ref[:, hs] = dv
            dct = jnp.where(sub == SM_F + h, dc, dct)
        dct_ref[...] = dct

    full = lambda w_, col: pl.BlockSpec((L, w_), lambda j: (0, col // w_), pipeline_mode=pl.Buffered(1))
    blk = lambda col: pl.BlockSpec((tq, FOX_W), lambda j: (j, col // FOX_W))
    return pl.pallas_call(
        body, name="fox_bwd", grid=(nq,),
        in_specs=[full(FOX_W, P_Q), blk(P_K), blk(P_V), full(FOX_W, SSD_W), full(FOX_W, 0), full(128, 0), full(128, 0),
                  pl.BlockSpec((128, tq), lambda j: (0, j))],
        out_specs=[pl.BlockSpec((L, FOX_W), lambda j: (0, 0)), pl.BlockSpec((tq, FOX_W), lambda j: (j, 0)),
                   pl.BlockSpec((tq, FOX_W), lambda j: (j, 0)), pl.BlockSpec((128, tq), lambda j: (0, j))],
        out_shape=[jax.ShapeDtypeStruct((L, FOX_W), f32)] * 3 + [jax.ShapeDtypeStruct((128, L), f32)],
        compiler_params=_cp(("arbitrary",)))(proj, proj, proj, dycat, o_acc, lse, cum, cumt)


HL = 128
AW = FOX_H * HL


def _np_place(rows, cols, pairs, dtype):
    import numpy as np
    m = np.zeros((rows, cols), np.float32)
    for r, c in pairs:
        m[r, c] = 1.0
    return jnp.asarray(m, dtype)


def _fox_consts():
    data = [(h * FOX_HD + d, h * HL + d) for h in range(FOX_H) for d in range(FOX_HD)]
    return dict(
        pq=_np_place(FOX_W, AW, data, bf16),
        pqt=_np_place(AW, FOX_W, [(c, r) for r, c in data], bf16),
        cum_a=[_np_place(128, AW, [(SM_F + h, h * HL + 64 + r) for h in range(FOX_H)], bf16) for r in range(3)],
        head_a=[_np_place(128, AW, [(h, h * HL + 64 + r) for h in range(FOX_H)], bf16) for r in range(3)],
        head_b=[_np_place(128, AW, [(h, h * HL + 67 + r) for h in range(FOX_H)], bf16) for r in range(3)],
        group=_np_place(FOX_W, 128, [(h * FOX_HD + d, h) for h in range(FOX_H) for d in range(FOX_HD)], f32),
        col_a=_np_place(AW, 128, [(h * HL + 64, SM_F + h) for h in range(FOX_H)], f32))


def _split3(x):
    hi = x.astype(bf16)
    r1 = x - hi.astype(f32)
    mid = r1.astype(bf16)
    return hi, mid, (r1 - mid.astype(f32)).astype(bf16)


def _slot_ones(tt, first):
    lane = lax.broadcasted_iota(jnp.int32, (tt, AW), 1) % HL
    return jnp.where((lane >= first) & (lane < first + 3), 1.0, 0.0)


def _fox_prep(proj, hp, cst, *, tt=256):
    L = proj.shape[0]

    def body(q_ref, k_ref, v_ref, sm_ref, hp_ref, pq_ref, c0_ref, c1_ref, c2_ref, qa_ref, ka_ref, va_ref, tot_ref):
        xx = sm_ref[...] + hp_ref[3:4, :]
        logf = jnp.minimum(xx, 0.0) - _log1pexp(xx)
        r = lax.broadcasted_iota(jnp.int32, (tt, tt), 0)
        c = lax.broadcasted_iota(jnp.int32, (tt, tt), 1)
        cum = _dot_hi(jnp.where(r >= c, 1.0, 0.0), logf)
        tot_ref[...] = jnp.broadcast_to(cum[tt - 1:tt, :], (8, 128))
        parts = _split3(-cum)
        pq = pq_ref[...]
        a_ones, b_ones = _slot_ones(tt, 64), _slot_ones(tt, 67)
        qa_ref[...] = (_dot((q_ref[...] * FOX_SCALE).astype(bf16), pq) + a_ones).astype(bf16)
        ka = _dot(k_ref[...].astype(bf16), pq) + b_ones
        for part, c_ref in zip(parts, (c0_ref, c1_ref, c2_ref)):
            ka = ka + _dot(part, c_ref[...])
        ka_ref[...] = ka.astype(bf16)
        va_ref[...] = (_dot(v_ref[...].astype(bf16), pq) + a_ones).astype(bf16)

    col = lambda c_: pl.BlockSpec((tt, FOX_W), lambda i: (i, c_ // FOX_W))
    out = pl.BlockSpec((tt, AW), lambda i: (i, 0))
    return pl.pallas_call(
        body, name="fox_prep", grid=(L // tt,),
        in_specs=[col(P_Q), col(P_K), col(P_V), pl.BlockSpec((tt, 128), lambda i: (i, P_SM // 128)),
                  _const_spec((8, 128)), _const_spec((FOX_W, AW))] + [_const_spec((128, AW))] * 3,
        out_specs=[out, out, out, pl.BlockSpec((8, 128), lambda i: (i, 0))],
        out_shape=[jax.ShapeDtypeStruct((L, AW), bf16)] * 3 + [jax.ShapeDtypeStruct((8 * (L // tt), 128), f32)],
        compiler_params=_cp(("parallel",)))(proj, proj, proj, proj, hp, cst["pq"], *cst["cum_a"])


def _fox_attn_fwd(qa, ka, va, tot, *, tq=256):
    L = qa.shape[0]

    def body(qa_ref, ka_ref, va_ref, tot_ref, o_ref, oa_ref, lse_ref):
        i = pl.program_id(0)
        row = lax.broadcasted_iota(jnp.int32, (tq, tq), 0)
        col = lax.broadcasted_iota(jnp.int32, (tq, tq), 1)
        diag_bias = jnp.where(row >= col, 0.0, NEG)

        def block(j, carry, gap, bias):
            r0 = pl.multiple_of(j * tq, tq)
            out = []
            for h in range(FOX_H):
                hl = slice(h * HL, (h + 1) * HL)
                m, acc = carry[h]
                s = _dot_nt(qa_ref[:, hl], ka_ref[pl.ds(r0, tq), hl])
                if bias is not None:
                    s = s + bias
                gap_h = gap[:, SM_F + h:SM_F + h + 1]
                m_new = jnp.maximum(m, jnp.max(s, -1, keepdims=True) + gap_h)
                p = jnp.exp(s - (m_new - gap_h))
                al = jnp.exp(m - m_new)
                pb = p.astype(bf16)
                p_lo = (p - pb.astype(f32)).astype(bf16)
                vj = va_ref[pl.ds(r0, tq), hl]
                out.append((m_new, al * acc + (_dot(pb, vj) + _dot(p_lo, vj))))
            return tuple(out)

        def step(k, state):
            carry, gap = state
            j = i - 1 - k
            gap = gap + tot_ref[pl.ds(pl.multiple_of(j * 8, 8), 1), :]
            return block(j, carry, gap, None), gap

        init = tuple((jnp.full((tq, 1), NEG, f32), jnp.zeros((tq, HL), f32)) for _ in range(FOX_H))
        zero_gap = jnp.zeros((1, 128), f32)
        carry = block(i, init, zero_gap, diag_bias)
        carry, _ = lax.fori_loop(0, i, step, (carry, zero_gap))
        lane = lax.broadcasted_iota(jnp.int32, (tq, 128), 1)
        lse_all = jnp.zeros((tq, 128), f32)
        for h in range(FOX_H):
            hs = slice(h * FOX_HD, (h + 1) * FOX_HD)
            m, acc = carry[h]
            l = acc[:, FOX_HD:FOX_HD + 1]
            o = acc[:, :FOX_HD] * (1.0 / l)
            o_ref[:, hs] = o.astype(bf16)
            oa_ref[:, hs] = o
            lse_all = jnp.where(lane == h, m + jnp.log(l), lse_all)
        lse_ref[...] = lse_all

    full = pl.BlockSpec((L, AW), lambda i: (0, 0), pipeline_mode=pl.Buffered(1))
    return pl.pallas_call(
        body, name="fox_fwd", grid=(L // tq,),
        in_specs=[pl.BlockSpec((tq, AW), lambda i: (i, 0)), full, full, _const_spec(tot.shape)],
        out_specs=[pl.BlockSpec((tq, FOX_W), lambda i: (i, 0)), pl.BlockSpec((tq, FOX_W), lambda i: (i, 0)),
                   pl.BlockSpec((tq, 128), lambda i: (i, 0))],
        out_shape=[jax.ShapeDtypeStruct((L, FOX_W), bf16), jax.ShapeDtypeStruct((L, FOX_W), f32),
                   jax.ShapeDtypeStruct((L, 128), f32)],
        compiler_params=_cp(("parallel",)))(qa, ka, va, tot)


def _fox_bprep(qa, dycat, o_acc, lse, cst, *, tt=256):
    L = qa.shape[0]

    def body(qa_ref, do_ref, oa_ref, lse_ref, pq_ref, g_ref, a0, a1, a2, b0, b1, b2, qb_ref, doa_ref, qbt_ref, doat_ref):
        dob = do_ref[...].astype(bf16)
        delta = _dot_hi(dob.astype(f32) * oa_ref[...], g_ref[...])
        doa = _dot(dob, pq_ref[...])
        for part, ref in zip(_split3(-delta), (a0, a1, a2)):
            doa = doa + _dot(part, ref[...])
        qb = qa_ref[...].astype(f32)
        for part, ref in zip(_split3(-lse_ref[...]), (b0, b1, b2)):
            qb = qb + _dot(part, ref[...])
        doa, qb = doa.astype(bf16), qb.astype(bf16)
        doa_ref[...] = doa
        qb_ref[...] = qb
        doat_ref[...] = doa.T
        qbt_ref[...] = qb.T

    tok = lambda w_: pl.BlockSpec((tt, w_), lambda i: (i, 0))
    tr = pl.BlockSpec((AW, tt), lambda i: (0, i))
    return pl.pallas_call(
        body, name="fox_bprep", grid=(L // tt,),
        in_specs=[tok(AW), pl.BlockSpec((tt, FOX_W), lambda i: (i, SSD_W // FOX_W)), tok(FOX_W), tok(128),
                  _const_spec((FOX_W, AW)), _const_spec((FOX_W, 128))] + [_const_spec((128, AW))] * 6,
        out_specs=[tok(AW), tok(AW), tr, tr],
        out_shape=[jax.ShapeDtypeStruct((L, AW), bf16)] * 2 + [jax.ShapeDtypeStruct((AW, L), bf16)] * 2,
        compiler_params=_cp(("parallel",)))(qa, dycat, o_acc, lse, cst["pq"], cst["group"], *cst["head_a"], *cst["head_b"])


def _fox_attn_bwd(ka, va, tot, qb, doa, qbt, doat, *, tq=256):
    L = ka.shape[0]
    nq = L // tq

    def body(ka_ref, va_ref, tot_ref, qb_ref, doa_ref, qbt_ref, doat_ref, dq_ref, dkt_ref, dvt_ref):
        j = pl.program_id(0)

        @pl.when(j == 0)
        def _():
            dq_ref[...] = jnp.zeros_like(dq_ref)
        row = lax.broadcasted_iota(jnp.int32, (tq, tq), 0)
        col = lax.broadcasted_iota(jnp.int32, (tq, tq), 1)
        diag_bias = jnp.where(row >= col, 0.0, NEG)

        def block(i, carry, gap, bias):
            r0 = pl.multiple_of(i * tq, tq)
            out = []
            for h in range(FOX_H):
                hl = slice(h * HL, (h + 1) * HL)
                dkt, dvt, dsum = carry[h]
                kj = ka_ref[:, hl]
                s = _dot_nt(qb_ref[pl.ds(r0, tq), hl], kj) + gap[:, SM_F + h:SM_F + h + 1]
                if bias is not None:
                    s = s + bias
                p = jnp.exp(s)
                ds = p * _dot_nt(doa_ref[pl.ds(r0, tq), hl], va_ref[:, hl])
                dsb = ds.astype(bf16)
                dq_ref[pl.ds(r0, tq), hl] += _dot(dsb, kj)
                out.append((dkt + _dot(qbt_ref[hl, pl.ds(r0, tq)], dsb),
                            dvt + _dot(doat_ref[hl, pl.ds(r0, tq)], p.astype(bf16)),
                            dsum + jnp.sum(ds, 0, keepdims=True)))
            return tuple(out)

        init = tuple((jnp.zeros((HL, tq), f32), jnp.zeros((HL, tq), f32), jnp.zeros((1, tq), f32))
                     for _ in range(FOX_H))
        def step(i, state):
            carry, gap = state
            gap = gap + tot_ref[pl.ds(pl.multiple_of((i - 1) * 8, 8), 1), :]
            return block(i, carry, gap, None), gap

        zero_gap = jnp.zeros((1, 128), f32)
        carry = block(j, init, zero_gap, diag_bias)
        carry, _ = lax.fori_loop(j + 1, nq, step, (carry, zero_gap))
        for h in range(FOX_H):
            hl = slice(h * HL, (h + 1) * HL)
            dkt_ref[hl, :] = carry[h][0]
            dvt_ref[hl, :] = carry[h][1]
            dkt_ref[h * HL + FOX_HD:h * HL + FOX_HD + 1, :] = carry[h][2]

    full = lambda shape: pl.BlockSpec(shape, lambda j: (0, 0), pipeline_mode=pl.Buffered(1))
    blk = pl.BlockSpec((tq, AW), lambda j: (j, 0))
    trb = pl.BlockSpec((AW, tq), lambda j: (0, j))
    return pl.pallas_call(
        body, name="fox_bwd", grid=(nq,),
        in_specs=[blk, blk, full(tot.shape), full((L, AW)), full((L, AW)), full((AW, L)), full((AW, L))],
        out_specs=[pl.BlockSpec((L, AW), lambda j: (0, 0)), trb, trb],
        out_shape=[jax.ShapeDtypeStruct((L, AW), f32), jax.ShapeDtypeStruct((AW, L), f32),
                   jax.ShapeDtypeStruct((AW, L), f32)],
        compiler_params=_cp(("arbitrary",)))(ka, va, tot, qb, doa, qbt, doat)


def _fox_post(dq, dkt, dvt, proj, ddt, hp, cst, *, tt=256):
    L = proj.shape[0]
    n = L // tt

    def body(dq_ref, dkt_ref, dvt_ref, sm_ref, ddt_ref, hp_ref, pqt_ref, ca_ref,
             dqo_ref, dko_ref, dvo_ref, dsm_ref, acc_ref, carry):
        @pl.when(pl.program_id(0) == 0)
        def _():
            carry[...] = jnp.zeros_like(carry)
            acc_ref[...] = jnp.zeros_like(acc_ref)
        pqt = pqt_ref[...]
        dk_full = dkt_ref[...].T
        dqo_ref[...] = _dot((dq_ref[...] * FOX_SCALE).astype(bf16), pqt).astype(bf16)
        dko_ref[...] = _dot(dk_full.astype(bf16), pqt).astype(bf16)
        dvo_ref[...] = _dot(dvt_ref[...].T.astype(bf16), pqt).astype(bf16)
        dc = -_dot_hi(dk_full, ca_ref[...])
        r = lax.broadcasted_iota(jnp.int32, (tt, tt), 0)
        c = lax.broadcasted_iota(jnp.int32, (tt, tt), 1)
        dl = _dot_hi(jnp.where(r <= c, 1.0, 0.0), dc) + carry[0:1, :]
        carry[0:1, :] += jnp.sum(dc, 0, keepdims=True)
        xx = sm_ref[...] + hp_ref[3:4, :]
        lane = lax.broadcasted_iota(jnp.int32, (tt, 128), 1)
        dlogit = jnp.where((lane >= SM_F) & (lane < SM_F + FOX_H), dl * _sigmoid(-xx), 0.0)
        acc_ref[3:4, :] += jnp.sum(dlogit, 0, keepdims=True)
        dsm_ref[...] = (dlogit + ddt_ref[...]).astype(bf16)

    rev = lambda w_: pl.BlockSpec((tt, w_), lambda i: (n - 1 - i, 0))
    revt = pl.BlockSpec((AW, tt), lambda i: (0, n - 1 - i))
    return pl.pallas_call(
        body, name="fox_post", grid=(n,),
        in_specs=[rev(AW), revt, revt, pl.BlockSpec((tt, 128), lambda i: (n - 1 - i, P_SM // 128)), rev(128),
                  _const_spec((8, 128)), _const_spec((AW, FOX_W)), _const_spec((AW, 128))],
        out_specs=[rev(FOX_W), rev(FOX_W), rev(FOX_W), rev(128), pl.BlockSpec((8, 128), lambda i: (0, 0))],
        out_shape=[jax.ShapeDtypeStruct((L, FOX_W), bf16)] * 3 + [jax.ShapeDtypeStruct((L, 128), bf16),
                                                                   jax.ShapeDtypeStruct((8, 128), f32)],
        scratch_shapes=[pltpu.VMEM((8, 128), f32)],
        compiler_params=_cp(("arbitrary",)))(dq, dkt, dvt, proj, ddt, hp, cst["pqt"], cst["col_a"])


SSD_GW = SSD_W // SSD_G
SSD_HPG = SSD_H // SSD_G


def _ssd_pre(x, xprev, sm, cp_ref, hp_ref, tc):
    pre = (cp_ref[4:5, :] + cp_ref[3:4, :] * x + cp_ref[2:3, :] * _shift_down(x, xprev, 1)
           + cp_ref[1:2, :] * _shift_down(x, xprev, 2) + cp_ref[0:1, :] * _shift_down(x, xprev, 3))
    sig = _sigmoid(pre)
    raw = sm + hp_ref[0:1, :]
    dt = jnp.maximum(raw, 0.0) + _log1pexp(raw)
    a_neg = -jnp.exp(hp_ref[1:2, :])
    r = lax.broadcasted_iota(jnp.int32, (tc, tc), 0)
    c = lax.broadcasted_iota(jnp.int32, (tc, tc), 1)
    cs = _dot_hi(jnp.where(r >= c, 1.0, 0.0), dt * a_neg)
    return pre, sig, raw, dt, a_neg, cs, cs.T, r >= c


def _ssd_fwd(proj, cp, hp, ng, *, tc=256):
    L = proj.shape[0]
    nc = L // tc

    def body(xc_ref, xp_ref, z_ref, sm_ref, cp_ref, hp_ref, ng_ref, y_ref, ypre_ref, sin_ref, s_scr):
        i = pl.program_id(0)

        @pl.when(i == 0)
        def _():
            s_scr[...] = jnp.zeros_like(s_scr)
        x = xc_ref[...]
        xprev = xp_ref[...] * jnp.where(i > 0, 1.0, 0.0)
        pre, sig, _, dt, _, cs, cst, tril = _ssd_pre(x, xprev, sm_ref[...], cp_ref, hp_ref, tc)
        xbc = pre * sig
        sin_ref[...] = s_scr[...]
        for g in range(SSD_G):
            bg = xbc[:, SSD_W + g * SSD_N:SSD_W + (g + 1) * SSD_N]
            cg = xbc[:, SSD_W + SSD_G * SSD_N + g * SSD_N:SSD_W + SSD_G * SSD_N + (g + 1) * SSD_N].astype(bf16)
            cb = _dot_nt(cg, bg.astype(bf16))
            for e in range(SSD_HPG):
                h = g * SSD_HPG + e
                hs = slice(h * SSD_HD, (h + 1) * SSD_HD)
                xs = xbc[:, hs]
                csc = cs[:, h:h + 1]
                lm = jnp.where(tril, jnp.exp(jnp.minimum(csc - cst[h:h + 1, :], 0.0)), 0.0)
                xdt = (xs * dt[:, h:h + 1]).astype(bf16)
                s_h = s_scr[:, hs]
                y = _dot((cb * lm).astype(bf16), xdt) + jnp.exp(csc) * _dot(cg, s_h.astype(bf16))
                ypre_ref[:, hs] = y + hp_ref[2:3, h:h + 1] * xs
                cl = cs[tc - 1:tc, h:h + 1]
                bd = (bg * jnp.exp(cl - csc)).astype(bf16)
                s_scr[:, hs] = jnp.exp(cl) * s_h + _dot_tn(bd, xdt)
        z = z_ref[...]
        yz = ypre_ref[...] * (z * _sigmoid(z))
        for g in range(SSD_G):
            gs = slice(g * SSD_GW, (g + 1) * SSD_GW)
            yg = yz[:, gs]
            r = lax.rsqrt(jnp.mean(yg * yg, -1, keepdims=True) + RMS_EPS)
            y_ref[:, gs] = (yg * r * ng_ref[0:1, gs]).astype(bf16)

    return pl.pallas_call(
        body, name="ssd_fwd", grid=(nc,),
        in_specs=[pl.BlockSpec((tc, 1024), lambda i: (i, 0)),
                  pl.BlockSpec((tc, 1024), lambda i: (jnp.maximum(i - 1, 0), 0)),
                  pl.BlockSpec((tc, SSD_W), lambda i: (i, P_Z // SSD_W)),
                  pl.BlockSpec((tc, 128), lambda i: (i, P_SM // 128)),
                  _const_spec((8, 1024)), _const_spec((8, 128)), _const_spec((8, SSD_W))],
        out_specs=[pl.BlockSpec((tc, SSD_W), lambda i: (i, 0)), pl.BlockSpec((tc, SSD_W), lambda i: (i, 0)),
                   pl.BlockSpec((SSD_N, SSD_W), lambda i: (i, 0))],
        out_shape=[jax.ShapeDtypeStruct((L, SSD_W), bf16), jax.ShapeDtypeStruct((L, SSD_W), f32),
                   jax.ShapeDtypeStruct((nc * SSD_N, SSD_W), f32)],
        scratch_shapes=[pltpu.VMEM((SSD_N, SSD_W), f32)],
        compiler_params=_cp(("arbitrary",)))(proj, proj, proj, proj, cp, hp, ng)


def _ssd_bwd(proj, dycat, ypre, sin, cp, hp, ng, *, tc=256):
    L = proj.shape[0]
    nc = L // tc

    def body(xc_ref, xp_ref, z_ref, sm_ref, cp_ref, hp_ref, ng_ref, sin_ref, ypre_ref, dy_ref,
             dxbc_ref, dz_ref, ddt_ref, acc1_ref, acc2_ref, ds_scr, dnext_scr, dxbc_scr):
        i = pl.program_id(0)
        c_idx = nc - 1 - i

        @pl.when(i == 0)
        def _():
            ds_scr[...] = jnp.zeros_like(ds_scr)
            dnext_scr[...] = jnp.zeros_like(dnext_scr)
            acc1_ref[...] = jnp.zeros_like(acc1_ref)
            acc2_ref[...] = jnp.zeros_like(acc2_ref)
        x = xc_ref[...]
        xprev = xp_ref[...] * jnp.where(c_idx > 0, 1.0, 0.0)
        pre, sig, raw, dt, a_neg, cs, cst, tril = _ssd_pre(x, xprev, sm_ref[...], cp_ref, hp_ref, tc)
        xbc = pre * sig
        z = z_ref[...]
        sz = _sigmoid(z)
        silz = z * sz
        yall = ypre_ref[...]
        yz = yall * silz
        dy = dy_ref[...]
        dyz_parts = []
        for g in range(SSD_G):
            gs = slice(g * SSD_GW, (g + 1) * SSD_GW)
            yg, dyg = yz[:, gs], dy[:, gs]
            r = lax.rsqrt(jnp.mean(yg * yg, -1, keepdims=True) + RMS_EPS)
            acc1_ref[5:6, gs] += jnp.sum(dyg * yg * r, 0, keepdims=True)
            dyn = dyg * ng_ref[0:1, gs]
            dyz_parts.append(r * (dyn - yg * (r * r) * jnp.mean(dyn * yg, -1, keepdims=True)))
        dyz = jnp.concatenate(dyz_parts, axis=1)
        dz_ref[...] = (dyz * yall * (sz * (1.0 + z * (1.0 - sz)))).astype(bf16)
        dyall = dyz * silz

        lane1 = lax.broadcasted_iota(jnp.int32, (1, 128), 1)
        sub = lax.broadcasted_iota(jnp.int32, (128, tc), 0)
        rowc = lax.broadcasted_iota(jnp.int32, (tc, 1), 0)
        dcs = jnp.zeros((tc, 128), f32)
        dcsr = jnp.zeros((128, tc), f32)
        ddt = jnp.zeros((tc, 128), f32)
        dd_row = jnp.zeros((1, 128), f32)
        for g in range(SSD_G):
            b0 = SSD_W + g * SSD_N
            c0 = SSD_W + SSD_G * SSD_N + g * SSD_N
            bg = xbc[:, b0:b0 + SSD_N]
            bgb = bg.astype(bf16)
            cgb = xbc[:, c0:c0 + SSD_N].astype(bf16)
            cb = _dot_nt(cgb, bgb)
            dbg = jnp.zeros((tc, SSD_N), f32)
            dcg = jnp.zeros((tc, SSD_N), f32)
            for e in range(SSD_HPG):
                h = g * SSD_HPG + e
                hs = slice(h * SSD_HD, (h + 1) * SSD_HD)
                oh = jnp.where(lane1 == h, 1.0, 0.0)
                xs = xbc[:, hs]
                dth = dt[:, h:h + 1]
                csc = cs[:, h:h + 1]
                lm = jnp.where(tril, jnp.exp(jnp.minimum(csc - cst[h:h + 1, :], 0.0)), 0.0)
                m = cb * lm
                xdt = (xs * dth).astype(bf16)
                s_h = sin_ref[:, hs]
                s_hb = s_h.astype(bf16)
                dyh = dyall[:, hs]
                dyb = dyh.astype(bf16)
                dd_row = dd_row + oh * jnp.sum(dyh * xs)
                dxs = hp_ref[2:3, h:h + 1] * dyh
                ecs = jnp.exp(csc)
                cs_prod = _dot(cgb, s_hb)
                dcsb = (ecs * dyh).astype(bf16)
                dcg = dcg + _dot_nt(dcsb, s_hb)
                ds_in = _dot_tn(cgb, dcsb)
                dcs_h = jnp.sum(dyh * ecs * cs_prod, -1, keepdims=True)
                dm = _dot_nt(dyb, xdt)
                w = dm * m
                dcs_h = dcs_h + jnp.sum(w, -1, keepdims=True)
                dcsr = jnp.where(sub == h, jnp.sum(w, 0, keepdims=True), dcsr)
                dcbb = (dm * lm).astype(bf16)
                dcg = dcg + _dot(dcbb, bgb)
                dbg = dbg + _dot_tn(dcbb, cgb)
                dxdt = _dot_tn(m.astype(bf16), dyb)
                dsn = ds_scr[:, hs]
                dsnb = dsn.astype(bf16)
                cl = cs[tc - 1:tc, h:h + 1]
                dec = jnp.exp(cl - csc)
                dxdt = dxdt + _dot((bg * dec).astype(bf16), dsnb)
                dbd = _dot_nt(xdt, dsnb)
                dbg = dbg + dbd * dec
                gdec = jnp.sum(dbd * bg, -1, keepdims=True) * dec
                ecl = jnp.exp(cl)
                dcl = jnp.sum(gdec) + jnp.sum(dsn * s_h) * ecl
                ds_scr[:, hs] = ecl * dsn + ds_in
                dcs_h = dcs_h - gdec + jnp.where(rowc == tc - 1, dcl, 0.0)
                dcs = dcs + dcs_h * oh
                dxbc_scr[:, hs] = dxs + dxdt * dth
                ddt = ddt + jnp.sum(dxdt * xs, -1, keepdims=True) * oh
            dxbc_scr[:, b0:b0 + SSD_N] = dbg
            dxbc_scr[:, c0:c0 + SSD_N] = dcg
        dcs = dcs - dcsr.T
        r_i = lax.broadcasted_iota(jnp.int32, (tc, tc), 0)
        c_i = lax.broadcasted_iota(jnp.int32, (tc, tc), 1)
        da = _dot_hi(jnp.where(r_i <= c_i, 1.0, 0.0), dcs)
        ddt = ddt + da * a_neg
        acc2_ref[1:2, :] += jnp.sum(da * dt, 0, keepdims=True) * a_neg
        lane = lax.broadcasted_iota(jnp.int32, (tc, 128), 1)
        ddraw = jnp.where(lane < SSD_H, ddt * _sigmoid(raw), 0.0)
        acc2_ref[0:1, :] += jnp.sum(ddraw, 0, keepdims=True)
        acc2_ref[2:3, :] += dd_row
        ddt_ref[...] = ddraw
        dpre = dxbc_scr[...] * (sig * (1.0 + pre * (1.0 - sig)))
        acc1_ref[4:5, :] += jnp.sum(dpre, 0, keepdims=True)
        for k in range(SSD_K):
            acc1_ref[k:k + 1, :] += jnp.sum(dpre * _shift_down(x, xprev, SSD_K - 1 - k), 0, keepdims=True)
        dnext = dnext_scr[...]
        dxbc_ref[...] = (cp_ref[3:4, :] * dpre + cp_ref[2:3, :] * _shift_up(dpre, dnext, 1)
                         + cp_ref[1:2, :] * _shift_up(dpre, dnext, 2)
                         + cp_ref[0:1, :] * _shift_up(dpre, dnext, 3)).astype(bf16)
        dnext_scr[...] = dpre

    rev = lambda w_, col: pl.BlockSpec((tc, w_), lambda i: (nc - 1 - i, col // w_))
    return pl.pallas_call(
        body, name="ssd_bwd", grid=(nc,),
        in_specs=[rev(1024, 0), pl.BlockSpec((tc, 1024), lambda i: (jnp.maximum(nc - 2 - i, 0), 0)),
                  rev(SSD_W, P_Z), rev(128, P_SM),
                  _const_spec((8, 1024)), _const_spec((8, 128)), _const_spec((8, SSD_W)),
                  pl.BlockSpec((SSD_N, SSD_W), lambda i: (nc - 1 - i, 0)), rev(SSD_W, 0), rev(SSD_W, 0)],
        out_specs=[rev(1024, 0), rev(SSD_W, 0), rev(128, 0),
                   pl.BlockSpec((8, 1024), lambda i: (0, 0)), pl.BlockSpec((8, 128), lambda i: (0, 0))],
        out_shape=[jax.ShapeDtypeStruct((L, 1024), bf16), jax.ShapeDtypeStruct((L, SSD_W), bf16),
                   jax.ShapeDtypeStruct((L, 128), f32), jax.ShapeDtypeStruct((8, 1024), f32),
                   jax.ShapeDtypeStruct((8, 128), f32)],
        scratch_shapes=[pltpu.VMEM((SSD_N, SSD_W), f32), pltpu.VMEM((tc, 1024), f32), pltpu.VMEM((tc, 1024), f32)],
        compiler_params=_cp(("arbitrary",)))(proj, proj, proj, proj, cp, hp, ng, sin, ypre, dycat)


def _pack_cols(w):
    pad = jnp.zeros(w.shape[:-1] + (P_W - P_SM - SSD_H - FOX_H,), w.dtype)
    return jnp.concatenate([w[..., 512:1536], w[..., 0:512], w[..., 1544:2312], w[..., 2316:3084],
                            w[..., 1536:1544], w[..., 2312:2316], pad], axis=-1)


def _unpack_cols(g):
    return jnp.concatenate([g[..., 1024:1536], g[..., 0:1024], g[..., 3072:3080], g[..., 1536:2304],
                            g[..., 3080:3084], g[..., 2304:3072]], axis=-1)


def _rows8(*rows):
    width = max(r.shape[-1] for r in rows)
    out = [jnp.pad(r.astype(f32), (0, width - r.shape[-1])) for r in rows]
    out += [jnp.zeros((width,), f32)] * (8 - len(out))
    return jnp.stack(out)


def _local_step(x, tgt, p, weights_of, grads_done):
    gb_in = _rows8(p["ln_in_g"], p["ln_in_b"])
    cst = _fox_consts()
    x0 = _ln_in_fwd(x, gb_in)
    saved = []
    for l in range(DEPTH):
        mv = [_rows8(p["mod"][l, 3 * j], p["mod"][l, 3 * j + 1], p["mod"][l, 3 * j + 2], p["ln_g"][l, j], p["ln_b"][l, j])
              for j in range(3)]
        cp = _rows8(*[p["ssd_conv_w"][l, k] for k in range(SSD_K)], p["ssd_conv_b"][l])
        hp = _rows8(jnp.pad(p["ssd_dt_bias"][l], (0, 120)), jnp.pad(p["ssd_a_log"][l], (0, 120)),
                    jnp.pad(p["ssd_d"][l], (0, 120)), jnp.pad(p["fox_f_bias"][l], (SM_F, 128 - SM_F - FOX_H)))
        ng = _rows8(p["ssd_norm_g"][l])
        scw = _rows8(*[p["sconv_w"][l, k] for k in range(SC_K)])
        w = dict(weights_of(l, "ffn1", x0))
        x1 = _ffn_fwd(x0, mv[0], w["ffn1_w_in"], w["ffn1_w_out"])
        w.update(weights_of(l, "mix", x1))
        proj = _inproj_fwd(x1, mv[1], w["mix_w_in"])
        y_ssd, ypre, sin = _ssd_fwd(proj, cp, hp, ng)
        qa, ka, va, tot = _fox_prep(proj, hp, cst)
        o, o_acc, lse = _fox_attn_fwd(qa, ka, va, tot)
        y_sc = _sconv_fwd(proj, scw)
        ycat = jnp.concatenate([y_ssd, o, y_sc], axis=1)
        x2 = _outproj_fwd(x1, ycat, mv[1], w["mix_w_out"])
        w.update(weights_of(l, "ffn2", x2))
        x3 = _ffn_fwd(x2, mv[2], w["ffn2_w_in"], w["ffn2_w_out"])
        saved.append((x0, x1, x2, mv, cp, hp, ng, scw, proj, ypre, sin, qa, ka, va, tot, o_acc, lse, ycat, w))
        x0 = x3
    dx, loss_acc = _loss_head(x0, tgt)

    g = {k: [None] * DEPTH for k in (
        "mod", "ln_g", "ln_b", "ssd_conv_w", "ssd_conv_b", "ssd_dt_bias", "ssd_a_log", "ssd_d", "ssd_norm_g",
        "fox_f_bias", "sconv_w")}
    def behind(small, tok):
        return small if tok is None else small + tok[0, 0]

    token = None
    for l in reversed(range(DEPTH)):
        x0, x1, x2, mv, cp, hp, ng, scw, proj, ypre, sin, qa, ka, va, tot, o_acc, lse, ycat, w = saved[l]
        dx, h, da, act, dyb, a2 = _ffn_bwd(x2, dx, behind(mv[2], token), w["ffn2_w_in"], w["ffn2_w_out"])
        token = grads_done(l, "ffn2", {"ffn2_w_in": _matmul_tn(h, da, tn=FS, tk=512, name="dw_ffn_in"),
                                       "ffn2_w_out": _matmul_tn(act, dyb, tn=D, tk=512, name="dw_ffn_out")})
        mv1 = behind(mv[1], token)
        dxp, dyb, dycat, a1 = _outproj_bwd(x1, ycat, dx, mv1, w["mix_w_out"])
        gw_mix_out = _matmul_tn(ycat, dyb, tn=D // 2, tk=512, name="dw_mix_out")
        dxbc, dz, ddt, acc1, acc2 = _ssd_bwd(proj, dycat, ypre, sin, cp, hp, ng)
        qb, doa, qbt, doat = _fox_bprep(qa, dycat, o_acc, lse, cst)
        dq, dkt, dvt = _fox_attn_bwd(ka, va, tot, qb, doa, qbt, doat)
        dq, dk, dv, dsm, accf = _fox_post(dq, dkt, dvt, proj, ddt, hp, cst)
        dsb, dsc, dsx, accs = _sconv_bwd(proj, dycat, scw)
        dproj = jnp.concatenate([dxbc, dz, dq, dk, dv, dsb, dsc, dsx, dsm], axis=1)
        dx, h, a1b = _inproj_bwd(x1, dxp, dproj, mv1, w["mix_w_in"])
        token = grads_done(l, "mix", {"mix_w_in": _matmul_tn(h, dproj, tn=P_W // 5, tk=512, name="dw_mix_in"),
                                      "mix_w_out": gw_mix_out})
        dx, h, da, act, dyb, a0 = _ffn_bwd(x0, dx, behind(mv[0], token), w["ffn1_w_in"], w["ffn1_w_out"])
        token = grads_done(l, "ffn1", {"ffn1_w_in": _matmul_tn(h, da, tn=FS, tk=512, name="dw_ffn_in"),
                                       "ffn1_w_out": _matmul_tn(act, dyb, tn=D, tk=512, name="dw_ffn_out")})
        g["mod"][l] = jnp.concatenate([a0[0:3], a1b[0:2], a1[2:3], a2[0:3]], axis=0)
        g["ln_g"][l] = jnp.stack([a0[3], a1[3], a2[3]])
        g["ln_b"][l] = jnp.stack([a0[4], a1[4], a2[4]])
        g["ssd_conv_w"][l] = acc1[0:SSD_K]
        g["ssd_conv_b"][l] = acc1[4]
        g["ssd_norm_g"][l] = acc1[5, :SSD_W]
        g["ssd_dt_bias"][l] = acc2[0, :SSD_H]
        g["ssd_a_log"][l] = acc2[1, :SSD_H]
        g["ssd_d"][l] = acc2[2, :SSD_H]
        g["fox_f_bias"][l] = accf[3, SM_F:SM_F + FOX_H]
        g["sconv_w"][l] = accs[0:SC_K]
    grad_x, a_in = _ln_in_bwd(x, dx, behind(gb_in, token))
    g = {k: jnp.stack(v) for k, v in g.items()}
    g["ln_in_g"], g["ln_in_b"] = a_in[0], a_in[1]
    return loss_acc[0, 0], grad_x, g


MESH = pl.DeviceIdType.MESH
ANY = pl.BlockSpec(memory_space=pl.ANY)


def _all_gather(shards, *, in_vmem, name):
    n_arr = len(shards)

    def body(*refs):
        x_refs, out_refs = refs[:n_arr], refs[n_arr:2 * n_arr]
        send_sems, recv_sems, local_sems = refs[2 * n_arr:]
        x, y, c = lax.axis_index("x"), lax.axis_index("y"), lax.axis_index("c")
        me, sibling = (x, y, c), (x, y, 1 - c)
        chips = [(1 - x, y), (x, 1 - y), (1 - x, 1 - y)]

        def copy(a, k, block, to, src=None):
            px, py, pc = block
            slot = out_refs[a].at[4 * px + 2 * py + pc]
            return pltpu.make_async_remote_copy(
                src_ref=slot if src is None else src, dst_ref=slot,
                send_sem=send_sems.at[7 * a + k], recv_sem=recv_sems.at[7 * a + k], device_id=to, device_id_type=MESH)

        mine, first, passed = [], [], []
        for a in range(n_arr):
            mine.append(pltpu.make_async_copy(x_refs[a], out_refs[a].at[4 * x + 2 * y + c], local_sems.at[a]))
            mine[-1].start()
            first.append(copy(a, 0, me, sibling, src=x_refs[a]))
            first += [copy(a, 1 + j, me, (*chip, c), src=x_refs[a]) for j, chip in enumerate(chips)]
        for cp in first:
            cp.start()
        for j, chip in enumerate(chips):
            for a in range(n_arr):
                copy(a, 1 + j, (*chip, c), me).wait_recv()
                passed.append(copy(a, 4 + j, (*chip, c), sibling))
                passed[-1].start()
        for a in range(n_arr):
            copy(a, 0, sibling, me).wait_recv()
            for j, chip in enumerate(chips):
                copy(a, 4 + j, (*chip, 1 - c), me).wait_recv()
        for cp in first + passed:
            cp.wait_send()
        for cp in mine:
            cp.wait()

    spec = pl.BlockSpec(memory_space=pltpu.VMEM) if in_vmem else ANY
    return pl.pallas_call(
        body, name=name, out_shape=[jax.ShapeDtypeStruct((N_DEV,) + s.shape, s.dtype) for s in shards],
        in_specs=[spec] * n_arr, out_specs=[spec] * n_arr,
        scratch_shapes=[pltpu.SemaphoreType.DMA((7 * n_arr,)), pltpu.SemaphoreType.DMA((7 * n_arr,)),
                        pltpu.SemaphoreType.DMA((n_arr,))],
    )(*shards)


def _swap_sibling(sends):
    n_arr = len(sends)

    def body(*refs):
        s_refs, o_refs = refs[:n_arr], refs[n_arr:2 * n_arr]
        send_sems, recv_sems = refs[2 * n_arr:]
        x, y, c = lax.axis_index("x"), lax.axis_index("y"), lax.axis_index("c")
        cps = [pltpu.make_async_remote_copy(
            src_ref=s_refs[a].at[:, 1 - c], dst_ref=o_refs[a], send_sem=send_sems.at[a], recv_sem=recv_sems.at[a],
            device_id=(x, y, 1 - c), device_id_type=MESH) for a in range(n_arr)]
        for cp in cps:
            cp.start()
        for cp in cps:
            cp.wait_recv()
        for cp in cps:
            cp.wait_send()

    return pl.pallas_call(
        body, name="swap_sibling",
        out_shape=[jax.ShapeDtypeStruct((s.shape[0],) + s.shape[2:], s.dtype) for s in sends],
        in_specs=[ANY] * n_arr, out_specs=[ANY] * n_arr,
        scratch_shapes=[pltpu.SemaphoreType.DMA((n_arr,)), pltpu.SemaphoreType.DMA((n_arr,))])(*sends)


def _exchange_chips(bufs):
    n_arr = len(bufs)

    def body(*refs):
        b_refs, o_refs = refs[:n_arr], refs[n_arr:2 * n_arr]
        send_sems, recv_sems, local_sems = refs[2 * n_arr:]
        x, y, c = lax.axis_index("x"), lax.axis_index("y"), lax.axis_index("c")
        mine = 2 * x + y
        peers = [(x, 1 - y), (1 - x, y), (1 - x, 1 - y)]

        def copy(a, k, src_slot, dst_slot):
            px, py = peers[k]
            return pltpu.make_async_remote_copy(
                src_ref=b_refs[a].at[src_slot], dst_ref=o_refs[a].at[dst_slot],
                send_sem=send_sems.at[3 * a + k], recv_sem=recv_sems.at[3 * a + k],
                device_id=(px, py, c), device_id_type=MESH)

        own = [pltpu.make_async_copy(b_refs[a].at[mine], o_refs[a].at[mine], local_sems.at[a]) for a in range(n_arr)]
        sent = [copy(a, k, 2 * px + py, mine) for a in range(n_arr) for k, (px, py) in enumerate(peers)]
        for cp in own + sent:
            cp.start()
        for a in range(n_arr):
            for k, (px, py) in enumerate(peers):
                copy(a, k, mine, 2 * px + py).wait_recv()
        for cp in sent:
            cp.wait_send()
        for cp in own:
            cp.wait()

    return pl.pallas_call(
        body, name="exchange_chips", out_shape=[jax.ShapeDtypeStruct(b.shape, b.dtype) for b in bufs],
        in_specs=[ANY] * n_arr, out_specs=[ANY] * n_arr,
        scratch_shapes=[pltpu.SemaphoreType.DMA((3 * n_arr,)), pltpu.SemaphoreType.DMA((3 * n_arr,)),
                        pltpu.SemaphoreType.DMA((n_arr,))])(*bufs)


HBM = pl.BlockSpec(memory_space=pltpu.HBM)
SEM = pl.BlockSpec(memory_space=pltpu.SEMAPHORE)
EFFECT = pltpu.SideEffectType.DATAFLOW_SIDE_EFFECTING


def _spread_copies(srcs, lands, send_sems, recv_sems, local_sems, scatter):
    x, y, c = lax.axis_index("x"), lax.axis_index("y"), lax.axis_index("c")
    me = 4 * x + 2 * y + c
    local, remote = [], []
    for a in range(len(srcs)):
        own = srcs[a].at[me] if scatter else srcs[a]
        local.append(pltpu.make_async_copy(own, lands[a].at[me], local_sems.at[a]))
        for r in range(1, N_DEV):
            px, py, pc = (1 - x if r & 4 else x), (1 - y if r & 2 else y), (1 - c if r & 1 else c)
            peer = 4 * px + 2 * py + pc
            k = (N_DEV - 1) * a + r - 1
            mk = functools.partial(pltpu.make_async_remote_copy, send_sem=send_sems.at[k], recv_sem=recv_sems.at[k],
                                   device_id=(px, py, pc), device_id_type=MESH)
            remote.append((mk(src_ref=srcs[a].at[peer] if scatter else srcs[a], dst_ref=lands[a].at[me]),
                           mk(src_ref=own, dst_ref=lands[a].at[peer])))
    return local, remote


def _spread_start(srcs, *, scatter, name, after=()):
    n, k = len(srcs), len(after)
    lands = [jax.ShapeDtypeStruct((N_DEV,) + s.shape[-2:], s.dtype) for s in srcs]

    def body(*refs):
        src, land = refs[:n], refs[n:2 * n]
        send_sems, recv_sems, local_sems = refs[2 * n + k:2 * n + k + 3]
        token = refs[-1]
        local, remote = _spread_copies(src, land, send_sems, recv_sems, local_sems, scatter)
        for cp in local:
            cp.start()
        for cp, _ in remote:
            cp.start()
        token[...] = jnp.zeros_like(token)

    nsem = (N_DEV - 1) * n
    out = pl.pallas_call(
        body, name=name,
        out_shape=(pltpu.SemaphoreType.DMA((nsem,)), pltpu.SemaphoreType.DMA((nsem,)), pltpu.SemaphoreType.DMA((n,)),
                   *[pltpu.HBM(s.shape, s.dtype) for s in srcs], *[pltpu.HBM(s.shape, s.dtype) for s in lands],
                   jax.ShapeDtypeStruct((8, 128), f32)),
        in_specs=[HBM] * (2 * n) + [ANY] * k,
        out_specs=(SEM, SEM, SEM, *[HBM] * (2 * n), pl.BlockSpec(memory_space=pltpu.VMEM)),
        input_output_aliases={i: 3 + i for i in range(2 * n)},
        compiler_params=pltpu.CompilerParams(has_side_effects=EFFECT),
    )(*[pltpu.with_memory_space_constraint(s, pltpu.HBM) for s in srcs],
      *[pltpu.with_memory_space_constraint(lax.empty(s.shape, s.dtype), pltpu.HBM) for s in lands], *after)
    return out[:-1], out[-1]


def _spread_wait(state, after, *, scatter, name):
    n = (len(state) - 3) // 2
    sems, thru = state[:3], state[3:]

    def body(*refs):
        src, land = refs[:n], refs[n:2 * n]
        send_sems, recv_sems, local_sems = refs[2 * n:2 * n + 3]
        local, remote = _spread_copies(src, land, send_sems, recv_sems, local_sems, scatter)
        for sent, received in remote:
            sent.wait_send()
            received.wait_recv()
        for cp in local:
            cp.wait()

    out = pl.pallas_call(
        body, name=name,
        out_shape=tuple(pltpu.HBM(t.shape, t.dtype) for t in thru),
        in_specs=[HBM] * (2 * n) + [SEM] * 3 + [ANY], out_specs=tuple([HBM] * (2 * n)),
        input_output_aliases={i: i for i in range(2 * n)},
        compiler_params=pltpu.CompilerParams(has_side_effects=EFFECT),
    )(*thru, *sems, after)
    return list(out[n:])


def _row_tile(r, c):
    if r * c <= 512 * 1024:
        return r
    t = r
    while t * c > 512 * 1024 and t % 2 == 0 and (t // 2) % 16 == 0:
        t //= 2
    return t


def _add_own_half(send, recv, c):
    nb, _, r, n = send.shape
    tr = _row_tile(r, n)

    def body(c_ref, s_ref, r_ref, o_ref):
        o_ref[...] = (s_ref[...].astype(f32) + r_ref[...].astype(f32)).astype(o_ref.dtype)

    return pl.pallas_call(
        body, name="add_own_half",
        grid_spec=pltpu.PrefetchScalarGridSpec(
            num_scalar_prefetch=1, grid=(nb, r // tr),
            in_specs=[pl.BlockSpec((None, None, tr, n), lambda j, i, cr: (j, cr[0], i, 0)),
                      pl.BlockSpec((None, tr, n), lambda j, i, cr: (j, i, 0))],
            out_specs=pl.BlockSpec((None, tr, n), lambda j, i, cr: (j, i, 0))),
        out_shape=jax.ShapeDtypeStruct((nb, r, n), bf16),
        compiler_params=_cp(("parallel", "parallel")))(jnp.reshape(c, (1,)).astype(jnp.int32), send, recv)


def _sum_slots(buf, *, tr, name):
    nb, r, n = buf.shape

    def body(b_ref, o_ref):
        acc = b_ref[0].astype(f32)
        for k in range(1, nb):
            acc = acc + b_ref[k].astype(f32)
        o_ref[...] = acc

    return pl.pallas_call(
        body, name=name, grid=(r // tr,),
        in_specs=[pl.BlockSpec((nb, tr, n), lambda i: (0, i, 0))],
        out_specs=pl.BlockSpec((tr, n), lambda i: (i, 0)),
        out_shape=jax.ShapeDtypeStruct((r, n), f32), compiler_params=_cp(("parallel",)))(buf)


def _ada_fwd(c_all, ada_w, ada_b_cols):
    n = ada_w.shape[-1]

    def body(c_ref, w_ref, b_ref, o_ref):
        cv = c_ref[...]
        ca = (cv * _sigmoid(cv)).astype(bf16)
        o_ref[...] = _dot(ca, w_ref[...].astype(bf16)) + b_ref[...]

    return pl.pallas_call(
        body, name="ada_fwd", grid=(DEPTH,),
        in_specs=[_const_spec((N_DEV, D)), pl.BlockSpec((None, D, n), lambda l: (l, 0, 0)),
                  pl.BlockSpec((None, 1, n), lambda l: (l, 0, 0))],
        out_specs=pl.BlockSpec((None, N_DEV, n), lambda l: (l, 0, 0)),
        out_shape=jax.ShapeDtypeStruct((DEPTH, N_DEV, n), f32), compiler_params=_cp(("parallel",)))(c_all, ada_w, ada_b_cols)


def _ada_bwd(c_all, dmod_cols):
    n = dmod_cols.shape[-1]

    def body(c_ref, d_ref, o_ref):
        cv = c_ref[...]
        ca = (cv * _sigmoid(cv)).astype(bf16)
        o_ref[...] = _dot_tn(ca, d_ref[...].astype(bf16))

    return pl.pallas_call(
        body, name="ada_bwd", grid=(DEPTH,),
        in_specs=[_const_spec((N_DEV, D)), pl.BlockSpec((None, N_DEV, n), lambda l: (l, 0, 0))],
        out_specs=pl.BlockSpec((None, D, n), lambda l: (l, 0, 0)),
        out_shape=jax.ShapeDtypeStruct((DEPTH, D, n), f32), compiler_params=_cp(("parallel",)))(c_all, dmod_cols)


def _adamw(w, g, m, v, *, tr, name):
    r, n = w.shape

    def body(w_ref, g_ref, m_ref, v_ref, d_ref, mo_ref, vo_ref):
        g_ = g_ref[...]
        m_ = ADAM_B1 * m_ref[...] + (1.0 - ADAM_B1) * g_
        v_ = ADAM_B2 * v_ref[...] + (1.0 - ADAM_B2) * jnp.square(g_)
        m_hat = m_ / (1.0 - ADAM_B1 ** ADAM_STEP)
        v_hat = v_ / (1.0 - ADAM_B2 ** ADAM_STEP)
        d_ref[...] = -ADAM_LR * (m_hat / (jnp.sqrt(v_hat) + ADAM_EPS) + ADAM_WD * w_ref[...])
        mo_ref[...] = m_
        vo_ref[...] = v_

    blk = pl.BlockSpec((tr, n), lambda i: (i, 0))
    return pl.pallas_call(
        body, name=name, grid=(r // tr,), in_specs=[blk] * 4, out_specs=[blk] * 3,
        out_shape=[jax.ShapeDtypeStruct((r, n), f32)] * 3, compiler_params=_cp(("parallel",)))(w, g, m, v)


WEIGHTS = ["ln_in_g", "ln_in_b", "ada_w", "ada_b", "ffn1_w_in", "ffn1_w_out", "mix_w_in", "mix_w_out", "ssd_conv_w",
           "ssd_conv_b", "ssd_dt_bias", "ssd_a_log", "ssd_d", "ssd_norm_g", "fox_f_bias", "sconv_w", "ffn2_w_in",
           "ffn2_w_out", "ln_g", "ln_b"]
BIG = ["ffn1_w_in", "ffn1_w_out", "ffn2_w_in", "ffn2_w_out", "mix_w_in", "mix_w_out"]
GROUPS = {"ffn1": ["ffn1_w_in", "ffn1_w_out"], "mix": ["mix_w_in", "mix_w_out"], "ffn2": ["ffn2_w_in", "ffn2_w_out"]}
COL_SHARDED = ("ffn1_w_in", "ffn2_w_in")
SMALL_SHARDED = {"ssd_conv_w": 128, "sconv_w": 32, "ln_g": 128, "ln_b": 128}
ADAM_TR = {"ada_w": 256, "ffn1_w_in": 512, "ffn2_w_in": 512, "ffn1_w_out": 352, "ffn2_w_out": 352, "mix_w_in": 64,
           "mix_w_out": 256}


def _pad_rows(v, mult=128):
    v = v.reshape(-1)
    return jnp.pad(v, (0, (-v.shape[0]) % mult))


def _pack_rows(parts, row_mult=8):
    flat = [_pad_rows(p.astype(f32)) for p in parts]
    offs, o = [], 0
    for f in flat:
        offs.append(o)
        o += f.shape[0] // 128
    buf = jnp.concatenate(flat).reshape(-1, 128)
    return jnp.pad(buf, ((0, (-buf.shape[0]) % row_mult), (0, 0))), offs


def _take(buf, off, shape):
    n = 1
    for s in shape:
        n *= s
    rows = -(-n // 128)
    lead = buf.shape[:-2]
    flat = buf[..., off:off + rows, :].reshape(lead + (rows * 128,))
    return flat[..., :n].reshape(lead + tuple(shape))


def _reduce_scatter_layer(g_layer, ci):
    sends = [t.reshape((4, 2) + t.shape[-2:]) for t in g_layer]
    got = _swap_sibling(sends)
    chip_sums = [_add_own_half(s, r, ci) for s, r in zip(sends, got)]
    got = _exchange_chips(chip_sums)
    return [_sum_slots(t, tr=_row_tile(*t.shape[-2:]), name="sum_chips") for t in got]


def kernel(*args):
    names = (["x", "c"] + WEIGHTS + ["loss_target"] + ["m_" + n for n in WEIGHTS] + ["v_" + n for n in WEIGHTS])
    assert len(args) == len(names)
    a = dict(zip(names, args))
    xi, yi, ci = lax.axis_index("x"), lax.axis_index("y"), lax.axis_index("c")
    me = 4 * xi + 2 * yi + ci

    small_in = [a["c"], a["ln_g"], a["ln_b"], a["ssd_conv_w"], a["sconv_w"]]
    buf, offs = _pack_rows(small_in)
    got, = _all_gather([buf], in_vmem=True, name="gather_small")
    c_all = _take(got, offs[0], (D,))
    full = {}
    for k, n in enumerate(["ln_g", "ln_b", "ssd_conv_w", "sconv_w"]):
        sh = a[n].shape
        t = _take(got, offs[k + 1], sh)
        full[n] = jnp.transpose(t, (1, 2, 0, 3)).reshape(sh[0], sh[1], N_DEV * sh[2])

    ncol = a["ada_w"].shape[-1]
    ada_b_cols = lax.dynamic_slice_in_dim(a["ada_b"], me * ncol, ncol, axis=1)[:, None, :]
    mod_cols = _ada_fwd(c_all, a["ada_w"], ada_b_cols)
    got, = _all_gather([mod_cols.reshape(DEPTH * N_DEV, ncol)], in_vmem=True, name="gather_mod")
    got = got.reshape(N_DEV, DEPTH, N_DEV, ncol)
    mod = lax.dynamic_index_in_dim(got, me, axis=2, keepdims=False)
    mod = jnp.transpose(mod, (1, 0, 2)).reshape(DEPTH, 9, D)

    shards = {n: a[n] for n in BIG}
    shards["mix_w_in"] = _pack_cols(a["mix_w_in"])
    def as_weights(names, got):
        return {n: (t if n in COL_SHARDED else t.reshape(-1, t.shape[-1])) for n, t in zip(names, got)}

    def group_shards(l, gn):
        return [shards[n][l].astype(bf16) for n in GROUPS[gn]]

    w_first = _all_gather(group_shards(0, "ffn1"), in_vmem=False, name="gather_weights")
    w_state, after = {}, [w_first[0]]
    for l, gn in [(0, "mix"), (0, "ffn2"), (1, "ffn1"), (1, "mix"), (1, "ffn2")]:
        w_state[l, gn], token = _spread_start(group_shards(l, gn), scatter=False, after=after,
                                              name=f"weights_{l}{gn}_start")
        after = [token]

    def weights_of(l, gn, after):
        if (l, gn) == (0, "ffn1"):
            return as_weights(GROUPS[gn], w_first)
        return as_weights(GROUPS[gn], _spread_wait(w_state[l, gn], after, scatter=False, name=f"weights_{l}{gn}_wait"))

    g_state = {}

    def grads_done(l, gn, gw):
        srcs = [gw[n].reshape((N_DEV,) + shards[n].shape[1:]) for n in GROUPS[gn]]
        g_state[l, gn], token = _spread_start(srcs, scatter=True, name=f"grads_{l}{gn}_start")
        return token

    p = {n: a[n] for n in ("ln_in_g", "ln_in_b", "ssd_conv_b", "ssd_dt_bias", "ssd_a_log", "ssd_d", "ssd_norm_g",
                           "fox_f_bias")}
    p.update(full)
    p["mod"] = mod + after[0][0, 0]

    loss_local, grad_x, g = _local_step(a["x"][0], a["loss_target"][0], p, weights_of, grads_done)
    loss = lax.psum(loss_local, ("x", "y", "c"))

    small_names = ["mod", "ln_in_g", "ln_in_b", "ssd_conv_b", "ssd_dt_bias", "ssd_a_log", "ssd_d", "ssd_norm_g",
                   "fox_f_bias", "ln_g", "ln_b", "ssd_conv_w", "sconv_w"]
    buf, offs = _pack_rows([g[n] for n in small_names])
    got, = _all_gather([buf], in_vmem=True, name="gather_small_grads")
    tot = _sum_slots(got, tr=buf.shape[0], name="sum_small_grads")
    grads = {}
    for k, n in enumerate(small_names[1:], start=1):
        t = _take(tot, offs[k], g[n].shape)
        if n in SMALL_SHARDED:
            w_ = SMALL_SHARDED[n]
            t = lax.dynamic_slice_in_dim(t, me * w_, w_, axis=2)
        grads[n] = t
    grads["ada_b"] = _take(tot, offs[0], (DEPTH, 9 * D))
    dmod_all = _take(got, offs[0], (DEPTH, 9 * D))
    dmod_cols = jnp.transpose(lax.dynamic_slice_in_dim(dmod_all, me * ncol, ncol, axis=2), (1, 0, 2))
    grads["ada_w"] = _ada_bwd(c_all, dmod_cols)

    def landed(l, gn, after):
        got = _spread_wait(g_state[l, gn], after, scatter=True, name=f"grads_{l}{gn}_wait")
        return [_sum_slots(t, tr=_row_tile(t.shape[-2], N_DEV // 2 * t.shape[-1]), name="sum_devices") for t in got]

    delta, new_m, new_v = {}, {}, {}

    def adamw(n):
        sh = a[n].shape
        two = lambda t: t.reshape(-1, sh[-1])
        outs = _adamw(two(a[n]), two(grads[n]), two(a["m_" + n]), two(a["v_" + n]), tr=ADAM_TR[n], name="adamw_" + n)
        delta[n], new_m[n], new_v[n] = (t.reshape(sh) for t in outs)

    small_params = [n for n in WEIGHTS if n not in ADAM_TR]
    packs = [_pack_rows([src[pre + n] for n in small_params])[0]
             for src, pre in ((a, ""), (grads, ""), (a, "m_"), (a, "v_"))]
    _, offs = _pack_rows([a[n] for n in small_params])
    outs = _adamw(*packs, tr=packs[0].shape[0], name="adamw_small")
    for k, n in enumerate(small_params):
        delta[n], new_m[n], new_v[n] = (_take(t, offs[k], a[n].shape) for t in outs)
    adamw("ada_w")
    for gn in ("ffn2", "mix"):
        per_layer = [landed(l, gn, grad_x) for l in range(DEPTH)]
        for k, n in enumerate(GROUPS[gn]):
            grads[n] = jnp.stack([per_layer[0][k], per_layer[1][k]])
            if n == "mix_w_in":
                grads[n] = _unpack_cols(grads[n])
            adamw(n)
    per_layer = [landed(0, "ffn1", delta["mix_w_out"]), landed(1, "ffn1", grad_x)]
    for k, n in enumerate(GROUPS["ffn1"]):
        grads[n] = jnp.stack([per_layer[0][k], per_layer[1][k]])
        adamw(n)

    return (loss, grad_x[None], *[grads[n] for n in WEIGHTS], *[delta[n] for n in WEIGHTS],
            *[new_m[n] for n in WEIGHTS], *[new_v[n] for n in WEIGHTS])
```

```python
import functools

import jax
import jax.numpy as jnp
from jax import lax
from jax.experimental import pallas as pl
from jax.experimental.pallas import tpu as pltpu

f32, bf16 = jnp.float32, jnp.bfloat16

D = 1024
F = 2816
DEPTH = 2
N_DEV = 8
SSD_W, SSD_HD, SSD_H, SSD_G, SSD_N, SSD_K = 512, 64, 8, 2, 128, 4
FOX_W, FOX_HD, FOX_H = 256, 64, 4
SC_W, SC_K = 256, 3
ALPHA = (2 * DEPTH) ** 0.25
LN_EPS = 1e-5
RMS_EPS = 1e-5
P_XBC, P_Z, P_Q, P_K, P_V, P_SB, P_SC, P_SX, P_SM = 0, 1024, 1536, 1792, 2048, 2304, 2560, 2816, 3072
P_W = 3200
SM_DT, SM_F = 0, 8
ADAM_LR, ADAM_B1, ADAM_B2, ADAM_EPS, ADAM_WD, ADAM_STEP = 0.001, 0.9, 0.999, 1e-08, 0.01, 10

VMEM_LIMIT = 56 * 1024 * 1024


def _cp(sem=None):
    return pltpu.CompilerParams(dimension_semantics=sem, vmem_limit_bytes=VMEM_LIMIT)


def _const_spec(shape):
    nd = len(shape)
    return pl.BlockSpec(shape, lambda *_: (0,) * nd, pipeline_mode=pl.Buffered(1))


def _sigmoid(x):
    return 1.0 / (1.0 + jnp.exp(-x))


def _ln_fwd(u, g, b):
    mu = jnp.mean(u, -1, keepdims=True)
    xc = u - mu
    rstd = lax.rsqrt(jnp.mean(xc * xc, -1, keepdims=True) + LN_EPS)
    xhat = xc * rstd
    return xhat * g + b, xhat, rstd


def _ln_bwd(dout, xhat, rstd, g):
    dxh = dout * g
    m1 = jnp.mean(dxh, -1, keepdims=True)
    m2 = jnp.mean(dxh * xhat, -1, keepdims=True)
    du = rstd * (dxh - m1 - xhat * m2)
    return du, jnp.sum(dout * xhat, 0, keepdims=True), jnp.sum(dout, 0, keepdims=True)


def _dot(a, b):
    return jnp.dot(a, b, preferred_element_type=f32)


def _dot_nt(a, b):
    return lax.dot_general(a, b, (((1,), (1,)), ((), ())), preferred_element_type=f32)


def _dot_tn(a, b):
    return lax.dot_general(a, b, (((0,), (0,)), ((), ())), preferred_element_type=f32)


def _dot_hi(a, b):
    return jnp.dot(a, b, preferred_element_type=f32, precision=lax.Precision.HIGHEST)


def _shift_down(cur, prev, s):
    if s == 0:
        return cur
    row = lax.broadcasted_iota(jnp.int32, cur.shape, 0)
    return jnp.where(row < s, pltpu.roll(prev, s, 0), pltpu.roll(cur, s, 0))


def _shift_up(cur, nxt, s):
    if s == 0:
        return cur
    t = cur.shape[0]
    row = lax.broadcasted_iota(jnp.int32, cur.shape, 0)
    return jnp.where(row >= t - s, pltpu.roll(nxt, t - s, 0), pltpu.roll(cur, t - s, 0))


def _ln_in_fwd(x, gb, *, tt=512):
    L = x.shape[0]

    def body(x_ref, gb_ref, o_ref):
        o_ref[...] = _ln_fwd(x_ref[...], gb_ref[0:1, :], gb_ref[1:2, :])[0]

    return pl.pallas_call(
        body, name="ln_in_fwd", grid=(L // tt,),
        in_specs=[pl.BlockSpec((tt, D), lambda i: (i, 0)), _const_spec((8, D))],
        out_specs=pl.BlockSpec((tt, D), lambda i: (i, 0)),
        out_shape=jax.ShapeDtypeStruct((L, D), f32), compiler_params=_cp(("parallel",)))(x, gb)


def _ln_in_bwd(x, dy, gb, *, tt=512):
    L = x.shape[0]

    def body(x_ref, dy_ref, gb_ref, dx_ref, acc_ref):
        @pl.when(pl.program_id(0) == 0)
        def _():
            acc_ref[...] = jnp.zeros_like(acc_ref)
        _, xhat, rstd = _ln_fwd(x_ref[...], gb_ref[0:1, :], gb_ref[1:2, :])
        du, dg, db = _ln_bwd(dy_ref[...], xhat, rstd, gb_ref[0:1, :])
        dx_ref[...] = du
        acc_ref[0:1, :] += dg
        acc_ref[1:2, :] += db

    return pl.pallas_call(
        body, name="ln_in_bwd", grid=(L // tt,),
        in_specs=[pl.BlockSpec((tt, D), lambda i: (i, 0)), pl.BlockSpec((tt, D), lambda i: (i, 0)), _const_spec((8, D))],
        out_specs=[pl.BlockSpec((tt, D), lambda i: (i, 0)), pl.BlockSpec((8, D), lambda i: (0, 0))],
        out_shape=[jax.ShapeDtypeStruct((L, D), f32), jax.ShapeDtypeStruct((8, D), f32)],
        compiler_params=_cp(("arbitrary",)))(x, dy, gb)


def _loss_head(y, tgt, *, tt=512):
    L = y.shape[0]

    def body(y_ref, t_ref, dy_ref, acc_ref):
        @pl.when(pl.program_id(0) == 0)
        def _():
            acc_ref[...] = jnp.zeros_like(acc_ref)
        e = y_ref[...] - t_ref[...]
        dy_ref[...] = e * (1.0 / D)
        acc_ref[...] += 0.5 * jnp.sum(jnp.mean(e * e, -1, keepdims=True))

    return pl.pallas_call(
        body, name="loss_head", grid=(L // tt,),
        in_specs=[pl.BlockSpec((tt, D), lambda i: (i, 0)), pl.BlockSpec((tt, D), lambda i: (i, 0))],
        out_specs=[pl.BlockSpec((tt, D), lambda i: (i, 0)), pl.BlockSpec((8, 128), lambda i: (0, 0))],
        out_shape=[jax.ShapeDtypeStruct((L, D), f32), jax.ShapeDtypeStruct((8, 128), f32)],
        compiler_params=_cp(("arbitrary",)))(y, tgt)


FFN_CH = 4
FS = F // FFN_CH


def _ffn_fwd(x, mv, w_in, w_out, *, tt=256):
    L = x.shape[0]

    def body(x_ref, mv_ref, wi_ref, wo_ref, o_ref):
        x = x_ref[...]
        h = (x * (1.0 + mv_ref[1:2, :]) + mv_ref[0:1, :]).astype(bf16)
        y = jnp.zeros((tt, D), f32)
        for c in range(FFN_CH):
            g = _dot(h, wi_ref[c])
            u = _dot(h, wi_ref[c + FFN_CH])
            act = (g * _sigmoid(g) * u).astype(bf16)
            y = y + _dot(act, wo_ref[c * FS:(c + 1) * FS, :])
        uu = ALPHA * x + (0.5 * mv_ref[2:3, :]) * y
        o_ref[...] = _ln_fwd(uu, mv_ref[3:4, :], mv_ref[4:5, :])[0]

    return pl.pallas_call(
        body, name="ffn_fwd", grid=(L // tt,),
        in_specs=[pl.BlockSpec((tt, D), lambda i: (i, 0)), _const_spec((8, D)),
                  _const_spec((2 * FFN_CH, D, FS)), _const_spec((F, D))],
        out_specs=pl.BlockSpec((tt, D), lambda i: (i, 0)),
        out_shape=jax.ShapeDtypeStruct((L, D), f32), compiler_params=_cp(("parallel",)))(x, mv, w_in, w_out)


def _ffn_bwd(x, dxo, mv, w_in, w_out, *, tt=256):
    L = x.shape[0]

    def body(x_ref, dxo_ref, mv_ref, wi_ref, wo_ref, dx_ref, h_ref, da_ref, act_ref, dy_ref, acc_ref, a_scr):
        @pl.when(pl.program_id(0) == 0)
        def _():
            acc_ref[...] = jnp.zeros_like(acc_ref)
        x = x_ref[...]
        scale1 = 1.0 + mv_ref[1:2, :]
        h = (x * scale1 + mv_ref[0:1, :]).astype(bf16)
        h_ref[...] = h.T
        y = jnp.zeros((tt, D), f32)
        for c in range(FFN_CH):
            g = _dot(h, wi_ref[c])
            u = _dot(h, wi_ref[c + FFN_CH])
            a_scr[c] = g
            a_scr[c + FFN_CH] = u
            act = (g * _sigmoid(g) * u).astype(bf16)
            act_ref[c] = act.T
            y = y + _dot(act, wo_ref[c * FS:(c + 1) * FS, :])
        hg = 0.5 * mv_ref[2:3, :]
        _, xhat, rstd = _ln_fwd(ALPHA * x + hg * y, mv_ref[3:4, :], mv_ref[4:5, :])
        du, dlg, dlb = _ln_bwd(dxo_ref[...], xhat, rstd, mv_ref[3:4, :])
        acc_ref[3:4, :] += dlg
        acc_ref[4:5, :] += dlb
        acc_ref[2:3, :] += jnp.sum(0.5 * y * du, 0, keepdims=True)
        dyb = (hg * du).astype(bf16)
        dy_ref[...] = dyb
        dh = jnp.zeros((tt, D), f32)
        for c in range(FFN_CH):
            g = a_scr[c]
            u = a_scr[c + FFN_CH]
            dact = _dot_nt(dyb, wo_ref[c * FS:(c + 1) * FS, :])
            s = _sigmoid(g)
            dg = (dact * u * (s * (1.0 + g * (1.0 - s)))).astype(bf16)
            dup = (dact * (g * s)).astype(bf16)
            da_ref[c] = dg
            da_ref[c + FFN_CH] = dup
            dh = dh + _dot_nt(dg, wi_ref[c])
            dh = dh + _dot_nt(dup, wi_ref[c + FFN_CH])
        dx_ref[...] = ALPHA * du + dh * scale1
        acc_ref[0:1, :] += jnp.sum(dh, 0, keepdims=True)
        acc_ref[1:2, :] += jnp.sum(dh * x, 0, keepdims=True)

    tok = lambda w: pl.BlockSpec((tt, w), lambda i: (i, 0))
    by_chunk = lambda n: pl.BlockSpec((n, tt, FS), lambda i: (0, i, 0))
    return pl.pallas_call(
        body, name="ffn_bwd", grid=(L // tt,),
        in_specs=[tok(D), tok(D), _const_spec((8, D)), _const_spec((2 * FFN_CH, D, FS)), _const_spec((F, D))],
        out_specs=[tok(D), pl.BlockSpec((D, tt), lambda i: (0, i)), by_chunk(2 * FFN_CH),
                   pl.BlockSpec((FFN_CH, FS, tt), lambda i: (0, 0, i)), tok(D), pl.BlockSpec((8, D), lambda i: (0, 0))],
        out_shape=[jax.ShapeDtypeStruct((L, D), f32), jax.ShapeDtypeStruct((D, L), bf16),
                   jax.ShapeDtypeStruct((2 * FFN_CH, L, FS), bf16), jax.ShapeDtypeStruct((FFN_CH, FS, L), bf16),
                   jax.ShapeDtypeStruct((L, D), bf16), jax.ShapeDtypeStruct((8, D), f32)],
        scratch_shapes=[pltpu.VMEM((2 * FFN_CH, tt, FS), f32)],
        compiler_params=_cp(("arbitrary",)))(x, dxo, mv, w_in, w_out)


DW_TK = 1024


def _matmul_tokens(a, b, *, tn, tk, name):
    ga, gb = a.ndim == 3, b.ndim == 3
    G = a.shape[0] if ga else (b.shape[0] if gb else 1)
    M, K = a.shape[-2:]
    N = b.shape[-1]
    nk = K // tk

    def body(a_ref, b_ref, o_ref, acc):
        k = pl.program_id(2)
        p = _dot(a_ref[...], b_ref[...])

        @pl.when(k == 0)
        def _():
            acc[...] = p

        @pl.when(k > 0)
        def _():
            acc[...] += p

        @pl.when(k == nk - 1)
        def _():
            o_ref[...] = acc[...].astype(bf16)

    a_spec = (pl.BlockSpec((None, M, tk), lambda g, j, k: (g, 0, k)) if ga
              else pl.BlockSpec((M, tk), lambda g, j, k: (0, k)))
    b_spec = (pl.BlockSpec((None, tk, tn), lambda g, j, k: (g, k, j)) if gb
              else pl.BlockSpec((tk, tn), lambda g, j, k: (k, j)))
    if ga or gb:
        o_spec, o_shape = pl.BlockSpec((None, M, tn), lambda g, j, k: (g, 0, j)), (G, M, N)
    else:
        o_spec, o_shape = pl.BlockSpec((M, tn), lambda g, j, k: (0, j)), (M, N)
    return pl.pallas_call(
        body, name=name, grid=(G, N // tn, nk), in_specs=[a_spec, b_spec], out_specs=o_spec,
        out_shape=jax.ShapeDtypeStruct(o_shape, bf16), scratch_shapes=[pltpu.VMEM((M, tn), f32)],
        compiler_params=_cp(("parallel", "parallel", "arbitrary")))(a, b)


def _inproj_fwd(x, mv, w, *, tt=512):
    L = x.shape[0]

    def body(x_ref, mv_ref, w_ref, o_ref):
        h = (x_ref[...] * (1.0 + mv_ref[1:2, :]) + mv_ref[0:1, :]).astype(bf16)
        o_ref[...] = _dot(h, w_ref[...])

    return pl.pallas_call(
        body, name="inproj_fwd", grid=(L // tt,),
        in_specs=[pl.BlockSpec((tt, D), lambda i: (i, 0)), _const_spec((8, D)), _const_spec((D, P_W))],
        out_specs=pl.BlockSpec((tt, P_W), lambda i: (i, 0)),
        out_shape=jax.ShapeDtypeStruct((L, P_W), f32), compiler_params=_cp(("parallel",)))(x, mv, w)


def _inproj_bwd(x, dx_part, dproj, mv, w, *, tt=512):
    L = x.shape[0]

    def body(x_ref, dxp_ref, dp_ref, mv_ref, w_ref, dx_ref, h_ref, acc_ref):
        @pl.when(pl.program_id(0) == 0)
        def _():
            acc_ref[...] = jnp.zeros_like(acc_ref)
        x = x_ref[...]
        scale1 = 1.0 + mv_ref[1:2, :]
        h_ref[...] = (x * scale1 + mv_ref[0:1, :]).astype(bf16).T
        dh = _dot_nt(dp_ref[...], w_ref[...])
        dx_ref[...] = dxp_ref[...] + dh * scale1
        acc_ref[0:1, :] += jnp.sum(dh, 0, keepdims=True)
        acc_ref[1:2, :] += jnp.sum(dh * x, 0, keepdims=True)

    tok = lambda w_: pl.BlockSpec((tt, w_), lambda i: (i, 0))
    return pl.pallas_call(
        body, name="inproj_bwd", grid=(L // tt,),
        in_specs=[tok(D), tok(D), tok(P_W), _const_spec((8, D)), _const_spec((D, P_W))],
        out_specs=[tok(D), pl.BlockSpec((D, tt), lambda i: (0, i)), pl.BlockSpec((8, D), lambda i: (0, 0))],
        out_shape=[jax.ShapeDtypeStruct((L, D), f32), jax.ShapeDtypeStruct((D, L), bf16),
                   jax.ShapeDtypeStruct((8, D), f32)],
        compiler_params=_cp(("arbitrary",)))(x, dx_part, dproj, mv, w)


def _outproj_fwd(x, ycat, mv, w, *, tt=512):
    L = x.shape[0]

    def body(x_ref, y_ref, mv_ref, w_ref, o_ref):
        y = _dot(y_ref[...], w_ref[...])
        uu = ALPHA * x_ref[...] + mv_ref[2:3, :] * y
        o_ref[...] = _ln_fwd(uu, mv_ref[3:4, :], mv_ref[4:5, :])[0]

    tok = lambda w_: pl.BlockSpec((tt, w_), lambda i: (i, 0))
    return pl.pallas_call(
        body, name="outproj_fwd", grid=(L // tt,),
        in_specs=[tok(D), tok(D), _const_spec((8, D)), _const_spec((D, D))],
        out_specs=tok(D),
        out_shape=jax.ShapeDtypeStruct((L, D), f32), compiler_params=_cp(("parallel",)))(x, ycat, mv, w)


def _outproj_bwd(x, ycat, dxo, mv, w, *, tt=512):
    L = x.shape[0]

    def body(x_ref, y_ref, dxo_ref, mv_ref, w_ref, dx_ref, dy_ref, dyc_ref, yt_ref, acc_ref):
        @pl.when(pl.program_id(0) == 0)
        def _():
            acc_ref[...] = jnp.zeros_like(acc_ref)
        yt_ref[...] = y_ref[...].T
        y = _dot(y_ref[...], w_ref[...])
        gate = mv_ref[2:3, :]
        _, xhat, rstd = _ln_fwd(ALPHA * x_ref[...] + gate * y, mv_ref[3:4, :], mv_ref[4:5, :])
        du, dlg, dlb = _ln_bwd(dxo_ref[...], xhat, rstd, mv_ref[3:4, :])
        acc_ref[3:4, :] += dlg
        acc_ref[4:5, :] += dlb
        acc_ref[2:3, :] += jnp.sum(y * du, 0, keepdims=True)
        dx_ref[...] = ALPHA * du
        dyb = (gate * du).astype(bf16)
        dy_ref[...] = dyb
        dyc_ref[...] = _dot_nt(dyb, w_ref[...])

    tok = lambda w_: pl.BlockSpec((tt, w_), lambda i: (i, 0))
    return pl.pallas_call(
        body, name="outproj_bwd", grid=(L // tt,),
        in_specs=[tok(D), tok(D), tok(D), _const_spec((8, D)), _const_spec((D, D))],
        out_specs=[tok(D), tok(D), tok(D), pl.BlockSpec((D, tt), lambda i: (0, i)), pl.BlockSpec((8, D), lambda i: (0, 0))],
        out_shape=[jax.ShapeDtypeStruct((L, D), f32), jax.ShapeDtypeStruct((L, D), bf16),
                   jax.ShapeDtypeStruct((L, D), f32), jax.ShapeDtypeStruct((D, L), bf16),
                   jax.ShapeDtypeStruct((8, D), f32)],
        compiler_params=_cp(("arbitrary",)))(x, ycat, dxo, mv, w)


def _sconv_fwd(proj, w, *, tt=512):
    L = proj.shape[0]
    cb = SC_W

    def body(b_ref, c_ref, x_ref, cp_ref, xp_ref, w_ref, o_ref):
        first = jnp.where(pl.program_id(0) > 0, 1.0, 0.0)
        u = c_ref[...] * x_ref[...]
        up = cp_ref[...] * xp_ref[...] * first
        v = w_ref[2:3, :] * u + w_ref[1:2, :] * _shift_down(u, up, 1) + w_ref[0:1, :] * _shift_down(u, up, 2)
        o_ref[...] = (b_ref[...] * v).astype(bf16)

    cur = lambda col: pl.BlockSpec((tt, cb), lambda i: (i, col // cb))
    prev = lambda col: pl.BlockSpec((tt, cb), lambda i: (jnp.maximum(i - 1, 0), col // cb))
    return pl.pallas_call(
        body, name="sconv_fwd", grid=(L // tt,),
        in_specs=[cur(P_SB), cur(P_SC), cur(P_SX), prev(P_SC), prev(P_SX), _const_spec((8, cb))],
        out_specs=pl.BlockSpec((tt, cb), lambda i: (i, 0)),
        out_shape=jax.ShapeDtypeStruct((L, cb), bf16), compiler_params=_cp(("parallel",)))(proj, proj, proj, proj, proj, w)


def _sconv_bwd(proj, dycat, w, *, tt=512):
    L = proj.shape[0]
    cb = SC_W
    n = L // tt

    def body(b_ref, c_ref, x_ref, cp_ref, xp_ref, bn_ref, dy_ref, dyn_ref, w_ref, db_ref, dc_ref, dx_ref, acc_ref):
        i = pl.program_id(0)

        @pl.when(i == 0)
        def _():
            acc_ref[...] = jnp.zeros_like(acc_ref)
        first = jnp.where(i > 0, 1.0, 0.0)
        last = jnp.where(i < n - 1, 1.0, 0.0)
        cg, xin, bg = c_ref[...], x_ref[...], b_ref[...]
        u = cg * xin
        up = cp_ref[...] * xp_ref[...] * first
        u1, u2 = _shift_down(u, up, 1), _shift_down(u, up, 2)
        v = w_ref[2:3, :] * u + w_ref[1:2, :] * u1 + w_ref[0:1, :] * u2
        dy = dy_ref[...]
        db_ref[...] = (dy * v).astype(bf16)
        dv = dy * bg
        dvn = dyn_ref[...] * bn_ref[...] * last
        du = w_ref[2:3, :] * dv + w_ref[1:2, :] * _shift_up(dv, dvn, 1) + w_ref[0:1, :] * _shift_up(dv, dvn, 2)
        acc_ref[2:3, :] += jnp.sum(dv * u, 0, keepdims=True)
        acc_ref[1:2, :] += jnp.sum(dv * u1, 0, keepdims=True)
        acc_ref[0:1, :] += jnp.sum(dv * u2, 0, keepdims=True)
        dc_ref[...] = (du * xin).astype(bf16)
        dx_ref[...] = (du * cg).astype(bf16)

    cur = lambda col: pl.BlockSpec((tt, cb), lambda i: (i, col // cb))
    prev = lambda col: pl.BlockSpec((tt, cb), lambda i: (jnp.maximum(i - 1, 0), col // cb))
    nxt = lambda col: pl.BlockSpec((tt, cb), lambda i: (jnp.minimum(i + 1, n - 1), col // cb))
    ycol = SSD_W + FOX_W
    out = pl.BlockSpec((tt, cb), lambda i: (i, 0))
    return pl.pallas_call(
        body, name="sconv_bwd", grid=(n,),
        in_specs=[cur(P_SB), cur(P_SC), cur(P_SX), prev(P_SC), prev(P_SX), nxt(P_SB), cur(ycol), nxt(ycol),
                  _const_spec((8, cb))],
        out_specs=[out, out, out, pl.BlockSpec((8, cb), lambda i: (0, 0))],
        out_shape=[jax.ShapeDtypeStruct((L, cb), bf16)] * 3 + [jax.ShapeDtypeStruct((8, cb), f32)],
        compiler_params=_cp(("arbitrary",)))(proj, proj, proj, proj, proj, proj, dycat, dycat, w)


def _log1pexp(x):
    return jnp.log(1.0 + jnp.exp(-jnp.abs(x)))


def _fox_gate_fwd(proj, hp, *, tt=256):
    L = proj.shape[0]

    def body(sm_ref, hp_ref, cum_ref, cumt_ref, carry):
        @pl.when(pl.program_id(0) == 0)
        def _():
            carry[...] = jnp.zeros_like(carry)
        xx = sm_ref[...] + hp_ref[3:4, :]
        logf = jnp.minimum(xx, 0.0) - _log1pexp(xx)
        r = lax.broadcasted_iota(jnp.int32, (tt, tt), 0)
        c = lax.broadcasted_iota(jnp.int32, (tt, tt), 1)
        cum = _dot_hi(jnp.where(r >= c, 1.0, 0.0), logf) + carry[0:1, :]
        cum_ref[...] = cum
        cumt_ref[...] = cum.T
        carry[0:1, :] = cum[tt - 1:tt, :]

    return pl.pallas_call(
        body, name="fox_gate_fwd", grid=(L // tt,),
        in_specs=[pl.BlockSpec((tt, 128), lambda i: (i, P_SM // 128)), _const_spec((8, 128))],
        out_specs=[pl.BlockSpec((tt, 128), lambda i: (i, 0)), pl.BlockSpec((128, tt), lambda i: (0, i))],
        out_shape=[jax.ShapeDtypeStruct((L, 128), f32), jax.ShapeDtypeStruct((128, L), f32)],
        scratch_shapes=[pltpu.VMEM((8, 128), f32)],
        compiler_params=_cp(("arbitrary",)))(proj, hp)


def _fox_gate_bwd(dcumt, proj, ddt, hp, *, tt=256):
    L = proj.shape[0]
    n = L // tt

    def body(dct_ref, sm_ref, ddt_ref, hp_ref, dsm_ref, acc_ref, carry):
        @pl.when(pl.program_id(0) == 0)
        def _():
            carry[...] = jnp.zeros_like(carry)
            acc_ref[...] = jnp.zeros_like(acc_ref)
        dc = dct_ref[...].T
        r = lax.broadcasted_iota(jnp.int32, (tt, tt), 0)
        c = lax.broadcasted_iota(jnp.int32, (tt, tt), 1)
        dl = _dot_hi(jnp.where(r <= c, 1.0, 0.0), dc) + carry[0:1, :]
        carry[0:1, :] += jnp.sum(dc, 0, keepdims=True)
        xx = sm_ref[...] + hp_ref[3:4, :]
        lane = lax.broadcasted_iota(jnp.int32, (tt, 128), 1)
        dlogit = jnp.where((lane >= SM_F) & (lane < SM_F + FOX_H), dl * _sigmoid(-xx), 0.0)
        acc_ref[3:4, :] += jnp.sum(dlogit, 0, keepdims=True)
        dsm_ref[...] = (dlogit + ddt_ref[...]).astype(bf16)

    return pl.pallas_call(
        body, name="fox_gate_bwd", grid=(n,),
        in_specs=[pl.BlockSpec((128, tt), lambda i: (0, n - 1 - i)),
                  pl.BlockSpec((tt, 128), lambda i: (n - 1 - i, P_SM // 128)),
                  pl.BlockSpec((tt, 128), lambda i: (n - 1 - i, 0)), _const_spec((8, 128))],
        out_specs=[pl.BlockSpec((tt, 128), lambda i: (n - 1 - i, 0)), pl.BlockSpec((8, 128), lambda i: (0, 0))],
        out_shape=[jax.ShapeDtypeStruct((L, 128), bf16), jax.ShapeDtypeStruct((8, 128), f32)],
        scratch_shapes=[pltpu.VMEM((8, 128), f32)],
        compiler_params=_cp(("arbitrary",)))(dcumt, proj, ddt, hp)


NEG = -1e30
FOX_SCALE = FOX_HD ** -0.5


def _fox_fwd(proj, cum, cumt, *, tq=256):
    L = proj.shape[0]
    heads = [slice(h * FOX_HD, (h + 1) * FOX_HD) for h in range(FOX_H)]

    def body(q_ref, k_ref, v_ref, cq_ref, ct_ref, o_ref, oa_ref, lse_ref):
        i = pl.program_id(0)
        row = lax.broadcasted_iota(jnp.int32, (tq, tq), 0)
        col = lax.broadcasted_iota(jnp.int32, (tq, tq), 1)
        diag_bias = jnp.where(row >= col, 0.0, NEG)
        qs = [(q_ref[:, hs] * FOX_SCALE).astype(bf16) for hs in heads]
        cqs = [cq_ref[:, SM_F + h:SM_F + h + 1] for h in range(FOX_H)]

        def block(r0, carry, bias):
            out = []
            for h, hs in enumerate(heads):
                m, l, acc, acc_lo = carry[h]
                kj = k_ref[pl.ds(r0, tq), hs].astype(bf16)
                vj = v_ref[pl.ds(r0, tq), hs].astype(bf16)
                s = _dot_nt(qs[h], kj) + (cqs[h] - ct_ref[SM_F + h:SM_F + h + 1, pl.ds(r0, tq)])
                if bias is not None:
                    s = s + bias
                m_new = jnp.maximum(m, jnp.max(s, -1, keepdims=True))
                p = jnp.exp(s - m_new)
                al = jnp.exp(m - m_new)
                pb = p.astype(bf16)
                p_lo = (p - pb.astype(f32)).astype(bf16)
                out.append((m_new, al * l + jnp.sum(p, -1, keepdims=True),
                            al * acc + _dot(pb, vj), al * acc_lo + _dot(p_lo, vj)))
            return tuple(out)

        init = tuple((jnp.full((tq, 1), NEG, f32), jnp.zeros((tq, 1), f32), jnp.zeros((tq, FOX_HD), f32),
                      jnp.zeros((tq, FOX_HD), f32)) for _ in heads)
        carry = lax.fori_loop(0, i, lambda j, c: block(pl.multiple_of(j * tq, tq), c, None), init)
        carry = block(pl.multiple_of(i * tq, tq), carry, diag_bias)
        lane = lax.broadcasted_iota(jnp.int32, (tq, 128), 1)
        lse_all = jnp.zeros((tq, 128), f32)
        for h, hs in enumerate(heads):
            m, l, acc, acc_lo = carry[h]
            inv = 1.0 / l
            o_ref[:, hs] = acc * inv
            oa_ref[:, hs] = (acc + acc_lo) * inv
            lse_all = jnp.where(lane == h, m + jnp.log(l), lse_all)
        lse_ref[...] = lse_all

    return pl.pallas_call(
        body, name="fox_fwd", grid=(L // tq,),
        in_specs=[pl.BlockSpec((tq, FOX_W), lambda i: (i, P_Q // FOX_W)),
                  pl.BlockSpec((L, FOX_W), lambda i: (0, P_K // FOX_W), pipeline_mode=pl.Buffered(1)),
                  pl.BlockSpec((L, FOX_W), lambda i: (0, P_V // FOX_W), pipeline_mode=pl.Buffered(1)),
                  pl.BlockSpec((tq, 128), lambda i: (i, 0)), _const_spec((128, L))],
        out_specs=[pl.BlockSpec((tq, FOX_W), lambda i: (i, 0)), pl.BlockSpec((tq, FOX_W), lambda i: (i, 0)),
                   pl.BlockSpec((tq, 128), lambda i: (i, 0))],
        out_shape=[jax.ShapeDtypeStruct((L, FOX_W), f32), jax.ShapeDtypeStruct((L, FOX_W), f32),
                   jax.ShapeDtypeStruct((L, 128), f32)],
        compiler_params=_cp(("parallel",)))(proj, proj, proj, cum, cumt)


def _fox_bwd(proj, dycat, o_acc, lse, cum, cumt, *, tq=256):
    L = proj.shape[0]
    nq = L // tq
    heads = [slice(h * FOX_HD, (h + 1) * FOX_HD) for h in range(FOX_H)]

    def body(q_ref, k_ref, v_ref, do_ref, oa_ref, lse_ref, cum_ref, ct_ref, dq_ref, dk_ref, dv_ref, dct_ref):
        j = pl.program_id(0)

        @pl.when(j == 0)
        def _():
            dq_ref[...] = jnp.zeros_like(dq_ref)
        row = lax.broadcasted_iota(jnp.int32, (tq, tq), 0)
        col = lax.broadcasted_iota(jnp.int32, (tq, tq), 1)
        diag_bias = jnp.where(row >= col, 0.0, NEG)
        kjs = [k_ref[:, hs].astype(bf16) for hs in heads]
        vjs = [v_ref[:, hs].astype(bf16) for hs in heads]
        cks = [ct_ref[SM_F + h:SM_F + h + 1, :] for h in range(FOX_H)]

        def block(r0, carry, bias):
            out = []
            for h, hs in enumerate(heads):
                dk, dv, dc = carry[h]
                qi = (q_ref[pl.ds(r0, tq), hs] * FOX_SCALE).astype(bf16)
                dob = do_ref[pl.ds(r0, tq), hs].astype(bf16)
                delta = jnp.sum(dob.astype(f32) * oa_ref[pl.ds(r0, tq), hs], -1, keepdims=True)
                s = _dot_nt(qi, kjs[h]) + (cum_ref[pl.ds(r0, tq), SM_F + h:SM_F + h + 1] - cks[h])
                if bias is not None:
                    s = s + bias
                p = jnp.exp(s - lse_ref[pl.ds(r0, tq), h:h + 1])
                ds = p * (_dot_nt(dob, vjs[h]) - delta)
                dsb = ds.astype(bf16)
                dq_ref[pl.ds(r0, tq), hs] += _dot(dsb, kjs[h]) * FOX_SCALE
                out.append((dk + _dot_tn(dsb, qi), dv + _dot_tn(p.astype(bf16), dob),
                            dc - jnp.sum(ds, 0, keepdims=True)))
            return tuple(out)

        init = tuple((jnp.zeros((tq, FOX_HD), f32), jnp.zeros((tq, FOX_HD), f32), jnp.zeros((1, tq), f32))
                     for _ in heads)
        carry = block(pl.multiple_of(j * tq, tq), init, diag_bias)
        carry = lax.fori_loop(j + 1, nq, lambda i, c: block(pl.multiple_of(i * tq, tq), c, None), carry)
        sub = lax.broadcasted_iota(jnp.int32, (128, tq), 0)
        dct = jnp.zeros((128, tq), f32)
        for h, hs in enumerate(heads):
            dk, dv, dc = carry[h]
            dk_ref[:, hs] = dk
            dv_ref[:, hs] = dv
            dct = jnp.where(sub == SM_F + h, dc, dct)
        dct_ref[...] = dct

    full = lambda w_, col: pl.BlockSpec((L, w_), lambda j: (0, col // w_), pipeline_mode=pl.Buffered(1))
    blk = lambda col: pl.BlockSpec((tq, FOX_W), lambda j: (j, col // FOX_W))
    return pl.pallas_call(
        body, name="fox_bwd", grid=(nq,),
        in_specs=[full(FOX_W, P_Q), blk(P_K), blk(P_V), full(FOX_W, SSD_W), full(FOX_W, 0), full(128, 0), full(128, 0),
                  pl.BlockSpec((128, tq), lambda j: (0, j))],
        out_specs=[pl.BlockSpec((L, FOX_W), lambda j: (0, 0)), pl.BlockSpec((tq, FOX_W), lambda j: (j, 0)),
                   pl.BlockSpec((tq, FOX_W), lambda j: (j, 0)), pl.BlockSpec((128, tq), lambda j: (0, j))],
        out_shape=[jax.ShapeDtypeStruct((L, FOX_W), f32)] * 3 + [jax.ShapeDtypeStruct((128, L), f32)],
        compiler_params=_cp(("arbitrary",)))(proj, proj, proj, dycat, o_acc, lse, cum, cumt)


HL = 128
AW = FOX_H * HL


def _np_place(rows, cols, pairs, dtype):
    import numpy as np
    m = np.zeros((rows, cols), np.float32)
    for r, c in pairs:
        m[r, c] = 1.0
    return jnp.asarray(m, dtype)


def _fox_consts():
    data = [(h * FOX_HD + d, h * HL + d) for h in range(FOX_H) for d in range(FOX_HD)]
    return dict(
        pq=_np_place(FOX_W, AW, data, bf16),
        pqt=_np_place(AW, FOX_W, [(c, r) for r, c in data], bf16),
        cum_a=[_np_place(128, AW, [(SM_F + h, h * HL + 64 + r) for h in range(FOX_H)], bf16) for r in range(3)],
        head_a=[_np_place(128, AW, [(h, h * HL + 64 + r) for h in range(FOX_H)], bf16) for r in range(3)],
        head_b=[_np_place(128, AW, [(h, h * HL + 67 + r) for h in range(FOX_H)], bf16) for r in range(3)],
        group=_np_place(FOX_W, 128, [(h * FOX_HD + d, h) for h in range(FOX_H) for d in range(FOX_HD)], f32),
        col_a=_np_place(AW, 128, [(h * HL + 64, SM_F + h) for h in range(FOX_H)], f32))


def _split3(x):
    hi = x.astype(bf16)
    r1 = x - hi.astype(f32)
    mid = r1.astype(bf16)
    return hi, mid, (r1 - mid.astype(f32)).astype(bf16)


def _slot_ones(tt, first):
    lane = lax.broadcasted_iota(jnp.int32, (tt, AW), 1) % HL
    return jnp.where((lane >= first) & (lane < first + 3), 1.0, 0.0)


def _fox_prep(proj, hp, cst, *, tt=256):
    L = proj.shape[0]

    def body(q_ref, k_ref, v_ref, sm_ref, hp_ref, pq_ref, c0_ref, c1_ref, c2_ref, qa_ref, ka_ref, va_ref, tot_ref):
        xx = sm_ref[...] + hp_ref[3:4, :]
        logf = jnp.minimum(xx, 0.0) - _log1pexp(xx)
        r = lax.broadcasted_iota(jnp.int32, (tt, tt), 0)
        c = lax.broadcasted_iota(jnp.int32, (tt, tt), 1)
        cum = _dot_hi(jnp.where(r >= c, 1.0, 0.0), logf)
        tot_ref[...] = jnp.broadcast_to(cum[tt - 1:tt, :], (8, 128))
        parts = _split3(-cum)
        pq = pq_ref[...]
        a_ones, b_ones = _slot_ones(tt, 64), _slot_ones(tt, 67)
        qa_ref[...] = (_dot((q_ref[...] * FOX_SCALE).astype(bf16), pq) + a_ones).astype(bf16)
        ka = _dot(k_ref[...].astype(bf16), pq) + b_ones
        for part, c_ref in zip(parts, (c0_ref, c1_ref, c2_ref)):
            ka = ka + _dot(part, c_ref[...])
        ka_ref[...] = ka.astype(bf16)
        va_ref[...] = (_dot(v_ref[...].astype(bf16), pq) + a_ones).astype(bf16)

    col = lambda c_: pl.BlockSpec((tt, FOX_W), lambda i: (i, c_ // FOX_W))
    out = pl.BlockSpec((tt, AW), lambda i: (i, 0))
    return pl.pallas_call(
        body, name="fox_prep", grid=(L // tt,),
        in_specs=[col(P_Q), col(P_K), col(P_V), pl.BlockSpec((tt, 128), lambda i: (i, P_SM // 128)),
                  _const_spec((8, 128)), _const_spec((FOX_W, AW))] + [_const_spec((128, AW))] * 3,
        out_specs=[out, out, out, pl.BlockSpec((8, 128), lambda i: (i, 0))],
        out_shape=[jax.ShapeDtypeStruct((L, AW), bf16)] * 3 + [jax.ShapeDtypeStruct((8 * (L // tt), 128), f32)],
        compiler_params=_cp(("parallel",)))(proj, proj, proj, proj, hp, cst["pq"], *cst["cum_a"])


def _fox_attn_fwd(qa, ka, va, tot, *, tq=256):
    L = qa.shape[0]

    def body(qa_ref, ka_ref, va_ref, tot_ref, o_ref, oa_ref, lse_ref):
        i = pl.program_id(0)
        row = lax.broadcasted_iota(jnp.int32, (tq, tq), 0)
        col = lax.broadcasted_iota(jnp.int32, (tq, tq), 1)
        diag_bias = jnp.where(row >= col, 0.0, NEG)

        def block(j, carry, t_j, bias):
            r0 = pl.multiple_of(j * tq, tq)
            out = []
            for h in range(FOX_H):
                hl = slice(h * HL, (h + 1) * HL)
                m, acc = carry[h]
                m_in = m - t_j[:, SM_F + h:SM_F + h + 1]
                s = _dot_nt(qa_ref[:, hl], ka_ref[pl.ds(r0, tq), hl])
                if bias is not None:
                    s = s + bias
                m_new = jnp.maximum(m_in, jnp.max(s, -1, keepdims=True))
                p = jnp.exp(s - m_new)
                al = jnp.exp(m_in - m_new)
                pb = p.astype(bf16)
                p_lo = (p - pb.astype(f32)).astype(bf16)
                vj = va_ref[pl.ds(r0, tq), hl]
                out.append((m_new, al * acc + (_dot(pb, vj) + _dot(p_lo, vj))))
            return tuple(out)

        def step(k, state):
            carry, gap = state
            j = i - 1 - k
            t_j = tot_ref[pl.ds(pl.multiple_of(j * 8, 8), 1), :]
            return block(j, carry, t_j, None), gap + t_j

        init = tuple((jnp.full((tq, 1), NEG, f32), jnp.zeros((tq, HL), f32)) for _ in range(FOX_H))
        zero_gap = jnp.zeros((1, 128), f32)
        carry = block(i, init, zero_gap, diag_bias)
        carry, gap = lax.fori_loop(0, i, step, (carry, zero_gap))
        lane = lax.broadcasted_iota(jnp.int32, (tq, 128), 1)
        lse_all = jnp.zeros((tq, 128), f32)
        for h in range(FOX_H):
            hs = slice(h * FOX_HD, (h + 1) * FOX_HD)
            m, acc = carry[h]
            l = acc[:, FOX_HD:FOX_HD + 1]
            o = acc[:, :FOX_HD] * (1.0 / l)
            o_ref[:, hs] = o.astype(bf16)
            oa_ref[:, hs] = o
            lse_all = jnp.where(lane == h, m + gap[:, SM_F + h:SM_F + h + 1] + jnp.log(l), lse_all)
        lse_ref[...] = lse_all

    full = pl.BlockSpec((L, AW), lambda i: (0, 0), pipeline_mode=pl.Buffered(1))
    return pl.pallas_call(
        body, name="fox_fwd", grid=(L // tq,),
        in_specs=[pl.BlockSpec((tq, AW), lambda i: (i, 0)), full, full, _const_spec(tot.shape)],
        out_specs=[pl.BlockSpec((tq, FOX_W), lambda i: (i, 0)), pl.BlockSpec((tq, FOX_W), lambda i: (i, 0)),
                   pl.BlockSpec((tq, 128), lambda i: (i, 0))],
        out_shape=[jax.ShapeDtypeStruct((L, FOX_W), bf16), jax.ShapeDtypeStruct((L, FOX_W), f32),
                   jax.ShapeDtypeStruct((L, 128), f32)],
        compiler_params=_cp(("parallel",)))(qa, ka, va, tot)


def _fox_bprep(qa, dycat, o_acc, lse, cst, *, tt=256):
    L = qa.shape[0]

    def body(qa_ref, do_ref, oa_ref, lse_ref, pq_ref, g_ref, a0, a1, a2, b0, b1, b2, qb_ref, doa_ref, qbt_ref, doat_ref):
        dob = do_ref[...].astype(bf16)
        delta = _dot_hi(dob.astype(f32) * oa_ref[...], g_ref[...])
        doa = _dot(dob, pq_ref[...])
        for part, ref in zip(_split3(-delta), (a0, a1, a2)):
            doa = doa + _dot(part, ref[...])
        qb = qa_ref[...].astype(f32)
        for part, ref in zip(_split3(-lse_ref[...]), (b0, b1, b2)):
            qb = qb + _dot(part, ref[...])
        doa, qb = doa.astype(bf16), qb.astype(bf16)
        doa_ref[...] = doa
        qb_ref[...] = qb
        doat_ref[...] = doa.T
        qbt_ref[...] = qb.T

    tok = lambda w_: pl.BlockSpec((tt, w_), lambda i: (i, 0))
    tr = pl.BlockSpec((AW, tt), lambda i: (0, i))
    return pl.pallas_call(
        body, name="fox_bprep", grid=(L // tt,),
        in_specs=[tok(AW), pl.BlockSpec((tt, FOX_W), lambda i: (i, SSD_W // FOX_W)), tok(FOX_W), tok(128),
                  _const_spec((FOX_W, AW)), _const_spec((FOX_W, 128))] + [_const_spec((128, AW))] * 6,
        out_specs=[tok(AW), tok(AW), tr, tr],
        out_shape=[jax.ShapeDtypeStruct((L, AW), bf16)] * 2 + [jax.ShapeDtypeStruct((AW, L), bf16)] * 2,
        compiler_params=_cp(("parallel",)))(qa, dycat, o_acc, lse, cst["pq"], cst["group"], *cst["head_a"], *cst["head_b"])


def _fox_attn_bwd(ka, va, tot, qb, doa, qbt, doat, *, tq=256):
    L = ka.shape[0]
    nq = L // tq

    def body(ka_ref, va_ref, tot_ref, qb_ref, doa_ref, qbt_ref, doat_ref, dq_ref, dkt_ref, dvt_ref):
        j = pl.program_id(0)

        @pl.when(j == 0)
        def _():
            dq_ref[...] = jnp.zeros_like(dq_ref)
        row = lax.broadcasted_iota(jnp.int32, (tq, tq), 0)
        col = lax.broadcasted_iota(jnp.int32, (tq, tq), 1)
        diag_bias = jnp.where(row >= col, 0.0, NEG)

        def block(i, carry, gap, bias):
            r0 = pl.multiple_of(i * tq, tq)
            out = []
            for h in range(FOX_H):
                hl = slice(h * HL, (h + 1) * HL)
                dkt, dvt, dsum = carry[h]
                kj = ka_ref[:, hl]
                s = _dot_nt(qb_ref[pl.ds(r0, tq), hl], kj) + gap[:, SM_F + h:SM_F + h + 1]
                if bias is not None:
                    s = s + bias
                p = jnp.exp(s)
                ds = p * _dot_nt(doa_ref[pl.ds(r0, tq), hl], va_ref[:, hl])
                dsb = ds.astype(bf16)
                dq_ref[pl.ds(r0, tq), hl] += _dot(dsb, kj)
                out.append((dkt + _dot(qbt_ref[hl, pl.ds(r0, tq)], dsb),
                            dvt + _dot(doat_ref[hl, pl.ds(r0, tq)], p.astype(bf16)),
                            dsum + jnp.sum(ds, 0, keepdims=True)))
            return tuple(out)

        init = tuple((jnp.zeros((HL, tq), f32), jnp.zeros((HL, tq), f32), jnp.zeros((1, tq), f32))
                     for _ in range(FOX_H))
        def step(i, state):
            carry, gap = state
            gap = gap + tot_ref[pl.ds(pl.multiple_of((i - 1) * 8, 8), 1), :]
            return block(i, carry, gap, None), gap

        zero_gap = jnp.zeros((1, 128), f32)
        carry = block(j, init, zero_gap, diag_bias)
        carry, _ = lax.fori_loop(j + 1, nq, step, (carry, zero_gap))
        for h in range(FOX_H):
            hl = slice(h * HL, (h + 1) * HL)
            dkt_ref[hl, :] = carry[h][0]
            dvt_ref[hl, :] = carry[h][1]
            dkt_ref[h * HL + FOX_HD:h * HL + FOX_HD + 1, :] = carry[h][2]

    full = lambda shape: pl.BlockSpec(shape, lambda j: (0, 0), pipeline_mode=pl.Buffered(1))
    blk = pl.BlockSpec((tq, AW), lambda j: (j, 0))
    trb = pl.BlockSpec((AW, tq), lambda j: (0, j))
    return pl.pallas_call(
        body, name="fox_bwd", grid=(nq,),
        in_specs=[blk, blk, full(tot.shape), full((L, AW)), full((L, AW)), full((AW, L)), full((AW, L))],
        out_specs=[pl.BlockSpec((L, AW), lambda j: (0, 0)), trb, trb],
        out_shape=[jax.ShapeDtypeStruct((L, AW), f32), jax.ShapeDtypeStruct((AW, L), f32),
                   jax.ShapeDtypeStruct((AW, L), f32)],
        compiler_params=_cp(("arbitrary",)))(ka, va, tot, qb, doa, qbt, doat)


def _fox_post(dq, dkt, dvt, proj, ddt, hp, cst, *, tt=256):
    L = proj.shape[0]
    n = L // tt

    def body(dq_ref, dkt_ref, dvt_ref, sm_ref, ddt_ref, hp_ref, pqt_ref, ca_ref,
             dqo_ref, dko_ref, dvo_ref, dsm_ref, acc_ref, carry):
        @pl.when(pl.program_id(0) == 0)
        def _():
            carry[...] = jnp.zeros_like(carry)
            acc_ref[...] = jnp.zeros_like(acc_ref)
        pqt = pqt_ref[...]
        dk_full = dkt_ref[...].T
        dqo_ref[...] = _dot((dq_ref[...] * FOX_SCALE).astype(bf16), pqt).astype(bf16)
        dko_ref[...] = _dot(dk_full.astype(bf16), pqt).astype(bf16)
        dvo_ref[...] = _dot(dvt_ref[...].T.astype(bf16), pqt).astype(bf16)
        dc = -_dot_hi(dk_full, ca_ref[...])
        r = lax.broadcasted_iota(jnp.int32, (tt, tt), 0)
        c = lax.broadcasted_iota(jnp.int32, (tt, tt), 1)
        dl = _dot_hi(jnp.where(r <= c, 1.0, 0.0), dc) + carry[0:1, :]
        carry[0:1, :] += jnp.sum(dc, 0, keepdims=True)
        xx = sm_ref[...] + hp_ref[3:4, :]
        lane = lax.broadcasted_iota(jnp.int32, (tt, 128), 1)
        dlogit = jnp.where((lane >= SM_F) & (lane < SM_F + FOX_H), dl * _sigmoid(-xx), 0.0)
        acc_ref[3:4, :] += jnp.sum(dlogit, 0, keepdims=True)
        dsm_ref[...] = (dlogit + ddt_ref[...]).astype(bf16)

    rev = lambda w_: pl.BlockSpec((tt, w_), lambda i: (n - 1 - i, 0))
    revt = pl.BlockSpec((AW, tt), lambda i: (0, n - 1 - i))
    return pl.pallas_call(
        body, name="fox_post", grid=(n,),
        in_specs=[rev(AW), revt, revt, pl.BlockSpec((tt, 128), lambda i: (n - 1 - i, P_SM // 128)), rev(128),
                  _const_spec((8, 128)), _const_spec((AW, FOX_W)), _const_spec((AW, 128))],
        out_specs=[rev(FOX_W), rev(FOX_W), rev(FOX_W), rev(128), pl.BlockSpec((8, 128), lambda i: (0, 0))],
        out_shape=[jax.ShapeDtypeStruct((L, FOX_W), bf16)] * 3 + [jax.ShapeDtypeStruct((L, 128), bf16),
                                                                   jax.ShapeDtypeStruct((8, 128), f32)],
        scratch_shapes=[pltpu.VMEM((8, 128), f32)],
        compiler_params=_cp(("arbitrary",)))(dq, dkt, dvt, proj, ddt, hp, cst["pqt"], cst["col_a"])


SSD_GW = SSD_W // SSD_G
SSD_HPG = SSD_H // SSD_G


def _ssd_pre(x, xprev, sm, cp_ref, hp_ref, tc):
    pre = (cp_ref[4:5, :] + cp_ref[3:4, :] * x + cp_ref[2:3, :] * _shift_down(x, xprev, 1)
           + cp_ref[1:2, :] * _shift_down(x, xprev, 2) + cp_ref[0:1, :] * _shift_down(x, xprev, 3))
    sig = _sigmoid(pre)
    raw = sm + hp_ref[0:1, :]
    dt = jnp.maximum(raw, 0.0) + _log1pexp(raw)
    a_neg = -jnp.exp(hp_ref[1:2, :])
    r = lax.broadcasted_iota(jnp.int32, (tc, tc), 0)
    c = lax.broadcasted_iota(jnp.int32, (tc, tc), 1)
    cs = _dot_hi(jnp.where(r >= c, 1.0, 0.0), dt * a_neg)
    return pre, sig, raw, dt, a_neg, cs, cs.T, r >= c


def _ssd_fwd(proj, cp, hp, ng, *, tc=256):
    L = proj.shape[0]
    nc = L // tc

    def body(xc_ref, xp_ref, z_ref, sm_ref, cp_ref, hp_ref, ng_ref, y_ref, ypre_ref, sin_ref, s_scr):
        i = pl.program_id(0)

        @pl.when(i == 0)
        def _():
            s_scr[...] = jnp.zeros_like(s_scr)
        x = xc_ref[...]
        xprev = xp_ref[...] * jnp.where(i > 0, 1.0, 0.0)
        pre, sig, _, dt, _, cs, cst, tril = _ssd_pre(x, xprev, sm_ref[...], cp_ref, hp_ref, tc)
        xbc = pre * sig
        sin_ref[...] = s_scr[...]
        for g in range(SSD_G):
            bg = xbc[:, SSD_W + g * SSD_N:SSD_W + (g + 1) * SSD_N]
            cg = xbc[:, SSD_W + SSD_G * SSD_N + g * SSD_N:SSD_W + SSD_G * SSD_N + (g + 1) * SSD_N].astype(bf16)
            cb = _dot_nt(cg, bg.astype(bf16))
            for e in range(SSD_HPG):
                h = g * SSD_HPG + e
                hs = slice(h * SSD_HD, (h + 1) * SSD_HD)
                xs = xbc[:, hs]
                csc = cs[:, h:h + 1]
                lm = jnp.where(tril, jnp.exp(jnp.minimum(csc - cst[h:h + 1, :], 0.0)), 0.0)
                xdt = (xs * dt[:, h:h + 1]).astype(bf16)
                s_h = s_scr[:, hs]
                y = _dot((cb * lm).astype(bf16), xdt) + jnp.exp(csc) * _dot(cg, s_h.astype(bf16))
                ypre_ref[:, hs] = y + hp_ref[2:3, h:h + 1] * xs
                cl = cs[tc - 1:tc, h:h + 1]
                bd = (bg * jnp.exp(cl - csc)).astype(bf16)
                s_scr[:, hs] = jnp.exp(cl) * s_h + _dot_tn(bd, xdt)
        z = z_ref[...]
        yz = ypre_ref[...] * (z * _sigmoid(z))
        for g in range(SSD_G):
            gs = slice(g * SSD_GW, (g + 1) * SSD_GW)
            yg = yz[:, gs]
            r = lax.rsqrt(jnp.mean(yg * yg, -1, keepdims=True) + RMS_EPS)
            y_ref[:, gs] = (yg * r * ng_ref[0:1, gs]).astype(bf16)

    return pl.pallas_call(
        body, name="ssd_fwd", grid=(nc,),
        in_specs=[pl.BlockSpec((tc, 1024), lambda i: (i, 0)),
                  pl.BlockSpec((tc, 1024), lambda i: (jnp.maximum(i - 1, 0), 0)),
                  pl.BlockSpec((tc, SSD_W), lambda i: (i, P_Z // SSD_W)),
                  pl.BlockSpec((tc, 128), lambda i: (i, P_SM // 128)),
                  _const_spec((8, 1024)), _const_spec((8, 128)), _const_spec((8, SSD_W))],
        out_specs=[pl.BlockSpec((tc, SSD_W), lambda i: (i, 0)), pl.BlockSpec((tc, SSD_W), lambda i: (i, 0)),
                   pl.BlockSpec((SSD_N, SSD_W), lambda i: (i, 0))],
        out_shape=[jax.ShapeDtypeStruct((L, SSD_W), bf16), jax.ShapeDtypeStruct((L, SSD_W), f32),
                   jax.ShapeDtypeStruct((nc * SSD_N, SSD_W), f32)],
        scratch_shapes=[pltpu.VMEM((SSD_N, SSD_W), f32)],
        compiler_params=_cp(("arbitrary",)))(proj, proj, proj, proj, cp, hp, ng)


def _ssd_bwd(proj, dycat, ypre, sin, cp, hp, ng, *, tc=256):
    L = proj.shape[0]
    nc = L // tc

    def body(xc_ref, xp_ref, z_ref, sm_ref, cp_ref, hp_ref, ng_ref, sin_ref, ypre_ref, dy_ref,
             dxbc_ref, dz_ref, ddt_ref, acc1_ref, acc2_ref, ds_scr, dnext_scr, dxbc_scr):
        i = pl.program_id(0)
        c_idx = nc - 1 - i

        @pl.when(i == 0)
        def _():
            ds_scr[...] = jnp.zeros_like(ds_scr)
            dnext_scr[...] = jnp.zeros_like(dnext_scr)
            acc1_ref[...] = jnp.zeros_like(acc1_ref)
            acc2_ref[...] = jnp.zeros_like(acc2_ref)
        x = xc_ref[...]
        xprev = xp_ref[...] * jnp.where(c_idx > 0, 1.0, 0.0)
        pre, sig, raw, dt, a_neg, cs, cst, tril = _ssd_pre(x, xprev, sm_ref[...], cp_ref, hp_ref, tc)
        xbc = pre * sig
        z = z_ref[...]
        sz = _sigmoid(z)
        silz = z * sz
        yall = ypre_ref[...]
        yz = yall * silz
        dy = dy_ref[...]
        dyz_parts = []
        for g in range(SSD_G):
            gs = slice(g * SSD_GW, (g + 1) * SSD_GW)
            yg, dyg = yz[:, gs], dy[:, gs]
            r = lax.rsqrt(jnp.mean(yg * yg, -1, keepdims=True) + RMS_EPS)
            acc1_ref[5:6, gs] += jnp.sum(dyg * yg * r, 0, keepdims=True)
            dyn = dyg * ng_ref[0:1, gs]
            dyz_parts.append(r * (dyn - yg * (r * r) * jnp.mean(dyn * yg, -1, keepdims=True)))
        dyz = jnp.concatenate(dyz_parts, axis=1)
        dz_ref[...] = (dyz * yall * (sz * (1.0 + z * (1.0 - sz)))).astype(bf16)
        dyall = dyz * silz

        lane1 = lax.broadcasted_iota(jnp.int32, (1, 128), 1)
        sub = lax.broadcasted_iota(jnp.int32, (128, tc), 0)
        rowc = lax.broadcasted_iota(jnp.int32, (tc, 1), 0)
        dcs = jnp.zeros((tc, 128), f32)
        dcsr = jnp.zeros((128, tc), f32)
        ddt = jnp.zeros((tc, 128), f32)
        dd_row = jnp.zeros((1, 128), f32)
        for g in range(SSD_G):
            b0 = SSD_W + g * SSD_N
            c0 = SSD_W + SSD_G * SSD_N + g * SSD_N
            bg = xbc[:, b0:b0 + SSD_N]
            bgb = bg.astype(bf16)
            cgb = xbc[:, c0:c0 + SSD_N].astype(bf16)
            cb = _dot_nt(cgb, bgb)
            dbg = jnp.zeros((tc, SSD_N), f32)
            dcg = jnp.zeros((tc, SSD_N), f32)
            for e in range(SSD_HPG):
                h = g * SSD_HPG + e
                hs = slice(h * SSD_HD, (h + 1) * SSD_HD)
                oh = jnp.where(lane1 == h, 1.0, 0.0)
                xs = xbc[:, hs]
                dth = dt[:, h:h + 1]
                csc = cs[:, h:h + 1]
                lm = jnp.where(tril, jnp.exp(jnp.minimum(csc - cst[h:h + 1, :], 0.0)), 0.0)
                m = cb * lm
                xdt = (xs * dth).astype(bf16)
                s_h = sin_ref[:, hs]
                s_hb = s_h.astype(bf16)
                dyh = dyall[:, hs]
                dyb = dyh.astype(bf16)
                dd_row = dd_row + oh * jnp.sum(dyh * xs)
                dxs = hp_ref[2:3, h:h + 1] * dyh
                ecs = jnp.exp(csc)
                cs_prod = _dot(cgb, s_hb)
                dcsb = (ecs * dyh).astype(bf16)
                dcg = dcg + _dot_nt(dcsb, s_hb)
                ds_in = _dot_tn(cgb, dcsb)
                dcs_h = jnp.sum(dyh * ecs * cs_prod, -1, keepdims=True)
                dm = _dot_nt(dyb, xdt)
                w = dm * m
                dcs_h = dcs_h + jnp.sum(w, -1, keepdims=True)
                dcsr = jnp.where(sub == h, jnp.sum(w, 0, keepdims=True), dcsr)
                dcbb = (dm * lm).astype(bf16)
                dcg = dcg + _dot(dcbb, bgb)
                dbg = dbg + _dot_tn(dcbb, cgb)
                dxdt = _dot_tn(m.astype(bf16), dyb)
                dsn = ds_scr[:, hs]
                dsnb = dsn.astype(bf16)
                cl = cs[tc - 1:tc, h:h + 1]
                dec = jnp.exp(cl - csc)
                dxdt = dxdt + _dot((bg * dec).astype(bf16), dsnb)
                dbd = _dot_nt(xdt, dsnb)
                dbg = dbg + dbd * dec
                gdec = jnp.sum(dbd * bg, -1, keepdims=True) * dec
                ecl = jnp.exp(cl)
                dcl = jnp.sum(gdec) + jnp.sum(dsn * s_h) * ecl
                ds_scr[:, hs] = ecl * dsn + ds_in
                dcs_h = dcs_h - gdec + jnp.where(rowc == tc - 1, dcl, 0.0)
                dcs = dcs + dcs_h * oh
                dxbc_scr[:, hs] = dxs + dxdt * dth
                ddt = ddt + jnp.sum(dxdt * xs, -1, keepdims=True) * oh
            dxbc_scr[:, b0:b0 + SSD_N] = dbg
            dxbc_scr[:, c0:c0 + SSD_N] = dcg
        dcs = dcs - dcsr.T
        r_i = lax.broadcasted_iota(jnp.int32, (tc, tc), 0)
        c_i = lax.broadcasted_iota(jnp.int32, (tc, tc), 1)
        da = _dot_hi(jnp.where(r_i <= c_i, 1.0, 0.0), dcs)
        ddt = ddt + da * a_neg
        acc2_ref[1:2, :] += jnp.sum(da * dt, 0, keepdims=True) * a_neg
        lane = lax.broadcasted_iota(jnp.int32, (tc, 128), 1)
        ddraw = jnp.where(lane < SSD_H, ddt * _sigmoid(raw), 0.0)
        acc2_ref[0:1, :] += jnp.sum(ddraw, 0, keepdims=True)
        acc2_ref[2:3, :] += dd_row
        ddt_ref[...] = ddraw
        dpre = dxbc_scr[...] * (sig * (1.0 + pre * (1.0 - sig)))
        acc1_ref[4:5, :] += jnp.sum(dpre, 0, keepdims=True)
        for k in range(SSD_K):
            acc1_ref[k:k + 1, :] += jnp.sum(dpre * _shift_down(x, xprev, SSD_K - 1 - k), 0, keepdims=True)
        dnext = dnext_scr[...]
        dxbc_ref[...] = (cp_ref[3:4, :] * dpre + cp_ref[2:3, :] * _shift_up(dpre, dnext, 1)
                         + cp_ref[1:2, :] * _shift_up(dpre, dnext, 2)
                         + cp_ref[0:1, :] * _shift_up(dpre, dnext, 3)).astype(bf16)
        dnext_scr[...] = dpre

    rev = lambda w_, col: pl.BlockSpec((tc, w_), lambda i: (nc - 1 - i, col // w_))
    return pl.pallas_call(
        body, name="ssd_bwd", grid=(nc,),
        in_specs=[rev(1024, 0), pl.BlockSpec((tc, 1024), lambda i: (jnp.maximum(nc - 2 - i, 0), 0)),
                  rev(SSD_W, P_Z), rev(128, P_SM),
                  _const_spec((8, 1024)), _const_spec((8, 128)), _const_spec((8, SSD_W)),
                  pl.BlockSpec((SSD_N, SSD_W), lambda i: (nc - 1 - i, 0)), rev(SSD_W, 0), rev(SSD_W, 0)],
        out_specs=[rev(1024, 0), rev(SSD_W, 0), rev(128, 0),
                   pl.BlockSpec((8, 1024), lambda i: (0, 0)), pl.BlockSpec((8, 128), lambda i: (0, 0))],
        out_shape=[jax.ShapeDtypeStruct((L, 1024), bf16), jax.ShapeDtypeStruct((L, SSD_W), bf16),
                   jax.ShapeDtypeStruct((L, 128), f32), jax.ShapeDtypeStruct((8, 1024), f32),
                   jax.ShapeDtypeStruct((8, 128), f32)],
        scratch_shapes=[pltpu.VMEM((SSD_N, SSD_W), f32), pltpu.VMEM((tc, 1024), f32), pltpu.VMEM((tc, 1024), f32)],
        compiler_params=_cp(("arbitrary",)))(proj, proj, proj, proj, cp, hp, ng, sin, ypre, dycat)


def _pack_cols(w):
    pad = jnp.zeros(w.shape[:-1] + (P_W - P_SM - SSD_H - FOX_H,), w.dtype)
    return jnp.concatenate([w[..., 512:1536], w[..., 0:512], w[..., 1544:2312], w[..., 2316:3084],
                            w[..., 1536:1544], w[..., 2312:2316], pad], axis=-1)


def _unpack_cols(g):
    return jnp.concatenate([g[..., 1024:1536], g[..., 0:1024], g[..., 3072:3080], g[..., 1536:2304],
                            g[..., 3080:3084], g[..., 2304:3072]], axis=-1)


def _rows8(*rows):
    width = max(r.shape[-1] for r in rows)
    out = [jnp.pad(r.astype(f32), (0, width - r.shape[-1])) for r in rows]
    out += [jnp.zeros((width,), f32)] * (8 - len(out))
    return jnp.stack(out)


def _local_step(x, tgt, p, weights_of, grads_done):
    gb_in = _rows8(p["ln_in_g"], p["ln_in_b"])
    cst = _fox_consts()
    x0 = _ln_in_fwd(x, gb_in)
    saved = []
    for l in range(DEPTH):
        mv = [_rows8(p["mod"][l, 3 * j], p["mod"][l, 3 * j + 1], p["mod"][l, 3 * j + 2], p["ln_g"][l, j], p["ln_b"][l, j])
              for j in range(3)]
        cp = _rows8(*[p["ssd_conv_w"][l, k] for k in range(SSD_K)], p["ssd_conv_b"][l])
        hp = _rows8(jnp.pad(p["ssd_dt_bias"][l], (0, 120)), jnp.pad(p["ssd_a_log"][l], (0, 120)),
                    jnp.pad(p["ssd_d"][l], (0, 120)), jnp.pad(p["fox_f_bias"][l], (SM_F, 128 - SM_F - FOX_H)))
        ng = _rows8(p["ssd_norm_g"][l])
        scw = _rows8(*[p["sconv_w"][l, k] for k in range(SC_K)])
        w = dict(weights_of(l, "ffn1", x0))
        x1 = _ffn_fwd(x0, mv[0], w["ffn1_w_in"], w["ffn1_w_out"])
        w.update(weights_of(l, "mix", x1))
        proj = _inproj_fwd(x1, mv[1], w["mix_w_in"])
        y_ssd, ypre, sin = _ssd_fwd(proj, cp, hp, ng)
        qa, ka, va, tot = _fox_prep(proj, hp, cst)
        o, o_acc, lse = _fox_attn_fwd(qa, ka, va, tot)
        y_sc = _sconv_fwd(proj, scw)
        ycat = jnp.concatenate([y_ssd, o, y_sc], axis=1)
        x2 = _outproj_fwd(x1, ycat, mv[1], w["mix_w_out"])
        w.update(weights_of(l, "ffn2", x2))
        x3 = _ffn_fwd(x2, mv[2], w["ffn2_w_in"], w["ffn2_w_out"])
        saved.append((x0, x1, x2, mv, cp, hp, ng, scw, proj, ypre, sin, qa, ka, va, tot, o_acc, lse, ycat, w))
        x0 = x3
    dx, loss_acc = _loss_head(x0, tgt)

    g = {k: [None] * DEPTH for k in (
        "mod", "ln_g", "ln_b", "ssd_conv_w", "ssd_conv_b", "ssd_dt_bias", "ssd_a_log", "ssd_d", "ssd_norm_g",
        "fox_f_bias", "sconv_w")}
    def behind(small, tok):
        return small if tok is None else small + tok[0, 0]

    token = None
    for l in reversed(range(DEPTH)):
        x0, x1, x2, mv, cp, hp, ng, scw, proj, ypre, sin, qa, ka, va, tot, o_acc, lse, ycat, w = saved[l]
        dx, ht, da, actt, dyb, a2 = _ffn_bwd(x2, dx, behind(mv[2], token), w["ffn2_w_in"], w["ffn2_w_out"])
        token = grads_done(l, "ffn2", {"ffn2_w_in": _matmul_tokens(ht, da, tn=FS, tk=DW_TK, name="dw_ffn_in"),
                                       "ffn2_w_out": _matmul_tokens(actt, dyb, tn=D, tk=DW_TK, name="dw_ffn_out")})
        mv1 = behind(mv[1], token)
        dxp, dyb, dycat, ycat_t, a1 = _outproj_bwd(x1, ycat, dx, mv1, w["mix_w_out"])
        gw_mix_out = _matmul_tokens(ycat_t, dyb, tn=D // 2, tk=DW_TK, name="dw_mix_out")
        dxbc, dz, ddt, acc1, acc2 = _ssd_bwd(proj, dycat, ypre, sin, cp, hp, ng)
        qb, doa, qbt, doat = _fox_bprep(qa, dycat, o_acc, lse, cst)
        dq, dkt, dvt = _fox_attn_bwd(ka, va, tot, qb, doa, qbt, doat)
        dq, dk, dv, dsm, accf = _fox_post(dq, dkt, dvt, proj, ddt, hp, cst)
        dsb, dsc, dsx, accs = _sconv_bwd(proj, dycat, scw)
        dproj = jnp.concatenate([dxbc, dz, dq, dk, dv, dsb, dsc, dsx, dsm], axis=1)
        dx, ht, a1b = _inproj_bwd(x1, dxp, dproj, mv1, w["mix_w_in"])
        token = grads_done(l, "mix", {"mix_w_in": _matmul_tokens(ht, dproj, tn=P_W // 5, tk=DW_TK, name="dw_mix_in"),
                                      "mix_w_out": gw_mix_out})
        dx, ht, da, actt, dyb, a0 = _ffn_bwd(x0, dx, behind(mv[0], token), w["ffn1_w_in"], w["ffn1_w_out"])
        token = grads_done(l, "ffn1", {"ffn1_w_in": _matmul_tokens(ht, da, tn=FS, tk=DW_TK, name="dw_ffn_in"),
                                       "ffn1_w_out": _matmul_tokens(actt, dyb, tn=D, tk=DW_TK, name="dw_ffn_out")})
        g["mod"][l] = jnp.concatenate([a0[0:3], a1b[0:2], a1[2:3], a2[0:3]], axis=0)
        g["ln_g"][l] = jnp.stack([a0[3], a1[3], a2[3]])
        g["ln_b"][l] = jnp.stack([a0[4], a1[4], a2[4]])
        g["ssd_conv_w"][l] = acc1[0:SSD_K]
        g["ssd_conv_b"][l] = acc1[4]
        g["ssd_norm_g"][l] = acc1[5, :SSD_W]
        g["ssd_dt_bias"][l] = acc2[0, :SSD_H]
        g["ssd_a_log"][l] = acc2[1, :SSD_H]
        g["ssd_d"][l] = acc2[2, :SSD_H]
        g["fox_f_bias"][l] = accf[3, SM_F:SM_F + FOX_H]
        g["sconv_w"][l] = accs[0:SC_K]
    grad_x, a_in = _ln_in_bwd(x, dx, behind(gb_in, token))
    g = {k: jnp.stack(v) for k, v in g.items()}
    g["ln_in_g"], g["ln_in_b"] = a_in[0], a_in[1]
    return loss_acc[0, 0], grad_x, g


MESH = pl.DeviceIdType.MESH
ANY = pl.BlockSpec(memory_space=pl.ANY)


def _all_gather(shards, *, in_vmem, name):
    n_arr = len(shards)

    def body(*refs):
        x_refs, out_refs = refs[:n_arr], refs[n_arr:2 * n_arr]
        send_sems, recv_sems, local_sems = refs[2 * n_arr:]
        x, y, c = lax.axis_index("x"), lax.axis_index("y"), lax.axis_index("c")
        me, sibling = (x, y, c), (x, y, 1 - c)
        chips = [(1 - x, y), (x, 1 - y), (1 - x, 1 - y)]

        def copy(a, k, block, to, src=None):
            px, py, pc = block
            slot = out_refs[a].at[4 * px + 2 * py + pc]
            return pltpu.make_async_remote_copy(
                src_ref=slot if src is None else src, dst_ref=slot,
                send_sem=send_sems.at[7 * a + k], recv_sem=recv_sems.at[7 * a + k], device_id=to, device_id_type=MESH)

        mine, first, passed = [], [], []
        for a in range(n_arr):
            mine.append(pltpu.make_async_copy(x_refs[a], out_refs[a].at[4 * x + 2 * y + c], local_sems.at[a]))
            mine[-1].start()
            first.append(copy(a, 0, me, sibling, src=x_refs[a]))
            first += [copy(a, 1 + j, me, (*chip, c), src=x_refs[a]) for j, chip in enumerate(chips)]
        for cp in first:
            cp.start()
        for j, chip in enumerate(chips):
            for a in range(n_arr):
                copy(a, 1 + j, (*chip, c), me).wait_recv()
                passed.append(copy(a, 4 + j, (*chip, c), sibling))
                passed[-1].start()
        for a in range(n_arr):
            copy(a, 0, sibling, me).wait_recv()
            for j, chip in enumerate(chips):
                copy(a, 4 + j, (*chip, 1 - c), me).wait_recv()
        for cp in first + passed:
            cp.wait_send()
        for cp in mine:
            cp.wait()

    spec = pl.BlockSpec(memory_space=pltpu.VMEM) if in_vmem else ANY
    return pl.pallas_call(
        body, name=name, out_shape=[jax.ShapeDtypeStruct((N_DEV,) + s.shape, s.dtype) for s in shards],
        in_specs=[spec] * n_arr, out_specs=[spec] * n_arr,
        scratch_shapes=[pltpu.SemaphoreType.DMA((7 * n_arr,)), pltpu.SemaphoreType.DMA((7 * n_arr,)),
                        pltpu.SemaphoreType.DMA((n_arr,))],
    )(*shards)


def _swap_sibling(sends):
    n_arr = len(sends)

    def body(*refs):
        s_refs, o_refs = refs[:n_arr], refs[n_arr:2 * n_arr]
        send_sems, recv_sems = refs[2 * n_arr:]
        x, y, c = lax.axis_index("x"), lax.axis_index("y"), lax.axis_index("c")
        cps = [pltpu.make_async_remote_copy(
            src_ref=s_refs[a].at[:, 1 - c], dst_ref=o_refs[a], send_sem=send_sems.at[a], recv_sem=recv_sems.at[a],
            device_id=(x, y, 1 - c), device_id_type=MESH) for a in range(n_arr)]
        for cp in cps:
            cp.start()
        for cp in cps:
            cp.wait_recv()
        for cp in cps:
            cp.wait_send()

    return pl.pallas_call(
        body, name="swap_sibling",
        out_shape=[jax.ShapeDtypeStruct((s.shape[0],) + s.shape[2:], s.dtype) for s in sends],
        in_specs=[ANY] * n_arr, out_specs=[ANY] * n_arr,
        scratch_shapes=[pltpu.SemaphoreType.DMA((n_arr,)), pltpu.SemaphoreType.DMA((n_arr,))])(*sends)


def _exchange_chips(bufs):
    n_arr = len(bufs)

    def body(*refs):
        b_refs, o_refs = refs[:n_arr], refs[n_arr:2 * n_arr]
        send_sems, recv_sems, local_sems = refs[2 * n_arr:]
        x, y, c = lax.axis_index("x"), lax.axis_index("y"), lax.axis_index("c")
        mine = 2 * x + y
        peers = [(x, 1 - y), (1 - x, y), (1 - x, 1 - y)]

        def copy(a, k, src_slot, dst_slot):
            px, py = peers[k]
            return pltpu.make_async_remote_copy(
                src_ref=b_refs[a].at[src_slot], dst_ref=o_refs[a].at[dst_slot],
                send_sem=send_sems.at[3 * a + k], recv_sem=recv_sems.at[3 * a + k],
                device_id=(px, py, c), device_id_type=MESH)

        own = [pltpu.make_async_copy(b_refs[a].at[mine], o_refs[a].at[mine], local_sems.at[a]) for a in range(n_arr)]
        sent = [copy(a, k, 2 * px + py, mine) for a in range(n_arr) for k, (px, py) in enumerate(peers)]
        for cp in own + sent:
            cp.start()
        for a in range(n_arr):
            for k, (px, py) in enumerate(peers):
                copy(a, k, mine, 2 * px + py).wait_recv()
        for cp in sent:
            cp.wait_send()
        for cp in own:
            cp.wait()

    return pl.pallas_call(
        body, name="exchange_chips", out_shape=[jax.ShapeDtypeStruct(b.shape, b.dtype) for b in bufs],
        in_specs=[ANY] * n_arr, out_specs=[ANY] * n_arr,
        scratch_shapes=[pltpu.SemaphoreType.DMA((3 * n_arr,)), pltpu.SemaphoreType.DMA((3 * n_arr,)),
                        pltpu.SemaphoreType.DMA((n_arr,))])(*bufs)


HBM = pl.BlockSpec(memory_space=pltpu.HBM)
SEM = pl.BlockSpec(memory_space=pltpu.SEMAPHORE)
EFFECT = pltpu.SideEffectType.DATAFLOW_SIDE_EFFECTING


def _spread_copies(srcs, lands, send_sems, recv_sems, local_sems, scatter):
    x, y, c = lax.axis_index("x"), lax.axis_index("y"), lax.axis_index("c")
    me = 4 * x + 2 * y + c
    local, remote = [], []
    for a in range(len(srcs)):
        own = srcs[a].at[me] if scatter else srcs[a]
        local.append(pltpu.make_async_copy(own, lands[a].at[me], local_sems.at[a]))
        for r in range(1, N_DEV):
            px, py, pc = (1 - x if r & 4 else x), (1 - y if r & 2 else y), (1 - c if r & 1 else c)
            peer = 4 * px + 2 * py + pc
            k = (N_DEV - 1) * a + r - 1
            mk = functools.partial(pltpu.make_async_remote_copy, send_sem=send_sems.at[k], recv_sem=recv_sems.at[k],
                                   device_id=(px, py, pc), device_id_type=MESH)
            remote.append((mk(src_ref=srcs[a].at[peer] if scatter else srcs[a], dst_ref=lands[a].at[me]),
                           mk(src_ref=own, dst_ref=lands[a].at[peer])))
    return local, remote


def _spread_start(srcs, *, scatter, name, after=()):
    n, k = len(srcs), len(after)
    lands = [jax.ShapeDtypeStruct((N_DEV,) + s.shape[-2:], s.dtype) for s in srcs]

    def body(*refs):
        src, land = refs[:n], refs[n:2 * n]
        send_sems, recv_sems, local_sems = refs[2 * n + k:2 * n + k + 3]
        token = refs[-1]
        local, remote = _spread_copies(src, land, send_sems, recv_sems, local_sems, scatter)
        for cp in local:
            cp.start()
        for cp, _ in remote:
            cp.start()
        token[...] = jnp.zeros_like(token)

    nsem = (N_DEV - 1) * n
    out = pl.pallas_call(
        body, name=name,
        out_shape=(pltpu.SemaphoreType.DMA((nsem,)), pltpu.SemaphoreType.DMA((nsem,)), pltpu.SemaphoreType.DMA((n,)),
                   *[pltpu.HBM(s.shape, s.dtype) for s in srcs], *[pltpu.HBM(s.shape, s.dtype) for s in lands],
                   jax.ShapeDtypeStruct((8, 128), f32)),
        in_specs=[HBM] * (2 * n) + [ANY] * k,
        out_specs=(SEM, SEM, SEM, *[HBM] * (2 * n), pl.BlockSpec(memory_space=pltpu.VMEM)),
        input_output_aliases={i: 3 + i for i in range(2 * n)},
        compiler_params=pltpu.CompilerParams(has_side_effects=EFFECT),
    )(*[pltpu.with_memory_space_constraint(s, pltpu.HBM) for s in srcs],
      *[pltpu.with_memory_space_constraint(lax.empty(s.shape, s.dtype), pltpu.HBM) for s in lands], *after)
    return out[:-1], out[-1]


def _spread_wait(state, after, *, scatter, name):
    n = (len(state) - 3) // 2
    sems, thru = state[:3], state[3:]

    def body(*refs):
        src, land = refs[:n], refs[n:2 * n]
        send_sems, recv_sems, local_sems = refs[2 * n:2 * n + 3]
        local, remote = _spread_copies(src, land, send_sems, recv_sems, local_sems, scatter)
        for sent, received in remote:
            sent.wait_send()
            received.wait_recv()
        for cp in local:
            cp.wait()

    out = pl.pallas_call(
        body, name=name,
        out_shape=tuple(pltpu.HBM(t.shape, t.dtype) for t in thru),
        in_specs=[HBM] * (2 * n) + [SEM] * 3 + [ANY], out_specs=tuple([HBM] * (2 * n)),
        input_output_aliases={i: i for i in range(2 * n)},
        compiler_params=pltpu.CompilerParams(has_side_effects=EFFECT),
    )(*thru, *sems, after)
    return list(out[n:])


def _row_tile(r, c):
    if r * c <= 512 * 1024:
        return r
    t = r
    while t * c > 512 * 1024 and t % 2 == 0 and (t // 2) % 16 == 0:
        t //= 2
    return t


def _add_own_half(send, recv, c):
    nb, _, r, n = send.shape
    tr = _row_tile(r, n)

    def body(c_ref, s_ref, r_ref, o_ref):
        o_ref[...] = (s_ref[...].astype(f32) + r_ref[...].astype(f32)).astype(o_ref.dtype)

    return pl.pallas_call(
        body, name="add_own_half",
        grid_spec=pltpu.PrefetchScalarGridSpec(
            num_scalar_prefetch=1, grid=(nb, r // tr),
            in_specs=[pl.BlockSpec((None, None, tr, n), lambda j, i, cr: (j, cr[0], i, 0)),
                      pl.BlockSpec((None, tr, n), lambda j, i, cr: (j, i, 0))],
            out_specs=pl.BlockSpec((None, tr, n), lambda j, i, cr: (j, i, 0))),
        out_shape=jax.ShapeDtypeStruct((nb, r, n), bf16),
        compiler_params=_cp(("parallel", "parallel")))(jnp.reshape(c, (1,)).astype(jnp.int32), send, recv)


def _sum_slots(buf, *, tr, name):
    nb, r, n = buf.shape

    def body(b_ref, o_ref):
        acc = b_ref[0].astype(f32)
        for k in range(1, nb):
            acc = acc + b_ref[k].astype(f32)
        o_ref[...] = acc

    return pl.pallas_call(
        body, name=name, grid=(r // tr,),
        in_specs=[pl.BlockSpec((nb, tr, n), lambda i: (0, i, 0))],
        out_specs=pl.BlockSpec((tr, n), lambda i: (i, 0)),
        out_shape=jax.ShapeDtypeStruct((r, n), f32), compiler_params=_cp(("parallel",)))(buf)


def _ada_fwd(c_all, ada_w, ada_b_cols):
    n = ada_w.shape[-1]

    def body(c_ref, w_ref, b_ref, o_ref):
        cv = c_ref[...]
        ca = (cv * _sigmoid(cv)).astype(bf16)
        o_ref[...] = _dot(ca, w_ref[...].astype(bf16)) + b_ref[...]

    return pl.pallas_call(
        body, name="ada_fwd", grid=(DEPTH,),
        in_specs=[_const_spec((N_DEV, D)), pl.BlockSpec((None, D, n), lambda l: (l, 0, 0)),
                  pl.BlockSpec((None, 1, n), lambda l: (l, 0, 0))],
        out_specs=pl.BlockSpec((None, N_DEV, n), lambda l: (l, 0, 0)),
        out_shape=jax.ShapeDtypeStruct((DEPTH, N_DEV, n), f32), compiler_params=_cp(("parallel",)))(c_all, ada_w, ada_b_cols)


def _ada_bwd(c_all, dmod_cols):
    n = dmod_cols.shape[-1]

    def body(c_ref, d_ref, o_ref):
        cv = c_ref[...]
        ca = (cv * _sigmoid(cv)).astype(bf16)
        o_ref[...] = _dot_tn(ca, d_ref[...].astype(bf16))

    return pl.pallas_call(
        body, name="ada_bwd", grid=(DEPTH,),
        in_specs=[_const_spec((N_DEV, D)), pl.BlockSpec((None, N_DEV, n), lambda l: (l, 0, 0))],
        out_specs=pl.BlockSpec((None, D, n), lambda l: (l, 0, 0)),
        out_shape=jax.ShapeDtypeStruct((DEPTH, D, n), f32), compiler_params=_cp(("parallel",)))(c_all, dmod_cols)


def _adamw(w, g, m, v, *, tr, name):
    r, n = w.shape

    def body(w_ref, g_ref, m_ref, v_ref, d_ref, mo_ref, vo_ref):
        g_ = g_ref[...]
        m_ = ADAM_B1 * m_ref[...] + (1.0 - ADAM_B1) * g_
        v_ = ADAM_B2 * v_ref[...] + (1.0 - ADAM_B2) * jnp.square(g_)
        m_hat = m_ / (1.0 - ADAM_B1 ** ADAM_STEP)
        v_hat = v_ / (1.0 - ADAM_B2 ** ADAM_STEP)
        d_ref[...] = -ADAM_LR * (m_hat / (jnp.sqrt(v_hat) + ADAM_EPS) + ADAM_WD * w_ref[...])
        mo_ref[...] = m_
        vo_ref[...] = v_

    blk = pl.BlockSpec((tr, n), lambda i: (i, 0))
    return pl.pallas_call(
        body, name=name, grid=(r // tr,), in_specs=[blk] * 4, out_specs=[blk] * 3,
        out_shape=[jax.ShapeDtypeStruct((r, n), f32)] * 3, compiler_params=_cp(("parallel",)))(w, g, m, v)


WEIGHTS = ["ln_in_g", "ln_in_b", "ada_w", "ada_b", "ffn1_w_in", "ffn1_w_out", "mix_w_in", "mix_w_out", "ssd_conv_w",
           "ssd_conv_b", "ssd_dt_bias", "ssd_a_log", "ssd_d", "ssd_norm_g", "fox_f_bias", "sconv_w", "ffn2_w_in",
           "ffn2_w_out", "ln_g", "ln_b"]
BIG = ["ffn1_w_in", "ffn1_w_out", "ffn2_w_in", "ffn2_w_out", "mix_w_in", "mix_w_out"]
GROUPS = {"ffn1": ["ffn1_w_in", "ffn1_w_out"], "mix": ["mix_w_in", "mix_w_out"], "ffn2": ["ffn2_w_in", "ffn2_w_out"]}
COL_SHARDED = ("ffn1_w_in", "ffn2_w_in")
SMALL_SHARDED = {"ssd_conv_w": 128, "sconv_w": 32, "ln_g": 128, "ln_b": 128}
ADAM_TR = {"ada_w": 256, "ffn1_w_in": 512, "ffn2_w_in": 512, "ffn1_w_out": 352, "ffn2_w_out": 352, "mix_w_in": 64,
           "mix_w_out": 256}


def _pad_rows(v, mult=128):
    v = v.reshape(-1)
    return jnp.pad(v, (0, (-v.shape[0]) % mult))


def _pack_rows(parts, row_mult=8):
    flat = [_pad_rows(p.astype(f32)) for p in parts]
    offs, o = [], 0
    for f in flat:
        offs.append(o)
        o += f.shape[0] // 128
    buf = jnp.concatenate(flat).reshape(-1, 128)
    return jnp.pad(buf, ((0, (-buf.shape[0]) % row_mult), (0, 0))), offs


def _take(buf, off, shape):
    n = 1
    for s in shape:
        n *= s
    rows = -(-n // 128)
    lead = buf.shape[:-2]
    flat = buf[..., off:off + rows, :].reshape(lead + (rows * 128,))
    return flat[..., :n].reshape(lead + tuple(shape))


def _reduce_scatter_layer(g_layer, ci):
    sends = [t.reshape((4, 2) + t.shape[-2:]) for t in g_layer]
    got = _swap_sibling(sends)
    chip_sums = [_add_own_half(s, r, ci) for s, r in zip(sends, got)]
    got = _exchange_chips(chip_sums)
    return [_sum_slots(t, tr=_row_tile(*t.shape[-2:]), name="sum_chips") for t in got]


def kernel(*args):
    names = (["x", "c"] + WEIGHTS + ["loss_target"] + ["m_" + n for n in WEIGHTS] + ["v_" + n for n in WEIGHTS])
    assert len(args) == len(names)
    a = dict(zip(names, args))
    xi, yi, ci = lax.axis_index("x"), lax.axis_index("y"), lax.axis_index("c")
    me = 4 * xi + 2 * yi + ci

    small_in = [a["c"], a["ln_g"], a["ln_b"], a["ssd_conv_w"], a["sconv_w"]]
    buf, offs = _pack_rows(small_in)
    got, = _all_gather([buf], in_vmem=True, name="gather_small")
    c_all = _take(got, offs[0], (D,))
    full = {}
    for k, n in enumerate(["ln_g", "ln_b", "ssd_conv_w", "sconv_w"]):
        sh = a[n].shape
        t = _take(got, offs[k + 1], sh)
        full[n] = jnp.transpose(t, (1, 2, 0, 3)).reshape(sh[0], sh[1], N_DEV * sh[2])

    ncol = a["ada_w"].shape[-1]
    ada_b_cols = lax.dynamic_slice_in_dim(a["ada_b"], me * ncol, ncol, axis=1)[:, None, :]
    mod_cols = _ada_fwd(c_all, a["ada_w"], ada_b_cols)
    got, = _all_gather([mod_cols.reshape(DEPTH * N_DEV, ncol)], in_vmem=True, name="gather_mod")
    got = got.reshape(N_DEV, DEPTH, N_DEV, ncol)
    mod = lax.dynamic_index_in_dim(got, me, axis=2, keepdims=False)
    mod = jnp.transpose(mod, (1, 0, 2)).reshape(DEPTH, 9, D)

    shards = {n: a[n] for n in BIG}
    shards["mix_w_in"] = _pack_cols(a["mix_w_in"])
    def as_weights(names, got):
        return {n: (t if n in COL_SHARDED else t.reshape(-1, t.shape[-1])) for n, t in zip(names, got)}

    def group_shards(l, gn):
        return [shards[n][l].astype(bf16) for n in GROUPS[gn]]

    w_first = _all_gather(group_shards(0, "ffn1"), in_vmem=False, name="gather_weights")
    w_state, after = {}, [w_first[0]]
    for l, gn in [(0, "mix"), (0, "ffn2"), (1, "ffn1"), (1, "mix"), (1, "ffn2")]:
        w_state[l, gn], token = _spread_start(group_shards(l, gn), scatter=False, after=after,
                                              name=f"weights_{l}{gn}_start")
        after = [token]

    def weights_of(l, gn, after):
        if (l, gn) == (0, "ffn1"):
            return as_weights(GROUPS[gn], w_first)
        return as_weights(GROUPS[gn], _spread_wait(w_state[l, gn], after, scatter=False, name=f"weights_{l}{gn}_wait"))

    g_state = {}

    def grads_done(l, gn, gw):
        srcs = [gw[n].reshape((N_DEV,) + shards[n].shape[1:]) for n in GROUPS[gn]]
        g_state[l, gn], token = _spread_start(srcs, scatter=True, name=f"grads_{l}{gn}_start")
        return token

    p = {n: a[n] for n in ("ln_in_g", "ln_in_b", "ssd_conv_b", "ssd_dt_bias", "ssd_a_log", "ssd_d", "ssd_norm_g",
                           "fox_f_bias")}
    p.update(full)
    p["mod"] = mod + after[0][0, 0]

    loss_local, grad_x, g = _local_step(a["x"][0], a["loss_target"][0], p, weights_of, grads_done)
    loss = lax.psum(loss_local, ("x", "y", "c"))

    small_names = ["mod", "ln_in_g", "ln_in_b", "ssd_conv_b", "ssd_dt_bias", "ssd_a_log", "ssd_d", "ssd_norm_g",
                   "fox_f_bias", "ln_g", "ln_b", "ssd_conv_w", "sconv_w"]
    buf, offs = _pack_rows([g[n] for n in small_names])
    got, = _all_gather([buf], in_vmem=True, name="gather_small_grads")
    tot = _sum_slots(got, tr=buf.shape[0], name="sum_small_grads")
    grads = {}
    for k, n in enumerate(small_names[1:], start=1):
        t = _take(tot, offs[k], g[n].shape)
        if n in SMALL_SHARDED:
            w_ = SMALL_SHARDED[n]
            t = lax.dynamic_slice_in_dim(t, me * w_, w_, axis=2)
        grads[n] = t
    grads["ada_b"] = _take(tot, offs[0], (DEPTH, 9 * D))
    dmod_all = _take(got, offs[0], (DEPTH, 9 * D))
    dmod_cols = jnp.transpose(lax.dynamic_slice_in_dim(dmod_all, me * ncol, ncol, axis=2), (1, 0, 2))
    grads["ada_w"] = _ada_bwd(c_all, dmod_cols)

    def landed(l, gn, after):
        got = _spread_wait(g_state[l, gn], after, scatter=True, name=f"grads_{l}{gn}_wait")
        return [_sum_slots(t, tr=_row_tile(t.shape[-2], N_DEV // 2 * t.shape[-1]), name="sum_devices") for t in got]

    delta, new_m, new_v = {}, {}, {}

    def adamw(n):
        sh = a[n].shape
        two = lambda t: t.reshape(-1, sh[-1])
        outs = _adamw(two(a[n]), two(grads[n]), two(a["m_" + n]), two(a["v_" + n]), tr=ADAM_TR[n], name="adamw_" + n)
        delta[n], new_m[n], new_v[n] = (t.reshape(sh) for t in outs)

    small_params = [n for n in WEIGHTS if n not in ADAM_TR]
    packs = [_pack_rows([src[pre + n] for n in small_params])[0]
             for src, pre in ((a, ""), (grads, ""), (a, "m_"), (a, "v_"))]
    _, offs = _pack_rows([a[n] for n in small_params])
    outs = _adamw(*packs, tr=packs[0].shape[0], name="adamw_small")
    for k, n in enumerate(small_params):
        delta[n], new_m[n], new_v[n] = (_take(t, offs[k], a[n].shape) for t in outs)
    adamw("ada_w")
    for gn in ("ffn2", "mix"):
        per_layer = [landed(l, gn, grad_x) for l in range(DEPTH)]
        for k, n in enumerate(GROUPS[gn]):
            grads[n] = jnp.stack([per_layer[0][k], per_layer[1][k]])
            if n == "mix_w_in":
                grads[n] = _unpack_cols(grads[n])
            adamw(n)
    per_layer = [landed(0, "ffn1", delta["mix_w_out"]), landed(1, "ffn1", grad_x)]
    for k, n in enumerate(GROUPS["ffn1"]):
        grads[n] = jnp.stack([per_layer[0][k], per_layer[1][k]])
        adamw(n)

    return (loss, grad_x[None], *[grads[n] for n in WEIGHTS], *[delta[n] for n in WEIGHTS],
            *[new_m[n] for n in WEIGHTS], *[new_v[n] for n in WEIGHTS])
```

```python
import functools

import jax
import jax.numpy as jnp
from jax import lax
from jax.experimental import pallas as pl
from jax.experimental.pallas import tpu as pltpu

f32, bf16 = jnp.float32, jnp.bfloat16

D = 1024
F = 2816
DEPTH = 2
N_DEV = 8
SSD_W, SSD_HD, SSD_H, SSD_G, SSD_N, SSD_K = 512, 64, 8, 2, 128, 4
FOX_W, FOX_HD, FOX_H = 256, 64, 4
SC_W, SC_K = 256, 3
ALPHA = (2 * DEPTH) ** 0.25
LN_EPS = 1e-5
RMS_EPS = 1e-5
P_XBC, P_Z, P_Q, P_K, P_V, P_SB, P_SC, P_SX, P_SM = 0, 1024, 1536, 1792, 2048, 2304, 2560, 2816, 3072
P_W = 3200
SM_DT, SM_F = 0, 8
ADAM_LR, ADAM_B1, ADAM_B2, ADAM_EPS, ADAM_WD, ADAM_STEP = 0.001, 0.9, 0.999, 1e-08, 0.01, 10

VMEM_LIMIT = 56 * 1024 * 1024


def _cp(sem=None):
    return pltpu.CompilerParams(dimension_semantics=sem, vmem_limit_bytes=VMEM_LIMIT)


def _const_spec(shape):
    nd = len(shape)
    return pl.BlockSpec(shape, lambda *_: (0,) * nd, pipeline_mode=pl.Buffered(1))


def _sigmoid(x):
    return 1.0 / (1.0 + jnp.exp(-x))


def _ln_fwd(u, g, b):
    mu = jnp.mean(u, -1, keepdims=True)
    xc = u - mu
    rstd = lax.rsqrt(jnp.mean(xc * xc, -1, keepdims=True) + LN_EPS)
    xhat = xc * rstd
    return xhat * g + b, xhat, rstd


def _ln_bwd(dout, xhat, rstd, g):
    dxh = dout * g
    m1 = jnp.mean(dxh, -1, keepdims=True)
    m2 = jnp.mean(dxh * xhat, -1, keepdims=True)
    du = rstd * (dxh - m1 - xhat * m2)
    return du, jnp.sum(dout * xhat, 0, keepdims=True), jnp.sum(dout, 0, keepdims=True)


def _dot(a, b):
    return jnp.dot(a, b, preferred_element_type=f32)


def _dot_nt(a, b):
    return lax.dot_general(a, b, (((1,), (1,)), ((), ())), preferred_element_type=f32)


def _dot_tn(a, b):
    return lax.dot_general(a, b, (((0,), (0,)), ((), ())), preferred_element_type=f32)


def _dot_hi(a, b):
    return jnp.dot(a, b, preferred_element_type=f32, precision=lax.Precision.HIGHEST)


def _shift_down(cur, prev, s):
    if s == 0:
        return cur
    row = lax.broadcasted_iota(jnp.int32, cur.shape, 0)
    return jnp.where(row < s, pltpu.roll(prev, s, 0), pltpu.roll(cur, s, 0))


def _shift_up(cur, nxt, s):
    if s == 0:
        return cur
    t = cur.shape[0]
    row = lax.broadcasted_iota(jnp.int32, cur.shape, 0)
    return jnp.where(row >= t - s, pltpu.roll(nxt, t - s, 0), pltpu.roll(cur, t - s, 0))


def _ln_in_fwd(x, gb, *, tt=512):
    L = x.shape[0]

    def body(x_ref, gb_ref, o_ref):
        o_ref[...] = _ln_fwd(x_ref[...], gb_ref[0:1, :], gb_ref[1:2, :])[0]

    return pl.pallas_call(
        body, name="ln_in_fwd", grid=(L // tt,),
        in_specs=[pl.BlockSpec((tt, D), lambda i: (i, 0)), _const_spec((8, D))],
        out_specs=pl.BlockSpec((tt, D), lambda i: (i, 0)),
        out_shape=jax.ShapeDtypeStruct((L, D), f32), compiler_params=_cp(("parallel",)))(x, gb)


def _ln_in_bwd(x, dy, gb, *, tt=512):
    L = x.shape[0]

    def body(x_ref, dy_ref, gb_ref, dx_ref, acc_ref):
        @pl.when(pl.program_id(0) == 0)
        def _():
            acc_ref[...] = jnp.zeros_like(acc_ref)
        _, xhat, rstd = _ln_fwd(x_ref[...], gb_ref[0:1, :], gb_ref[1:2, :])
        du, dg, db = _ln_bwd(dy_ref[...], xhat, rstd, gb_ref[0:1, :])
        dx_ref[...] = du
        acc_ref[0:1, :] += dg
        acc_ref[1:2, :] += db

    return pl.pallas_call(
        body, name="ln_in_bwd", grid=(L // tt,),
        in_specs=[pl.BlockSpec((tt, D), lambda i: (i, 0)), pl.BlockSpec((tt, D), lambda i: (i, 0)), _const_spec((8, D))],
        out_specs=[pl.BlockSpec((tt, D), lambda i: (i, 0)), pl.BlockSpec((8, D), lambda i: (0, 0))],
        out_shape=[jax.ShapeDtypeStruct((L, D), f32), jax.ShapeDtypeStruct((8, D), f32)],
        compiler_params=_cp(("arbitrary",)))(x, dy, gb)


def _loss_head(y, tgt, *, tt=512):
    L = y.shape[0]

    def body(y_ref, t_ref, dy_ref, acc_ref):
        @pl.when(pl.program_id(0) == 0)
        def _():
            acc_ref[...] = jnp.zeros_like(acc_ref)
        e = y_ref[...] - t_ref[...]
        dy_ref[...] = e * (1.0 / D)
        acc_ref[...] += 0.5 * jnp.sum(jnp.mean(e * e, -1, keepdims=True))

    return pl.pallas_call(
        body, name="loss_head", grid=(L // tt,),
        in_specs=[pl.BlockSpec((tt, D), lambda i: (i, 0)), pl.BlockSpec((tt, D), lambda i: (i, 0))],
        out_specs=[pl.BlockSpec((tt, D), lambda i: (i, 0)), pl.BlockSpec((8, 128), lambda i: (0, 0))],
        out_shape=[jax.ShapeDtypeStruct((L, D), f32), jax.ShapeDtypeStruct((8, 128), f32)],
        compiler_params=_cp(("arbitrary",)))(y, tgt)


FFN_CH = 4
FS = F // FFN_CH


def _ffn_fwd(x, mv, w_in, w_out, *, tt=256):
    L = x.shape[0]

    def body(x_ref, mv_ref, wi_ref, wo_ref, o_ref):
        x = x_ref[...]
        h = (x * (1.0 + mv_ref[1:2, :]) + mv_ref[0:1, :]).astype(bf16)
        y = jnp.zeros((tt, D), f32)
        for c in range(FFN_CH):
            g = _dot(h, wi_ref[c])
            u = _dot(h, wi_ref[c + FFN_CH])
            act = (g * _sigmoid(g) * u).astype(bf16)
            y = y + _dot(act, wo_ref[c * FS:(c + 1) * FS, :])
        uu = ALPHA * x + (0.5 * mv_ref[2:3, :]) * y
        o_ref[...] = _ln_fwd(uu, mv_ref[3:4, :], mv_ref[4:5, :])[0]

    return pl.pallas_call(
        body, name="ffn_fwd", grid=(L // tt,),
        in_specs=[pl.BlockSpec((tt, D), lambda i: (i, 0)), _const_spec((8, D)),
                  _const_spec((2 * FFN_CH, D, FS)), _const_spec((F, D))],
        out_specs=pl.BlockSpec((tt, D), lambda i: (i, 0)),
        out_shape=jax.ShapeDtypeStruct((L, D), f32), compiler_params=_cp(("parallel",)))(x, mv, w_in, w_out)


def _ffn_bwd(x, dxo, mv, w_in, w_out, *, tt=256):
    L = x.shape[0]

    def body(x_ref, dxo_ref, mv_ref, wi_ref, wo_ref, dx_ref, h_ref, da_ref, act_ref, dy_ref, acc_ref, a_scr):
        @pl.when(pl.program_id(0) == 0)
        def _():
            acc_ref[...] = jnp.zeros_like(acc_ref)
        x = x_ref[...]
        scale1 = 1.0 + mv_ref[1:2, :]
        h = (x * scale1 + mv_ref[0:1, :]).astype(bf16)
        h_ref[...] = h.T
        y = jnp.zeros((tt, D), f32)
        for c in range(FFN_CH):
            g = _dot(h, wi_ref[c])
            u = _dot(h, wi_ref[c + FFN_CH])
            a_scr[c] = g
            a_scr[c + FFN_CH] = u
            act = (g * _sigmoid(g) * u).astype(bf16)
            act_ref[c] = act.T
            y = y + _dot(act, wo_ref[c * FS:(c + 1) * FS, :])
        hg = 0.5 * mv_ref[2:3, :]
        _, xhat, rstd = _ln_fwd(ALPHA * x + hg * y, mv_ref[3:4, :], mv_ref[4:5, :])
        du, dlg, dlb = _ln_bwd(dxo_ref[...], xhat, rstd, mv_ref[3:4, :])
        acc_ref[3:4, :] += dlg
        acc_ref[4:5, :] += dlb
        acc_ref[2:3, :] += jnp.sum(0.5 * y * du, 0, keepdims=True)
        dyb = (hg * du).astype(bf16)
        dy_ref[...] = dyb
        dh = jnp.zeros((tt, D), f32)
        for c in range(FFN_CH):
            g = a_scr[c]
            u = a_scr[c + FFN_CH]
            dact = _dot_nt(dyb, wo_ref[c * FS:(c + 1) * FS, :])
            s = _sigmoid(g)
            dg = (dact * u * (s * (1.0 + g * (1.0 - s)))).astype(bf16)
            dup = (dact * (g * s)).astype(bf16)
            da_ref[c] = dg
            da_ref[c + FFN_CH] = dup
            dh = dh + _dot_nt(dg, wi_ref[c])
            dh = dh + _dot_nt(dup, wi_ref[c + FFN_CH])
        dx_ref[...] = ALPHA * du + dh * scale1
        acc_ref[0:1, :] += jnp.sum(dh, 0, keepdims=True)
        acc_ref[1:2, :] += jnp.sum(dh * x, 0, keepdims=True)

    tok = lambda w: pl.BlockSpec((tt, w), lambda i: (i, 0))
    by_chunk = lambda n: pl.BlockSpec((n, tt, FS), lambda i: (0, i, 0))
    return pl.pallas_call(
        body, name="ffn_bwd", grid=(L // tt,),
        in_specs=[tok(D), tok(D), _const_spec((8, D)), _const_spec((2 * FFN_CH, D, FS)), _const_spec((F, D))],
        out_specs=[tok(D), pl.BlockSpec((D, tt), lambda i: (0, i)), by_chunk(2 * FFN_CH),
                   pl.BlockSpec((FFN_CH, FS, tt), lambda i: (0, 0, i)), tok(D), pl.BlockSpec((8, D), lambda i: (0, 0))],
        out_shape=[jax.ShapeDtypeStruct((L, D), f32), jax.ShapeDtypeStruct((D, L), bf16),
                   jax.ShapeDtypeStruct((2 * FFN_CH, L, FS), bf16), jax.ShapeDtypeStruct((FFN_CH, FS, L), bf16),
                   jax.ShapeDtypeStruct((L, D), bf16), jax.ShapeDtypeStruct((8, D), f32)],
        scratch_shapes=[pltpu.VMEM((2 * FFN_CH, tt, FS), f32)],
        compiler_params=_cp(("arbitrary",)))(x, dxo, mv, w_in, w_out)


DW_TK = 4096


def _matmul_tokens(a, b, *, tn, tk, name, after=None):
    ga, gb = a.ndim == 3, b.ndim == 3
    extra = [] if after is None else [after]
    G = a.shape[0] if ga else (b.shape[0] if gb else 1)
    M, K = a.shape[-2:]
    N = b.shape[-1]
    tk = min(tk, K)
    nk = K // tk

    def body(a_ref, b_ref, *rest):
        o_ref, acc = rest[len(extra):]
        k = pl.program_id(2)
        p = _dot(a_ref[...], b_ref[...])

        @pl.when(k == 0)
        def _():
            acc[...] = p

        @pl.when(k > 0)
        def _():
            acc[...] += p

        @pl.when(k == nk - 1)
        def _():
            o_ref[...] = acc[...].astype(bf16)

    a_spec = (pl.BlockSpec((None, M, tk), lambda g, j, k: (g, 0, k)) if ga
              else pl.BlockSpec((M, tk), lambda g, j, k: (0, k)))
    b_spec = (pl.BlockSpec((None, tk, tn), lambda g, j, k: (g, k, j)) if gb
              else pl.BlockSpec((tk, tn), lambda g, j, k: (k, j)))
    if ga or gb:
        o_spec, o_shape = pl.BlockSpec((None, M, tn), lambda g, j, k: (g, 0, j)), (G, M, N)
    else:
        o_spec, o_shape = pl.BlockSpec((M, tn), lambda g, j, k: (0, j)), (M, N)
    return pl.pallas_call(
        body, name=name, grid=(G, N // tn, nk), in_specs=[a_spec, b_spec] + [ANY] * len(extra), out_specs=o_spec,
        out_shape=jax.ShapeDtypeStruct(o_shape, bf16), scratch_shapes=[pltpu.VMEM((M, tn), f32)],
        compiler_params=_cp(("parallel", "parallel", "arbitrary")))(a, b, *extra)


def _inproj_fwd(x, mv, w, *, tt=512):
    L = x.shape[0]

    def body(x_ref, mv_ref, w_ref, o_ref):
        h = (x_ref[...] * (1.0 + mv_ref[1:2, :]) + mv_ref[0:1, :]).astype(bf16)
        o_ref[...] = _dot(h, w_ref[...])

    return pl.pallas_call(
        body, name="inproj_fwd", grid=(L // tt,),
        in_specs=[pl.BlockSpec((tt, D), lambda i: (i, 0)), _const_spec((8, D)), _const_spec((D, P_W))],
        out_specs=pl.BlockSpec((tt, P_W), lambda i: (i, 0)),
        out_shape=jax.ShapeDtypeStruct((L, P_W), f32), compiler_params=_cp(("parallel",)))(x, mv, w)


def _inproj_bwd(x, dx_part, dproj, mv, w, *, tt=512):
    L = x.shape[0]

    def body(x_ref, dxp_ref, dp_ref, mv_ref, w_ref, dx_ref, h_ref, acc_ref):
        @pl.when(pl.program_id(0) == 0)
        def _():
            acc_ref[...] = jnp.zeros_like(acc_ref)
        x = x_ref[...]
        scale1 = 1.0 + mv_ref[1:2, :]
        h_ref[...] = (x * scale1 + mv_ref[0:1, :]).astype(bf16).T
        dh = _dot_nt(dp_ref[...], w_ref[...])
        dx_ref[...] = dxp_ref[...] + dh * scale1
        acc_ref[0:1, :] += jnp.sum(dh, 0, keepdims=True)
        acc_ref[1:2, :] += jnp.sum(dh * x, 0, keepdims=True)

    tok = lambda w_: pl.BlockSpec((tt, w_), lambda i: (i, 0))
    return pl.pallas_call(
        body, name="inproj_bwd", grid=(L // tt,),
        in_specs=[tok(D), tok(D), tok(P_W), _const_spec((8, D)), _const_spec((D, P_W))],
        out_specs=[tok(D), pl.BlockSpec((D, tt), lambda i: (0, i)), pl.BlockSpec((8, D), lambda i: (0, 0))],
        out_shape=[jax.ShapeDtypeStruct((L, D), f32), jax.ShapeDtypeStruct((D, L), bf16),
                   jax.ShapeDtypeStruct((8, D), f32)],
        compiler_params=_cp(("arbitrary",)))(x, dx_part, dproj, mv, w)


def _outproj_fwd(x, ycat, mv, w, *, tt=512):
    L = x.shape[0]

    def body(x_ref, y_ref, mv_ref, w_ref, o_ref):
        y = _dot(y_ref[...], w_ref[...])
        uu = ALPHA * x_ref[...] + mv_ref[2:3, :] * y
        o_ref[...] = _ln_fwd(uu, mv_ref[3:4, :], mv_ref[4:5, :])[0]

    tok = lambda w_: pl.BlockSpec((tt, w_), lambda i: (i, 0))
    return pl.pallas_call(
        body, name="outproj_fwd", grid=(L // tt,),
        in_specs=[tok(D), tok(D), _const_spec((8, D)), _const_spec((D, D))],
        out_specs=tok(D),
        out_shape=jax.ShapeDtypeStruct((L, D), f32), compiler_params=_cp(("parallel",)))(x, ycat, mv, w)


def _outproj_bwd(x, ycat, dxo, mv, w, *, tt=512):
    L = x.shape[0]

    def body(x_ref, y_ref, dxo_ref, mv_ref, w_ref, dx_ref, dy_ref, dyc_ref, yt_ref, acc_ref):
        @pl.when(pl.program_id(0) == 0)
        def _():
            acc_ref[...] = jnp.zeros_like(acc_ref)
        yt_ref[...] = y_ref[...].T
        y = _dot(y_ref[...], w_ref[...])
        gate = mv_ref[2:3, :]
        _, xhat, rstd = _ln_fwd(ALPHA * x_ref[...] + gate * y, mv_ref[3:4, :], mv_ref[4:5, :])
        du, dlg, dlb = _ln_bwd(dxo_ref[...], xhat, rstd, mv_ref[3:4, :])
        acc_ref[3:4, :] += dlg
        acc_ref[4:5, :] += dlb
        acc_ref[2:3, :] += jnp.sum(y * du, 0, keepdims=True)
        dx_ref[...] = ALPHA * du
        dyb = (gate * du).astype(bf16)
        dy_ref[...] = dyb
        dyc_ref[...] = _dot_nt(dyb, w_ref[...])

    tok = lambda w_: pl.BlockSpec((tt, w_), lambda i: (i, 0))
    return pl.pallas_call(
        body, name="outproj_bwd", grid=(L // tt,),
        in_specs=[tok(D), tok(D), tok(D), _const_spec((8, D)), _const_spec((D, D))],
        out_specs=[tok(D), tok(D), tok(D), pl.BlockSpec((D, tt), lambda i: (0, i)), pl.BlockSpec((8, D), lambda i: (0, 0))],
        out_shape=[jax.ShapeDtypeStruct((L, D), f32), jax.ShapeDtypeStruct((L, D), bf16),
                   jax.ShapeDtypeStruct((L, D), f32), jax.ShapeDtypeStruct((D, L), bf16),
                   jax.ShapeDtypeStruct((8, D), f32)],
        compiler_params=_cp(("arbitrary",)))(x, ycat, dxo, mv, w)


def _sconv_fwd(proj, w, *, tt=512):
    L = proj.shape[0]
    cb = SC_W

    def body(b_ref, c_ref, x_ref, cp_ref, xp_ref, w_ref, o_ref):
        first = jnp.where(pl.program_id(0) > 0, 1.0, 0.0)
        u = c_ref[...] * x_ref[...]
        up = cp_ref[...] * xp_ref[...] * first
        v = w_ref[2:3, :] * u + w_ref[1:2, :] * _shift_down(u, up, 1) + w_ref[0:1, :] * _shift_down(u, up, 2)
        o_ref[...] = (b_ref[...] * v).astype(bf16)

    cur = lambda col: pl.BlockSpec((tt, cb), lambda i: (i, col // cb))
    prev = lambda col: pl.BlockSpec((tt, cb), lambda i: (jnp.maximum(i - 1, 0), col // cb))
    return pl.pallas_call(
        body, name="sconv_fwd", grid=(L // tt,),
        in_specs=[cur(P_SB), cur(P_SC), cur(P_SX), prev(P_SC), prev(P_SX), _const_spec((8, cb))],
        out_specs=pl.BlockSpec((tt, cb), lambda i: (i, 0)),
        out_shape=jax.ShapeDtypeStruct((L, cb), bf16), compiler_params=_cp(("parallel",)))(proj, proj, proj, proj, proj, w)


def _sconv_bwd(proj, dycat, w, *, tt=512):
    L = proj.shape[0]
    cb = SC_W
    n = L // tt

    def body(b_ref, c_ref, x_ref, cp_ref, xp_ref, bn_ref, dy_ref, dyn_ref, w_ref, db_ref, dc_ref, dx_ref, acc_ref):
        i = pl.program_id(0)

        @pl.when(i == 0)
        def _():
            acc_ref[...] = jnp.zeros_like(acc_ref)
        first = jnp.where(i > 0, 1.0, 0.0)
        last = jnp.where(i < n - 1, 1.0, 0.0)
        cg, xin, bg = c_ref[...], x_ref[...], b_ref[...]
        u = cg * xin
        up = cp_ref[...] * xp_ref[...] * first
        u1, u2 = _shift_down(u, up, 1), _shift_down(u, up, 2)
        v = w_ref[2:3, :] * u + w_ref[1:2, :] * u1 + w_ref[0:1, :] * u2
        dy = dy_ref[...]
        db_ref[...] = (dy * v).astype(bf16)
        dv = dy * bg
        dvn = dyn_ref[...] * bn_ref[...] * last
        du = w_ref[2:3, :] * dv + w_ref[1:2, :] * _shift_up(dv, dvn, 1) + w_ref[0:1, :] * _shift_up(dv, dvn, 2)
        acc_ref[2:3, :] += jnp.sum(dv * u, 0, keepdims=True)
        acc_ref[1:2, :] += jnp.sum(dv * u1, 0, keepdims=True)
        acc_ref[0:1, :] += jnp.sum(dv * u2, 0, keepdims=True)
        dc_ref[...] = (du * xin).astype(bf16)
        dx_ref[...] = (du * cg).astype(bf16)

    cur = lambda col: pl.BlockSpec((tt, cb), lambda i: (i, col // cb))
    prev = lambda col: pl.BlockSpec((tt, cb), lambda i: (jnp.maximum(i - 1, 0), col // cb))
    nxt = lambda col: pl.BlockSpec((tt, cb), lambda i: (jnp.minimum(i + 1, n - 1), col // cb))
    ycol = SSD_W + FOX_W
    out = pl.BlockSpec((tt, cb), lambda i: (i, 0))
    return pl.pallas_call(
        body, name="sconv_bwd", grid=(n,),
        in_specs=[cur(P_SB), cur(P_SC), cur(P_SX), prev(P_SC), prev(P_SX), nxt(P_SB), cur(ycol), nxt(ycol),
                  _const_spec((8, cb))],
        out_specs=[out, out, out, pl.BlockSpec((8, cb), lambda i: (0, 0))],
        out_shape=[jax.ShapeDtypeStruct((L, cb), bf16)] * 3 + [jax.ShapeDtypeStruct((8, cb), f32)],
        compiler_params=_cp(("arbitrary",)))(proj, proj, proj, proj, proj, proj, dycat, dycat, w)


def _log1pexp(x):
    return jnp.log(1.0 + jnp.exp(-jnp.abs(x)))


def _fox_gate_fwd(proj, hp, *, tt=256):
    L = proj.shape[0]

    def body(sm_ref, hp_ref, cum_ref, cumt_ref, carry):
        @pl.when(pl.program_id(0) == 0)
        def _():
            carry[...] = jnp.zeros_like(carry)
        xx = sm_ref[...] + hp_ref[3:4, :]
        logf = jnp.minimum(xx, 0.0) - _log1pexp(xx)
        r = lax.broadcasted_iota(jnp.int32, (tt, tt), 0)
        c = lax.broadcasted_iota(jnp.int32, (tt, tt), 1)
        cum = _dot_hi(jnp.where(r >= c, 1.0, 0.0), logf) + carry[0:1, :]
        cum_ref[...] = cum
        cumt_ref[...] = cum.T
        carry[0:1, :] = cum[tt - 1:tt, :]

    return pl.pallas_call(
        body, name="fox_gate_fwd", grid=(L // tt,),
        in_specs=[pl.BlockSpec((tt, 128), lambda i: (i, P_SM // 128)), _const_spec((8, 128))],
        out_specs=[pl.BlockSpec((tt, 128), lambda i: (i, 0)), pl.BlockSpec((128, tt), lambda i: (0, i))],
        out_shape=[jax.ShapeDtypeStruct((L, 128), f32), jax.ShapeDtypeStruct((128, L), f32)],
        scratch_shapes=[pltpu.VMEM((8, 128), f32)],
        compiler_params=_cp(("arbitrary",)))(proj, hp)


def _fox_gate_bwd(dcumt, proj, ddt, hp, *, tt=256):
    L = proj.shape[0]
    n = L // tt

    def body(dct_ref, sm_ref, ddt_ref, hp_ref, dsm_ref, acc_ref, carry):
        @pl.when(pl.program_id(0) == 0)
        def _():
            carry[...] = jnp.zeros_like(carry)
            acc_ref[...] = jnp.zeros_like(acc_ref)
        dc = dct_ref[...].T
        r = lax.broadcasted_iota(jnp.int32, (tt, tt), 0)
        c = lax.broadcasted_iota(jnp.int32, (tt, tt), 1)
        dl = _dot_hi(jnp.where(r <= c, 1.0, 0.0), dc) + carry[0:1, :]
        carry[0:1, :] += jnp.sum(dc, 0, keepdims=True)
        xx = sm_ref[...] + hp_ref[3:4, :]
        lane = lax.broadcasted_iota(jnp.int32, (tt, 128), 1)
        dlogit = jnp.where((lane >= SM_F) & (lane < SM_F + FOX_H), dl * _sigmoid(-xx), 0.0)
        acc_ref[3:4, :] += jnp.sum(dlogit, 0, keepdims=True)
        dsm_ref[...] = (dlogit + ddt_ref[...]).astype(bf16)

    return pl.pallas_call(
        body, name="fox_gate_bwd", grid=(n,),
        in_specs=[pl.BlockSpec((128, tt), lambda i: (0, n - 1 - i)),
                  pl.BlockSpec((tt, 128), lambda i: (n - 1 - i, P_SM // 128)),
                  pl.BlockSpec((tt, 128), lambda i: (n - 1 - i, 0)), _const_spec((8, 128))],
        out_specs=[pl.BlockSpec((tt, 128), lambda i: (n - 1 - i, 0)), pl.BlockSpec((8, 128), lambda i: (0, 0))],
        out_shape=[jax.ShapeDtypeStruct((L, 128), bf16), jax.ShapeDtypeStruct((8, 128), f32)],
        scratch_shapes=[pltpu.VMEM((8, 128), f32)],
        compiler_params=_cp(("arbitrary",)))(dcumt, proj, ddt, hp)


NEG = -1e30
FOX_SCALE = FOX_HD ** -0.5


def _fox_fwd(proj, cum, cumt, *, tq=256):
    L = proj.shape[0]
    heads = [slice(h * FOX_HD, (h + 1) * FOX_HD) for h in range(FOX_H)]

    def body(q_ref, k_ref, v_ref, cq_ref, ct_ref, o_ref, oa_ref, lse_ref):
        i = pl.program_id(0)
        row = lax.broadcasted_iota(jnp.int32, (tq, tq), 0)
        col = lax.broadcasted_iota(jnp.int32, (tq, tq), 1)
        diag_bias = jnp.where(row >= col, 0.0, NEG)
        qs = [(q_ref[:, hs] * FOX_SCALE).astype(bf16) for hs in heads]
        cqs = [cq_ref[:, SM_F + h:SM_F + h + 1] for h in range(FOX_H)]

        def block(r0, carry, bias):
            out = []
            for h, hs in enumerate(heads):
                m, l, acc, acc_lo = carry[h]
                kj = k_ref[pl.ds(r0, tq), hs].astype(bf16)
                vj = v_ref[pl.ds(r0, tq), hs].astype(bf16)
                s = _dot_nt(qs[h], kj) + (cqs[h] - ct_ref[SM_F + h:SM_F + h + 1, pl.ds(r0, tq)])
                if bias is not None:
                    s = s + bias
                m_new = jnp.maximum(m, jnp.max(s, -1, keepdims=True))
                p = jnp.exp(s - m_new)
                al = jnp.exp(m - m_new)
                pb = p.astype(bf16)
                p_lo = (p - pb.astype(f32)).astype(bf16)
                out.append((m_new, al * l + jnp.sum(p, -1, keepdims=True),
                            al * acc + _dot(pb, vj), al * acc_lo + _dot(p_lo, vj)))
            return tuple(out)

        init = tuple((jnp.full((tq, 1), NEG, f32), jnp.zeros((tq, 1), f32), jnp.zeros((tq, FOX_HD), f32),
                      jnp.zeros((tq, FOX_HD), f32)) for _ in heads)
        carry = lax.fori_loop(0, i, lambda j, c: block(pl.multiple_of(j * tq, tq), c, None), init)
        carry = block(pl.multiple_of(i * tq, tq), carry, diag_bias)
        lane = lax.broadcasted_iota(jnp.int32, (tq, 128), 1)
        lse_all = jnp.zeros((tq, 128), f32)
        for h, hs in enumerate(heads):
            m, l, acc, acc_lo = carry[h]
            inv = 1.0 / l
            o_ref[:, hs] = acc * inv
            oa_ref[:, hs] = (acc + acc_lo) * inv
            lse_all = jnp.where(lane == h, m + jnp.log(l), lse_all)
        lse_ref[...] = lse_all

    return pl.pallas_call(
        body, name="fox_fwd", grid=(L // tq,),
        in_specs=[pl.BlockSpec((tq, FOX_W), lambda i: (i, P_Q // FOX_W)),
                  pl.BlockSpec((L, FOX_W), lambda i: (0, P_K // FOX_W), pipeline_mode=pl.Buffered(1)),
                  pl.BlockSpec((L, FOX_W), lambda i: (0, P_V // FOX_W), pipeline_mode=pl.Buffered(1)),
                  pl.BlockSpec((tq, 128), lambda i: (i, 0)), _const_spec((128, L))],
        out_specs=[pl.BlockSpec((tq, FOX_W), lambda i: (i, 0)), pl.BlockSpec((tq, FOX_W), lambda i: (i, 0)),
                   pl.BlockSpec((tq, 128), lambda i: (i, 0))],
        out_shape=[jax.ShapeDtypeStruct((L, FOX_W), f32), jax.ShapeDtypeStruct((L, FOX_W), f32),
                   jax.ShapeDtypeStruct((L, 128), f32)],
        compiler_params=_cp(("parallel",)))(proj, proj, proj, cum, cumt)


def _fox_bwd(proj, dycat, o_acc, lse, cum, cumt, *, tq=256):
    L = proj.shape[0]
    nq = L // tq
    heads = [slice(h * FOX_HD, (h + 1) * FOX_HD) for h in range(FOX_H)]

    def body(q_ref, k_ref, v_ref, do_ref, oa_ref, lse_ref, cum_ref, ct_ref, dq_ref, dk_ref, dv_ref, dct_ref):
        j = pl.program_id(0)

        @pl.when(j == 0)
        def _():
            dq_ref[...] = jnp.zeros_like(dq_ref)
        row = lax.broadcasted_iota(jnp.int32, (tq, tq), 0)
        col = lax.broadcasted_iota(jnp.int32, (tq, tq), 1)
        diag_bias = jnp.where(row >= col, 0.0, NEG)
        kjs = [k_ref[:, hs].astype(bf16) for hs in heads]
        vjs = [v_ref[:, hs].astype(bf16) for hs in heads]
        cks = [ct_ref[SM_F + h:SM_F + h + 1, :] for h in range(FOX_H)]

        def block(r0, carry, bias):
            out = []
            for h, hs in enumerate(heads):
                dk, dv, dc = carry[h]
                qi = (q_ref[pl.ds(r0, tq), hs] * FOX_SCALE).astype(bf16)
                dob = do_ref[pl.ds(r0, tq), hs].astype(bf16)
                delta = jnp.sum(dob.astype(f32) * oa_ref[pl.ds(r0, tq), hs], -1, keepdims=True)
                s = _dot_nt(qi, kjs[h]) + (cum_ref[pl.ds(r0, tq), SM_F + h:SM_F + h + 1] - cks[h])
                if bias is not None:
                    s = s + bias
                p = jnp.exp(s - lse_ref[pl.ds(r0, tq), h:h + 1])
                ds = p * (_dot_nt(dob, vjs[h]) - delta)
                dsb = ds.astype(bf16)
                dq_ref[pl.ds(r0, tq), hs] += _dot(dsb, kjs[h]) * FOX_SCALE
                out.append((dk + _dot_tn(dsb, qi), dv + _dot_tn(p.astype(bf16), dob),
                            dc - jnp.sum(ds, 0, keepdims=True)))
            return tuple(out)

        init = tuple((jnp.zeros((tq, FOX_HD), f32), jnp.zeros((tq, FOX_HD), f32), jnp.zeros((1, tq), f32))
                     for _ in heads)
        carry = block(pl.multiple_of(j * tq, tq), init, diag_bias)
        carry = lax.fori_loop(j + 1, nq, lambda i, c: block(pl.multiple_of(i * tq, tq), c, None), carry)
        sub = lax.broadcasted_iota(jnp.int32, (128, tq), 0)
        dct = jnp.zeros((128, tq), f32)
        for h, hs in enumerate(heads):
            dk, dv, dc = carry[h]
            dk_ref[:, hs] = dk
            dv_ref[:, hs] = dv
            dct = jnp.where(sub == SM_F + h, dc, dct)
        dct_ref[...] = dct

    full = lambda w_, col: pl.BlockSpec((L, w_), lambda j: (0, col // w_), pipeline_mode=pl.Buffered(1))
    blk = lambda col: pl.BlockSpec((tq, FOX_W), lambda j: (j, col // FOX_W))
    return pl.pallas_call(
        body, name="fox_bwd", grid=(nq,),
        in_specs=[full(FOX_W, P_Q), blk(P_K), blk(P_V), full(FOX_W, SSD_W), full(FOX_W, 0), full(128, 0), full(128, 0),
                  pl.BlockSpec((128, tq), lambda j: (0, j))],
        out_specs=[pl.BlockSpec((L, FOX_W), lambda j: (0, 0)), pl.BlockSpec((tq, FOX_W), lambda j: (j, 0)),
                   pl.BlockSpec((tq, FOX_W), lambda j: (j, 0)), pl.BlockSpec((128, tq), lambda j: (0, j))],
        out_shape=[jax.ShapeDtypeStruct((L, FOX_W), f32)] * 3 + [jax.ShapeDtypeStruct((128, L), f32)],
        compiler_params=_cp(("arbitrary",)))(proj, proj, proj, dycat, o_acc, lse, cum, cumt)


HL = 128
AW = FOX_H * HL


def _np_place(rows, cols, pairs, dtype):
    import numpy as np
    m = np.zeros((rows, cols), np.float32)
    for r, c in pairs:
        m[r, c] = 1.0
    return jnp.asarray(m, dtype)


def _fox_consts():
    data = [(h * FOX_HD + d, h * HL + d) for h in range(FOX_H) for d in range(FOX_HD)]
    return dict(
        pq=_np_place(FOX_W, AW, data, bf16),
        pqt=_np_place(AW, FOX_W, [(c, r) for r, c in data], bf16),
        cum_a=[_np_place(128, AW, [(SM_F + h, h * HL + 64 + r) for h in range(FOX_H)], bf16) for r in range(3)],
        head_a=[_np_place(128, AW, [(h, h * HL + 64 + r) for h in range(FOX_H)], bf16) for r in range(3)],
        head_b=[_np_place(128, AW, [(h, h * HL + 67 + r) for h in range(FOX_H)], bf16) for r in range(3)],
        group=_np_place(FOX_W, 128, [(h * FOX_HD + d, h) for h in range(FOX_H) for d in range(FOX_HD)], f32),
        col_a=_np_place(AW, 128, [(h * HL + 64, SM_F + h) for h in range(FOX_H)], f32))


def _split3(x):
    hi = x.astype(bf16)
    r1 = x - hi.astype(f32)
    mid = r1.astype(bf16)
    return hi, mid, (r1 - mid.astype(f32)).astype(bf16)


def _slot_ones(tt, first):
    lane = lax.broadcasted_iota(jnp.int32, (tt, AW), 1) % HL
    return jnp.where((lane >= first) & (lane < first + 3), 1.0, 0.0)


def _fox_prep(proj, hp, cst, *, tt=256):
    L = proj.shape[0]

    def body(q_ref, k_ref, v_ref, sm_ref, hp_ref, pq_ref, c0_ref, c1_ref, c2_ref, qa_ref, ka_ref, va_ref, tot_ref):
        xx = sm_ref[...] + hp_ref[3:4, :]
        logf = jnp.minimum(xx, 0.0) - _log1pexp(xx)
        r = lax.broadcasted_iota(jnp.int32, (tt, tt), 0)
        c = lax.broadcasted_iota(jnp.int32, (tt, tt), 1)
        cum = _dot_hi(jnp.where(r >= c, 1.0, 0.0), logf)
        tot_ref[...] = jnp.broadcast_to(cum[tt - 1:tt, :], (8, 128))
        parts = _split3(-cum)
        pq = pq_ref[...]
        a_ones, b_ones = _slot_ones(tt, 64), _slot_ones(tt, 67)
        qa_ref[...] = (_dot((q_ref[...] * FOX_SCALE).astype(bf16), pq) + a_ones).astype(bf16)
        ka = _dot(k_ref[...].astype(bf16), pq) + b_ones
        for part, c_ref in zip(parts, (c0_ref, c1_ref, c2_ref)):
            ka = ka + _dot(part, c_ref[...])
        ka_ref[...] = ka.astype(bf16)
        va_ref[...] = (_dot(v_ref[...].astype(bf16), pq) + a_ones).astype(bf16)

    col = lambda c_: pl.BlockSpec((tt, FOX_W), lambda i: (i, c_ // FOX_W))
    out = pl.BlockSpec((tt, AW), lambda i: (i, 0))
    return pl.pallas_call(
        body, name="fox_prep", grid=(L // tt,),
        in_specs=[col(P_Q), col(P_K), col(P_V), pl.BlockSpec((tt, 128), lambda i: (i, P_SM // 128)),
                  _const_spec((8, 128)), _const_spec((FOX_W, AW))] + [_const_spec((128, AW))] * 3,
        out_specs=[out, out, out, pl.BlockSpec((8, 128), lambda i: (i, 0))],
        out_shape=[jax.ShapeDtypeStruct((L, AW), bf16)] * 3 + [jax.ShapeDtypeStruct((8 * (L // tt), 128), f32)],
        compiler_params=_cp(("parallel",)))(proj, proj, proj, proj, hp, cst["pq"], *cst["cum_a"])


def _fox_attn_fwd(qa, ka, va, tot, *, tq=256):
    L = qa.shape[0]

    def body(qa_ref, ka_ref, va_ref, tot_ref, o_ref, oa_ref, lse_ref):
        i = pl.program_id(0)
        row = lax.broadcasted_iota(jnp.int32, (tq, tq), 0)
        col = lax.broadcasted_iota(jnp.int32, (tq, tq), 1)
        diag_bias = jnp.where(row >= col, 0.0, NEG)

        def block(j, carry, t_j, bias):
            r0 = pl.multiple_of(j * tq, tq)
            out = []
            for h in range(FOX_H):
                hl = slice(h * HL, (h + 1) * HL)
                m, acc = carry[h]
                m_in = m - t_j[h]
                s = _dot_nt(qa_ref[:, hl], ka_ref[pl.ds(r0, tq), hl])
                if bias is not None:
                    s = s + bias
                m_new = jnp.maximum(m_in, jnp.max(s, -1, keepdims=True))
                p = jnp.exp(s - m_new)
                al = jnp.exp(m_in - m_new)
                pb = p.astype(bf16)
                p_lo = (p - pb.astype(f32)).astype(bf16)
                vj = va_ref[pl.ds(r0, tq), hl]
                out.append((m_new, al * acc + (_dot(pb, vj) + _dot(p_lo, vj))))
            return tuple(out)

        def step(k, state):
            carry, gap = state
            j = i - 1 - k
            t_j = [tot_ref[j, h] for h in range(FOX_H)]
            return block(j, carry, t_j, None), tuple(g + t for g, t in zip(gap, t_j))

        init = tuple((jnp.full((tq, 1), NEG, f32), jnp.zeros((tq, HL), f32)) for _ in range(FOX_H))
        zero_gap = tuple(jnp.zeros((), f32) for _ in range(FOX_H))
        carry = block(i, init, zero_gap, diag_bias)
        carry, gap = lax.fori_loop(0, i, step, (carry, zero_gap))
        lane = lax.broadcasted_iota(jnp.int32, (tq, 128), 1)
        lse_all = jnp.zeros((tq, 128), f32)
        for h in range(FOX_H):
            hs = slice(h * FOX_HD, (h + 1) * FOX_HD)
            m, acc = carry[h]
            l = acc[:, FOX_HD:FOX_HD + 1]
            o = acc[:, :FOX_HD] * (1.0 / l)
            o_ref[:, hs] = o.astype(bf16)
            oa_ref[:, hs] = o
            lse_all = jnp.where(lane == h, m + gap[h] + jnp.log(l), lse_all)
        lse_ref[...] = lse_all

    full = pl.BlockSpec((L, AW), lambda i: (0, 0), pipeline_mode=pl.Buffered(1))
    return pl.pallas_call(
        body, name="fox_fwd", grid=(L // tq,),
        in_specs=[pl.BlockSpec((tq, AW), lambda i: (i, 0)), full, full, pl.BlockSpec(memory_space=pltpu.SMEM)],
        out_specs=[pl.BlockSpec((tq, FOX_W), lambda i: (i, 0)), pl.BlockSpec((tq, FOX_W), lambda i: (i, 0)),
                   pl.BlockSpec((tq, 128), lambda i: (i, 0))],
        out_shape=[jax.ShapeDtypeStruct((L, FOX_W), bf16), jax.ShapeDtypeStruct((L, FOX_W), f32),
                   jax.ShapeDtypeStruct((L, 128), f32)],
        compiler_params=_cp(("parallel",)))(qa, ka, va, tot)


def _fox_bprep(qa, dycat, o_acc, lse, cst, *, tt=256):
    L = qa.shape[0]

    def body(qa_ref, do_ref, oa_ref, lse_ref, pq_ref, g_ref, a0, a1, a2, b0, b1, b2, qb_ref, doa_ref, qbt_ref, doat_ref):
        dob = do_ref[...].astype(bf16)
        delta = _dot_hi(dob.astype(f32) * oa_ref[...], g_ref[...])
        doa = _dot(dob, pq_ref[...])
        for part, ref in zip(_split3(-delta), (a0, a1, a2)):
            doa = doa + _dot(part, ref[...])
        qb = qa_ref[...].astype(f32)
        for part, ref in zip(_split3(-lse_ref[...]), (b0, b1, b2)):
            qb = qb + _dot(part, ref[...])
        doa, qb = doa.astype(bf16), qb.astype(bf16)
        doa_ref[...] = doa
        qb_ref[...] = qb
        doat_ref[...] = doa.T
        qbt_ref[...] = qb.T

    tok = lambda w_: pl.BlockSpec((tt, w_), lambda i: (i, 0))
    tr = pl.BlockSpec((AW, tt), lambda i: (0, i))
    return pl.pallas_call(
        body, name="fox_bprep", grid=(L // tt,),
        in_specs=[tok(AW), pl.BlockSpec((tt, FOX_W), lambda i: (i, SSD_W // FOX_W)), tok(FOX_W), tok(128),
                  _const_spec((FOX_W, AW)), _const_spec((FOX_W, 128))] + [_const_spec((128, AW))] * 6,
        out_specs=[tok(AW), tok(AW), tr, tr],
        out_shape=[jax.ShapeDtypeStruct((L, AW), bf16)] * 2 + [jax.ShapeDtypeStruct((AW, L), bf16)] * 2,
        compiler_params=_cp(("parallel",)))(qa, dycat, o_acc, lse, cst["pq"], cst["group"], *cst["head_a"], *cst["head_b"])


def _fox_attn_bwd(ka, va, tot, qb, doa, qbt, doat, *, tq=256):
    L = ka.shape[0]
    nq = L // tq

    def body(ka_ref, va_ref, tot_ref, qb_ref, doa_ref, qbt_ref, doat_ref, dq_ref, dkt_ref, dvt_ref):
        j = pl.program_id(0)

        @pl.when(j == 0)
        def _():
            dq_ref[...] = jnp.zeros_like(dq_ref)
        row = lax.broadcasted_iota(jnp.int32, (tq, tq), 0)
        col = lax.broadcasted_iota(jnp.int32, (tq, tq), 1)
        diag_bias = jnp.where(row >= col, 0.0, NEG)

        def block(i, carry, gap, bias):
            r0 = pl.multiple_of(i * tq, tq)
            out = []
            for h in range(FOX_H):
                hl = slice(h * HL, (h + 1) * HL)
                dkt, dvt, dsum = carry[h]
                kj = ka_ref[:, hl]
                s = _dot_nt(qb_ref[pl.ds(r0, tq), hl], kj) + gap[h]
                if bias is not None:
                    s = s + bias
                p = jnp.exp(s)
                ds = p * _dot_nt(doa_ref[pl.ds(r0, tq), hl], va_ref[:, hl])
                dsb = ds.astype(bf16)
                dq_ref[pl.ds(r0, tq), hl] += _dot(dsb, kj)
                out.append((dkt + _dot(qbt_ref[hl, pl.ds(r0, tq)], dsb),
                            dvt + _dot(doat_ref[hl, pl.ds(r0, tq)], p.astype(bf16)),
                            dsum + jnp.sum(ds, 0, keepdims=True)))
            return tuple(out)

        init = tuple((jnp.zeros((HL, tq), f32), jnp.zeros((HL, tq), f32), jnp.zeros((1, tq), f32))
                     for _ in range(FOX_H))
        def step(i, state):
            carry, gap = state
            gap = tuple(g + tot_ref[i - 1, h] for h, g in enumerate(gap))
            return block(i, carry, gap, None), gap

        zero_gap = tuple(jnp.zeros((), f32) for _ in range(FOX_H))
        carry = block(j, init, zero_gap, diag_bias)
        carry, _ = lax.fori_loop(j + 1, nq, step, (carry, zero_gap))
        for h in range(FOX_H):
            hl = slice(h * HL, (h + 1) * HL)
            dkt_ref[hl, :] = carry[h][0]
            dvt_ref[hl, :] = carry[h][1]
            dkt_ref[h * HL + FOX_HD:h * HL + FOX_HD + 1, :] = carry[h][2]

    full = lambda shape: pl.BlockSpec(shape, lambda j: (0, 0), pipeline_mode=pl.Buffered(1))
    blk = pl.BlockSpec((tq, AW), lambda j: (j, 0))
    trb = pl.BlockSpec((AW, tq), lambda j: (0, j))
    return pl.pallas_call(
        body, name="fox_bwd", grid=(nq,),
        in_specs=[blk, blk, pl.BlockSpec(memory_space=pltpu.SMEM), full((L, AW)), full((L, AW)), full((AW, L)),
                  full((AW, L))],
        out_specs=[pl.BlockSpec((L, AW), lambda j: (0, 0)), trb, trb],
        out_shape=[jax.ShapeDtypeStruct((L, AW), f32), jax.ShapeDtypeStruct((AW, L), f32),
                   jax.ShapeDtypeStruct((AW, L), f32)],
        compiler_params=_cp(("arbitrary",)))(ka, va, tot, qb, doa, qbt, doat)


def _fox_post(dq, dkt, dvt, proj, ddt, hp, cst, *, tt=256):
    L = proj.shape[0]
    n = L // tt

    def body(dq_ref, dkt_ref, dvt_ref, sm_ref, ddt_ref, hp_ref, pqt_ref, ca_ref,
             dqo_ref, dko_ref, dvo_ref, dsm_ref, acc_ref, carry):
        @pl.when(pl.program_id(0) == 0)
        def _():
            carry[...] = jnp.zeros_like(carry)
            acc_ref[...] = jnp.zeros_like(acc_ref)
        pqt = pqt_ref[...]
        dk_full = dkt_ref[...].T
        dqo_ref[...] = _dot((dq_ref[...] * FOX_SCALE).astype(bf16), pqt).astype(bf16)
        dko_ref[...] = _dot(dk_full.astype(bf16), pqt).astype(bf16)
        dvo_ref[...] = _dot(dvt_ref[...].T.astype(bf16), pqt).astype(bf16)
        dc = -_dot_hi(dk_full, ca_ref[...])
        r = lax.broadcasted_iota(jnp.int32, (tt, tt), 0)
        c = lax.broadcasted_iota(jnp.int32, (tt, tt), 1)
        dl = _dot_hi(jnp.where(r <= c, 1.0, 0.0), dc) + carry[0:1, :]
        carry[0:1, :] += jnp.sum(dc, 0, keepdims=True)
        xx = sm_ref[...] + hp_ref[3:4, :]
        lane = lax.broadcasted_iota(jnp.int32, (tt, 128), 1)
        dlogit = jnp.where((lane >= SM_F) & (lane < SM_F + FOX_H), dl * _sigmoid(-xx), 0.0)
        acc_ref[3:4, :] += jnp.sum(dlogit, 0, keepdims=True)
        dsm_ref[...] = (dlogit + ddt_ref[...]).astype(bf16)

    rev = lambda w_: pl.BlockSpec((tt, w_), lambda i: (n - 1 - i, 0))
    revt = pl.BlockSpec((AW, tt), lambda i: (0, n - 1 - i))
    return pl.pallas_call(
        body, name="fox_post", grid=(n,),
        in_specs=[rev(AW), revt, revt, pl.BlockSpec((tt, 128), lambda i: (n - 1 - i, P_SM // 128)), rev(128),
                  _const_spec((8, 128)), _const_spec((AW, FOX_W)), _const_spec((AW, 128))],
        out_specs=[rev(FOX_W), rev(FOX_W), rev(FOX_W), rev(128), pl.BlockSpec((8, 128), lambda i: (0, 0))],
        out_shape=[jax.ShapeDtypeStruct((L, FOX_W), bf16)] * 3 + [jax.ShapeDtypeStruct((L, 128), bf16),
                                                                   jax.ShapeDtypeStruct((8, 128), f32)],
        scratch_shapes=[pltpu.VMEM((8, 128), f32)],
        compiler_params=_cp(("arbitrary",)))(dq, dkt, dvt, proj, ddt, hp, cst["pqt"], cst["col_a"])


SSD_GW = SSD_W // SSD_G
SSD_HPG = SSD_H // SSD_G


def _ssd_pre(x, xprev, sm, cp_ref, hp_ref, tc):
    pre = (cp_ref[4:5, :] + cp_ref[3:4, :] * x + cp_ref[2:3, :] * _shift_down(x, xprev, 1)
           + cp_ref[1:2, :] * _shift_down(x, xprev, 2) + cp_ref[0:1, :] * _shift_down(x, xprev, 3))
    sig = _sigmoid(pre)
    raw = sm + hp_ref[0:1, :]
    dt = jnp.maximum(raw, 0.0) + _log1pexp(raw)
    a_neg = -jnp.exp(hp_ref[1:2, :])
    r = lax.broadcasted_iota(jnp.int32, (tc, tc), 0)
    c = lax.broadcasted_iota(jnp.int32, (tc, tc), 1)
    cs = _dot_hi(jnp.where(r >= c, 1.0, 0.0), dt * a_neg)
    return pre, sig, raw, dt, a_neg, cs, cs.T, r >= c


def _ssd_fwd(proj, cp, hp, ng, *, tc=256):
    L = proj.shape[0]
    nc = L // tc

    def body(xc_ref, xp_ref, z_ref, sm_ref, cp_ref, hp_ref, ng_ref, y_ref, ypre_ref, sin_ref, s_scr):
        i = pl.program_id(0)

        @pl.when(i == 0)
        def _():
            s_scr[...] = jnp.zeros_like(s_scr)
        x = xc_ref[...]
        xprev = xp_ref[...] * jnp.where(i > 0, 1.0, 0.0)
        pre, sig, _, dt, _, cs, cst, tril = _ssd_pre(x, xprev, sm_ref[...], cp_ref, hp_ref, tc)
        xbc = pre * sig
        sin_ref[...] = s_scr[...]
        for g in range(SSD_G):
            bg = xbc[:, SSD_W + g * SSD_N:SSD_W + (g + 1) * SSD_N]
            cg = xbc[:, SSD_W + SSD_G * SSD_N + g * SSD_N:SSD_W + SSD_G * SSD_N + (g + 1) * SSD_N].astype(bf16)
            cb = _dot_nt(cg, bg.astype(bf16))
            for e in range(SSD_HPG):
                h = g * SSD_HPG + e
                hs = slice(h * SSD_HD, (h + 1) * SSD_HD)
                xs = xbc[:, hs]
                csc = cs[:, h:h + 1]
                lm = jnp.where(tril, jnp.exp(jnp.minimum(csc - cst[h:h + 1, :], 0.0)), 0.0)
                xdt = (xs * dt[:, h:h + 1]).astype(bf16)
                s_h = s_scr[:, hs]
                y = _dot((cb * lm).astype(bf16), xdt) + jnp.exp(csc) * _dot(cg, s_h.astype(bf16))
                ypre_ref[:, hs] = y + hp_ref[2:3, h:h + 1] * xs
                cl = cs[tc - 1:tc, h:h + 1]
                bd = (bg * jnp.exp(cl - csc)).astype(bf16)
                s_scr[:, hs] = jnp.exp(cl) * s_h + _dot_tn(bd, xdt)
        z = z_ref[...]
        yz = ypre_ref[...] * (z * _sigmoid(z))
        for g in range(SSD_G):
            gs = slice(g * SSD_GW, (g + 1) * SSD_GW)
            yg = yz[:, gs]
            r = lax.rsqrt(jnp.mean(yg * yg, -1, keepdims=True) + RMS_EPS)
            y_ref[:, gs] = (yg * r * ng_ref[0:1, gs]).astype(bf16)

    return pl.pallas_call(
        body, name="ssd_fwd", grid=(nc,),
        in_specs=[pl.BlockSpec((tc, 1024), lambda i: (i, 0)),
                  pl.BlockSpec((tc, 1024), lambda i: (jnp.maximum(i - 1, 0), 0)),
                  pl.BlockSpec((tc, SSD_W), lambda i: (i, P_Z // SSD_W)),
                  pl.BlockSpec((tc, 128), lambda i: (i, P_SM // 128)),
                  _const_spec((8, 1024)), _const_spec((8, 128)), _const_spec((8, SSD_W))],
        out_specs=[pl.BlockSpec((tc, SSD_W), lambda i: (i, 0)), pl.BlockSpec((tc, SSD_W), lambda i: (i, 0)),
                   pl.BlockSpec((SSD_N, SSD_W), lambda i: (i, 0))],
        out_shape=[jax.ShapeDtypeStruct((L, SSD_W), bf16), jax.ShapeDtypeStruct((L, SSD_W), f32),
                   jax.ShapeDtypeStruct((nc * SSD_N, SSD_W), f32)],
        scratch_shapes=[pltpu.VMEM((SSD_N, SSD_W), f32)],
        compiler_params=_cp(("arbitrary",)))(proj, proj, proj, proj, cp, hp, ng)


def _ssd_bwd(proj, dycat, ypre, sin, cp, hp, ng, *, tc=256):
    L = proj.shape[0]
    nc = L // tc

    def body(xc_ref, xp_ref, z_ref, sm_ref, cp_ref, hp_ref, ng_ref, sin_ref, ypre_ref, dy_ref,
             dxbc_ref, dz_ref, ddt_ref, acc1_ref, acc2_ref, ds_scr, dnext_scr, dxbc_scr):
        i = pl.program_id(0)
        c_idx = nc - 1 - i

        @pl.when(i == 0)
        def _():
            ds_scr[...] = jnp.zeros_like(ds_scr)
            dnext_scr[...] = jnp.zeros_like(dnext_scr)
            acc1_ref[...] = jnp.zeros_like(acc1_ref)
            acc2_ref[...] = jnp.zeros_like(acc2_ref)
        x = xc_ref[...]
        xprev = xp_ref[...] * jnp.where(c_idx > 0, 1.0, 0.0)
        pre, sig, raw, dt, a_neg, cs, cst, tril = _ssd_pre(x, xprev, sm_ref[...], cp_ref, hp_ref, tc)
        xbc = pre * sig
        z = z_ref[...]
        sz = _sigmoid(z)
        silz = z * sz
        yall = ypre_ref[...]
        yz = yall * silz
        dy = dy_ref[...]
        dyz_parts = []
        for g in range(SSD_G):
            gs = slice(g * SSD_GW, (g + 1) * SSD_GW)
            yg, dyg = yz[:, gs], dy[:, gs]
            r = lax.rsqrt(jnp.mean(yg * yg, -1, keepdims=True) + RMS_EPS)
            acc1_ref[5:6, gs] += jnp.sum(dyg * yg * r, 0, keepdims=True)
            dyn = dyg * ng_ref[0:1, gs]
            dyz_parts.append(r * (dyn - yg * (r * r) * jnp.mean(dyn * yg, -1, keepdims=True)))
        dyz = jnp.concatenate(dyz_parts, axis=1)
        dz_ref[...] = (dyz * yall * (sz * (1.0 + z * (1.0 - sz)))).astype(bf16)
        dyall = dyz * silz

        lane1 = lax.broadcasted_iota(jnp.int32, (1, 128), 1)
        sub = lax.broadcasted_iota(jnp.int32, (128, tc), 0)
        rowc = lax.broadcasted_iota(jnp.int32, (tc, 1), 0)
        dcs = jnp.zeros((tc, 128), f32)
        dcsr = jnp.zeros((128, tc), f32)
        ddt = jnp.zeros((tc, 128), f32)
        dd_row = jnp.zeros((1, 128), f32)
        for g in range(SSD_G):
            b0 = SSD_W + g * SSD_N
            c0 = SSD_W + SSD_G * SSD_N + g * SSD_N
            bg = xbc[:, b0:b0 + SSD_N]
            bgb = bg.astype(bf16)
            cgb = xbc[:, c0:c0 + SSD_N].astype(bf16)
            cb = _dot_nt(cgb, bgb)
            dbg = jnp.zeros((tc, SSD_N), f32)
            dcg = jnp.zeros((tc, SSD_N), f32)
            for e in range(SSD_HPG):
                h = g * SSD_HPG + e
                hs = slice(h * SSD_HD, (h + 1) * SSD_HD)
                oh = jnp.where(lane1 == h, 1.0, 0.0)
                xs = xbc[:, hs]
                dth = dt[:, h:h + 1]
                csc = cs[:, h:h + 1]
                lm = jnp.where(tril, jnp.exp(jnp.minimum(csc - cst[h:h + 1, :], 0.0)), 0.0)
                m = cb * lm
                xdt = (xs * dth).astype(bf16)
                s_h = sin_ref[:, hs]
                s_hb = s_h.astype(bf16)
                dyh = dyall[:, hs]
                dyb = dyh.astype(bf16)
                dd_row = dd_row + oh * jnp.sum(dyh * xs)
                dxs = hp_ref[2:3, h:h + 1] * dyh
                ecs = jnp.exp(csc)
                cs_prod = _dot(cgb, s_hb)
                dcsb = (ecs * dyh).astype(bf16)
                dcg = dcg + _dot_nt(dcsb, s_hb)
                ds_in = _dot_tn(cgb, dcsb)
                dcs_h = jnp.sum(dyh * ecs * cs_prod, -1, keepdims=True)
                dm = _dot_nt(dyb, xdt)
                w = dm * m
                dcs_h = dcs_h + jnp.sum(w, -1, keepdims=True)
                dcsr = jnp.where(sub == h, jnp.sum(w, 0, keepdims=True), dcsr)
                dcbb = (dm * lm).astype(bf16)
                dcg = dcg + _dot(dcbb, bgb)
                dbg = dbg + _dot_tn(dcbb, cgb)
                dxdt = _dot_tn(m.astype(bf16), dyb)
                dsn = ds_scr[:, hs]
                dsnb = dsn.astype(bf16)
                cl = cs[tc - 1:tc, h:h + 1]
                dec = jnp.exp(cl - csc)
                dxdt = dxdt + _dot((bg * dec).astype(bf16), dsnb)
                dbd = _dot_nt(xdt, dsnb)
                dbg = dbg + dbd * dec
                gdec = jnp.sum(dbd * bg, -1, keepdims=True) * dec
                ecl = jnp.exp(cl)
                dcl = jnp.sum(gdec) + jnp.sum(dsn * s_h) * ecl
                ds_scr[:, hs] = ecl * dsn + ds_in
                dcs_h = dcs_h - gdec + jnp.where(rowc == tc - 1, dcl, 0.0)
                dcs = dcs + dcs_h * oh
                dxbc_scr[:, hs] = dxs + dxdt * dth
                ddt = ddt + jnp.sum(dxdt * xs, -1, keepdims=True) * oh
            dxbc_scr[:, b0:b0 + SSD_N] = dbg
            dxbc_scr[:, c0:c0 + SSD_N] = dcg
        dcs = dcs - dcsr.T
        r_i = lax.broadcasted_iota(jnp.int32, (tc, tc), 0)
        c_i = lax.broadcasted_iota(jnp.int32, (tc, tc), 1)
        da = _dot_hi(jnp.where(r_i <= c_i, 1.0, 0.0), dcs)
        ddt = ddt + da * a_neg
        acc2_ref[1:2, :] += jnp.sum(da * dt, 0, keepdims=True) * a_neg
        lane = lax.broadcasted_iota(jnp.int32, (tc, 128), 1)
        ddraw = jnp.where(lane < SSD_H, ddt * _sigmoid(raw), 0.0)
        acc2_ref[0:1, :] += jnp.sum(ddraw, 0, keepdims=True)
        acc2_ref[2:3, :] += dd_row
        ddt_ref[...] = ddraw
        dpre = dxbc_scr[...] * (sig * (1.0 + pre * (1.0 - sig)))
        acc1_ref[4:5, :] += jnp.sum(dpre, 0, keepdims=True)
        for k in range(SSD_K):
            acc1_ref[k:k + 1, :] += jnp.sum(dpre * _shift_down(x, xprev, SSD_K - 1 - k), 0, keepdims=True)
        dnext = dnext_scr[...]
        dxbc_ref[...] = (cp_ref[3:4, :] * dpre + cp_ref[2:3, :] * _shift_up(dpre, dnext, 1)
                         + cp_ref[1:2, :] * _shift_up(dpre, dnext, 2)
                         + cp_ref[0:1, :] * _shift_up(dpre, dnext, 3)).astype(bf16)
        dnext_scr[...] = dpre

    rev = lambda w_, col: pl.BlockSpec((tc, w_), lambda i: (nc - 1 - i, col // w_))
    return pl.pallas_call(
        body, name="ssd_bwd", grid=(nc,),
        in_specs=[rev(1024, 0), pl.BlockSpec((tc, 1024), lambda i: (jnp.maximum(nc - 2 - i, 0), 0)),
                  rev(SSD_W, P_Z), rev(128, P_SM),
                  _const_spec((8, 1024)), _const_spec((8, 128)), _const_spec((8, SSD_W)),
                  pl.BlockSpec((SSD_N, SSD_W), lambda i: (nc - 1 - i, 0)), rev(SSD_W, 0), rev(SSD_W, 0)],
        out_specs=[rev(1024, 0), rev(SSD_W, 0), rev(128, 0),
                   pl.BlockSpec((8, 1024), lambda i: (0, 0)), pl.BlockSpec((8, 128), lambda i: (0, 0))],
        out_shape=[jax.ShapeDtypeStruct((L, 1024), bf16), jax.ShapeDtypeStruct((L, SSD_W), bf16),
                   jax.ShapeDtypeStruct((L, 128), f32), jax.ShapeDtypeStruct((8, 1024), f32),
                   jax.ShapeDtypeStruct((8, 128), f32)],
        scratch_shapes=[pltpu.VMEM((SSD_N, SSD_W), f32), pltpu.VMEM((tc, 1024), f32), pltpu.VMEM((tc, 1024), f32)],
        compiler_params=_cp(("arbitrary",)))(proj, proj, proj, proj, cp, hp, ng, sin, ypre, dycat)


def _pack_cols(w):
    pad = jnp.zeros(w.shape[:-1] + (P_W - P_SM - SSD_H - FOX_H,), w.dtype)
    return jnp.concatenate([w[..., 512:1536], w[..., 0:512], w[..., 1544:2312], w[..., 2316:3084],
                            w[..., 1536:1544], w[..., 2312:2316], pad], axis=-1)


def _unpack_cols(g):
    return jnp.concatenate([g[..., 1024:1536], g[..., 0:1024], g[..., 3072:3080], g[..., 1536:2304],
                            g[..., 3080:3084], g[..., 2304:3072]], axis=-1)


def _rows8(*rows):
    width = max(r.shape[-1] for r in rows)
    out = [jnp.pad(r.astype(f32), (0, width - r.shape[-1])) for r in rows]
    out += [jnp.zeros((width,), f32)] * (8 - len(out))
    return jnp.stack(out)


def _local_step(x, tgt, p, weights_of, grads_done):
    gb_in = _rows8(p["ln_in_g"], p["ln_in_b"])
    cst = _fox_consts()
    x0 = _ln_in_fwd(x, gb_in)
    saved = []
    for l in range(DEPTH):
        mv = [_rows8(p["mod"][l, 3 * j], p["mod"][l, 3 * j + 1], p["mod"][l, 3 * j + 2], p["ln_g"][l, j], p["ln_b"][l, j])
              for j in range(3)]
        cp = _rows8(*[p["ssd_conv_w"][l, k] for k in range(SSD_K)], p["ssd_conv_b"][l])
        hp = _rows8(jnp.pad(p["ssd_dt_bias"][l], (0, 120)), jnp.pad(p["ssd_a_log"][l], (0, 120)),
                    jnp.pad(p["ssd_d"][l], (0, 120)), jnp.pad(p["fox_f_bias"][l], (SM_F, 128 - SM_F - FOX_H)))
        ng = _rows8(p["ssd_norm_g"][l])
        scw = _rows8(*[p["sconv_w"][l, k] for k in range(SC_K)])
        w = dict(weights_of(l, "ffn1", x0))
        x1 = _ffn_fwd(x0, mv[0], w["ffn1_w_in"], w["ffn1_w_out"])
        w.update(weights_of(l, "mix", x1))
        proj = _inproj_fwd(x1, mv[1], w["mix_w_in"])
        y_ssd, ypre, sin = _ssd_fwd(proj, cp, hp, ng)
        qa, ka, va, tot = _fox_prep(proj, hp, cst)
        tot = tot[::8, SM_F:SM_F + FOX_H]
        o, o_acc, lse = _fox_attn_fwd(qa, ka, va, tot)
        y_sc = _sconv_fwd(proj, scw)
        ycat = jnp.concatenate([y_ssd, o, y_sc], axis=1)
        x2 = _outproj_fwd(x1, ycat, mv[1], w["mix_w_out"])
        w.update(weights_of(l, "ffn2", x2))
        x3 = _ffn_fwd(x2, mv[2], w["ffn2_w_in"], w["ffn2_w_out"])
        saved.append((x0, x1, x2, mv, cp, hp, ng, scw, proj, ypre, sin, qa, ka, va, tot, o_acc, lse, ycat, w))
        x0 = x3
    dx, loss_acc = _loss_head(x0, tgt)

    g = {k: [None] * DEPTH for k in (
        "mod", "ln_g", "ln_b", "ssd_conv_w", "ssd_conv_b", "ssd_dt_bias", "ssd_a_log", "ssd_d", "ssd_norm_g",
        "fox_f_bias", "sconv_w")}
    def behind(small, tok):
        return small if tok is None else small + tok[0, 0]

    token = None
    for l in reversed(range(DEPTH)):
        x0, x1, x2, mv, cp, hp, ng, scw, proj, ypre, sin, qa, ka, va, tot, o_acc, lse, ycat, w = saved[l]
        dx, ht, da, actt, dyb, a2 = _ffn_bwd(x2, dx, behind(mv[2], token), w["ffn2_w_in"], w["ffn2_w_out"])
        token = grads_done(l, "ffn2_w_out", {"ffn2_w_out": _matmul_tokens(actt, dyb, tn=D, tk=DW_TK, name="dw_ffn_out")})
        token = grads_done(l, "ffn2_w_in", {"ffn2_w_in": _matmul_tokens(ht, da, tn=FS, tk=DW_TK, name="dw_ffn_in",
                                                                        after=token)})
        mv1 = behind(mv[1], token)
        dxp, dyb, dycat, ycat_t, a1 = _outproj_bwd(x1, ycat, dx, mv1, w["mix_w_out"])
        gw_mix_out = _matmul_tokens(ycat_t, dyb, tn=D // 2, tk=DW_TK, name="dw_mix_out")
        dxbc, dz, ddt, acc1, acc2 = _ssd_bwd(proj, dycat, ypre, sin, cp, hp, ng)
        qb, doa, qbt, doat = _fox_bprep(qa, dycat, o_acc, lse, cst)
        dq, dkt, dvt = _fox_attn_bwd(ka, va, tot, qb, doa, qbt, doat)
        dq, dk, dv, dsm, accf = _fox_post(dq, dkt, dvt, proj, ddt, hp, cst)
        dsb, dsc, dsx, accs = _sconv_bwd(proj, dycat, scw)
        dproj = jnp.concatenate([dxbc, dz, dq, dk, dv, dsb, dsc, dsx, dsm], axis=1)
        dx, ht, a1b = _inproj_bwd(x1, dxp, dproj, mv1, w["mix_w_in"])
        token = grads_done(l, "mix", {"mix_w_in": _matmul_tokens(ht, dproj, tn=P_W // 5, tk=DW_TK, name="dw_mix_in"),
                                      "mix_w_out": gw_mix_out})
        dx, ht, da, actt, dyb, a0 = _ffn_bwd(x0, dx, behind(mv[0], token), w["ffn1_w_in"], w["ffn1_w_out"])
        token = grads_done(l, "ffn1_w_out", {"ffn1_w_out": _matmul_tokens(actt, dyb, tn=D, tk=DW_TK, name="dw_ffn_out")})
        token = grads_done(l, "ffn1_w_in", {"ffn1_w_in": _matmul_tokens(ht, da, tn=FS, tk=DW_TK, name="dw_ffn_in",
                                                                        after=token)})
        g["mod"][l] = jnp.concatenate([a0[0:3], a1b[0:2], a1[2:3], a2[0:3]], axis=0)
        g["ln_g"][l] = jnp.stack([a0[3], a1[3], a2[3]])
        g["ln_b"][l] = jnp.stack([a0[4], a1[4], a2[4]])
        g["ssd_conv_w"][l] = acc1[0:SSD_K]
        g["ssd_conv_b"][l] = acc1[4]
        g["ssd_norm_g"][l] = acc1[5, :SSD_W]
        g["ssd_dt_bias"][l] = acc2[0, :SSD_H]
        g["ssd_a_log"][l] = acc2[1, :SSD_H]
        g["ssd_d"][l] = acc2[2, :SSD_H]
        g["fox_f_bias"][l] = accf[3, SM_F:SM_F + FOX_H]
        g["sconv_w"][l] = accs[0:SC_K]
    grad_x, a_in = _ln_in_bwd(x, dx, behind(gb_in, token))
    g = {k: jnp.stack(v) for k, v in g.items()}
    g["ln_in_g"], g["ln_in_b"] = a_in[0], a_in[1]
    return loss_acc[0, 0], grad_x, g


MESH = pl.DeviceIdType.MESH
ANY = pl.BlockSpec(memory_space=pl.ANY)


def _all_gather(shards, *, in_vmem, name):
    n_arr = len(shards)

    def body(*refs):
        x_refs, out_refs = refs[:n_arr], refs[n_arr:2 * n_arr]
        send_sems, recv_sems, local_sems = refs[2 * n_arr:]
        x, y, c = lax.axis_index("x"), lax.axis_index("y"), lax.axis_index("c")
        me, sibling = (x, y, c), (x, y, 1 - c)
        chips = [(1 - x, y), (x, 1 - y), (1 - x, 1 - y)]

        def copy(a, k, block, to, src=None):
            px, py, pc = block
            slot = out_refs[a].at[4 * px + 2 * py + pc]
            return pltpu.make_async_remote_copy(
                src_ref=slot if src is None else src, dst_ref=slot,
                send_sem=send_sems.at[7 * a + k], recv_sem=recv_sems.at[7 * a + k], device_id=to, device_id_type=MESH)

        mine, first, passed = [], [], []
        for a in range(n_arr):
            mine.append(pltpu.make_async_copy(x_refs[a], out_refs[a].at[4 * x + 2 * y + c], local_sems.at[a]))
            mine[-1].start()
            first.append(copy(a, 0, me, sibling, src=x_refs[a]))
            first += [copy(a, 1 + j, me, (*chip, c), src=x_refs[a]) for j, chip in enumerate(chips)]
        for cp in first:
            cp.start()
        for j, chip in enumerate(chips):
            for a in range(n_arr):
                copy(a, 1 + j, (*chip, c), me).wait_recv()
                passed.append(copy(a, 4 + j, (*chip, c), sibling))
                passed[-1].start()
        for a in range(n_arr):
            copy(a, 0, sibling, me).wait_recv()
            for j, chip in enumerate(chips):
                copy(a, 4 + j, (*chip, 1 - c), me).wait_recv()
        for cp in first + passed:
            cp.wait_send()
        for cp in mine:
            cp.wait()

    spec = pl.BlockSpec(memory_space=pltpu.VMEM) if in_vmem else ANY
    return pl.pallas_call(
        body, name=name, out_shape=[jax.ShapeDtypeStruct((N_DEV,) + s.shape, s.dtype) for s in shards],
        in_specs=[spec] * n_arr, out_specs=[spec] * n_arr,
        scratch_shapes=[pltpu.SemaphoreType.DMA((7 * n_arr,)), pltpu.SemaphoreType.DMA((7 * n_arr,)),
                        pltpu.SemaphoreType.DMA((n_arr,))],
    )(*shards)


def _swap_sibling(sends):
    n_arr = len(sends)

    def body(*refs):
        s_refs, o_refs = refs[:n_arr], refs[n_arr:2 * n_arr]
        send_sems, recv_sems = refs[2 * n_arr:]
        x, y, c = lax.axis_index("x"), lax.axis_index("y"), lax.axis_index("c")
        cps = [pltpu.make_async_remote_copy(
            src_ref=s_refs[a].at[:, 1 - c], dst_ref=o_refs[a], send_sem=send_sems.at[a], recv_sem=recv_sems.at[a],
            device_id=(x, y, 1 - c), device_id_type=MESH) for a in range(n_arr)]
        for cp in cps:
            cp.start()
        for cp in cps:
            cp.wait_recv()
        for cp in cps:
            cp.wait_send()

    return pl.pallas_call(
        body, name="swap_sibling",
        out_shape=[jax.ShapeDtypeStruct((s.shape[0],) + s.shape[2:], s.dtype) for s in sends],
        in_specs=[ANY] * n_arr, out_specs=[ANY] * n_arr,
        scratch_shapes=[pltpu.SemaphoreType.DMA((n_arr,)), pltpu.SemaphoreType.DMA((n_arr,))])(*sends)


def _exchange_chips(bufs):
    n_arr = len(bufs)

    def body(*refs):
        b_refs, o_refs = refs[:n_arr], refs[n_arr:2 * n_arr]
        send_sems, recv_sems, local_sems = refs[2 * n_arr:]
        x, y, c = lax.axis_index("x"), lax.axis_index("y"), lax.axis_index("c")
        mine = 2 * x + y
        peers = [(x, 1 - y), (1 - x, y), (1 - x, 1 - y)]

        def copy(a, k, src_slot, dst_slot):
            px, py = peers[k]
            return pltpu.make_async_remote_copy(
                src_ref=b_refs[a].at[src_slot], dst_ref=o_refs[a].at[dst_slot],
                send_sem=send_sems.at[3 * a + k], recv_sem=recv_sems.at[3 * a + k],
                device_id=(px, py, c), device_id_type=MESH)

        own = [pltpu.make_async_copy(b_refs[a].at[mine], o_refs[a].at[mine], local_sems.at[a]) for a in range(n_arr)]
        sent = [copy(a, k, 2 * px + py, mine) for a in range(n_arr) for k, (px, py) in enumerate(peers)]
        for cp in own + sent:
            cp.start()
        for a in range(n_arr):
            for k, (px, py) in enumerate(peers):
                copy(a, k, mine, 2 * px + py).wait_recv()
        for cp in sent:
            cp.wait_send()
        for cp in own:
            cp.wait()

    return pl.pallas_call(
        body, name="exchange_chips", out_shape=[jax.ShapeDtypeStruct(b.shape, b.dtype) for b in bufs],
        in_specs=[ANY] * n_arr, out_specs=[ANY] * n_arr,
        scratch_shapes=[pltpu.SemaphoreType.DMA((3 * n_arr,)), pltpu.SemaphoreType.DMA((3 * n_arr,)),
                        pltpu.SemaphoreType.DMA((n_arr,))])(*bufs)


HBM = pl.BlockSpec(memory_space=pltpu.HBM)
SEM = pl.BlockSpec(memory_space=pltpu.SEMAPHORE)
EFFECT = pltpu.SideEffectType.DATAFLOW_SIDE_EFFECTING


def _spread_copies(srcs, lands, send_sems, recv_sems, local_sems, scatter):
    x, y, c = lax.axis_index("x"), lax.axis_index("y"), lax.axis_index("c")
    me = 4 * x + 2 * y + c
    local, remote = [], []
    for a in range(len(srcs)):
        own = srcs[a].at[me] if scatter else srcs[a]
        local.append(pltpu.make_async_copy(own, lands[a].at[me], local_sems.at[a]))
        for r in range(1, N_DEV):
            px, py, pc = (1 - x if r & 4 else x), (1 - y if r & 2 else y), (1 - c if r & 1 else c)
            peer = 4 * px + 2 * py + pc
            k = (N_DEV - 1) * a + r - 1
            mk = functools.partial(pltpu.make_async_remote_copy, send_sem=send_sems.at[k], recv_sem=recv_sems.at[k],
                                   device_id=(px, py, pc), device_id_type=MESH)
            remote.append((mk(src_ref=srcs[a].at[peer] if scatter else srcs[a], dst_ref=lands[a].at[me]),
                           mk(src_ref=own, dst_ref=lands[a].at[peer])))
    return local, remote


def _spread_start(srcs, *, scatter, name, after=()):
    n, k = len(srcs), len(after)
    lands = [jax.ShapeDtypeStruct((N_DEV,) + s.shape[-2:], s.dtype) for s in srcs]

    def body(*refs):
        src, land = refs[:n], refs[n:2 * n]
        send_sems, recv_sems, local_sems = refs[2 * n + k:2 * n + k + 3]
        token = refs[-1]
        local, remote = _spread_copies(src, land, send_sems, recv_sems, local_sems, scatter)
        for cp in local:
            cp.start()
        for cp, _ in remote:
            cp.start()
        token[...] = jnp.zeros_like(token)

    nsem = (N_DEV - 1) * n
    out = pl.pallas_call(
        body, name=name,
        out_shape=(pltpu.SemaphoreType.DMA((nsem,)), pltpu.SemaphoreType.DMA((nsem,)), pltpu.SemaphoreType.DMA((n,)),
                   *[pltpu.HBM(s.shape, s.dtype) for s in srcs], *[pltpu.HBM(s.shape, s.dtype) for s in lands],
                   jax.ShapeDtypeStruct((8, 128), f32)),
        in_specs=[HBM] * (2 * n) + [ANY] * k,
        out_specs=(SEM, SEM, SEM, *[HBM] * (2 * n), pl.BlockSpec(memory_space=pltpu.VMEM)),
        input_output_aliases={i: 3 + i for i in range(2 * n)},
        compiler_params=pltpu.CompilerParams(has_side_effects=EFFECT),
    )(*[pltpu.with_memory_space_constraint(s, pltpu.HBM) for s in srcs],
      *[pltpu.with_memory_space_constraint(lax.empty(s.shape, s.dtype), pltpu.HBM) for s in lands], *after)
    return out[:-1], out[-1]


def _spread_wait(state, after, *, scatter, name):
    n = (len(state) - 3) // 2
    sems, thru = state[:3], state[3:]

    def body(*refs):
        src, land = refs[:n], refs[n:2 * n]
        send_sems, recv_sems, local_sems = refs[2 * n:2 * n + 3]
        local, remote = _spread_copies(src, land, send_sems, recv_sems, local_sems, scatter)
        for sent, received in remote:
            sent.wait_send()
            received.wait_recv()
        for cp in local:
            cp.wait()

    out = pl.pallas_call(
        body, name=name,
        out_shape=tuple(pltpu.HBM(t.shape, t.dtype) for t in thru),
        in_specs=[HBM] * (2 * n) + [SEM] * 3 + [ANY], out_specs=tuple([HBM] * (2 * n)),
        input_output_aliases={i: i for i in range(2 * n)},
        compiler_params=pltpu.CompilerParams(has_side_effects=EFFECT),
    )(*thru, *sems, after)
    return list(out[n:])


def _row_tile(r, c):
    if r * c <= 512 * 1024:
        return r
    t = r
    while t * c > 512 * 1024 and t % 2 == 0 and (t // 2) % 16 == 0:
        t //= 2
    return t


def _add_own_half(send, recv, c):
    nb, _, r, n = send.shape
    tr = _row_tile(r, n)

    def body(c_ref, s_ref, r_ref, o_ref):
        o_ref[...] = (s_ref[...].astype(f32) + r_ref[...].astype(f32)).astype(o_ref.dtype)

    return pl.pallas_call(
        body, name="add_own_half",
        grid_spec=pltpu.PrefetchScalarGridSpec(
            num_scalar_prefetch=1, grid=(nb, r // tr),
            in_specs=[pl.BlockSpec((None, None, tr, n), lambda j, i, cr: (j, cr[0], i, 0)),
                      pl.BlockSpec((None, tr, n), lambda j, i, cr: (j, i, 0))],
            out_specs=pl.BlockSpec((None, tr, n), lambda j, i, cr: (j, i, 0))),
        out_shape=jax.ShapeDtypeStruct((nb, r, n), bf16),
        compiler_params=_cp(("parallel", "parallel")))(jnp.reshape(c, (1,)).astype(jnp.int32), send, recv)


def _sum_slots(buf, *, tr, name):
    nb, r, n = buf.shape

    def body(b_ref, o_ref):
        acc = b_ref[0].astype(f32)
        for k in range(1, nb):
            acc = acc + b_ref[k].astype(f32)
        o_ref[...] = acc

    return pl.pallas_call(
        body, name=name, grid=(r // tr,),
        in_specs=[pl.BlockSpec((nb, tr, n), lambda i: (0, i, 0))],
        out_specs=pl.BlockSpec((tr, n), lambda i: (i, 0)),
        out_shape=jax.ShapeDtypeStruct((r, n), f32), compiler_params=_cp(("parallel",)))(buf)


def _ada_fwd(c_all, ada_w, ada_b_cols):
    n = ada_w.shape[-1]

    def body(c_ref, w_ref, b_ref, o_ref):
        cv = c_ref[...]
        ca = (cv * _sigmoid(cv)).astype(bf16)
        o_ref[...] = _dot(ca, w_ref[...].astype(bf16)) + b_ref[...]

    return pl.pallas_call(
        body, name="ada_fwd", grid=(DEPTH,),
        in_specs=[_const_spec((N_DEV, D)), pl.BlockSpec((None, D, n), lambda l: (l, 0, 0)),
                  pl.BlockSpec((None, 1, n), lambda l: (l, 0, 0))],
        out_specs=pl.BlockSpec((None, N_DEV, n), lambda l: (l, 0, 0)),
        out_shape=jax.ShapeDtypeStruct((DEPTH, N_DEV, n), f32), compiler_params=_cp(("parallel",)))(c_all, ada_w, ada_b_cols)


def _ada_bwd(c_all, dmod_cols):
    n = dmod_cols.shape[-1]

    def body(c_ref, d_ref, o_ref):
        cv = c_ref[...]
        ca = (cv * _sigmoid(cv)).astype(bf16)
        o_ref[...] = _dot_tn(ca, d_ref[...].astype(bf16))

    return pl.pallas_call(
        body, name="ada_bwd", grid=(DEPTH,),
        in_specs=[_const_spec((N_DEV, D)), pl.BlockSpec((None, N_DEV, n), lambda l: (l, 0, 0))],
        out_specs=pl.BlockSpec((None, D, n), lambda l: (l, 0, 0)),
        out_shape=jax.ShapeDtypeStruct((DEPTH, D, n), f32), compiler_params=_cp(("parallel",)))(c_all, dmod_cols)


def _adamw(w, g, m, v, *, tr, name):
    r, n = w.shape

    def body(w_ref, g_ref, m_ref, v_ref, d_ref, mo_ref, vo_ref):
        g_ = g_ref[...]
        m_ = ADAM_B1 * m_ref[...] + (1.0 - ADAM_B1) * g_
        v_ = ADAM_B2 * v_ref[...] + (1.0 - ADAM_B2) * jnp.square(g_)
        m_hat = m_ / (1.0 - ADAM_B1 ** ADAM_STEP)
        v_hat = v_ / (1.0 - ADAM_B2 ** ADAM_STEP)
        d_ref[...] = -ADAM_LR * (m_hat / (jnp.sqrt(v_hat) + ADAM_EPS) + ADAM_WD * w_ref[...])
        mo_ref[...] = m_
        vo_ref[...] = v_

    blk = pl.BlockSpec((tr, n), lambda i: (i, 0))
    return pl.pallas_call(
        body, name=name, grid=(r // tr,), in_specs=[blk] * 4, out_specs=[blk] * 3,
        out_shape=[jax.ShapeDtypeStruct((r, n), f32)] * 3, compiler_params=_cp(("parallel",)))(w, g, m, v)


def _adamw_landed(w, m, v, land0, land1, *, tr, name):
    _, r, n = w.shape
    nrt = r // tr

    def body(w_ref, m_ref, v_ref, l0_ref, l1_ref, g_ref, d_ref, mo_ref, vo_ref):
        layer = pl.program_id(0)

        def total(ref):
            acc = ref[0].astype(f32)
            for k in range(1, N_DEV):
                acc = acc + ref[k].astype(f32)
            return acc

        @pl.when(layer == 0)
        def _():
            g_ref[...] = total(l0_ref)

        @pl.when(layer == 1)
        def _():
            g_ref[...] = total(l1_ref)
        g_ = g_ref[...]
        m_ = ADAM_B1 * m_ref[...] + (1.0 - ADAM_B1) * g_
        v_ = ADAM_B2 * v_ref[...] + (1.0 - ADAM_B2) * jnp.square(g_)
        m_hat = m_ / (1.0 - ADAM_B1 ** ADAM_STEP)
        v_hat = v_ / (1.0 - ADAM_B2 ** ADAM_STEP)
        d_ref[...] = -ADAM_LR * (m_hat / (jnp.sqrt(v_hat) + ADAM_EPS) + ADAM_WD * w_ref[...])
        mo_ref[...] = m_
        vo_ref[...] = v_

    blk = pl.BlockSpec((None, tr, n), lambda l, i: (l, i, 0))
    land0_spec = pl.BlockSpec((N_DEV, tr, n), lambda l, i: (0, jnp.where(l == 0, i, nrt - 1), 0))
    land1_spec = pl.BlockSpec((N_DEV, tr, n), lambda l, i: (0, jnp.where(l == 1, i, 0), 0))
    return pl.pallas_call(
        body, name=name, grid=(DEPTH, nrt), in_specs=[blk] * 3 + [land0_spec, land1_spec], out_specs=[blk] * 4,
        out_shape=[jax.ShapeDtypeStruct(w.shape, f32)] * 4,
        compiler_params=_cp(("arbitrary", "arbitrary")))(w, m, v, land0, land1)


WEIGHTS = ["ln_in_g", "ln_in_b", "ada_w", "ada_b", "ffn1_w_in", "ffn1_w_out", "mix_w_in", "mix_w_out", "ssd_conv_w",
           "ssd_conv_b", "ssd_dt_bias", "ssd_a_log", "ssd_d", "ssd_norm_g", "fox_f_bias", "sconv_w", "ffn2_w_in",
           "ffn2_w_out", "ln_g", "ln_b"]
BIG = ["ffn1_w_in", "ffn1_w_out", "ffn2_w_in", "ffn2_w_out", "mix_w_in", "mix_w_out"]
GROUPS = {"ffn1": ["ffn1_w_in", "ffn1_w_out"], "mix": ["mix_w_in", "mix_w_out"], "ffn2": ["ffn2_w_in", "ffn2_w_out"]}
COL_SHARDED = ("ffn1_w_in", "ffn2_w_in")
SMALL_SHARDED = {"ssd_conv_w": 128, "sconv_w": 32, "ln_g": 128, "ln_b": 128}
ADAM_TR = {"ada_w": 256, "ffn1_w_in": 512, "ffn2_w_in": 512, "ffn1_w_out": 352, "ffn2_w_out": 352, "mix_w_in": 64,
           "mix_w_out": 256}
LAND_TR = {"ffn1_w_in": 128, "ffn2_w_in": 128, "ffn1_w_out": 176, "ffn2_w_out": 176, "mix_w_in": 32, "mix_w_out": 128}


def _pad_rows(v, mult=128):
    v = v.reshape(-1)
    return jnp.pad(v, (0, (-v.shape[0]) % mult))


def _pack_rows(parts, row_mult=8):
    flat = [_pad_rows(p.astype(f32)) for p in parts]
    offs, o = [], 0
    for f in flat:
        offs.append(o)
        o += f.shape[0] // 128
    buf = jnp.concatenate(flat).reshape(-1, 128)
    return jnp.pad(buf, ((0, (-buf.shape[0]) % row_mult), (0, 0))), offs


def _take(buf, off, shape):
    n = 1
    for s in shape:
        n *= s
    rows = -(-n // 128)
    lead = buf.shape[:-2]
    flat = buf[..., off:off + rows, :].reshape(lead + (rows * 128,))
    return flat[..., :n].reshape(lead + tuple(shape))


def _reduce_scatter_layer(g_layer, ci):
    sends = [t.reshape((4, 2) + t.shape[-2:]) for t in g_layer]
    got = _swap_sibling(sends)
    chip_sums = [_add_own_half(s, r, ci) for s, r in zip(sends, got)]
    got = _exchange_chips(chip_sums)
    return [_sum_slots(t, tr=_row_tile(*t.shape[-2:]), name="sum_chips") for t in got]


def kernel(*args):
    names = (["x", "c"] + WEIGHTS + ["loss_target"] + ["m_" + n for n in WEIGHTS] + ["v_" + n for n in WEIGHTS])
    assert len(args) == len(names)
    a = dict(zip(names, args))
    xi, yi, ci = lax.axis_index("x"), lax.axis_index("y"), lax.axis_index("c")
    me = 4 * xi + 2 * yi + ci

    small_in = [a["c"], a["ln_g"], a["ln_b"], a["ssd_conv_w"], a["sconv_w"]]
    buf, offs = _pack_rows(small_in)
    got, = _all_gather([buf], in_vmem=True, name="gather_small")
    c_all = _take(got, offs[0], (D,))
    full = {}
    for k, n in enumerate(["ln_g", "ln_b", "ssd_conv_w", "sconv_w"]):
        sh = a[n].shape
        t = _take(got, offs[k + 1], sh)
        full[n] = jnp.transpose(t, (1, 2, 0, 3)).reshape(sh[0], sh[1], N_DEV * sh[2])

    ncol = a["ada_w"].shape[-1]
    ada_b_cols = lax.dynamic_slice_in_dim(a["ada_b"], me * ncol, ncol, axis=1)[:, None, :]
    mod_cols = _ada_fwd(c_all, a["ada_w"], ada_b_cols)
    got, = _all_gather([mod_cols.reshape(DEPTH * N_DEV, ncol)], in_vmem=True, name="gather_mod")
    got = got.reshape(N_DEV, DEPTH, N_DEV, ncol)
    mod = lax.dynamic_index_in_dim(got, me, axis=2, keepdims=False)
    mod = jnp.transpose(mod, (1, 0, 2)).reshape(DEPTH, 9, D)

    shards = {n: a[n] for n in BIG}
    shards["mix_w_in"] = _pack_cols(a["mix_w_in"])
    def as_weights(names, got):
        return {n: (t if n in COL_SHARDED else t.reshape(-1, t.shape[-1])) for n, t in zip(names, got)}

    def group_shards(l, gn):
        return [shards[n][l].astype(bf16) for n in GROUPS[gn]]

    w_first = _all_gather(group_shards(0, "ffn1"), in_vmem=False, name="gather_weights")
    w_state, after = {}, [w_first[0]]
    for l, gn in [(0, "mix"), (0, "ffn2"), (1, "ffn1"), (1, "mix"), (1, "ffn2")]:
        w_state[l, gn], token = _spread_start(group_shards(l, gn), scatter=False, after=after,
                                              name=f"weights_{l}{gn}_start")
        after = [token]

    def weights_of(l, gn, after):
        if (l, gn) == (0, "ffn1"):
            return as_weights(GROUPS[gn], w_first)
        return as_weights(GROUPS[gn], _spread_wait(w_state[l, gn], after, scatter=False, name=f"weights_{l}{gn}_wait"))

    g_state = {}

    def grads_done(l, gn, gw):
        srcs = [t.reshape((N_DEV,) + shards[n].shape[1:]) for n, t in gw.items()]
        g_state[l, gn], token = _spread_start(srcs, scatter=True, name=f"grads_{l}{gn}_start")
        return token

    p = {n: a[n] for n in ("ln_in_g", "ln_in_b", "ssd_conv_b", "ssd_dt_bias", "ssd_a_log", "ssd_d", "ssd_norm_g",
                           "fox_f_bias")}
    p.update(full)
    p["mod"] = mod + after[0][0, 0]

    loss_local, grad_x, g = _local_step(a["x"][0], a["loss_target"][0], p, weights_of, grads_done)
    loss = lax.psum(loss_local, ("x", "y", "c"))

    small_names = ["mod", "ln_in_g", "ln_in_b", "ssd_conv_b", "ssd_dt_bias", "ssd_a_log", "ssd_d", "ssd_norm_g",
                   "fox_f_bias", "ln_g", "ln_b", "ssd_conv_w", "sconv_w"]
    buf, offs = _pack_rows([g[n] for n in small_names])
    got, = _all_gather([buf], in_vmem=True, name="gather_small_grads")
    tot = _sum_slots(got, tr=buf.shape[0], name="sum_small_grads")
    grads = {}
    for k, n in enumerate(small_names[1:], start=1):
        t = _take(tot, offs[k], g[n].shape)
        if n in SMALL_SHARDED:
            w_ = SMALL_SHARDED[n]
            t = lax.dynamic_slice_in_dim(t, me * w_, w_, axis=2)
        grads[n] = t
    grads["ada_b"] = _take(tot, offs[0], (DEPTH, 9 * D))
    dmod_all = _take(got, offs[0], (DEPTH, 9 * D))
    dmod_cols = jnp.transpose(lax.dynamic_slice_in_dim(dmod_all, me * ncol, ncol, axis=2), (1, 0, 2))
    grads["ada_w"] = _ada_bwd(c_all, dmod_cols)

    delta, new_m, new_v = {}, {}, {}

    def adamw(n):
        sh = a[n].shape
        two = lambda t: t.reshape(-1, sh[-1])
        outs = _adamw(two(a[n]), two(grads[n]), two(a["m_" + n]), two(a["v_" + n]), tr=ADAM_TR[n], name="adamw_" + n)
        delta[n], new_m[n], new_v[n] = (t.reshape(sh) for t in outs)

    small_params = [n for n in WEIGHTS if n not in ADAM_TR]
    packs = [_pack_rows([src[pre + n] for n in small_params])[0]
             for src, pre in ((a, ""), (grads, ""), (a, "m_"), (a, "v_"))]
    _, offs = _pack_rows([a[n] for n in small_params])
    outs = _adamw(*packs, tr=packs[0].shape[0], name="adamw_small")
    for k, n in enumerate(small_params):
        delta[n], new_m[n], new_v[n] = (_take(t, offs[k], a[n].shape) for t in outs)
    adamw("ada_w")
    after = grad_x
    for gn, names in [("ffn2_w_out", ["ffn2_w_out"]), ("ffn2_w_in", ["ffn2_w_in"]), ("mix", GROUPS["mix"]),
                      ("ffn1_w_out", ["ffn1_w_out"]), ("ffn1_w_in", ["ffn1_w_in"])]:
        lands = [_spread_wait(g_state[l, gn], after if l == 0 else grad_x, scatter=True, name=f"grads_{l}{gn}_wait")
                 for l in range(DEPTH)]
        for k, n in enumerate(names):
            if n == "mix_w_in":
                sums = [_sum_slots(t[k], tr=LAND_TR[n], name="sum_devices") for t in lands]
                grads[n] = _unpack_cols(jnp.stack(sums))
                adamw(n)
            else:
                grads[n], delta[n], new_m[n], new_v[n] = _adamw_landed(
                    a[n], a["m_" + n], a["v_" + n], lands[0][k], lands[1][k], tr=LAND_TR[n], name="adamw_" + n)
            after = delta[n]

    return (loss, grad_x[None], *[grads[n] for n in WEIGHTS], *[delta[n] for n in WEIGHTS],
            *[new_m[n] for n in WEIGHTS], *[new_v[n] for n in WEIGHTS])
```

```python
import functools

import jax
import jax.numpy as jnp
from jax import lax
from jax.experimental import pallas as pl
from jax.experimental.pallas import tpu as pltpu

f32, bf16 = jnp.float32, jnp.bfloat16

D = 1024
F = 2816
DEPTH = 2
N_DEV = 8
SSD_W, SSD_HD, SSD_H, SSD_G, SSD_N, SSD_K = 512, 64, 8, 2, 128, 4
FOX_W, FOX_HD, FOX_H = 256, 64, 4
SC_W, SC_K = 256, 3
ALPHA = (2 * DEPTH) ** 0.25
LN_EPS = 1e-5
RMS_EPS = 1e-5
P_XBC, P_Z, P_Q, P_K, P_V, P_SB, P_SC, P_SX, P_SM = 0, 1024, 1536, 1792, 2048, 2304, 2560, 2816, 3072
P_W = 3200
SM_DT, SM_F = 0, 8
ADAM_LR, ADAM_B1, ADAM_B2, ADAM_EPS, ADAM_WD, ADAM_STEP = 0.001, 0.9, 0.999, 1e-08, 0.01, 10

VMEM_LIMIT = 56 * 1024 * 1024


def _cp(sem=None):
    return pltpu.CompilerParams(dimension_semantics=sem, vmem_limit_bytes=VMEM_LIMIT)


def _const_spec(shape):
    nd = len(shape)
    return pl.BlockSpec(shape, lambda *_: (0,) * nd, pipeline_mode=pl.Buffered(1))


def _sigmoid(x):
    return 1.0 / (1.0 + jnp.exp(-x))


def _ln_fwd(u, g, b):
    mu = jnp.mean(u, -1, keepdims=True)
    xc = u - mu
    rstd = lax.rsqrt(jnp.mean(xc * xc, -1, keepdims=True) + LN_EPS)
    xhat = xc * rstd
    return xhat * g + b, xhat, rstd


def _ln_bwd(dout, xhat, rstd, g):
    dxh = dout * g
    m1 = jnp.mean(dxh, -1, keepdims=True)
    m2 = jnp.mean(dxh * xhat, -1, keepdims=True)
    du = rstd * (dxh - m1 - xhat * m2)
    return du, jnp.sum(dout * xhat, 0, keepdims=True), jnp.sum(dout, 0, keepdims=True)


def _dot(a, b):
    return jnp.dot(a, b, preferred_element_type=f32)


def _dot_nt(a, b):
    return lax.dot_general(a, b, (((1,), (1,)), ((), ())), preferred_element_type=f32)


def _dot_tn(a, b):
    return lax.dot_general(a, b, (((0,), (0,)), ((), ())), preferred_element_type=f32)


def _dot_hi(a, b):
    return jnp.dot(a, b, preferred_element_type=f32, precision=lax.Precision.HIGHEST)


def _shift_down(cur, prev, s):
    if s == 0:
        return cur
    row = lax.broadcasted_iota(jnp.int32, cur.shape, 0)
    return jnp.where(row < s, pltpu.roll(prev, s, 0), pltpu.roll(cur, s, 0))


def _shift_up(cur, nxt, s):
    if s == 0:
        return cur
    t = cur.shape[0]
    row = lax.broadcasted_iota(jnp.int32, cur.shape, 0)
    return jnp.where(row >= t - s, pltpu.roll(nxt, t - s, 0), pltpu.roll(cur, t - s, 0))


def _ln_in_fwd(x, gb, *, tt=512):
    L = x.shape[0]

    def body(x_ref, gb_ref, o_ref):
        o_ref[...] = _ln_fwd(x_ref[...], gb_ref[0:1, :], gb_ref[1:2, :])[0]

    return pl.pallas_call(
        body, name="ln_in_fwd", grid=(L // tt,),
        in_specs=[pl.BlockSpec((tt, D), lambda i: (i, 0)), _const_spec((8, D))],
        out_specs=pl.BlockSpec((tt, D), lambda i: (i, 0)),
        out_shape=jax.ShapeDtypeStruct((L, D), f32), compiler_params=_cp(("parallel",)))(x, gb)


def _ln_in_bwd(x, dy, gb, *, tt=512):
    L = x.shape[0]

    def body(x_ref, dy_ref, gb_ref, dx_ref, acc_ref):
        @pl.when(pl.program_id(0) == 0)
        def _():
            acc_ref[...] = jnp.zeros_like(acc_ref)
        _, xhat, rstd = _ln_fwd(x_ref[...], gb_ref[0:1, :], gb_ref[1:2, :])
        du, dg, db = _ln_bwd(dy_ref[...], xhat, rstd, gb_ref[0:1, :])
        dx_ref[...] = du
        acc_ref[0:1, :] += dg
        acc_ref[1:2, :] += db

    return pl.pallas_call(
        body, name="ln_in_bwd", grid=(L // tt,),
        in_specs=[pl.BlockSpec((tt, D), lambda i: (i, 0)), pl.BlockSpec((tt, D), lambda i: (i, 0)), _const_spec((8, D))],
        out_specs=[pl.BlockSpec((tt, D), lambda i: (i, 0)), pl.BlockSpec((8, D), lambda i: (0, 0))],
        out_shape=[jax.ShapeDtypeStruct((L, D), f32), jax.ShapeDtypeStruct((8, D), f32)],
        compiler_params=_cp(("arbitrary",)))(x, dy, gb)


def _loss_head(y, tgt, *, tt=512):
    L = y.shape[0]

    def body(y_ref, t_ref, dy_ref, acc_ref):
        @pl.when(pl.program_id(0) == 0)
        def _():
            acc_ref[...] = jnp.zeros_like(acc_ref)
        e = y_ref[...] - t_ref[...]
        dy_ref[...] = e * (1.0 / D)
        acc_ref[...] += 0.5 * jnp.sum(jnp.mean(e * e, -1, keepdims=True))

    return pl.pallas_call(
        body, name="loss_head", grid=(L // tt,),
        in_specs=[pl.BlockSpec((tt, D), lambda i: (i, 0)), pl.BlockSpec((tt, D), lambda i: (i, 0))],
        out_specs=[pl.BlockSpec((tt, D), lambda i: (i, 0)), pl.BlockSpec((8, 128), lambda i: (0, 0))],
        out_shape=[jax.ShapeDtypeStruct((L, D), f32), jax.ShapeDtypeStruct((8, 128), f32)],
        compiler_params=_cp(("arbitrary",)))(y, tgt)


FFN_CH = 4
FS = F // FFN_CH


def _ffn_fwd(x, mv, w_in, w_out, *, tt=256):
    L = x.shape[0]

    def body(x_ref, mv_ref, wi_ref, wo_ref, o_ref):
        x = x_ref[...]
        h = (x * (1.0 + mv_ref[1:2, :]) + mv_ref[0:1, :]).astype(bf16)
        y = jnp.zeros((tt, D), f32)
        for c in range(FFN_CH):
            g = _dot(h, wi_ref[c])
            u = _dot(h, wi_ref[c + FFN_CH])
            act = (g * _sigmoid(g) * u).astype(bf16)
            y = y + _dot(act, wo_ref[c * FS:(c + 1) * FS, :])
        uu = ALPHA * x + (0.5 * mv_ref[2:3, :]) * y
        o_ref[...] = _ln_fwd(uu, mv_ref[3:4, :], mv_ref[4:5, :])[0]

    return pl.pallas_call(
        body, name="ffn_fwd", grid=(L // tt,),
        in_specs=[pl.BlockSpec((tt, D), lambda i: (i, 0)), _const_spec((8, D)),
                  _const_spec((2 * FFN_CH, D, FS)), _const_spec((F, D))],
        out_specs=pl.BlockSpec((tt, D), lambda i: (i, 0)),
        out_shape=jax.ShapeDtypeStruct((L, D), f32), compiler_params=_cp(("parallel",)))(x, mv, w_in, w_out)


def _ffn_bwd(x, dxo, mv, w_in, w_out, *, tt=256):
    L = x.shape[0]

    def body(x_ref, dxo_ref, mv_ref, wi_ref, wo_ref, dx_ref, h_ref, da_ref, act_ref, dy_ref, acc_ref, a_scr):
        @pl.when(pl.program_id(0) == 0)
        def _():
            acc_ref[...] = jnp.zeros_like(acc_ref)
        x = x_ref[...]
        scale1 = 1.0 + mv_ref[1:2, :]
        h = (x * scale1 + mv_ref[0:1, :]).astype(bf16)
        h_ref[...] = h.T
        y = jnp.zeros((tt, D), f32)
        for c in range(FFN_CH):
            g = _dot(h, wi_ref[c])
            u = _dot(h, wi_ref[c + FFN_CH])
            a_scr[c] = g
            a_scr[c + FFN_CH] = u
            act = (g * _sigmoid(g) * u).astype(bf16)
            act_ref[c] = act.T
            y = y + _dot(act, wo_ref[c * FS:(c + 1) * FS, :])
        hg = 0.5 * mv_ref[2:3, :]
        _, xhat, rstd = _ln_fwd(ALPHA * x + hg * y, mv_ref[3:4, :], mv_ref[4:5, :])
        du, dlg, dlb = _ln_bwd(dxo_ref[...], xhat, rstd, mv_ref[3:4, :])
        acc_ref[3:4, :] += dlg
        acc_ref[4:5, :] += dlb
        acc_ref[2:3, :] += jnp.sum(0.5 * y * du, 0, keepdims=True)
        dyb = (hg * du).astype(bf16)
        dy_ref[...] = dyb
        dh = jnp.zeros((tt, D), f32)
        for c in range(FFN_CH):
            g = a_scr[c]
            u = a_scr[c + FFN_CH]
            dact = _dot_nt(dyb, wo_ref[c * FS:(c + 1) * FS, :])
            s = _sigmoid(g)
            dg = (dact * u * (s * (1.0 + g * (1.0 - s)))).astype(bf16)
            dup = (dact * (g * s)).astype(bf16)
            da_ref[c] = dg
            da_ref[c + FFN_CH] = dup
            dh = dh + _dot_nt(dg, wi_ref[c])
            dh = dh + _dot_nt(dup, wi_ref[c + FFN_CH])
        dx_ref[...] = ALPHA * du + dh * scale1
        acc_ref[0:1, :] += jnp.sum(dh, 0, keepdims=True)
        acc_ref[1:2, :] += jnp.sum(dh * x, 0, keepdims=True)

    tok = lambda w: pl.BlockSpec((tt, w), lambda i: (i, 0))
    by_chunk = lambda n: pl.BlockSpec((n, tt, FS), lambda i: (0, i, 0))
    return pl.pallas_call(
        body, name="ffn_bwd", grid=(L // tt,),
        in_specs=[tok(D), tok(D), _const_spec((8, D)), _const_spec((2 * FFN_CH, D, FS)), _const_spec((F, D))],
        out_specs=[tok(D), pl.BlockSpec((D, tt), lambda i: (0, i)), by_chunk(2 * FFN_CH),
                   pl.BlockSpec((FFN_CH, FS, tt), lambda i: (0, 0, i)), tok(D), pl.BlockSpec((8, D), lambda i: (0, 0))],
        out_shape=[jax.ShapeDtypeStruct((L, D), f32), jax.ShapeDtypeStruct((D, L), bf16),
                   jax.ShapeDtypeStruct((2 * FFN_CH, L, FS), bf16), jax.ShapeDtypeStruct((FFN_CH, FS, L), bf16),
                   jax.ShapeDtypeStruct((L, D), bf16), jax.ShapeDtypeStruct((8, D), f32)],
        scratch_shapes=[pltpu.VMEM((2 * FFN_CH, tt, FS), f32)],
        compiler_params=_cp(("arbitrary",)))(x, dxo, mv, w_in, w_out)


DW_TK = 4096


def _matmul_tokens(a, b, *, tn, tk, name, after=None):
    ga, gb = a.ndim == 3, b.ndim == 3
    extra = [] if after is None else [after]
    G = a.shape[0] if ga else (b.shape[0] if gb else 1)
    M, K = a.shape[-2:]
    N = b.shape[-1]
    tk = min(tk, K)
    nk = K // tk

    def body(a_ref, b_ref, *rest):
        o_ref, acc = rest[len(extra):]
        k = pl.program_id(2)
        p = _dot(a_ref[...], b_ref[...])

        @pl.when(k == 0)
        def _():
            acc[...] = p

        @pl.when(k > 0)
        def _():
            acc[...] += p

        @pl.when(k == nk - 1)
        def _():
            o_ref[...] = acc[...].astype(bf16)

    a_spec = (pl.BlockSpec((None, M, tk), lambda g, j, k: (g, 0, k)) if ga
              else pl.BlockSpec((M, tk), lambda g, j, k: (0, k)))
    b_spec = (pl.BlockSpec((None, tk, tn), lambda g, j, k: (g, k, j)) if gb
              else pl.BlockSpec((tk, tn), lambda g, j, k: (k, j)))
    if ga or gb:
        o_spec, o_shape = pl.BlockSpec((None, M, tn), lambda g, j, k: (g, 0, j)), (G, M, N)
    else:
        o_spec, o_shape = pl.BlockSpec((M, tn), lambda g, j, k: (0, j)), (M, N)
    return pl.pallas_call(
        body, name=name, grid=(G, N // tn, nk), in_specs=[a_spec, b_spec] + [ANY] * len(extra), out_specs=o_spec,
        out_shape=jax.ShapeDtypeStruct(o_shape, bf16), scratch_shapes=[pltpu.VMEM((M, tn), f32)],
        compiler_params=_cp(("parallel", "parallel", "arbitrary")))(a, b, *extra)


def _inproj_fwd(x, mv, w, *, tt=512):
    L = x.shape[0]

    def body(x_ref, mv_ref, w_ref, o_ref):
        h = (x_ref[...] * (1.0 + mv_ref[1:2, :]) + mv_ref[0:1, :]).astype(bf16)
        o_ref[...] = _dot(h, w_ref[...])

    return pl.pallas_call(
        body, name="inproj_fwd", grid=(L // tt,),
        in_specs=[pl.BlockSpec((tt, D), lambda i: (i, 0)), _const_spec((8, D)), _const_spec((D, P_W))],
        out_specs=pl.BlockSpec((tt, P_W), lambda i: (i, 0)),
        out_shape=jax.ShapeDtypeStruct((L, P_W), f32), compiler_params=_cp(("parallel",)))(x, mv, w)


def _inproj_bwd(x, dx_part, dproj, mv, w, *, tt=512):
    L = x.shape[0]

    def body(x_ref, dxp_ref, dp_ref, mv_ref, w_ref, dx_ref, h_ref, acc_ref):
        @pl.when(pl.program_id(0) == 0)
        def _():
            acc_ref[...] = jnp.zeros_like(acc_ref)
        x = x_ref[...]
        scale1 = 1.0 + mv_ref[1:2, :]
        h_ref[...] = (x * scale1 + mv_ref[0:1, :]).astype(bf16).T
        dh = _dot_nt(dp_ref[...], w_ref[...])
        dx_ref[...] = dxp_ref[...] + dh * scale1
        acc_ref[0:1, :] += jnp.sum(dh, 0, keepdims=True)
        acc_ref[1:2, :] += jnp.sum(dh * x, 0, keepdims=True)

    tok = lambda w_: pl.BlockSpec((tt, w_), lambda i: (i, 0))
    return pl.pallas_call(
        body, name="inproj_bwd", grid=(L // tt,),
        in_specs=[tok(D), tok(D), tok(P_W), _const_spec((8, D)), _const_spec((D, P_W))],
        out_specs=[tok(D), pl.BlockSpec((D, tt), lambda i: (0, i)), pl.BlockSpec((8, D), lambda i: (0, 0))],
        out_shape=[jax.ShapeDtypeStruct((L, D), f32), jax.ShapeDtypeStruct((D, L), bf16),
                   jax.ShapeDtypeStruct((8, D), f32)],
        compiler_params=_cp(("arbitrary",)))(x, dx_part, dproj, mv, w)


def _outproj_fwd(x, ycat, mv, w, *, tt=512):
    L = x.shape[0]

    def body(x_ref, y_ref, mv_ref, w_ref, o_ref):
        y = _dot(y_ref[...], w_ref[...])
        uu = ALPHA * x_ref[...] + mv_ref[2:3, :] * y
        o_ref[...] = _ln_fwd(uu, mv_ref[3:4, :], mv_ref[4:5, :])[0]

    tok = lambda w_: pl.BlockSpec((tt, w_), lambda i: (i, 0))
    return pl.pallas_call(
        body, name="outproj_fwd", grid=(L // tt,),
        in_specs=[tok(D), tok(D), _const_spec((8, D)), _const_spec((D, D))],
        out_specs=tok(D),
        out_shape=jax.ShapeDtypeStruct((L, D), f32), compiler_params=_cp(("parallel",)))(x, ycat, mv, w)


def _outproj_bwd(x, ycat, dxo, mv, w, *, tt=512):
    L = x.shape[0]

    def body(x_ref, y_ref, dxo_ref, mv_ref, w_ref, dx_ref, dy_ref, dyc_ref, yt_ref, acc_ref):
        @pl.when(pl.program_id(0) == 0)
        def _():
            acc_ref[...] = jnp.zeros_like(acc_ref)
        yt_ref[...] = y_ref[...].T
        y = _dot(y_ref[...], w_ref[...])
        gate = mv_ref[2:3, :]
        _, xhat, rstd = _ln_fwd(ALPHA * x_ref[...] + gate * y, mv_ref[3:4, :], mv_ref[4:5, :])
        du, dlg, dlb = _ln_bwd(dxo_ref[...], xhat, rstd, mv_ref[3:4, :])
        acc_ref[3:4, :] += dlg
        acc_ref[4:5, :] += dlb
        acc_ref[2:3, :] += jnp.sum(y * du, 0, keepdims=True)
        dx_ref[...] = ALPHA * du
        dyb = (gate * du).astype(bf16)
        dy_ref[...] = dyb
        dyc_ref[...] = _dot_nt(dyb, w_ref[...])

    tok = lambda w_: pl.BlockSpec((tt, w_), lambda i: (i, 0))
    return pl.pallas_call(
        body, name="outproj_bwd", grid=(L // tt,),
        in_specs=[tok(D), tok(D), tok(D), _const_spec((8, D)), _const_spec((D, D))],
        out_specs=[tok(D), tok(D), tok(D), pl.BlockSpec((D, tt), lambda i: (0, i)), pl.BlockSpec((8, D), lambda i: (0, 0))],
        out_shape=[jax.ShapeDtypeStruct((L, D), f32), jax.ShapeDtypeStruct((L, D), bf16),
                   jax.ShapeDtypeStruct((L, D), f32), jax.ShapeDtypeStruct((D, L), bf16),
                   jax.ShapeDtypeStruct((8, D), f32)],
        compiler_params=_cp(("arbitrary",)))(x, ycat, dxo, mv, w)


def _sconv_fwd(proj, w, *, tt=512):
    L = proj.shape[0]
    cb = SC_W

    def body(b_ref, c_ref, x_ref, cp_ref, xp_ref, w_ref, o_ref):
        first = jnp.where(pl.program_id(0) > 0, 1.0, 0.0)
        u = c_ref[...] * x_ref[...]
        up = cp_ref[...] * xp_ref[...] * first
        v = w_ref[2:3, :] * u + w_ref[1:2, :] * _shift_down(u, up, 1) + w_ref[0:1, :] * _shift_down(u, up, 2)
        o_ref[...] = (b_ref[...] * v).astype(bf16)

    cur = lambda col: pl.BlockSpec((tt, cb), lambda i: (i, col // cb))
    prev = lambda col: pl.BlockSpec((tt, cb), lambda i: (jnp.maximum(i - 1, 0), col // cb))
    return pl.pallas_call(
        body, name="sconv_fwd", grid=(L // tt,),
        in_specs=[cur(P_SB), cur(P_SC), cur(P_SX), prev(P_SC), prev(P_SX), _const_spec((8, cb))],
        out_specs=pl.BlockSpec((tt, cb), lambda i: (i, 0)),
        out_shape=jax.ShapeDtypeStruct((L, cb), bf16), compiler_params=_cp(("parallel",)))(proj, proj, proj, proj, proj, w)


def _sconv_bwd(proj, dycat, w, *, tt=512):
    L = proj.shape[0]
    cb = SC_W
    n = L // tt

    def body(b_ref, c_ref, x_ref, cp_ref, xp_ref, bn_ref, dy_ref, dyn_ref, w_ref, db_ref, dc_ref, dx_ref, acc_ref):
        i = pl.program_id(0)

        @pl.when(i == 0)
        def _():
            acc_ref[...] = jnp.zeros_like(acc_ref)
        first = jnp.where(i > 0, 1.0, 0.0)
        last = jnp.where(i < n - 1, 1.0, 0.0)
        cg, xin, bg = c_ref[...], x_ref[...], b_ref[...]
        u = cg * xin
        up = cp_ref[...] * xp_ref[...] * first
        u1, u2 = _shift_down(u, up, 1), _shift_down(u, up, 2)
        v = w_ref[2:3, :] * u + w_ref[1:2, :] * u1 + w_ref[0:1, :] * u2
        dy = dy_ref[...]
        db_ref[...] = (dy * v).astype(bf16)
        dv = dy * bg
        dvn = dyn_ref[...] * bn_ref[...] * last
        du = w_ref[2:3, :] * dv + w_ref[1:2, :] * _shift_up(dv, dvn, 1) + w_ref[0:1, :] * _shift_up(dv, dvn, 2)
        acc_ref[2:3, :] += jnp.sum(dv * u, 0, keepdims=True)
        acc_ref[1:2, :] += jnp.sum(dv * u1, 0, keepdims=True)
        acc_ref[0:1, :] += jnp.sum(dv * u2, 0, keepdims=True)
        dc_ref[...] = (du * xin).astype(bf16)
        dx_ref[...] = (du * cg).astype(bf16)

    cur = lambda col: pl.BlockSpec((tt, cb), lambda i: (i, col // cb))
    prev = lambda col: pl.BlockSpec((tt, cb), lambda i: (jnp.maximum(i - 1, 0), col // cb))
    nxt = lambda col: pl.BlockSpec((tt, cb), lambda i: (jnp.minimum(i + 1, n - 1), col // cb))
    ycol = SSD_W + FOX_W
    out = pl.BlockSpec((tt, cb), lambda i: (i, 0))
    return pl.pallas_call(
        body, name="sconv_bwd", grid=(n,),
        in_specs=[cur(P_SB), cur(P_SC), cur(P_SX), prev(P_SC), prev(P_SX), nxt(P_SB), cur(ycol), nxt(ycol),
                  _const_spec((8, cb))],
        out_specs=[out, out, out, pl.BlockSpec((8, cb), lambda i: (0, 0))],
        out_shape=[jax.ShapeDtypeStruct((L, cb), bf16)] * 3 + [jax.ShapeDtypeStruct((8, cb), f32)],
        compiler_params=_cp(("arbitrary",)))(proj, proj, proj, proj, proj, proj, dycat, dycat, w)


def _log1pexp(x):
    return jnp.log(1.0 + jnp.exp(-jnp.abs(x)))


NEG = -1e30
FOX_SCALE = FOX_HD ** -0.5


HL = 128
AW = FOX_H * HL


def _np_place(rows, cols, pairs, dtype):
    import numpy as np
    m = np.zeros((rows, cols), np.float32)
    for r, c in pairs:
        m[r, c] = 1.0
    return jnp.asarray(m, dtype)


def _fox_consts():
    data = [(h * FOX_HD + d, h * HL + d) for h in range(FOX_H) for d in range(FOX_HD)]
    return dict(
        pq=_np_place(FOX_W, AW, data, bf16),
        pqt=_np_place(AW, FOX_W, [(c, r) for r, c in data], bf16),
        cum_a=[_np_place(128, AW, [(SM_F + h, h * HL + 64 + r) for h in range(FOX_H)], bf16) for r in range(3)],
        head_a=[_np_place(128, AW, [(h, h * HL + 64 + r) for h in range(FOX_H)], bf16) for r in range(3)],
        head_b=[_np_place(128, AW, [(h, h * HL + 67 + r) for h in range(FOX_H)], bf16) for r in range(3)],
        group=_np_place(FOX_W, 128, [(h * FOX_HD + d, h) for h in range(FOX_H) for d in range(FOX_HD)], f32),
        col_a=_np_place(AW, 128, [(h * HL + 64, SM_F + h) for h in range(FOX_H)], f32))


def _split3(x):
    hi = x.astype(bf16)
    r1 = x - hi.astype(f32)
    mid = r1.astype(bf16)
    return hi, mid, (r1 - mid.astype(f32)).astype(bf16)


def _slot_ones(tt, first):
    lane = lax.broadcasted_iota(jnp.int32, (tt, AW), 1) % HL
    return jnp.where((lane >= first) & (lane < first + 3), 1.0, 0.0)


def _fox_prep(proj, hp, cst, *, tt=256):
    L = proj.shape[0]

    def body(q_ref, k_ref, v_ref, sm_ref, hp_ref, pq_ref, c0_ref, c1_ref, c2_ref, qa_ref, ka_ref, va_ref, tot_ref):
        xx = sm_ref[...] + hp_ref[3:4, :]
        logf = jnp.minimum(xx, 0.0) - _log1pexp(xx)
        r = lax.broadcasted_iota(jnp.int32, (tt, tt), 0)
        c = lax.broadcasted_iota(jnp.int32, (tt, tt), 1)
        cum = _dot_hi(jnp.where(r >= c, 1.0, 0.0), logf)
        tot_ref[...] = jnp.broadcast_to(cum[tt - 1:tt, :], (8, 128))
        parts = _split3(-cum)
        pq = pq_ref[...]
        a_ones, b_ones = _slot_ones(tt, 64), _slot_ones(tt, 67)
        qa_ref[...] = (_dot((q_ref[...] * FOX_SCALE).astype(bf16), pq) + a_ones).astype(bf16)
        ka = _dot(k_ref[...].astype(bf16), pq) + b_ones
        for part, c_ref in zip(parts, (c0_ref, c1_ref, c2_ref)):
            ka = ka + _dot(part, c_ref[...])
        ka_ref[...] = ka.astype(bf16)
        va_ref[...] = (_dot(v_ref[...].astype(bf16), pq) + a_ones).astype(bf16)

    col = lambda c_: pl.BlockSpec((tt, FOX_W), lambda i: (i, c_ // FOX_W))
    out = pl.BlockSpec((tt, AW), lambda i: (i, 0))
    return pl.pallas_call(
        body, name="fox_prep", grid=(L // tt,),
        in_specs=[col(P_Q), col(P_K), col(P_V), pl.BlockSpec((tt, 128), lambda i: (i, P_SM // 128)),
                  _const_spec((8, 128)), _const_spec((FOX_W, AW))] + [_const_spec((128, AW))] * 3,
        out_specs=[out, out, out, pl.BlockSpec((8, 128), lambda i: (i, 0))],
        out_shape=[jax.ShapeDtypeStruct((L, AW), bf16)] * 3 + [jax.ShapeDtypeStruct((8 * (L // tt), 128), f32)],
        compiler_params=_cp(("parallel",)))(proj, proj, proj, proj, hp, cst["pq"], *cst["cum_a"])


def _fox_attn_fwd(qa, ka, va, tot, *, tq=256):
    L = qa.shape[0]

    def body(qa_ref, ka_ref, va_ref, tot_ref, o_ref, oa_ref, lse_ref):
        i = pl.program_id(0)
        row = lax.broadcasted_iota(jnp.int32, (tq, tq), 0)
        col = lax.broadcasted_iota(jnp.int32, (tq, tq), 1)
        diag_bias = jnp.where(row >= col, 0.0, NEG)

        def block(j, carry, t_j, bias):
            r0 = pl.multiple_of(j * tq, tq)
            out = []
            for h in range(FOX_H):
                hl = slice(h * HL, (h + 1) * HL)
                m, acc = carry[h]
                m_in = m - t_j[h]
                s = _dot_nt(qa_ref[:, hl], ka_ref[pl.ds(r0, tq), hl])
                if bias is not None:
                    s = s + bias
                m_new = jnp.maximum(m_in, jnp.max(s, -1, keepdims=True))
                p = jnp.exp(s - m_new)
                al = jnp.exp(m_in - m_new)
                pb = p.astype(bf16)
                p_lo = (p - pb.astype(f32)).astype(bf16)
                vj = va_ref[pl.ds(r0, tq), hl]
                out.append((m_new, al * acc + (_dot(pb, vj) + _dot(p_lo, vj))))
            return tuple(out)

        def step(k, state):
            carry, gap = state
            j = i - 1 - k
            t_j = [tot_ref[j, h] for h in range(FOX_H)]
            return block(j, carry, t_j, None), tuple(g + t for g, t in zip(gap, t_j))

        init = tuple((jnp.full((tq, 1), NEG, f32), jnp.zeros((tq, HL), f32)) for _ in range(FOX_H))
        zero_gap = tuple(jnp.zeros((), f32) for _ in range(FOX_H))
        carry = block(i, init, zero_gap, diag_bias)
        carry, gap = lax.fori_loop(0, i, step, (carry, zero_gap))
        lane = lax.broadcasted_iota(jnp.int32, (tq, 128), 1)
        lse_all = jnp.zeros((tq, 128), f32)
        for h in range(FOX_H):
            hs = slice(h * FOX_HD, (h + 1) * FOX_HD)
            m, acc = carry[h]
            l = acc[:, FOX_HD:FOX_HD + 1]
            o = acc[:, :FOX_HD] * (1.0 / l)
            o_ref[:, hs] = o.astype(bf16)
            oa_ref[:, hs] = o
            lse_all = jnp.where(lane == h, m + gap[h] + jnp.log(l), lse_all)
        lse_ref[...] = lse_all

    full = pl.BlockSpec((L, AW), lambda i: (0, 0), pipeline_mode=pl.Buffered(1))
    return pl.pallas_call(
        body, name="fox_fwd", grid=(L // tq,),
        in_specs=[pl.BlockSpec((tq, AW), lambda i: (i, 0)), full, full, pl.BlockSpec(memory_space=pltpu.SMEM)],
        out_specs=[pl.BlockSpec((tq, FOX_W), lambda i: (i, 0)), pl.BlockSpec((tq, FOX_W), lambda i: (i, 0)),
                   pl.BlockSpec((tq, 128), lambda i: (i, 0))],
        out_shape=[jax.ShapeDtypeStruct((L, FOX_W), bf16), jax.ShapeDtypeStruct((L, FOX_W), f32),
                   jax.ShapeDtypeStruct((L, 128), f32)],
        compiler_params=_cp(("parallel",)))(qa, ka, va, tot)


def _fox_bprep(qa, dycat, o_acc, lse, cst, *, tt=256):
    L = qa.shape[0]

    def body(qa_ref, do_ref, oa_ref, lse_ref, pq_ref, g_ref, a0, a1, a2, b0, b1, b2, qb_ref, doa_ref, qbt_ref, doat_ref):
        dob = do_ref[...].astype(bf16)
        delta = _dot_hi(dob.astype(f32) * oa_ref[...], g_ref[...])
        doa = _dot(dob, pq_ref[...])
        for part, ref in zip(_split3(-delta), (a0, a1, a2)):
            doa = doa + _dot(part, ref[...])
        qb = qa_ref[...].astype(f32)
        for part, ref in zip(_split3(-lse_ref[...]), (b0, b1, b2)):
            qb = qb + _dot(part, ref[...])
        doa, qb = doa.astype(bf16), qb.astype(bf16)
        doa_ref[...] = doa
        qb_ref[...] = qb
        doat_ref[...] = doa.T
        qbt_ref[...] = qb.T

    tok = lambda w_: pl.BlockSpec((tt, w_), lambda i: (i, 0))
    tr = pl.BlockSpec((AW, tt), lambda i: (0, i))
    return pl.pallas_call(
        body, name="fox_bprep", grid=(L // tt,),
        in_specs=[tok(AW), pl.BlockSpec((tt, FOX_W), lambda i: (i, SSD_W // FOX_W)), tok(FOX_W), tok(128),
                  _const_spec((FOX_W, AW)), _const_spec((FOX_W, 128))] + [_const_spec((128, AW))] * 6,
        out_specs=[tok(AW), tok(AW), tr, tr],
        out_shape=[jax.ShapeDtypeStruct((L, AW), bf16)] * 2 + [jax.ShapeDtypeStruct((AW, L), bf16)] * 2,
        compiler_params=_cp(("parallel",)))(qa, dycat, o_acc, lse, cst["pq"], cst["group"], *cst["head_a"], *cst["head_b"])


def _fox_attn_bwd(ka, va, tot, qb, doa, qbt, doat, *, tq=256):
    L = ka.shape[0]
    nq = L // tq

    def body(ka_ref, va_ref, tot_ref, qb_ref, doa_ref, qbt_ref, doat_ref, dq_ref, dkt_ref, dvt_ref):
        j = pl.program_id(0)

        @pl.when(j == 0)
        def _():
            dq_ref[...] = jnp.zeros_like(dq_ref)
        row = lax.broadcasted_iota(jnp.int32, (tq, tq), 0)
        col = lax.broadcasted_iota(jnp.int32, (tq, tq), 1)
        diag_bias = jnp.where(row >= col, 0.0, NEG)

        def block(i, carry, gap, bias):
            r0 = pl.multiple_of(i * tq, tq)
            out = []
            for h in range(FOX_H):
                hl = slice(h * HL, (h + 1) * HL)
                dkt, dvt, dsum = carry[h]
                kj = ka_ref[:, hl]
                s = _dot_nt(qb_ref[pl.ds(r0, tq), hl], kj) + gap[h]
                if bias is not None:
                    s = s + bias
                p = jnp.exp(s)
                ds = p * _dot_nt(doa_ref[pl.ds(r0, tq), hl], va_ref[:, hl])
                dsb = ds.astype(bf16)
                dq_ref[pl.ds(r0, tq), hl] += _dot(dsb, kj)
                out.append((dkt + _dot(qbt_ref[hl, pl.ds(r0, tq)], dsb),
                            dvt + _dot(doat_ref[hl, pl.ds(r0, tq)], p.astype(bf16)),
                            dsum + jnp.sum(ds, 0, keepdims=True)))
            return tuple(out)

        init = tuple((jnp.zeros((HL, tq), f32), jnp.zeros((HL, tq), f32), jnp.zeros((1, tq), f32))
                     for _ in range(FOX_H))
        def step(i, state):
            carry, gap = state
            gap = tuple(g + tot_ref[i - 1, h] for h, g in enumerate(gap))
            return block(i, carry, gap, None), gap

        zero_gap = tuple(jnp.zeros((), f32) for _ in range(FOX_H))
        carry = block(j, init, zero_gap, diag_bias)
        carry, _ = lax.fori_loop(j + 1, nq, step, (carry, zero_gap))
        for h in range(FOX_H):
            hl = slice(h * HL, (h + 1) * HL)
            dkt_ref[hl, :] = carry[h][0]
            dvt_ref[hl, :] = carry[h][1]
            dkt_ref[h * HL + FOX_HD:h * HL + FOX_HD + 1, :] = carry[h][2]

    full = lambda shape: pl.BlockSpec(shape, lambda j: (0, 0), pipeline_mode=pl.Buffered(1))
    blk = pl.BlockSpec((tq, AW), lambda j: (j, 0))
    trb = pl.BlockSpec((AW, tq), lambda j: (0, j))
    return pl.pallas_call(
        body, name="fox_bwd", grid=(nq,),
        in_specs=[blk, blk, pl.BlockSpec(memory_space=pltpu.SMEM), full((L, AW)), full((L, AW)), full((AW, L)),
                  full((AW, L))],
        out_specs=[pl.BlockSpec((L, AW), lambda j: (0, 0)), trb, trb],
        out_shape=[jax.ShapeDtypeStruct((L, AW), f32), jax.ShapeDtypeStruct((AW, L), f32),
                   jax.ShapeDtypeStruct((AW, L), f32)],
        compiler_params=_cp(("arbitrary",)))(ka, va, tot, qb, doa, qbt, doat)


def _fox_post(dq, dkt, dvt, proj, ddt, hp, cst, *, tt=256):
    L = proj.shape[0]
    n = L // tt

    def body(dq_ref, dkt_ref, dvt_ref, sm_ref, ddt_ref, hp_ref, pqt_ref, ca_ref,
             dqo_ref, dko_ref, dvo_ref, dsm_ref, acc_ref, carry):
        @pl.when(pl.program_id(0) == 0)
        def _():
            carry[...] = jnp.zeros_like(carry)
            acc_ref[...] = jnp.zeros_like(acc_ref)
        pqt = pqt_ref[...]
        dk_full = dkt_ref[...].T
        dqo_ref[...] = _dot((dq_ref[...] * FOX_SCALE).astype(bf16), pqt).astype(bf16)
        dko_ref[...] = _dot(dk_full.astype(bf16), pqt).astype(bf16)
        dvo_ref[...] = _dot(dvt_ref[...].T.astype(bf16), pqt).astype(bf16)
        dc = -_dot_hi(dk_full, ca_ref[...])
        r = lax.broadcasted_iota(jnp.int32, (tt, tt), 0)
        c = lax.broadcasted_iota(jnp.int32, (tt, tt), 1)
        dl = _dot_hi(jnp.where(r <= c, 1.0, 0.0), dc) + carry[0:1, :]
        carry[0:1, :] += jnp.sum(dc, 0, keepdims=True)
        xx = sm_ref[...] + hp_ref[3:4, :]
        lane = lax.broadcasted_iota(jnp.int32, (tt, 128), 1)
        dlogit = jnp.where((lane >= SM_F) & (lane < SM_F + FOX_H), dl * _sigmoid(-xx), 0.0)
        acc_ref[3:4, :] += jnp.sum(dlogit, 0, keepdims=True)
        dsm_ref[...] = (dlogit + ddt_ref[...]).astype(bf16)

    rev = lambda w_: pl.BlockSpec((tt, w_), lambda i: (n - 1 - i, 0))
    revt = pl.BlockSpec((AW, tt), lambda i: (0, n - 1 - i))
    return pl.pallas_call(
        body, name="fox_post", grid=(n,),
        in_specs=[rev(AW), revt, revt, pl.BlockSpec((tt, 128), lambda i: (n - 1 - i, P_SM // 128)), rev(128),
                  _const_spec((8, 128)), _const_spec((AW, FOX_W)), _const_spec((AW, 128))],
        out_specs=[rev(FOX_W), rev(FOX_W), rev(FOX_W), rev(128), pl.BlockSpec((8, 128), lambda i: (0, 0))],
        out_shape=[jax.ShapeDtypeStruct((L, FOX_W), bf16)] * 3 + [jax.ShapeDtypeStruct((L, 128), bf16),
                                                                   jax.ShapeDtypeStruct((8, 128), f32)],
        scratch_shapes=[pltpu.VMEM((8, 128), f32)],
        compiler_params=_cp(("arbitrary",)))(dq, dkt, dvt, proj, ddt, hp, cst["pqt"], cst["col_a"])


SSD_GW = SSD_W // SSD_G
SSD_HPG = SSD_H // SSD_G


def _ssd_pre(x, xprev, sm, cp_ref, hp_ref, tc):
    pre = (cp_ref[4:5, :] + cp_ref[3:4, :] * x + cp_ref[2:3, :] * _shift_down(x, xprev, 1)
           + cp_ref[1:2, :] * _shift_down(x, xprev, 2) + cp_ref[0:1, :] * _shift_down(x, xprev, 3))
    sig = _sigmoid(pre)
    raw = sm + hp_ref[0:1, :]
    dt = jnp.maximum(raw, 0.0) + _log1pexp(raw)
    a_neg = -jnp.exp(hp_ref[1:2, :])
    r = lax.broadcasted_iota(jnp.int32, (tc, tc), 0)
    c = lax.broadcasted_iota(jnp.int32, (tc, tc), 1)
    cs = _dot_hi(jnp.where(r >= c, 1.0, 0.0), dt * a_neg)
    return pre, sig, raw, dt, a_neg, cs, cs.T, r >= c


def _ssd_fwd(proj, cp, hp, ng, *, tc=256):
    L = proj.shape[0]
    nc = L // tc

    def body(xc_ref, xp_ref, z_ref, sm_ref, cp_ref, hp_ref, ng_ref, y_ref, ypre_ref, sin_ref, s_scr):
        i = pl.program_id(0)

        @pl.when(i == 0)
        def _():
            s_scr[...] = jnp.zeros_like(s_scr)
        x = xc_ref[...]
        xprev = xp_ref[...] * jnp.where(i > 0, 1.0, 0.0)
        pre, sig, _, dt, _, cs, cst, tril = _ssd_pre(x, xprev, sm_ref[...], cp_ref, hp_ref, tc)
        xbc = pre * sig
        sin_ref[...] = s_scr[...]
        for g in range(SSD_G):
            bg = xbc[:, SSD_W + g * SSD_N:SSD_W + (g + 1) * SSD_N]
            cg = xbc[:, SSD_W + SSD_G * SSD_N + g * SSD_N:SSD_W + SSD_G * SSD_N + (g + 1) * SSD_N].astype(bf16)
            cb = _dot_nt(cg, bg.astype(bf16))
            for e in range(SSD_HPG):
                h = g * SSD_HPG + e
                hs = slice(h * SSD_HD, (h + 1) * SSD_HD)
                xs = xbc[:, hs]
                csc = cs[:, h:h + 1]
                lm = jnp.where(tril, jnp.exp(jnp.minimum(csc - cst[h:h + 1, :], 0.0)), 0.0)
                xdt = (xs * dt[:, h:h + 1]).astype(bf16)
                s_h = s_scr[:, hs]
                y = _dot((cb * lm).astype(bf16), xdt) + jnp.exp(csc) * _dot(cg, s_h.astype(bf16))
                ypre_ref[:, hs] = y + hp_ref[2:3, h:h + 1] * xs
                cl = cs[tc - 1:tc, h:h + 1]
                bd = (bg * jnp.exp(cl - csc)).astype(bf16)
                s_scr[:, hs] = jnp.exp(cl) * s_h + _dot_tn(bd, xdt)
        z = z_ref[...]
        yz = ypre_ref[...] * (z * _sigmoid(z))
        for g in range(SSD_G):
            gs = slice(g * SSD_GW, (g + 1) * SSD_GW)
            yg = yz[:, gs]
            r = lax.rsqrt(jnp.mean(yg * yg, -1, keepdims=True) + RMS_EPS)
            y_ref[:, gs] = (yg * r * ng_ref[0:1, gs]).astype(bf16)

    return pl.pallas_call(
        body, name="ssd_fwd", grid=(nc,),
        in_specs=[pl.BlockSpec((tc, 1024), lambda i: (i, 0)),
                  pl.BlockSpec((tc, 1024), lambda i: (jnp.maximum(i - 1, 0), 0)),
                  pl.BlockSpec((tc, SSD_W), lambda i: (i, P_Z // SSD_W)),
                  pl.BlockSpec((tc, 128), lambda i: (i, P_SM // 128)),
                  _const_spec((8, 1024)), _const_spec((8, 128)), _const_spec((8, SSD_W))],
        out_specs=[pl.BlockSpec((tc, SSD_W), lambda i: (i, 0)), pl.BlockSpec((tc, SSD_W), lambda i: (i, 0)),
                   pl.BlockSpec((SSD_N, SSD_W), lambda i: (i, 0))],
        out_shape=[jax.ShapeDtypeStruct((L, SSD_W), bf16), jax.ShapeDtypeStruct((L, SSD_W), f32),
                   jax.ShapeDtypeStruct((nc * SSD_N, SSD_W), f32)],
        scratch_shapes=[pltpu.VMEM((SSD_N, SSD_W), f32)],
        compiler_params=_cp(("arbitrary",)))(proj, proj, proj, proj, cp, hp, ng)


def _ssd_bwd(proj, dycat, ypre, sin, cp, hp, ng, *, tc=256):
    L = proj.shape[0]
    nc = L // tc

    def body(xc_ref, xp_ref, z_ref, sm_ref, cp_ref, hp_ref, ng_ref, sin_ref, ypre_ref, dy_ref,
             dxbc_ref, dz_ref, ddt_ref, acc1_ref, acc2_ref, ds_scr, dnext_scr, dxbc_scr):
        i = pl.program_id(0)
        c_idx = nc - 1 - i

        @pl.when(i == 0)
        def _():
            ds_scr[...] = jnp.zeros_like(ds_scr)
            dnext_scr[...] = jnp.zeros_like(dnext_scr)
            acc1_ref[...] = jnp.zeros_like(acc1_ref)
            acc2_ref[...] = jnp.zeros_like(acc2_ref)
        x = xc_ref[...]
        xprev = xp_ref[...] * jnp.where(c_idx > 0, 1.0, 0.0)
        pre, sig, raw, dt, a_neg, cs, cst, tril = _ssd_pre(x, xprev, sm_ref[...], cp_ref, hp_ref, tc)
        xbc = pre * sig
        z = z_ref[...]
        sz = _sigmoid(z)
        silz = z * sz
        yall = ypre_ref[...]
        yz = yall * silz
        dy = dy_ref[...]
        dyz_parts = []
        for g in range(SSD_G):
            gs = slice(g * SSD_GW, (g + 1) * SSD_GW)
            yg, dyg = yz[:, gs], dy[:, gs]
            r = lax.rsqrt(jnp.mean(yg * yg, -1, keepdims=True) + RMS_EPS)
            acc1_ref[5:6, gs] += jnp.sum(dyg * yg * r, 0, keepdims=True)
            dyn = dyg * ng_ref[0:1, gs]
            dyz_parts.append(r * (dyn - yg * (r * r) * jnp.mean(dyn * yg, -1, keepdims=True)))
        dyz = jnp.concatenate(dyz_parts, axis=1)
        dz_ref[...] = (dyz * yall * (sz * (1.0 + z * (1.0 - sz)))).astype(bf16)
        dyall = dyz * silz

        lane1 = lax.broadcasted_iota(jnp.int32, (1, 128), 1)
        sub = lax.broadcasted_iota(jnp.int32, (128, tc), 0)
        rowc = lax.broadcasted_iota(jnp.int32, (tc, 1), 0)
        dcs = jnp.zeros((tc, 128), f32)
        dcsr = jnp.zeros((128, tc), f32)
        ddt = jnp.zeros((tc, 128), f32)
        dd_row = jnp.zeros((1, 128), f32)
        for g in range(SSD_G):
            b0 = SSD_W + g * SSD_N
            c0 = SSD_W + SSD_G * SSD_N + g * SSD_N
            bg = xbc[:, b0:b0 + SSD_N]
            bgb = bg.astype(bf16)
            cgb = xbc[:, c0:c0 + SSD_N].astype(bf16)
            cb = _dot_nt(cgb, bgb)
            dbg = jnp.zeros((tc, SSD_N), f32)
            dcg = jnp.zeros((tc, SSD_N), f32)
            for e in range(SSD_HPG):
                h = g * SSD_HPG + e
                hs = slice(h * SSD_HD, (h + 1) * SSD_HD)
                oh = jnp.where(lane1 == h, 1.0, 0.0)
                xs = xbc[:, hs]
                dth = dt[:, h:h + 1]
                csc = cs[:, h:h + 1]
                lm = jnp.where(tril, jnp.exp(jnp.minimum(csc - cst[h:h + 1, :], 0.0)), 0.0)
                m = cb * lm
                xdt = (xs * dth).astype(bf16)
                s_h = sin_ref[:, hs]
                s_hb = s_h.astype(bf16)
                dyh = dyall[:, hs]
                dyb = dyh.astype(bf16)
                dd_row = dd_row + oh * jnp.sum(dyh * xs)
                dxs = hp_ref[2:3, h:h + 1] * dyh
                ecs = jnp.exp(csc)
                cs_prod = _dot(cgb, s_hb)
                dcsb = (ecs * dyh).astype(bf16)
                dcg = dcg + _dot_nt(dcsb, s_hb)
                ds_in = _dot_tn(cgb, dcsb)
                dcs_h = jnp.sum(dyh * ecs * cs_prod, -1, keepdims=True)
                dm = _dot_nt(dyb, xdt)
                w = dm * m
                dcs_h = dcs_h + jnp.sum(w, -1, keepdims=True)
                dcsr = jnp.where(sub == h, jnp.sum(w, 0, keepdims=True), dcsr)
                dcbb = (dm * lm).astype(bf16)
                dcg = dcg + _dot(dcbb, bgb)
                dbg = dbg + _dot_tn(dcbb, cgb)
                dxdt = _dot_tn(m.astype(bf16), dyb)
                dsn = ds_scr[:, hs]
                dsnb = dsn.astype(bf16)
                cl = cs[tc - 1:tc, h:h + 1]
                dec = jnp.exp(cl - csc)
                dxdt = dxdt + _dot((bg * dec).astype(bf16), dsnb)
                dbd = _dot_nt(xdt, dsnb)
                dbg = dbg + dbd * dec
                gdec = jnp.sum(dbd * bg, -1, keepdims=True) * dec
                ecl = jnp.exp(cl)
                dcl = jnp.sum(gdec) + jnp.sum(dsn * s_h) * ecl
                ds_scr[:, hs] = ecl * dsn + ds_in
                dcs_h = dcs_h - gdec + jnp.where(rowc == tc - 1, dcl, 0.0)
                dcs = dcs + dcs_h * oh
                dxbc_scr[:, hs] = dxs + dxdt * dth
                ddt = ddt + jnp.sum(dxdt * xs, -1, keepdims=True) * oh
            dxbc_scr[:, b0:b0 + SSD_N] = dbg
            dxbc_scr[:, c0:c0 + SSD_N] = dcg
        dcs = dcs - dcsr.T
        r_i = lax.broadcasted_iota(jnp.int32, (tc, tc), 0)
        c_i = lax.broadcasted_iota(jnp.int32, (tc, tc), 1)
        da = _dot_hi(jnp.where(r_i <= c_i, 1.0, 0.0), dcs)
        ddt = ddt + da * a_neg
        acc2_ref[1:2, :] += jnp.sum(da * dt, 0, keepdims=True) * a_neg
        lane = lax.broadcasted_iota(jnp.int32, (tc, 128), 1)
        ddraw = jnp.where(lane < SSD_H, ddt * _sigmoid(raw), 0.0)
        acc2_ref[0:1, :] += jnp.sum(ddraw, 0, keepdims=True)
        acc2_ref[2:3, :] += dd_row
        ddt_ref[...] = ddraw
        dpre = dxbc_scr[...] * (sig * (1.0 + pre * (1.0 - sig)))
        acc1_ref[4:5, :] += jnp.sum(dpre, 0, keepdims=True)
        for k in range(SSD_K):
            acc1_ref[k:k + 1, :] += jnp.sum(dpre * _shift_down(x, xprev, SSD_K - 1 - k), 0, keepdims=True)
        dnext = dnext_scr[...]
        dxbc_ref[...] = (cp_ref[3:4, :] * dpre + cp_ref[2:3, :] * _shift_up(dpre, dnext, 1)
                         + cp_ref[1:2, :] * _shift_up(dpre, dnext, 2)
                         + cp_ref[0:1, :] * _shift_up(dpre, dnext, 3)).astype(bf16)
        dnext_scr[...] = dpre

    rev = lambda w_, col: pl.BlockSpec((tc, w_), lambda i: (nc - 1 - i, col // w_))
    return pl.pallas_call(
        body, name="ssd_bwd", grid=(nc,),
        in_specs=[rev(1024, 0), pl.BlockSpec((tc, 1024), lambda i: (jnp.maximum(nc - 2 - i, 0), 0)),
                  rev(SSD_W, P_Z), rev(128, P_SM),
                  _const_spec((8, 1024)), _const_spec((8, 128)), _const_spec((8, SSD_W)),
                  pl.BlockSpec((SSD_N, SSD_W), lambda i: (nc - 1 - i, 0)), rev(SSD_W, 0), rev(SSD_W, 0)],
        out_specs=[rev(1024, 0), rev(SSD_W, 0), rev(128, 0),
                   pl.BlockSpec((8, 1024), lambda i: (0, 0)), pl.BlockSpec((8, 128), lambda i: (0, 0))],
        out_shape=[jax.ShapeDtypeStruct((L, 1024), bf16), jax.ShapeDtypeStruct((L, SSD_W), bf16),
                   jax.ShapeDtypeStruct((L, 128), f32), jax.ShapeDtypeStruct((8, 1024), f32),
                   jax.ShapeDtypeStruct((8, 128), f32)],
        scratch_shapes=[pltpu.VMEM((SSD_N, SSD_W), f32), pltpu.VMEM((tc, 1024), f32), pltpu.VMEM((tc, 1024), f32)],
        compiler_params=_cp(("arbitrary",)))(proj, proj, proj, proj, cp, hp, ng, sin, ypre, dycat)


def _pack_cols(w):
    pad = jnp.zeros(w.shape[:-1] + (P_W - P_SM - SSD_H - FOX_H,), w.dtype)
    return jnp.concatenate([w[..., 512:1536], w[..., 0:512], w[..., 1544:2312], w[..., 2316:3084],
                            w[..., 1536:1544], w[..., 2312:2316], pad], axis=-1)


def _unpack_cols(g):
    return jnp.concatenate([g[..., 1024:1536], g[..., 0:1024], g[..., 3072:3080], g[..., 1536:2304],
                            g[..., 3080:3084], g[..., 2304:3072]], axis=-1)


def _rows8(*rows):
    width = max(r.shape[-1] for r in rows)
    out = [jnp.pad(r.astype(f32), (0, width - r.shape[-1])) for r in rows]
    out += [jnp.zeros((width,), f32)] * (8 - len(out))
    return jnp.stack(out)


def _local_step(x, tgt, p, weights_of, grads_done):
    gb_in = _rows8(p["ln_in_g"], p["ln_in_b"])
    cst = _fox_consts()
    x0 = _ln_in_fwd(x, gb_in)
    saved = []
    for l in range(DEPTH):
        mv = [_rows8(p["mod"][l, 3 * j], p["mod"][l, 3 * j + 1], p["mod"][l, 3 * j + 2], p["ln_g"][l, j], p["ln_b"][l, j])
              for j in range(3)]
        cp = _rows8(*[p["ssd_conv_w"][l, k] for k in range(SSD_K)], p["ssd_conv_b"][l])
        hp = _rows8(jnp.pad(p["ssd_dt_bias"][l], (0, 120)), jnp.pad(p["ssd_a_log"][l], (0, 120)),
                    jnp.pad(p["ssd_d"][l], (0, 120)), jnp.pad(p["fox_f_bias"][l], (SM_F, 128 - SM_F - FOX_H)))
        ng = _rows8(p["ssd_norm_g"][l])
        scw = _rows8(*[p["sconv_w"][l, k] for k in range(SC_K)])
        w = dict(weights_of(l, "ffn1", x0))
        x1 = _ffn_fwd(x0, mv[0], w["ffn1_w_in"], w["ffn1_w_out"])
        w.update(weights_of(l, "mix", x1))
        proj = _inproj_fwd(x1, mv[1], w["mix_w_in"])
        y_ssd, ypre, sin = _ssd_fwd(proj, cp, hp, ng)
        qa, ka, va, tot = _fox_prep(proj, hp, cst)
        tot = tot[::8, SM_F:SM_F + FOX_H]
        o, o_acc, lse = _fox_attn_fwd(qa, ka, va, tot)
        y_sc = _sconv_fwd(proj, scw)
        ycat = jnp.concatenate([y_ssd, o, y_sc], axis=1)
        x2 = _outproj_fwd(x1, ycat, mv[1], w["mix_w_out"])
        w.update(weights_of(l, "ffn2", x2))
        x3 = _ffn_fwd(x2, mv[2], w["ffn2_w_in"], w["ffn2_w_out"])
        saved.append((x0, x1, x2, mv, cp, hp, ng, scw, proj, ypre, sin, qa, ka, va, tot, o_acc, lse, ycat, w))
        x0 = x3
    dx, loss_acc = _loss_head(x0, tgt)

    g = {k: [None] * DEPTH for k in (
        "mod", "ln_g", "ln_b", "ssd_conv_w", "ssd_conv_b", "ssd_dt_bias", "ssd_a_log", "ssd_d", "ssd_norm_g",
        "fox_f_bias", "sconv_w")}
    def behind(small, tok):
        return small if tok is None else small + tok[0, 0]

    token = None
    for l in reversed(range(DEPTH)):
        x0, x1, x2, mv, cp, hp, ng, scw, proj, ypre, sin, qa, ka, va, tot, o_acc, lse, ycat, w = saved[l]
        dx, ht, da, actt, dyb, a2 = _ffn_bwd(x2, dx, behind(mv[2], token), w["ffn2_w_in"], w["ffn2_w_out"])
        token = grads_done(l, "ffn2_w_out", {"ffn2_w_out": _matmul_tokens(actt, dyb, tn=D, tk=DW_TK, name="dw_ffn_out")})
        token = grads_done(l, "ffn2_w_in", {"ffn2_w_in": _matmul_tokens(ht, da, tn=FS, tk=DW_TK, name="dw_ffn_in",
                                                                        after=token)})
        mv1 = behind(mv[1], token)
        dxp, dyb, dycat, ycat_t, a1 = _outproj_bwd(x1, ycat, dx, mv1, w["mix_w_out"])
        gw_mix_out = _matmul_tokens(ycat_t, dyb, tn=D // 2, tk=DW_TK, name="dw_mix_out")
        dxbc, dz, ddt, acc1, acc2 = _ssd_bwd(proj, dycat, ypre, sin, cp, hp, ng)
        qb, doa, qbt, doat = _fox_bprep(qa, dycat, o_acc, lse, cst)
        dq, dkt, dvt = _fox_attn_bwd(ka, va, tot, qb, doa, qbt, doat)
        dq, dk, dv, dsm, accf = _fox_post(dq, dkt, dvt, proj, ddt, hp, cst)
        dsb, dsc, dsx, accs = _sconv_bwd(proj, dycat, scw)
        dproj = jnp.concatenate([dxbc, dz, dq, dk, dv, dsb, dsc, dsx, dsm], axis=1)
        dx, ht, a1b = _inproj_bwd(x1, dxp, dproj, mv1, w["mix_w_in"])
        token = grads_done(l, "mix", {"mix_w_in": _matmul_tokens(ht, dproj, tn=P_W // 5, tk=DW_TK, name="dw_mix_in"),
                                      "mix_w_out": gw_mix_out})
        dx, ht, da, actt, dyb, a0 = _ffn_bwd(x0, dx, behind(mv[0], token), w["ffn1_w_in"], w["ffn1_w_out"])
        token = grads_done(l, "ffn1_w_out", {"ffn1_w_out": _matmul_tokens(actt, dyb, tn=D, tk=DW_TK, name="dw_ffn_out")})
        token = grads_done(l, "ffn1_w_in", {"ffn1_w_in": _matmul_tokens(ht, da, tn=FS, tk=DW_TK, name="dw_ffn_in",
                                                                        after=token)})
        g["mod"][l] = jnp.concatenate([a0[0:3], a1b[0:2], a1[2:3], a2[0:3]], axis=0)
        g["ln_g"][l] = jnp.stack([a0[3], a1[3], a2[3]])
        g["ln_b"][l] = jnp.stack([a0[4], a1[4], a2[4]])
        g["ssd_conv_w"][l] = acc1[0:SSD_K]
        g["ssd_conv_b"][l] = acc1[4]
        g["ssd_norm_g"][l] = acc1[5, :SSD_W]
        g["ssd_dt_bias"][l] = acc2[0, :SSD_H]
        g["ssd_a_log"][l] = acc2[1, :SSD_H]
        g["ssd_d"][l] = acc2[2, :SSD_H]
        g["fox_f_bias"][l] = accf[3, SM_F:SM_F + FOX_H]
        g["sconv_w"][l] = accs[0:SC_K]
    grad_x, a_in = _ln_in_bwd(x, dx, behind(gb_in, token))
    g = {k: jnp.stack(v) for k, v in g.items()}
    g["ln_in_g"], g["ln_in_b"] = a_in[0], a_in[1]
    return loss_acc[0, 0], grad_x, g


MESH = pl.DeviceIdType.MESH
ANY = pl.BlockSpec(memory_space=pl.ANY)


def _all_gather(shards, *, in_vmem, name):
    n_arr = len(shards)

    def body(*refs):
        x_refs, out_refs = refs[:n_arr], refs[n_arr:2 * n_arr]
        send_sems, recv_sems, local_sems = refs[2 * n_arr:]
        x, y, c = lax.axis_index("x"), lax.axis_index("y"), lax.axis_index("c")
        me, sibling = (x, y, c), (x, y, 1 - c)
        chips = [(1 - x, y), (x, 1 - y), (1 - x, 1 - y)]

        def copy(a, k, block, to, src=None):
            px, py, pc = block
            slot = out_refs[a].at[4 * px + 2 * py + pc]
            return pltpu.make_async_remote_copy(
                src_ref=slot if src is None else src, dst_ref=slot,
                send_sem=send_sems.at[7 * a + k], recv_sem=recv_sems.at[7 * a + k], device_id=to, device_id_type=MESH)

        mine, first, passed = [], [], []
        for a in range(n_arr):
            mine.append(pltpu.make_async_copy(x_refs[a], out_refs[a].at[4 * x + 2 * y + c], local_sems.at[a]))
            mine[-1].start()
            first.append(copy(a, 0, me, sibling, src=x_refs[a]))
            first += [copy(a, 1 + j, me, (*chip, c), src=x_refs[a]) for j, chip in enumerate(chips)]
        for cp in first:
            cp.start()
        for j, chip in enumerate(chips):
            for a in range(n_arr):
                copy(a, 1 + j, (*chip, c), me).wait_recv()
                passed.append(copy(a, 4 + j, (*chip, c), sibling))
                passed[-1].start()
        for a in range(n_arr):
            copy(a, 0, sibling, me).wait_recv()
            for j, chip in enumerate(chips):
                copy(a, 4 + j, (*chip, 1 - c), me).wait_recv()
        for cp in first + passed:
            cp.wait_send()
        for cp in mine:
            cp.wait()

    spec = pl.BlockSpec(memory_space=pltpu.VMEM) if in_vmem else ANY
    return pl.pallas_call(
        body, name=name, out_shape=[jax.ShapeDtypeStruct((N_DEV,) + s.shape, s.dtype) for s in shards],
        in_specs=[spec] * n_arr, out_specs=[spec] * n_arr,
        scratch_shapes=[pltpu.SemaphoreType.DMA((7 * n_arr,)), pltpu.SemaphoreType.DMA((7 * n_arr,)),
                        pltpu.SemaphoreType.DMA((n_arr,))],
    )(*shards)


HBM = pl.BlockSpec(memory_space=pltpu.HBM)
SEM = pl.BlockSpec(memory_space=pltpu.SEMAPHORE)
EFFECT = pltpu.SideEffectType.DATAFLOW_SIDE_EFFECTING


def _spread_copies(srcs, lands, send_sems, recv_sems, local_sems, scatter):
    x, y, c = lax.axis_index("x"), lax.axis_index("y"), lax.axis_index("c")
    me = 4 * x + 2 * y + c
    local, remote = [], []
    for a in range(len(srcs)):
        own = srcs[a].at[me] if scatter else srcs[a]
        local.append(pltpu.make_async_copy(own, lands[a].at[me], local_sems.at[a]))
        for r in range(1, N_DEV):
            px, py, pc = (1 - x if r & 4 else x), (1 - y if r & 2 else y), (1 - c if r & 1 else c)
            peer = 4 * px + 2 * py + pc
            k = (N_DEV - 1) * a + r - 1
            mk = functools.partial(pltpu.make_async_remote_copy, send_sem=send_sems.at[k], recv_sem=recv_sems.at[k],
                                   device_id=(px, py, pc), device_id_type=MESH)
            remote.append((mk(src_ref=srcs[a].at[peer] if scatter else srcs[a], dst_ref=lands[a].at[me]),
                           mk(src_ref=own, dst_ref=lands[a].at[peer])))
    return local, remote


def _spread_start(srcs, *, scatter, name, after=()):
    n, k = len(srcs), len(after)
    lands = [jax.ShapeDtypeStruct((N_DEV,) + s.shape[-2:], s.dtype) for s in srcs]

    def body(*refs):
        src, land = refs[:n], refs[n:2 * n]
        send_sems, recv_sems, local_sems = refs[2 * n + k:2 * n + k + 3]
        token = refs[-1]
        local, remote = _spread_copies(src, land, send_sems, recv_sems, local_sems, scatter)
        for cp in local:
            cp.start()
        for cp, _ in remote:
            cp.start()
        token[...] = jnp.zeros_like(token)

    nsem = (N_DEV - 1) * n
    out = pl.pallas_call(
        body, name=name,
        out_shape=(pltpu.SemaphoreType.DMA((nsem,)), pltpu.SemaphoreType.DMA((nsem,)), pltpu.SemaphoreType.DMA((n,)),
                   *[pltpu.HBM(s.shape, s.dtype) for s in srcs], *[pltpu.HBM(s.shape, s.dtype) for s in lands],
                   jax.ShapeDtypeStruct((8, 128), f32)),
        in_specs=[HBM] * (2 * n) + [ANY] * k,
        out_specs=(SEM, SEM, SEM, *[HBM] * (2 * n), pl.BlockSpec(memory_space=pltpu.VMEM)),
        input_output_aliases={i: 3 + i for i in range(2 * n)},
        compiler_params=pltpu.CompilerParams(has_side_effects=EFFECT),
    )(*[pltpu.with_memory_space_constraint(s, pltpu.HBM) for s in srcs],
      *[pltpu.with_memory_space_constraint(lax.empty(s.shape, s.dtype), pltpu.HBM) for s in lands], *after)
    return out[:-1], out[-1]


def _spread_wait(state, after, *, scatter, name):
    n = (len(state) - 3) // 2
    sems, thru = state[:3], state[3:]

    def body(*refs):
        src, land = refs[:n], refs[n:2 * n]
        send_sems, recv_sems, local_sems = refs[2 * n:2 * n + 3]
        local, remote = _spread_copies(src, land, send_sems, recv_sems, local_sems, scatter)
        for sent, received in remote:
            sent.wait_send()
            received.wait_recv()
        for cp in local:
            cp.wait()

    out = pl.pallas_call(
        body, name=name,
        out_shape=tuple(pltpu.HBM(t.shape, t.dtype) for t in thru),
        in_specs=[HBM] * (2 * n) + [SEM] * 3 + [ANY], out_specs=tuple([HBM] * (2 * n)),
        input_output_aliases={i: i for i in range(2 * n)},
        compiler_params=pltpu.CompilerParams(has_side_effects=EFFECT),
    )(*thru, *sems, after)
    return list(out[n:])


def _sum_slots(buf, *, tr, name):
    nb, r, n = buf.shape

    def body(b_ref, o_ref):
        acc = b_ref[0].astype(f32)
        for k in range(1, nb):
            acc = acc + b_ref[k].astype(f32)
        o_ref[...] = acc

    return pl.pallas_call(
        body, name=name, grid=(r // tr,),
        in_specs=[pl.BlockSpec((nb, tr, n), lambda i: (0, i, 0))],
        out_specs=pl.BlockSpec((tr, n), lambda i: (i, 0)),
        out_shape=jax.ShapeDtypeStruct((r, n), f32), compiler_params=_cp(("parallel",)))(buf)


def _ada_fwd(c_all, ada_w, ada_b_cols):
    n = ada_w.shape[-1]

    def body(c_ref, w_ref, b_ref, o_ref):
        cv = c_ref[...]
        ca = (cv * _sigmoid(cv)).astype(bf16)
        o_ref[...] = _dot(ca, w_ref[...].astype(bf16)) + b_ref[...]

    return pl.pallas_call(
        body, name="ada_fwd", grid=(DEPTH,),
        in_specs=[_const_spec((N_DEV, D)), pl.BlockSpec((None, D, n), lambda l: (l, 0, 0)),
                  pl.BlockSpec((None, 1, n), lambda l: (l, 0, 0))],
        out_specs=pl.BlockSpec((None, N_DEV, n), lambda l: (l, 0, 0)),
        out_shape=jax.ShapeDtypeStruct((DEPTH, N_DEV, n), f32), compiler_params=_cp(("parallel",)))(c_all, ada_w, ada_b_cols)


def _ada_bwd(c_all, dmod_cols):
    n = dmod_cols.shape[-1]

    def body(c_ref, d_ref, o_ref):
        cv = c_ref[...]
        ca = (cv * _sigmoid(cv)).astype(bf16)
        o_ref[...] = _dot_tn(ca, d_ref[...].astype(bf16))

    return pl.pallas_call(
        body, name="ada_bwd", grid=(DEPTH,),
        in_specs=[_const_spec((N_DEV, D)), pl.BlockSpec((None, N_DEV, n), lambda l: (l, 0, 0))],
        out_specs=pl.BlockSpec((None, D, n), lambda l: (l, 0, 0)),
        out_shape=jax.ShapeDtypeStruct((DEPTH, D, n), f32), compiler_params=_cp(("parallel",)))(c_all, dmod_cols)


def _adamw(w, g, m, v, *, tr, name):
    r, n = w.shape

    def body(w_ref, g_ref, m_ref, v_ref, d_ref, mo_ref, vo_ref):
        g_ = g_ref[...]
        m_ = ADAM_B1 * m_ref[...] + (1.0 - ADAM_B1) * g_
        v_ = ADAM_B2 * v_ref[...] + (1.0 - ADAM_B2) * jnp.square(g_)
        m_hat = m_ / (1.0 - ADAM_B1 ** ADAM_STEP)
        v_hat = v_ / (1.0 - ADAM_B2 ** ADAM_STEP)
        d_ref[...] = -ADAM_LR * (m_hat / (jnp.sqrt(v_hat) + ADAM_EPS) + ADAM_WD * w_ref[...])
        mo_ref[...] = m_
        vo_ref[...] = v_

    blk = pl.BlockSpec((tr, n), lambda i: (i, 0))
    return pl.pallas_call(
        body, name=name, grid=(r // tr,), in_specs=[blk] * 4, out_specs=[blk] * 3,
        out_shape=[jax.ShapeDtypeStruct((r, n), f32)] * 3, compiler_params=_cp(("parallel",)))(w, g, m, v)


def _adamw_landed(w, m, v, land0, land1, *, tr, name):
    _, r, n = w.shape
    nrt = r // tr

    def body(w_ref, m_ref, v_ref, l0_ref, l1_ref, g_ref, d_ref, mo_ref, vo_ref):
        layer = pl.program_id(0)

        def total(ref):
            acc = ref[0].astype(f32)
            for k in range(1, N_DEV):
                acc = acc + ref[k].astype(f32)
            return acc

        @pl.when(layer == 0)
        def _():
            g_ref[...] = total(l0_ref)

        @pl.when(layer == 1)
        def _():
            g_ref[...] = total(l1_ref)
        g_ = g_ref[...]
        m_ = ADAM_B1 * m_ref[...] + (1.0 - ADAM_B1) * g_
        v_ = ADAM_B2 * v_ref[...] + (1.0 - ADAM_B2) * jnp.square(g_)
        m_hat = m_ / (1.0 - ADAM_B1 ** ADAM_STEP)
        v_hat = v_ / (1.0 - ADAM_B2 ** ADAM_STEP)
        d_ref[...] = -ADAM_LR * (m_hat / (jnp.sqrt(v_hat) + ADAM_EPS) + ADAM_WD * w_ref[...])
        mo_ref[...] = m_
        vo_ref[...] = v_

    blk = pl.BlockSpec((None, tr, n), lambda l, i: (l, i, 0))
    land0_spec = pl.BlockSpec((N_DEV, tr, n), lambda l, i: (0, jnp.where(l == 0, i, nrt - 1), 0))
    land1_spec = pl.BlockSpec((N_DEV, tr, n), lambda l, i: (0, jnp.where(l == 1, i, 0), 0))
    return pl.pallas_call(
        body, name=name, grid=(DEPTH, nrt), in_specs=[blk] * 3 + [land0_spec, land1_spec], out_specs=[blk] * 4,
        out_shape=[jax.ShapeDtypeStruct(w.shape, f32)] * 4,
        compiler_params=_cp(("arbitrary", "arbitrary")))(w, m, v, land0, land1)


WEIGHTS = ["ln_in_g", "ln_in_b", "ada_w", "ada_b", "ffn1_w_in", "ffn1_w_out", "mix_w_in", "mix_w_out", "ssd_conv_w",
           "ssd_conv_b", "ssd_dt_bias", "ssd_a_log", "ssd_d", "ssd_norm_g", "fox_f_bias", "sconv_w", "ffn2_w_in",
           "ffn2_w_out", "ln_g", "ln_b"]
BIG = ["ffn1_w_in", "ffn1_w_out", "ffn2_w_in", "ffn2_w_out", "mix_w_in", "mix_w_out"]
GROUPS = {"ffn1": ["ffn1_w_in", "ffn1_w_out"], "mix": ["mix_w_in", "mix_w_out"], "ffn2": ["ffn2_w_in", "ffn2_w_out"]}
COL_SHARDED = ("ffn1_w_in", "ffn2_w_in")
SMALL_SHARDED = {"ssd_conv_w": 128, "sconv_w": 32, "ln_g": 128, "ln_b": 128}
ADAM_TR = {"ada_w": 256, "ffn1_w_in": 512, "ffn2_w_in": 512, "ffn1_w_out": 352, "ffn2_w_out": 352, "mix_w_in": 64,
           "mix_w_out": 256}
LAND_TR = {"ffn1_w_in": 128, "ffn2_w_in": 128, "ffn1_w_out": 176, "ffn2_w_out": 176, "mix_w_in": 32, "mix_w_out": 128}


def _pad_rows(v, mult=128):
    v = v.reshape(-1)
    return jnp.pad(v, (0, (-v.shape[0]) % mult))


def _pack_rows(parts, row_mult=8):
    flat = [_pad_rows(p.astype(f32)) for p in parts]
    offs, o = [], 0
    for f in flat:
        offs.append(o)
        o += f.shape[0] // 128
    buf = jnp.concatenate(flat).reshape(-1, 128)
    return jnp.pad(buf, ((0, (-buf.shape[0]) % row_mult), (0, 0))), offs


def _take(buf, off, shape):
    n = 1
    for s in shape:
        n *= s
    rows = -(-n // 128)
    lead = buf.shape[:-2]
    flat = buf[..., off:off + rows, :].reshape(lead + (rows * 128,))
    return flat[..., :n].reshape(lead + tuple(shape))


def kernel(*args):
    names = (["x", "c"] + WEIGHTS + ["loss_target"] + ["m_" + n for n in WEIGHTS] + ["v_" + n for n in WEIGHTS])
    assert len(args) == len(names)
    a = dict(zip(names, args))
    xi, yi, ci = lax.axis_index("x"), lax.axis_index("y"), lax.axis_index("c")
    me = 4 * xi + 2 * yi + ci

    small_in = [a["c"], a["ln_g"], a["ln_b"], a["ssd_conv_w"], a["sconv_w"]]
    buf, offs = _pack_rows(small_in)
    got, = _all_gather([buf], in_vmem=True, name="gather_small")
    c_all = _take(got, offs[0], (D,))
    full = {}
    for k, n in enumerate(["ln_g", "ln_b", "ssd_conv_w", "sconv_w"]):
        sh = a[n].shape
        t = _take(got, offs[k + 1], sh)
        full[n] = jnp.transpose(t, (1, 2, 0, 3)).reshape(sh[0], sh[1], N_DEV * sh[2])

    ncol = a["ada_w"].shape[-1]
    ada_b_cols = lax.dynamic_slice_in_dim(a["ada_b"], me * ncol, ncol, axis=1)[:, None, :]
    mod_cols = _ada_fwd(c_all, a["ada_w"], ada_b_cols)
    got, = _all_gather([mod_cols.reshape(DEPTH * N_DEV, ncol)], in_vmem=True, name="gather_mod")
    got = got.reshape(N_DEV, DEPTH, N_DEV, ncol)
    mod = lax.dynamic_index_in_dim(got, me, axis=2, keepdims=False)
    mod = jnp.transpose(mod, (1, 0, 2)).reshape(DEPTH, 9, D)

    shards = {n: a[n] for n in BIG}
    shards["mix_w_in"] = _pack_cols(a["mix_w_in"])
    def as_weights(names, got):
        return {n: (t if n in COL_SHARDED else t.reshape(-1, t.shape[-1])) for n, t in zip(names, got)}

    def group_shards(l, gn):
        return [shards[n][l].astype(bf16) for n in GROUPS[gn]]

    w_first = _all_gather(group_shards(0, "ffn1"), in_vmem=False, name="gather_weights")
    w_state, after = {}, [w_first[0]]
    for l, gn in [(0, "mix"), (0, "ffn2"), (1, "ffn1"), (1, "mix"), (1, "ffn2")]:
        w_state[l, gn], token = _spread_start(group_shards(l, gn), scatter=False, after=after,
                                              name=f"weights_{l}{gn}_start")
        after = [token]

    def weights_of(l, gn, after):
        if (l, gn) == (0, "ffn1"):
            return as_weights(GROUPS[gn], w_first)
        return as_weights(GROUPS[gn], _spread_wait(w_state[l, gn], after, scatter=False, name=f"weights_{l}{gn}_wait"))

    g_state = {}

    last_send = {}

    def grads_done(l, gn, gw, after=()):
        if (l, gn) == (0, "ffn1_w_in") and not after:
            last_send.update(gw)
            return None
        srcs = [t.reshape((N_DEV,) + shards[n].shape[1:]) for n, t in gw.items()]
        g_state[l, gn], token = _spread_start(srcs, scatter=True, after=after, name=f"grads_{l}{gn}_start")
        return token

    p = {n: a[n] for n in ("ln_in_g", "ln_in_b", "ssd_conv_b", "ssd_dt_bias", "ssd_a_log", "ssd_d", "ssd_norm_g",
                           "fox_f_bias")}
    p.update(full)
    p["mod"] = mod + after[0][0, 0]

    loss_local, grad_x, g = _local_step(a["x"][0], a["loss_target"][0], p, weights_of, grads_done)
    loss = lax.psum(loss_local, ("x", "y", "c"))

    small_names = ["mod", "ln_in_g", "ln_in_b", "ssd_conv_b", "ssd_dt_bias", "ssd_a_log", "ssd_d", "ssd_norm_g",
                   "fox_f_bias", "ln_g", "ln_b", "ssd_conv_w", "sconv_w"]
    buf, offs = _pack_rows([g[n] for n in small_names])
    got, = _all_gather([buf], in_vmem=True, name="gather_small_grads")
    tot = _sum_slots(got, tr=buf.shape[0], name="sum_small_grads")
    grads_done(0, "ffn1_w_in", last_send, after=(tot,))
    grads = {}
    for k, n in enumerate(small_names[1:], start=1):
        t = _take(tot, offs[k], g[n].shape)
        if n in SMALL_SHARDED:
            w_ = SMALL_SHARDED[n]
            t = lax.dynamic_slice_in_dim(t, me * w_, w_, axis=2)
        grads[n] = t
    grads["ada_b"] = _take(tot, offs[0], (DEPTH, 9 * D))
    dmod_all = _take(got, offs[0], (DEPTH, 9 * D))
    dmod_cols = jnp.transpose(lax.dynamic_slice_in_dim(dmod_all, me * ncol, ncol, axis=2), (1, 0, 2))
    grads["ada_w"] = _ada_bwd(c_all, dmod_cols)

    delta, new_m, new_v = {}, {}, {}

    def adamw(n):
        sh = a[n].shape
        two = lambda t: t.reshape(-1, sh[-1])
        outs = _adamw(two(a[n]), two(grads[n]), two(a["m_" + n]), two(a["v_" + n]), tr=ADAM_TR[n], name="adamw_" + n)
        delta[n], new_m[n], new_v[n] = (t.reshape(sh) for t in outs)

    small_params = [n for n in WEIGHTS if n not in ADAM_TR]
    packs = [_pack_rows([src[pre + n] for n in small_params])[0]
             for src, pre in ((a, ""), (grads, ""), (a, "m_"), (a, "v_"))]
    _, offs = _pack_rows([a[n] for n in small_params])
    outs = _adamw(*packs, tr=packs[0].shape[0], name="adamw_small")
    for k, n in enumerate(small_params):
        delta[n], new_m[n], new_v[n] = (_take(t, offs[k], a[n].shape) for t in outs)
    adamw("ada_w")
    after = grad_x
    for gn, names in [("ffn2_w_out", ["ffn2_w_out"]), ("ffn2_w_in", ["ffn2_w_in"]), ("mix", GROUPS["mix"]),
                      ("ffn1_w_out", ["ffn1_w_out"]), ("ffn1_w_in", ["ffn1_w_in"])]:
        lands = [_spread_wait(g_state[l, gn], after if l == 0 else grad_x, scatter=True, name=f"grads_{l}{gn}_wait")
                 for l in range(DEPTH)]
        for k, n in enumerate(names):
            if n == "mix_w_in":
                sums = [_sum_slots(t[k], tr=LAND_TR[n], name="sum_devices") for t in lands]
                grads[n] = _unpack_cols(jnp.stack(sums))
                adamw(n)
            else:
                grads[n], delta[n], new_m[n], new_v[n] = _adamw_landed(
                    a[n], a["m_" + n], a["v_" + n], lands[0][k], lands[1][k], tr=LAND_TR[n], name="adamw_" + n)
            after = delta[n]

    return (loss, grad_x[None], *[grads[n] for n in WEIGHTS], *[delta[n] for n in WEIGHTS],
            *[new_m[n] for n in WEIGHTS], *[new_v[n] for n in WEIGHTS])
```

```python
import functools

import jax
import jax.numpy as jnp
from jax import lax
from jax.experimental import pallas as pl
from jax.experimental.pallas import tpu as pltpu

f32, bf16 = jnp.float32, jnp.bfloat16

D = 1024
F = 2816
DEPTH = 2
N_DEV = 8
SSD_W, SSD_HD, SSD_H, SSD_G, SSD_N, SSD_K = 512, 64, 8, 2, 128, 4
FOX_W, FOX_HD, FOX_H = 256, 64, 4
SC_W, SC_K = 256, 3
ALPHA = (2 * DEPTH) ** 0.25
LN_EPS = 1e-5
RMS_EPS = 1e-5
P_XBC, P_Z, P_Q, P_K, P_V, P_SB, P_SC, P_SX, P_SM = 0, 1024, 1536, 1792, 2048, 2304, 2560, 2816, 3072
P_W = 3200
SM_DT, SM_F = 0, 8
ADAM_LR, ADAM_B1, ADAM_B2, ADAM_EPS, ADAM_WD, ADAM_STEP = 0.001, 0.9, 0.999, 1e-08, 0.01, 10

VMEM_LIMIT = 56 * 1024 * 1024


def _cp(sem=None):
    return pltpu.CompilerParams(dimension_semantics=sem, vmem_limit_bytes=VMEM_LIMIT)


def _const_spec(shape):
    nd = len(shape)
    return pl.BlockSpec(shape, lambda *_: (0,) * nd, pipeline_mode=pl.Buffered(1))


def _sigmoid(x):
    return 1.0 / (1.0 + jnp.exp(-x))


def _ln_fwd(u, g, b):
    mu = jnp.mean(u, -1, keepdims=True)
    xc = u - mu
    rstd = lax.rsqrt(jnp.mean(xc * xc, -1, keepdims=True) + LN_EPS)
    xhat = xc * rstd
    return xhat * g + b, xhat, rstd


def _ln_bwd(dout, xhat, rstd, g):
    dxh = dout * g
    m1 = jnp.mean(dxh, -1, keepdims=True)
    m2 = jnp.mean(dxh * xhat, -1, keepdims=True)
    du = rstd * (dxh - m1 - xhat * m2)
    return du, jnp.sum(dout * xhat, 0, keepdims=True), jnp.sum(dout, 0, keepdims=True)


def _dot(a, b):
    return jnp.dot(a, b, preferred_element_type=f32)


def _dot_nt(a, b):
    return lax.dot_general(a, b, (((1,), (1,)), ((), ())), preferred_element_type=f32)


def _dot_tn(a, b):
    return lax.dot_general(a, b, (((0,), (0,)), ((), ())), preferred_element_type=f32)


def _dot_hi(a, b):
    return jnp.dot(a, b, preferred_element_type=f32, precision=lax.Precision.HIGHEST)


def _shift_down(cur, prev, s):
    if s == 0:
        return cur
    row = lax.broadcasted_iota(jnp.int32, cur.shape, 0)
    return jnp.where(row < s, pltpu.roll(prev, s, 0), pltpu.roll(cur, s, 0))


def _shift_up(cur, nxt, s):
    if s == 0:
        return cur
    t = cur.shape[0]
    row = lax.broadcasted_iota(jnp.int32, cur.shape, 0)
    return jnp.where(row >= t - s, pltpu.roll(nxt, t - s, 0), pltpu.roll(cur, t - s, 0))


def _ln_in_fwd(x, gb, *, tt=512):
    L = x.shape[0]

    def body(x_ref, gb_ref, o_ref):
        o_ref[...] = _ln_fwd(x_ref[...], gb_ref[0:1, :], gb_ref[1:2, :])[0]

    return pl.pallas_call(
        body, name="ln_in_fwd", grid=(L // tt,),
        in_specs=[pl.BlockSpec((tt, D), lambda i: (i, 0)), _const_spec((8, D))],
        out_specs=pl.BlockSpec((tt, D), lambda i: (i, 0)),
        out_shape=jax.ShapeDtypeStruct((L, D), f32), compiler_params=_cp(("parallel",)))(x, gb)


def _ln_in_bwd(x, dy, gb, *, tt=512):
    L = x.shape[0]

    def body(x_ref, dy_ref, gb_ref, dx_ref, acc_ref):
        @pl.when(pl.program_id(0) == 0)
        def _():
            acc_ref[...] = jnp.zeros_like(acc_ref)
        _, xhat, rstd = _ln_fwd(x_ref[...], gb_ref[0:1, :], gb_ref[1:2, :])
        du, dg, db = _ln_bwd(dy_ref[...], xhat, rstd, gb_ref[0:1, :])
        dx_ref[...] = du
        acc_ref[0:1, :] += dg
        acc_ref[1:2, :] += db

    return pl.pallas_call(
        body, name="ln_in_bwd", grid=(L // tt,),
        in_specs=[pl.BlockSpec((tt, D), lambda i: (i, 0)), pl.BlockSpec((tt, D), lambda i: (i, 0)), _const_spec((8, D))],
        out_specs=[pl.BlockSpec((tt, D), lambda i: (i, 0)), pl.BlockSpec((8, D), lambda i: (0, 0))],
        out_shape=[jax.ShapeDtypeStruct((L, D), f32), jax.ShapeDtypeStruct((8, D), f32)],
        compiler_params=_cp(("arbitrary",)))(x, dy, gb)


def _loss_head(y, tgt, *, tt=512):
    L = y.shape[0]

    def body(y_ref, t_ref, dy_ref, acc_ref):
        @pl.when(pl.program_id(0) == 0)
        def _():
            acc_ref[...] = jnp.zeros_like(acc_ref)
        e = y_ref[...] - t_ref[...]
        dy_ref[...] = e * (1.0 / D)
        acc_ref[...] += 0.5 * jnp.sum(jnp.mean(e * e, -1, keepdims=True))

    return pl.pallas_call(
        body, name="loss_head", grid=(L // tt,),
        in_specs=[pl.BlockSpec((tt, D), lambda i: (i, 0)), pl.BlockSpec((tt, D), lambda i: (i, 0))],
        out_specs=[pl.BlockSpec((tt, D), lambda i: (i, 0)), pl.BlockSpec((8, 128), lambda i: (0, 0))],
        out_shape=[jax.ShapeDtypeStruct((L, D), f32), jax.ShapeDtypeStruct((8, 128), f32)],
        compiler_params=_cp(("arbitrary",)))(y, tgt)


FFN_CH = 4
FS = F // FFN_CH


def _ffn_fwd(x, mv, w_in, w_out, *, tt=256):
    L = x.shape[0]

    def body(x_ref, mv_ref, wi_ref, wo_ref, o_ref):
        x = x_ref[...]
        h = (x * (1.0 + mv_ref[1:2, :]) + mv_ref[0:1, :]).astype(bf16)
        y = jnp.zeros((tt, D), f32)
        for c in range(FFN_CH):
            g = _dot_nt(h, wi_ref[c])
            u = _dot_nt(h, wi_ref[c + FFN_CH])
            act = (g * _sigmoid(g) * u).astype(bf16)
            y = y + _dot(act, wo_ref[c * FS:(c + 1) * FS, :])
        uu = ALPHA * x + (0.5 * mv_ref[2:3, :]) * y
        o_ref[...] = _ln_fwd(uu, mv_ref[3:4, :], mv_ref[4:5, :])[0]

    return pl.pallas_call(
        body, name="ffn_fwd", grid=(L // tt,),
        in_specs=[pl.BlockSpec((tt, D), lambda i: (i, 0)), _const_spec((8, D)),
                  _const_spec((2 * FFN_CH, FS, D)), _const_spec((F, D))],
        out_specs=pl.BlockSpec((tt, D), lambda i: (i, 0)),
        out_shape=jax.ShapeDtypeStruct((L, D), f32), compiler_params=_cp(("parallel",)))(x, mv, w_in, w_out)


def _ffn_bwd(x, dxo, mv, w_in, w_out, *, tt=256):
    L = x.shape[0]

    def body(x_ref, dxo_ref, mv_ref, wi_ref, wo_ref, dx_ref, h_ref, da_ref, act_ref, dy_ref, acc_ref, a_scr):
        @pl.when(pl.program_id(0) == 0)
        def _():
            acc_ref[...] = jnp.zeros_like(acc_ref)
        x = x_ref[...]
        scale1 = 1.0 + mv_ref[1:2, :]
        h = (x * scale1 + mv_ref[0:1, :]).astype(bf16)
        h_ref[...] = h
        y = jnp.zeros((tt, D), f32)
        for c in range(FFN_CH):
            g = _dot_nt(h, wi_ref[c])
            u = _dot_nt(h, wi_ref[c + FFN_CH])
            a_scr[c] = g
            a_scr[c + FFN_CH] = u
            act = (g * _sigmoid(g) * u).astype(bf16)
            act_ref[c] = act.T
            y = y + _dot(act, wo_ref[c * FS:(c + 1) * FS, :])
        hg = 0.5 * mv_ref[2:3, :]
        _, xhat, rstd = _ln_fwd(ALPHA * x + hg * y, mv_ref[3:4, :], mv_ref[4:5, :])
        du, dlg, dlb = _ln_bwd(dxo_ref[...], xhat, rstd, mv_ref[3:4, :])
        acc_ref[3:4, :] += dlg
        acc_ref[4:5, :] += dlb
        acc_ref[2:3, :] += jnp.sum(0.5 * y * du, 0, keepdims=True)
        dyb = (hg * du).astype(bf16)
        dy_ref[...] = dyb
        dh = jnp.zeros((tt, D), f32)
        for c in range(FFN_CH):
            g = a_scr[c]
            u = a_scr[c + FFN_CH]
            dact = _dot_nt(dyb, wo_ref[c * FS:(c + 1) * FS, :])
            s = _sigmoid(g)
            dg = (dact * u * (s * (1.0 + g * (1.0 - s)))).astype(bf16)
            dup = (dact * (g * s)).astype(bf16)
            da_ref[c] = dg.T
            da_ref[c + FFN_CH] = dup.T
            dh = dh + _dot(dg, wi_ref[c])
            dh = dh + _dot(dup, wi_ref[c + FFN_CH])
        dx_ref[...] = ALPHA * du + dh * scale1
        acc_ref[0:1, :] += jnp.sum(dh, 0, keepdims=True)
        acc_ref[1:2, :] += jnp.sum(dh * x, 0, keepdims=True)

    tok = lambda w: pl.BlockSpec((tt, w), lambda i: (i, 0))
    by_chunk = lambda n: pl.BlockSpec((n, FS, tt), lambda i: (0, 0, i))
    return pl.pallas_call(
        body, name="ffn_bwd", grid=(L // tt,),
        in_specs=[tok(D), tok(D), _const_spec((8, D)), _const_spec((2 * FFN_CH, FS, D)), _const_spec((F, D))],
        out_specs=[tok(D), tok(D), by_chunk(2 * FFN_CH), by_chunk(FFN_CH), tok(D), pl.BlockSpec((8, D), lambda i: (0, 0))],
        out_shape=[jax.ShapeDtypeStruct((L, D), f32), jax.ShapeDtypeStruct((L, D), bf16),
                   jax.ShapeDtypeStruct((2 * FFN_CH, FS, L), bf16), jax.ShapeDtypeStruct((FFN_CH, FS, L), bf16),
                   jax.ShapeDtypeStruct((L, D), bf16), jax.ShapeDtypeStruct((8, D), f32)],
        scratch_shapes=[pltpu.VMEM((2 * FFN_CH, tt, FS), f32)],
        compiler_params=_cp(("arbitrary",)))(x, dxo, mv, w_in, w_out)


DW_TK = 4096


def _matmul_tokens(a, b, *, tn, tk, name, after=None):
    ga, gb = a.ndim == 3, b.ndim == 3
    extra = [] if after is None else [after]
    G = a.shape[0] if ga else (b.shape[0] if gb else 1)
    M, K = a.shape[-2:]
    N = b.shape[-1]
    tk = min(tk, K)
    nk = K // tk

    def body(a_ref, b_ref, *rest):
        o_ref, acc = rest[len(extra):]
        k = pl.program_id(2)
        p = _dot(a_ref[...], b_ref[...])

        @pl.when(k == 0)
        def _():
            acc[...] = p

        @pl.when(k > 0)
        def _():
            acc[...] += p

        @pl.when(k == nk - 1)
        def _():
            o_ref[...] = acc[...].astype(bf16)

    a_spec = (pl.BlockSpec((None, M, tk), lambda g, j, k: (g, 0, k)) if ga
              else pl.BlockSpec((M, tk), lambda g, j, k: (0, k)))
    b_spec = (pl.BlockSpec((None, tk, tn), lambda g, j, k: (g, k, j)) if gb
              else pl.BlockSpec((tk, tn), lambda g, j, k: (k, j)))
    if ga or gb:
        o_spec, o_shape = pl.BlockSpec((None, M, tn), lambda g, j, k: (g, 0, j)), (G, M, N)
    else:
        o_spec, o_shape = pl.BlockSpec((M, tn), lambda g, j, k: (0, j)), (M, N)
    return pl.pallas_call(
        body, name=name, grid=(G, N // tn, nk), in_specs=[a_spec, b_spec] + [ANY] * len(extra), out_specs=o_spec,
        out_shape=jax.ShapeDtypeStruct(o_shape, bf16), scratch_shapes=[pltpu.VMEM((M, tn), f32)],
        compiler_params=_cp(("parallel", "parallel", "arbitrary")))(a, b, *extra)


def _inproj_fwd(x, mv, w, *, tt=512):
    L = x.shape[0]

    def body(x_ref, mv_ref, w_ref, o_ref):
        h = (x_ref[...] * (1.0 + mv_ref[1:2, :]) + mv_ref[0:1, :]).astype(bf16)
        o_ref[...] = _dot(h, w_ref[...])

    return pl.pallas_call(
        body, name="inproj_fwd", grid=(L // tt,),
        in_specs=[pl.BlockSpec((tt, D), lambda i: (i, 0)), _const_spec((8, D)), _const_spec((D, P_W))],
        out_specs=pl.BlockSpec((tt, P_W), lambda i: (i, 0)),
        out_shape=jax.ShapeDtypeStruct((L, P_W), f32), compiler_params=_cp(("parallel",)))(x, mv, w)


def _inproj_bwd(x, dx_part, dproj, mv, w, *, tt=512):
    L = x.shape[0]

    def body(x_ref, dxp_ref, dp_ref, mv_ref, w_ref, dx_ref, h_ref, acc_ref):
        @pl.when(pl.program_id(0) == 0)
        def _():
            acc_ref[...] = jnp.zeros_like(acc_ref)
        x = x_ref[...]
        scale1 = 1.0 + mv_ref[1:2, :]
        h_ref[...] = (x * scale1 + mv_ref[0:1, :]).astype(bf16).T
        dh = _dot_nt(dp_ref[...], w_ref[...])
        dx_ref[...] = dxp_ref[...] + dh * scale1
        acc_ref[0:1, :] += jnp.sum(dh, 0, keepdims=True)
        acc_ref[1:2, :] += jnp.sum(dh * x, 0, keepdims=True)

    tok = lambda w_: pl.BlockSpec((tt, w_), lambda i: (i, 0))
    return pl.pallas_call(
        body, name="inproj_bwd", grid=(L // tt,),
        in_specs=[tok(D), tok(D), tok(P_W), _const_spec((8, D)), _const_spec((D, P_W))],
        out_specs=[tok(D), pl.BlockSpec((D, tt), lambda i: (0, i)), pl.BlockSpec((8, D), lambda i: (0, 0))],
        out_shape=[jax.ShapeDtypeStruct((L, D), f32), jax.ShapeDtypeStruct((D, L), bf16),
                   jax.ShapeDtypeStruct((8, D), f32)],
        compiler_params=_cp(("arbitrary",)))(x, dx_part, dproj, mv, w)


def _outproj_fwd(x, ycat, mv, w, *, tt=512):
    L = x.shape[0]

    def body(x_ref, y_ref, mv_ref, w_ref, o_ref):
        y = _dot(y_ref[...], w_ref[...])
        uu = ALPHA * x_ref[...] + mv_ref[2:3, :] * y
        o_ref[...] = _ln_fwd(uu, mv_ref[3:4, :], mv_ref[4:5, :])[0]

    tok = lambda w_: pl.BlockSpec((tt, w_), lambda i: (i, 0))
    return pl.pallas_call(
        body, name="outproj_fwd", grid=(L // tt,),
        in_specs=[tok(D), tok(D), _const_spec((8, D)), _const_spec((D, D))],
        out_specs=tok(D),
        out_shape=jax.ShapeDtypeStruct((L, D), f32), compiler_params=_cp(("parallel",)))(x, ycat, mv, w)


def _outproj_bwd(x, ycat, dxo, mv, w, *, tt=512):
    L = x.shape[0]

    def body(x_ref, y_ref, dxo_ref, mv_ref, w_ref, dx_ref, dy_ref, dyc_ref, yt_ref, acc_ref):
        @pl.when(pl.program_id(0) == 0)
        def _():
            acc_ref[...] = jnp.zeros_like(acc_ref)
        yt_ref[...] = y_ref[...].T
        y = _dot(y_ref[...], w_ref[...])
        gate = mv_ref[2:3, :]
        _, xhat, rstd = _ln_fwd(ALPHA * x_ref[...] + gate * y, mv_ref[3:4, :], mv_ref[4:5, :])
        du, dlg, dlb = _ln_bwd(dxo_ref[...], xhat, rstd, mv_ref[3:4, :])
        acc_ref[3:4, :] += dlg
        acc_ref[4:5, :] += dlb
        acc_ref[2:3, :] += jnp.sum(y * du, 0, keepdims=True)
        dx_ref[...] = ALPHA * du
        dyb = (gate * du).astype(bf16)
        dy_ref[...] = dyb
        dyc_ref[...] = _dot_nt(dyb, w_ref[...])

    tok = lambda w_: pl.BlockSpec((tt, w_), lambda i: (i, 0))
    return pl.pallas_call(
        body, name="outproj_bwd", grid=(L // tt,),
        in_specs=[tok(D), tok(D), tok(D), _const_spec((8, D)), _const_spec((D, D))],
        out_specs=[tok(D), tok(D), tok(D), pl.BlockSpec((D, tt), lambda i: (0, i)), pl.BlockSpec((8, D), lambda i: (0, 0))],
        out_shape=[jax.ShapeDtypeStruct((L, D), f32), jax.ShapeDtypeStruct((L, D), bf16),
                   jax.ShapeDtypeStruct((L, D), f32), jax.ShapeDtypeStruct((D, L), bf16),
                   jax.ShapeDtypeStruct((8, D), f32)],
        compiler_params=_cp(("arbitrary",)))(x, ycat, dxo, mv, w)


def _sconv_fwd(proj, w, *, tt=512):
    L = proj.shape[0]
    cb = SC_W

    def body(b_ref, c_ref, x_ref, cp_ref, xp_ref, w_ref, o_ref):
        first = jnp.where(pl.program_id(0) > 0, 1.0, 0.0)
        u = c_ref[...] * x_ref[...]
        up = cp_ref[...] * xp_ref[...] * first
        v = w_ref[2:3, :] * u + w_ref[1:2, :] * _shift_down(u, up, 1) + w_ref[0:1, :] * _shift_down(u, up, 2)
        o_ref[...] = (b_ref[...] * v).astype(bf16)

    cur = lambda col: pl.BlockSpec((tt, cb), lambda i: (i, col // cb))
    prev = lambda col: pl.BlockSpec((tt, cb), lambda i: (jnp.maximum(i - 1, 0), col // cb))
    return pl.pallas_call(
        body, name="sconv_fwd", grid=(L // tt,),
        in_specs=[cur(P_SB), cur(P_SC), cur(P_SX), prev(P_SC), prev(P_SX), _const_spec((8, cb))],
        out_specs=pl.BlockSpec((tt, cb), lambda i: (i, 0)),
        out_shape=jax.ShapeDtypeStruct((L, cb), bf16), compiler_params=_cp(("parallel",)))(proj, proj, proj, proj, proj, w)


def _sconv_bwd(proj, dycat, w, *, tt=512):
    L = proj.shape[0]
    cb = SC_W
    n = L // tt

    def body(b_ref, c_ref, x_ref, cp_ref, xp_ref, bn_ref, dy_ref, dyn_ref, w_ref, db_ref, dc_ref, dx_ref, acc_ref):
        i = pl.program_id(0)

        @pl.when(i == 0)
        def _():
            acc_ref[...] = jnp.zeros_like(acc_ref)
        first = jnp.where(i > 0, 1.0, 0.0)
        last = jnp.where(i < n - 1, 1.0, 0.0)
        cg, xin, bg = c_ref[...], x_ref[...], b_ref[...]
        u = cg * xin
        up = cp_ref[...] * xp_ref[...] * first
        u1, u2 = _shift_down(u, up, 1), _shift_down(u, up, 2)
        v = w_ref[2:3, :] * u + w_ref[1:2, :] * u1 + w_ref[0:1, :] * u2
        dy = dy_ref[...]
        db_ref[...] = (dy * v).astype(bf16)
        dv = dy * bg
        dvn = dyn_ref[...] * bn_ref[...] * last
        du = w_ref[2:3, :] * dv + w_ref[1:2, :] * _shift_up(dv, dvn, 1) + w_ref[0:1, :] * _shift_up(dv, dvn, 2)
        acc_ref[2:3, :] += jnp.sum(dv * u, 0, keepdims=True)
        acc_ref[1:2, :] += jnp.sum(dv * u1, 0, keepdims=True)
        acc_ref[0:1, :] += jnp.sum(dv * u2, 0, keepdims=True)
        dc_ref[...] = (du * xin).astype(bf16)
        dx_ref[...] = (du * cg).astype(bf16)

    cur = lambda col: pl.BlockSpec((tt, cb), lambda i: (i, col // cb))
    prev = lambda col: pl.BlockSpec((tt, cb), lambda i: (jnp.maximum(i - 1, 0), col // cb))
    nxt = lambda col: pl.BlockSpec((tt, cb), lambda i: (jnp.minimum(i + 1, n - 1), col // cb))
    ycol = SSD_W + FOX_W
    out = pl.BlockSpec((tt, cb), lambda i: (i, 0))
    return pl.pallas_call(
        body, name="sconv_bwd", grid=(n,),
        in_specs=[cur(P_SB), cur(P_SC), cur(P_SX), prev(P_SC), prev(P_SX), nxt(P_SB), cur(ycol), nxt(ycol),
                  _const_spec((8, cb))],
        out_specs=[out, out, out, pl.BlockSpec((8, cb), lambda i: (0, 0))],
        out_shape=[jax.ShapeDtypeStruct((L, cb), bf16)] * 3 + [jax.ShapeDtypeStruct((8, cb), f32)],
        compiler_params=_cp(("arbitrary",)))(proj, proj, proj, proj, proj, proj, dycat, dycat, w)


def _log1pexp(x):
    return jnp.log(1.0 + jnp.exp(-jnp.abs(x)))


NEG = -1e30
FOX_SCALE = FOX_HD ** -0.5


HL = 128
AW = FOX_H * HL


def _np_place(rows, cols, pairs, dtype):
    import numpy as np
    m = np.zeros((rows, cols), np.float32)
    for r, c in pairs:
        m[r, c] = 1.0
    return jnp.asarray(m, dtype)


def _fox_consts():
    data = [(h * FOX_HD + d, h * HL + d) for h in range(FOX_H) for d in range(FOX_HD)]
    return dict(
        pq=_np_place(FOX_W, AW, data, bf16),
        pqt=_np_place(AW, FOX_W, [(c, r) for r, c in data], bf16),
        cum_a=[_np_place(128, AW, [(SM_F + h, h * HL + 64 + r) for h in range(FOX_H)], bf16) for r in range(3)],
        head_a=[_np_place(128, AW, [(h, h * HL + 64 + r) for h in range(FOX_H)], bf16) for r in range(3)],
        head_b=[_np_place(128, AW, [(h, h * HL + 67 + r) for h in range(FOX_H)], bf16) for r in range(3)],
        group=_np_place(FOX_W, 128, [(h * FOX_HD + d, h) for h in range(FOX_H) for d in range(FOX_HD)], f32),
        col_a=_np_place(AW, 128, [(h * HL + 64, SM_F + h) for h in range(FOX_H)], f32))


def _split3(x):
    hi = x.astype(bf16)
    r1 = x - hi.astype(f32)
    mid = r1.astype(bf16)
    return hi, mid, (r1 - mid.astype(f32)).astype(bf16)


def _slot_ones(tt, first):
    lane = lax.broadcasted_iota(jnp.int32, (tt, AW), 1) % HL
    return jnp.where((lane >= first) & (lane < first + 3), 1.0, 0.0)


def _fox_prep(proj, hp, cst, *, tt=256):
    L = proj.shape[0]

    def body(q_ref, k_ref, v_ref, sm_ref, hp_ref, pq_ref, c0_ref, c1_ref, c2_ref, qa_ref, ka_ref, va_ref, tot_ref):
        xx = sm_ref[...] + hp_ref[3:4, :]
        logf = jnp.minimum(xx, 0.0) - _log1pexp(xx)
        r = lax.broadcasted_iota(jnp.int32, (tt, tt), 0)
        c = lax.broadcasted_iota(jnp.int32, (tt, tt), 1)
        cum = _dot_hi(jnp.where(r >= c, 1.0, 0.0), logf)
        tot_ref[...] = jnp.broadcast_to(cum[tt - 1:tt, :], (8, 128))
        parts = _split3(-cum)
        pq = pq_ref[...]
        a_ones, b_ones = _slot_ones(tt, 64), _slot_ones(tt, 67)
        qa_ref[...] = (_dot((q_ref[...] * FOX_SCALE).astype(bf16), pq) + a_ones).astype(bf16)
        ka = _dot(k_ref[...].astype(bf16), pq) + b_ones
        for part, c_ref in zip(parts, (c0_ref, c1_ref, c2_ref)):
            ka = ka + _dot(part, c_ref[...])
        ka_ref[...] = ka.astype(bf16)
        va_ref[...] = (_dot(v_ref[...].astype(bf16), pq) + a_ones).astype(bf16)

    col = lambda c_: pl.BlockSpec((tt, FOX_W), lambda i: (i, c_ // FOX_W))
    out = pl.BlockSpec((tt, AW), lambda i: (i, 0))
    return pl.pallas_call(
        body, name="fox_prep", grid=(L // tt,),
        in_specs=[col(P_Q), col(P_K), col(P_V), pl.BlockSpec((tt, 128), lambda i: (i, P_SM // 128)),
                  _const_spec((8, 128)), _const_spec((FOX_W, AW))] + [_const_spec((128, AW))] * 3,
        out_specs=[out, out, out, pl.BlockSpec((8, 128), lambda i: (i, 0))],
        out_shape=[jax.ShapeDtypeStruct((L, AW), bf16)] * 3 + [jax.ShapeDtypeStruct((8 * (L // tt), 128), f32)],
        compiler_params=_cp(("parallel",)))(proj, proj, proj, proj, hp, cst["pq"], *cst["cum_a"])


def _fox_attn_fwd(qa, ka, va, tot, *, tq=256):
    L = qa.shape[0]

    def body(qa_ref, ka_ref, va_ref, tot_ref, o_ref, oa_ref, lse_ref):
        i = pl.program_id(0)
        row = lax.broadcasted_iota(jnp.int32, (tq, tq), 0)
        col = lax.broadcasted_iota(jnp.int32, (tq, tq), 1)
        diag_bias = jnp.where(row >= col, 0.0, NEG)

        def block(j, carry, t_j, bias):
            r0 = pl.multiple_of(j * tq, tq)
            out = []
            for h in range(FOX_H):
                hl = slice(h * HL, (h + 1) * HL)
                m, acc = carry[h]
                m_in = m - t_j[h]
                s = _dot_nt(qa_ref[:, hl], ka_ref[pl.ds(r0, tq), hl])
                if bias is not None:
                    s = s + bias
                m_new = jnp.maximum(m_in, jnp.max(s, -1, keepdims=True))
                p = jnp.exp(s - m_new)
                al = jnp.exp(m_in - m_new)
                pb = p.astype(bf16)
                p_lo = (p - pb.astype(f32)).astype(bf16)
                vj = va_ref[pl.ds(r0, tq), hl]
                out.append((m_new, al * acc + (_dot(pb, vj) + _dot(p_lo, vj))))
            return tuple(out)

        def step(k, state):
            carry, gap = state
            j = i - 1 - k
            t_j = [tot_ref[j, h] for h in range(FOX_H)]
            return block(j, carry, t_j, None), tuple(g + t for g, t in zip(gap, t_j))

        init = tuple((jnp.full((tq, 1), NEG, f32), jnp.zeros((tq, HL), f32)) for _ in range(FOX_H))
        zero_gap = tuple(jnp.zeros((), f32) for _ in range(FOX_H))
        carry = block(i, init, zero_gap, diag_bias)
        carry, gap = lax.fori_loop(0, i, step, (carry, zero_gap))
        lane = lax.broadcasted_iota(jnp.int32, (tq, 128), 1)
        lse_all = jnp.zeros((tq, 128), f32)
        for h in range(FOX_H):
            hs = slice(h * FOX_HD, (h + 1) * FOX_HD)
            m, acc = carry[h]
            l = acc[:, FOX_HD:FOX_HD + 1]
            o = acc[:, :FOX_HD] * (1.0 / l)
            o_ref[:, hs] = o.astype(bf16)
            oa_ref[:, hs] = o
            lse_all = jnp.where(lane == h, m + gap[h] + jnp.log(l), lse_all)
        lse_ref[...] = lse_all

    full = pl.BlockSpec((L, AW), lambda i: (0, 0), pipeline_mode=pl.Buffered(1))
    return pl.pallas_call(
        body, name="fox_fwd", grid=(L // tq,),
        in_specs=[pl.BlockSpec((tq, AW), lambda i: (i, 0)), full, full, pl.BlockSpec(memory_space=pltpu.SMEM)],
        out_specs=[pl.BlockSpec((tq, FOX_W), lambda i: (i, 0)), pl.BlockSpec((tq, FOX_W), lambda i: (i, 0)),
                   pl.BlockSpec((tq, 128), lambda i: (i, 0))],
        out_shape=[jax.ShapeDtypeStruct((L, FOX_W), bf16), jax.ShapeDtypeStruct((L, FOX_W), f32),
                   jax.ShapeDtypeStruct((L, 128), f32)],
        compiler_params=_cp(("parallel",)))(qa, ka, va, tot)


def _fox_bprep(qa, dycat, o_acc, lse, cst, *, tt=256):
    L = qa.shape[0]

    def body(qa_ref, do_ref, oa_ref, lse_ref, pq_ref, g_ref, a0, a1, a2, b0, b1, b2, qb_ref, doa_ref, qbt_ref, doat_ref):
        dob = do_ref[...].astype(bf16)
        delta = _dot_hi(dob.astype(f32) * oa_ref[...], g_ref[...])
        doa = _dot(dob, pq_ref[...])
        for part, ref in zip(_split3(-delta), (a0, a1, a2)):
            doa = doa + _dot(part, ref[...])
        qb = qa_ref[...].astype(f32)
        for part, ref in zip(_split3(-lse_ref[...]), (b0, b1, b2)):
            qb = qb + _dot(part, ref[...])
        doa, qb = doa.astype(bf16), qb.astype(bf16)
        doa_ref[...] = doa
        qb_ref[...] = qb
        doat_ref[...] = doa.T
        qbt_ref[...] = qb.T

    tok = lambda w_: pl.BlockSpec((tt, w_), lambda i: (i, 0))
    tr = pl.BlockSpec((AW, tt), lambda i: (0, i))
    return pl.pallas_call(
        body, name="fox_bprep", grid=(L // tt,),
        in_specs=[tok(AW), pl.BlockSpec((tt, FOX_W), lambda i: (i, SSD_W // FOX_W)), tok(FOX_W), tok(128),
                  _const_spec((FOX_W, AW)), _const_spec((FOX_W, 128))] + [_const_spec((128, AW))] * 6,
        out_specs=[tok(AW), tok(AW), tr, tr],
        out_shape=[jax.ShapeDtypeStruct((L, AW), bf16)] * 2 + [jax.ShapeDtypeStruct((AW, L), bf16)] * 2,
        compiler_params=_cp(("parallel",)))(qa, dycat, o_acc, lse, cst["pq"], cst["group"], *cst["head_a"], *cst["head_b"])


def _fox_attn_bwd(ka, va, tot, qb, doa, qbt, doat, *, tq=256):
    L = ka.shape[0]
    nq = L // tq

    def body(ka_ref, va_ref, tot_ref, qb_ref, doa_ref, qbt_ref, doat_ref, dq_ref, dkt_ref, dvt_ref):
        j = pl.program_id(0)

        @pl.when(j == 0)
        def _():
            dq_ref[...] = jnp.zeros_like(dq_ref)
        row = lax.broadcasted_iota(jnp.int32, (tq, tq), 0)
        col = lax.broadcasted_iota(jnp.int32, (tq, tq), 1)
        diag_bias = jnp.where(row >= col, 0.0, NEG)

        def block(i, carry, gap, bias):
            r0 = pl.multiple_of(i * tq, tq)
            out = []
            for h in range(FOX_H):
                hl = slice(h * HL, (h + 1) * HL)
                dkt, dvt, dsum = carry[h]
                kj = ka_ref[:, hl]
                s = _dot_nt(qb_ref[pl.ds(r0, tq), hl], kj) + gap[h]
                if bias is not None:
                    s = s + bias
                p = jnp.exp(s)
                ds = p * _dot_nt(doa_ref[pl.ds(r0, tq), hl], va_ref[:, hl])
                dsb = ds.astype(bf16)
                dq_ref[pl.ds(r0, tq), hl] += _dot(dsb, kj)
                out.append((dkt + _dot(qbt_ref[hl, pl.ds(r0, tq)], dsb),
                            dvt + _dot(doat_ref[hl, pl.ds(r0, tq)], p.astype(bf16)),
                            dsum + jnp.sum(ds, 0, keepdims=True)))
            return tuple(out)

        init = tuple((jnp.zeros((HL, tq), f32), jnp.zeros((HL, tq), f32), jnp.zeros((1, tq), f32))
                     for _ in range(FOX_H))
        def step(i, state):
            carry, gap = state
            gap = tuple(g + tot_ref[i - 1, h] for h, g in enumerate(gap))
            return block(i, carry, gap, None), gap

        zero_gap = tuple(jnp.zeros((), f32) for _ in range(FOX_H))
        carry = block(j, init, zero_gap, diag_bias)
        carry, _ = lax.fori_loop(j + 1, nq, step, (carry, zero_gap))
        for h in range(FOX_H):
            hl = slice(h * HL, (h + 1) * HL)
            dkt_ref[hl, :] = carry[h][0]
            dvt_ref[hl, :] = carry[h][1]
            dkt_ref[h * HL + FOX_HD:h * HL + FOX_HD + 1, :] = carry[h][2]

    full = lambda shape: pl.BlockSpec(shape, lambda j: (0, 0), pipeline_mode=pl.Buffered(1))
    blk = pl.BlockSpec((tq, AW), lambda j: (j, 0))
    trb = pl.BlockSpec((AW, tq), lambda j: (0, j))
    return pl.pallas_call(
        body, name="fox_bwd", grid=(nq,),
        in_specs=[blk, blk, pl.BlockSpec(memory_space=pltpu.SMEM), full((L, AW)), full((L, AW)), full((AW, L)),
                  full((AW, L))],
        out_specs=[pl.BlockSpec((L, AW), lambda j: (0, 0)), trb, trb],
        out_shape=[jax.ShapeDtypeStruct((L, AW), f32), jax.ShapeDtypeStruct((AW, L), f32),
                   jax.ShapeDtypeStruct((AW, L), f32)],
        compiler_params=_cp(("arbitrary",)))(ka, va, tot, qb, doa, qbt, doat)


def _fox_post(dq, dkt, dvt, proj, ddt, hp, cst, *, tt=256):
    L = proj.shape[0]
    n = L // tt

    def body(dq_ref, dkt_ref, dvt_ref, sm_ref, ddt_ref, hp_ref, pqt_ref, ca_ref,
             dqo_ref, dko_ref, dvo_ref, dsm_ref, acc_ref, carry):
        @pl.when(pl.program_id(0) == 0)
        def _():
            carry[...] = jnp.zeros_like(carry)
            acc_ref[...] = jnp.zeros_like(acc_ref)
        pqt = pqt_ref[...]
        dk_full = dkt_ref[...].T
        dqo_ref[...] = _dot((dq_ref[...] * FOX_SCALE).astype(bf16), pqt).astype(bf16)
        dko_ref[...] = _dot(dk_full.astype(bf16), pqt).astype(bf16)
        dvo_ref[...] = _dot(dvt_ref[...].T.astype(bf16), pqt).astype(bf16)
        dc = -_dot_hi(dk_full, ca_ref[...])
        r = lax.broadcasted_iota(jnp.int32, (tt, tt), 0)
        c = lax.broadcasted_iota(jnp.int32, (tt, tt), 1)
        dl = _dot_hi(jnp.where(r <= c, 1.0, 0.0), dc) + carry[0:1, :]
        carry[0:1, :] += jnp.sum(dc, 0, keepdims=True)
        xx = sm_ref[...] + hp_ref[3:4, :]
        lane = lax.broadcasted_iota(jnp.int32, (tt, 128), 1)
        dlogit = jnp.where((lane >= SM_F) & (lane < SM_F + FOX_H), dl * _sigmoid(-xx), 0.0)
        acc_ref[3:4, :] += jnp.sum(dlogit, 0, keepdims=True)
        dsm_ref[...] = (dlogit + ddt_ref[...]).astype(bf16)

    rev = lambda w_: pl.BlockSpec((tt, w_), lambda i: (n - 1 - i, 0))
    revt = pl.BlockSpec((AW, tt), lambda i: (0, n - 1 - i))
    return pl.pallas_call(
        body, name="fox_post", grid=(n,),
        in_specs=[rev(AW), revt, revt, pl.BlockSpec((tt, 128), lambda i: (n - 1 - i, P_SM // 128)), rev(128),
                  _const_spec((8, 128)), _const_spec((AW, FOX_W)), _const_spec((AW, 128))],
        out_specs=[rev(FOX_W), rev(FOX_W), rev(FOX_W), rev(128), pl.BlockSpec((8, 128), lambda i: (0, 0))],
        out_shape=[jax.ShapeDtypeStruct((L, FOX_W), bf16)] * 3 + [jax.ShapeDtypeStruct((L, 128), bf16),
                                                                   jax.ShapeDtypeStruct((8, 128), f32)],
        scratch_shapes=[pltpu.VMEM((8, 128), f32)],
        compiler_params=_cp(("arbitrary",)))(dq, dkt, dvt, proj, ddt, hp, cst["pqt"], cst["col_a"])


SSD_GW = SSD_W // SSD_G
SSD_HPG = SSD_H // SSD_G


def _ssd_pre(x, xprev, sm, cp_ref, hp_ref, tc):
    pre = (cp_ref[4:5, :] + cp_ref[3:4, :] * x + cp_ref[2:3, :] * _shift_down(x, xprev, 1)
           + cp_ref[1:2, :] * _shift_down(x, xprev, 2) + cp_ref[0:1, :] * _shift_down(x, xprev, 3))
    sig = _sigmoid(pre)
    raw = sm + hp_ref[0:1, :]
    dt = jnp.maximum(raw, 0.0) + _log1pexp(raw)
    a_neg = -jnp.exp(hp_ref[1:2, :])
    r = lax.broadcasted_iota(jnp.int32, (tc, tc), 0)
    c = lax.broadcasted_iota(jnp.int32, (tc, tc), 1)
    cs = _dot_hi(jnp.where(r >= c, 1.0, 0.0), dt * a_neg)
    return pre, sig, raw, dt, a_neg, cs, cs.T, r >= c


def _ssd_fwd(proj, cp, hp, ng, *, tc=256):
    L = proj.shape[0]
    nc = L // tc

    def body(xc_ref, xp_ref, z_ref, sm_ref, cp_ref, hp_ref, ng_ref, y_ref, ypre_ref, sin_ref, s_scr):
        i = pl.program_id(0)

        @pl.when(i == 0)
        def _():
            s_scr[...] = jnp.zeros_like(s_scr)
        x = xc_ref[...]
        xprev = xp_ref[...] * jnp.where(i > 0, 1.0, 0.0)
        pre, sig, _, dt, _, cs, cst, tril = _ssd_pre(x, xprev, sm_ref[...], cp_ref, hp_ref, tc)
        xbc = pre * sig
        sin_ref[...] = s_scr[...]
        for g in range(SSD_G):
            bg = xbc[:, SSD_W + g * SSD_N:SSD_W + (g + 1) * SSD_N]
            cg = xbc[:, SSD_W + SSD_G * SSD_N + g * SSD_N:SSD_W + SSD_G * SSD_N + (g + 1) * SSD_N].astype(bf16)
            cb = _dot_nt(cg, bg.astype(bf16))
            for e in range(SSD_HPG):
                h = g * SSD_HPG + e
                hs = slice(h * SSD_HD, (h + 1) * SSD_HD)
                xs = xbc[:, hs]
                csc = cs[:, h:h + 1]
                lm = jnp.where(tril, jnp.exp(jnp.minimum(csc - cst[h:h + 1, :], 0.0)), 0.0)
                xdt = (xs * dt[:, h:h + 1]).astype(bf16)
                s_h = s_scr[:, hs]
                y = _dot((cb * lm).astype(bf16), xdt) + jnp.exp(csc) * _dot(cg, s_h.astype(bf16))
                ypre_ref[:, hs] = y + hp_ref[2:3, h:h + 1] * xs
                cl = cs[tc - 1:tc, h:h + 1]
                bd = (bg * jnp.exp(cl - csc)).astype(bf16)
                s_scr[:, hs] = jnp.exp(cl) * s_h + _dot_tn(bd, xdt)
        z = z_ref[...]
        yz = ypre_ref[...] * (z * _sigmoid(z))
        for g in range(SSD_G):
            gs = slice(g * SSD_GW, (g + 1) * SSD_GW)
            yg = yz[:, gs]
            r = lax.rsqrt(jnp.mean(yg * yg, -1, keepdims=True) + RMS_EPS)
            y_ref[:, gs] = (yg * r * ng_ref[0:1, gs]).astype(bf16)

    return pl.pallas_call(
        body, name="ssd_fwd", grid=(nc,),
        in_specs=[pl.BlockSpec((tc, 1024), lambda i: (i, 0)),
                  pl.BlockSpec((tc, 1024), lambda i: (jnp.maximum(i - 1, 0), 0)),
                  pl.BlockSpec((tc, SSD_W), lambda i: (i, P_Z // SSD_W)),
                  pl.BlockSpec((tc, 128), lambda i: (i, P_SM // 128)),
                  _const_spec((8, 1024)), _const_spec((8, 128)), _const_spec((8, SSD_W))],
        out_specs=[pl.BlockSpec((tc, SSD_W), lambda i: (i, 0)), pl.BlockSpec((tc, SSD_W), lambda i: (i, 0)),
                   pl.BlockSpec((SSD_N, SSD_W), lambda i: (i, 0))],
        out_shape=[jax.ShapeDtypeStruct((L, SSD_W), bf16), jax.ShapeDtypeStruct((L, SSD_W), f32),
                   jax.ShapeDtypeStruct((nc * SSD_N, SSD_W), f32)],
        scratch_shapes=[pltpu.VMEM((SSD_N, SSD_W), f32)],
        compiler_params=_cp(("arbitrary",)))(proj, proj, proj, proj, cp, hp, ng)


def _ssd_bwd(proj, dycat, ypre, sin, cp, hp, ng, *, tc=256):
    L = proj.shape[0]
    nc = L // tc

    def body(xc_ref, xp_ref, z_ref, sm_ref, cp_ref, hp_ref, ng_ref, sin_ref, ypre_ref, dy_ref,
             dxbc_ref, dz_ref, ddt_ref, acc1_ref, acc2_ref, ds_scr, dnext_scr, dxbc_scr):
        i = pl.program_id(0)
        c_idx = nc - 1 - i

        @pl.when(i == 0)
        def _():
            ds_scr[...] = jnp.zeros_like(ds_scr)
            dnext_scr[...] = jnp.zeros_like(dnext_scr)
            acc1_ref[...] = jnp.zeros_like(acc1_ref)
            acc2_ref[...] = jnp.zeros_like(acc2_ref)
        x = xc_ref[...]
        xprev = xp_ref[...] * jnp.where(c_idx > 0, 1.0, 0.0)
        pre, sig, raw, dt, a_neg, cs, cst, tril = _ssd_pre(x, xprev, sm_ref[...], cp_ref, hp_ref, tc)
        xbc = pre * sig
        z = z_ref[...]
        sz = _sigmoid(z)
        silz = z * sz
        yall = ypre_ref[...]
        yz = yall * silz
        dy = dy_ref[...]
        dyz_parts = []
        for g in range(SSD_G):
            gs = slice(g * SSD_GW, (g + 1) * SSD_GW)
            yg, dyg = yz[:, gs], dy[:, gs]
            r = lax.rsqrt(jnp.mean(yg * yg, -1, keepdims=True) + RMS_EPS)
            acc1_ref[5:6, gs] += jnp.sum(dyg * yg * r, 0, keepdims=True)
            dyn = dyg * ng_ref[0:1, gs]
            dyz_parts.append(r * (dyn - yg * (r * r) * jnp.mean(dyn * yg, -1, keepdims=True)))
        dyz = jnp.concatenate(dyz_parts, axis=1)
        dz_ref[...] = (dyz * yall * (sz * (1.0 + z * (1.0 - sz)))).astype(bf16)
        dyall = dyz * silz

        lane1 = lax.broadcasted_iota(jnp.int32, (1, 128), 1)
        sub = lax.broadcasted_iota(jnp.int32, (128, tc), 0)
        rowc = lax.broadcasted_iota(jnp.int32, (tc, 1), 0)
        dcs = jnp.zeros((tc, 128), f32)
        dcsr = jnp.zeros((128, tc), f32)
        ddt = jnp.zeros((tc, 128), f32)
        dd_row = jnp.zeros((1, 128), f32)
        for g in range(SSD_G):
            b0 = SSD_W + g * SSD_N
            c0 = SSD_W + SSD_G * SSD_N + g * SSD_N
            bg = xbc[:, b0:b0 + SSD_N]
            bgb = bg.astype(bf16)
            cgb = xbc[:, c0:c0 + SSD_N].astype(bf16)
            cb = _dot_nt(cgb, bgb)
            dbg = jnp.zeros((tc, SSD_N), f32)
            dcg = jnp.zeros((tc, SSD_N), f32)
            for e in range(SSD_HPG):
                h = g * SSD_HPG + e
                hs = slice(h * SSD_HD, (h + 1) * SSD_HD)
                oh = jnp.where(lane1 == h, 1.0, 0.0)
                xs = xbc[:, hs]
                dth = dt[:, h:h + 1]
                csc = cs[:, h:h + 1]
                lm = jnp.where(tril, jnp.exp(jnp.minimum(csc - cst[h:h + 1, :], 0.0)), 0.0)
                m = cb * lm
                xdt = (xs * dth).astype(bf16)
                s_h = sin_ref[:, hs]
                s_hb = s_h.astype(bf16)
                dyh = dyall[:, hs]
                dyb = dyh.astype(bf16)
                dd_row = dd_row + oh * jnp.sum(dyh * xs)
                dxs = hp_ref[2:3, h:h + 1] * dyh
                ecs = jnp.exp(csc)
                cs_prod = _dot(cgb, s_hb)
                dcsb = (ecs * dyh).astype(bf16)
                dcg = dcg + _dot_nt(dcsb, s_hb)
                ds_in = _dot_tn(cgb, dcsb)
                dcs_h = jnp.sum(dyh * ecs * cs_prod, -1, keepdims=True)
                dm = _dot_nt(dyb, xdt)
                w = dm * m
                dcs_h = dcs_h + jnp.sum(w, -1, keepdims=True)
                dcsr = jnp.where(sub == h, jnp.sum(w, 0, keepdims=True), dcsr)
                dcbb = (dm * lm).astype(bf16)
                dcg = dcg + _dot(dcbb, bgb)
                dbg = dbg + _dot_tn(dcbb, cgb)
                dxdt = _dot_tn(m.astype(bf16), dyb)
                dsn = ds_scr[:, hs]
                dsnb = dsn.astype(bf16)
                cl = cs[tc - 1:tc, h:h + 1]
                dec = jnp.exp(cl - csc)
                dxdt = dxdt + _dot((bg * dec).astype(bf16), dsnb)
                dbd = _dot_nt(xdt, dsnb)
                dbg = dbg + dbd * dec
                gdec = jnp.sum(dbd * bg, -1, keepdims=True) * dec
                ecl = jnp.exp(cl)
                dcl = jnp.sum(gdec) + jnp.sum(dsn * s_h) * ecl
                ds_scr[:, hs] = ecl * dsn + ds_in
                dcs_h = dcs_h - gdec + jnp.where(rowc == tc - 1, dcl, 0.0)
                dcs = dcs + dcs_h * oh
                dxbc_scr[:, hs] = dxs + dxdt * dth
                ddt = ddt + jnp.sum(dxdt * xs, -1, keepdims=True) * oh
            dxbc_scr[:, b0:b0 + SSD_N] = dbg
            dxbc_scr[:, c0:c0 + SSD_N] = dcg
        dcs = dcs - dcsr.T
        r_i = lax.broadcasted_iota(jnp.int32, (tc, tc), 0)
        c_i = lax.broadcasted_iota(jnp.int32, (tc, tc), 1)
        da = _dot_hi(jnp.where(r_i <= c_i, 1.0, 0.0), dcs)
        ddt = ddt + da * a_neg
        acc2_ref[1:2, :] += jnp.sum(da * dt, 0, keepdims=True) * a_neg
        lane = lax.broadcasted_iota(jnp.int32, (tc, 128), 1)
        ddraw = jnp.where(lane < SSD_H, ddt * _sigmoid(raw), 0.0)
        acc2_ref[0:1, :] += jnp.sum(ddraw, 0, keepdims=True)
        acc2_ref[2:3, :] += dd_row
        ddt_ref[...] = ddraw
        dpre = dxbc_scr[...] * (sig * (1.0 + pre * (1.0 - sig)))
        acc1_ref[4:5, :] += jnp.sum(dpre, 0, keepdims=True)
        for k in range(SSD_K):
            acc1_ref[k:k + 1, :] += jnp.sum(dpre * _shift_down(x, xprev, SSD_K - 1 - k), 0, keepdims=True)
        dnext = dnext_scr[...]
        dxbc_ref[...] = (cp_ref[3:4, :] * dpre + cp_ref[2:3, :] * _shift_up(dpre, dnext, 1)
                         + cp_ref[1:2, :] * _shift_up(dpre, dnext, 2)
                         + cp_ref[0:1, :] * _shift_up(dpre, dnext, 3)).astype(bf16)
        dnext_scr[...] = dpre

    rev = lambda w_, col: pl.BlockSpec((tc, w_), lambda i: (nc - 1 - i, col // w_))
    return pl.pallas_call(
        body, name="ssd_bwd", grid=(nc,),
        in_specs=[rev(1024, 0), pl.BlockSpec((tc, 1024), lambda i: (jnp.maximum(nc - 2 - i, 0), 0)),
                  rev(SSD_W, P_Z), rev(128, P_SM),
                  _const_spec((8, 1024)), _const_spec((8, 128)), _const_spec((8, SSD_W)),
                  pl.BlockSpec((SSD_N, SSD_W), lambda i: (nc - 1 - i, 0)), rev(SSD_W, 0), rev(SSD_W, 0)],
        out_specs=[rev(1024, 0), rev(SSD_W, 0), rev(128, 0),
                   pl.BlockSpec((8, 1024), lambda i: (0, 0)), pl.BlockSpec((8, 128), lambda i: (0, 0))],
        out_shape=[jax.ShapeDtypeStruct((L, 1024), bf16), jax.ShapeDtypeStruct((L, SSD_W), bf16),
                   jax.ShapeDtypeStruct((L, 128), f32), jax.ShapeDtypeStruct((8, 1024), f32),
                   jax.ShapeDtypeStruct((8, 128), f32)],
        scratch_shapes=[pltpu.VMEM((SSD_N, SSD_W), f32), pltpu.VMEM((tc, 1024), f32), pltpu.VMEM((tc, 1024), f32)],
        compiler_params=_cp(("arbitrary",)))(proj, proj, proj, proj, cp, hp, ng, sin, ypre, dycat)


def _pack_cols(w):
    pad = jnp.zeros(w.shape[:-1] + (P_W - P_SM - SSD_H - FOX_H,), w.dtype)
    return jnp.concatenate([w[..., 512:1536], w[..., 0:512], w[..., 1544:2312], w[..., 2316:3084],
                            w[..., 1536:1544], w[..., 2312:2316], pad], axis=-1)


def _unpack_cols(g):
    return jnp.concatenate([g[..., 1024:1536], g[..., 0:1024], g[..., 3072:3080], g[..., 1536:2304],
                            g[..., 3080:3084], g[..., 2304:3072]], axis=-1)


def _rows8(*rows):
    width = max(r.shape[-1] for r in rows)
    out = [jnp.pad(r.astype(f32), (0, width - r.shape[-1])) for r in rows]
    out += [jnp.zeros((width,), f32)] * (8 - len(out))
    return jnp.stack(out)


def _local_step(x, tgt, p, weights_of, grads_done):
    gb_in = _rows8(p["ln_in_g"], p["ln_in_b"])
    cst = _fox_consts()
    x0 = _ln_in_fwd(x, gb_in)
    saved = []
    for l in range(DEPTH):
        mv = [_rows8(p["mod"][l, 3 * j], p["mod"][l, 3 * j + 1], p["mod"][l, 3 * j + 2], p["ln_g"][l, j], p["ln_b"][l, j])
              for j in range(3)]
        cp = _rows8(*[p["ssd_conv_w"][l, k] for k in range(SSD_K)], p["ssd_conv_b"][l])
        hp = _rows8(jnp.pad(p["ssd_dt_bias"][l], (0, 120)), jnp.pad(p["ssd_a_log"][l], (0, 120)),
                    jnp.pad(p["ssd_d"][l], (0, 120)), jnp.pad(p["fox_f_bias"][l], (SM_F, 128 - SM_F - FOX_H)))
        ng = _rows8(p["ssd_norm_g"][l])
        scw = _rows8(*[p["sconv_w"][l, k] for k in range(SC_K)])
        w = dict(weights_of(l, "ffn1", x0))
        x1 = _ffn_fwd(x0, mv[0], w["ffn1_w_in"], w["ffn1_w_out"])
        w.update(weights_of(l, "mix", x1))
        proj = _inproj_fwd(x1, mv[1], w["mix_w_in"])
        y_ssd, ypre, sin = _ssd_fwd(proj, cp, hp, ng)
        qa, ka, va, tot = _fox_prep(proj, hp, cst)
        tot = tot[::8, SM_F:SM_F + FOX_H]
        o, o_acc, lse = _fox_attn_fwd(qa, ka, va, tot)
        y_sc = _sconv_fwd(proj, scw)
        ycat = jnp.concatenate([y_ssd, o, y_sc], axis=1)
        x2 = _outproj_fwd(x1, ycat, mv[1], w["mix_w_out"])
        w.update(weights_of(l, "ffn2", x2))
        x3 = _ffn_fwd(x2, mv[2], w["ffn2_w_in"], w["ffn2_w_out"])
        saved.append((x0, x1, x2, mv, cp, hp, ng, scw, proj, ypre, sin, qa, ka, va, tot, o_acc, lse, ycat, w))
        x0 = x3
    dx, loss_acc = _loss_head(x0, tgt)

    g = {k: [None] * DEPTH for k in (
        "mod", "ln_g", "ln_b", "ssd_conv_w", "ssd_conv_b", "ssd_dt_bias", "ssd_a_log", "ssd_d", "ssd_norm_g",
        "fox_f_bias", "sconv_w")}
    def behind(small, tok):
        return small if tok is None else small + tok[0, 0]

    token = None
    for l in reversed(range(DEPTH)):
        x0, x1, x2, mv, cp, hp, ng, scw, proj, ypre, sin, qa, ka, va, tot, o_acc, lse, ycat, w = saved[l]
        dx, h, dat, actt, dyb, a2 = _ffn_bwd(x2, dx, behind(mv[2], token), w["ffn2_w_in"], w["ffn2_w_out"])
        token = grads_done(l, "ffn2_w_out", {"ffn2_w_out": _matmul_tokens(actt, dyb, tn=D, tk=DW_TK, name="dw_ffn_out")})
        token = grads_done(l, "ffn2_w_in", {"ffn2_w_in": _matmul_tokens(dat, h, tn=D, tk=DW_TK, name="dw_ffn_in",
                                                                        after=token)})
        mv1 = behind(mv[1], token)
        dxp, dyb, dycat, ycat_t, a1 = _outproj_bwd(x1, ycat, dx, mv1, w["mix_w_out"])
        gw_mix_out = _matmul_tokens(ycat_t, dyb, tn=D // 2, tk=DW_TK, name="dw_mix_out")
        dxbc, dz, ddt, acc1, acc2 = _ssd_bwd(proj, dycat, ypre, sin, cp, hp, ng)
        qb, doa, qbt, doat = _fox_bprep(qa, dycat, o_acc, lse, cst)
        dq, dkt, dvt = _fox_attn_bwd(ka, va, tot, qb, doa, qbt, doat)
        dq, dk, dv, dsm, accf = _fox_post(dq, dkt, dvt, proj, ddt, hp, cst)
        dsb, dsc, dsx, accs = _sconv_bwd(proj, dycat, scw)
        dproj = jnp.concatenate([dxbc, dz, dq, dk, dv, dsb, dsc, dsx, dsm], axis=1)
        dx, ht, a1b = _inproj_bwd(x1, dxp, dproj, mv1, w["mix_w_in"])
        token = grads_done(l, "mix", {"mix_w_in": _matmul_tokens(ht, dproj, tn=P_W // 5, tk=DW_TK, name="dw_mix_in"),
                                      "mix_w_out": gw_mix_out})
        dx, h, dat, actt, dyb, a0 = _ffn_bwd(x0, dx, behind(mv[0], token), w["ffn1_w_in"], w["ffn1_w_out"])
        token = grads_done(l, "ffn1_w_out", {"ffn1_w_out": _matmul_tokens(actt, dyb, tn=D, tk=DW_TK, name="dw_ffn_out")})
        token = grads_done(l, "ffn1_w_in", {"ffn1_w_in": _matmul_tokens(dat, h, tn=D, tk=DW_TK, name="dw_ffn_in",
                                                                        after=token)})
        g["mod"][l] = jnp.concatenate([a0[0:3], a1b[0:2], a1[2:3], a2[0:3]], axis=0)
        g["ln_g"][l] = jnp.stack([a0[3], a1[3], a2[3]])
        g["ln_b"][l] = jnp.stack([a0[4], a1[4], a2[4]])
        g["ssd_conv_w"][l] = acc1[0:SSD_K]
        g["ssd_conv_b"][l] = acc1[4]
        g["ssd_norm_g"][l] = acc1[5, :SSD_W]
        g["ssd_dt_bias"][l] = acc2[0, :SSD_H]
        g["ssd_a_log"][l] = acc2[1, :SSD_H]
        g["ssd_d"][l] = acc2[2, :SSD_H]
        g["fox_f_bias"][l] = accf[3, SM_F:SM_F + FOX_H]
        g["sconv_w"][l] = accs[0:SC_K]
    grad_x, a_in = _ln_in_bwd(x, dx, behind(gb_in, token))
    g = {k: jnp.stack(v) for k, v in g.items()}
    g["ln_in_g"], g["ln_in_b"] = a_in[0], a_in[1]
    return loss_acc[0, 0], grad_x, g


MESH = pl.DeviceIdType.MESH
ANY = pl.BlockSpec(memory_space=pl.ANY)


def _all_gather(shards, *, in_vmem, name):
    n_arr = len(shards)

    def body(*refs):
        x_refs, out_refs = refs[:n_arr], refs[n_arr:2 * n_arr]
        send_sems, recv_sems, local_sems = refs[2 * n_arr:]
        x, y, c = lax.axis_index("x"), lax.axis_index("y"), lax.axis_index("c")
        me, sibling = (x, y, c), (x, y, 1 - c)
        chips = [(1 - x, y), (x, 1 - y), (1 - x, 1 - y)]

        def copy(a, k, block, to, src=None):
            px, py, pc = block
            slot = out_refs[a].at[4 * px + 2 * py + pc]
            return pltpu.make_async_remote_copy(
                src_ref=slot if src is None else src, dst_ref=slot,
                send_sem=send_sems.at[7 * a + k], recv_sem=recv_sems.at[7 * a + k], device_id=to, device_id_type=MESH)

        mine, first, passed = [], [], []
        for a in range(n_arr):
            mine.append(pltpu.make_async_copy(x_refs[a], out_refs[a].at[4 * x + 2 * y + c], local_sems.at[a]))
            mine[-1].start()
            first.append(copy(a, 0, me, sibling, src=x_refs[a]))
            first += [copy(a, 1 + j, me, (*chip, c), src=x_refs[a]) for j, chip in enumerate(chips)]
        for cp in first:
            cp.start()
        for j, chip in enumerate(chips):
            for a in range(n_arr):
                copy(a, 1 + j, (*chip, c), me).wait_recv()
                passed.append(copy(a, 4 + j, (*chip, c), sibling))
                passed[-1].start()
        for a in range(n_arr):
            copy(a, 0, sibling, me).wait_recv()
            for j, chip in enumerate(chips):
                copy(a, 4 + j, (*chip, 1 - c), me).wait_recv()
        for cp in first + passed:
            cp.wait_send()
        for cp in mine:
            cp.wait()

    spec = pl.BlockSpec(memory_space=pltpu.VMEM) if in_vmem else ANY
    return pl.pallas_call(
        body, name=name, out_shape=[jax.ShapeDtypeStruct((N_DEV,) + s.shape, s.dtype) for s in shards],
        in_specs=[spec] * n_arr, out_specs=[spec] * n_arr,
        scratch_shapes=[pltpu.SemaphoreType.DMA((7 * n_arr,)), pltpu.SemaphoreType.DMA((7 * n_arr,)),
                        pltpu.SemaphoreType.DMA((n_arr,))],
    )(*shards)


HBM = pl.BlockSpec(memory_space=pltpu.HBM)
SEM = pl.BlockSpec(memory_space=pltpu.SEMAPHORE)
EFFECT = pltpu.SideEffectType.DATAFLOW_SIDE_EFFECTING


def _spread_copies(srcs, lands, send_sems, recv_sems, local_sems, scatter):
    x, y, c = lax.axis_index("x"), lax.axis_index("y"), lax.axis_index("c")
    me = 4 * x + 2 * y + c
    local, remote = [], []
    for a in range(len(srcs)):
        own = srcs[a].at[me] if scatter else srcs[a]
        local.append(pltpu.make_async_copy(own, lands[a].at[me], local_sems.at[a]))
        for r in range(1, N_DEV):
            px, py, pc = (1 - x if r & 4 else x), (1 - y if r & 2 else y), (1 - c if r & 1 else c)
            peer = 4 * px + 2 * py + pc
            k = (N_DEV - 1) * a + r - 1
            mk = functools.partial(pltpu.make_async_remote_copy, send_sem=send_sems.at[k], recv_sem=recv_sems.at[k],
                                   device_id=(px, py, pc), device_id_type=MESH)
            remote.append((mk(src_ref=srcs[a].at[peer] if scatter else srcs[a], dst_ref=lands[a].at[me]),
                           mk(src_ref=own, dst_ref=lands[a].at[peer])))
    return local, remote


def _spread_start(srcs, *, scatter, name, after=()):
    n, k = len(srcs), len(after)
    lands = [jax.ShapeDtypeStruct((N_DEV,) + s.shape[-2:], s.dtype) for s in srcs]

    def body(*refs):
        src, land = refs[:n], refs[n:2 * n]
        send_sems, recv_sems, local_sems = refs[2 * n + k:2 * n + k + 3]
        token = refs[-1]
        local, remote = _spread_copies(src, land, send_sems, recv_sems, local_sems, scatter)
        for cp in local:
            cp.start()
        for cp, _ in remote:
            cp.start()
        token[...] = jnp.zeros_like(token)

    nsem = (N_DEV - 1) * n
    out = pl.pallas_call(
        body, name=name,
        out_shape=(pltpu.SemaphoreType.DMA((nsem,)), pltpu.SemaphoreType.DMA((nsem,)), pltpu.SemaphoreType.DMA((n,)),
                   *[pltpu.HBM(s.shape, s.dtype) for s in srcs], *[pltpu.HBM(s.shape, s.dtype) for s in lands],
                   jax.ShapeDtypeStruct((8, 128), f32)),
        in_specs=[HBM] * (2 * n) + [ANY] * k,
        out_specs=(SEM, SEM, SEM, *[HBM] * (2 * n), pl.BlockSpec(memory_space=pltpu.VMEM)),
        input_output_aliases={i: 3 + i for i in range(2 * n)},
        compiler_params=pltpu.CompilerParams(has_side_effects=EFFECT),
    )(*[pltpu.with_memory_space_constraint(s, pltpu.HBM) for s in srcs],
      *[pltpu.with_memory_space_constraint(lax.empty(s.shape, s.dtype), pltpu.HBM) for s in lands], *after)
    return out[:-1], out[-1]


def _spread_wait(state, after, *, scatter, name):
    n = (len(state) - 3) // 2
    sems, thru = state[:3], state[3:]

    def body(*refs):
        src, land = refs[:n], refs[n:2 * n]
        send_sems, recv_sems, local_sems = refs[2 * n:2 * n + 3]
        local, remote = _spread_copies(src, land, send_sems, recv_sems, local_sems, scatter)
        for sent, received in remote:
            sent.wait_send()
            received.wait_recv()
        for cp in local:
            cp.wait()

    out = pl.pallas_call(
        body, name=name,
        out_shape=tuple(pltpu.HBM(t.shape, t.dtype) for t in thru),
        in_specs=[HBM] * (2 * n) + [SEM] * 3 + [ANY], out_specs=tuple([HBM] * (2 * n)),
        input_output_aliases={i: i for i in range(2 * n)},
        compiler_params=pltpu.CompilerParams(has_side_effects=EFFECT),
    )(*thru, *sems, after)
    return list(out[n:])


def _sum_slots(buf, *, tr, name):
    nb, r, n = buf.shape

    def body(b_ref, o_ref):
        acc = b_ref[0].astype(f32)
        for k in range(1, nb):
            acc = acc + b_ref[k].astype(f32)
        o_ref[...] = acc

    return pl.pallas_call(
        body, name=name, grid=(r // tr,),
        in_specs=[pl.BlockSpec((nb, tr, n), lambda i: (0, i, 0))],
        out_specs=pl.BlockSpec((tr, n), lambda i: (i, 0)),
        out_shape=jax.ShapeDtypeStruct((r, n), f32), compiler_params=_cp(("parallel",)))(buf)


def _ada_fwd(c_all, ada_w, ada_b_cols):
    n = ada_w.shape[-1]

    def body(c_ref, w_ref, b_ref, o_ref):
        cv = c_ref[...]
        ca = (cv * _sigmoid(cv)).astype(bf16)
        o_ref[...] = _dot(ca, w_ref[...].astype(bf16)) + b_ref[...]

    return pl.pallas_call(
        body, name="ada_fwd", grid=(DEPTH,),
        in_specs=[_const_spec((N_DEV, D)), pl.BlockSpec((None, D, n), lambda l: (l, 0, 0)),
                  pl.BlockSpec((None, 1, n), lambda l: (l, 0, 0))],
        out_specs=pl.BlockSpec((None, N_DEV, n), lambda l: (l, 0, 0)),
        out_shape=jax.ShapeDtypeStruct((DEPTH, N_DEV, n), f32), compiler_params=_cp(("parallel",)))(c_all, ada_w, ada_b_cols)


def _ada_bwd(c_all, dmod_cols):
    n = dmod_cols.shape[-1]

    def body(c_ref, d_ref, o_ref):
        cv = c_ref[...]
        ca = (cv * _sigmoid(cv)).astype(bf16)
        o_ref[...] = _dot_tn(ca, d_ref[...].astype(bf16))

    return pl.pallas_call(
        body, name="ada_bwd", grid=(DEPTH,),
        in_specs=[_const_spec((N_DEV, D)), pl.BlockSpec((None, N_DEV, n), lambda l: (l, 0, 0))],
        out_specs=pl.BlockSpec((None, D, n), lambda l: (l, 0, 0)),
        out_shape=jax.ShapeDtypeStruct((DEPTH, D, n), f32), compiler_params=_cp(("parallel",)))(c_all, dmod_cols)


def _adamw(w, g, m, v, *, tr, name):
    r, n = w.shape

    def body(w_ref, g_ref, m_ref, v_ref, d_ref, mo_ref, vo_ref):
        g_ = g_ref[...]
        m_ = ADAM_B1 * m_ref[...] + (1.0 - ADAM_B1) * g_
        v_ = ADAM_B2 * v_ref[...] + (1.0 - ADAM_B2) * jnp.square(g_)
        m_hat = m_ / (1.0 - ADAM_B1 ** ADAM_STEP)
        v_hat = v_ / (1.0 - ADAM_B2 ** ADAM_STEP)
        d_ref[...] = -ADAM_LR * (m_hat / (jnp.sqrt(v_hat) + ADAM_EPS) + ADAM_WD * w_ref[...])
        mo_ref[...] = m_
        vo_ref[...] = v_

    blk = pl.BlockSpec((tr, n), lambda i: (i, 0))
    return pl.pallas_call(
        body, name=name, grid=(r // tr,), in_specs=[blk] * 4, out_specs=[blk] * 3,
        out_shape=[jax.ShapeDtypeStruct((r, n), f32)] * 3, compiler_params=_cp(("parallel",)))(w, g, m, v)


def _adamw_landed(w, m, v, land0, land1, *, tr, name):
    _, r, n = w.shape
    nrt = r // tr

    def body(w_ref, m_ref, v_ref, l0_ref, l1_ref, g_ref, d_ref, mo_ref, vo_ref):
        layer = pl.program_id(0)

        def total(ref):
            acc = ref[0].astype(f32)
            for k in range(1, N_DEV):
                acc = acc + ref[k].astype(f32)
            return acc

        @pl.when(layer == 0)
        def _():
            g_ref[...] = total(l0_ref)

        @pl.when(layer == 1)
        def _():
            g_ref[...] = total(l1_ref)
        g_ = g_ref[...]
        m_ = ADAM_B1 * m_ref[...] + (1.0 - ADAM_B1) * g_
        v_ = ADAM_B2 * v_ref[...] + (1.0 - ADAM_B2) * jnp.square(g_)
        m_hat = m_ / (1.0 - ADAM_B1 ** ADAM_STEP)
        v_hat = v_ / (1.0 - ADAM_B2 ** ADAM_STEP)
        d_ref[...] = -ADAM_LR * (m_hat / (jnp.sqrt(v_hat) + ADAM_EPS) + ADAM_WD * w_ref[...])
        mo_ref[...] = m_
        vo_ref[...] = v_

    blk = pl.BlockSpec((None, tr, n), lambda l, i: (l, i, 0))
    land0_spec = pl.BlockSpec((N_DEV, tr, n), lambda l, i: (0, jnp.where(l == 0, i, nrt - 1), 0))
    land1_spec = pl.BlockSpec((N_DEV, tr, n), lambda l, i: (0, jnp.where(l == 1, i, 0), 0))
    return pl.pallas_call(
        body, name=name, grid=(DEPTH, nrt), in_specs=[blk] * 3 + [land0_spec, land1_spec], out_specs=[blk] * 4,
        out_shape=[jax.ShapeDtypeStruct(w.shape, f32)] * 4,
        compiler_params=_cp(("arbitrary", "arbitrary")))(w, m, v, land0, land1)


WEIGHTS = ["ln_in_g", "ln_in_b", "ada_w", "ada_b", "ffn1_w_in", "ffn1_w_out", "mix_w_in", "mix_w_out", "ssd_conv_w",
           "ssd_conv_b", "ssd_dt_bias", "ssd_a_log", "ssd_d", "ssd_norm_g", "fox_f_bias", "sconv_w", "ffn2_w_in",
           "ffn2_w_out", "ln_g", "ln_b"]
BIG = ["ffn1_w_in", "ffn1_w_out", "ffn2_w_in", "ffn2_w_out", "mix_w_in", "mix_w_out"]
GROUPS = {"ffn1": ["ffn1_w_in", "ffn1_w_out"], "mix": ["mix_w_in", "mix_w_out"], "ffn2": ["ffn2_w_in", "ffn2_w_out"]}
COL_SHARDED = ("ffn1_w_in", "ffn2_w_in")
SMALL_SHARDED = {"ssd_conv_w": 128, "sconv_w": 32, "ln_g": 128, "ln_b": 128}
ADAM_TR = {"ada_w": 256, "ffn1_w_in": 512, "ffn2_w_in": 512, "ffn1_w_out": 352, "ffn2_w_out": 352, "mix_w_in": 64,
           "mix_w_out": 256}
LAND_TR = {"ffn1_w_in": 176, "ffn2_w_in": 176, "ffn1_w_out": 176, "ffn2_w_out": 176, "mix_w_in": 32, "mix_w_out": 128}


def _pad_rows(v, mult=128):
    v = v.reshape(-1)
    return jnp.pad(v, (0, (-v.shape[0]) % mult))


def _pack_rows(parts, row_mult=8):
    flat = [_pad_rows(p.astype(f32)) for p in parts]
    offs, o = [], 0
    for f in flat:
        offs.append(o)
        o += f.shape[0] // 128
    buf = jnp.concatenate(flat).reshape(-1, 128)
    return jnp.pad(buf, ((0, (-buf.shape[0]) % row_mult), (0, 0))), offs


def _take(buf, off, shape):
    n = 1
    for s in shape:
        n *= s
    rows = -(-n // 128)
    lead = buf.shape[:-2]
    flat = buf[..., off:off + rows, :].reshape(lead + (rows * 128,))
    return flat[..., :n].reshape(lead + tuple(shape))


def kernel(*args):
    names = (["x", "c"] + WEIGHTS + ["loss_target"] + ["m_" + n for n in WEIGHTS] + ["v_" + n for n in WEIGHTS])
    assert len(args) == len(names)
    a = dict(zip(names, args))
    xi, yi, ci = lax.axis_index("x"), lax.axis_index("y"), lax.axis_index("c")
    me = 4 * xi + 2 * yi + ci

    small_in = [a["c"], a["ln_g"], a["ln_b"], a["ssd_conv_w"], a["sconv_w"]]
    buf, offs = _pack_rows(small_in)
    got, = _all_gather([buf], in_vmem=True, name="gather_small")
    c_all = _take(got, offs[0], (D,))
    full = {}
    for k, n in enumerate(["ln_g", "ln_b", "ssd_conv_w", "sconv_w"]):
        sh = a[n].shape
        t = _take(got, offs[k + 1], sh)
        full[n] = jnp.transpose(t, (1, 2, 0, 3)).reshape(sh[0], sh[1], N_DEV * sh[2])

    ncol = a["ada_w"].shape[-1]
    ada_b_cols = lax.dynamic_slice_in_dim(a["ada_b"], me * ncol, ncol, axis=1)[:, None, :]
    mod_cols = _ada_fwd(c_all, a["ada_w"], ada_b_cols)
    got, = _all_gather([mod_cols.reshape(DEPTH * N_DEV, ncol)], in_vmem=True, name="gather_mod")
    got = got.reshape(N_DEV, DEPTH, N_DEV, ncol)
    mod = lax.dynamic_index_in_dim(got, me, axis=2, keepdims=False)
    mod = jnp.transpose(mod, (1, 0, 2)).reshape(DEPTH, 9, D)

    flip = lambda t: jnp.swapaxes(t, 1, 2)
    shards = {n: (flip(a[n]) if n in COL_SHARDED else a[n]) for n in BIG}
    shards["mix_w_in"] = _pack_cols(a["mix_w_in"])
    def as_weights(names, got):
        return {n: (t if n in COL_SHARDED else t.reshape(-1, t.shape[-1])) for n, t in zip(names, got)}

    def group_shards(l, gn):
        return [shards[n][l].astype(bf16) for n in GROUPS[gn]]

    w_first = _all_gather(group_shards(0, "ffn1"), in_vmem=False, name="gather_weights")
    w_state, after = {}, [w_first[0]]
    for l, gn in [(0, "mix"), (0, "ffn2"), (1, "ffn1"), (1, "mix"), (1, "ffn2")]:
        w_state[l, gn], token = _spread_start(group_shards(l, gn), scatter=False, after=after,
                                              name=f"weights_{l}{gn}_start")
        after = [token]

    def weights_of(l, gn, after):
        if (l, gn) == (0, "ffn1"):
            return as_weights(GROUPS[gn], w_first)
        return as_weights(GROUPS[gn], _spread_wait(w_state[l, gn], after, scatter=False, name=f"weights_{l}{gn}_wait"))

    g_state = {}

    last_send = {}

    def grads_done(l, gn, gw, after=()):
        if (l, gn) == (0, "ffn1_w_in") and not after:
            last_send.update(gw)
            return None
        srcs = [t.reshape((N_DEV,) + shards[n].shape[1:]) for n, t in gw.items()]
        g_state[l, gn], token = _spread_start(srcs, scatter=True, after=after, name=f"grads_{l}{gn}_start")
        return token

    p = {n: a[n] for n in ("ln_in_g", "ln_in_b", "ssd_conv_b", "ssd_dt_bias", "ssd_a_log", "ssd_d", "ssd_norm_g",
                           "fox_f_bias")}
    p.update(full)
    p["mod"] = mod + after[0][0, 0]

    loss_local, grad_x, g = _local_step(a["x"][0], a["loss_target"][0], p, weights_of, grads_done)
    loss = lax.psum(loss_local, ("x", "y", "c"))

    small_names = ["mod", "ln_in_g", "ln_in_b", "ssd_conv_b", "ssd_dt_bias", "ssd_a_log", "ssd_d", "ssd_norm_g",
                   "fox_f_bias", "ln_g", "ln_b", "ssd_conv_w", "sconv_w"]
    buf, offs = _pack_rows([g[n] for n in small_names])
    got, = _all_gather([buf], in_vmem=True, name="gather_small_grads")
    tot = _sum_slots(got, tr=buf.shape[0], name="sum_small_grads")
    grads_done(0, "ffn1_w_in", last_send, after=(tot,))
    grads = {}
    for k, n in enumerate(small_names[1:], start=1):
        t = _take(tot, offs[k], g[n].shape)
        if n in SMALL_SHARDED:
            w_ = SMALL_SHARDED[n]
            t = lax.dynamic_slice_in_dim(t, me * w_, w_, axis=2)
        grads[n] = t
    grads["ada_b"] = _take(tot, offs[0], (DEPTH, 9 * D))
    dmod_all = _take(got, offs[0], (DEPTH, 9 * D))
    dmod_cols = jnp.transpose(lax.dynamic_slice_in_dim(dmod_all, me * ncol, ncol, axis=2), (1, 0, 2))
    grads["ada_w"] = _ada_bwd(c_all, dmod_cols)

    delta, new_m, new_v = {}, {}, {}

    def adamw(n):
        sh = a[n].shape
        two = lambda t: t.reshape(-1, sh[-1])
        outs = _adamw(two(a[n]), two(grads[n]), two(a["m_" + n]), two(a["v_" + n]), tr=ADAM_TR[n], name="adamw_" + n)
        delta[n], new_m[n], new_v[n] = (t.reshape(sh) for t in outs)

    small_params = [n for n in WEIGHTS if n not in ADAM_TR]
    packs = [_pack_rows([src[pre + n] for n in small_params])[0]
             for src, pre in ((a, ""), (grads, ""), (a, "m_"), (a, "v_"))]
    _, offs = _pack_rows([a[n] for n in small_params])
    outs = _adamw(*packs, tr=packs[0].shape[0], name="adamw_small")
    for k, n in enumerate(small_params):
        delta[n], new_m[n], new_v[n] = (_take(t, offs[k], a[n].shape) for t in outs)
    adamw("ada_w")
    after = grad_x
    for gn, names in [("ffn2_w_out", ["ffn2_w_out"]), ("ffn2_w_in", ["ffn2_w_in"]), ("mix", GROUPS["mix"]),
                      ("ffn1_w_out", ["ffn1_w_out"]), ("ffn1_w_in", ["ffn1_w_in"])]:
        lands = [_spread_wait(g_state[l, gn], after if l == 0 else grad_x, scatter=True, name=f"grads_{l}{gn}_wait")
                 for l in range(DEPTH)]
        for k, n in enumerate(names):
            if n == "mix_w_in":
                sums = [_sum_slots(t[k], tr=LAND_TR[n], name="sum_devices") for t in lands]
                grads[n] = _unpack_cols(jnp.stack(sums))
                adamw(n)
            elif n in COL_SHARDED:
                grads[n], delta[n], new_m[n], new_v[n] = (flip(t) for t in _adamw_landed(
                    flip(a[n]), flip(a["m_" + n]), flip(a["v_" + n]), lands[0][k], lands[1][k], tr=LAND_TR[n],
                    name="adamw_" + n))
            else:
                grads[n], delta[n], new_m[n], new_v[n] = _adamw_landed(
                    a[n], a["m_" + n], a["v_" + n], lands[0][k], lands[1][k], tr=LAND_TR[n], name="adamw_" + n)
            after = delta[n]

    return (loss, grad_x[None], *[grads[n] for n in WEIGHTS], *[delta[n] for n in WEIGHTS],
            *[new_m[n] for n in WEIGHTS], *[new_v[n] for n in WEIGHTS])
```

```python
import functools

import jax
import jax.numpy as jnp
from jax import lax
from jax.experimental import pallas as pl
from jax.experimental.pallas import tpu as pltpu

f32, bf16 = jnp.float32, jnp.bfloat16

D = 1024
F = 2816
DEPTH = 2
N_DEV = 8
SSD_W, SSD_HD, SSD_H, SSD_G, SSD_N, SSD_K = 512, 64, 8, 2, 128, 4
FOX_W, FOX_HD, FOX_H = 256, 64, 4
SC_W, SC_K = 256, 3
ALPHA = (2 * DEPTH) ** 0.25
LN_EPS = 1e-5
RMS_EPS = 1e-5
P_XBC, P_Z, P_Q, P_K, P_V, P_SB, P_SC, P_SX, P_SM = 0, 1024, 1536, 1792, 2048, 2304, 2560, 2816, 3072
P_W = 3200
SM_DT, SM_F = 0, 8
ADAM_LR, ADAM_B1, ADAM_B2, ADAM_EPS, ADAM_WD, ADAM_STEP = 0.001, 0.9, 0.999, 1e-08, 0.01, 10

VMEM_LIMIT = 56 * 1024 * 1024


def _cp(sem=None):
    return pltpu.CompilerParams(dimension_semantics=sem, vmem_limit_bytes=VMEM_LIMIT)


def _const_spec(shape):
    nd = len(shape)
    return pl.BlockSpec(shape, lambda *_: (0,) * nd, pipeline_mode=pl.Buffered(1))


def _sigmoid(x):
    return 1.0 / (1.0 + jnp.exp(-x))


def _ln_fwd(u, g, b):
    mu = jnp.mean(u, -1, keepdims=True)
    xc = u - mu
    rstd = lax.rsqrt(jnp.mean(xc * xc, -1, keepdims=True) + LN_EPS)
    xhat = xc * rstd
    return xhat * g + b, xhat, rstd


def _ln_bwd(dout, xhat, rstd, g):
    dxh = dout * g
    m1 = jnp.mean(dxh, -1, keepdims=True)
    m2 = jnp.mean(dxh * xhat, -1, keepdims=True)
    du = rstd * (dxh - m1 - xhat * m2)
    return du, jnp.sum(dout * xhat, 0, keepdims=True), jnp.sum(dout, 0, keepdims=True)


def _dot(a, b):
    return jnp.dot(a, b, preferred_element_type=f32)


def _dot_nt(a, b):
    return lax.dot_general(a, b, (((1,), (1,)), ((), ())), preferred_element_type=f32)


def _dot_tn(a, b):
    return lax.dot_general(a, b, (((0,), (0,)), ((), ())), preferred_element_type=f32)


def _dot_hi(a, b):
    return jnp.dot(a, b, preferred_element_type=f32, precision=lax.Precision.HIGHEST)


def _shift_down(cur, prev, s):
    if s == 0:
        return cur
    row = lax.broadcasted_iota(jnp.int32, cur.shape, 0)
    return jnp.where(row < s, pltpu.roll(prev, s, 0), pltpu.roll(cur, s, 0))


def _shift_up(cur, nxt, s):
    if s == 0:
        return cur
    t = cur.shape[0]
    row = lax.broadcasted_iota(jnp.int32, cur.shape, 0)
    return jnp.where(row >= t - s, pltpu.roll(nxt, t - s, 0), pltpu.roll(cur, t - s, 0))


def _ln_in_fwd(x, gb, *, tt=512):
    L = x.shape[0]

    def body(x_ref, gb_ref, o_ref):
        o_ref[...] = _ln_fwd(x_ref[...], gb_ref[0:1, :], gb_ref[1:2, :])[0]

    return pl.pallas_call(
        body, name="ln_in_fwd", grid=(L // tt,),
        in_specs=[pl.BlockSpec((tt, D), lambda i: (i, 0)), _const_spec((8, D))],
        out_specs=pl.BlockSpec((tt, D), lambda i: (i, 0)),
        out_shape=jax.ShapeDtypeStruct((L, D), f32), compiler_params=_cp(("parallel",)))(x, gb)


def _ln_in_bwd(x, dy, gb, *, tt=512):
    L = x.shape[0]

    def body(x_ref, dy_ref, gb_ref, dx_ref, acc_ref):
        @pl.when(pl.program_id(0) == 0)
        def _():
            acc_ref[...] = jnp.zeros_like(acc_ref)
        _, xhat, rstd = _ln_fwd(x_ref[...], gb_ref[0:1, :], gb_ref[1:2, :])
        du, dg, db = _ln_bwd(dy_ref[...], xhat, rstd, gb_ref[0:1, :])
        dx_ref[...] = du
        acc_ref[0:1, :] += dg
        acc_ref[1:2, :] += db

    return pl.pallas_call(
        body, name="ln_in_bwd", grid=(L // tt,),
        in_specs=[pl.BlockSpec((tt, D), lambda i: (i, 0)), pl.BlockSpec((tt, D), lambda i: (i, 0)), _const_spec((8, D))],
        out_specs=[pl.BlockSpec((tt, D), lambda i: (i, 0)), pl.BlockSpec((8, D), lambda i: (0, 0))],
        out_shape=[jax.ShapeDtypeStruct((L, D), f32), jax.ShapeDtypeStruct((8, D), f32)],
        compiler_params=_cp(("arbitrary",)))(x, dy, gb)


def _loss_head(y, tgt, *, tt=512):
    L = y.shape[0]

    def body(y_ref, t_ref, dy_ref, acc_ref):
        @pl.when(pl.program_id(0) == 0)
        def _():
            acc_ref[...] = jnp.zeros_like(acc_ref)
        e = y_ref[...] - t_ref[...]
        dy_ref[...] = e * (1.0 / D)
        acc_ref[...] += 0.5 * jnp.sum(jnp.mean(e * e, -1, keepdims=True))

    return pl.pallas_call(
        body, name="loss_head", grid=(L // tt,),
        in_specs=[pl.BlockSpec((tt, D), lambda i: (i, 0)), pl.BlockSpec((tt, D), lambda i: (i, 0))],
        out_specs=[pl.BlockSpec((tt, D), lambda i: (i, 0)), pl.BlockSpec((8, 128), lambda i: (0, 0))],
        out_shape=[jax.ShapeDtypeStruct((L, D), f32), jax.ShapeDtypeStruct((8, 128), f32)],
        compiler_params=_cp(("arbitrary",)))(y, tgt)


FFN_CH = 4
FS = F // FFN_CH


def _ffn_fwd(x, mv, w_in, w_out, *, tt=256):
    L = x.shape[0]

    def body(x_ref, mv_ref, wi_ref, wo_ref, o_ref):
        x = x_ref[...]
        h = (x * (1.0 + mv_ref[1:2, :]) + mv_ref[0:1, :]).astype(bf16)
        y = jnp.zeros((tt, D), f32)
        for c in range(FFN_CH):
            g = _dot_nt(h, wi_ref[c])
            u = _dot_nt(h, wi_ref[c + FFN_CH])
            act = (g * _sigmoid(g) * u).astype(bf16)
            y = y + _dot(act, wo_ref[c * FS:(c + 1) * FS, :])
        uu = ALPHA * x + (0.5 * mv_ref[2:3, :]) * y
        o_ref[...] = _ln_fwd(uu, mv_ref[3:4, :], mv_ref[4:5, :])[0]

    return pl.pallas_call(
        body, name="ffn_fwd", grid=(L // tt,),
        in_specs=[pl.BlockSpec((tt, D), lambda i: (i, 0)), _const_spec((8, D)),
                  _const_spec((2 * FFN_CH, FS, D)), _const_spec((F, D))],
        out_specs=pl.BlockSpec((tt, D), lambda i: (i, 0)),
        out_shape=jax.ShapeDtypeStruct((L, D), f32), compiler_params=_cp(("parallel",)))(x, mv, w_in, w_out)


def _ffn_bwd(x, dxo, mv, w_in, w_out, *, tt=256):
    L = x.shape[0]

    def body(x_ref, dxo_ref, mv_ref, wi_ref, wo_ref, dx_ref, h_ref, da_ref, act_ref, dy_ref, acc_ref, a_scr):
        @pl.when(pl.program_id(0) == 0)
        def _():
            acc_ref[...] = jnp.zeros_like(acc_ref)
        x = x_ref[...]
        scale1 = 1.0 + mv_ref[1:2, :]
        h = (x * scale1 + mv_ref[0:1, :]).astype(bf16)
        h_ref[...] = h
        y = jnp.zeros((tt, D), f32)
        for c in range(FFN_CH):
            g = _dot_nt(h, wi_ref[c])
            u = _dot_nt(h, wi_ref[c + FFN_CH])
            a_scr[c] = g
            a_scr[c + FFN_CH] = u
            act = (g * _sigmoid(g) * u).astype(bf16)
            act_ref[c] = act.T
            y = y + _dot(act, wo_ref[c * FS:(c + 1) * FS, :])
        hg = 0.5 * mv_ref[2:3, :]
        _, xhat, rstd = _ln_fwd(ALPHA * x + hg * y, mv_ref[3:4, :], mv_ref[4:5, :])
        du, dlg, dlb = _ln_bwd(dxo_ref[...], xhat, rstd, mv_ref[3:4, :])
        acc_ref[3:4, :] += dlg
        acc_ref[4:5, :] += dlb
        acc_ref[2:3, :] += jnp.sum(0.5 * y * du, 0, keepdims=True)
        dyb = (hg * du).astype(bf16)
        dy_ref[...] = dyb
        dh = jnp.zeros((tt, D), f32)
        for c in range(FFN_CH):
            g = a_scr[c]
            u = a_scr[c + FFN_CH]
            dact = _dot_nt(dyb, wo_ref[c * FS:(c + 1) * FS, :])
            s = _sigmoid(g)
            dg = (dact * u * (s * (1.0 + g * (1.0 - s)))).astype(bf16)
            dup = (dact * (g * s)).astype(bf16)
            da_ref[c] = dg.T
            da_ref[c + FFN_CH] = dup.T
            dh = dh + _dot(dg, wi_ref[c])
            dh = dh + _dot(dup, wi_ref[c + FFN_CH])
        dx_ref[...] = ALPHA * du + dh * scale1
        acc_ref[0:1, :] += jnp.sum(dh, 0, keepdims=True)
        acc_ref[1:2, :] += jnp.sum(dh * x, 0, keepdims=True)

    tok = lambda w: pl.BlockSpec((tt, w), lambda i: (i, 0))
    by_chunk = lambda n: pl.BlockSpec((n, FS, tt), lambda i: (0, 0, i))
    return pl.pallas_call(
        body, name="ffn_bwd", grid=(L // tt,),
        in_specs=[tok(D), tok(D), _const_spec((8, D)), _const_spec((2 * FFN_CH, FS, D)), _const_spec((F, D))],
        out_specs=[tok(D), tok(D), by_chunk(2 * FFN_CH), by_chunk(FFN_CH), tok(D), pl.BlockSpec((8, D), lambda i: (0, 0))],
        out_shape=[jax.ShapeDtypeStruct((L, D), f32), jax.ShapeDtypeStruct((L, D), bf16),
                   jax.ShapeDtypeStruct((2 * FFN_CH, FS, L), bf16), jax.ShapeDtypeStruct((FFN_CH, FS, L), bf16),
                   jax.ShapeDtypeStruct((L, D), bf16), jax.ShapeDtypeStruct((8, D), f32)],
        scratch_shapes=[pltpu.VMEM((2 * FFN_CH, tt, FS), f32)],
        compiler_params=_cp(("arbitrary",)))(x, dxo, mv, w_in, w_out)


DW_TK = 4096


def _matmul_tokens(a, b, *, tn, tk, name, after=None):
    ga, gb = a.ndim == 3, b.ndim == 3
    extra = [] if after is None else [after]
    G = a.shape[0] if ga else (b.shape[0] if gb else 1)
    M, K = a.shape[-2:]
    N = b.shape[-1]
    tk = min(tk, K)
    nk = K // tk

    def body(a_ref, b_ref, *rest):
        o_ref, acc = rest[len(extra):]
        k = pl.program_id(2)
        p = _dot(a_ref[...], b_ref[...])

        @pl.when(k == 0)
        def _():
            acc[...] = p

        @pl.when(k > 0)
        def _():
            acc[...] += p

        @pl.when(k == nk - 1)
        def _():
            o_ref[...] = acc[...].astype(bf16)

    a_spec = (pl.BlockSpec((None, M, tk), lambda g, j, k: (g, 0, k)) if ga
              else pl.BlockSpec((M, tk), lambda g, j, k: (0, k)))
    b_spec = (pl.BlockSpec((None, tk, tn), lambda g, j, k: (g, k, j)) if gb
              else pl.BlockSpec((tk, tn), lambda g, j, k: (k, j)))
    if ga or gb:
        o_spec, o_shape = pl.BlockSpec((None, M, tn), lambda g, j, k: (g, 0, j)), (G, M, N)
    else:
        o_spec, o_shape = pl.BlockSpec((M, tn), lambda g, j, k: (0, j)), (M, N)
    return pl.pallas_call(
        body, name=name, grid=(G, N // tn, nk), in_specs=[a_spec, b_spec] + [ANY] * len(extra), out_specs=o_spec,
        out_shape=jax.ShapeDtypeStruct(o_shape, bf16), scratch_shapes=[pltpu.VMEM((M, tn), f32)],
        compiler_params=_cp(("parallel", "parallel", "arbitrary")))(a, b, *extra)


def _inproj_fwd(x, mv, w, *, tt=512):
    L = x.shape[0]

    def body(x_ref, mv_ref, w_ref, o_ref):
        h = (x_ref[...] * (1.0 + mv_ref[1:2, :]) + mv_ref[0:1, :]).astype(bf16)
        o_ref[...] = _dot(h, w_ref[...])

    return pl.pallas_call(
        body, name="inproj_fwd", grid=(L // tt,),
        in_specs=[pl.BlockSpec((tt, D), lambda i: (i, 0)), _const_spec((8, D)), _const_spec((D, P_W))],
        out_specs=pl.BlockSpec((tt, P_W), lambda i: (i, 0)),
        out_shape=jax.ShapeDtypeStruct((L, P_W), f32), compiler_params=_cp(("parallel",)))(x, mv, w)


def _inproj_bwd(x, dx_part, dproj, mv, w, *, tt=512):
    L = x.shape[0]

    def body(x_ref, dxp_ref, dp_ref, mv_ref, w_ref, dx_ref, h_ref, acc_ref):
        @pl.when(pl.program_id(0) == 0)
        def _():
            acc_ref[...] = jnp.zeros_like(acc_ref)
        x = x_ref[...]
        scale1 = 1.0 + mv_ref[1:2, :]
        h_ref[...] = (x * scale1 + mv_ref[0:1, :]).astype(bf16).T
        dh = _dot_nt(dp_ref[...], w_ref[...])
        dx_ref[...] = dxp_ref[...] + dh * scale1
        acc_ref[0:1, :] += jnp.sum(dh, 0, keepdims=True)
        acc_ref[1:2, :] += jnp.sum(dh * x, 0, keepdims=True)

    tok = lambda w_: pl.BlockSpec((tt, w_), lambda i: (i, 0))
    return pl.pallas_call(
        body, name="inproj_bwd", grid=(L // tt,),
        in_specs=[tok(D), tok(D), tok(P_W), _const_spec((8, D)), _const_spec((D, P_W))],
        out_specs=[tok(D), pl.BlockSpec((D, tt), lambda i: (0, i)), pl.BlockSpec((8, D), lambda i: (0, 0))],
        out_shape=[jax.ShapeDtypeStruct((L, D), f32), jax.ShapeDtypeStruct((D, L), bf16),
                   jax.ShapeDtypeStruct((8, D), f32)],
        compiler_params=_cp(("arbitrary",)))(x, dx_part, dproj, mv, w)


def _outproj_fwd(x, ycat, mv, w, *, tt=512):
    L = x.shape[0]

    def body(x_ref, y_ref, mv_ref, w_ref, o_ref):
        y = _dot(y_ref[...], w_ref[...])
        uu = ALPHA * x_ref[...] + mv_ref[2:3, :] * y
        o_ref[...] = _ln_fwd(uu, mv_ref[3:4, :], mv_ref[4:5, :])[0]

    tok = lambda w_: pl.BlockSpec((tt, w_), lambda i: (i, 0))
    return pl.pallas_call(
        body, name="outproj_fwd", grid=(L // tt,),
        in_specs=[tok(D), tok(D), _const_spec((8, D)), _const_spec((D, D))],
        out_specs=tok(D),
        out_shape=jax.ShapeDtypeStruct((L, D), f32), compiler_params=_cp(("parallel",)))(x, ycat, mv, w)


def _outproj_bwd(x, ycat, dxo, mv, w, *, tt=512):
    L = x.shape[0]

    def body(x_ref, y_ref, dxo_ref, mv_ref, w_ref, dx_ref, dy_ref, dyc_ref, yt_ref, acc_ref):
        @pl.when(pl.program_id(0) == 0)
        def _():
            acc_ref[...] = jnp.zeros_like(acc_ref)
        yt_ref[...] = y_ref[...].T
        y = _dot(y_ref[...], w_ref[...])
        gate = mv_ref[2:3, :]
        _, xhat, rstd = _ln_fwd(ALPHA * x_ref[...] + gate * y, mv_ref[3:4, :], mv_ref[4:5, :])
        du, dlg, dlb = _ln_bwd(dxo_ref[...], xhat, rstd, mv_ref[3:4, :])
        acc_ref[3:4, :] += dlg
        acc_ref[4:5, :] += dlb
        acc_ref[2:3, :] += jnp.sum(y * du, 0, keepdims=True)
        dx_ref[...] = ALPHA * du
        dyb = (gate * du).astype(bf16)
        dy_ref[...] = dyb
        dyc_ref[...] = _dot_nt(dyb, w_ref[...])

    tok = lambda w_: pl.BlockSpec((tt, w_), lambda i: (i, 0))
    return pl.pallas_call(
        body, name="outproj_bwd", grid=(L // tt,),
        in_specs=[tok(D), tok(D), tok(D), _const_spec((8, D)), _const_spec((D, D))],
        out_specs=[tok(D), tok(D), tok(D), pl.BlockSpec((D, tt), lambda i: (0, i)), pl.BlockSpec((8, D), lambda i: (0, 0))],
        out_shape=[jax.ShapeDtypeStruct((L, D), f32), jax.ShapeDtypeStruct((L, D), bf16),
                   jax.ShapeDtypeStruct((L, D), f32), jax.ShapeDtypeStruct((D, L), bf16),
                   jax.ShapeDtypeStruct((8, D), f32)],
        compiler_params=_cp(("arbitrary",)))(x, ycat, dxo, mv, w)


def _sconv_fwd(proj, w, *, tt=512):
    L = proj.shape[0]
    cb = SC_W

    def body(b_ref, c_ref, x_ref, cp_ref, xp_ref, w_ref, o_ref):
        first = jnp.where(pl.program_id(0) > 0, 1.0, 0.0)
        u = c_ref[...] * x_ref[...]
        up = cp_ref[...] * xp_ref[...] * first
        v = w_ref[2:3, :] * u + w_ref[1:2, :] * _shift_down(u, up, 1) + w_ref[0:1, :] * _shift_down(u, up, 2)
        o_ref[...] = (b_ref[...] * v).astype(bf16)

    cur = lambda col: pl.BlockSpec((tt, cb), lambda i: (i, col // cb))
    prev = lambda col: pl.BlockSpec((tt, cb), lambda i: (jnp.maximum(i - 1, 0), col // cb))
    return pl.pallas_call(
        body, name="sconv_fwd", grid=(L // tt,),
        in_specs=[cur(P_SB), cur(P_SC), cur(P_SX), prev(P_SC), prev(P_SX), _const_spec((8, cb))],
        out_specs=pl.BlockSpec((tt, cb), lambda i: (i, 0)),
        out_shape=jax.ShapeDtypeStruct((L, cb), bf16), compiler_params=_cp(("parallel",)))(proj, proj, proj, proj, proj, w)


def _sconv_bwd(proj, dycat, w, *, tt=512):
    L = proj.shape[0]
    cb = SC_W
    n = L // tt

    def body(b_ref, c_ref, x_ref, cp_ref, xp_ref, bn_ref, dy_ref, dyn_ref, w_ref, db_ref, dc_ref, dx_ref, acc_ref):
        i = pl.program_id(0)

        @pl.when(i == 0)
        def _():
            acc_ref[...] = jnp.zeros_like(acc_ref)
        first = jnp.where(i > 0, 1.0, 0.0)
        last = jnp.where(i < n - 1, 1.0, 0.0)
        cg, xin, bg = c_ref[...], x_ref[...], b_ref[...]
        u = cg * xin
        up = cp_ref[...] * xp_ref[...] * first
        u1, u2 = _shift_down(u, up, 1), _shift_down(u, up, 2)
        v = w_ref[2:3, :] * u + w_ref[1:2, :] * u1 + w_ref[0:1, :] * u2
        dy = dy_ref[...]
        db_ref[...] = (dy * v).astype(bf16)
        dv = dy * bg
        dvn = dyn_ref[...] * bn_ref[...] * last
        du = w_ref[2:3, :] * dv + w_ref[1:2, :] * _shift_up(dv, dvn, 1) + w_ref[0:1, :] * _shift_up(dv, dvn, 2)
        acc_ref[2:3, :] += jnp.sum(dv * u, 0, keepdims=True)
        acc_ref[1:2, :] += jnp.sum(dv * u1, 0, keepdims=True)
        acc_ref[0:1, :] += jnp.sum(dv * u2, 0, keepdims=True)
        dc_ref[...] = (du * xin).astype(bf16)
        dx_ref[...] = (du * cg).astype(bf16)

    cur = lambda col: pl.BlockSpec((tt, cb), lambda i: (i, col // cb))
    prev = lambda col: pl.BlockSpec((tt, cb), lambda i: (jnp.maximum(i - 1, 0), col // cb))
    nxt = lambda col: pl.BlockSpec((tt, cb), lambda i: (jnp.minimum(i + 1, n - 1), col // cb))
    ycol = SSD_W + FOX_W
    out = pl.BlockSpec((tt, cb), lambda i: (i, 0))
    return pl.pallas_call(
        body, name="sconv_bwd", grid=(n,),
        in_specs=[cur(P_SB), cur(P_SC), cur(P_SX), prev(P_SC), prev(P_SX), nxt(P_SB), cur(ycol), nxt(ycol),
                  _const_spec((8, cb))],
        out_specs=[out, out, out, pl.BlockSpec((8, cb), lambda i: (0, 0))],
        out_shape=[jax.ShapeDtypeStruct((L, cb), bf16)] * 3 + [jax.ShapeDtypeStruct((8, cb), f32)],
        compiler_params=_cp(("arbitrary",)))(proj, proj, proj, proj, proj, proj, dycat, dycat, w)


def _log1pexp(x):
    return jnp.log(1.0 + jnp.exp(-jnp.abs(x)))


NEG = -1e30
FOX_SCALE = FOX_HD ** -0.5


HL = 128
AW = FOX_H * HL


def _np_place(rows, cols, pairs, dtype):
    import numpy as np
    m = np.zeros((rows, cols), np.float32)
    for r, c in pairs:
        m[r, c] = 1.0
    return jnp.asarray(m, dtype)


def _fox_consts():
    data = [(h * FOX_HD + d, h * HL + d) for h in range(FOX_H) for d in range(FOX_HD)]
    return dict(
        pq=_np_place(FOX_W, AW, data, bf16),
        pqt=_np_place(AW, FOX_W, [(c, r) for r, c in data], bf16),
        cum_a=[_np_place(128, AW, [(SM_F + h, h * HL + 64 + r) for h in range(FOX_H)], bf16) for r in range(3)],
        head_a=[_np_place(128, AW, [(h, h * HL + 64 + r) for h in range(FOX_H)], bf16) for r in range(3)],
        head_b=[_np_place(128, AW, [(h, h * HL + 67 + r) for h in range(FOX_H)], bf16) for r in range(3)],
        group=_np_place(FOX_W, 128, [(h * FOX_HD + d, h) for h in range(FOX_H) for d in range(FOX_HD)], f32),
        col_a=_np_place(AW, 128, [(h * HL + 64, SM_F + h) for h in range(FOX_H)], f32))


def _split3(x):
    hi = x.astype(bf16)
    r1 = x - hi.astype(f32)
    mid = r1.astype(bf16)
    return hi, mid, (r1 - mid.astype(f32)).astype(bf16)


def _slot_ones(tt, first):
    lane = lax.broadcasted_iota(jnp.int32, (tt, AW), 1) % HL
    return jnp.where((lane >= first) & (lane < first + 3), 1.0, 0.0)


def _fox_prep(proj, hp, cst, *, tt=256):
    L = proj.shape[0]

    def body(q_ref, k_ref, v_ref, sm_ref, hp_ref, pq_ref, c0_ref, c1_ref, c2_ref, qa_ref, ka_ref, va_ref, tot_ref):
        xx = sm_ref[...] + hp_ref[3:4, :]
        logf = jnp.minimum(xx, 0.0) - _log1pexp(xx)
        r = lax.broadcasted_iota(jnp.int32, (tt, tt), 0)
        c = lax.broadcasted_iota(jnp.int32, (tt, tt), 1)
        cum = _dot_hi(jnp.where(r >= c, 1.0, 0.0), logf)
        tot_ref[...] = jnp.broadcast_to(cum[tt - 1:tt, :], (8, 128))
        parts = _split3(-cum)
        pq = pq_ref[...]
        a_ones, b_ones = _slot_ones(tt, 64), _slot_ones(tt, 67)
        qa_ref[...] = (_dot((q_ref[...] * FOX_SCALE).astype(bf16), pq) + a_ones).astype(bf16)
        ka = _dot(k_ref[...].astype(bf16), pq) + b_ones
        for part, c_ref in zip(parts, (c0_ref, c1_ref, c2_ref)):
            ka = ka + _dot(part, c_ref[...])
        ka_ref[...] = ka.astype(bf16)
        va_ref[...] = (_dot(v_ref[...].astype(bf16), pq) + a_ones).astype(bf16)

    col = lambda c_: pl.BlockSpec((tt, FOX_W), lambda i: (i, c_ // FOX_W))
    out = pl.BlockSpec((tt, AW), lambda i: (i, 0))
    return pl.pallas_call(
        body, name="fox_prep", grid=(L // tt,),
        in_specs=[col(P_Q), col(P_K), col(P_V), pl.BlockSpec((tt, 128), lambda i: (i, P_SM // 128)),
                  _const_spec((8, 128)), _const_spec((FOX_W, AW))] + [_const_spec((128, AW))] * 3,
        out_specs=[out, out, out, pl.BlockSpec((8, 128), lambda i: (i, 0))],
        out_shape=[jax.ShapeDtypeStruct((L, AW), bf16)] * 3 + [jax.ShapeDtypeStruct((8 * (L // tt), 128), f32)],
        compiler_params=_cp(("parallel",)))(proj, proj, proj, proj, hp, cst["pq"], *cst["cum_a"])


def _fox_attn_fwd(qa, ka, va, tot, *, tq=256):
    L = qa.shape[0]

    def body(qa_ref, ka_ref, va_ref, tot_ref, o_ref, oa_ref, lse_ref):
        i = pl.program_id(0)
        row = lax.broadcasted_iota(jnp.int32, (tq, tq), 0)
        col = lax.broadcasted_iota(jnp.int32, (tq, tq), 1)
        diag_bias = jnp.where(row >= col, 0.0, NEG)

        def block(j, carry, t_j, bias):
            r0 = pl.multiple_of(j * tq, tq)
            out = []
            for h in range(FOX_H):
                hl = slice(h * HL, (h + 1) * HL)
                m, acc = carry[h]
                m_in = m - t_j[h]
                s = _dot_nt(qa_ref[:, hl], ka_ref[pl.ds(r0, tq), hl])
                if bias is not None:
                    s = s + bias
                m_new = jnp.maximum(m_in, jnp.max(s, -1, keepdims=True))
                p = jnp.exp(s - m_new)
                al = jnp.exp(m_in - m_new)
                pb = p.astype(bf16)
                p_lo = (p - pb.astype(f32)).astype(bf16)
                vj = va_ref[pl.ds(r0, tq), hl]
                out.append((m_new, al * acc + (_dot(pb, vj) + _dot(p_lo, vj))))
            return tuple(out)

        def step(k, state):
            carry, gap = state
            j = i - 1 - k
            t_j = [tot_ref[j, h] for h in range(FOX_H)]
            return block(j, carry, t_j, None), tuple(g + t for g, t in zip(gap, t_j))

        init = tuple((jnp.full((tq, 1), NEG, f32), jnp.zeros((tq, HL), f32)) for _ in range(FOX_H))
        zero_gap = tuple(jnp.zeros((), f32) for _ in range(FOX_H))
        carry = block(i, init, zero_gap, diag_bias)
        carry, gap = lax.fori_loop(0, i, step, (carry, zero_gap))
        lane = lax.broadcasted_iota(jnp.int32, (tq, 128), 1)
        lse_all = jnp.zeros((tq, 128), f32)
        for h in range(FOX_H):
            hs = slice(h * FOX_HD, (h + 1) * FOX_HD)
            m, acc = carry[h]
            l = acc[:, FOX_HD:FOX_HD + 1]
            o = acc[:, :FOX_HD] * (1.0 / l)
            o_ref[:, hs] = o.astype(bf16)
            oa_ref[:, hs] = o
            lse_all = jnp.where(lane == h, m + gap[h] + jnp.log(l), lse_all)
        lse_ref[...] = lse_all

    full = pl.BlockSpec((L, AW), lambda i: (0, 0), pipeline_mode=pl.Buffered(1))
    return pl.pallas_call(
        body, name="fox_fwd", grid=(L // tq,),
        in_specs=[pl.BlockSpec((tq, AW), lambda i: (i, 0)), full, full, pl.BlockSpec(memory_space=pltpu.SMEM)],
        out_specs=[pl.BlockSpec((tq, FOX_W), lambda i: (i, 0)), pl.BlockSpec((tq, FOX_W), lambda i: (i, 0)),
                   pl.BlockSpec((tq, 128), lambda i: (i, 0))],
        out_shape=[jax.ShapeDtypeStruct((L, FOX_W), bf16), jax.ShapeDtypeStruct((L, FOX_W), f32),
                   jax.ShapeDtypeStruct((L, 128), f32)],
        compiler_params=_cp(("parallel",)))(qa, ka, va, tot)


def _fox_bprep(qa, dycat, o_acc, lse, cst, *, tt=256):
    L = qa.shape[0]

    def body(qa_ref, do_ref, oa_ref, lse_ref, pq_ref, g_ref, a0, a1, a2, b0, b1, b2, qb_ref, doa_ref, qbt_ref, doat_ref):
        dob = do_ref[...].astype(bf16)
        delta = _dot_hi(dob.astype(f32) * oa_ref[...], g_ref[...])
        doa = _dot(dob, pq_ref[...])
        for part, ref in zip(_split3(-delta), (a0, a1, a2)):
            doa = doa + _dot(part, ref[...])
        qb = qa_ref[...].astype(f32)
        for part, ref in zip(_split3(-lse_ref[...]), (b0, b1, b2)):
            qb = qb + _dot(part, ref[...])
        doa, qb = doa.astype(bf16), qb.astype(bf16)
        doa_ref[...] = doa
        qb_ref[...] = qb
        doat_ref[...] = doa.T
        qbt_ref[...] = qb.T

    tok = lambda w_: pl.BlockSpec((tt, w_), lambda i: (i, 0))
    tr = pl.BlockSpec((AW, tt), lambda i: (0, i))
    return pl.pallas_call(
        body, name="fox_bprep", grid=(L // tt,),
        in_specs=[tok(AW), pl.BlockSpec((tt, FOX_W), lambda i: (i, SSD_W // FOX_W)), tok(FOX_W), tok(128),
                  _const_spec((FOX_W, AW)), _const_spec((FOX_W, 128))] + [_const_spec((128, AW))] * 6,
        out_specs=[tok(AW), tok(AW), tr, tr],
        out_shape=[jax.ShapeDtypeStruct((L, AW), bf16)] * 2 + [jax.ShapeDtypeStruct((AW, L), bf16)] * 2,
        compiler_params=_cp(("parallel",)))(qa, dycat, o_acc, lse, cst["pq"], cst["group"], *cst["head_a"], *cst["head_b"])


def _fox_attn_bwd(ka, va, tot, qb, doa, qbt, doat, *, tq=256):
    L = ka.shape[0]
    nq = L // tq

    def body(ka_ref, va_ref, tot_ref, qb_ref, doa_ref, qbt_ref, doat_ref, dq_ref, dkt_ref, dvt_ref):
        j = pl.program_id(0)

        @pl.when(j == 0)
        def _():
            dq_ref[...] = jnp.zeros_like(dq_ref)
        row = lax.broadcasted_iota(jnp.int32, (tq, tq), 0)
        col = lax.broadcasted_iota(jnp.int32, (tq, tq), 1)
        diag_bias = jnp.where(row >= col, 0.0, NEG)

        def block(i, carry, gap, bias):
            r0 = pl.multiple_of(i * tq, tq)
            out = []
            for h in range(FOX_H):
                hl = slice(h * HL, (h + 1) * HL)
                dkt, dvt, dsum = carry[h]
                kj = ka_ref[:, hl]
                s = _dot_nt(qb_ref[pl.ds(r0, tq), hl], kj) + gap[h]
                if bias is not None:
                    s = s + bias
                p = jnp.exp(s)
                ds = p * _dot_nt(doa_ref[pl.ds(r0, tq), hl], va_ref[:, hl])
                dsb = ds.astype(bf16)
                dq_ref[pl.ds(r0, tq), hl] += _dot(dsb, kj)
                out.append((dkt + _dot(qbt_ref[hl, pl.ds(r0, tq)], dsb),
                            dvt + _dot(doat_ref[hl, pl.ds(r0, tq)], p.astype(bf16)),
                            dsum + jnp.sum(ds, 0, keepdims=True)))
            return tuple(out)

        init = tuple((jnp.zeros((HL, tq), f32), jnp.zeros((HL, tq), f32), jnp.zeros((1, tq), f32))
                     for _ in range(FOX_H))
        def step(i, state):
            carry, gap = state
            gap = tuple(g + tot_ref[i - 1, h] for h, g in enumerate(gap))
            return block(i, carry, gap, None), gap

        zero_gap = tuple(jnp.zeros((), f32) for _ in range(FOX_H))
        carry = block(j, init, zero_gap, diag_bias)
        carry, _ = lax.fori_loop(j + 1, nq, step, (carry, zero_gap))
        for h in range(FOX_H):
            hl = slice(h * HL, (h + 1) * HL)
            dkt_ref[hl, :] = carry[h][0]
            dvt_ref[hl, :] = carry[h][1]
            dkt_ref[h * HL + FOX_HD:h * HL + FOX_HD + 1, :] = carry[h][2]

    full = lambda shape: pl.BlockSpec(shape, lambda j: (0, 0), pipeline_mode=pl.Buffered(1))
    blk = pl.BlockSpec((tq, AW), lambda j: (j, 0))
    trb = pl.BlockSpec((AW, tq), lambda j: (0, j))
    return pl.pallas_call(
        body, name="fox_bwd", grid=(nq,),
        in_specs=[blk, blk, pl.BlockSpec(memory_space=pltpu.SMEM), full((L, AW)), full((L, AW)), full((AW, L)),
                  full((AW, L))],
        out_specs=[pl.BlockSpec((L, AW), lambda j: (0, 0)), trb, trb],
        out_shape=[jax.ShapeDtypeStruct((L, AW), f32), jax.ShapeDtypeStruct((AW, L), f32),
                   jax.ShapeDtypeStruct((AW, L), f32)],
        compiler_params=_cp(("arbitrary",)))(ka, va, tot, qb, doa, qbt, doat)


def _fox_post(dq, dkt, dvt, proj, ddt, hp, cst, *, tt=256):
    L = proj.shape[0]
    n = L // tt

    def body(dq_ref, dkt_ref, dvt_ref, sm_ref, ddt_ref, hp_ref, pqt_ref, ca_ref,
             dqo_ref, dko_ref, dvo_ref, dsm_ref, acc_ref, carry):
        @pl.when(pl.program_id(0) == 0)
        def _():
            carry[...] = jnp.zeros_like(carry)
            acc_ref[...] = jnp.zeros_like(acc_ref)
        pqt = pqt_ref[...]
        dk_full = dkt_ref[...].T
        dqo_ref[...] = _dot((dq_ref[...] * FOX_SCALE).astype(bf16), pqt).astype(bf16)
        dko_ref[...] = _dot(dk_full.astype(bf16), pqt).astype(bf16)
        dvo_ref[...] = _dot(dvt_ref[...].T.astype(bf16), pqt).astype(bf16)
        dc = -_dot_hi(dk_full, ca_ref[...])
        r = lax.broadcasted_iota(jnp.int32, (tt, tt), 0)
        c = lax.broadcasted_iota(jnp.int32, (tt, tt), 1)
        dl = _dot_hi(jnp.where(r <= c, 1.0, 0.0), dc) + carry[0:1, :]
        carry[0:1, :] += jnp.sum(dc, 0, keepdims=True)
        xx = sm_ref[...] + hp_ref[3:4, :]
        lane = lax.broadcasted_iota(jnp.int32, (tt, 128), 1)
        dlogit = jnp.where((lane >= SM_F) & (lane < SM_F + FOX_H), dl * _sigmoid(-xx), 0.0)
        acc_ref[3:4, :] += jnp.sum(dlogit, 0, keepdims=True)
        dsm_ref[...] = (dlogit + ddt_ref[...]).astype(bf16)

    rev = lambda w_: pl.BlockSpec((tt, w_), lambda i: (n - 1 - i, 0))
    revt = pl.BlockSpec((AW, tt), lambda i: (0, n - 1 - i))
    return pl.pallas_call(
        body, name="fox_post", grid=(n,),
        in_specs=[rev(AW), revt, revt, pl.BlockSpec((tt, 128), lambda i: (n - 1 - i, P_SM // 128)), rev(128),
                  _const_spec((8, 128)), _const_spec((AW, FOX_W)), _const_spec((AW, 128))],
        out_specs=[rev(FOX_W), rev(FOX_W), rev(FOX_W), rev(128), pl.BlockSpec((8, 128), lambda i: (0, 0))],
        out_shape=[jax.ShapeDtypeStruct((L, FOX_W), bf16)] * 3 + [jax.ShapeDtypeStruct((L, 128), bf16),
                                                                   jax.ShapeDtypeStruct((8, 128), f32)],
        scratch_shapes=[pltpu.VMEM((8, 128), f32)],
        compiler_params=_cp(("arbitrary",)))(dq, dkt, dvt, proj, ddt, hp, cst["pqt"], cst["col_a"])


SSD_GW = SSD_W // SSD_G
SSD_HPG = SSD_H // SSD_G


def _ssd_pre(x, xprev, sm, cp_ref, hp_ref, tc):
    pre = (cp_ref[4:5, :] + cp_ref[3:4, :] * x + cp_ref[2:3, :] * _shift_down(x, xprev, 1)
           + cp_ref[1:2, :] * _shift_down(x, xprev, 2) + cp_ref[0:1, :] * _shift_down(x, xprev, 3))
    sig = _sigmoid(pre)
    raw = sm + hp_ref[0:1, :]
    dt = jnp.maximum(raw, 0.0) + _log1pexp(raw)
    a_neg = -jnp.exp(hp_ref[1:2, :])
    r = lax.broadcasted_iota(jnp.int32, (tc, tc), 0)
    c = lax.broadcasted_iota(jnp.int32, (tc, tc), 1)
    cs = _dot_hi(jnp.where(r >= c, 1.0, 0.0), dt * a_neg)
    return pre, sig, raw, dt, a_neg, cs, cs.T, r >= c


def _ssd_fwd(proj, cp, hp, ng, *, tc=256):
    L = proj.shape[0]
    nc = L // tc

    def body(xc_ref, xp_ref, z_ref, sm_ref, cp_ref, hp_ref, ng_ref, y_ref, ypre_ref, sin_ref, s_scr):
        i = pl.program_id(0)

        @pl.when(i == 0)
        def _():
            s_scr[...] = jnp.zeros_like(s_scr)
        x = xc_ref[...]
        xprev = xp_ref[...] * jnp.where(i > 0, 1.0, 0.0)
        pre, sig, _, dt, _, cs, cst, tril = _ssd_pre(x, xprev, sm_ref[...], cp_ref, hp_ref, tc)
        xbc = pre * sig
        sin_ref[...] = s_scr[...]
        for g in range(SSD_G):
            bg = xbc[:, SSD_W + g * SSD_N:SSD_W + (g + 1) * SSD_N]
            cg = xbc[:, SSD_W + SSD_G * SSD_N + g * SSD_N:SSD_W + SSD_G * SSD_N + (g + 1) * SSD_N].astype(bf16)
            cb = _dot_nt(cg, bg.astype(bf16))
            for e in range(SSD_HPG):
                h = g * SSD_HPG + e
                hs = slice(h * SSD_HD, (h + 1) * SSD_HD)
                xs = xbc[:, hs]
                csc = cs[:, h:h + 1]
                lm = jnp.where(tril, jnp.exp(jnp.minimum(csc - cst[h:h + 1, :], 0.0)), 0.0)
                xdt = (xs * dt[:, h:h + 1]).astype(bf16)
                s_h = s_scr[:, hs]
                y = _dot((cb * lm).astype(bf16), xdt) + jnp.exp(csc) * _dot(cg, s_h.astype(bf16))
                ypre_ref[:, hs] = y + hp_ref[2:3, h:h + 1] * xs
                cl = cs[tc - 1:tc, h:h + 1]
                bd = (bg * jnp.exp(cl - csc)).astype(bf16)
                s_scr[:, hs] = jnp.exp(cl) * s_h + _dot_tn(bd, xdt)
        z = z_ref[...]
        yz = ypre_ref[...] * (z * _sigmoid(z))
        for g in range(SSD_G):
            gs = slice(g * SSD_GW, (g + 1) * SSD_GW)
            yg = yz[:, gs]
            r = lax.rsqrt(jnp.mean(yg * yg, -1, keepdims=True) + RMS_EPS)
            y_ref[:, gs] = (yg * r * ng_ref[0:1, gs]).astype(bf16)

    return pl.pallas_call(
        body, name="ssd_fwd", grid=(nc,),
        in_specs=[pl.BlockSpec((tc, 1024), lambda i: (i, 0)),
                  pl.BlockSpec((tc, 1024), lambda i: (jnp.maximum(i - 1, 0), 0)),
                  pl.BlockSpec((tc, SSD_W), lambda i: (i, P_Z // SSD_W)),
                  pl.BlockSpec((tc, 128), lambda i: (i, P_SM // 128)),
                  _const_spec((8, 1024)), _const_spec((8, 128)), _const_spec((8, SSD_W))],
        out_specs=[pl.BlockSpec((tc, SSD_W), lambda i: (i, 0)), pl.BlockSpec((tc, SSD_W), lambda i: (i, 0)),
                   pl.BlockSpec((SSD_N, SSD_W), lambda i: (i, 0))],
        out_shape=[jax.ShapeDtypeStruct((L, SSD_W), bf16), jax.ShapeDtypeStruct((L, SSD_W), f32),
                   jax.ShapeDtypeStruct((nc * SSD_N, SSD_W), f32)],
        scratch_shapes=[pltpu.VMEM((SSD_N, SSD_W), f32)],
        compiler_params=_cp(("arbitrary",)))(proj, proj, proj, proj, cp, hp, ng)


def _ssd_bwd(proj, dycat, ypre, sin, cp, hp, ng, *, tc=256):
    L = proj.shape[0]
    nc = L // tc

    def body(xc_ref, xp_ref, z_ref, sm_ref, cp_ref, hp_ref, ng_ref, sin_ref, ypre_ref, dy_ref,
             dxbc_ref, dz_ref, ddt_ref, acc1_ref, acc2_ref, ds_scr, dnext_scr, dxbc_scr):
        i = pl.program_id(0)
        c_idx = nc - 1 - i

        @pl.when(i == 0)
        def _():
            ds_scr[...] = jnp.zeros_like(ds_scr)
            dnext_scr[...] = jnp.zeros_like(dnext_scr)
            acc1_ref[...] = jnp.zeros_like(acc1_ref)
            acc2_ref[...] = jnp.zeros_like(acc2_ref)
        x = xc_ref[...]
        xprev = xp_ref[...] * jnp.where(c_idx > 0, 1.0, 0.0)
        pre, sig, raw, dt, a_neg, cs, cst, tril = _ssd_pre(x, xprev, sm_ref[...], cp_ref, hp_ref, tc)
        xbc = pre * sig
        z = z_ref[...]
        sz = _sigmoid(z)
        silz = z * sz
        yall = ypre_ref[...]
        yz = yall * silz
        dy = dy_ref[...]
        dyz_parts = []
        for g in range(SSD_G):
            gs = slice(g * SSD_GW, (g + 1) * SSD_GW)
            yg, dyg = yz[:, gs], dy[:, gs]
            r = lax.rsqrt(jnp.mean(yg * yg, -1, keepdims=True) + RMS_EPS)
            acc1_ref[5:6, gs] += jnp.sum(dyg * yg * r, 0, keepdims=True)
            dyn = dyg * ng_ref[0:1, gs]
            dyz_parts.append(r * (dyn - yg * (r * r) * jnp.mean(dyn * yg, -1, keepdims=True)))
        dyz = jnp.concatenate(dyz_parts, axis=1)
        dz_ref[...] = (dyz * yall * (sz * (1.0 + z * (1.0 - sz)))).astype(bf16)
        dyall = dyz * silz

        lane1 = lax.broadcasted_iota(jnp.int32, (1, 128), 1)
        sub = lax.broadcasted_iota(jnp.int32, (128, tc), 0)
        rowc = lax.broadcasted_iota(jnp.int32, (tc, 1), 0)
        dcs = jnp.zeros((tc, 128), f32)
        dcsr = jnp.zeros((128, tc), f32)
        ddt = jnp.zeros((tc, 128), f32)
        dd_row = jnp.zeros((1, 128), f32)
        for g in range(SSD_G):
            b0 = SSD_W + g * SSD_N
            c0 = SSD_W + SSD_G * SSD_N + g * SSD_N
            bg = xbc[:, b0:b0 + SSD_N]
            bgb = bg.astype(bf16)
            cgb = xbc[:, c0:c0 + SSD_N].astype(bf16)
            cb = _dot_nt(cgb, bgb)
            dbg = jnp.zeros((tc, SSD_N), f32)
            dcg = jnp.zeros((tc, SSD_N), f32)
            for e in range(SSD_HPG):
                h = g * SSD_HPG + e
                hs = slice(h * SSD_HD, (h + 1) * SSD_HD)
                oh = jnp.where(lane1 == h, 1.0, 0.0)
                xs = xbc[:, hs]
                dth = dt[:, h:h + 1]
                csc = cs[:, h:h + 1]
                lm = jnp.where(tril, jnp.exp(jnp.minimum(csc - cst[h:h + 1, :], 0.0)), 0.0)
                m = cb * lm
                xdt = (xs * dth).astype(bf16)
                s_h = sin_ref[:, hs]
                s_hb = s_h.astype(bf16)
                dyh = dyall[:, hs]
                dyb = dyh.astype(bf16)
                dd_row = dd_row + oh * jnp.sum(dyh * xs)
                dxs = hp_ref[2:3, h:h + 1] * dyh
                ecs = jnp.exp(csc)
                cs_prod = _dot(cgb, s_hb)
                dcsb = (ecs * dyh).astype(bf16)
                dcg = dcg + _dot_nt(dcsb, s_hb)
                ds_in = _dot_tn(cgb, dcsb)
                dcs_h = jnp.sum(dyh * ecs * cs_prod, -1, keepdims=True)
                dm = _dot_nt(dyb, xdt)
                w = dm * m
                dcs_h = dcs_h + jnp.sum(w, -1, keepdims=True)
                dcsr = jnp.where(sub == h, jnp.sum(w, 0, keepdims=True), dcsr)
                dcbb = (dm * lm).astype(bf16)
                dcg = dcg + _dot(dcbb, bgb)
                dbg = dbg + _dot_tn(dcbb, cgb)
                dxdt = _dot_tn(m.astype(bf16), dyb)
                dsn = ds_scr[:, hs]
                dsnb = dsn.astype(bf16)
                cl = cs[tc - 1:tc, h:h + 1]
                dec = jnp.exp(cl - csc)
                dxdt = dxdt + _dot((bg * dec).astype(bf16), dsnb)
                dbd = _dot_nt(xdt, dsnb)
                dbg = dbg + dbd * dec
                gdec = jnp.sum(dbd * bg, -1, keepdims=True) * dec
                ecl = jnp.exp(cl)
                dcl = jnp.sum(gdec) + jnp.sum(dsn * s_h) * ecl
                ds_scr[:, hs] = ecl * dsn + ds_in
                dcs_h = dcs_h - gdec + jnp.where(rowc == tc - 1, dcl, 0.0)
                dcs = dcs + dcs_h * oh
                dxbc_scr[:, hs] = dxs + dxdt * dth
                ddt = ddt + jnp.sum(dxdt * xs, -1, keepdims=True) * oh
            dxbc_scr[:, b0:b0 + SSD_N] = dbg
            dxbc_scr[:, c0:c0 + SSD_N] = dcg
        dcs = dcs - dcsr.T
        r_i = lax.broadcasted_iota(jnp.int32, (tc, tc), 0)
        c_i = lax.broadcasted_iota(jnp.int32, (tc, tc), 1)
        da = _dot_hi(jnp.where(r_i <= c_i, 1.0, 0.0), dcs)
        ddt = ddt + da * a_neg
        acc2_ref[1:2, :] += jnp.sum(da * dt, 0, keepdims=True) * a_neg
        lane = lax.broadcasted_iota(jnp.int32, (tc, 128), 1)
        ddraw = jnp.where(lane < SSD_H, ddt * _sigmoid(raw), 0.0)
        acc2_ref[0:1, :] += jnp.sum(ddraw, 0, keepdims=True)
        acc2_ref[2:3, :] += dd_row
        ddt_ref[...] = ddraw
        dpre = dxbc_scr[...] * (sig * (1.0 + pre * (1.0 - sig)))
        acc1_ref[4:5, :] += jnp.sum(dpre, 0, keepdims=True)
        for k in range(SSD_K):
            acc1_ref[k:k + 1, :] += jnp.sum(dpre * _shift_down(x, xprev, SSD_K - 1 - k), 0, keepdims=True)
        dnext = dnext_scr[...]
        dxbc_ref[...] = (cp_ref[3:4, :] * dpre + cp_ref[2:3, :] * _shift_up(dpre, dnext, 1)
                         + cp_ref[1:2, :] * _shift_up(dpre, dnext, 2)
                         + cp_ref[0:1, :] * _shift_up(dpre, dnext, 3)).astype(bf16)
        dnext_scr[...] = dpre

    rev = lambda w_, col: pl.BlockSpec((tc, w_), lambda i: (nc - 1 - i, col // w_))
    return pl.pallas_call(
        body, name="ssd_bwd", grid=(nc,),
        in_specs=[rev(1024, 0), pl.BlockSpec((tc, 1024), lambda i: (jnp.maximum(nc - 2 - i, 0), 0)),
                  rev(SSD_W, P_Z), rev(128, P_SM),
                  _const_spec((8, 1024)), _const_spec((8, 128)), _const_spec((8, SSD_W)),
                  pl.BlockSpec((SSD_N, SSD_W), lambda i: (nc - 1 - i, 0)), rev(SSD_W, 0), rev(SSD_W, 0)],
        out_specs=[rev(1024, 0), rev(SSD_W, 0), rev(128, 0),
                   pl.BlockSpec((8, 1024), lambda i: (0, 0)), pl.BlockSpec((8, 128), lambda i: (0, 0))],
        out_shape=[jax.ShapeDtypeStruct((L, 1024), bf16), jax.ShapeDtypeStruct((L, SSD_W), bf16),
                   jax.ShapeDtypeStruct((L, 128), f32), jax.ShapeDtypeStruct((8, 1024), f32),
                   jax.ShapeDtypeStruct((8, 128), f32)],
        scratch_shapes=[pltpu.VMEM((SSD_N, SSD_W), f32), pltpu.VMEM((tc, 1024), f32), pltpu.VMEM((tc, 1024), f32)],
        compiler_params=_cp(("arbitrary",)))(proj, proj, proj, proj, cp, hp, ng, sin, ypre, dycat)


def _pack_cols(w):
    pad = jnp.zeros(w.shape[:-1] + (P_W - P_SM - SSD_H - FOX_H,), w.dtype)
    return jnp.concatenate([w[..., 512:1536], w[..., 0:512], w[..., 1544:2312], w[..., 2316:3084],
                            w[..., 1536:1544], w[..., 2312:2316], pad], axis=-1)


def _unpack_cols(g):
    return jnp.concatenate([g[..., 1024:1536], g[..., 0:1024], g[..., 3072:3080], g[..., 1536:2304],
                            g[..., 3080:3084], g[..., 2304:3072]], axis=-1)


def _rows8(*rows):
    width = max(r.shape[-1] for r in rows)
    out = [jnp.pad(r.astype(f32), (0, width - r.shape[-1])) for r in rows]
    out += [jnp.zeros((width,), f32)] * (8 - len(out))
    return jnp.stack(out)


def _local_step(x, tgt, p, weights_of, grads_done):
    gb_in = _rows8(p["ln_in_g"], p["ln_in_b"])
    cst = _fox_consts()
    rows_to_8 = lambda t: jnp.pad(t, [(0, 0)] * (t.ndim - 2) + [(0, 8 - t.shape[-2]), (0, 0)])
    lanes = lambda t, lo: jnp.pad(t, ((0, 0), (lo, 128 - lo - t.shape[-1])))[:, None, :]
    mvs = rows_to_8(jnp.concatenate([p["mod"].reshape(DEPTH, 3, 3, D), p["ln_g"][:, :, None], p["ln_b"][:, :, None]], 2))
    cps = rows_to_8(jnp.concatenate([p["ssd_conv_w"], p["ssd_conv_b"][:, None]], 1))
    hps = rows_to_8(jnp.concatenate([lanes(p["ssd_dt_bias"], 0), lanes(p["ssd_a_log"], 0), lanes(p["ssd_d"], 0),
                                     lanes(p["fox_f_bias"], SM_F)], 1))
    ngs = rows_to_8(p["ssd_norm_g"][:, None])
    scws = rows_to_8(p["sconv_w"])
    x0 = _ln_in_fwd(x, gb_in)
    saved = []
    for l in range(DEPTH):
        mv = [mvs[l, j] for j in range(3)]
        cp, hp, ng, scw = cps[l], hps[l], ngs[l], scws[l]
        w = dict(weights_of(l, "ffn1", x0))
        x1 = _ffn_fwd(x0, mv[0], w["ffn1_w_in"], w["ffn1_w_out"])
        w.update(weights_of(l, "mix", x1))
        proj = _inproj_fwd(x1, mv[1], w["mix_w_in"])
        y_ssd, ypre, sin = _ssd_fwd(proj, cp, hp, ng)
        qa, ka, va, tot = _fox_prep(proj, hp, cst)
        tot = tot[::8, SM_F:SM_F + FOX_H]
        o, o_acc, lse = _fox_attn_fwd(qa, ka, va, tot)
        y_sc = _sconv_fwd(proj, scw)
        ycat = jnp.concatenate([y_ssd, o, y_sc], axis=1)
        x2 = _outproj_fwd(x1, ycat, mv[1], w["mix_w_out"])
        w.update(weights_of(l, "ffn2", x2))
        x3 = _ffn_fwd(x2, mv[2], w["ffn2_w_in"], w["ffn2_w_out"])
        saved.append((x0, x1, x2, mv, cp, hp, ng, scw, proj, ypre, sin, qa, ka, va, tot, o_acc, lse, ycat, w))
        x0 = x3
    dx, loss_acc = _loss_head(x0, tgt)

    g = {k: [None] * DEPTH for k in (
        "mod", "ln_g", "ln_b", "ssd_conv_w", "ssd_conv_b", "ssd_dt_bias", "ssd_a_log", "ssd_d", "ssd_norm_g",
        "fox_f_bias", "sconv_w")}
    def behind(small, tok):
        return small if tok is None else small + tok[0, 0]

    token = None
    for l in reversed(range(DEPTH)):
        x0, x1, x2, mv, cp, hp, ng, scw, proj, ypre, sin, qa, ka, va, tot, o_acc, lse, ycat, w = saved[l]
        dx, h, dat, actt, dyb, a2 = _ffn_bwd(x2, dx, behind(mv[2], token), w["ffn2_w_in"], w["ffn2_w_out"])
        token = grads_done(l, "ffn2_w_out", {"ffn2_w_out": _matmul_tokens(actt, dyb, tn=D, tk=DW_TK, name="dw_ffn_out")})
        token = grads_done(l, "ffn2_w_in", {"ffn2_w_in": _matmul_tokens(dat, h, tn=D, tk=DW_TK, name="dw_ffn_in",
                                                                        after=token)})
        mv1 = behind(mv[1], token)
        dxp, dyb, dycat, ycat_t, a1 = _outproj_bwd(x1, ycat, dx, mv1, w["mix_w_out"])
        gw_mix_out = _matmul_tokens(ycat_t, dyb, tn=D // 2, tk=DW_TK, name="dw_mix_out")
        dxbc, dz, ddt, acc1, acc2 = _ssd_bwd(proj, dycat, ypre, sin, cp, hp, ng)
        qb, doa, qbt, doat = _fox_bprep(qa, dycat, o_acc, lse, cst)
        dq, dkt, dvt = _fox_attn_bwd(ka, va, tot, qb, doa, qbt, doat)
        dq, dk, dv, dsm, accf = _fox_post(dq, dkt, dvt, proj, ddt, hp, cst)
        dsb, dsc, dsx, accs = _sconv_bwd(proj, dycat, scw)
        dproj = jnp.concatenate([dxbc, dz, dq, dk, dv, dsb, dsc, dsx, dsm], axis=1)
        dx, ht, a1b = _inproj_bwd(x1, dxp, dproj, mv1, w["mix_w_in"])
        token = grads_done(l, "mix", {"mix_w_in": _matmul_tokens(ht, dproj, tn=P_W // 5, tk=DW_TK, name="dw_mix_in"),
                                      "mix_w_out": gw_mix_out})
        dx, h, dat, actt, dyb, a0 = _ffn_bwd(x0, dx, behind(mv[0], token), w["ffn1_w_in"], w["ffn1_w_out"])
        token = grads_done(l, "ffn1_w_out", {"ffn1_w_out": _matmul_tokens(actt, dyb, tn=D, tk=DW_TK, name="dw_ffn_out")})
        token = grads_done(l, "ffn1_w_in", {"ffn1_w_in": _matmul_tokens(dat, h, tn=D, tk=DW_TK, name="dw_ffn_in",
                                                                        after=token)})
        g["mod"][l] = jnp.concatenate([a0[0:3], a1b[0:2], a1[2:3], a2[0:3]], axis=0)
        g["ln_g"][l] = jnp.stack([a0[3], a1[3], a2[3]])
        g["ln_b"][l] = jnp.stack([a0[4], a1[4], a2[4]])
        g["ssd_conv_w"][l] = acc1[0:SSD_K]
        g["ssd_conv_b"][l] = acc1[4]
        g["ssd_norm_g"][l] = acc1[5, :SSD_W]
        g["ssd_dt_bias"][l] = acc2[0, :SSD_H]
        g["ssd_a_log"][l] = acc2[1, :SSD_H]
        g["ssd_d"][l] = acc2[2, :SSD_H]
        g["fox_f_bias"][l] = accf[3, SM_F:SM_F + FOX_H]
        g["sconv_w"][l] = accs[0:SC_K]
    grad_x, a_in = _ln_in_bwd(x, dx, behind(gb_in, token))
    g = {k: jnp.stack(v) for k, v in g.items()}
    g["ln_in_g"], g["ln_in_b"] = a_in[0], a_in[1]
    return loss_acc[0, 0], grad_x, g


MESH = pl.DeviceIdType.MESH
ANY = pl.BlockSpec(memory_space=pl.ANY)


def _all_gather(shards, *, in_vmem, name):
    n_arr = len(shards)

    def body(*refs):
        x_refs, out_refs = refs[:n_arr], refs[n_arr:2 * n_arr]
        send_sems, recv_sems, local_sems = refs[2 * n_arr:]
        x, y, c = lax.axis_index("x"), lax.axis_index("y"), lax.axis_index("c")
        me, sibling = (x, y, c), (x, y, 1 - c)
        chips = [(1 - x, y), (x, 1 - y), (1 - x, 1 - y)]

        def copy(a, k, block, to, src=None):
            px, py, pc = block
            slot = out_refs[a].at[4 * px + 2 * py + pc]
            return pltpu.make_async_remote_copy(
                src_ref=slot if src is None else src, dst_ref=slot,
                send_sem=send_sems.at[7 * a + k], recv_sem=recv_sems.at[7 * a + k], device_id=to, device_id_type=MESH)

        mine, first, passed = [], [], []
        for a in range(n_arr):
            mine.append(pltpu.make_async_copy(x_refs[a], out_refs[a].at[4 * x + 2 * y + c], local_sems.at[a]))
            mine[-1].start()
            first.append(copy(a, 0, me, sibling, src=x_refs[a]))
            first += [copy(a, 1 + j, me, (*chip, c), src=x_refs[a]) for j, chip in enumerate(chips)]
        for cp in first:
            cp.start()
        for j, chip in enumerate(chips):
            for a in range(n_arr):
                copy(a, 1 + j, (*chip, c), me).wait_recv()
                passed.append(copy(a, 4 + j, (*chip, c), sibling))
                passed[-1].start()
        for a in range(n_arr):
            copy(a, 0, sibling, me).wait_recv()
            for j, chip in enumerate(chips):
                copy(a, 4 + j, (*chip, 1 - c), me).wait_recv()
        for cp in first + passed:
            cp.wait_send()
        for cp in mine:
            cp.wait()

    spec = pl.BlockSpec(memory_space=pltpu.VMEM) if in_vmem else ANY
    return pl.pallas_call(
        body, name=name, out_shape=[jax.ShapeDtypeStruct((N_DEV,) + s.shape, s.dtype) for s in shards],
        in_specs=[spec] * n_arr, out_specs=[spec] * n_arr,
        scratch_shapes=[pltpu.SemaphoreType.DMA((7 * n_arr,)), pltpu.SemaphoreType.DMA((7 * n_arr,)),
                        pltpu.SemaphoreType.DMA((n_arr,))],
    )(*shards)


HBM = pl.BlockSpec(memory_space=pltpu.HBM)
SEM = pl.BlockSpec(memory_space=pltpu.SEMAPHORE)
EFFECT = pltpu.SideEffectType.DATAFLOW_SIDE_EFFECTING


def _spread_copies(srcs, lands, send_sems, recv_sems, local_sems, scatter):
    x, y, c = lax.axis_index("x"), lax.axis_index("y"), lax.axis_index("c")
    me = 4 * x + 2 * y + c
    local, remote = [], []
    for a in range(len(srcs)):
        own = srcs[a].at[me] if scatter else srcs[a]
        local.append(pltpu.make_async_copy(own, lands[a].at[me], local_sems.at[a]))
        for r in range(1, N_DEV):
            px, py, pc = (1 - x if r & 4 else x), (1 - y if r & 2 else y), (1 - c if r & 1 else c)
            peer = 4 * px + 2 * py + pc
            k = (N_DEV - 1) * a + r - 1
            mk = functools.partial(pltpu.make_async_remote_copy, send_sem=send_sems.at[k], recv_sem=recv_sems.at[k],
                                   device_id=(px, py, pc), device_id_type=MESH)
            remote.append((mk(src_ref=srcs[a].at[peer] if scatter else srcs[a], dst_ref=lands[a].at[me]),
                           mk(src_ref=own, dst_ref=lands[a].at[peer])))
    return local, remote


def _spread_start(srcs, *, scatter, name, after=()):
    n, k = len(srcs), len(after)
    lands = [jax.ShapeDtypeStruct((N_DEV,) + s.shape[-2:], s.dtype) for s in srcs]

    def body(*refs):
        src, land = refs[:n], refs[n:2 * n]
        send_sems, recv_sems, local_sems = refs[2 * n + k:2 * n + k + 3]
        token = refs[-1]
        local, remote = _spread_copies(src, land, send_sems, recv_sems, local_sems, scatter)
        for cp in local:
            cp.start()
        for cp, _ in remote:
            cp.start()
        token[...] = jnp.zeros_like(token)

    nsem = (N_DEV - 1) * n
    out = pl.pallas_call(
        body, name=name,
        out_shape=(pltpu.SemaphoreType.DMA((nsem,)), pltpu.SemaphoreType.DMA((nsem,)), pltpu.SemaphoreType.DMA((n,)),
                   *[pltpu.HBM(s.shape, s.dtype) for s in srcs], *[pltpu.HBM(s.shape, s.dtype) for s in lands],
                   jax.ShapeDtypeStruct((8, 128), f32)),
        in_specs=[HBM] * (2 * n) + [ANY] * k,
        out_specs=(SEM, SEM, SEM, *[HBM] * (2 * n), pl.BlockSpec(memory_space=pltpu.VMEM)),
        input_output_aliases={i: 3 + i for i in range(2 * n)},
        compiler_params=pltpu.CompilerParams(has_side_effects=EFFECT),
    )(*[pltpu.with_memory_space_constraint(s, pltpu.HBM) for s in srcs],
      *[pltpu.with_memory_space_constraint(lax.empty(s.shape, s.dtype), pltpu.HBM) for s in lands], *after)
    return out[:-1], out[-1]


def _spread_wait(state, after, *, scatter, name):
    n = (len(state) - 3) // 2
    sems, thru = state[:3], state[3:]

    def body(*refs):
        src, land = refs[:n], refs[n:2 * n]
        send_sems, recv_sems, local_sems = refs[2 * n:2 * n + 3]
        local, remote = _spread_copies(src, land, send_sems, recv_sems, local_sems, scatter)
        for sent, received in remote:
            sent.wait_send()
            received.wait_recv()
        for cp in local:
            cp.wait()

    out = pl.pallas_call(
        body, name=name,
        out_shape=tuple(pltpu.HBM(t.shape, t.dtype) for t in thru),
        in_specs=[HBM] * (2 * n) + [SEM] * 3 + [ANY], out_specs=tuple([HBM] * (2 * n)),
        input_output_aliases={i: i for i in range(2 * n)},
        compiler_params=pltpu.CompilerParams(has_side_effects=EFFECT),
    )(*thru, *sems, after)
    return list(out[n:])


def _sum_slots(buf, *, tr, name):
    nb, r, n = buf.shape

    def body(b_ref, o_ref):
        acc = b_ref[0].astype(f32)
        for k in range(1, nb):
            acc = acc + b_ref[k].astype(f32)
        o_ref[...] = acc

    return pl.pallas_call(
        body, name=name, grid=(r // tr,),
        in_specs=[pl.BlockSpec((nb, tr, n), lambda i: (0, i, 0))],
        out_specs=pl.BlockSpec((tr, n), lambda i: (i, 0)),
        out_shape=jax.ShapeDtypeStruct((r, n), f32), compiler_params=_cp(("parallel",)))(buf)


def _ada_fwd(c_all, ada_w, ada_b_cols):
    n = ada_w.shape[-1]

    def body(c_ref, w_ref, b_ref, o_ref):
        cv = c_ref[...]
        ca = (cv * _sigmoid(cv)).astype(bf16)
        o_ref[...] = _dot(ca, w_ref[...].astype(bf16)) + b_ref[...]

    return pl.pallas_call(
        body, name="ada_fwd", grid=(DEPTH,),
        in_specs=[_const_spec((N_DEV, D)), pl.BlockSpec((None, D, n), lambda l: (l, 0, 0)),
                  pl.BlockSpec((None, 1, n), lambda l: (l, 0, 0))],
        out_specs=pl.BlockSpec((None, N_DEV, n), lambda l: (l, 0, 0)),
        out_shape=jax.ShapeDtypeStruct((DEPTH, N_DEV, n), f32), compiler_params=_cp(("parallel",)))(c_all, ada_w, ada_b_cols)


def _ada_bwd(c_all, dmod_cols):
    n = dmod_cols.shape[-1]

    def body(c_ref, d_ref, o_ref):
        cv = c_ref[...]
        ca = (cv * _sigmoid(cv)).astype(bf16)
        o_ref[...] = _dot_tn(ca, d_ref[...].astype(bf16))

    return pl.pallas_call(
        body, name="ada_bwd", grid=(DEPTH,),
        in_specs=[_const_spec((N_DEV, D)), pl.BlockSpec((None, N_DEV, n), lambda l: (l, 0, 0))],
        out_specs=pl.BlockSpec((None, D, n), lambda l: (l, 0, 0)),
        out_shape=jax.ShapeDtypeStruct((DEPTH, D, n), f32), compiler_params=_cp(("parallel",)))(c_all, dmod_cols)


def _adamw(w, g, m, v, *, tr, name):
    r, n = w.shape

    def body(w_ref, g_ref, m_ref, v_ref, d_ref, mo_ref, vo_ref):
        g_ = g_ref[...]
        m_ = ADAM_B1 * m_ref[...] + (1.0 - ADAM_B1) * g_
        v_ = ADAM_B2 * v_ref[...] + (1.0 - ADAM_B2) * jnp.square(g_)
        m_hat = m_ / (1.0 - ADAM_B1 ** ADAM_STEP)
        v_hat = v_ / (1.0 - ADAM_B2 ** ADAM_STEP)
        d_ref[...] = -ADAM_LR * (m_hat / (jnp.sqrt(v_hat) + ADAM_EPS) + ADAM_WD * w_ref[...])
        mo_ref[...] = m_
        vo_ref[...] = v_

    blk = pl.BlockSpec((tr, n), lambda i: (i, 0))
    return pl.pallas_call(
        body, name=name, grid=(r // tr,), in_specs=[blk] * 4, out_specs=[blk] * 3,
        out_shape=[jax.ShapeDtypeStruct((r, n), f32)] * 3, compiler_params=_cp(("parallel",)))(w, g, m, v)


def _adamw_landed(w, m, v, land0, land1, *, tr, name):
    _, r, n = w.shape
    nrt = r // tr

    def body(w_ref, m_ref, v_ref, l0_ref, l1_ref, g_ref, d_ref, mo_ref, vo_ref):
        layer = pl.program_id(0)

        def total(ref):
            acc = ref[0].astype(f32)
            for k in range(1, N_DEV):
                acc = acc + ref[k].astype(f32)
            return acc

        @pl.when(layer == 0)
        def _():
            g_ref[...] = total(l0_ref)

        @pl.when(layer == 1)
        def _():
            g_ref[...] = total(l1_ref)
        g_ = g_ref[...]
        m_ = ADAM_B1 * m_ref[...] + (1.0 - ADAM_B1) * g_
        v_ = ADAM_B2 * v_ref[...] + (1.0 - ADAM_B2) * jnp.square(g_)
        m_hat = m_ / (1.0 - ADAM_B1 ** ADAM_STEP)
        v_hat = v_ / (1.0 - ADAM_B2 ** ADAM_STEP)
        d_ref[...] = -ADAM_LR * (m_hat / (jnp.sqrt(v_hat) + ADAM_EPS) + ADAM_WD * w_ref[...])
        mo_ref[...] = m_
        vo_ref[...] = v_

    blk = pl.BlockSpec((None, tr, n), lambda l, i: (l, i, 0))
    land0_spec = pl.BlockSpec((N_DEV, tr, n), lambda l, i: (0, jnp.where(l == 0, i, nrt - 1), 0))
    land1_spec = pl.BlockSpec((N_DEV, tr, n), lambda l, i: (0, jnp.where(l == 1, i, 0), 0))
    return pl.pallas_call(
        body, name=name, grid=(DEPTH, nrt), in_specs=[blk] * 3 + [land0_spec, land1_spec], out_specs=[blk] * 4,
        out_shape=[jax.ShapeDtypeStruct(w.shape, f32)] * 4,
        compiler_params=_cp(("arbitrary", "arbitrary")))(w, m, v, land0, land1)


WEIGHTS = ["ln_in_g", "ln_in_b", "ada_w", "ada_b", "ffn1_w_in", "ffn1_w_out", "mix_w_in", "mix_w_out", "ssd_conv_w",
           "ssd_conv_b", "ssd_dt_bias", "ssd_a_log", "ssd_d", "ssd_norm_g", "fox_f_bias", "sconv_w", "ffn2_w_in",
           "ffn2_w_out", "ln_g", "ln_b"]
BIG = ["ffn1_w_in", "ffn1_w_out", "ffn2_w_in", "ffn2_w_out", "mix_w_in", "mix_w_out"]
GROUPS = {"ffn1": ["ffn1_w_in", "ffn1_w_out"], "mix": ["mix_w_in", "mix_w_out"], "ffn2": ["ffn2_w_in", "ffn2_w_out"]}
COL_SHARDED = ("ffn1_w_in", "ffn2_w_in")
SMALL_SHARDED = {"ssd_conv_w": 128, "sconv_w": 32, "ln_g": 128, "ln_b": 128}
ADAM_TR = {"ada_w": 256, "ffn1_w_in": 512, "ffn2_w_in": 512, "ffn1_w_out": 352, "ffn2_w_out": 352, "mix_w_in": 64,
           "mix_w_out": 256}
LAND_TR = {"ffn1_w_in": 176, "ffn2_w_in": 176, "ffn1_w_out": 176, "ffn2_w_out": 176, "mix_w_in": 32, "mix_w_out": 128}


def _pad_rows(v, mult=128):
    v = v.reshape(-1)
    return jnp.pad(v, (0, (-v.shape[0]) % mult))


def _pack_rows(parts, row_mult=8):
    flat = [_pad_rows(p.astype(f32)) for p in parts]
    offs, o = [], 0
    for f in flat:
        offs.append(o)
        o += f.shape[0] // 128
    buf = jnp.concatenate(flat).reshape(-1, 128)
    return jnp.pad(buf, ((0, (-buf.shape[0]) % row_mult), (0, 0))), offs


def _take(buf, off, shape):
    n = 1
    for s in shape:
        n *= s
    rows = -(-n // 128)
    lead = buf.shape[:-2]
    flat = buf[..., off:off + rows, :].reshape(lead + (rows * 128,))
    return flat[..., :n].reshape(lead + tuple(shape))


def kernel(*args):
    names = (["x", "c"] + WEIGHTS + ["loss_target"] + ["m_" + n for n in WEIGHTS] + ["v_" + n for n in WEIGHTS])
    assert len(args) == len(names)
    a = dict(zip(names, args))
    xi, yi, ci = lax.axis_index("x"), lax.axis_index("y"), lax.axis_index("c")
    me = 4 * xi + 2 * yi + ci

    small_in = [a["c"], a["ln_g"], a["ln_b"], a["ssd_conv_w"], a["sconv_w"]]
    buf, offs = _pack_rows(small_in)
    got, = _all_gather([buf], in_vmem=True, name="gather_small")
    c_all = _take(got, offs[0], (D,))
    full = {}
    for k, n in enumerate(["ln_g", "ln_b", "ssd_conv_w", "sconv_w"]):
        sh = a[n].shape
        t = _take(got, offs[k + 1], sh)
        full[n] = jnp.transpose(t, (1, 2, 0, 3)).reshape(sh[0], sh[1], N_DEV * sh[2])

    ncol = a["ada_w"].shape[-1]
    ada_b_cols = lax.dynamic_slice_in_dim(a["ada_b"], me * ncol, ncol, axis=1)[:, None, :]
    mod_cols = _ada_fwd(c_all, a["ada_w"], ada_b_cols)
    got, = _all_gather([mod_cols.reshape(DEPTH * N_DEV, ncol)], in_vmem=True, name="gather_mod")
    got = got.reshape(N_DEV, DEPTH, N_DEV, ncol)
    mod = lax.dynamic_index_in_dim(got, me, axis=2, keepdims=False)
    mod = jnp.transpose(mod, (1, 0, 2)).reshape(DEPTH, 9, D)

    flip = lambda t: jnp.swapaxes(t, 1, 2)
    shards = {n: (flip(a[n]) if n in COL_SHARDED else a[n]) for n in BIG}
    shards["mix_w_in"] = _pack_cols(a["mix_w_in"])
    def as_weights(names, got):
        return {n: (t if n in COL_SHARDED else t.reshape(-1, t.shape[-1])) for n, t in zip(names, got)}

    def group_shards(l, gn):
        return [shards[n][l].astype(bf16) for n in GROUPS[gn]]

    w_first = _all_gather(group_shards(0, "ffn1"), in_vmem=False, name="gather_weights")
    w_state, after = {}, [w_first[0]]
    for l, gn in [(0, "mix"), (0, "ffn2"), (1, "ffn1"), (1, "mix"), (1, "ffn2")]:
        w_state[l, gn], token = _spread_start(group_shards(l, gn), scatter=False, after=after,
                                              name=f"weights_{l}{gn}_start")
        after = [token]

    def weights_of(l, gn, after):
        if (l, gn) == (0, "ffn1"):
            return as_weights(GROUPS[gn], w_first)
        return as_weights(GROUPS[gn], _spread_wait(w_state[l, gn], after, scatter=False, name=f"weights_{l}{gn}_wait"))

    g_state = {}

    last_send = {}

    def grads_done(l, gn, gw, after=()):
        if (l, gn) == (0, "ffn1_w_in") and not after:
            last_send.update(gw)
            return None
        srcs = [t.reshape((N_DEV,) + shards[n].shape[1:]) for n, t in gw.items()]
        g_state[l, gn], token = _spread_start(srcs, scatter=True, after=after, name=f"grads_{l}{gn}_start")
        return token

    p = {n: a[n] for n in ("ln_in_g", "ln_in_b", "ssd_conv_b", "ssd_dt_bias", "ssd_a_log", "ssd_d", "ssd_norm_g",
                           "fox_f_bias")}
    p.update(full)
    p["mod"] = mod + after[0][0, 0]

    loss_local, grad_x, g = _local_step(a["x"][0], a["loss_target"][0], p, weights_of, grads_done)
    loss = lax.psum(loss_local, ("x", "y", "c"))

    small_names = ["mod", "ln_in_g", "ln_in_b", "ssd_conv_b", "ssd_dt_bias", "ssd_a_log", "ssd_d", "ssd_norm_g",
                   "fox_f_bias", "ln_g", "ln_b", "ssd_conv_w", "sconv_w"]
    buf, offs = _pack_rows([g[n] for n in small_names])
    got, = _all_gather([buf], in_vmem=True, name="gather_small_grads")
    tot = _sum_slots(got, tr=buf.shape[0], name="sum_small_grads")
    grads_done(0, "ffn1_w_in", last_send, after=(tot,))
    grads = {}
    for k, n in enumerate(small_names[1:], start=1):
        t = _take(tot, offs[k], g[n].shape)
        if n in SMALL_SHARDED:
            w_ = SMALL_SHARDED[n]
            t = lax.dynamic_slice_in_dim(t, me * w_, w_, axis=2)
        grads[n] = t
    grads["ada_b"] = _take(tot, offs[0], (DEPTH, 9 * D))
    dmod_all = _take(got, offs[0], (DEPTH, 9 * D))
    dmod_cols = jnp.transpose(lax.dynamic_slice_in_dim(dmod_all, me * ncol, ncol, axis=2), (1, 0, 2))
    grads["ada_w"] = _ada_bwd(c_all, dmod_cols)

    delta, new_m, new_v = {}, {}, {}

    def adamw(n):
        sh = a[n].shape
        two = lambda t: t.reshape(-1, sh[-1])
        outs = _adamw(two(a[n]), two(grads[n]), two(a["m_" + n]), two(a["v_" + n]), tr=ADAM_TR[n], name="adamw_" + n)
        delta[n], new_m[n], new_v[n] = (t.reshape(sh) for t in outs)

    small_params = [n for n in WEIGHTS if n not in ADAM_TR]
    packs = [_pack_rows([src[pre + n] for n in small_params])[0]
             for src, pre in ((a, ""), (grads, ""), (a, "m_"), (a, "v_"))]
    _, offs = _pack_rows([a[n] for n in small_params])
    outs = _adamw(*packs, tr=packs[0].shape[0], name="adamw_small")
    for k, n in enumerate(small_params):
        delta[n], new_m[n], new_v[n] = (_take(t, offs[k], a[n].shape) for t in outs)
    adamw("ada_w")
    after = grad_x
    for gn, names in [("ffn2_w_out", ["ffn2_w_out"]), ("ffn2_w_in", ["ffn2_w_in"]), ("mix", GROUPS["mix"]),
                      ("ffn1_w_out", ["ffn1_w_out"]), ("ffn1_w_in", ["ffn1_w_in"])]:
        lands = [_spread_wait(g_state[l, gn], after if l == 0 else grad_x, scatter=True, name=f"grads_{l}{gn}_wait")
                 for l in range(DEPTH)]
        for k, n in enumerate(names):
            if n == "mix_w_in":
                sums = [_sum_slots(t[k], tr=LAND_TR[n], name="sum_devices") for t in lands]
                grads[n] = _unpack_cols(jnp.stack(sums))
                adamw(n)
            elif n in COL_SHARDED:
                grads[n], delta[n], new_m[n], new_v[n] = (flip(t) for t in _adamw_landed(
                    flip(a[n]), flip(a["m_" + n]), flip(a["v_" + n]), lands[0][k], lands[1][k], tr=LAND_TR[n],
                    name="adamw_" + n))
            else:
                grads[n], delta[n], new_m[n], new_v[n] = _adamw_landed(
                    a[n], a["m_" + n], a["v_" + n], lands[0][k], lands[1][k], tr=LAND_TR[n], name="adamw_" + n)
            after = delta[n]

    return (loss, grad_x[None], *[grads[n] for n in WEIGHTS], *[delta[n] for n in WEIGHTS],
            *[new_m[n] for n in WEIGHTS], *[new_v[n] for n in WEIGHTS])
```

```python
import functools

import jax
import jax.numpy as jnp
from jax import lax
from jax.experimental import pallas as pl
from jax.experimental.pallas import tpu as pltpu

f32, bf16 = jnp.float32, jnp.bfloat16

D = 1024
F = 2816
DEPTH = 2
N_DEV = 8
SSD_W, SSD_HD, SSD_H, SSD_G, SSD_N, SSD_K = 512, 64, 8, 2, 128, 4
FOX_W, FOX_HD, FOX_H = 256, 64, 4
SC_W, SC_K = 256, 3
ALPHA = (2 * DEPTH) ** 0.25
LN_EPS = 1e-5
RMS_EPS = 1e-5
P_XBC, P_Z, P_Q, P_K, P_V, P_SB, P_SC, P_SX, P_SM = 0, 1024, 1536, 1792, 2048, 2304, 2560, 2816, 3072
P_W = 3200
SM_DT, SM_F = 0, 8
ADAM_LR, ADAM_B1, ADAM_B2, ADAM_EPS, ADAM_WD, ADAM_STEP = 0.001, 0.9, 0.999, 1e-08, 0.01, 10

VMEM_LIMIT = 56 * 1024 * 1024


def _cp(sem=None):
    return pltpu.CompilerParams(dimension_semantics=sem, vmem_limit_bytes=VMEM_LIMIT)


def _const_spec(shape):
    nd = len(shape)
    return pl.BlockSpec(shape, lambda *_: (0,) * nd, pipeline_mode=pl.Buffered(1))


def _sigmoid(x):
    return 1.0 / (1.0 + jnp.exp(-x))


def _ln_fwd(u, g, b):
    mu = jnp.mean(u, -1, keepdims=True)
    xc = u - mu
    rstd = lax.rsqrt(jnp.mean(xc * xc, -1, keepdims=True) + LN_EPS)
    xhat = xc * rstd
    return xhat * g + b, xhat, rstd


def _ln_bwd(dout, xhat, rstd, g):
    dxh = dout * g
    m1 = jnp.mean(dxh, -1, keepdims=True)
    m2 = jnp.mean(dxh * xhat, -1, keepdims=True)
    du = rstd * (dxh - m1 - xhat * m2)
    return du, jnp.sum(dout * xhat, 0, keepdims=True), jnp.sum(dout, 0, keepdims=True)


def _dot(a, b):
    return jnp.dot(a, b, preferred_element_type=f32)


def _dot_nt(a, b):
    return lax.dot_general(a, b, (((1,), (1,)), ((), ())), preferred_element_type=f32)


def _dot_tn(a, b):
    return lax.dot_general(a, b, (((0,), (0,)), ((), ())), preferred_element_type=f32)


def _dot_hi(a, b):
    return jnp.dot(a, b, preferred_element_type=f32, precision=lax.Precision.HIGHEST)


def _shift_down(cur, prev, s):
    if s == 0:
        return cur
    row = lax.broadcasted_iota(jnp.int32, cur.shape, 0)
    return jnp.where(row < s, pltpu.roll(prev, s, 0), pltpu.roll(cur, s, 0))


def _shift_up(cur, nxt, s):
    if s == 0:
        return cur
    t = cur.shape[0]
    row = lax.broadcasted_iota(jnp.int32, cur.shape, 0)
    return jnp.where(row >= t - s, pltpu.roll(nxt, t - s, 0), pltpu.roll(cur, t - s, 0))


def _ln_in_fwd(x, gb, *, tt=512):
    L = x.shape[0]

    def body(x_ref, gb_ref, o_ref):
        o_ref[...] = _ln_fwd(x_ref[...], gb_ref[0:1, :], gb_ref[1:2, :])[0]

    return pl.pallas_call(
        body, name="ln_in_fwd", grid=(L // tt,),
        in_specs=[pl.BlockSpec((tt, D), lambda i: (i, 0)), _const_spec((8, D))],
        out_specs=pl.BlockSpec((tt, D), lambda i: (i, 0)),
        out_shape=jax.ShapeDtypeStruct((L, D), f32), compiler_params=_cp(("parallel",)))(x, gb)


def _ln_in_bwd(x, dy, gb, *, tt=512):
    L = x.shape[0]

    def body(x_ref, dy_ref, gb_ref, dx_ref, acc_ref):
        @pl.when(pl.program_id(0) == 0)
        def _():
            acc_ref[...] = jnp.zeros_like(acc_ref)
        _, xhat, rstd = _ln_fwd(x_ref[...], gb_ref[0:1, :], gb_ref[1:2, :])
        du, dg, db = _ln_bwd(dy_ref[...], xhat, rstd, gb_ref[0:1, :])
        dx_ref[...] = du
        acc_ref[0:1, :] += dg
        acc_ref[1:2, :] += db

    return pl.pallas_call(
        body, name="ln_in_bwd", grid=(L // tt,),
        in_specs=[pl.BlockSpec((tt, D), lambda i: (i, 0)), pl.BlockSpec((tt, D), lambda i: (i, 0)), _const_spec((8, D))],
        out_specs=[pl.BlockSpec((tt, D), lambda i: (i, 0)), pl.BlockSpec((8, D), lambda i: (0, 0))],
        out_shape=[jax.ShapeDtypeStruct((L, D), f32), jax.ShapeDtypeStruct((8, D), f32)],
        compiler_params=_cp(("arbitrary",)))(x, dy, gb)


def _loss_head(y, tgt, *, tt=512):
    L = y.shape[0]

    def body(y_ref, t_ref, dy_ref, acc_ref):
        @pl.when(pl.program_id(0) == 0)
        def _():
            acc_ref[...] = jnp.zeros_like(acc_ref)
        e = y_ref[...] - t_ref[...]
        dy_ref[...] = e * (1.0 / D)
        acc_ref[...] += 0.5 * jnp.sum(jnp.mean(e * e, -1, keepdims=True))

    return pl.pallas_call(
        body, name="loss_head", grid=(L // tt,),
        in_specs=[pl.BlockSpec((tt, D), lambda i: (i, 0)), pl.BlockSpec((tt, D), lambda i: (i, 0))],
        out_specs=[pl.BlockSpec((tt, D), lambda i: (i, 0)), pl.BlockSpec((8, 128), lambda i: (0, 0))],
        out_shape=[jax.ShapeDtypeStruct((L, D), f32), jax.ShapeDtypeStruct((8, 128), f32)],
        compiler_params=_cp(("arbitrary",)))(y, tgt)


FFN_CH = 4
FS = F // FFN_CH


def _ffn_fwd(x, mv, w_in, w_out, *, tt=256):
    L = x.shape[0]

    def body(x_ref, mv_ref, wi_ref, wo_ref, o_ref):
        x = x_ref[...]
        h = (x * (1.0 + mv_ref[1:2, :]) + mv_ref[0:1, :]).astype(bf16)
        y = jnp.zeros((tt, D), f32)
        for c in range(FFN_CH):
            g = _dot_nt(h, wi_ref[c])
            u = _dot_nt(h, wi_ref[c + FFN_CH])
            act = (g * _sigmoid(g) * u).astype(bf16)
            y = y + _dot(act, wo_ref[c * FS:(c + 1) * FS, :])
        uu = ALPHA * x + (0.5 * mv_ref[2:3, :]) * y
        o_ref[...] = _ln_fwd(uu, mv_ref[3:4, :], mv_ref[4:5, :])[0]

    return pl.pallas_call(
        body, name="ffn_fwd", grid=(L // tt,),
        in_specs=[pl.BlockSpec((tt, D), lambda i: (i, 0)), _const_spec((8, D)),
                  _const_spec((2 * FFN_CH, FS, D)), _const_spec((F, D))],
        out_specs=pl.BlockSpec((tt, D), lambda i: (i, 0)),
        out_shape=jax.ShapeDtypeStruct((L, D), f32), compiler_params=_cp(("parallel",)))(x, mv, w_in, w_out)


def _ffn_bwd(x, dxo, mv, w_in, w_out, *, tt=256):
    L = x.shape[0]

    def body(x_ref, dxo_ref, mv_ref, wi_ref, wo_ref, dx_ref, h_ref, da_ref, act_ref, dy_ref, acc_ref, a_scr):
        @pl.when(pl.program_id(0) == 0)
        def _():
            acc_ref[...] = jnp.zeros_like(acc_ref)
        x = x_ref[...]
        scale1 = 1.0 + mv_ref[1:2, :]
        h = (x * scale1 + mv_ref[0:1, :]).astype(bf16)
        h_ref[...] = h
        y = jnp.zeros((tt, D), f32)
        for c in range(FFN_CH):
            g = _dot_nt(h, wi_ref[c])
            u = _dot_nt(h, wi_ref[c + FFN_CH])
            a_scr[c] = g
            a_scr[c + FFN_CH] = u
            act = (g * _sigmoid(g) * u).astype(bf16)
            act_ref[c] = act.T
            y = y + _dot(act, wo_ref[c * FS:(c + 1) * FS, :])
        hg = 0.5 * mv_ref[2:3, :]
        _, xhat, rstd = _ln_fwd(ALPHA * x + hg * y, mv_ref[3:4, :], mv_ref[4:5, :])
        du, dlg, dlb = _ln_bwd(dxo_ref[...], xhat, rstd, mv_ref[3:4, :])
        acc_ref[3:4, :] += dlg
        acc_ref[4:5, :] += dlb
        acc_ref[2:3, :] += jnp.sum(0.5 * y * du, 0, keepdims=True)
        dyb = (hg * du).astype(bf16)
        dy_ref[...] = dyb
        dh = jnp.zeros((tt, D), f32)
        for c in range(FFN_CH):
            g = a_scr[c]
            u = a_scr[c + FFN_CH]
            dact = _dot_nt(dyb, wo_ref[c * FS:(c + 1) * FS, :])
            s = _sigmoid(g)
            dg = (dact * u * (s * (1.0 + g * (1.0 - s)))).astype(bf16)
            dup = (dact * (g * s)).astype(bf16)
            da_ref[c] = dg.T
            da_ref[c + FFN_CH] = dup.T
            dh = dh + _dot(dg, wi_ref[c])
            dh = dh + _dot(dup, wi_ref[c + FFN_CH])
        dx_ref[...] = ALPHA * du + dh * scale1
        acc_ref[0:1, :] += jnp.sum(dh, 0, keepdims=True)
        acc_ref[1:2, :] += jnp.sum(dh * x, 0, keepdims=True)

    tok = lambda w: pl.BlockSpec((tt, w), lambda i: (i, 0))
    by_chunk = lambda n: pl.BlockSpec((n, FS, tt), lambda i: (0, 0, i))
    return pl.pallas_call(
        body, name="ffn_bwd", grid=(L // tt,),
        in_specs=[tok(D), tok(D), _const_spec((8, D)), _const_spec((2 * FFN_CH, FS, D)), _const_spec((F, D))],
        out_specs=[tok(D), tok(D), by_chunk(2 * FFN_CH), by_chunk(FFN_CH), tok(D), pl.BlockSpec((8, D), lambda i: (0, 0))],
        out_shape=[jax.ShapeDtypeStruct((L, D), f32), jax.ShapeDtypeStruct((L, D), bf16),
                   jax.ShapeDtypeStruct((2 * FFN_CH, FS, L), bf16), jax.ShapeDtypeStruct((FFN_CH, FS, L), bf16),
                   jax.ShapeDtypeStruct((L, D), bf16), jax.ShapeDtypeStruct((8, D), f32)],
        scratch_shapes=[pltpu.VMEM((2 * FFN_CH, tt, FS), f32)],
        compiler_params=_cp(("arbitrary",)))(x, dxo, mv, w_in, w_out)


DW_TK = 4096


def _matmul_tokens(a, b, *, tn, tk, name, after=None):
    ga, gb = a.ndim == 3, b.ndim == 3
    extra = [] if after is None else [after]
    G = a.shape[0] if ga else (b.shape[0] if gb else 1)
    M, K = a.shape[-2:]
    N = b.shape[-1]
    tk = min(tk, K)
    nk = K // tk

    def body(a_ref, b_ref, *rest):
        o_ref, acc = rest[len(extra):]
        k = pl.program_id(2)
        p = _dot(a_ref[...], b_ref[...])

        @pl.when(k == 0)
        def _():
            acc[...] = p

        @pl.when(k > 0)
        def _():
            acc[...] += p

        @pl.when(k == nk - 1)
        def _():
            o_ref[...] = acc[...].astype(bf16)

    a_spec = (pl.BlockSpec((None, M, tk), lambda g, j, k: (g, 0, k)) if ga
              else pl.BlockSpec((M, tk), lambda g, j, k: (0, k)))
    b_spec = (pl.BlockSpec((None, tk, tn), lambda g, j, k: (g, k, j)) if gb
              else pl.BlockSpec((tk, tn), lambda g, j, k: (k, j)))
    if ga or gb:
        o_spec, o_shape = pl.BlockSpec((None, M, tn), lambda g, j, k: (g, 0, j)), (G, M, N)
    else:
        o_spec, o_shape = pl.BlockSpec((M, tn), lambda g, j, k: (0, j)), (M, N)
    return pl.pallas_call(
        body, name=name, grid=(G, N // tn, nk), in_specs=[a_spec, b_spec] + [ANY] * len(extra), out_specs=o_spec,
        out_shape=jax.ShapeDtypeStruct(o_shape, bf16), scratch_shapes=[pltpu.VMEM((M, tn), f32)],
        compiler_params=_cp(("parallel", "parallel", "arbitrary")))(a, b, *extra)


def _inproj_fwd(x, mv, w, *, tt=512):
    L = x.shape[0]

    def body(x_ref, mv_ref, w_ref, o_ref):
        h = (x_ref[...] * (1.0 + mv_ref[1:2, :]) + mv_ref[0:1, :]).astype(bf16)
        o_ref[...] = _dot(h, w_ref[...])

    return pl.pallas_call(
        body, name="inproj_fwd", grid=(L // tt,),
        in_specs=[pl.BlockSpec((tt, D), lambda i: (i, 0)), _const_spec((8, D)), _const_spec((D, P_W))],
        out_specs=pl.BlockSpec((tt, P_W), lambda i: (i, 0)),
        out_shape=jax.ShapeDtypeStruct((L, P_W), f32), compiler_params=_cp(("parallel",)))(x, mv, w)


def _inproj_bwd(x, dx_part, dproj, mv, w, *, tt=512):
    L = x.shape[0]

    def body(x_ref, dxp_ref, dp_ref, mv_ref, w_ref, dx_ref, h_ref, acc_ref):
        @pl.when(pl.program_id(0) == 0)
        def _():
            acc_ref[...] = jnp.zeros_like(acc_ref)
        x = x_ref[...]
        scale1 = 1.0 + mv_ref[1:2, :]
        h_ref[...] = (x * scale1 + mv_ref[0:1, :]).astype(bf16).T
        dh = _dot_nt(dp_ref[...], w_ref[...])
        dx_ref[...] = dxp_ref[...] + dh * scale1
        acc_ref[0:1, :] += jnp.sum(dh, 0, keepdims=True)
        acc_ref[1:2, :] += jnp.sum(dh * x, 0, keepdims=True)

    tok = lambda w_: pl.BlockSpec((tt, w_), lambda i: (i, 0))
    return pl.pallas_call(
        body, name="inproj_bwd", grid=(L // tt,),
        in_specs=[tok(D), tok(D), tok(P_W), _const_spec((8, D)), _const_spec((D, P_W))],
        out_specs=[tok(D), pl.BlockSpec((D, tt), lambda i: (0, i)), pl.BlockSpec((8, D), lambda i: (0, 0))],
        out_shape=[jax.ShapeDtypeStruct((L, D), f32), jax.ShapeDtypeStruct((D, L), bf16),
                   jax.ShapeDtypeStruct((8, D), f32)],
        compiler_params=_cp(("arbitrary",)))(x, dx_part, dproj, mv, w)


def _outproj_fwd(x, ycat, mv, w, *, tt=512):
    L = x.shape[0]

    def body(x_ref, y_ref, mv_ref, w_ref, o_ref):
        y = _dot(y_ref[...], w_ref[...])
        uu = ALPHA * x_ref[...] + mv_ref[2:3, :] * y
        o_ref[...] = _ln_fwd(uu, mv_ref[3:4, :], mv_ref[4:5, :])[0]

    tok = lambda w_: pl.BlockSpec((tt, w_), lambda i: (i, 0))
    return pl.pallas_call(
        body, name="outproj_fwd", grid=(L // tt,),
        in_specs=[tok(D), tok(D), _const_spec((8, D)), _const_spec((D, D))],
        out_specs=tok(D),
        out_shape=jax.ShapeDtypeStruct((L, D), f32), compiler_params=_cp(("parallel",)))(x, ycat, mv, w)


def _outproj_bwd(x, ycat, dxo, mv, w, *, tt=512):
    L = x.shape[0]

    def body(x_ref, y_ref, dxo_ref, mv_ref, w_ref, dx_ref, dy_ref, dyc_ref, yt_ref, acc_ref):
        @pl.when(pl.program_id(0) == 0)
        def _():
            acc_ref[...] = jnp.zeros_like(acc_ref)
        yt_ref[...] = y_ref[...].T
        y = _dot(y_ref[...], w_ref[...])
        gate = mv_ref[2:3, :]
        _, xhat, rstd = _ln_fwd(ALPHA * x_ref[...] + gate * y, mv_ref[3:4, :], mv_ref[4:5, :])
        du, dlg, dlb = _ln_bwd(dxo_ref[...], xhat, rstd, mv_ref[3:4, :])
        acc_ref[3:4, :] += dlg
        acc_ref[4:5, :] += dlb
        acc_ref[2:3, :] += jnp.sum(y * du, 0, keepdims=True)
        dx_ref[...] = ALPHA * du
        dyb = (gate * du).astype(bf16)
        dy_ref[...] = dyb
        dyc_ref[...] = _dot_nt(dyb, w_ref[...])

    tok = lambda w_: pl.BlockSpec((tt, w_), lambda i: (i, 0))
    return pl.pallas_call(
        body, name="outproj_bwd", grid=(L // tt,),
        in_specs=[tok(D), tok(D), tok(D), _const_spec((8, D)), _const_spec((D, D))],
        out_specs=[tok(D), tok(D), tok(D), pl.BlockSpec((D, tt), lambda i: (0, i)), pl.BlockSpec((8, D), lambda i: (0, 0))],
        out_shape=[jax.ShapeDtypeStruct((L, D), f32), jax.ShapeDtypeStruct((L, D), bf16),
                   jax.ShapeDtypeStruct((L, D), f32), jax.ShapeDtypeStruct((D, L), bf16),
                   jax.ShapeDtypeStruct((8, D), f32)],
        compiler_params=_cp(("arbitrary",)))(x, ycat, dxo, mv, w)


def _sconv_fwd(proj, w, *, tt=512):
    L = proj.shape[0]
    cb = SC_W

    def body(b_ref, c_ref, x_ref, cp_ref, xp_ref, w_ref, o_ref):
        first = jnp.where(pl.program_id(0) > 0, 1.0, 0.0)
        u = c_ref[...] * x_ref[...]
        up = cp_ref[...] * xp_ref[...] * first
        v = w_ref[2:3, :] * u + w_ref[1:2, :] * _shift_down(u, up, 1) + w_ref[0:1, :] * _shift_down(u, up, 2)
        o_ref[...] = (b_ref[...] * v).astype(bf16)

    cur = lambda col: pl.BlockSpec((tt, cb), lambda i: (i, col // cb))
    prev = lambda col: pl.BlockSpec((tt, cb), lambda i: (jnp.maximum(i - 1, 0), col // cb))
    return pl.pallas_call(
        body, name="sconv_fwd", grid=(L // tt,),
        in_specs=[cur(P_SB), cur(P_SC), cur(P_SX), prev(P_SC), prev(P_SX), _const_spec((8, cb))],
        out_specs=pl.BlockSpec((tt, cb), lambda i: (i, 0)),
        out_shape=jax.ShapeDtypeStruct((L, cb), bf16), compiler_params=_cp(("parallel",)))(proj, proj, proj, proj, proj, w)


def _sconv_bwd(proj, dycat, w, *, tt=512):
    L = proj.shape[0]
    cb = SC_W
    n = L // tt

    def body(b_ref, c_ref, x_ref, cp_ref, xp_ref, bn_ref, dy_ref, dyn_ref, w_ref, db_ref, dc_ref, dx_ref, acc_ref):
        i = pl.program_id(0)

        @pl.when(i == 0)
        def _():
            acc_ref[...] = jnp.zeros_like(acc_ref)
        first = jnp.where(i > 0, 1.0, 0.0)
        last = jnp.where(i < n - 1, 1.0, 0.0)
        cg, xin, bg = c_ref[...], x_ref[...], b_ref[...]
        u = cg * xin
        up = cp_ref[...] * xp_ref[...] * first
        u1, u2 = _shift_down(u, up, 1), _shift_down(u, up, 2)
        v = w_ref[2:3, :] * u + w_ref[1:2, :] * u1 + w_ref[0:1, :] * u2
        dy = dy_ref[...]
        db_ref[...] = (dy * v).astype(bf16)
        dv = dy * bg
        dvn = dyn_ref[...] * bn_ref[...] * last
        du = w_ref[2:3, :] * dv + w_ref[1:2, :] * _shift_up(dv, dvn, 1) + w_ref[0:1, :] * _shift_up(dv, dvn, 2)
        acc_ref[2:3, :] += jnp.sum(dv * u, 0, keepdims=True)
        acc_ref[1:2, :] += jnp.sum(dv * u1, 0, keepdims=True)
        acc_ref[0:1, :] += jnp.sum(dv * u2, 0, keepdims=True)
        dc_ref[...] = (du * xin).astype(bf16)
        dx_ref[...] = (du * cg).astype(bf16)

    cur = lambda col: pl.BlockSpec((tt, cb), lambda i: (i, col // cb))
    prev = lambda col: pl.BlockSpec((tt, cb), lambda i: (jnp.maximum(i - 1, 0), col // cb))
    nxt = lambda col: pl.BlockSpec((tt, cb), lambda i: (jnp.minimum(i + 1, n - 1), col // cb))
    ycol = SSD_W + FOX_W
    out = pl.BlockSpec((tt, cb), lambda i: (i, 0))
    return pl.pallas_call(
        body, name="sconv_bwd", grid=(n,),
        in_specs=[cur(P_SB), cur(P_SC), cur(P_SX), prev(P_SC), prev(P_SX), nxt(P_SB), cur(ycol), nxt(ycol),
                  _const_spec((8, cb))],
        out_specs=[out, out, out, pl.BlockSpec((8, cb), lambda i: (0, 0))],
        out_shape=[jax.ShapeDtypeStruct((L, cb), bf16)] * 3 + [jax.ShapeDtypeStruct((8, cb), f32)],
        compiler_params=_cp(("arbitrary",)))(proj, proj, proj, proj, proj, proj, dycat, dycat, w)


def _log1pexp(x):
    return jnp.log(1.0 + jnp.exp(-jnp.abs(x)))


NEG = -1e30
FOX_SCALE = FOX_HD ** -0.5


HL = 128
AW = FOX_H * HL


def _np_place(rows, cols, pairs, dtype):
    import numpy as np
    m = np.zeros((rows, cols), np.float32)
    for r, c in pairs:
        m[r, c] = 1.0
    return jnp.asarray(m, dtype)


def _fox_consts():
    data = [(h * FOX_HD + d, h * HL + d) for h in range(FOX_H) for d in range(FOX_HD)]
    return dict(
        pq=_np_place(FOX_W, AW, data, bf16),
        pqt=_np_place(AW, FOX_W, [(c, r) for r, c in data], bf16),
        cum_a=[_np_place(128, AW, [(SM_F + h, h * HL + 64 + r) for h in range(FOX_H)], bf16) for r in range(3)],
        head_a=[_np_place(128, AW, [(h, h * HL + 64 + r) for h in range(FOX_H)], bf16) for r in range(3)],
        head_b=[_np_place(128, AW, [(h, h * HL + 67 + r) for h in range(FOX_H)], bf16) for r in range(3)],
        group=_np_place(FOX_W, 128, [(h * FOX_HD + d, h) for h in range(FOX_H) for d in range(FOX_HD)], f32),
        col_a=_np_place(AW, 128, [(h * HL + 64, SM_F + h) for h in range(FOX_H)], f32))


def _split3(x):
    hi = x.astype(bf16)
    r1 = x - hi.astype(f32)
    mid = r1.astype(bf16)
    return hi, mid, (r1 - mid.astype(f32)).astype(bf16)


def _slot_ones(tt, first):
    lane = lax.broadcasted_iota(jnp.int32, (tt, AW), 1) % HL
    return jnp.where((lane >= first) & (lane < first + 3), 1.0, 0.0)


def _fox_prep(proj, hp, cst, *, tt=256):
    L = proj.shape[0]

    def body(q_ref, k_ref, v_ref, sm_ref, hp_ref, pq_ref, c0_ref, c1_ref, c2_ref, qa_ref, ka_ref, va_ref, tot_ref):
        xx = sm_ref[...] + hp_ref[3:4, :]
        logf = jnp.minimum(xx, 0.0) - _log1pexp(xx)
        r = lax.broadcasted_iota(jnp.int32, (tt, tt), 0)
        c = lax.broadcasted_iota(jnp.int32, (tt, tt), 1)
        cum = _dot_hi(jnp.where(r >= c, 1.0, 0.0), logf)
        tot_ref[...] = jnp.broadcast_to(cum[tt - 1:tt, :], (8, 128))
        parts = _split3(-cum)
        pq = pq_ref[...]
        a_ones, b_ones = _slot_ones(tt, 64), _slot_ones(tt, 67)
        qa_ref[...] = (_dot((q_ref[...] * FOX_SCALE).astype(bf16), pq) + a_ones).astype(bf16)
        ka = _dot(k_ref[...].astype(bf16), pq) + b_ones
        for part, c_ref in zip(parts, (c0_ref, c1_ref, c2_ref)):
            ka = ka + _dot(part, c_ref[...])
        ka_ref[...] = ka.astype(bf16)
        va_ref[...] = (_dot(v_ref[...].astype(bf16), pq) + a_ones).astype(bf16)

    col = lambda c_: pl.BlockSpec((tt, FOX_W), lambda i: (i, c_ // FOX_W))
    out = pl.BlockSpec((tt, AW), lambda i: (i, 0))
    return pl.pallas_call(
        body, name="fox_prep", grid=(L // tt,),
        in_specs=[col(P_Q), col(P_K), col(P_V), pl.BlockSpec((tt, 128), lambda i: (i, P_SM // 128)),
                  _const_spec((8, 128)), _const_spec((FOX_W, AW))] + [_const_spec((128, AW))] * 3,
        out_specs=[out, out, out, pl.BlockSpec((8, 128), lambda i: (i, 0))],
        out_shape=[jax.ShapeDtypeStruct((L, AW), bf16)] * 3 + [jax.ShapeDtypeStruct((8 * (L // tt), 128), f32)],
        compiler_params=_cp(("parallel",)))(proj, proj, proj, proj, hp, cst["pq"], *cst["cum_a"])


def _fox_attn_fwd(qa, ka, va, tot, *, tq=256):
    L = qa.shape[0]

    def body(qa_ref, ka_ref, va_ref, tot_ref, o_ref, oa_ref, lse_ref):
        i = pl.program_id(0)
        row = lax.broadcasted_iota(jnp.int32, (tq, tq), 0)
        col = lax.broadcasted_iota(jnp.int32, (tq, tq), 1)
        diag_bias = jnp.where(row >= col, 0.0, NEG)

        def block(j, carry, t_j, bias):
            r0 = pl.multiple_of(j * tq, tq)
            hls = [slice(h * HL, (h + 1) * HL) for h in range(FOX_H)]
            ss = [_dot_nt(qa_ref[:, hl], ka_ref[pl.ds(r0, tq), hl]) for hl in hls]
            if bias is not None:
                ss = [s + bias for s in ss]
            m_ins = [carry[h][0] - t_j[h] for h in range(FOX_H)]
            m_news = [jnp.maximum(m_in, jnp.max(s, -1, keepdims=True)) for m_in, s in zip(m_ins, ss)]
            ps = [jnp.exp(s - m_new) for s, m_new in zip(ss, m_news)]
            pbs = [p.astype(bf16) for p in ps]
            p_los = [(p - pb.astype(f32)).astype(bf16) for p, pb in zip(ps, pbs)]
            pvs = [_dot(pb, va_ref[pl.ds(r0, tq), hl]) + _dot(p_lo, va_ref[pl.ds(r0, tq), hl])
                   for pb, p_lo, hl in zip(pbs, p_los, hls)]
            return tuple((m_new, jnp.exp(m_in - m_new) * carry[h][1] + pv)
                         for h, (m_in, m_new, pv) in enumerate(zip(m_ins, m_news, pvs)))

        def step(k, state):
            carry, gap = state
            j = i - 1 - k
            t_j = [tot_ref[j, h] for h in range(FOX_H)]
            return block(j, carry, t_j, None), tuple(g + t for g, t in zip(gap, t_j))

        init = tuple((jnp.full((tq, 1), NEG, f32), jnp.zeros((tq, HL), f32)) for _ in range(FOX_H))
        zero_gap = tuple(jnp.zeros((), f32) for _ in range(FOX_H))
        carry = block(i, init, zero_gap, diag_bias)
        carry, gap = lax.fori_loop(0, i, step, (carry, zero_gap))
        lane = lax.broadcasted_iota(jnp.int32, (tq, 128), 1)
        lse_all = jnp.zeros((tq, 128), f32)
        for h in range(FOX_H):
            hs = slice(h * FOX_HD, (h + 1) * FOX_HD)
            m, acc = carry[h]
            l = acc[:, FOX_HD:FOX_HD + 1]
            o = acc[:, :FOX_HD] * (1.0 / l)
            o_ref[:, hs] = o.astype(bf16)
            oa_ref[:, hs] = o
            lse_all = jnp.where(lane == h, m + gap[h] + jnp.log(l), lse_all)
        lse_ref[...] = lse_all

    full = pl.BlockSpec((L, AW), lambda i: (0, 0), pipeline_mode=pl.Buffered(1))
    return pl.pallas_call(
        body, name="fox_fwd", grid=(L // tq,),
        in_specs=[pl.BlockSpec((tq, AW), lambda i: (i, 0)), full, full, pl.BlockSpec(memory_space=pltpu.SMEM)],
        out_specs=[pl.BlockSpec((tq, FOX_W), lambda i: (i, 0)), pl.BlockSpec((tq, FOX_W), lambda i: (i, 0)),
                   pl.BlockSpec((tq, 128), lambda i: (i, 0))],
        out_shape=[jax.ShapeDtypeStruct((L, FOX_W), bf16), jax.ShapeDtypeStruct((L, FOX_W), f32),
                   jax.ShapeDtypeStruct((L, 128), f32)],
        compiler_params=_cp(("parallel",)))(qa, ka, va, tot)


def _fox_bprep(qa, dycat, o_acc, lse, cst, *, tt=256):
    L = qa.shape[0]

    def body(qa_ref, do_ref, oa_ref, lse_ref, pq_ref, g_ref, a0, a1, a2, b0, b1, b2, qb_ref, doa_ref, qbt_ref, doat_ref):
        dob = do_ref[...].astype(bf16)
        delta = _dot_hi(dob.astype(f32) * oa_ref[...], g_ref[...])
        doa = _dot(dob, pq_ref[...])
        for part, ref in zip(_split3(-delta), (a0, a1, a2)):
            doa = doa + _dot(part, ref[...])
        qb = qa_ref[...].astype(f32)
        for part, ref in zip(_split3(-lse_ref[...]), (b0, b1, b2)):
            qb = qb + _dot(part, ref[...])
        doa, qb = doa.astype(bf16), qb.astype(bf16)
        doa_ref[...] = doa
        qb_ref[...] = qb
        doat_ref[...] = doa.T
        qbt_ref[...] = qb.T

    tok = lambda w_: pl.BlockSpec((tt, w_), lambda i: (i, 0))
    tr = pl.BlockSpec((AW, tt), lambda i: (0, i))
    return pl.pallas_call(
        body, name="fox_bprep", grid=(L // tt,),
        in_specs=[tok(AW), pl.BlockSpec((tt, FOX_W), lambda i: (i, SSD_W // FOX_W)), tok(FOX_W), tok(128),
                  _const_spec((FOX_W, AW)), _const_spec((FOX_W, 128))] + [_const_spec((128, AW))] * 6,
        out_specs=[tok(AW), tok(AW), tr, tr],
        out_shape=[jax.ShapeDtypeStruct((L, AW), bf16)] * 2 + [jax.ShapeDtypeStruct((AW, L), bf16)] * 2,
        compiler_params=_cp(("parallel",)))(qa, dycat, o_acc, lse, cst["pq"], cst["group"], *cst["head_a"], *cst["head_b"])


def _fox_attn_bwd(ka, va, tot, qb, doa, qbt, doat, *, tq=256):
    L = ka.shape[0]
    nq = L // tq

    def body(ka_ref, va_ref, tot_ref, qb_ref, doa_ref, qbt_ref, doat_ref, dq_ref, dkt_ref, dvt_ref):
        j = pl.program_id(0)

        @pl.when(j == 0)
        def _():
            dq_ref[...] = jnp.zeros_like(dq_ref)
        row = lax.broadcasted_iota(jnp.int32, (tq, tq), 0)
        col = lax.broadcasted_iota(jnp.int32, (tq, tq), 1)
        diag_bias = jnp.where(row >= col, 0.0, NEG)

        def block(i, carry, gap, bias):
            r0 = pl.multiple_of(i * tq, tq)
            rows = pl.ds(r0, tq)
            hls = [slice(h * HL, (h + 1) * HL) for h in range(FOX_H)]
            ss = [_dot_nt(qb_ref[rows, hl], ka_ref[:, hl]) + gap[h] for h, hl in enumerate(hls)]
            if bias is not None:
                ss = [s + bias for s in ss]
            ps = [jnp.exp(s) for s in ss]
            dss = [p * _dot_nt(doa_ref[rows, hl], va_ref[:, hl]) for p, hl in zip(ps, hls)]
            dsbs = [ds.astype(bf16) for ds in dss]
            for dsb, hl in zip(dsbs, hls):
                dq_ref[rows, hl] += _dot(dsb, ka_ref[:, hl])
            return tuple((carry[h][0] + _dot(qbt_ref[hl, rows], dsbs[h]),
                          carry[h][1] + _dot(doat_ref[hl, rows], ps[h].astype(bf16)),
                          carry[h][2] + jnp.sum(dss[h], 0, keepdims=True)) for h, hl in enumerate(hls))

        init = tuple((jnp.zeros((HL, tq), f32), jnp.zeros((HL, tq), f32), jnp.zeros((1, tq), f32))
                     for _ in range(FOX_H))
        def step(i, state):
            carry, gap = state
            gap = tuple(g + tot_ref[i - 1, h] for h, g in enumerate(gap))
            return block(i, carry, gap, None), gap

        zero_gap = tuple(jnp.zeros((), f32) for _ in range(FOX_H))
        carry = block(j, init, zero_gap, diag_bias)
        carry, _ = lax.fori_loop(j + 1, nq, step, (carry, zero_gap))
        for h in range(FOX_H):
            hl = slice(h * HL, (h + 1) * HL)
            dkt_ref[hl, :] = carry[h][0]
            dvt_ref[hl, :] = carry[h][1]
            dkt_ref[h * HL + FOX_HD:h * HL + FOX_HD + 1, :] = carry[h][2]

    full = lambda shape: pl.BlockSpec(shape, lambda j: (0, 0), pipeline_mode=pl.Buffered(1))
    blk = pl.BlockSpec((tq, AW), lambda j: (j, 0))
    trb = pl.BlockSpec((AW, tq), lambda j: (0, j))
    return pl.pallas_call(
        body, name="fox_bwd", grid=(nq,),
        in_specs=[blk, blk, pl.BlockSpec(memory_space=pltpu.SMEM), full((L, AW)), full((L, AW)), full((AW, L)),
                  full((AW, L))],
        out_specs=[pl.BlockSpec((L, AW), lambda j: (0, 0)), trb, trb],
        out_shape=[jax.ShapeDtypeStruct((L, AW), f32), jax.ShapeDtypeStruct((AW, L), f32),
                   jax.ShapeDtypeStruct((AW, L), f32)],
        compiler_params=_cp(("arbitrary",)))(ka, va, tot, qb, doa, qbt, doat)


def _fox_post(dq, dkt, dvt, proj, ddt, hp, cst, *, tt=256):
    L = proj.shape[0]
    n = L // tt

    def body(dq_ref, dkt_ref, dvt_ref, sm_ref, ddt_ref, hp_ref, pqt_ref, ca_ref,
             dqo_ref, dko_ref, dvo_ref, dsm_ref, acc_ref, carry):
        @pl.when(pl.program_id(0) == 0)
        def _():
            carry[...] = jnp.zeros_like(carry)
            acc_ref[...] = jnp.zeros_like(acc_ref)
        pqt = pqt_ref[...]
        dk_full = dkt_ref[...].T
        dqo_ref[...] = _dot((dq_ref[...] * FOX_SCALE).astype(bf16), pqt).astype(bf16)
        dko_ref[...] = _dot(dk_full.astype(bf16), pqt).astype(bf16)
        dvo_ref[...] = _dot(dvt_ref[...].T.astype(bf16), pqt).astype(bf16)
        dc = -_dot_hi(dk_full, ca_ref[...])
        r = lax.broadcasted_iota(jnp.int32, (tt, tt), 0)
        c = lax.broadcasted_iota(jnp.int32, (tt, tt), 1)
        dl = _dot_hi(jnp.where(r <= c, 1.0, 0.0), dc) + carry[0:1, :]
        carry[0:1, :] += jnp.sum(dc, 0, keepdims=True)
        xx = sm_ref[...] + hp_ref[3:4, :]
        lane = lax.broadcasted_iota(jnp.int32, (tt, 128), 1)
        dlogit = jnp.where((lane >= SM_F) & (lane < SM_F + FOX_H), dl * _sigmoid(-xx), 0.0)
        acc_ref[3:4, :] += jnp.sum(dlogit, 0, keepdims=True)
        dsm_ref[...] = (dlogit + ddt_ref[...]).astype(bf16)

    rev = lambda w_: pl.BlockSpec((tt, w_), lambda i: (n - 1 - i, 0))
    revt = pl.BlockSpec((AW, tt), lambda i: (0, n - 1 - i))
    return pl.pallas_call(
        body, name="fox_post", grid=(n,),
        in_specs=[rev(AW), revt, revt, pl.BlockSpec((tt, 128), lambda i: (n - 1 - i, P_SM // 128)), rev(128),
                  _const_spec((8, 128)), _const_spec((AW, FOX_W)), _const_spec((AW, 128))],
        out_specs=[rev(FOX_W), rev(FOX_W), rev(FOX_W), rev(128), pl.BlockSpec((8, 128), lambda i: (0, 0))],
        out_shape=[jax.ShapeDtypeStruct((L, FOX_W), bf16)] * 3 + [jax.ShapeDtypeStruct((L, 128), bf16),
                                                                   jax.ShapeDtypeStruct((8, 128), f32)],
        scratch_shapes=[pltpu.VMEM((8, 128), f32)],
        compiler_params=_cp(("arbitrary",)))(dq, dkt, dvt, proj, ddt, hp, cst["pqt"], cst["col_a"])


SSD_GW = SSD_W // SSD_G
SSD_HPG = SSD_H // SSD_G


def _ssd_pre(x, xprev, sm, cp_ref, hp_ref, tc):
    pre = (cp_ref[4:5, :] + cp_ref[3:4, :] * x + cp_ref[2:3, :] * _shift_down(x, xprev, 1)
           + cp_ref[1:2, :] * _shift_down(x, xprev, 2) + cp_ref[0:1, :] * _shift_down(x, xprev, 3))
    sig = _sigmoid(pre)
    raw = sm + hp_ref[0:1, :]
    dt = jnp.maximum(raw, 0.0) + _log1pexp(raw)
    a_neg = -jnp.exp(hp_ref[1:2, :])
    r = lax.broadcasted_iota(jnp.int32, (tc, tc), 0)
    c = lax.broadcasted_iota(jnp.int32, (tc, tc), 1)
    cs = _dot_hi(jnp.where(r >= c, 1.0, 0.0), dt * a_neg)
    return pre, sig, raw, dt, a_neg, cs, cs.T, r >= c


def _ssd_fwd(proj, cp, hp, ng, *, tc=256):
    L = proj.shape[0]
    nc = L // tc

    def body(xc_ref, xp_ref, z_ref, sm_ref, cp_ref, hp_ref, ng_ref, y_ref, ypre_ref, sin_ref, s_scr):
        i = pl.program_id(0)

        @pl.when(i == 0)
        def _():
            s_scr[...] = jnp.zeros_like(s_scr)
        x = xc_ref[...]
        xprev = xp_ref[...] * jnp.where(i > 0, 1.0, 0.0)
        pre, sig, _, dt, _, cs, cst, tril = _ssd_pre(x, xprev, sm_ref[...], cp_ref, hp_ref, tc)
        xbc = pre * sig
        sin_ref[...] = s_scr[...]
        for g in range(SSD_G):
            bg = xbc[:, SSD_W + g * SSD_N:SSD_W + (g + 1) * SSD_N]
            cg = xbc[:, SSD_W + SSD_G * SSD_N + g * SSD_N:SSD_W + SSD_G * SSD_N + (g + 1) * SSD_N].astype(bf16)
            cb = _dot_nt(cg, bg.astype(bf16))
            for e in range(SSD_HPG):
                h = g * SSD_HPG + e
                hs = slice(h * SSD_HD, (h + 1) * SSD_HD)
                xs = xbc[:, hs]
                csc = cs[:, h:h + 1]
                lm = jnp.where(tril, jnp.exp(jnp.minimum(csc - cst[h:h + 1, :], 0.0)), 0.0)
                xdt = (xs * dt[:, h:h + 1]).astype(bf16)
                s_h = s_scr[:, hs]
                y = _dot((cb * lm).astype(bf16), xdt) + jnp.exp(csc) * _dot(cg, s_h.astype(bf16))
                ypre_ref[:, hs] = y + hp_ref[2:3, h:h + 1] * xs
                cl = cs[tc - 1:tc, h:h + 1]
                bd = (bg * jnp.exp(cl - csc)).astype(bf16)
                s_scr[:, hs] = jnp.exp(cl) * s_h + _dot_tn(bd, xdt)
        z = z_ref[...]
        yz = ypre_ref[...] * (z * _sigmoid(z))
        for g in range(SSD_G):
            gs = slice(g * SSD_GW, (g + 1) * SSD_GW)
            yg = yz[:, gs]
            r = lax.rsqrt(jnp.mean(yg * yg, -1, keepdims=True) + RMS_EPS)
            y_ref[:, gs] = (yg * r * ng_ref[0:1, gs]).astype(bf16)

    return pl.pallas_call(
        body, name="ssd_fwd", grid=(nc,),
        in_specs=[pl.BlockSpec((tc, 1024), lambda i: (i, 0)),
                  pl.BlockSpec((tc, 1024), lambda i: (jnp.maximum(i - 1, 0), 0)),
                  pl.BlockSpec((tc, SSD_W), lambda i: (i, P_Z // SSD_W)),
                  pl.BlockSpec((tc, 128), lambda i: (i, P_SM // 128)),
                  _const_spec((8, 1024)), _const_spec((8, 128)), _const_spec((8, SSD_W))],
        out_specs=[pl.BlockSpec((tc, SSD_W), lambda i: (i, 0)), pl.BlockSpec((tc, SSD_W), lambda i: (i, 0)),
                   pl.BlockSpec((SSD_N, SSD_W), lambda i: (i, 0))],
        out_shape=[jax.ShapeDtypeStruct((L, SSD_W), bf16), jax.ShapeDtypeStruct((L, SSD_W), f32),
                   jax.ShapeDtypeStruct((nc * SSD_N, SSD_W), f32)],
        scratch_shapes=[pltpu.VMEM((SSD_N, SSD_W), f32)],
        compiler_params=_cp(("arbitrary",)))(proj, proj, proj, proj, cp, hp, ng)


def _ssd_bwd(proj, dycat, ypre, sin, cp, hp, ng, *, tc=256):
    L = proj.shape[0]
    nc = L // tc

    def body(xc_ref, xp_ref, z_ref, sm_ref, cp_ref, hp_ref, ng_ref, sin_ref, ypre_ref, dy_ref,
             dxbc_ref, dz_ref, ddt_ref, acc1_ref, acc2_ref, ds_scr, dnext_scr, dxbc_scr):
        i = pl.program_id(0)
        c_idx = nc - 1 - i

        @pl.when(i == 0)
        def _():
            ds_scr[...] = jnp.zeros_like(ds_scr)
            dnext_scr[...] = jnp.zeros_like(dnext_scr)
            acc1_ref[...] = jnp.zeros_like(acc1_ref)
            acc2_ref[...] = jnp.zeros_like(acc2_ref)
        x = xc_ref[...]
        xprev = xp_ref[...] * jnp.where(c_idx > 0, 1.0, 0.0)
        pre, sig, raw, dt, a_neg, cs, cst, tril = _ssd_pre(x, xprev, sm_ref[...], cp_ref, hp_ref, tc)
        xbc = pre * sig
        z = z_ref[...]
        sz = _sigmoid(z)
        silz = z * sz
        yall = ypre_ref[...]
        yz = yall * silz
        dy = dy_ref[...]
        dyz_parts = []
        for g in range(SSD_G):
            gs = slice(g * SSD_GW, (g + 1) * SSD_GW)
            yg, dyg = yz[:, gs], dy[:, gs]
            r = lax.rsqrt(jnp.mean(yg * yg, -1, keepdims=True) + RMS_EPS)
            acc1_ref[5:6, gs] += jnp.sum(dyg * yg * r, 0, keepdims=True)
            dyn = dyg * ng_ref[0:1, gs]
            dyz_parts.append(r * (dyn - yg * (r * r) * jnp.mean(dyn * yg, -1, keepdims=True)))
        dyz = jnp.concatenate(dyz_parts, axis=1)
        dz_ref[...] = (dyz * yall * (sz * (1.0 + z * (1.0 - sz)))).astype(bf16)
        dyall = dyz * silz

        lane1 = lax.broadcasted_iota(jnp.int32, (1, 128), 1)
        sub = lax.broadcasted_iota(jnp.int32, (128, tc), 0)
        rowc = lax.broadcasted_iota(jnp.int32, (tc, 1), 0)
        dcs = jnp.zeros((tc, 128), f32)
        dcsr = jnp.zeros((128, tc), f32)
        ddt = jnp.zeros((tc, 128), f32)
        dd_row = jnp.zeros((1, 128), f32)
        for g in range(SSD_G):
            b0 = SSD_W + g * SSD_N
            c0 = SSD_W + SSD_G * SSD_N + g * SSD_N
            bg = xbc[:, b0:b0 + SSD_N]
            bgb = bg.astype(bf16)
            cgb = xbc[:, c0:c0 + SSD_N].astype(bf16)
            cb = _dot_nt(cgb, bgb)
            dbg = jnp.zeros((tc, SSD_N), f32)
            dcg = jnp.zeros((tc, SSD_N), f32)
            for e in range(SSD_HPG):
                h = g * SSD_HPG + e
                hs = slice(h * SSD_HD, (h + 1) * SSD_HD)
                oh = jnp.where(lane1 == h, 1.0, 0.0)
                xs = xbc[:, hs]
                dth = dt[:, h:h + 1]
                csc = cs[:, h:h + 1]
                lm = jnp.where(tril, jnp.exp(jnp.minimum(csc - cst[h:h + 1, :], 0.0)), 0.0)
                m = cb * lm
                xdt = (xs * dth).astype(bf16)
                s_h = sin_ref[:, hs]
                s_hb = s_h.astype(bf16)
                dyh = dyall[:, hs]
                dyb = dyh.astype(bf16)
                dd_row = dd_row + oh * jnp.sum(dyh * xs)
                dxs = hp_ref[2:3, h:h + 1] * dyh
                ecs = jnp.exp(csc)
                cs_prod = _dot(cgb, s_hb)
                dcsb = (ecs * dyh).astype(bf16)
                dcg = dcg + _dot_nt(dcsb, s_hb)
                ds_in = _dot_tn(cgb, dcsb)
                dcs_h = jnp.sum(dyh * ecs * cs_prod, -1, keepdims=True)
                dm = _dot_nt(dyb, xdt)
                w = dm * m
                dcs_h = dcs_h + jnp.sum(w, -1, keepdims=True)
                dcsr = jnp.where(sub == h, jnp.sum(w, 0, keepdims=True), dcsr)
                dcbb = (dm * lm).astype(bf16)
                dcg = dcg + _dot(dcbb, bgb)
                dbg = dbg + _dot_tn(dcbb, cgb)
                dxdt = _dot_tn(m.astype(bf16), dyb)
                dsn = ds_scr[:, hs]
                dsnb = dsn.astype(bf16)
                cl = cs[tc - 1:tc, h:h + 1]
                dec = jnp.exp(cl - csc)
                dxdt = dxdt + _dot((bg * dec).astype(bf16), dsnb)
                dbd = _dot_nt(xdt, dsnb)
                dbg = dbg + dbd * dec
                gdec = jnp.sum(dbd * bg, -1, keepdims=True) * dec
                ecl = jnp.exp(cl)
                dcl = jnp.sum(gdec) + jnp.sum(dsn * s_h) * ecl
                ds_scr[:, hs] = ecl * dsn + ds_in
                dcs_h = dcs_h - gdec + jnp.where(rowc == tc - 1, dcl, 0.0)
                dcs = dcs + dcs_h * oh
                dxbc_scr[:, hs] = dxs + dxdt * dth
                ddt = ddt + jnp.sum(dxdt * xs, -1, keepdims=True) * oh
            dxbc_scr[:, b0:b0 + SSD_N] = dbg
            dxbc_scr[:, c0:c0 + SSD_N] = dcg
        dcs = dcs - dcsr.T
        r_i = lax.broadcasted_iota(jnp.int32, (tc, tc), 0)
        c_i = lax.broadcasted_iota(jnp.int32, (tc, tc), 1)
        da = _dot_hi(jnp.where(r_i <= c_i, 1.0, 0.0), dcs)
        ddt = ddt + da * a_neg
        acc2_ref[1:2, :] += jnp.sum(da * dt, 0, keepdims=True) * a_neg
        lane = lax.broadcasted_iota(jnp.int32, (tc, 128), 1)
        ddraw = jnp.where(lane < SSD_H, ddt * _sigmoid(raw), 0.0)
        acc2_ref[0:1, :] += jnp.sum(ddraw, 0, keepdims=True)
        acc2_ref[2:3, :] += dd_row
        ddt_ref[...] = ddraw
        dpre = dxbc_scr[...] * (sig * (1.0 + pre * (1.0 - sig)))
        acc1_ref[4:5, :] += jnp.sum(dpre, 0, keepdims=True)
        for k in range(SSD_K):
            acc1_ref[k:k + 1, :] += jnp.sum(dpre * _shift_down(x, xprev, SSD_K - 1 - k), 0, keepdims=True)
        dnext = dnext_scr[...]
        dxbc_ref[...] = (cp_ref[3:4, :] * dpre + cp_ref[2:3, :] * _shift_up(dpre, dnext, 1)
                         + cp_ref[1:2, :] * _shift_up(dpre, dnext, 2)
                         + cp_ref[0:1, :] * _shift_up(dpre, dnext, 3)).astype(bf16)
        dnext_scr[...] = dpre

    rev = lambda w_, col: pl.BlockSpec((tc, w_), lambda i: (nc - 1 - i, col // w_))
    return pl.pallas_call(
        body, name="ssd_bwd", grid=(nc,),
        in_specs=[rev(1024, 0), pl.BlockSpec((tc, 1024), lambda i: (jnp.maximum(nc - 2 - i, 0), 0)),
                  rev(SSD_W, P_Z), rev(128, P_SM),
                  _const_spec((8, 1024)), _const_spec((8, 128)), _const_spec((8, SSD_W)),
                  pl.BlockSpec((SSD_N, SSD_W), lambda i: (nc - 1 - i, 0)), rev(SSD_W, 0), rev(SSD_W, 0)],
        out_specs=[rev(1024, 0), rev(SSD_W, 0), rev(128, 0),
                   pl.BlockSpec((8, 1024), lambda i: (0, 0)), pl.BlockSpec((8, 128), lambda i: (0, 0))],
        out_shape=[jax.ShapeDtypeStruct((L, 1024), bf16), jax.ShapeDtypeStruct((L, SSD_W), bf16),
                   jax.ShapeDtypeStruct((L, 128), f32), jax.ShapeDtypeStruct((8, 1024), f32),
                   jax.ShapeDtypeStruct((8, 128), f32)],
        scratch_shapes=[pltpu.VMEM((SSD_N, SSD_W), f32), pltpu.VMEM((tc, 1024), f32), pltpu.VMEM((tc, 1024), f32)],
        compiler_params=_cp(("arbitrary",)))(proj, proj, proj, proj, cp, hp, ng, sin, ypre, dycat)


def _pack_cols(w):
    pad = jnp.zeros(w.shape[:-1] + (P_W - P_SM - SSD_H - FOX_H,), w.dtype)
    return jnp.concatenate([w[..., 512:1536], w[..., 0:512], w[..., 1544:2312], w[..., 2316:3084],
                            w[..., 1536:1544], w[..., 2312:2316], pad], axis=-1)


def _unpack_cols(g):
    return jnp.concatenate([g[..., 1024:1536], g[..., 0:1024], g[..., 3072:3080], g[..., 1536:2304],
                            g[..., 3080:3084], g[..., 2304:3072]], axis=-1)


def _rows8(*rows):
    width = max(r.shape[-1] for r in rows)
    out = [jnp.pad(r.astype(f32), (0, width - r.shape[-1])) for r in rows]
    out += [jnp.zeros((width,), f32)] * (8 - len(out))
    return jnp.stack(out)


def _local_step(x, tgt, p, weights_of, grads_done):
    gb_in = _rows8(p["ln_in_g"], p["ln_in_b"])
    cst = _fox_consts()
    rows_to_8 = lambda t: jnp.pad(t, [(0, 0)] * (t.ndim - 2) + [(0, 8 - t.shape[-2]), (0, 0)])
    lanes = lambda t, lo: jnp.pad(t, ((0, 0), (lo, 128 - lo - t.shape[-1])))[:, None, :]
    mvs = rows_to_8(jnp.concatenate([p["mod"].reshape(DEPTH, 3, 3, D), p["ln_g"][:, :, None], p["ln_b"][:, :, None]], 2))
    cps = rows_to_8(jnp.concatenate([p["ssd_conv_w"], p["ssd_conv_b"][:, None]], 1))
    hps = rows_to_8(jnp.concatenate([lanes(p["ssd_dt_bias"], 0), lanes(p["ssd_a_log"], 0), lanes(p["ssd_d"], 0),
                                     lanes(p["fox_f_bias"], SM_F)], 1))
    ngs = rows_to_8(p["ssd_norm_g"][:, None])
    scws = rows_to_8(p["sconv_w"])
    x0 = _ln_in_fwd(x, gb_in)
    saved = []
    for l in range(DEPTH):
        mv = [mvs[l, j] for j in range(3)]
        cp, hp, ng, scw = cps[l], hps[l], ngs[l], scws[l]
        w = dict(weights_of(l, "ffn1", x0))
        x1 = _ffn_fwd(x0, mv[0], w["ffn1_w_in"], w["ffn1_w_out"])
        w.update(weights_of(l, "mix", x1))
        proj = _inproj_fwd(x1, mv[1], w["mix_w_in"])
        y_ssd, ypre, sin = _ssd_fwd(proj, cp, hp, ng)
        qa, ka, va, tot = _fox_prep(proj, hp, cst)
        tot = tot[::8, SM_F:SM_F + FOX_H]
        o, o_acc, lse = _fox_attn_fwd(qa, ka, va, tot)
        y_sc = _sconv_fwd(proj, scw)
        ycat = jnp.concatenate([y_ssd, o, y_sc], axis=1)
        x2 = _outproj_fwd(x1, ycat, mv[1], w["mix_w_out"])
        w.update(weights_of(l, "ffn2", x2))
        x3 = _ffn_fwd(x2, mv[2], w["ffn2_w_in"], w["ffn2_w_out"])
        saved.append((x0, x1, x2, mv, cp, hp, ng, scw, proj, ypre, sin, qa, ka, va, tot, o_acc, lse, ycat, w))
        x0 = x3
    dx, loss_acc = _loss_head(x0, tgt)

    g = {k: [None] * DEPTH for k in (
        "mod", "ln_g", "ln_b", "ssd_conv_w", "ssd_conv_b", "ssd_dt_bias", "ssd_a_log", "ssd_d", "ssd_norm_g",
        "fox_f_bias", "sconv_w")}
    def behind(small, tok):
        return small if tok is None else small + tok[0, 0]

    token = None
    for l in reversed(range(DEPTH)):
        x0, x1, x2, mv, cp, hp, ng, scw, proj, ypre, sin, qa, ka, va, tot, o_acc, lse, ycat, w = saved[l]
        dx, h, dat, actt, dyb, a2 = _ffn_bwd(x2, dx, behind(mv[2], token), w["ffn2_w_in"], w["ffn2_w_out"])
        token = grads_done(l, "ffn2_w_out", {"ffn2_w_out": _matmul_tokens(actt, dyb, tn=D, tk=DW_TK, name="dw_ffn_out")})
        token = grads_done(l, "ffn2_w_in", {"ffn2_w_in": _matmul_tokens(dat, h, tn=D, tk=DW_TK, name="dw_ffn_in",
                                                                        after=token)})
        mv1 = behind(mv[1], token)
        dxp, dyb, dycat, ycat_t, a1 = _outproj_bwd(x1, ycat, dx, mv1, w["mix_w_out"])
        gw_mix_out = _matmul_tokens(ycat_t, dyb, tn=D // 2, tk=DW_TK, name="dw_mix_out")
        dxbc, dz, ddt, acc1, acc2 = _ssd_bwd(proj, dycat, ypre, sin, cp, hp, ng)
        qb, doa, qbt, doat = _fox_bprep(qa, dycat, o_acc, lse, cst)
        dq, dkt, dvt = _fox_attn_bwd(ka, va, tot, qb, doa, qbt, doat)
        dq, dk, dv, dsm, accf = _fox_post(dq, dkt, dvt, proj, ddt, hp, cst)
        dsb, dsc, dsx, accs = _sconv_bwd(proj, dycat, scw)
        dproj = jnp.concatenate([dxbc, dz, dq, dk, dv, dsb, dsc, dsx, dsm], axis=1)
        dx, ht, a1b = _inproj_bwd(x1, dxp, dproj, mv1, w["mix_w_in"])
        token = grads_done(l, "mix", {"mix_w_in": _matmul_tokens(ht, dproj, tn=P_W // 5, tk=DW_TK, name="dw_mix_in"),
                                      "mix_w_out": gw_mix_out})
        dx, h, dat, actt, dyb, a0 = _ffn_bwd(x0, dx, behind(mv[0], token), w["ffn1_w_in"], w["ffn1_w_out"])
        token = grads_done(l, "ffn1_w_out", {"ffn1_w_out": _matmul_tokens(actt, dyb, tn=D, tk=DW_TK, name="dw_ffn_out")})
        token = grads_done(l, "ffn1_w_in", {"ffn1_w_in": _matmul_tokens(dat, h, tn=D, tk=DW_TK, name="dw_ffn_in",
                                                                        after=token)})
        g["mod"][l] = jnp.concatenate([a0[0:3], a1b[0:2], a1[2:3], a2[0:3]], axis=0)
        g["ln_g"][l] = jnp.stack([a0[3], a1[3], a2[3]])
        g["ln_b"][l] = jnp.stack([a0[4], a1[4], a2[4]])
        g["ssd_conv_w"][l] = acc1[0:SSD_K]
        g["ssd_conv_b"][l] = acc1[4]
        g["ssd_norm_g"][l] = acc1[5, :SSD_W]
        g["ssd_dt_bias"][l] = acc2[0, :SSD_H]
        g["ssd_a_log"][l] = acc2[1, :SSD_H]
        g["ssd_d"][l] = acc2[2, :SSD_H]
        g["fox_f_bias"][l] = accf[3, SM_F:SM_F + FOX_H]
        g["sconv_w"][l] = accs[0:SC_K]
    grad_x, a_in = _ln_in_bwd(x, dx, behind(gb_in, token))
    g = {k: jnp.stack(v) for k, v in g.items()}
    g["ln_in_g"], g["ln_in_b"] = a_in[0], a_in[1]
    return loss_acc[0, 0], grad_x, g


MESH = pl.DeviceIdType.MESH
ANY = pl.BlockSpec(memory_space=pl.ANY)


def _all_gather(shards, *, in_vmem, name):
    n_arr = len(shards)

    def body(*refs):
        x_refs, out_refs = refs[:n_arr], refs[n_arr:2 * n_arr]
        send_sems, recv_sems, local_sems = refs[2 * n_arr:]
        x, y, c = lax.axis_index("x"), lax.axis_index("y"), lax.axis_index("c")
        me, sibling = (x, y, c), (x, y, 1 - c)
        chips = [(1 - x, y), (x, 1 - y), (1 - x, 1 - y)]

        def copy(a, k, block, to, src=None):
            px, py, pc = block
            slot = out_refs[a].at[4 * px + 2 * py + pc]
            return pltpu.make_async_remote_copy(
                src_ref=slot if src is None else src, dst_ref=slot,
                send_sem=send_sems.at[7 * a + k], recv_sem=recv_sems.at[7 * a + k], device_id=to, device_id_type=MESH)

        mine, first, passed = [], [], []
        for a in range(n_arr):
            mine.append(pltpu.make_async_copy(x_refs[a], out_refs[a].at[4 * x + 2 * y + c], local_sems.at[a]))
            mine[-1].start()
            first.append(copy(a, 0, me, sibling, src=x_refs[a]))
            first += [copy(a, 1 + j, me, (*chip, c), src=x_refs[a]) for j, chip in enumerate(chips)]
        for cp in first:
            cp.start()
        for j, chip in enumerate(chips):
            for a in range(n_arr):
                copy(a, 1 + j, (*chip, c), me).wait_recv()
                passed.append(copy(a, 4 + j, (*chip, c), sibling))
                passed[-1].start()
        for a in range(n_arr):
            copy(a, 0, sibling, me).wait_recv()
            for j, chip in enumerate(chips):
                copy(a, 4 + j, (*chip, 1 - c), me).wait_recv()
        for cp in first + passed:
            cp.wait_send()
        for cp in mine:
            cp.wait()

    spec = pl.BlockSpec(memory_space=pltpu.VMEM) if in_vmem else ANY
    return pl.pallas_call(
        body, name=name, out_shape=[jax.ShapeDtypeStruct((N_DEV,) + s.shape, s.dtype) for s in shards],
        in_specs=[spec] * n_arr, out_specs=[spec] * n_arr,
        scratch_shapes=[pltpu.SemaphoreType.DMA((7 * n_arr,)), pltpu.SemaphoreType.DMA((7 * n_arr,)),
                        pltpu.SemaphoreType.DMA((n_arr,))],
    )(*shards)


HBM = pl.BlockSpec(memory_space=pltpu.HBM)
SEM = pl.BlockSpec(memory_space=pltpu.SEMAPHORE)
EFFECT = pltpu.SideEffectType.DATAFLOW_SIDE_EFFECTING


def _spread_copies(srcs, lands, send_sems, recv_sems, local_sems, scatter):
    x, y, c = lax.axis_index("x"), lax.axis_index("y"), lax.axis_index("c")
    me = 4 * x + 2 * y + c
    local, remote = [], []
    for a in range(len(srcs)):
        own = srcs[a].at[me] if scatter else srcs[a]
        local.append(pltpu.make_async_copy(own, lands[a].at[me], local_sems.at[a]))
        for r in range(1, N_DEV):
            px, py, pc = (1 - x if r & 4 else x), (1 - y if r & 2 else y), (1 - c if r & 1 else c)
            peer = 4 * px + 2 * py + pc
            k = (N_DEV - 1) * a + r - 1
            mk = functools.partial(pltpu.make_async_remote_copy, send_sem=send_sems.at[k], recv_sem=recv_sems.at[k],
                                   device_id=(px, py, pc), device_id_type=MESH)
            remote.append((mk(src_ref=srcs[a].at[peer] if scatter else srcs[a], dst_ref=lands[a].at[me]),
                           mk(src_ref=own, dst_ref=lands[a].at[peer])))
    return local, remote


def _spread_start(srcs, *, scatter, name, after=()):
    n, k = len(srcs), len(after)
    lands = [jax.ShapeDtypeStruct((N_DEV,) + s.shape[-2:], s.dtype) for s in srcs]

    def body(*refs):
        src, land = refs[:n], refs[n:2 * n]
        send_sems, recv_sems, local_sems = refs[2 * n + k:2 * n + k + 3]
        token = refs[-1]
        local, remote = _spread_copies(src, land, send_sems, recv_sems, local_sems, scatter)
        for cp in local:
            cp.start()
        for cp, _ in remote:
            cp.start()
        token[...] = jnp.zeros_like(token)

    nsem = (N_DEV - 1) * n
    out = pl.pallas_call(
        body, name=name,
        out_shape=(pltpu.SemaphoreType.DMA((nsem,)), pltpu.SemaphoreType.DMA((nsem,)), pltpu.SemaphoreType.DMA((n,)),
                   *[pltpu.HBM(s.shape, s.dtype) for s in srcs], *[pltpu.HBM(s.shape, s.dtype) for s in lands],
                   jax.ShapeDtypeStruct((8, 128), f32)),
        in_specs=[HBM] * (2 * n) + [ANY] * k,
        out_specs=(SEM, SEM, SEM, *[HBM] * (2 * n), pl.BlockSpec(memory_space=pltpu.VMEM)),
        input_output_aliases={i: 3 + i for i in range(2 * n)},
        compiler_params=pltpu.CompilerParams(has_side_effects=EFFECT),
    )(*[pltpu.with_memory_space_constraint(s, pltpu.HBM) for s in srcs],
      *[pltpu.with_memory_space_constraint(lax.empty(s.shape, s.dtype), pltpu.HBM) for s in lands], *after)
    return out[:-1], out[-1]


def _spread_wait(state, after, *, scatter, name):
    n = (len(state) - 3) // 2
    sems, thru = state[:3], state[3:]

    def body(*refs):
        src, land = refs[:n], refs[n:2 * n]
        send_sems, recv_sems, local_sems = refs[2 * n:2 * n + 3]
        local, remote = _spread_copies(src, land, send_sems, recv_sems, local_sems, scatter)
        for sent, received in remote:
            sent.wait_send()
            received.wait_recv()
        for cp in local:
            cp.wait()

    out = pl.pallas_call(
        body, name=name,
        out_shape=tuple(pltpu.HBM(t.shape, t.dtype) for t in thru),
        in_specs=[HBM] * (2 * n) + [SEM] * 3 + [ANY], out_specs=tuple([HBM] * (2 * n)),
        input_output_aliases={i: i for i in range(2 * n)},
        compiler_params=pltpu.CompilerParams(has_side_effects=EFFECT),
    )(*thru, *sems, after)
    return list(out[n:])


def _sum_slots(buf, *, tr, name):
    nb, r, n = buf.shape

    def body(b_ref, o_ref):
        acc = b_ref[0].astype(f32)
        for k in range(1, nb):
            acc = acc + b_ref[k].astype(f32)
        o_ref[...] = acc

    return pl.pallas_call(
        body, name=name, grid=(r // tr,),
        in_specs=[pl.BlockSpec((nb, tr, n), lambda i: (0, i, 0))],
        out_specs=pl.BlockSpec((tr, n), lambda i: (i, 0)),
        out_shape=jax.ShapeDtypeStruct((r, n), f32), compiler_params=_cp(("parallel",)))(buf)


def _ada_fwd(c_all, ada_w, ada_b_cols):
    n = ada_w.shape[-1]

    def body(c_ref, w_ref, b_ref, o_ref):
        cv = c_ref[...]
        ca = (cv * _sigmoid(cv)).astype(bf16)
        o_ref[...] = _dot(ca, w_ref[...].astype(bf16)) + b_ref[...]

    return pl.pallas_call(
        body, name="ada_fwd", grid=(DEPTH,),
        in_specs=[_const_spec((N_DEV, D)), pl.BlockSpec((None, D, n), lambda l: (l, 0, 0)),
                  pl.BlockSpec((None, 1, n), lambda l: (l, 0, 0))],
        out_specs=pl.BlockSpec((None, N_DEV, n), lambda l: (l, 0, 0)),
        out_shape=jax.ShapeDtypeStruct((DEPTH, N_DEV, n), f32), compiler_params=_cp(("parallel",)))(c_all, ada_w, ada_b_cols)


def _ada_bwd(c_all, dmod_cols):
    n = dmod_cols.shape[-1]

    def body(c_ref, d_ref, o_ref):
        cv = c_ref[...]
        ca = (cv * _sigmoid(cv)).astype(bf16)
        o_ref[...] = _dot_tn(ca, d_ref[...].astype(bf16))

    return pl.pallas_call(
        body, name="ada_bwd", grid=(DEPTH,),
        in_specs=[_const_spec((N_DEV, D)), pl.BlockSpec((None, N_DEV, n), lambda l: (l, 0, 0))],
        out_specs=pl.BlockSpec((None, D, n), lambda l: (l, 0, 0)),
        out_shape=jax.ShapeDtypeStruct((DEPTH, D, n), f32), compiler_params=_cp(("parallel",)))(c_all, dmod_cols)


def _adamw(w, g, m, v, *, tr, name):
    r, n = w.shape

    def body(w_ref, g_ref, m_ref, v_ref, d_ref, mo_ref, vo_ref):
        g_ = g_ref[...]
        m_ = ADAM_B1 * m_ref[...] + (1.0 - ADAM_B1) * g_
        v_ = ADAM_B2 * v_ref[...] + (1.0 - ADAM_B2) * jnp.square(g_)
        m_hat = m_ / (1.0 - ADAM_B1 ** ADAM_STEP)
        v_hat = v_ / (1.0 - ADAM_B2 ** ADAM_STEP)
        d_ref[...] = -ADAM_LR * (m_hat / (jnp.sqrt(v_hat) + ADAM_EPS) + ADAM_WD * w_ref[...])
        mo_ref[...] = m_
        vo_ref[...] = v_

    blk = pl.BlockSpec((tr, n), lambda i: (i, 0))
    return pl.pallas_call(
        body, name=name, grid=(r // tr,), in_specs=[blk] * 4, out_specs=[blk] * 3,
        out_shape=[jax.ShapeDtypeStruct((r, n), f32)] * 3, compiler_params=_cp(("parallel",)))(w, g, m, v)


def _adamw_landed(w, m, v, land0, land1, *, tr, name):
    _, r, n = w.shape
    nrt = r // tr

    def body(w_ref, m_ref, v_ref, l0_ref, l1_ref, g_ref, d_ref, mo_ref, vo_ref):
        layer = pl.program_id(0)

        def total(ref):
            acc = ref[0].astype(f32)
            for k in range(1, N_DEV):
                acc = acc + ref[k].astype(f32)
            return acc

        @pl.when(layer == 0)
        def _():
            g_ref[...] = total(l0_ref)

        @pl.when(layer == 1)
        def _():
            g_ref[...] = total(l1_ref)
        g_ = g_ref[...]
        m_ = ADAM_B1 * m_ref[...] + (1.0 - ADAM_B1) * g_
        v_ = ADAM_B2 * v_ref[...] + (1.0 - ADAM_B2) * jnp.square(g_)
        m_hat = m_ / (1.0 - ADAM_B1 ** ADAM_STEP)
        v_hat = v_ / (1.0 - ADAM_B2 ** ADAM_STEP)
        d_ref[...] = -ADAM_LR * (m_hat / (jnp.sqrt(v_hat) + ADAM_EPS) + ADAM_WD * w_ref[...])
        mo_ref[...] = m_
        vo_ref[...] = v_

    blk = pl.BlockSpec((None, tr, n), lambda l, i: (l, i, 0))
    land0_spec = pl.BlockSpec((N_DEV, tr, n), lambda l, i: (0, jnp.where(l == 0, i, nrt - 1), 0))
    land1_spec = pl.BlockSpec((N_DEV, tr, n), lambda l, i: (0, jnp.where(l == 1, i, 0), 0))
    return pl.pallas_call(
        body, name=name, grid=(DEPTH, nrt), in_specs=[blk] * 3 + [land0_spec, land1_spec], out_specs=[blk] * 4,
        out_shape=[jax.ShapeDtypeStruct(w.shape, f32)] * 4,
        compiler_params=_cp(("arbitrary", "arbitrary")))(w, m, v, land0, land1)


WEIGHTS = ["ln_in_g", "ln_in_b", "ada_w", "ada_b", "ffn1_w_in", "ffn1_w_out", "mix_w_in", "mix_w_out", "ssd_conv_w",
           "ssd_conv_b", "ssd_dt_bias", "ssd_a_log", "ssd_d", "ssd_norm_g", "fox_f_bias", "sconv_w", "ffn2_w_in",
           "ffn2_w_out", "ln_g", "ln_b"]
BIG = ["ffn1_w_in", "ffn1_w_out", "ffn2_w_in", "ffn2_w_out", "mix_w_in", "mix_w_out"]
GROUPS = {"ffn1": ["ffn1_w_in", "ffn1_w_out"], "mix": ["mix_w_in", "mix_w_out"], "ffn2": ["ffn2_w_in", "ffn2_w_out"]}
COL_SHARDED = ("ffn1_w_in", "ffn2_w_in")
SMALL_SHARDED = {"ssd_conv_w": 128, "sconv_w": 32, "ln_g": 128, "ln_b": 128}
ADAM_TR = {"ada_w": 256, "ffn1_w_in": 512, "ffn2_w_in": 512, "ffn1_w_out": 352, "ffn2_w_out": 352, "mix_w_in": 64,
           "mix_w_out": 256}
LAND_TR = {"ffn1_w_in": 176, "ffn2_w_in": 176, "ffn1_w_out": 176, "ffn2_w_out": 176, "mix_w_in": 32, "mix_w_out": 128}


def _pad_rows(v, mult=128):
    v = v.reshape(-1)
    return jnp.pad(v, (0, (-v.shape[0]) % mult))


def _pack_rows(parts, row_mult=8):
    flat = [_pad_rows(p.astype(f32)) for p in parts]
    offs, o = [], 0
    for f in flat:
        offs.append(o)
        o += f.shape[0] // 128
    buf = jnp.concatenate(flat).reshape(-1, 128)
    return jnp.pad(buf, ((0, (-buf.shape[0]) % row_mult), (0, 0))), offs


def _take(buf, off, shape):
    n = 1
    for s in shape:
        n *= s
    rows = -(-n // 128)
    lead = buf.shape[:-2]
    flat = buf[..., off:off + rows, :].reshape(lead + (rows * 128,))
    return flat[..., :n].reshape(lead + tuple(shape))


def kernel(*args):
    names = (["x", "c"] + WEIGHTS + ["loss_target"] + ["m_" + n for n in WEIGHTS] + ["v_" + n for n in WEIGHTS])
    assert len(args) == len(names)
    a = dict(zip(names, args))
    xi, yi, ci = lax.axis_index("x"), lax.axis_index("y"), lax.axis_index("c")
    me = 4 * xi + 2 * yi + ci

    small_in = [a["c"], a["ln_g"], a["ln_b"], a["ssd_conv_w"], a["sconv_w"]]
    buf, offs = _pack_rows(small_in)
    got, = _all_gather([buf], in_vmem=True, name="gather_small")
    c_all = _take(got, offs[0], (D,))
    full = {}
    for k, n in enumerate(["ln_g", "ln_b", "ssd_conv_w", "sconv_w"]):
        sh = a[n].shape
        t = _take(got, offs[k + 1], sh)
        full[n] = jnp.transpose(t, (1, 2, 0, 3)).reshape(sh[0], sh[1], N_DEV * sh[2])

    ncol = a["ada_w"].shape[-1]
    ada_b_cols = lax.dynamic_slice_in_dim(a["ada_b"], me * ncol, ncol, axis=1)[:, None, :]
    mod_cols = _ada_fwd(c_all, a["ada_w"], ada_b_cols)
    got, = _all_gather([mod_cols.reshape(DEPTH * N_DEV, ncol)], in_vmem=True, name="gather_mod")
    got = got.reshape(N_DEV, DEPTH, N_DEV, ncol)
    mod = lax.dynamic_index_in_dim(got, me, axis=2, keepdims=False)
    mod = jnp.transpose(mod, (1, 0, 2)).reshape(DEPTH, 9, D)

    flip = lambda t: jnp.swapaxes(t, 1, 2)
    shards = {n: (flip(a[n]) if n in COL_SHARDED else a[n]) for n in BIG}
    shards["mix_w_in"] = _pack_cols(a["mix_w_in"])
    def as_weights(names, got):
        return {n: (t if n in COL_SHARDED else t.reshape(-1, t.shape[-1])) for n, t in zip(names, got)}

    def group_shards(l, gn):
        return [shards[n][l].astype(bf16) for n in GROUPS[gn]]

    w_first = _all_gather(group_shards(0, "ffn1"), in_vmem=False, name="gather_weights")
    w_state, after = {}, [w_first[0]]
    for l, gn in [(0, "mix"), (0, "ffn2"), (1, "ffn1"), (1, "mix"), (1, "ffn2")]:
        w_state[l, gn], token = _spread_start(group_shards(l, gn), scatter=False, after=after,
                                              name=f"weights_{l}{gn}_start")
        after = [token]

    def weights_of(l, gn, after):
        if (l, gn) == (0, "ffn1"):
            return as_weights(GROUPS[gn], w_first)
        return as_weights(GROUPS[gn], _spread_wait(w_state[l, gn], after, scatter=False, name=f"weights_{l}{gn}_wait"))

    g_state = {}

    last_send = {}

    def grads_done(l, gn, gw, after=()):
        if (l, gn) == (0, "ffn1_w_in") and not after:
            last_send.update(gw)
            return None
        srcs = [t.reshape((N_DEV,) + shards[n].shape[1:]) for n, t in gw.items()]
        g_state[l, gn], token = _spread_start(srcs, scatter=True, after=after, name=f"grads_{l}{gn}_start")
        return token

    p = {n: a[n] for n in ("ln_in_g", "ln_in_b", "ssd_conv_b", "ssd_dt_bias", "ssd_a_log", "ssd_d", "ssd_norm_g",
                           "fox_f_bias")}
    p.update(full)
    p["mod"] = mod + after[0][0, 0]

    loss_local, grad_x, g = _local_step(a["x"][0], a["loss_target"][0], p, weights_of, grads_done)
    loss = lax.psum(loss_local, ("x", "y", "c"))

    small_names = ["mod", "ln_in_g", "ln_in_b", "ssd_conv_b", "ssd_dt_bias", "ssd_a_log", "ssd_d", "ssd_norm_g",
                   "fox_f_bias", "ln_g", "ln_b", "ssd_conv_w", "sconv_w"]
    buf, offs = _pack_rows([g[n] for n in small_names])
    got, = _all_gather([buf], in_vmem=True, name="gather_small_grads")
    tot = _sum_slots(got, tr=buf.shape[0], name="sum_small_grads")
    grads_done(0, "ffn1_w_in", last_send, after=(tot,))
    grads = {}
    for k, n in enumerate(small_names[1:], start=1):
        t = _take(tot, offs[k], g[n].shape)
        if n in SMALL_SHARDED:
            w_ = SMALL_SHARDED[n]
            t = lax.dynamic_slice_in_dim(t, me * w_, w_, axis=2)
        grads[n] = t
    grads["ada_b"] = _take(tot, offs[0], (DEPTH, 9 * D))
    dmod_all = _take(got, offs[0], (DEPTH, 9 * D))
    dmod_cols = jnp.transpose(lax.dynamic_slice_in_dim(dmod_all, me * ncol, ncol, axis=2), (1, 0, 2))
    grads["ada_w"] = _ada_bwd(c_all, dmod_cols)

    delta, new_m, new_v = {}, {}, {}

    def adamw(n):
        sh = a[n].shape
        two = lambda t: t.reshape(-1, sh[-1])
        outs = _adamw(two(a[n]), two(grads[n]), two(a["m_" + n]), two(a["v_" + n]), tr=ADAM_TR[n], name="adamw_" + n)
        delta[n], new_m[n], new_v[n] = (t.reshape(sh) for t in outs)

    small_params = [n for n in WEIGHTS if n not in ADAM_TR]
    packs = [_pack_rows([src[pre + n] for n in small_params])[0]
             for src, pre in ((a, ""), (grads, ""), (a, "m_"), (a, "v_"))]
    _, offs = _pack_rows([a[n] for n in small_params])
    outs = _adamw(*packs, tr=packs[0].shape[0], name="adamw_small")
    for k, n in enumerate(small_params):
        delta[n], new_m[n], new_v[n] = (_take(t, offs[k], a[n].shape) for t in outs)
    adamw("ada_w")
    after = grad_x
    for gn, names in [("ffn2_w_out", ["ffn2_w_out"]), ("ffn2_w_in", ["ffn2_w_in"]), ("mix", GROUPS["mix"]),
                      ("ffn1_w_out", ["ffn1_w_out"]), ("ffn1_w_in", ["ffn1_w_in"])]:
        lands = [_spread_wait(g_state[l, gn], after if l == 0 else grad_x, scatter=True, name=f"grads_{l}{gn}_wait")
                 for l in range(DEPTH)]
        for k, n in enumerate(names):
            if n == "mix_w_in":
                sums = [_sum_slots(t[k], tr=LAND_TR[n], name="sum_devices") for t in lands]
                grads[n] = _unpack_cols(jnp.stack(sums))
                adamw(n)
            elif n in COL_SHARDED:
                grads[n], delta[n], new_m[n], new_v[n] = (flip(t) for t in _adamw_landed(
                    flip(a[n]), flip(a["m_" + n]), flip(a["v_" + n]), lands[0][k], lands[1][k], tr=LAND_TR[n],
                    name="adamw_" + n))
            else:
                grads[n], delta[n], new_m[n], new_v[n] = _adamw_landed(
                    a[n], a["m_" + n], a["v_" + n], lands[0][k], lands[1][k], tr=LAND_TR[n], name="adamw_" + n)
            after = delta[n]

    return (loss, grad_x[None], *[grads[n] for n in WEIGHTS], *[delta[n] for n in WEIGHTS],
            *[new_m[n] for n in WEIGHTS], *[new_v[n] for n in WEIGHTS])
```

```python
import functools

import jax
import jax.numpy as jnp
from jax import lax
from jax.experimental import pallas as pl
from jax.experimental.pallas import tpu as pltpu

f32, bf16 = jnp.float32, jnp.bfloat16

D = 1024
F = 2816
DEPTH = 2
N_DEV = 8
SSD_W, SSD_HD, SSD_H, SSD_G, SSD_N, SSD_K = 512, 64, 8, 2, 128, 4
FOX_W, FOX_HD, FOX_H = 256, 64, 4
SC_W, SC_K = 256, 3
ALPHA = (2 * DEPTH) ** 0.25
LN_EPS = 1e-5
RMS_EPS = 1e-5
P_XBC, P_Z, P_Q, P_K, P_V, P_SB, P_SC, P_SX, P_SM = 0, 1024, 1536, 1792, 2048, 2304, 2560, 2816, 3072
P_W = 3200
SM_DT, SM_F = 0, 8
ADAM_LR, ADAM_B1, ADAM_B2, ADAM_EPS, ADAM_WD, ADAM_STEP = 0.001, 0.9, 0.999, 1e-08, 0.01, 10

VMEM_LIMIT = 56 * 1024 * 1024


def _cp(sem=None):
    return pltpu.CompilerParams(dimension_semantics=sem, vmem_limit_bytes=VMEM_LIMIT)


def _const_spec(shape):
    nd = len(shape)
    return pl.BlockSpec(shape, lambda *_: (0,) * nd, pipeline_mode=pl.Buffered(1))


def _sigmoid(x):
    return 1.0 / (1.0 + jnp.exp(-x))


def _ln_fwd(u, g, b):
    mu = jnp.mean(u, -1, keepdims=True)
    xc = u - mu
    rstd = lax.rsqrt(jnp.mean(xc * xc, -1, keepdims=True) + LN_EPS)
    xhat = xc * rstd
    return xhat * g + b, xhat, rstd


def _ln_bwd(dout, xhat, rstd, g):
    dxh = dout * g
    m1 = jnp.mean(dxh, -1, keepdims=True)
    m2 = jnp.mean(dxh * xhat, -1, keepdims=True)
    du = rstd * (dxh - m1 - xhat * m2)
    return du, jnp.sum(dout * xhat, 0, keepdims=True), jnp.sum(dout, 0, keepdims=True)


def _dot(a, b):
    return jnp.dot(a, b, preferred_element_type=f32)


def _dot_nt(a, b):
    return lax.dot_general(a, b, (((1,), (1,)), ((), ())), preferred_element_type=f32)


def _dot_tn(a, b):
    return lax.dot_general(a, b, (((0,), (0,)), ((), ())), preferred_element_type=f32)


def _dot_hi(a, b):
    return jnp.dot(a, b, preferred_element_type=f32, precision=lax.Precision.HIGHEST)


def _shift_down(cur, prev, s):
    if s == 0:
        return cur
    row = lax.broadcasted_iota(jnp.int32, cur.shape, 0)
    return jnp.where(row < s, pltpu.roll(prev, s, 0), pltpu.roll(cur, s, 0))


def _shift_up(cur, nxt, s):
    if s == 0:
        return cur
    t = cur.shape[0]
    row = lax.broadcasted_iota(jnp.int32, cur.shape, 0)
    return jnp.where(row >= t - s, pltpu.roll(nxt, t - s, 0), pltpu.roll(cur, t - s, 0))


def _ln_in_fwd(x, gb, *, tt=512):
    L = x.shape[0]

    def body(x_ref, gb_ref, o_ref):
        o_ref[...] = _ln_fwd(x_ref[...], gb_ref[0:1, :], gb_ref[1:2, :])[0]

    return pl.pallas_call(
        body, name="ln_in_fwd", grid=(L // tt,),
        in_specs=[pl.BlockSpec((tt, D), lambda i: (i, 0)), _const_spec((8, D))],
        out_specs=pl.BlockSpec((tt, D), lambda i: (i, 0)),
        out_shape=jax.ShapeDtypeStruct((L, D), f32), compiler_params=_cp(("parallel",)))(x, gb)


def _ln_in_bwd(x, dy, gb, *, tt=512):
    L = x.shape[0]

    def body(x_ref, dy_ref, gb_ref, dx_ref, acc_ref):
        @pl.when(pl.program_id(0) == 0)
        def _():
            acc_ref[...] = jnp.zeros_like(acc_ref)
        _, xhat, rstd = _ln_fwd(x_ref[...], gb_ref[0:1, :], gb_ref[1:2, :])
        du, dg, db = _ln_bwd(dy_ref[...], xhat, rstd, gb_ref[0:1, :])
        dx_ref[...] = du
        acc_ref[0:1, :] += dg
        acc_ref[1:2, :] += db

    return pl.pallas_call(
        body, name="ln_in_bwd", grid=(L // tt,),
        in_specs=[pl.BlockSpec((tt, D), lambda i: (i, 0)), pl.BlockSpec((tt, D), lambda i: (i, 0)), _const_spec((8, D))],
        out_specs=[pl.BlockSpec((tt, D), lambda i: (i, 0)), pl.BlockSpec((8, D), lambda i: (0, 0))],
        out_shape=[jax.ShapeDtypeStruct((L, D), f32), jax.ShapeDtypeStruct((8, D), f32)],
        compiler_params=_cp(("arbitrary",)))(x, dy, gb)


def _loss_head(y, tgt, *, tt=512):
    L = y.shape[0]

    def body(y_ref, t_ref, dy_ref, acc_ref):
        @pl.when(pl.program_id(0) == 0)
        def _():
            acc_ref[...] = jnp.zeros_like(acc_ref)
        e = y_ref[...] - t_ref[...]
        dy_ref[...] = e * (1.0 / D)
        acc_ref[...] += 0.5 * jnp.sum(jnp.mean(e * e, -1, keepdims=True))

    return pl.pallas_call(
        body, name="loss_head", grid=(L // tt,),
        in_specs=[pl.BlockSpec((tt, D), lambda i: (i, 0)), pl.BlockSpec((tt, D), lambda i: (i, 0))],
        out_specs=[pl.BlockSpec((tt, D), lambda i: (i, 0)), pl.BlockSpec((8, 128), lambda i: (0, 0))],
        out_shape=[jax.ShapeDtypeStruct((L, D), f32), jax.ShapeDtypeStruct((8, 128), f32)],
        compiler_params=_cp(("arbitrary",)))(y, tgt)


FFN_CH = 4
FS = F // FFN_CH


def _ffn_fwd(x, mv, w_in, w_out, *, tt=256):
    L = x.shape[0]

    def body(x_ref, mv_ref, wi_ref, wo_ref, o_ref):
        x = x_ref[...]
        h = (x * (1.0 + mv_ref[1:2, :]) + mv_ref[0:1, :]).astype(bf16)
        y = jnp.zeros((tt, D), f32)
        for c in range(FFN_CH):
            g = _dot_nt(h, wi_ref[c])
            u = _dot_nt(h, wi_ref[c + FFN_CH])
            act = (g * _sigmoid(g) * u).astype(bf16)
            y = y + _dot(act, wo_ref[c * FS:(c + 1) * FS, :])
        uu = ALPHA * x + (0.5 * mv_ref[2:3, :]) * y
        o_ref[...] = _ln_fwd(uu, mv_ref[3:4, :], mv_ref[4:5, :])[0]

    return pl.pallas_call(
        body, name="ffn_fwd", grid=(L // tt,),
        in_specs=[pl.BlockSpec((tt, D), lambda i: (i, 0)), _const_spec((8, D)),
                  _const_spec((2 * FFN_CH, FS, D)), _const_spec((F, D))],
        out_specs=pl.BlockSpec((tt, D), lambda i: (i, 0)),
        out_shape=jax.ShapeDtypeStruct((L, D), f32), compiler_params=_cp(("parallel",)))(x, mv, w_in, w_out)


def _ffn_bwd(x, dxo, mv, w_in, w_out, *, tt=256):
    L = x.shape[0]

    def body(x_ref, dxo_ref, mv_ref, wi_ref, wo_ref, dx_ref, h_ref, da_ref, act_ref, dy_ref, acc_ref, a_scr):
        @pl.when(pl.program_id(0) == 0)
        def _():
            acc_ref[...] = jnp.zeros_like(acc_ref)
        x = x_ref[...]
        scale1 = 1.0 + mv_ref[1:2, :]
        h = (x * scale1 + mv_ref[0:1, :]).astype(bf16)
        h_ref[...] = h
        y = jnp.zeros((tt, D), f32)
        for c in range(FFN_CH):
            g = _dot_nt(h, wi_ref[c])
            u = _dot_nt(h, wi_ref[c + FFN_CH])
            a_scr[c] = g
            a_scr[c + FFN_CH] = u
            act = (g * _sigmoid(g) * u).astype(bf16)
            act_ref[c] = act.T
            y = y + _dot(act, wo_ref[c * FS:(c + 1) * FS, :])
        hg = 0.5 * mv_ref[2:3, :]
        _, xhat, rstd = _ln_fwd(ALPHA * x + hg * y, mv_ref[3:4, :], mv_ref[4:5, :])
        du, dlg, dlb = _ln_bwd(dxo_ref[...], xhat, rstd, mv_ref[3:4, :])
        acc_ref[3:4, :] += dlg
        acc_ref[4:5, :] += dlb
        acc_ref[2:3, :] += jnp.sum(0.5 * y * du, 0, keepdims=True)
        dyb = (hg * du).astype(bf16)
        dy_ref[...] = dyb
        dh = jnp.zeros((tt, D), f32)
        for c in range(FFN_CH):
            g = a_scr[c]
            u = a_scr[c + FFN_CH]
            dact = _dot_nt(dyb, wo_ref[c * FS:(c + 1) * FS, :])
            s = _sigmoid(g)
            dg = (dact * u * (s * (1.0 + g * (1.0 - s)))).astype(bf16)
            dup = (dact * (g * s)).astype(bf16)
            da_ref[c] = dg.T
            da_ref[c + FFN_CH] = dup.T
            dh = dh + _dot(dg, wi_ref[c])
            dh = dh + _dot(dup, wi_ref[c + FFN_CH])
        dx_ref[...] = ALPHA * du + dh * scale1
        acc_ref[0:1, :] += jnp.sum(dh, 0, keepdims=True)
        acc_ref[1:2, :] += jnp.sum(dh * x, 0, keepdims=True)

    tok = lambda w: pl.BlockSpec((tt, w), lambda i: (i, 0))
    by_chunk = lambda n: pl.BlockSpec((n, FS, tt), lambda i: (0, 0, i))
    return pl.pallas_call(
        body, name="ffn_bwd", grid=(L // tt,),
        in_specs=[tok(D), tok(D), _const_spec((8, D)), _const_spec((2 * FFN_CH, FS, D)), _const_spec((F, D))],
        out_specs=[tok(D), tok(D), by_chunk(2 * FFN_CH), by_chunk(FFN_CH), tok(D), pl.BlockSpec((8, D), lambda i: (0, 0))],
        out_shape=[jax.ShapeDtypeStruct((L, D), f32), jax.ShapeDtypeStruct((L, D), bf16),
                   jax.ShapeDtypeStruct((2 * FFN_CH, FS, L), bf16), jax.ShapeDtypeStruct((FFN_CH, FS, L), bf16),
                   jax.ShapeDtypeStruct((L, D), bf16), jax.ShapeDtypeStruct((8, D), f32)],
        scratch_shapes=[pltpu.VMEM((2 * FFN_CH, tt, FS), f32)],
        compiler_params=_cp(("arbitrary",)))(x, dxo, mv, w_in, w_out)


DW_TK = 4096


def _matmul_tokens(a, b, *, tn, tk, name, after=None):
    ga, gb = a.ndim == 3, b.ndim == 3
    extra = [] if after is None else [after]
    G = a.shape[0] if ga else (b.shape[0] if gb else 1)
    M, K = a.shape[-2:]
    N = b.shape[-1]
    tk = min(tk, K)
    nk = K // tk

    def body(a_ref, b_ref, *rest):
        o_ref, acc = rest[len(extra):]
        k = pl.program_id(2)
        p = _dot(a_ref[...], b_ref[...])

        @pl.when(k == 0)
        def _():
            acc[...] = p

        @pl.when(k > 0)
        def _():
            acc[...] += p

        @pl.when(k == nk - 1)
        def _():
            o_ref[...] = acc[...].astype(bf16)

    a_spec = (pl.BlockSpec((None, M, tk), lambda g, j, k: (g, 0, k)) if ga
              else pl.BlockSpec((M, tk), lambda g, j, k: (0, k)))
    b_spec = (pl.BlockSpec((None, tk, tn), lambda g, j, k: (g, k, j)) if gb
              else pl.BlockSpec((tk, tn), lambda g, j, k: (k, j)))
    if ga or gb:
        o_spec, o_shape = pl.BlockSpec((None, M, tn), lambda g, j, k: (g, 0, j)), (G, M, N)
    else:
        o_spec, o_shape = pl.BlockSpec((M, tn), lambda g, j, k: (0, j)), (M, N)
    return pl.pallas_call(
        body, name=name, grid=(G, N // tn, nk), in_specs=[a_spec, b_spec] + [ANY] * len(extra), out_specs=o_spec,
        out_shape=jax.ShapeDtypeStruct(o_shape, bf16), scratch_shapes=[pltpu.VMEM((M, tn), f32)],
        compiler_params=_cp(("parallel", "parallel", "arbitrary")))(a, b, *extra)


def _inproj_fwd(x, mv, w, *, tt=512):
    L = x.shape[0]

    def body(x_ref, mv_ref, w_ref, o_ref):
        h = (x_ref[...] * (1.0 + mv_ref[1:2, :]) + mv_ref[0:1, :]).astype(bf16)
        o_ref[...] = _dot(h, w_ref[...])

    return pl.pallas_call(
        body, name="inproj_fwd", grid=(L // tt,),
        in_specs=[pl.BlockSpec((tt, D), lambda i: (i, 0)), _const_spec((8, D)), _const_spec((D, P_W))],
        out_specs=pl.BlockSpec((tt, P_W), lambda i: (i, 0)),
        out_shape=jax.ShapeDtypeStruct((L, P_W), f32), compiler_params=_cp(("parallel",)))(x, mv, w)


def _inproj_bwd(x, dx_part, dproj, mv, w, *, tt=512):
    L = x.shape[0]

    def body(x_ref, dxp_ref, dp_ref, mv_ref, w_ref, dx_ref, h_ref, acc_ref):
        @pl.when(pl.program_id(0) == 0)
        def _():
            acc_ref[...] = jnp.zeros_like(acc_ref)
        x = x_ref[...]
        scale1 = 1.0 + mv_ref[1:2, :]
        h_ref[...] = (x * scale1 + mv_ref[0:1, :]).astype(bf16).T
        dh = _dot_nt(dp_ref[...], w_ref[...])
        dx_ref[...] = dxp_ref[...] + dh * scale1
        acc_ref[0:1, :] += jnp.sum(dh, 0, keepdims=True)
        acc_ref[1:2, :] += jnp.sum(dh * x, 0, keepdims=True)

    tok = lambda w_: pl.BlockSpec((tt, w_), lambda i: (i, 0))
    return pl.pallas_call(
        body, name="inproj_bwd", grid=(L // tt,),
        in_specs=[tok(D), tok(D), tok(P_W), _const_spec((8, D)), _const_spec((D, P_W))],
        out_specs=[tok(D), pl.BlockSpec((D, tt), lambda i: (0, i)), pl.BlockSpec((8, D), lambda i: (0, 0))],
        out_shape=[jax.ShapeDtypeStruct((L, D), f32), jax.ShapeDtypeStruct((D, L), bf16),
                   jax.ShapeDtypeStruct((8, D), f32)],
        compiler_params=_cp(("arbitrary",)))(x, dx_part, dproj, mv, w)


def _outproj_fwd(x, ycat, mv, w, *, tt=512):
    L = x.shape[0]

    def body(x_ref, y_ref, mv_ref, w_ref, o_ref):
        y = _dot(y_ref[...], w_ref[...])
        uu = ALPHA * x_ref[...] + mv_ref[2:3, :] * y
        o_ref[...] = _ln_fwd(uu, mv_ref[3:4, :], mv_ref[4:5, :])[0]

    tok = lambda w_: pl.BlockSpec((tt, w_), lambda i: (i, 0))
    return pl.pallas_call(
        body, name="outproj_fwd", grid=(L // tt,),
        in_specs=[tok(D), tok(D), _const_spec((8, D)), _const_spec((D, D))],
        out_specs=tok(D),
        out_shape=jax.ShapeDtypeStruct((L, D), f32), compiler_params=_cp(("parallel",)))(x, ycat, mv, w)


def _outproj_bwd(x, ycat, dxo, mv, w, *, tt=512):
    L = x.shape[0]

    def body(x_ref, y_ref, dxo_ref, mv_ref, w_ref, dx_ref, dy_ref, dyc_ref, yt_ref, acc_ref):
        @pl.when(pl.program_id(0) == 0)
        def _():
            acc_ref[...] = jnp.zeros_like(acc_ref)
        yt_ref[...] = y_ref[...].T
        y = _dot(y_ref[...], w_ref[...])
        gate = mv_ref[2:3, :]
        _, xhat, rstd = _ln_fwd(ALPHA * x_ref[...] + gate * y, mv_ref[3:4, :], mv_ref[4:5, :])
        du, dlg, dlb = _ln_bwd(dxo_ref[...], xhat, rstd, mv_ref[3:4, :])
        acc_ref[3:4, :] += dlg
        acc_ref[4:5, :] += dlb
        acc_ref[2:3, :] += jnp.sum(y * du, 0, keepdims=True)
        dx_ref[...] = ALPHA * du
        dyb = (gate * du).astype(bf16)
        dy_ref[...] = dyb
        dyc_ref[...] = _dot_nt(dyb, w_ref[...])

    tok = lambda w_: pl.BlockSpec((tt, w_), lambda i: (i, 0))
    return pl.pallas_call(
        body, name="outproj_bwd", grid=(L // tt,),
        in_specs=[tok(D), tok(D), tok(D), _const_spec((8, D)), _const_spec((D, D))],
        out_specs=[tok(D), tok(D), tok(D), pl.BlockSpec((D, tt), lambda i: (0, i)), pl.BlockSpec((8, D), lambda i: (0, 0))],
        out_shape=[jax.ShapeDtypeStruct((L, D), f32), jax.ShapeDtypeStruct((L, D), bf16),
                   jax.ShapeDtypeStruct((L, D), f32), jax.ShapeDtypeStruct((D, L), bf16),
                   jax.ShapeDtypeStruct((8, D), f32)],
        compiler_params=_cp(("arbitrary",)))(x, ycat, dxo, mv, w)


def _sconv_fwd(proj, w, *, tt=512):
    L = proj.shape[0]
    cb = SC_W

    def body(b_ref, c_ref, x_ref, cp_ref, xp_ref, w_ref, o_ref):
        first = jnp.where(pl.program_id(0) > 0, 1.0, 0.0)
        u = c_ref[...] * x_ref[...]
        up = cp_ref[...] * xp_ref[...] * first
        v = w_ref[2:3, :] * u + w_ref[1:2, :] * _shift_down(u, up, 1) + w_ref[0:1, :] * _shift_down(u, up, 2)
        o_ref[...] = (b_ref[...] * v).astype(bf16)

    cur = lambda col: pl.BlockSpec((tt, cb), lambda i: (i, col // cb))
    prev = lambda col: pl.BlockSpec((tt, cb), lambda i: (jnp.maximum(i - 1, 0), col // cb))
    return pl.pallas_call(
        body, name="sconv_fwd", grid=(L // tt,),
        in_specs=[cur(P_SB), cur(P_SC), cur(P_SX), prev(P_SC), prev(P_SX), _const_spec((8, cb))],
        out_specs=pl.BlockSpec((tt, cb), lambda i: (i, 0)),
        out_shape=jax.ShapeDtypeStruct((L, cb), bf16), compiler_params=_cp(("parallel",)))(proj, proj, proj, proj, proj, w)


def _sconv_bwd(proj, dycat, w, *, tt=512):
    L = proj.shape[0]
    cb = SC_W
    n = L // tt

    def body(b_ref, c_ref, x_ref, cp_ref, xp_ref, bn_ref, dy_ref, dyn_ref, w_ref, db_ref, dc_ref, dx_ref, acc_ref):
        i = pl.program_id(0)

        @pl.when(i == 0)
        def _():
            acc_ref[...] = jnp.zeros_like(acc_ref)
        first = jnp.where(i > 0, 1.0, 0.0)
        last = jnp.where(i < n - 1, 1.0, 0.0)
        cg, xin, bg = c_ref[...], x_ref[...], b_ref[...]
        u = cg * xin
        up = cp_ref[...] * xp_ref[...] * first
        u1, u2 = _shift_down(u, up, 1), _shift_down(u, up, 2)
        v = w_ref[2:3, :] * u + w_ref[1:2, :] * u1 + w_ref[0:1, :] * u2
        dy = dy_ref[...]
        db_ref[...] = (dy * v).astype(bf16)
        dv = dy * bg
        dvn = dyn_ref[...] * bn_ref[...] * last
        du = w_ref[2:3, :] * dv + w_ref[1:2, :] * _shift_up(dv, dvn, 1) + w_ref[0:1, :] * _shift_up(dv, dvn, 2)
        acc_ref[2:3, :] += jnp.sum(dv * u, 0, keepdims=True)
        acc_ref[1:2, :] += jnp.sum(dv * u1, 0, keepdims=True)
        acc_ref[0:1, :] += jnp.sum(dv * u2, 0, keepdims=True)
        dc_ref[...] = (du * xin).astype(bf16)
        dx_ref[...] = (du * cg).astype(bf16)

    cur = lambda col: pl.BlockSpec((tt, cb), lambda i: (i, col // cb))
    prev = lambda col: pl.BlockSpec((tt, cb), lambda i: (jnp.maximum(i - 1, 0), col // cb))
    nxt = lambda col: pl.BlockSpec((tt, cb), lambda i: (jnp.minimum(i + 1, n - 1), col // cb))
    ycol = SSD_W + FOX_W
    out = pl.BlockSpec((tt, cb), lambda i: (i, 0))
    return pl.pallas_call(
        body, name="sconv_bwd", grid=(n,),
        in_specs=[cur(P_SB), cur(P_SC), cur(P_SX), prev(P_SC), prev(P_SX), nxt(P_SB), cur(ycol), nxt(ycol),
                  _const_spec((8, cb))],
        out_specs=[out, out, out, pl.BlockSpec((8, cb), lambda i: (0, 0))],
        out_shape=[jax.ShapeDtypeStruct((L, cb), bf16)] * 3 + [jax.ShapeDtypeStruct((8, cb), f32)],
        compiler_params=_cp(("arbitrary",)))(proj, proj, proj, proj, proj, proj, dycat, dycat, w)


def _log1pexp(x):
    return jnp.log(1.0 + jnp.exp(-jnp.abs(x)))


NEG = -1e30
FOX_SCALE = FOX_HD ** -0.5


HL = 128
AW = FOX_H * HL


def _np_place(rows, cols, pairs, dtype):
    import numpy as np
    m = np.zeros((rows, cols), np.float32)
    for r, c in pairs:
        m[r, c] = 1.0
    return jnp.asarray(m, dtype)


def _fox_consts():
    data = [(h * FOX_HD + d, h * HL + d) for h in range(FOX_H) for d in range(FOX_HD)]
    return dict(
        pq=_np_place(FOX_W, AW, data, bf16),
        pqt=_np_place(AW, FOX_W, [(c, r) for r, c in data], bf16),
        cum_a=[_np_place(128, AW, [(SM_F + h, h * HL + 64 + r) for h in range(FOX_H)], bf16) for r in range(3)],
        head_a=[_np_place(128, AW, [(h, h * HL + 64 + r) for h in range(FOX_H)], bf16) for r in range(3)],
        head_b=[_np_place(128, AW, [(h, h * HL + 67 + r) for h in range(FOX_H)], bf16) for r in range(3)],
        group=_np_place(FOX_W, 128, [(h * FOX_HD + d, h) for h in range(FOX_H) for d in range(FOX_HD)], f32),
        col_a=_np_place(AW, 128, [(h * HL + 64, SM_F + h) for h in range(FOX_H)], f32))


def _split3(x):
    hi = x.astype(bf16)
    r1 = x - hi.astype(f32)
    mid = r1.astype(bf16)
    return hi, mid, (r1 - mid.astype(f32)).astype(bf16)


def _slot_ones(tt, first):
    lane = lax.broadcasted_iota(jnp.int32, (tt, AW), 1) % HL
    return jnp.where((lane >= first) & (lane < first + 3), 1.0, 0.0)


def _fox_prep(proj, hp, cst, *, tt=256):
    L = proj.shape[0]

    def body(q_ref, k_ref, v_ref, sm_ref, hp_ref, pq_ref, c0_ref, c1_ref, c2_ref, qa_ref, ka_ref, va_ref, tot_ref):
        xx = sm_ref[...] + hp_ref[3:4, :]
        logf = jnp.minimum(xx, 0.0) - _log1pexp(xx)
        r = lax.broadcasted_iota(jnp.int32, (tt, tt), 0)
        c = lax.broadcasted_iota(jnp.int32, (tt, tt), 1)
        cum = _dot_hi(jnp.where(r >= c, 1.0, 0.0), logf)
        tot_ref[...] = jnp.broadcast_to(cum[tt - 1:tt, :], (8, 128))
        parts = _split3(-cum)
        pq = pq_ref[...]
        a_ones, b_ones = _slot_ones(tt, 64), _slot_ones(tt, 67)
        qa_ref[...] = (_dot((q_ref[...] * FOX_SCALE).astype(bf16), pq) + a_ones).astype(bf16)
        ka = _dot(k_ref[...].astype(bf16), pq) + b_ones
        for part, c_ref in zip(parts, (c0_ref, c1_ref, c2_ref)):
            ka = ka + _dot(part, c_ref[...])
        ka_ref[...] = ka.astype(bf16)
        va_ref[...] = (_dot(v_ref[...].astype(bf16), pq) + a_ones).astype(bf16)

    col = lambda c_: pl.BlockSpec((tt, FOX_W), lambda i: (i, c_ // FOX_W))
    out = pl.BlockSpec((tt, AW), lambda i: (i, 0))
    return pl.pallas_call(
        body, name="fox_prep", grid=(L // tt,),
        in_specs=[col(P_Q), col(P_K), col(P_V), pl.BlockSpec((tt, 128), lambda i: (i, P_SM // 128)),
                  _const_spec((8, 128)), _const_spec((FOX_W, AW))] + [_const_spec((128, AW))] * 3,
        out_specs=[out, out, out, pl.BlockSpec((8, 128), lambda i: (i, 0))],
        out_shape=[jax.ShapeDtypeStruct((L, AW), bf16)] * 3 + [jax.ShapeDtypeStruct((8 * (L // tt), 128), f32)],
        compiler_params=_cp(("parallel",)))(proj, proj, proj, proj, hp, cst["pq"], *cst["cum_a"])


def _fox_attn_fwd(qa, ka, va, tot, *, tq=256):
    L = qa.shape[0]

    def body(qa_ref, ka_ref, va_ref, tot_ref, o_ref, oa_ref, lse_ref):
        i = pl.program_id(0)
        row = lax.broadcasted_iota(jnp.int32, (tq, tq), 0)
        col = lax.broadcasted_iota(jnp.int32, (tq, tq), 1)
        diag_bias = jnp.where(row >= col, 0.0, NEG)

        def block(j, carry, t_j, bias):
            r0 = pl.multiple_of(j * tq, tq)
            hls = [slice(h * HL, (h + 1) * HL) for h in range(FOX_H)]
            ss = [_dot_nt(qa_ref[:, hl], ka_ref[pl.ds(r0, tq), hl]) for hl in hls]
            if bias is not None:
                ss = [s + bias for s in ss]
            m_ins = [carry[h][0] - t_j[h] for h in range(FOX_H)]
            m_news = [jnp.maximum(m_in, jnp.max(s, -1, keepdims=True)) for m_in, s in zip(m_ins, ss)]
            ps = [jnp.exp(s - m_new) for s, m_new in zip(ss, m_news)]
            pbs = [p.astype(bf16) for p in ps]
            p_los = [(p - pb.astype(f32)).astype(bf16) for p, pb in zip(ps, pbs)]
            pvs = [_dot(pb, va_ref[pl.ds(r0, tq), hl]) + _dot(p_lo, va_ref[pl.ds(r0, tq), hl])
                   for pb, p_lo, hl in zip(pbs, p_los, hls)]
            return tuple((m_new, jnp.exp(m_in - m_new) * carry[h][1] + pv)
                         for h, (m_in, m_new, pv) in enumerate(zip(m_ins, m_news, pvs)))

        def step(k, state):
            carry, gap = state
            j = i - 1 - k
            t_j = [tot_ref[j, h] for h in range(FOX_H)]
            return block(j, carry, t_j, None), tuple(g + t for g, t in zip(gap, t_j))

        init = tuple((jnp.full((tq, 1), NEG, f32), jnp.zeros((tq, HL), f32)) for _ in range(FOX_H))
        zero_gap = tuple(jnp.zeros((), f32) for _ in range(FOX_H))
        carry = block(i, init, zero_gap, diag_bias)
        carry, gap = lax.fori_loop(0, i, step, (carry, zero_gap))
        lane = lax.broadcasted_iota(jnp.int32, (tq, 128), 1)
        lse_all = jnp.zeros((tq, 128), f32)
        for h in range(FOX_H):
            hs = slice(h * FOX_HD, (h + 1) * FOX_HD)
            m, acc = carry[h]
            l = acc[:, FOX_HD:FOX_HD + 1]
            o = acc[:, :FOX_HD] * (1.0 / l)
            o_ref[:, hs] = o.astype(bf16)
            oa_ref[:, hs] = o
            lse_all = jnp.where(lane == h, m + gap[h] + jnp.log(l), lse_all)
        lse_ref[...] = lse_all

    full = pl.BlockSpec((L, AW), lambda i: (0, 0), pipeline_mode=pl.Buffered(1))
    return pl.pallas_call(
        body, name="fox_fwd", grid=(L // tq,),
        in_specs=[pl.BlockSpec((tq, AW), lambda i: (i, 0)), full, full, pl.BlockSpec(memory_space=pltpu.SMEM)],
        out_specs=[pl.BlockSpec((tq, FOX_W), lambda i: (i, 0)), pl.BlockSpec((tq, FOX_W), lambda i: (i, 0)),
                   pl.BlockSpec((tq, 128), lambda i: (i, 0))],
        out_shape=[jax.ShapeDtypeStruct((L, FOX_W), bf16), jax.ShapeDtypeStruct((L, FOX_W), f32),
                   jax.ShapeDtypeStruct((L, 128), f32)],
        compiler_params=_cp(("parallel",)))(qa, ka, va, tot)


def _fox_bprep(qa, dycat, o_acc, lse, cst, *, tt=256):
    L = qa.shape[0]

    def body(qa_ref, do_ref, oa_ref, lse_ref, pq_ref, g_ref, a0, a1, a2, b0, b1, b2, qb_ref, doa_ref, qbt_ref, doat_ref):
        dob = do_ref[...].astype(bf16)
        delta = _dot_hi(dob.astype(f32) * oa_ref[...], g_ref[...])
        doa = _dot(dob, pq_ref[...])
        for part, ref in zip(_split3(-delta), (a0, a1, a2)):
            doa = doa + _dot(part, ref[...])
        qb = qa_ref[...].astype(f32)
        for part, ref in zip(_split3(-lse_ref[...]), (b0, b1, b2)):
            qb = qb + _dot(part, ref[...])
        doa, qb = doa.astype(bf16), qb.astype(bf16)
        doa_ref[...] = doa
        qb_ref[...] = qb
        doat_ref[...] = doa.T
        qbt_ref[...] = qb.T

    tok = lambda w_: pl.BlockSpec((tt, w_), lambda i: (i, 0))
    tr = pl.BlockSpec((AW, tt), lambda i: (0, i))
    return pl.pallas_call(
        body, name="fox_bprep", grid=(L // tt,),
        in_specs=[tok(AW), pl.BlockSpec((tt, FOX_W), lambda i: (i, SSD_W // FOX_W)), tok(FOX_W), tok(128),
                  _const_spec((FOX_W, AW)), _const_spec((FOX_W, 128))] + [_const_spec((128, AW))] * 6,
        out_specs=[tok(AW), tok(AW), tr, tr],
        out_shape=[jax.ShapeDtypeStruct((L, AW), bf16)] * 2 + [jax.ShapeDtypeStruct((AW, L), bf16)] * 2,
        compiler_params=_cp(("parallel",)))(qa, dycat, o_acc, lse, cst["pq"], cst["group"], *cst["head_a"], *cst["head_b"])


def _fox_attn_bwd(ka, va, tot, qb, doa, qbt, doat, *, tq=256):
    L = ka.shape[0]
    nq = L // tq

    def body(ka_ref, va_ref, tot_ref, qb_ref, doa_ref, qbt_ref, doat_ref, dq_ref, dkt_ref, dvt_ref):
        j = pl.program_id(0)

        @pl.when(j == 0)
        def _():
            dq_ref[...] = jnp.zeros_like(dq_ref)
        row = lax.broadcasted_iota(jnp.int32, (tq, tq), 0)
        col = lax.broadcasted_iota(jnp.int32, (tq, tq), 1)
        diag_bias = jnp.where(row >= col, 0.0, NEG)

        def block(i, carry, gap, bias):
            r0 = pl.multiple_of(i * tq, tq)
            rows = pl.ds(r0, tq)
            hls = [slice(h * HL, (h + 1) * HL) for h in range(FOX_H)]
            ss = [_dot_nt(qb_ref[rows, hl], ka_ref[:, hl]) + gap[h] for h, hl in enumerate(hls)]
            if bias is not None:
                ss = [s + bias for s in ss]
            ps = [jnp.exp(s) for s in ss]
            dss = [p * _dot_nt(doa_ref[rows, hl], va_ref[:, hl]) for p, hl in zip(ps, hls)]
            dsbs = [ds.astype(bf16) for ds in dss]
            for dsb, hl in zip(dsbs, hls):
                dq_ref[rows, hl] += _dot(dsb, ka_ref[:, hl])
            return tuple((carry[h][0] + _dot(qbt_ref[hl, rows], dsbs[h]),
                          carry[h][1] + _dot(doat_ref[hl, rows], ps[h].astype(bf16)),
                          carry[h][2] + jnp.sum(dss[h], 0, keepdims=True)) for h, hl in enumerate(hls))

        init = tuple((jnp.zeros((HL, tq), f32), jnp.zeros((HL, tq), f32), jnp.zeros((1, tq), f32))
                     for _ in range(FOX_H))
        def step(i, state):
            carry, gap = state
            gap = tuple(g + tot_ref[i - 1, h] for h, g in enumerate(gap))
            return block(i, carry, gap, None), gap

        zero_gap = tuple(jnp.zeros((), f32) for _ in range(FOX_H))
        carry = block(j, init, zero_gap, diag_bias)
        carry, _ = lax.fori_loop(j + 1, nq, step, (carry, zero_gap))
        for h in range(FOX_H):
            hl = slice(h * HL, (h + 1) * HL)
            dkt_ref[hl, :] = carry[h][0]
            dvt_ref[hl, :] = carry[h][1]
            dkt_ref[h * HL + FOX_HD:h * HL + FOX_HD + 1, :] = carry[h][2]

    full = lambda shape: pl.BlockSpec(shape, lambda j: (0, 0), pipeline_mode=pl.Buffered(1))
    blk = pl.BlockSpec((tq, AW), lambda j: (j, 0))
    trb = pl.BlockSpec((AW, tq), lambda j: (0, j))
    return pl.pallas_call(
        body, name="fox_bwd", grid=(nq,),
        in_specs=[blk, blk, pl.BlockSpec(memory_space=pltpu.SMEM), full((L, AW)), full((L, AW)), full((AW, L)),
                  full((AW, L))],
        out_specs=[pl.BlockSpec((L, AW), lambda j: (0, 0)), trb, trb],
        out_shape=[jax.ShapeDtypeStruct((L, AW), f32), jax.ShapeDtypeStruct((AW, L), f32),
                   jax.ShapeDtypeStruct((AW, L), f32)],
        compiler_params=_cp(("arbitrary",)))(ka, va, tot, qb, doa, qbt, doat)


def _fox_post(dq, dkt, dvt, proj, ddt, hp, cst, *, tt=256):
    L = proj.shape[0]
    n = L // tt

    def body(dq_ref, dkt_ref, dvt_ref, sm_ref, ddt_ref, hp_ref, pqt_ref, ca_ref,
             dqo_ref, dko_ref, dvo_ref, dsm_ref, acc_ref, carry):
        @pl.when(pl.program_id(0) == 0)
        def _():
            carry[...] = jnp.zeros_like(carry)
            acc_ref[...] = jnp.zeros_like(acc_ref)
        pqt = pqt_ref[...]
        dk_full = dkt_ref[...].T
        dqo_ref[...] = _dot((dq_ref[...] * FOX_SCALE).astype(bf16), pqt).astype(bf16)
        dko_ref[...] = _dot(dk_full.astype(bf16), pqt).astype(bf16)
        dvo_ref[...] = _dot(dvt_ref[...].T.astype(bf16), pqt).astype(bf16)
        dc = -_dot_hi(dk_full, ca_ref[...])
        r = lax.broadcasted_iota(jnp.int32, (tt, tt), 0)
        c = lax.broadcasted_iota(jnp.int32, (tt, tt), 1)
        dl = _dot_hi(jnp.where(r <= c, 1.0, 0.0), dc) + carry[0:1, :]
        carry[0:1, :] += jnp.sum(dc, 0, keepdims=True)
        xx = sm_ref[...] + hp_ref[3:4, :]
        lane = lax.broadcasted_iota(jnp.int32, (tt, 128), 1)
        dlogit = jnp.where((lane >= SM_F) & (lane < SM_F + FOX_H), dl * _sigmoid(-xx), 0.0)
        acc_ref[3:4, :] += jnp.sum(dlogit, 0, keepdims=True)
        dsm_ref[...] = (dlogit + ddt_ref[...]).astype(bf16)

    rev = lambda w_: pl.BlockSpec((tt, w_), lambda i: (n - 1 - i, 0))
    revt = pl.BlockSpec((AW, tt), lambda i: (0, n - 1 - i))
    return pl.pallas_call(
        body, name="fox_post", grid=(n,),
        in_specs=[rev(AW), revt, revt, pl.BlockSpec((tt, 128), lambda i: (n - 1 - i, P_SM // 128)), rev(128),
                  _const_spec((8, 128)), _const_spec((AW, FOX_W)), _const_spec((AW, 128))],
        out_specs=[rev(FOX_W), rev(FOX_W), rev(FOX_W), rev(128), pl.BlockSpec((8, 128), lambda i: (0, 0))],
        out_shape=[jax.ShapeDtypeStruct((L, FOX_W), bf16)] * 3 + [jax.ShapeDtypeStruct((L, 128), bf16),
                                                                   jax.ShapeDtypeStruct((8, 128), f32)],
        scratch_shapes=[pltpu.VMEM((8, 128), f32)],
        compiler_params=_cp(("arbitrary",)))(dq, dkt, dvt, proj, ddt, hp, cst["pqt"], cst["col_a"])


SSD_GW = SSD_W // SSD_G
SSD_HPG = SSD_H // SSD_G


def _ssd_pre(x, xprev, sm, cp_ref, hp_ref, tc):
    pre = (cp_ref[4:5, :] + cp_ref[3:4, :] * x + cp_ref[2:3, :] * _shift_down(x, xprev, 1)
           + cp_ref[1:2, :] * _shift_down(x, xprev, 2) + cp_ref[0:1, :] * _shift_down(x, xprev, 3))
    sig = _sigmoid(pre)
    raw = sm + hp_ref[0:1, :]
    dt = jnp.maximum(raw, 0.0) + _log1pexp(raw)
    a_neg = -jnp.exp(hp_ref[1:2, :])
    r = lax.broadcasted_iota(jnp.int32, (tc, tc), 0)
    c = lax.broadcasted_iota(jnp.int32, (tc, tc), 1)
    cs = _dot_hi(jnp.where(r >= c, 1.0, 0.0), dt * a_neg)
    return pre, sig, raw, dt, a_neg, cs, cs.T, r >= c


def _ssd_fwd(proj, cp, hp, ng, *, tc=256):
    L = proj.shape[0]
    nc = L // tc

    def body(xc_ref, xp_ref, z_ref, sm_ref, cp_ref, hp_ref, ng_ref, y_ref, ypre_ref, sin_ref, s_scr):
        i = pl.program_id(0)

        @pl.when(i == 0)
        def _():
            s_scr[...] = jnp.zeros_like(s_scr)
        x = xc_ref[...]
        xprev = xp_ref[...] * jnp.where(i > 0, 1.0, 0.0)
        pre, sig, _, dt, _, cs, cst, tril = _ssd_pre(x, xprev, sm_ref[...], cp_ref, hp_ref, tc)
        xbc = pre * sig
        sin_ref[...] = s_scr[...]
        for g in range(SSD_G):
            bg = xbc[:, SSD_W + g * SSD_N:SSD_W + (g + 1) * SSD_N]
            cg = xbc[:, SSD_W + SSD_G * SSD_N + g * SSD_N:SSD_W + SSD_G * SSD_N + (g + 1) * SSD_N].astype(bf16)
            cb = _dot_nt(cg, bg.astype(bf16))
            heads = [g * SSD_HPG + e for e in range(SSD_HPG)]
            hss = [slice(h * SSD_HD, (h + 1) * SSD_HD) for h in heads]
            xss = [xbc[:, hs] for hs in hss]
            cscs = [cs[:, h:h + 1] for h in heads]
            ms = [(cb * jnp.where(tril, jnp.exp(jnp.minimum(csc - cst[h:h + 1, :], 0.0)), 0.0)).astype(bf16)
                  for h, csc in zip(heads, cscs)]
            xdts = [(xs * dt[:, h:h + 1]).astype(bf16) for h, xs in zip(heads, xss)]
            s_hs = [s_scr[:, hs] for hs in hss]
            ys = [_dot(m, xdt) + jnp.exp(csc) * _dot(cg, s_h.astype(bf16))
                  for m, xdt, csc, s_h in zip(ms, xdts, cscs, s_hs)]
            cls = [cs[tc - 1:tc, h:h + 1] for h in heads]
            upd = [_dot_tn((bg * jnp.exp(cl - csc)).astype(bf16), xdt) for cl, csc, xdt in zip(cls, cscs, xdts)]
            for k, (h, hs) in enumerate(zip(heads, hss)):
                ypre_ref[:, hs] = ys[k] + hp_ref[2:3, h:h + 1] * xss[k]
                s_scr[:, hs] = jnp.exp(cls[k]) * s_hs[k] + upd[k]
        z = z_ref[...]
        yz = ypre_ref[...] * (z * _sigmoid(z))
        for g in range(SSD_G):
            gs = slice(g * SSD_GW, (g + 1) * SSD_GW)
            yg = yz[:, gs]
            r = lax.rsqrt(jnp.mean(yg * yg, -1, keepdims=True) + RMS_EPS)
            y_ref[:, gs] = (yg * r * ng_ref[0:1, gs]).astype(bf16)

    return pl.pallas_call(
        body, name="ssd_fwd", grid=(nc,),
        in_specs=[pl.BlockSpec((tc, 1024), lambda i: (i, 0)),
                  pl.BlockSpec((tc, 1024), lambda i: (jnp.maximum(i - 1, 0), 0)),
                  pl.BlockSpec((tc, SSD_W), lambda i: (i, P_Z // SSD_W)),
                  pl.BlockSpec((tc, 128), lambda i: (i, P_SM // 128)),
                  _const_spec((8, 1024)), _const_spec((8, 128)), _const_spec((8, SSD_W))],
        out_specs=[pl.BlockSpec((tc, SSD_W), lambda i: (i, 0)), pl.BlockSpec((tc, SSD_W), lambda i: (i, 0)),
                   pl.BlockSpec((SSD_N, SSD_W), lambda i: (i, 0))],
        out_shape=[jax.ShapeDtypeStruct((L, SSD_W), bf16), jax.ShapeDtypeStruct((L, SSD_W), f32),
                   jax.ShapeDtypeStruct((nc * SSD_N, SSD_W), f32)],
        scratch_shapes=[pltpu.VMEM((SSD_N, SSD_W), f32)],
        compiler_params=_cp(("arbitrary",)))(proj, proj, proj, proj, cp, hp, ng)


def _ssd_bwd(proj, dycat, ypre, sin, cp, hp, ng, *, tc=256):
    L = proj.shape[0]
    nc = L // tc

    def body(xc_ref, xp_ref, z_ref, sm_ref, cp_ref, hp_ref, ng_ref, sin_ref, ypre_ref, dy_ref,
             dxbc_ref, dz_ref, ddt_ref, acc1_ref, acc2_ref, ds_scr, dnext_scr, dxbc_scr):
        i = pl.program_id(0)
        c_idx = nc - 1 - i

        @pl.when(i == 0)
        def _():
            ds_scr[...] = jnp.zeros_like(ds_scr)
            dnext_scr[...] = jnp.zeros_like(dnext_scr)
            acc1_ref[...] = jnp.zeros_like(acc1_ref)
            acc2_ref[...] = jnp.zeros_like(acc2_ref)
        x = xc_ref[...]
        xprev = xp_ref[...] * jnp.where(c_idx > 0, 1.0, 0.0)
        pre, sig, raw, dt, a_neg, cs, cst, tril = _ssd_pre(x, xprev, sm_ref[...], cp_ref, hp_ref, tc)
        xbc = pre * sig
        z = z_ref[...]
        sz = _sigmoid(z)
        silz = z * sz
        yall = ypre_ref[...]
        yz = yall * silz
        dy = dy_ref[...]
        dyz_parts = []
        for g in range(SSD_G):
            gs = slice(g * SSD_GW, (g + 1) * SSD_GW)
            yg, dyg = yz[:, gs], dy[:, gs]
            r = lax.rsqrt(jnp.mean(yg * yg, -1, keepdims=True) + RMS_EPS)
            acc1_ref[5:6, gs] += jnp.sum(dyg * yg * r, 0, keepdims=True)
            dyn = dyg * ng_ref[0:1, gs]
            dyz_parts.append(r * (dyn - yg * (r * r) * jnp.mean(dyn * yg, -1, keepdims=True)))
        dyz = jnp.concatenate(dyz_parts, axis=1)
        dz_ref[...] = (dyz * yall * (sz * (1.0 + z * (1.0 - sz)))).astype(bf16)
        dyall = dyz * silz

        lane1 = lax.broadcasted_iota(jnp.int32, (1, 128), 1)
        sub = lax.broadcasted_iota(jnp.int32, (128, tc), 0)
        rowc = lax.broadcasted_iota(jnp.int32, (tc, 1), 0)
        dcs = jnp.zeros((tc, 128), f32)
        dcsr = jnp.zeros((128, tc), f32)
        ddt = jnp.zeros((tc, 128), f32)
        dd_row = jnp.zeros((1, 128), f32)
        for g in range(SSD_G):
            b0 = SSD_W + g * SSD_N
            c0 = SSD_W + SSD_G * SSD_N + g * SSD_N
            bg = xbc[:, b0:b0 + SSD_N]
            bgb = bg.astype(bf16)
            cgb = xbc[:, c0:c0 + SSD_N].astype(bf16)
            cb = _dot_nt(cgb, bgb)
            dbg = jnp.zeros((tc, SSD_N), f32)
            dcg = jnp.zeros((tc, SSD_N), f32)
            heads = [g * SSD_HPG + e for e in range(SSD_HPG)]
            hss = [slice(h * SSD_HD, (h + 1) * SSD_HD) for h in heads]
            ohs = [jnp.where(lane1 == h, 1.0, 0.0) for h in heads]
            xss = [xbc[:, hs] for hs in hss]
            dths = [dt[:, h:h + 1] for h in heads]
            cscs = [cs[:, h:h + 1] for h in heads]
            lms = [jnp.where(tril, jnp.exp(jnp.minimum(csc - cst[h:h + 1, :], 0.0)), 0.0) for h, csc in zip(heads, cscs)]
            ms = [cb * lm for lm in lms]
            xdts = [(xs * dth).astype(bf16) for xs, dth in zip(xss, dths)]
            s_hs = [sin_ref[:, hs] for hs in hss]
            s_hbs = [s_h.astype(bf16) for s_h in s_hs]
            dyhs = [dyall[:, hs] for hs in hss]
            dybs = [dyh.astype(bf16) for dyh in dyhs]
            ecss = [jnp.exp(csc) for csc in cscs]
            cs_prods = [_dot(cgb, s_hb) for s_hb in s_hbs]
            dcsbs = [(ecs * dyh).astype(bf16) for ecs, dyh in zip(ecss, dyhs)]
            dcg_a = [_dot_nt(dcsb, s_hb) for dcsb, s_hb in zip(dcsbs, s_hbs)]
            ds_ins = [_dot_tn(cgb, dcsb) for dcsb in dcsbs]
            dms = [_dot_nt(dyb, xdt) for dyb, xdt in zip(dybs, xdts)]
            ws = [dm * m for dm, m in zip(dms, ms)]
            dcbbs = [(dm * lm).astype(bf16) for dm, lm in zip(dms, lms)]
            dcg_b = [_dot(dcbb, bgb) for dcbb in dcbbs]
            dbg_a = [_dot_tn(dcbb, cgb) for dcbb in dcbbs]
            dxdts = [_dot_tn(m.astype(bf16), dyb) for m, dyb in zip(ms, dybs)]
            dsns = [ds_scr[:, hs] for hs in hss]
            dsnbs = [dsn.astype(bf16) for dsn in dsns]
            cls = [cs[tc - 1:tc, h:h + 1] for h in heads]
            decs = [jnp.exp(cl - csc) for cl, csc in zip(cls, cscs)]
            dxdts = [dxdt + _dot((bg * dec).astype(bf16), dsnb) for dxdt, dec, dsnb in zip(dxdts, decs, dsnbs)]
            dbds = [_dot_nt(xdt, dsnb) for xdt, dsnb in zip(xdts, dsnbs)]
            gdecs = [jnp.sum(dbd * bg, -1, keepdims=True) * dec for dbd, dec in zip(dbds, decs)]
            ecls = [jnp.exp(cl) for cl in cls]
            for k, (h, hs) in enumerate(zip(heads, hss)):
                dd_row = dd_row + ohs[k] * jnp.sum(dyhs[k] * xss[k])
                dcg = dcg + dcg_a[k] + dcg_b[k]
                dbg = dbg + dbg_a[k] + dbds[k] * decs[k]
                dcl = jnp.sum(gdecs[k]) + jnp.sum(dsns[k] * s_hs[k]) * ecls[k]
                ds_scr[:, hs] = ecls[k] * dsns[k] + ds_ins[k]
                dcs_h = (jnp.sum(dyhs[k] * ecss[k] * cs_prods[k], -1, keepdims=True)
                         + jnp.sum(ws[k], -1, keepdims=True) - gdecs[k] + jnp.where(rowc == tc - 1, dcl, 0.0))
                dcs = dcs + dcs_h * ohs[k]
                dcsr = jnp.where(sub == h, jnp.sum(ws[k], 0, keepdims=True), dcsr)
                dxbc_scr[:, hs] = hp_ref[2:3, h:h + 1] * dyhs[k] + dxdts[k] * dths[k]
                ddt = ddt + jnp.sum(dxdts[k] * xss[k], -1, keepdims=True) * ohs[k]
            dxbc_scr[:, b0:b0 + SSD_N] = dbg
            dxbc_scr[:, c0:c0 + SSD_N] = dcg
        dcs = dcs - dcsr.T
        r_i = lax.broadcasted_iota(jnp.int32, (tc, tc), 0)
        c_i = lax.broadcasted_iota(jnp.int32, (tc, tc), 1)
        da = _dot_hi(jnp.where(r_i <= c_i, 1.0, 0.0), dcs)
        ddt = ddt + da * a_neg
        acc2_ref[1:2, :] += jnp.sum(da * dt, 0, keepdims=True) * a_neg
        lane = lax.broadcasted_iota(jnp.int32, (tc, 128), 1)
        ddraw = jnp.where(lane < SSD_H, ddt * _sigmoid(raw), 0.0)
        acc2_ref[0:1, :] += jnp.sum(ddraw, 0, keepdims=True)
        acc2_ref[2:3, :] += dd_row
        ddt_ref[...] = ddraw
        dpre = dxbc_scr[...] * (sig * (1.0 + pre * (1.0 - sig)))
        acc1_ref[4:5, :] += jnp.sum(dpre, 0, keepdims=True)
        for k in range(SSD_K):
            acc1_ref[k:k + 1, :] += jnp.sum(dpre * _shift_down(x, xprev, SSD_K - 1 - k), 0, keepdims=True)
        dnext = dnext_scr[...]
        dxbc_ref[...] = (cp_ref[3:4, :] * dpre + cp_ref[2:3, :] * _shift_up(dpre, dnext, 1)
                         + cp_ref[1:2, :] * _shift_up(dpre, dnext, 2)
                         + cp_ref[0:1, :] * _shift_up(dpre, dnext, 3)).astype(bf16)
        dnext_scr[...] = dpre

    rev = lambda w_, col: pl.BlockSpec((tc, w_), lambda i: (nc - 1 - i, col // w_))
    return pl.pallas_call(
        body, name="ssd_bwd", grid=(nc,),
        in_specs=[rev(1024, 0), pl.BlockSpec((tc, 1024), lambda i: (jnp.maximum(nc - 2 - i, 0), 0)),
                  rev(SSD_W, P_Z), rev(128, P_SM),
                  _const_spec((8, 1024)), _const_spec((8, 128)), _const_spec((8, SSD_W)),
                  pl.BlockSpec((SSD_N, SSD_W), lambda i: (nc - 1 - i, 0)), rev(SSD_W, 0), rev(SSD_W, 0)],
        out_specs=[rev(1024, 0), rev(SSD_W, 0), rev(128, 0),
                   pl.BlockSpec((8, 1024), lambda i: (0, 0)), pl.BlockSpec((8, 128), lambda i: (0, 0))],
        out_shape=[jax.ShapeDtypeStruct((L, 1024), bf16), jax.ShapeDtypeStruct((L, SSD_W), bf16),
                   jax.ShapeDtypeStruct((L, 128), f32), jax.ShapeDtypeStruct((8, 1024), f32),
                   jax.ShapeDtypeStruct((8, 128), f32)],
        scratch_shapes=[pltpu.VMEM((SSD_N, SSD_W), f32), pltpu.VMEM((tc, 1024), f32), pltpu.VMEM((tc, 1024), f32)],
        compiler_params=_cp(("arbitrary",)))(proj, proj, proj, proj, cp, hp, ng, sin, ypre, dycat)


def _pack_cols(w):
    pad = jnp.zeros(w.shape[:-1] + (P_W - P_SM - SSD_H - FOX_H,), w.dtype)
    return jnp.concatenate([w[..., 512:1536], w[..., 0:512], w[..., 1544:2312], w[..., 2316:3084],
                            w[..., 1536:1544], w[..., 2312:2316], pad], axis=-1)


def _unpack_cols(g):
    return jnp.concatenate([g[..., 1024:1536], g[..., 0:1024], g[..., 3072:3080], g[..., 1536:2304],
                            g[..., 3080:3084], g[..., 2304:3072]], axis=-1)


def _rows8(*rows):
    width = max(r.shape[-1] for r in rows)
    out = [jnp.pad(r.astype(f32), (0, width - r.shape[-1])) for r in rows]
    out += [jnp.zeros((width,), f32)] * (8 - len(out))
    return jnp.stack(out)


def _local_step(x, tgt, p, weights_of, grads_done):
    gb_in = _rows8(p["ln_in_g"], p["ln_in_b"])
    cst = _fox_consts()
    rows_to_8 = lambda t: jnp.pad(t, [(0, 0)] * (t.ndim - 2) + [(0, 8 - t.shape[-2]), (0, 0)])
    lanes = lambda t, lo: jnp.pad(t, ((0, 0), (lo, 128 - lo - t.shape[-1])))[:, None, :]
    mvs = rows_to_8(jnp.concatenate([p["mod"].reshape(DEPTH, 3, 3, D), p["ln_g"][:, :, None], p["ln_b"][:, :, None]], 2))
    cps = rows_to_8(jnp.concatenate([p["ssd_conv_w"], p["ssd_conv_b"][:, None]], 1))
    hps = rows_to_8(jnp.concatenate([lanes(p["ssd_dt_bias"], 0), lanes(p["ssd_a_log"], 0), lanes(p["ssd_d"], 0),
                                     lanes(p["fox_f_bias"], SM_F)], 1))
    ngs = rows_to_8(p["ssd_norm_g"][:, None])
    scws = rows_to_8(p["sconv_w"])
    x0 = _ln_in_fwd(x, gb_in)
    saved = []
    for l in range(DEPTH):
        mv = [mvs[l, j] for j in range(3)]
        cp, hp, ng, scw = cps[l], hps[l], ngs[l], scws[l]
        w = dict(weights_of(l, "ffn1", x0))
        x1 = _ffn_fwd(x0, mv[0], w["ffn1_w_in"], w["ffn1_w_out"])
        w.update(weights_of(l, "mix", x1))
        proj = _inproj_fwd(x1, mv[1], w["mix_w_in"])
        y_ssd, ypre, sin = _ssd_fwd(proj, cp, hp, ng)
        qa, ka, va, tot = _fox_prep(proj, hp, cst)
        tot = tot[::8, SM_F:SM_F + FOX_H]
        o, o_acc, lse = _fox_attn_fwd(qa, ka, va, tot)
        y_sc = _sconv_fwd(proj, scw)
        ycat = jnp.concatenate([y_ssd, o, y_sc], axis=1)
        x2 = _outproj_fwd(x1, ycat, mv[1], w["mix_w_out"])
        w.update(weights_of(l, "ffn2", x2))
        x3 = _ffn_fwd(x2, mv[2], w["ffn2_w_in"], w["ffn2_w_out"])
        saved.append((x0, x1, x2, mv, cp, hp, ng, scw, proj, ypre, sin, qa, ka, va, tot, o_acc, lse, ycat, w))
        x0 = x3
    dx, loss_acc = _loss_head(x0, tgt)

    g = {k: [None] * DEPTH for k in (
        "mod", "ln_g", "ln_b", "ssd_conv_w", "ssd_conv_b", "ssd_dt_bias", "ssd_a_log", "ssd_d", "ssd_norm_g",
        "fox_f_bias", "sconv_w")}
    def behind(small, tok):
        return small if tok is None else small + tok[0, 0]

    token = None
    for l in reversed(range(DEPTH)):
        x0, x1, x2, mv, cp, hp, ng, scw, proj, ypre, sin, qa, ka, va, tot, o_acc, lse, ycat, w = saved[l]
        dx, h, dat, actt, dyb, a2 = _ffn_bwd(x2, dx, behind(mv[2], token), w["ffn2_w_in"], w["ffn2_w_out"])
        token = grads_done(l, "ffn2_w_out", {"ffn2_w_out": _matmul_tokens(actt, dyb, tn=D, tk=DW_TK, name="dw_ffn_out")})
        token = grads_done(l, "ffn2_w_in", {"ffn2_w_in": _matmul_tokens(dat, h, tn=D, tk=DW_TK, name="dw_ffn_in",
                                                                        after=token)})
        mv1 = behind(mv[1], token)
        dxp, dyb, dycat, ycat_t, a1 = _outproj_bwd(x1, ycat, dx, mv1, w["mix_w_out"])
        gw_mix_out = _matmul_tokens(ycat_t, dyb, tn=D // 2, tk=DW_TK, name="dw_mix_out")
        dxbc, dz, ddt, acc1, acc2 = _ssd_bwd(proj, dycat, ypre, sin, cp, hp, ng)
        qb, doa, qbt, doat = _fox_bprep(qa, dycat, o_acc, lse, cst)
        dq, dkt, dvt = _fox_attn_bwd(ka, va, tot, qb, doa, qbt, doat)
        dq, dk, dv, dsm, accf = _fox_post(dq, dkt, dvt, proj, ddt, hp, cst)
        dsb, dsc, dsx, accs = _sconv_bwd(proj, dycat, scw)
        dproj = jnp.concatenate([dxbc, dz, dq, dk, dv, dsb, dsc, dsx, dsm], axis=1)
        dx, ht, a1b = _inproj_bwd(x1, dxp, dproj, mv1, w["mix_w_in"])
        token = grads_done(l, "mix", {"mix_w_in": _matmul_tokens(ht, dproj, tn=P_W // 5, tk=DW_TK, name="dw_mix_in"),
                                      "mix_w_out": gw_mix_out})
        dx, h, dat, actt, dyb, a0 = _ffn_bwd(x0, dx, behind(mv[0], token), w["ffn1_w_in"], w["ffn1_w_out"])
        token = grads_done(l, "ffn1_w_out", {"ffn1_w_out": _matmul_tokens(actt, dyb, tn=D, tk=DW_TK, name="dw_ffn_out")})
        token = grads_done(l, "ffn1_w_in", {"ffn1_w_in": _matmul_tokens(dat, h, tn=D, tk=DW_TK, name="dw_ffn_in",
                                                                        after=token)})
        g["mod"][l] = jnp.concatenate([a0[0:3], a1b[0:2], a1[2:3], a2[0:3]], axis=0)
        g["ln_g"][l] = jnp.stack([a0[3], a1[3], a2[3]])
        g["ln_b"][l] = jnp.stack([a0[4], a1[4], a2[4]])
        g["ssd_conv_w"][l] = acc1[0:SSD_K]
        g["ssd_conv_b"][l] = acc1[4]
        g["ssd_norm_g"][l] = acc1[5, :SSD_W]
        g["ssd_dt_bias"][l] = acc2[0, :SSD_H]
        g["ssd_a_log"][l] = acc2[1, :SSD_H]
        g["ssd_d"][l] = acc2[2, :SSD_H]
        g["fox_f_bias"][l] = accf[3, SM_F:SM_F + FOX_H]
        g["sconv_w"][l] = accs[0:SC_K]
    grad_x, a_in = _ln_in_bwd(x, dx, behind(gb_in, token))
    g = {k: jnp.stack(v) for k, v in g.items()}
    g["ln_in_g"], g["ln_in_b"] = a_in[0], a_in[1]
    return loss_acc[0, 0], grad_x, g


MESH = pl.DeviceIdType.MESH
ANY = pl.BlockSpec(memory_space=pl.ANY)


def _all_gather(shards, *, in_vmem, name):
    n_arr = len(shards)

    def body(*refs):
        x_refs, out_refs = refs[:n_arr], refs[n_arr:2 * n_arr]
        send_sems, recv_sems, local_sems = refs[2 * n_arr:]
        x, y, c = lax.axis_index("x"), lax.axis_index("y"), lax.axis_index("c")
        me, sibling = (x, y, c), (x, y, 1 - c)
        chips = [(1 - x, y), (x, 1 - y), (1 - x, 1 - y)]

        def copy(a, k, block, to, src=None):
            px, py, pc = block
            slot = out_refs[a].at[4 * px + 2 * py + pc]
            return pltpu.make_async_remote_copy(
                src_ref=slot if src is None else src, dst_ref=slot,
                send_sem=send_sems.at[7 * a + k], recv_sem=recv_sems.at[7 * a + k], device_id=to, device_id_type=MESH)

        mine, first, passed = [], [], []
        for a in range(n_arr):
            mine.append(pltpu.make_async_copy(x_refs[a], out_refs[a].at[4 * x + 2 * y + c], local_sems.at[a]))
            mine[-1].start()
            first.append(copy(a, 0, me, sibling, src=x_refs[a]))
            first += [copy(a, 1 + j, me, (*chip, c), src=x_refs[a]) for j, chip in enumerate(chips)]
        for cp in first:
            cp.start()
        for j, chip in enumerate(chips):
            for a in range(n_arr):
                copy(a, 1 + j, (*chip, c), me).wait_recv()
                passed.append(copy(a, 4 + j, (*chip, c), sibling))
                passed[-1].start()
        for a in range(n_arr):
            copy(a, 0, sibling, me).wait_recv()
            for j, chip in enumerate(chips):
                copy(a, 4 + j, (*chip, 1 - c), me).wait_recv()
        for cp in first + passed:
            cp.wait_send()
        for cp in mine:
            cp.wait()

    spec = pl.BlockSpec(memory_space=pltpu.VMEM) if in_vmem else ANY
    return pl.pallas_call(
        body, name=name, out_shape=[jax.ShapeDtypeStruct((N_DEV,) + s.shape, s.dtype) for s in shards],
        in_specs=[spec] * n_arr, out_specs=[spec] * n_arr,
        scratch_shapes=[pltpu.SemaphoreType.DMA((7 * n_arr,)), pltpu.SemaphoreType.DMA((7 * n_arr,)),
                        pltpu.SemaphoreType.DMA((n_arr,))],
    )(*shards)


HBM = pl.BlockSpec(memory_space=pltpu.HBM)
SEM = pl.BlockSpec(memory_space=pltpu.SEMAPHORE)
EFFECT = pltpu.SideEffectType.DATAFLOW_SIDE_EFFECTING


def _spread_copies(srcs, lands, send_sems, recv_sems, local_sems, scatter):
    x, y, c = lax.axis_index("x"), lax.axis_index("y"), lax.axis_index("c")
    me = 4 * x + 2 * y + c
    local, remote = [], []
    for a in range(len(srcs)):
        own = srcs[a].at[me] if scatter else srcs[a]
        local.append(pltpu.make_async_copy(own, lands[a].at[me], local_sems.at[a]))
        for r in range(1, N_DEV):
            px, py, pc = (1 - x if r & 4 else x), (1 - y if r & 2 else y), (1 - c if r & 1 else c)
            peer = 4 * px + 2 * py + pc
            k = (N_DEV - 1) * a + r - 1
            mk = functools.partial(pltpu.make_async_remote_copy, send_sem=send_sems.at[k], recv_sem=recv_sems.at[k],
                                   device_id=(px, py, pc), device_id_type=MESH)
            remote.append((mk(src_ref=srcs[a].at[peer] if scatter else srcs[a], dst_ref=lands[a].at[me]),
                           mk(src_ref=own, dst_ref=lands[a].at[peer])))
    return local, remote


def _spread_start(srcs, *, scatter, name, after=()):
    n, k = len(srcs), len(after)
    lands = [jax.ShapeDtypeStruct((N_DEV,) + s.shape[-2:], s.dtype) for s in srcs]

    def body(*refs):
        src, land = refs[:n], refs[n:2 * n]
        send_sems, recv_sems, local_sems = refs[2 * n + k:2 * n + k + 3]
        token = refs[-1]
        local, remote = _spread_copies(src, land, send_sems, recv_sems, local_sems, scatter)
        for cp in local:
            cp.start()
        for cp, _ in remote:
            cp.start()
        token[...] = jnp.zeros_like(token)

    nsem = (N_DEV - 1) * n
    out = pl.pallas_call(
        body, name=name,
        out_shape=(pltpu.SemaphoreType.DMA((nsem,)), pltpu.SemaphoreType.DMA((nsem,)), pltpu.SemaphoreType.DMA((n,)),
                   *[pltpu.HBM(s.shape, s.dtype) for s in srcs], *[pltpu.HBM(s.shape, s.dtype) for s in lands],
                   jax.ShapeDtypeStruct((8, 128), f32)),
        in_specs=[HBM] * (2 * n) + [ANY] * k,
        out_specs=(SEM, SEM, SEM, *[HBM] * (2 * n), pl.BlockSpec(memory_space=pltpu.VMEM)),
        input_output_aliases={i: 3 + i for i in range(2 * n)},
        compiler_params=pltpu.CompilerParams(has_side_effects=EFFECT),
    )(*[pltpu.with_memory_space_constraint(s, pltpu.HBM) for s in srcs],
      *[pltpu.with_memory_space_constraint(lax.empty(s.shape, s.dtype), pltpu.HBM) for s in lands], *after)
    return out[:-1], out[-1]


def _spread_wait(state, after, *, scatter, name):
    n = (len(state) - 3) // 2
    sems, thru = state[:3], state[3:]

    def body(*refs):
        src, land = refs[:n], refs[n:2 * n]
        send_sems, recv_sems, local_sems = refs[2 * n:2 * n + 3]
        local, remote = _spread_copies(src, land, send_sems, recv_sems, local_sems, scatter)
        for sent, received in remote:
            sent.wait_send()
            received.wait_recv()
        for cp in local:
            cp.wait()

    out = pl.pallas_call(
        body, name=name,
        out_shape=tuple(pltpu.HBM(t.shape, t.dtype) for t in thru),
        in_specs=[HBM] * (2 * n) + [SEM] * 3 + [ANY], out_specs=tuple([HBM] * (2 * n)),
        input_output_aliases={i: i for i in range(2 * n)},
        compiler_params=pltpu.CompilerParams(has_side_effects=EFFECT),
    )(*thru, *sems, after)
    return list(out[n:])


def _sum_slots(buf, *, tr, name):
    nb, r, n = buf.shape

    def body(b_ref, o_ref):
        acc = b_ref[0].astype(f32)
        for k in range(1, nb):
            acc = acc + b_ref[k].astype(f32)
        o_ref[...] = acc

    return pl.pallas_call(
        body, name=name, grid=(r // tr,),
        in_specs=[pl.BlockSpec((nb, tr, n), lambda i: (0, i, 0))],
        out_specs=pl.BlockSpec((tr, n), lambda i: (i, 0)),
        out_shape=jax.ShapeDtypeStruct((r, n), f32), compiler_params=_cp(("parallel",)))(buf)


def _ada_fwd(c_all, ada_w, ada_b_cols):
    n = ada_w.shape[-1]

    def body(c_ref, w_ref, b_ref, o_ref):
        cv = c_ref[...]
        ca = (cv * _sigmoid(cv)).astype(bf16)
        o_ref[...] = _dot(ca, w_ref[...].astype(bf16)) + b_ref[...]

    return pl.pallas_call(
        body, name="ada_fwd", grid=(DEPTH,),
        in_specs=[_const_spec((N_DEV, D)), pl.BlockSpec((None, D, n), lambda l: (l, 0, 0)),
                  pl.BlockSpec((None, 1, n), lambda l: (l, 0, 0))],
        out_specs=pl.BlockSpec((None, N_DEV, n), lambda l: (l, 0, 0)),
        out_shape=jax.ShapeDtypeStruct((DEPTH, N_DEV, n), f32), compiler_params=_cp(("parallel",)))(c_all, ada_w, ada_b_cols)


def _ada_bwd(c_all, dmod_cols):
    n = dmod_cols.shape[-1]

    def body(c_ref, d_ref, o_ref):
        cv = c_ref[...]
        ca = (cv * _sigmoid(cv)).astype(bf16)
        o_ref[...] = _dot_tn(ca, d_ref[...].astype(bf16))

    return pl.pallas_call(
        body, name="ada_bwd", grid=(DEPTH,),
        in_specs=[_const_spec((N_DEV, D)), pl.BlockSpec((None, N_DEV, n), lambda l: (l, 0, 0))],
        out_specs=pl.BlockSpec((None, D, n), lambda l: (l, 0, 0)),
        out_shape=jax.ShapeDtypeStruct((DEPTH, D, n), f32), compiler_params=_cp(("parallel",)))(c_all, dmod_cols)


def _adamw(w, g, m, v, *, tr, name):
    r, n = w.shape

    def body(w_ref, g_ref, m_ref, v_ref, d_ref, mo_ref, vo_ref):
        g_ = g_ref[...]
        m_ = ADAM_B1 * m_ref[...] + (1.0 - ADAM_B1) * g_
        v_ = ADAM_B2 * v_ref[...] + (1.0 - ADAM_B2) * jnp.square(g_)
        m_hat = m_ / (1.0 - ADAM_B1 ** ADAM_STEP)
        v_hat = v_ / (1.0 - ADAM_B2 ** ADAM_STEP)
        d_ref[...] = -ADAM_LR * (m_hat / (jnp.sqrt(v_hat) + ADAM_EPS) + ADAM_WD * w_ref[...])
        mo_ref[...] = m_
        vo_ref[...] = v_

    blk = pl.BlockSpec((tr, n), lambda i: (i, 0))
    return pl.pallas_call(
        body, name=name, grid=(r // tr,), in_specs=[blk] * 4, out_specs=[blk] * 3,
        out_shape=[jax.ShapeDtypeStruct((r, n), f32)] * 3, compiler_params=_cp(("parallel",)))(w, g, m, v)


def _adamw_landed(w, m, v, land0, land1, *, tr, name):
    _, r, n = w.shape
    nrt = r // tr

    def body(w_ref, m_ref, v_ref, l0_ref, l1_ref, g_ref, d_ref, mo_ref, vo_ref):
        layer = pl.program_id(0)

        def total(ref):
            acc = ref[0].astype(f32)
            for k in range(1, N_DEV):
                acc = acc + ref[k].astype(f32)
            return acc

        @pl.when(layer == 0)
        def _():
            g_ref[...] = total(l0_ref)

        @pl.when(layer == 1)
        def _():
            g_ref[...] = total(l1_ref)
        g_ = g_ref[...]
        m_ = ADAM_B1 * m_ref[...] + (1.0 - ADAM_B1) * g_
        v_ = ADAM_B2 * v_ref[...] + (1.0 - ADAM_B2) * jnp.square(g_)
        m_hat = m_ / (1.0 - ADAM_B1 ** ADAM_STEP)
        v_hat = v_ / (1.0 - ADAM_B2 ** ADAM_STEP)
        d_ref[...] = -ADAM_LR * (m_hat / (jnp.sqrt(v_hat) + ADAM_EPS) + ADAM_WD * w_ref[...])
        mo_ref[...] = m_
        vo_ref[...] = v_

    blk = pl.BlockSpec((None, tr, n), lambda l, i: (l, i, 0))
    land0_spec = pl.BlockSpec((N_DEV, tr, n), lambda l, i: (0, jnp.where(l == 0, i, nrt - 1), 0))
    land1_spec = pl.BlockSpec((N_DEV, tr, n), lambda l, i: (0, jnp.where(l == 1, i, 0), 0))
    return pl.pallas_call(
        body, name=name, grid=(DEPTH, nrt), in_specs=[blk] * 3 + [land0_spec, land1_spec], out_specs=[blk] * 4,
        out_shape=[jax.ShapeDtypeStruct(w.shape, f32)] * 4,
        compiler_params=_cp(("arbitrary", "arbitrary")))(w, m, v, land0, land1)


WEIGHTS = ["ln_in_g", "ln_in_b", "ada_w", "ada_b", "ffn1_w_in", "ffn1_w_out", "mix_w_in", "mix_w_out", "ssd_conv_w",
           "ssd_conv_b", "ssd_dt_bias", "ssd_a_log", "ssd_d", "ssd_norm_g", "fox_f_bias", "sconv_w", "ffn2_w_in",
           "ffn2_w_out", "ln_g", "ln_b"]
BIG = ["ffn1_w_in", "ffn1_w_out", "ffn2_w_in", "ffn2_w_out", "mix_w_in", "mix_w_out"]
GROUPS = {"ffn1": ["ffn1_w_in", "ffn1_w_out"], "mix": ["mix_w_in", "mix_w_out"], "ffn2": ["ffn2_w_in", "ffn2_w_out"]}
COL_SHARDED = ("ffn1_w_in", "ffn2_w_in")
SMALL_SHARDED = {"ssd_conv_w": 128, "sconv_w": 32, "ln_g": 128, "ln_b": 128}
ADAM_TR = {"ada_w": 256, "ffn1_w_in": 512, "ffn2_w_in": 512, "ffn1_w_out": 352, "ffn2_w_out": 352, "mix_w_in": 64,
           "mix_w_out": 256}
LAND_TR = {"ffn1_w_in": 176, "ffn2_w_in": 176, "ffn1_w_out": 176, "ffn2_w_out": 176, "mix_w_in": 32, "mix_w_out": 128}


def _pad_rows(v, mult=128):
    v = v.reshape(-1)
    return jnp.pad(v, (0, (-v.shape[0]) % mult))


def _pack_rows(parts, row_mult=8):
    flat = [_pad_rows(p.astype(f32)) for p in parts]
    offs, o = [], 0
    for f in flat:
        offs.append(o)
        o += f.shape[0] // 128
    buf = jnp.concatenate(flat).reshape(-1, 128)
    return jnp.pad(buf, ((0, (-buf.shape[0]) % row_mult), (0, 0))), offs


def _take(buf, off, shape):
    n = 1
    for s in shape:
        n *= s
    rows = -(-n // 128)
    lead = buf.shape[:-2]
    flat = buf[..., off:off + rows, :].reshape(lead + (rows * 128,))
    return flat[..., :n].reshape(lead + tuple(shape))


def kernel(*args):
    names = (["x", "c"] + WEIGHTS + ["loss_target"] + ["m_" + n for n in WEIGHTS] + ["v_" + n for n in WEIGHTS])
    assert len(args) == len(names)
    a = dict(zip(names, args))
    xi, yi, ci = lax.axis_index("x"), lax.axis_index("y"), lax.axis_index("c")
    me = 4 * xi + 2 * yi + ci

    small_in = [a["c"], a["ln_g"], a["ln_b"], a["ssd_conv_w"], a["sconv_w"]]
    buf, offs = _pack_rows(small_in)
    got, = _all_gather([buf], in_vmem=True, name="gather_small")
    c_all = _take(got, offs[0], (D,))
    full = {}
    for k, n in enumerate(["ln_g", "ln_b", "ssd_conv_w", "sconv_w"]):
        sh = a[n].shape
        t = _take(got, offs[k + 1], sh)
        full[n] = jnp.transpose(t, (1, 2, 0, 3)).reshape(sh[0], sh[1], N_DEV * sh[2])

    ncol = a["ada_w"].shape[-1]
    ada_b_cols = lax.dynamic_slice_in_dim(a["ada_b"], me * ncol, ncol, axis=1)[:, None, :]
    mod_cols = _ada_fwd(c_all, a["ada_w"], ada_b_cols)
    got, = _all_gather([mod_cols.reshape(DEPTH * N_DEV, ncol)], in_vmem=True, name="gather_mod")
    got = got.reshape(N_DEV, DEPTH, N_DEV, ncol)
    mod = lax.dynamic_index_in_dim(got, me, axis=2, keepdims=False)
    mod = jnp.transpose(mod, (1, 0, 2)).reshape(DEPTH, 9, D)

    flip = lambda t: jnp.swapaxes(t, 1, 2)
    shards = {n: (flip(a[n]) if n in COL_SHARDED else a[n]) for n in BIG}
    shards["mix_w_in"] = _pack_cols(a["mix_w_in"])
    def as_weights(names, got):
        return {n: (t if n in COL_SHARDED else t.reshape(-1, t.shape[-1])) for n, t in zip(names, got)}

    def group_shards(l, gn):
        return [shards[n][l].astype(bf16) for n in GROUPS[gn]]

    w_first = _all_gather(group_shards(0, "ffn1"), in_vmem=False, name="gather_weights")
    w_state, after = {}, [w_first[0]]
    for l, gn in [(0, "mix"), (0, "ffn2"), (1, "ffn1"), (1, "mix"), (1, "ffn2")]:
        w_state[l, gn], token = _spread_start(group_shards(l, gn), scatter=False, after=after,
                                              name=f"weights_{l}{gn}_start")
        after = [token]

    def weights_of(l, gn, after):
        if (l, gn) == (0, "ffn1"):
            return as_weights(GROUPS[gn], w_first)
        return as_weights(GROUPS[gn], _spread_wait(w_state[l, gn], after, scatter=False, name=f"weights_{l}{gn}_wait"))

    g_state = {}

    last_send = {}

    def grads_done(l, gn, gw, after=()):
        if (l, gn) == (0, "ffn1_w_in") and not after:
            last_send.update(gw)
            return None
        srcs = [t.reshape((N_DEV,) + shards[n].shape[1:]) for n, t in gw.items()]
        g_state[l, gn], token = _spread_start(srcs, scatter=True, after=after, name=f"grads_{l}{gn}_start")
        return token

    p = {n: a[n] for n in ("ln_in_g", "ln_in_b", "ssd_conv_b", "ssd_dt_bias", "ssd_a_log", "ssd_d", "ssd_norm_g",
                           "fox_f_bias")}
    p.update(full)
    p["mod"] = mod + after[0][0, 0]

    loss_local, grad_x, g = _local_step(a["x"][0], a["loss_target"][0], p, weights_of, grads_done)
    loss = lax.psum(loss_local, ("x", "y", "c"))

    small_names = ["mod", "ln_in_g", "ln_in_b", "ssd_conv_b", "ssd_dt_bias", "ssd_a_log", "ssd_d", "ssd_norm_g",
                   "fox_f_bias", "ln_g", "ln_b", "ssd_conv_w", "sconv_w"]
    buf, offs = _pack_rows([g[n] for n in small_names])
    got, = _all_gather([buf], in_vmem=True, name="gather_small_grads")
    tot = _sum_slots(got, tr=buf.shape[0], name="sum_small_grads")
    grads_done(0, "ffn1_w_in", last_send, after=(tot,))
    grads = {}
    for k, n in enumerate(small_names[1:], start=1):
        t = _take(tot, offs[k], g[n].shape)
        if n in SMALL_SHARDED:
            w_ = SMALL_SHARDED[n]
            t = lax.dynamic_slice_in_dim(t, me * w_, w_, axis=2)
        grads[n] = t
    grads["ada_b"] = _take(tot, offs[0], (DEPTH, 9 * D))
    dmod_all = _take(got, offs[0], (DEPTH, 9 * D))
    dmod_cols = jnp.transpose(lax.dynamic_slice_in_dim(dmod_all, me * ncol, ncol, axis=2), (1, 0, 2))
    grads["ada_w"] = _ada_bwd(c_all, dmod_cols)

    delta, new_m, new_v = {}, {}, {}

    def adamw(n):
        sh = a[n].shape
        two = lambda t: t.reshape(-1, sh[-1])
        outs = _adamw(two(a[n]), two(grads[n]), two(a["m_" + n]), two(a["v_" + n]), tr=ADAM_TR[n], name="adamw_" + n)
        delta[n], new_m[n], new_v[n] = (t.reshape(sh) for t in outs)

    small_params = [n for n in WEIGHTS if n not in ADAM_TR]
    packs = [_pack_rows([src[pre + n] for n in small_params])[0]
             for src, pre in ((a, ""), (grads, ""), (a, "m_"), (a, "v_"))]
    _, offs = _pack_rows([a[n] for n in small_params])
    outs = _adamw(*packs, tr=packs[0].shape[0], name="adamw_small")
    for k, n in enumerate(small_params):
        delta[n], new_m[n], new_v[n] = (_take(t, offs[k], a[n].shape) for t in outs)
    adamw("ada_w")
    after = grad_x
    for gn, names in [("ffn2_w_out", ["ffn2_w_out"]), ("ffn2_w_in", ["ffn2_w_in"]), ("mix", GROUPS["mix"]),
                      ("ffn1_w_out", ["ffn1_w_out"]), ("ffn1_w_in", ["ffn1_w_in"])]:
        lands = [_spread_wait(g_state[l, gn], after if l == 0 else grad_x, scatter=True, name=f"grads_{l}{gn}_wait")
                 for l in range(DEPTH)]
        for k, n in enumerate(names):
            if n == "mix_w_in":
                sums = [_sum_slots(t[k], tr=LAND_TR[n], name="sum_devices") for t in lands]
                grads[n] = _unpack_cols(jnp.stack(sums))
                adamw(n)
            elif n in COL_SHARDED:
                grads[n], delta[n], new_m[n], new_v[n] = (flip(t) for t in _adamw_landed(
                    flip(a[n]), flip(a["m_" + n]), flip(a["v_" + n]), lands[0][k], lands[1][k], tr=LAND_TR[n],
                    name="adamw_" + n))
            else:
                grads[n], delta[n], new_m[n], new_v[n] = _adamw_landed(
                    a[n], a["m_" + n], a["v_" + n], lands[0][k], lands[1][k], tr=LAND_TR[n], name="adamw_" + n)
            after = delta[n]

    return (loss, grad_x[None], *[grads[n] for n in WEIGHTS], *[delta[n] for n in WEIGHTS],
            *[new_m[n] for n in WEIGHTS], *[new_v[n] for n in WEIGHTS])
```

```python
import functools

import jax
import jax.numpy as jnp
from jax import lax
from jax.experimental import pallas as pl
from jax.experimental.pallas import tpu as pltpu

f32, bf16 = jnp.float32, jnp.bfloat16

D = 1024
F = 2816
DEPTH = 2
N_DEV = 8
SSD_W, SSD_HD, SSD_H, SSD_G, SSD_N, SSD_K = 512, 64, 8, 2, 128, 4
FOX_W, FOX_HD, FOX_H = 256, 64, 4
SC_W, SC_K = 256, 3
ALPHA = (2 * DEPTH) ** 0.25
LN_EPS = 1e-5
RMS_EPS = 1e-5
P_XBC, P_Z, P_Q, P_K, P_V, P_SB, P_SC, P_SX, P_SM = 0, 1024, 1536, 1792, 2048, 2304, 2560, 2816, 3072
P_W = 3200
SM_DT, SM_F = 0, 8
ADAM_LR, ADAM_B1, ADAM_B2, ADAM_EPS, ADAM_WD, ADAM_STEP = 0.001, 0.9, 0.999, 1e-08, 0.01, 10

VMEM_LIMIT = 56 * 1024 * 1024


def _cp(sem=None):
    return pltpu.CompilerParams(dimension_semantics=sem, vmem_limit_bytes=VMEM_LIMIT)


def _const_spec(shape):
    nd = len(shape)
    return pl.BlockSpec(shape, lambda *_: (0,) * nd, pipeline_mode=pl.Buffered(1))


def _sigmoid(x):
    return 1.0 / (1.0 + jnp.exp(-x))


def _ln_fwd(u, g, b):
    mu = jnp.mean(u, -1, keepdims=True)
    xc = u - mu
    rstd = lax.rsqrt(jnp.mean(xc * xc, -1, keepdims=True) + LN_EPS)
    xhat = xc * rstd
    return xhat * g + b, xhat, rstd


def _ln_bwd(dout, xhat, rstd, g):
    dxh = dout * g
    m1 = jnp.mean(dxh, -1, keepdims=True)
    m2 = jnp.mean(dxh * xhat, -1, keepdims=True)
    du = rstd * (dxh - m1 - xhat * m2)
    return du, jnp.sum(dout * xhat, 0, keepdims=True), jnp.sum(dout, 0, keepdims=True)


def _dot(a, b):
    return jnp.dot(a, b, preferred_element_type=f32)


def _dot_nt(a, b):
    return lax.dot_general(a, b, (((1,), (1,)), ((), ())), preferred_element_type=f32)


def _dot_tn(a, b):
    return lax.dot_general(a, b, (((0,), (0,)), ((), ())), preferred_element_type=f32)


def _dot_hi(a, b):
    return jnp.dot(a, b, preferred_element_type=f32, precision=lax.Precision.HIGHEST)


def _shift_down(cur, prev, s):
    if s == 0:
        return cur
    row = lax.broadcasted_iota(jnp.int32, cur.shape, 0)
    return jnp.where(row < s, pltpu.roll(prev, s, 0), pltpu.roll(cur, s, 0))


def _shift_up(cur, nxt, s):
    if s == 0:
        return cur
    t = cur.shape[0]
    row = lax.broadcasted_iota(jnp.int32, cur.shape, 0)
    return jnp.where(row >= t - s, pltpu.roll(nxt, t - s, 0), pltpu.roll(cur, t - s, 0))


def _ln_in_fwd(x, gb, *, tt=512):
    L = x.shape[0]

    def body(x_ref, gb_ref, o_ref):
        o_ref[...] = _ln_fwd(x_ref[...], gb_ref[0:1, :], gb_ref[1:2, :])[0]

    return pl.pallas_call(
        body, name="ln_in_fwd", grid=(L // tt,),
        in_specs=[pl.BlockSpec((tt, D), lambda i: (i, 0)), _const_spec((8, D))],
        out_specs=pl.BlockSpec((tt, D), lambda i: (i, 0)),
        out_shape=jax.ShapeDtypeStruct((L, D), f32), compiler_params=_cp(("parallel",)))(x, gb)


def _ln_in_bwd(x, dy, gb, *, tt=512):
    L = x.shape[0]

    def body(x_ref, dy_ref, gb_ref, dx_ref, acc_ref):
        @pl.when(pl.program_id(0) == 0)
        def _():
            acc_ref[...] = jnp.zeros_like(acc_ref)
        _, xhat, rstd = _ln_fwd(x_ref[...], gb_ref[0:1, :], gb_ref[1:2, :])
        du, dg, db = _ln_bwd(dy_ref[...], xhat, rstd, gb_ref[0:1, :])
        dx_ref[...] = du
        acc_ref[0:1, :] += dg
        acc_ref[1:2, :] += db

    return pl.pallas_call(
        body, name="ln_in_bwd", grid=(L // tt,),
        in_specs=[pl.BlockSpec((tt, D), lambda i: (i, 0)), pl.BlockSpec((tt, D), lambda i: (i, 0)), _const_spec((8, D))],
        out_specs=[pl.BlockSpec((tt, D), lambda i: (i, 0)), pl.BlockSpec((8, D), lambda i: (0, 0))],
        out_shape=[jax.ShapeDtypeStruct((L, D), f32), jax.ShapeDtypeStruct((8, D), f32)],
        compiler_params=_cp(("arbitrary",)))(x, dy, gb)


def _loss_head(y, tgt, *, tt=512):
    L = y.shape[0]

    def body(y_ref, t_ref, dy_ref, acc_ref):
        @pl.when(pl.program_id(0) == 0)
        def _():
            acc_ref[...] = jnp.zeros_like(acc_ref)
        e = y_ref[...] - t_ref[...]
        dy_ref[...] = e * (1.0 / D)
        acc_ref[...] += 0.5 * jnp.sum(jnp.mean(e * e, -1, keepdims=True))

    return pl.pallas_call(
        body, name="loss_head", grid=(L // tt,),
        in_specs=[pl.BlockSpec((tt, D), lambda i: (i, 0)), pl.BlockSpec((tt, D), lambda i: (i, 0))],
        out_specs=[pl.BlockSpec((tt, D), lambda i: (i, 0)), pl.BlockSpec((8, 128), lambda i: (0, 0))],
        out_shape=[jax.ShapeDtypeStruct((L, D), f32), jax.ShapeDtypeStruct((8, 128), f32)],
        compiler_params=_cp(("arbitrary",)))(y, tgt)


FFN_CH = 4
FS = F // FFN_CH


def _ffn_fwd(x, mv, w_in, w_out, *, tt=256):
    L = x.shape[0]

    def body(x_ref, mv_ref, wi_ref, wo_ref, o_ref):
        x = x_ref[...]
        h = (x * (1.0 + mv_ref[1:2, :]) + mv_ref[0:1, :]).astype(bf16)
        gs = [_dot_nt(h, wi_ref[c]) for c in range(FFN_CH)]
        us = [_dot_nt(h, wi_ref[c + FFN_CH]) for c in range(FFN_CH)]
        acts = [(g * _sigmoid(g) * u).astype(bf16) for g, u in zip(gs, us)]
        y = _dot(acts[0], wo_ref[0:FS, :])
        for c in range(1, FFN_CH):
            y = y + _dot(acts[c], wo_ref[c * FS:(c + 1) * FS, :])
        uu = ALPHA * x + (0.5 * mv_ref[2:3, :]) * y
        o_ref[...] = _ln_fwd(uu, mv_ref[3:4, :], mv_ref[4:5, :])[0]

    return pl.pallas_call(
        body, name="ffn_fwd", grid=(L // tt,),
        in_specs=[pl.BlockSpec((tt, D), lambda i: (i, 0)), _const_spec((8, D)),
                  _const_spec((2 * FFN_CH, FS, D)), _const_spec((F, D))],
        out_specs=pl.BlockSpec((tt, D), lambda i: (i, 0)),
        out_shape=jax.ShapeDtypeStruct((L, D), f32), compiler_params=_cp(("parallel",)))(x, mv, w_in, w_out)


def _ffn_bwd(x, dxo, mv, w_in, w_out, *, tt=256):
    L = x.shape[0]

    def body(x_ref, dxo_ref, mv_ref, wi_ref, wo_ref, dx_ref, h_ref, da_ref, act_ref, dy_ref, acc_ref, a_scr):
        @pl.when(pl.program_id(0) == 0)
        def _():
            acc_ref[...] = jnp.zeros_like(acc_ref)
        x = x_ref[...]
        scale1 = 1.0 + mv_ref[1:2, :]
        h = (x * scale1 + mv_ref[0:1, :]).astype(bf16)
        h_ref[...] = h
        y = jnp.zeros((tt, D), f32)
        for c in range(FFN_CH):
            g = _dot_nt(h, wi_ref[c])
            u = _dot_nt(h, wi_ref[c + FFN_CH])
            a_scr[c] = g
            a_scr[c + FFN_CH] = u
            act = (g * _sigmoid(g) * u).astype(bf16)
            act_ref[c] = act.T
            y = y + _dot(act, wo_ref[c * FS:(c + 1) * FS, :])
        hg = 0.5 * mv_ref[2:3, :]
        _, xhat, rstd = _ln_fwd(ALPHA * x + hg * y, mv_ref[3:4, :], mv_ref[4:5, :])
        du, dlg, dlb = _ln_bwd(dxo_ref[...], xhat, rstd, mv_ref[3:4, :])
        acc_ref[3:4, :] += dlg
        acc_ref[4:5, :] += dlb
        acc_ref[2:3, :] += jnp.sum(0.5 * y * du, 0, keepdims=True)
        dyb = (hg * du).astype(bf16)
        dy_ref[...] = dyb
        dh = jnp.zeros((tt, D), f32)
        for c in range(FFN_CH):
            g = a_scr[c]
            u = a_scr[c + FFN_CH]
            dact = _dot_nt(dyb, wo_ref[c * FS:(c + 1) * FS, :])
            s = _sigmoid(g)
            dg = (dact * u * (s * (1.0 + g * (1.0 - s)))).astype(bf16)
            dup = (dact * (g * s)).astype(bf16)
            da_ref[c] = dg.T
            da_ref[c + FFN_CH] = dup.T
            dh = dh + _dot(dg, wi_ref[c])
            dh = dh + _dot(dup, wi_ref[c + FFN_CH])
        dx_ref[...] = ALPHA * du + dh * scale1
        acc_ref[0:1, :] += jnp.sum(dh, 0, keepdims=True)
        acc_ref[1:2, :] += jnp.sum(dh * x, 0, keepdims=True)

    tok = lambda w: pl.BlockSpec((tt, w), lambda i: (i, 0))
    by_chunk = lambda n: pl.BlockSpec((n, FS, tt), lambda i: (0, 0, i))
    return pl.pallas_call(
        body, name="ffn_bwd", grid=(L // tt,),
        in_specs=[tok(D), tok(D), _const_spec((8, D)), _const_spec((2 * FFN_CH, FS, D)), _const_spec((F, D))],
        out_specs=[tok(D), tok(D), by_chunk(2 * FFN_CH), by_chunk(FFN_CH), tok(D), pl.BlockSpec((8, D), lambda i: (0, 0))],
        out_shape=[jax.ShapeDtypeStruct((L, D), f32), jax.ShapeDtypeStruct((L, D), bf16),
                   jax.ShapeDtypeStruct((2 * FFN_CH, FS, L), bf16), jax.ShapeDtypeStruct((FFN_CH, FS, L), bf16),
                   jax.ShapeDtypeStruct((L, D), bf16), jax.ShapeDtypeStruct((8, D), f32)],
        scratch_shapes=[pltpu.VMEM((2 * FFN_CH, tt, FS), f32)],
        compiler_params=_cp(("arbitrary",)))(x, dxo, mv, w_in, w_out)


DW_TK = 4096


def _matmul_tokens(a, b, *, tn, tk, name, after=None):
    ga, gb = a.ndim == 3, b.ndim == 3
    extra = [] if after is None else [after]
    G = a.shape[0] if ga else (b.shape[0] if gb else 1)
    M, K = a.shape[-2:]
    N = b.shape[-1]
    tk = min(tk, K)
    nk = K // tk

    def body(a_ref, b_ref, *rest):
        o_ref, acc = rest[len(extra):]
        k = pl.program_id(2)
        p = _dot(a_ref[...], b_ref[...])

        @pl.when(k == 0)
        def _():
            acc[...] = p

        @pl.when(k > 0)
        def _():
            acc[...] += p

        @pl.when(k == nk - 1)
        def _():
            o_ref[...] = acc[...].astype(bf16)

    a_spec = (pl.BlockSpec((None, M, tk), lambda g, j, k: (g, 0, k)) if ga
              else pl.BlockSpec((M, tk), lambda g, j, k: (0, k)))
    b_spec = (pl.BlockSpec((None, tk, tn), lambda g, j, k: (g, k, j)) if gb
              else pl.BlockSpec((tk, tn), lambda g, j, k: (k, j)))
    if ga or gb:
        o_spec, o_shape = pl.BlockSpec((None, M, tn), lambda g, j, k: (g, 0, j)), (G, M, N)
    else:
        o_spec, o_shape = pl.BlockSpec((M, tn), lambda g, j, k: (0, j)), (M, N)
    return pl.pallas_call(
        body, name=name, grid=(G, N // tn, nk), in_specs=[a_spec, b_spec] + [ANY] * len(extra), out_specs=o_spec,
        out_shape=jax.ShapeDtypeStruct(o_shape, bf16), scratch_shapes=[pltpu.VMEM((M, tn), f32)],
        compiler_params=_cp(("parallel", "parallel", "arbitrary")))(a, b, *extra)


def _inproj_fwd(x, mv, w, *, tt=512):
    L = x.shape[0]

    def body(x_ref, mv_ref, w_ref, o_ref):
        h = (x_ref[...] * (1.0 + mv_ref[1:2, :]) + mv_ref[0:1, :]).astype(bf16)
        o_ref[...] = _dot(h, w_ref[...])

    return pl.pallas_call(
        body, name="inproj_fwd", grid=(L // tt,),
        in_specs=[pl.BlockSpec((tt, D), lambda i: (i, 0)), _const_spec((8, D)), _const_spec((D, P_W))],
        out_specs=pl.BlockSpec((tt, P_W), lambda i: (i, 0)),
        out_shape=jax.ShapeDtypeStruct((L, P_W), f32), compiler_params=_cp(("parallel",)))(x, mv, w)


def _inproj_bwd(x, dx_part, dproj, mv, w, *, tt=512):
    L = x.shape[0]

    def body(x_ref, dxp_ref, dp_ref, mv_ref, w_ref, dx_ref, h_ref, acc_ref):
        @pl.when(pl.program_id(0) == 0)
        def _():
            acc_ref[...] = jnp.zeros_like(acc_ref)
        x = x_ref[...]
        scale1 = 1.0 + mv_ref[1:2, :]
        h_ref[...] = (x * scale1 + mv_ref[0:1, :]).astype(bf16).T
        dh = _dot_nt(dp_ref[...], w_ref[...])
        dx_ref[...] = dxp_ref[...] + dh * scale1
        acc_ref[0:1, :] += jnp.sum(dh, 0, keepdims=True)
        acc_ref[1:2, :] += jnp.sum(dh * x, 0, keepdims=True)

    tok = lambda w_: pl.BlockSpec((tt, w_), lambda i: (i, 0))
    return pl.pallas_call(
        body, name="inproj_bwd", grid=(L // tt,),
        in_specs=[tok(D), tok(D), tok(P_W), _const_spec((8, D)), _const_spec((D, P_W))],
        out_specs=[tok(D), pl.BlockSpec((D, tt), lambda i: (0, i)), pl.BlockSpec((8, D), lambda i: (0, 0))],
        out_shape=[jax.ShapeDtypeStruct((L, D), f32), jax.ShapeDtypeStruct((D, L), bf16),
                   jax.ShapeDtypeStruct((8, D), f32)],
        compiler_params=_cp(("arbitrary",)))(x, dx_part, dproj, mv, w)


def _outproj_fwd(x, ycat, mv, w, *, tt=512):
    L = x.shape[0]

    def body(x_ref, y_ref, mv_ref, w_ref, o_ref):
        y = _dot(y_ref[...], w_ref[...])
        uu = ALPHA * x_ref[...] + mv_ref[2:3, :] * y
        o_ref[...] = _ln_fwd(uu, mv_ref[3:4, :], mv_ref[4:5, :])[0]

    tok = lambda w_: pl.BlockSpec((tt, w_), lambda i: (i, 0))
    return pl.pallas_call(
        body, name="outproj_fwd", grid=(L // tt,),
        in_specs=[tok(D), tok(D), _const_spec((8, D)), _const_spec((D, D))],
        out_specs=tok(D),
        out_shape=jax.ShapeDtypeStruct((L, D), f32), compiler_params=_cp(("parallel",)))(x, ycat, mv, w)


def _outproj_bwd(x, ycat, dxo, mv, w, *, tt=512):
    L = x.shape[0]

    def body(x_ref, y_ref, dxo_ref, mv_ref, w_ref, dx_ref, dy_ref, dyc_ref, yt_ref, acc_ref):
        @pl.when(pl.program_id(0) == 0)
        def _():
            acc_ref[...] = jnp.zeros_like(acc_ref)
        yt_ref[...] = y_ref[...].T
        y = _dot(y_ref[...], w_ref[...])
        gate = mv_ref[2:3, :]
        _, xhat, rstd = _ln_fwd(ALPHA * x_ref[...] + gate * y, mv_ref[3:4, :], mv_ref[4:5, :])
        du, dlg, dlb = _ln_bwd(dxo_ref[...], xhat, rstd, mv_ref[3:4, :])
        acc_ref[3:4, :] += dlg
        acc_ref[4:5, :] += dlb
        acc_ref[2:3, :] += jnp.sum(y * du, 0, keepdims=True)
        dx_ref[...] = ALPHA * du
        dyb = (gate * du).astype(bf16)
        dy_ref[...] = dyb
        dyc_ref[...] = _dot_nt(dyb, w_ref[...])

    tok = lambda w_: pl.BlockSpec((tt, w_), lambda i: (i, 0))
    return pl.pallas_call(
        body, name="outproj_bwd", grid=(L // tt,),
        in_specs=[tok(D), tok(D), tok(D), _const_spec((8, D)), _const_spec((D, D))],
        out_specs=[tok(D), tok(D), tok(D), pl.BlockSpec((D, tt), lambda i: (0, i)), pl.BlockSpec((8, D), lambda i: (0, 0))],
        out_shape=[jax.ShapeDtypeStruct((L, D), f32), jax.ShapeDtypeStruct((L, D), bf16),
                   jax.ShapeDtypeStruct((L, D), f32), jax.ShapeDtypeStruct((D, L), bf16),
                   jax.ShapeDtypeStruct((8, D), f32)],
        compiler_params=_cp(("arbitrary",)))(x, ycat, dxo, mv, w)


def _sconv_fwd(proj, w, *, tt=512):
    L = proj.shape[0]
    cb = SC_W

    def body(b_ref, c_ref, x_ref, cp_ref, xp_ref, w_ref, o_ref):
        first = jnp.where(pl.program_id(0) > 0, 1.0, 0.0)
        u = c_ref[...] * x_ref[...]
        up = cp_ref[...] * xp_ref[...] * first
        v = w_ref[2:3, :] * u + w_ref[1:2, :] * _shift_down(u, up, 1) + w_ref[0:1, :] * _shift_down(u, up, 2)
        o_ref[...] = (b_ref[...] * v).astype(bf16)

    cur = lambda col: pl.BlockSpec((tt, cb), lambda i: (i, col // cb))
    prev = lambda col: pl.BlockSpec((tt, cb), lambda i: (jnp.maximum(i - 1, 0), col // cb))
    return pl.pallas_call(
        body, name="sconv_fwd", grid=(L // tt,),
        in_specs=[cur(P_SB), cur(P_SC), cur(P_SX), prev(P_SC), prev(P_SX), _const_spec((8, cb))],
        out_specs=pl.BlockSpec((tt, cb), lambda i: (i, 0)),
        out_shape=jax.ShapeDtypeStruct((L, cb), bf16), compiler_params=_cp(("parallel",)))(proj, proj, proj, proj, proj, w)


def _sconv_bwd(proj, dycat, w, *, tt=512):
    L = proj.shape[0]
    cb = SC_W
    n = L // tt

    def body(b_ref, c_ref, x_ref, cp_ref, xp_ref, bn_ref, dy_ref, dyn_ref, w_ref, db_ref, dc_ref, dx_ref, acc_ref):
        i = pl.program_id(0)

        @pl.when(i == 0)
        def _():
            acc_ref[...] = jnp.zeros_like(acc_ref)
        first = jnp.where(i > 0, 1.0, 0.0)
        last = jnp.where(i < n - 1, 1.0, 0.0)
        cg, xin, bg = c_ref[...], x_ref[...], b_ref[...]
        u = cg * xin
        up = cp_ref[...] * xp_ref[...] * first
        u1, u2 = _shift_down(u, up, 1), _shift_down(u, up, 2)
        v = w_ref[2:3, :] * u + w_ref[1:2, :] * u1 + w_ref[0:1, :] * u2
        dy = dy_ref[...]
        db_ref[...] = (dy * v).astype(bf16)
        dv = dy * bg
        dvn = dyn_ref[...] * bn_ref[...] * last
        du = w_ref[2:3, :] * dv + w_ref[1:2, :] * _shift_up(dv, dvn, 1) + w_ref[0:1, :] * _shift_up(dv, dvn, 2)
        acc_ref[2:3, :] += jnp.sum(dv * u, 0, keepdims=True)
        acc_ref[1:2, :] += jnp.sum(dv * u1, 0, keepdims=True)
        acc_ref[0:1, :] += jnp.sum(dv * u2, 0, keepdims=True)
        dc_ref[...] = (du * xin).astype(bf16)
        dx_ref[...] = (du * cg).astype(bf16)

    cur = lambda col: pl.BlockSpec((tt, cb), lambda i: (i, col // cb))
    prev = lambda col: pl.BlockSpec((tt, cb), lambda i: (jnp.maximum(i - 1, 0), col // cb))
    nxt = lambda col: pl.BlockSpec((tt, cb), lambda i: (jnp.minimum(i + 1, n - 1), col // cb))
    ycol = SSD_W + FOX_W
    out = pl.BlockSpec((tt, cb), lambda i: (i, 0))
    return pl.pallas_call(
        body, name="sconv_bwd", grid=(n,),
        in_specs=[cur(P_SB), cur(P_SC), cur(P_SX), prev(P_SC), prev(P_SX), nxt(P_SB), cur(ycol), nxt(ycol),
                  _const_spec((8, cb))],
        out_specs=[out, out, out, pl.BlockSpec((8, cb), lambda i: (0, 0))],
        out_shape=[jax.ShapeDtypeStruct((L, cb), bf16)] * 3 + [jax.ShapeDtypeStruct((8, cb), f32)],
        compiler_params=_cp(("arbitrary",)))(proj, proj, proj, proj, proj, proj, dycat, dycat, w)


def _log1pexp(x):
    return jnp.log(1.0 + jnp.exp(-jnp.abs(x)))


NEG = -1e30
FOX_SCALE = FOX_HD ** -0.5


HL = 128
AW = FOX_H * HL


def _np_place(rows, cols, pairs, dtype):
    import numpy as np
    m = np.zeros((rows, cols), np.float32)
    for r, c in pairs:
        m[r, c] = 1.0
    return jnp.asarray(m, dtype)


def _fox_consts():
    data = [(h * FOX_HD + d, h * HL + d) for h in range(FOX_H) for d in range(FOX_HD)]
    return dict(
        pq=_np_place(FOX_W, AW, data, bf16),
        pqt=_np_place(AW, FOX_W, [(c, r) for r, c in data], bf16),
        cum_a=[_np_place(128, AW, [(SM_F + h, h * HL + 64 + r) for h in range(FOX_H)], bf16) for r in range(3)],
        head_a=[_np_place(128, AW, [(h, h * HL + 64 + r) for h in range(FOX_H)], bf16) for r in range(3)],
        head_b=[_np_place(128, AW, [(h, h * HL + 67 + r) for h in range(FOX_H)], bf16) for r in range(3)],
        group=_np_place(FOX_W, 128, [(h * FOX_HD + d, h) for h in range(FOX_H) for d in range(FOX_HD)], f32),
        col_a=_np_place(AW, 128, [(h * HL + 64, SM_F + h) for h in range(FOX_H)], f32))


def _split3(x):
    hi = x.astype(bf16)
    r1 = x - hi.astype(f32)
    mid = r1.astype(bf16)
    return hi, mid, (r1 - mid.astype(f32)).astype(bf16)


def _slot_ones(tt, first):
    lane = lax.broadcasted_iota(jnp.int32, (tt, AW), 1) % HL
    return jnp.where((lane >= first) & (lane < first + 3), 1.0, 0.0)


def _fox_prep(proj, hp, cst, *, tt=256):
    L = proj.shape[0]

    def body(q_ref, k_ref, v_ref, sm_ref, hp_ref, pq_ref, c0_ref, c1_ref, c2_ref, qa_ref, ka_ref, va_ref, tot_ref):
        xx = sm_ref[...] + hp_ref[3:4, :]
        logf = jnp.minimum(xx, 0.0) - _log1pexp(xx)
        r = lax.broadcasted_iota(jnp.int32, (tt, tt), 0)
        c = lax.broadcasted_iota(jnp.int32, (tt, tt), 1)
        cum = _dot_hi(jnp.where(r >= c, 1.0, 0.0), logf)
        tot_ref[...] = jnp.broadcast_to(cum[tt - 1:tt, :], (8, 128))
        parts = _split3(-cum)
        pq = pq_ref[...]
        a_ones, b_ones = _slot_ones(tt, 64), _slot_ones(tt, 67)
        qa_ref[...] = (_dot((q_ref[...] * FOX_SCALE).astype(bf16), pq) + a_ones).astype(bf16)
        ka = _dot(k_ref[...].astype(bf16), pq) + b_ones
        for part, c_ref in zip(parts, (c0_ref, c1_ref, c2_ref)):
            ka = ka + _dot(part, c_ref[...])
        ka_ref[...] = ka.astype(bf16)
        va_ref[...] = (_dot(v_ref[...].astype(bf16), pq) + a_ones).astype(bf16)

    col = lambda c_: pl.BlockSpec((tt, FOX_W), lambda i: (i, c_ // FOX_W))
    out = pl.BlockSpec((tt, AW), lambda i: (i, 0))
    return pl.pallas_call(
        body, name="fox_prep", grid=(L // tt,),
        in_specs=[col(P_Q), col(P_K), col(P_V), pl.BlockSpec((tt, 128), lambda i: (i, P_SM // 128)),
                  _const_spec((8, 128)), _const_spec((FOX_W, AW))] + [_const_spec((128, AW))] * 3,
        out_specs=[out, out, out, pl.BlockSpec((8, 128), lambda i: (i, 0))],
        out_shape=[jax.ShapeDtypeStruct((L, AW), bf16)] * 3 + [jax.ShapeDtypeStruct((8 * (L // tt), 128), f32)],
        compiler_params=_cp(("parallel",)))(proj, proj, proj, proj, hp, cst["pq"], *cst["cum_a"])


def _fox_attn_fwd(qa, ka, va, tot, *, tq=256):
    L = qa.shape[0]

    def body(qa_ref, ka_ref, va_ref, tot_ref, o_ref, oa_ref, lse_ref):
        i = pl.program_id(0)
        row = lax.broadcasted_iota(jnp.int32, (tq, tq), 0)
        col = lax.broadcasted_iota(jnp.int32, (tq, tq), 1)
        diag_bias = jnp.where(row >= col, 0.0, NEG)

        def block(j, carry, t_j, bias):
            r0 = pl.multiple_of(j * tq, tq)
            hls = [slice(h * HL, (h + 1) * HL) for h in range(FOX_H)]
            ss = [_dot_nt(qa_ref[:, hl], ka_ref[pl.ds(r0, tq), hl]) for hl in hls]
            if bias is not None:
                ss = [s + bias for s in ss]
            m_ins = [carry[h][0] - t_j[h] for h in range(FOX_H)]
            m_news = [jnp.maximum(m_in, jnp.max(s, -1, keepdims=True)) for m_in, s in zip(m_ins, ss)]
            ps = [jnp.exp(s - m_new) for s, m_new in zip(ss, m_news)]
            pbs = [p.astype(bf16) for p in ps]
            p_los = [(p - pb.astype(f32)).astype(bf16) for p, pb in zip(ps, pbs)]
            pvs = [_dot(pb, va_ref[pl.ds(r0, tq), hl]) + _dot(p_lo, va_ref[pl.ds(r0, tq), hl])
                   for pb, p_lo, hl in zip(pbs, p_los, hls)]
            return tuple((m_new, jnp.exp(m_in - m_new) * carry[h][1] + pv)
                         for h, (m_in, m_new, pv) in enumerate(zip(m_ins, m_news, pvs)))

        def step(k, state):
            carry, gap = state
            j = i - 1 - k
            t_j = [tot_ref[j, h] for h in range(FOX_H)]
            return block(j, carry, t_j, None), tuple(g + t for g, t in zip(gap, t_j))

        init = tuple((jnp.full((tq, 1), NEG, f32), jnp.zeros((tq, HL), f32)) for _ in range(FOX_H))
        zero_gap = tuple(jnp.zeros((), f32) for _ in range(FOX_H))
        carry = block(i, init, zero_gap, diag_bias)
        carry, gap = lax.fori_loop(0, i, step, (carry, zero_gap))
        lane = lax.broadcasted_iota(jnp.int32, (tq, 128), 1)
        lse_all = jnp.zeros((tq, 128), f32)
        for h in range(FOX_H):
            hs = slice(h * FOX_HD, (h + 1) * FOX_HD)
            m, acc = carry[h]
            l = acc[:, FOX_HD:FOX_HD + 1]
            o = acc[:, :FOX_HD] * (1.0 / l)
            o_ref[:, hs] = o.astype(bf16)
            oa_ref[:, hs] = o
            lse_all = jnp.where(lane == h, m + gap[h] + jnp.log(l), lse_all)
        lse_ref[...] = lse_all

    full = pl.BlockSpec((L, AW), lambda i: (0, 0), pipeline_mode=pl.Buffered(1))
    return pl.pallas_call(
        body, name="fox_fwd", grid=(L // tq,),
        in_specs=[pl.BlockSpec((tq, AW), lambda i: (i, 0)), full, full, pl.BlockSpec(memory_space=pltpu.SMEM)],
        out_specs=[pl.BlockSpec((tq, FOX_W), lambda i: (i, 0)), pl.BlockSpec((tq, FOX_W), lambda i: (i, 0)),
                   pl.BlockSpec((tq, 128), lambda i: (i, 0))],
        out_shape=[jax.ShapeDtypeStruct((L, FOX_W), bf16), jax.ShapeDtypeStruct((L, FOX_W), f32),
                   jax.ShapeDtypeStruct((L, 128), f32)],
        compiler_params=_cp(("parallel",)))(qa, ka, va, tot)


def _fox_bprep(qa, dycat, o_acc, lse, cst, *, tt=256):
    L = qa.shape[0]

    def body(qa_ref, do_ref, oa_ref, lse_ref, pq_ref, g_ref, a0, a1, a2, b0, b1, b2, qb_ref, doa_ref, qbt_ref, doat_ref):
        dob = do_ref[...].astype(bf16)
        delta = _dot_hi(dob.astype(f32) * oa_ref[...], g_ref[...])
        doa = _dot(dob, pq_ref[...])
        for part, ref in zip(_split3(-delta), (a0, a1, a2)):
            doa = doa + _dot(part, ref[...])
        qb = qa_ref[...].astype(f32)
        for part, ref in zip(_split3(-lse_ref[...]), (b0, b1, b2)):
            qb = qb + _dot(part, ref[...])
        doa, qb = doa.astype(bf16), qb.astype(bf16)
        doa_ref[...] = doa
        qb_ref[...] = qb
        doat_ref[...] = doa.T
        qbt_ref[...] = qb.T

    tok = lambda w_: pl.BlockSpec((tt, w_), lambda i: (i, 0))
    tr = pl.BlockSpec((AW, tt), lambda i: (0, i))
    return pl.pallas_call(
        body, name="fox_bprep", grid=(L // tt,),
        in_specs=[tok(AW), pl.BlockSpec((tt, FOX_W), lambda i: (i, SSD_W // FOX_W)), tok(FOX_W), tok(128),
                  _const_spec((FOX_W, AW)), _const_spec((FOX_W, 128))] + [_const_spec((128, AW))] * 6,
        out_specs=[tok(AW), tok(AW), tr, tr],
        out_shape=[jax.ShapeDtypeStruct((L, AW), bf16)] * 2 + [jax.ShapeDtypeStruct((AW, L), bf16)] * 2,
        compiler_params=_cp(("parallel",)))(qa, dycat, o_acc, lse, cst["pq"], cst["group"], *cst["head_a"], *cst["head_b"])


def _fox_attn_bwd(ka, va, tot, qb, doa, qbt, doat, *, tq=256):
    L = ka.shape[0]
    nq = L // tq

    def body(ka_ref, va_ref, tot_ref, qb_ref, doa_ref, qbt_ref, doat_ref, dq_ref, dkt_ref, dvt_ref):
        j = pl.program_id(0)

        @pl.when(j == 0)
        def _():
            dq_ref[...] = jnp.zeros_like(dq_ref)
        row = lax.broadcasted_iota(jnp.int32, (tq, tq), 0)
        col = lax.broadcasted_iota(jnp.int32, (tq, tq), 1)
        diag_bias = jnp.where(row >= col, 0.0, NEG)

        def block(i, carry, gap, bias):
            r0 = pl.multiple_of(i * tq, tq)
            rows = pl.ds(r0, tq)
            hls = [slice(h * HL, (h + 1) * HL) for h in range(FOX_H)]
            ss = [_dot_nt(qb_ref[rows, hl], ka_ref[:, hl]) + gap[h] for h, hl in enumerate(hls)]
            if bias is not None:
                ss = [s + bias for s in ss]
            ps = [jnp.exp(s) for s in ss]
            dss = [p * _dot_nt(doa_ref[rows, hl], va_ref[:, hl]) for p, hl in zip(ps, hls)]
            dsbs = [ds.astype(bf16) for ds in dss]
            for dsb, hl in zip(dsbs, hls):
                dq_ref[rows, hl] += _dot(dsb, ka_ref[:, hl])
            return tuple((carry[h][0] + _dot(qbt_ref[hl, rows], dsbs[h]),
                          carry[h][1] + _dot(doat_ref[hl, rows], ps[h].astype(bf16)),
                          carry[h][2] + jnp.sum(dss[h], 0, keepdims=True)) for h, hl in enumerate(hls))

        init = tuple((jnp.zeros((HL, tq), f32), jnp.zeros((HL, tq), f32), jnp.zeros((1, tq), f32))
                     for _ in range(FOX_H))
        def step(i, state):
            carry, gap = state
            gap = tuple(g + tot_ref[i - 1, h] for h, g in enumerate(gap))
            return block(i, carry, gap, None), gap

        zero_gap = tuple(jnp.zeros((), f32) for _ in range(FOX_H))
        carry = block(j, init, zero_gap, diag_bias)
        carry, _ = lax.fori_loop(j + 1, nq, step, (carry, zero_gap))
        for h in range(FOX_H):
            hl = slice(h * HL, (h + 1) * HL)
            dkt_ref[hl, :] = carry[h][0]
            dvt_ref[hl, :] = carry[h][1]
            dkt_ref[h * HL + FOX_HD:h * HL + FOX_HD + 1, :] = carry[h][2]

    full = lambda shape: pl.BlockSpec(shape, lambda j: (0, 0), pipeline_mode=pl.Buffered(1))
    blk = pl.BlockSpec((tq, AW), lambda j: (j, 0))
    trb = pl.BlockSpec((AW, tq), lambda j: (0, j))
    return pl.pallas_call(
        body, name="fox_bwd", grid=(nq,),
        in_specs=[blk, blk, pl.BlockSpec(memory_space=pltpu.SMEM), full((L, AW)), full((L, AW)), full((AW, L)),
                  full((AW, L))],
        out_specs=[pl.BlockSpec((L, AW), lambda j: (0, 0)), trb, trb],
        out_shape=[jax.ShapeDtypeStruct((L, AW), f32), jax.ShapeDtypeStruct((AW, L), f32),
                   jax.ShapeDtypeStruct((AW, L), f32)],
        compiler_params=_cp(("arbitrary",)))(ka, va, tot, qb, doa, qbt, doat)


def _fox_post(dq, dkt, dvt, proj, ddt, hp, cst, *, tt=256):
    L = proj.shape[0]
    n = L // tt

    def body(dq_ref, dkt_ref, dvt_ref, sm_ref, ddt_ref, hp_ref, pqt_ref, ca_ref,
             dqo_ref, dko_ref, dvo_ref, dsm_ref, acc_ref, carry):
        @pl.when(pl.program_id(0) == 0)
        def _():
            carry[...] = jnp.zeros_like(carry)
            acc_ref[...] = jnp.zeros_like(acc_ref)
        pqt = pqt_ref[...]
        dk_full = dkt_ref[...].T
        dqo_ref[...] = _dot((dq_ref[...] * FOX_SCALE).astype(bf16), pqt).astype(bf16)
        dko_ref[...] = _dot(dk_full.astype(bf16), pqt).astype(bf16)
        dvo_ref[...] = _dot(dvt_ref[...].T.astype(bf16), pqt).astype(bf16)
        dc = -_dot_hi(dk_full, ca_ref[...])
        r = lax.broadcasted_iota(jnp.int32, (tt, tt), 0)
        c = lax.broadcasted_iota(jnp.int32, (tt, tt), 1)
        dl = _dot_hi(jnp.where(r <= c, 1.0, 0.0), dc) + carry[0:1, :]
        carry[0:1, :] += jnp.sum(dc, 0, keepdims=True)
        xx = sm_ref[...] + hp_ref[3:4, :]
        lane = lax.broadcasted_iota(jnp.int32, (tt, 128), 1)
        dlogit = jnp.where((lane >= SM_F) & (lane < SM_F + FOX_H), dl * _sigmoid(-xx), 0.0)
        acc_ref[3:4, :] += jnp.sum(dlogit, 0, keepdims=True)
        dsm_ref[...] = (dlogit + ddt_ref[...]).astype(bf16)

    rev = lambda w_: pl.BlockSpec((tt, w_), lambda i: (n - 1 - i, 0))
    revt = pl.BlockSpec((AW, tt), lambda i: (0, n - 1 - i))
    return pl.pallas_call(
        body, name="fox_post", grid=(n,),
        in_specs=[rev(AW), revt, revt, pl.BlockSpec((tt, 128), lambda i: (n - 1 - i, P_SM // 128)), rev(128),
                  _const_spec((8, 128)), _const_spec((AW, FOX_W)), _const_spec((AW, 128))],
        out_specs=[rev(FOX_W), rev(FOX_W), rev(FOX_W), rev(128), pl.BlockSpec((8, 128), lambda i: (0, 0))],
        out_shape=[jax.ShapeDtypeStruct((L, FOX_W), bf16)] * 3 + [jax.ShapeDtypeStruct((L, 128), bf16),
                                                                   jax.ShapeDtypeStruct((8, 128), f32)],
        scratch_shapes=[pltpu.VMEM((8, 128), f32)],
        compiler_params=_cp(("arbitrary",)))(dq, dkt, dvt, proj, ddt, hp, cst["pqt"], cst["col_a"])


SSD_GW = SSD_W // SSD_G
SSD_HPG = SSD_H // SSD_G


def _ssd_pre(x, xprev, sm, cp_ref, hp_ref, tc):
    pre = (cp_ref[4:5, :] + cp_ref[3:4, :] * x + cp_ref[2:3, :] * _shift_down(x, xprev, 1)
           + cp_ref[1:2, :] * _shift_down(x, xprev, 2) + cp_ref[0:1, :] * _shift_down(x, xprev, 3))
    sig = _sigmoid(pre)
    raw = sm + hp_ref[0:1, :]
    dt = jnp.maximum(raw, 0.0) + _log1pexp(raw)
    a_neg = -jnp.exp(hp_ref[1:2, :])
    r = lax.broadcasted_iota(jnp.int32, (tc, tc), 0)
    c = lax.broadcasted_iota(jnp.int32, (tc, tc), 1)
    cs = _dot_hi(jnp.where(r >= c, 1.0, 0.0), dt * a_neg)
    return pre, sig, raw, dt, a_neg, cs, cs.T, r >= c


def _ssd_fwd(proj, cp, hp, ng, *, tc=256):
    L = proj.shape[0]
    nc = L // tc

    def body(xc_ref, xp_ref, z_ref, sm_ref, cp_ref, hp_ref, ng_ref, y_ref, ypre_ref, sin_ref, s_scr):
        i = pl.program_id(0)

        @pl.when(i == 0)
        def _():
            s_scr[...] = jnp.zeros_like(s_scr)
        x = xc_ref[...]
        xprev = xp_ref[...] * jnp.where(i > 0, 1.0, 0.0)
        pre, sig, _, dt, _, cs, cst, tril = _ssd_pre(x, xprev, sm_ref[...], cp_ref, hp_ref, tc)
        xbc = pre * sig
        sin_ref[...] = s_scr[...]
        for g in range(SSD_G):
            bg = xbc[:, SSD_W + g * SSD_N:SSD_W + (g + 1) * SSD_N]
            cg = xbc[:, SSD_W + SSD_G * SSD_N + g * SSD_N:SSD_W + SSD_G * SSD_N + (g + 1) * SSD_N].astype(bf16)
            cb = _dot_nt(cg, bg.astype(bf16))
            heads = [g * SSD_HPG + e for e in range(SSD_HPG)]
            hss = [slice(h * SSD_HD, (h + 1) * SSD_HD) for h in heads]
            xss = [xbc[:, hs] for hs in hss]
            cscs = [cs[:, h:h + 1] for h in heads]
            ms = [(cb * jnp.where(tril, jnp.exp(jnp.minimum(csc - cst[h:h + 1, :], 0.0)), 0.0)).astype(bf16)
                  for h, csc in zip(heads, cscs)]
            xdts = [(xs * dt[:, h:h + 1]).astype(bf16) for h, xs in zip(heads, xss)]
            s_hs = [s_scr[:, hs] for hs in hss]
            ys = [_dot(m, xdt) + jnp.exp(csc) * _dot(cg, s_h.astype(bf16))
                  for m, xdt, csc, s_h in zip(ms, xdts, cscs, s_hs)]
            cls = [cs[tc - 1:tc, h:h + 1] for h in heads]
            upd = [_dot_tn((bg * jnp.exp(cl - csc)).astype(bf16), xdt) for cl, csc, xdt in zip(cls, cscs, xdts)]
            for k, (h, hs) in enumerate(zip(heads, hss)):
                ypre_ref[:, hs] = ys[k] + hp_ref[2:3, h:h + 1] * xss[k]
                s_scr[:, hs] = jnp.exp(cls[k]) * s_hs[k] + upd[k]
        z = z_ref[...]
        yz = ypre_ref[...] * (z * _sigmoid(z))
        for g in range(SSD_G):
            gs = slice(g * SSD_GW, (g + 1) * SSD_GW)
            yg = yz[:, gs]
            r = lax.rsqrt(jnp.mean(yg * yg, -1, keepdims=True) + RMS_EPS)
            y_ref[:, gs] = (yg * r * ng_ref[0:1, gs]).astype(bf16)

    return pl.pallas_call(
        body, name="ssd_fwd", grid=(nc,),
        in_specs=[pl.BlockSpec((tc, 1024), lambda i: (i, 0)),
                  pl.BlockSpec((tc, 1024), lambda i: (jnp.maximum(i - 1, 0), 0)),
                  pl.BlockSpec((tc, SSD_W), lambda i: (i, P_Z // SSD_W)),
                  pl.BlockSpec((tc, 128), lambda i: (i, P_SM // 128)),
                  _const_spec((8, 1024)), _const_spec((8, 128)), _const_spec((8, SSD_W))],
        out_specs=[pl.BlockSpec((tc, SSD_W), lambda i: (i, 0)), pl.BlockSpec((tc, SSD_W), lambda i: (i, 0)),
                   pl.BlockSpec((SSD_N, SSD_W), lambda i: (i, 0))],
        out_shape=[jax.ShapeDtypeStruct((L, SSD_W), bf16), jax.ShapeDtypeStruct((L, SSD_W), f32),
                   jax.ShapeDtypeStruct((nc * SSD_N, SSD_W), f32)],
        scratch_shapes=[pltpu.VMEM((SSD_N, SSD_W), f32)],
        compiler_params=_cp(("arbitrary",)))(proj, proj, proj, proj, cp, hp, ng)


def _ssd_bwd(proj, dycat, ypre, sin, cp, hp, ng, *, tc=256):
    L = proj.shape[0]
    nc = L // tc

    def body(xc_ref, xp_ref, z_ref, sm_ref, cp_ref, hp_ref, ng_ref, sin_ref, ypre_ref, dy_ref,
             dxbc_ref, dz_ref, ddt_ref, acc1_ref, acc2_ref, ds_scr, dnext_scr, dxbc_scr):
        i = pl.program_id(0)
        c_idx = nc - 1 - i

        @pl.when(i == 0)
        def _():
            ds_scr[...] = jnp.zeros_like(ds_scr)
            dnext_scr[...] = jnp.zeros_like(dnext_scr)
            acc1_ref[...] = jnp.zeros_like(acc1_ref)
            acc2_ref[...] = jnp.zeros_like(acc2_ref)
        x = xc_ref[...]
        xprev = xp_ref[...] * jnp.where(c_idx > 0, 1.0, 0.0)
        pre, sig, raw, dt, a_neg, cs, cst, tril = _ssd_pre(x, xprev, sm_ref[...], cp_ref, hp_ref, tc)
        xbc = pre * sig
        z = z_ref[...]
        sz = _sigmoid(z)
        silz = z * sz
        yall = ypre_ref[...]
        yz = yall * silz
        dy = dy_ref[...]
        dyz_parts = []
        for g in range(SSD_G):
            gs = slice(g * SSD_GW, (g + 1) * SSD_GW)
            yg, dyg = yz[:, gs], dy[:, gs]
            r = lax.rsqrt(jnp.mean(yg * yg, -1, keepdims=True) + RMS_EPS)
            acc1_ref[5:6, gs] += jnp.sum(dyg * yg * r, 0, keepdims=True)
            dyn = dyg * ng_ref[0:1, gs]
            dyz_parts.append(r * (dyn - yg * (r * r) * jnp.mean(dyn * yg, -1, keepdims=True)))
        dyz = jnp.concatenate(dyz_parts, axis=1)
        dz_ref[...] = (dyz * yall * (sz * (1.0 + z * (1.0 - sz)))).astype(bf16)
        dyall = dyz * silz

        lane1 = lax.broadcasted_iota(jnp.int32, (1, 128), 1)
        sub = lax.broadcasted_iota(jnp.int32, (128, tc), 0)
        rowc = lax.broadcasted_iota(jnp.int32, (tc, 1), 0)
        dcs = jnp.zeros((tc, 128), f32)
        dcsr = jnp.zeros((128, tc), f32)
        ddt = jnp.zeros((tc, 128), f32)
        dd_row = jnp.zeros((1, 128), f32)
        for g in range(SSD_G):
            b0 = SSD_W + g * SSD_N
            c0 = SSD_W + SSD_G * SSD_N + g * SSD_N
            bg = xbc[:, b0:b0 + SSD_N]
            bgb = bg.astype(bf16)
            cgb = xbc[:, c0:c0 + SSD_N].astype(bf16)
            cb = _dot_nt(cgb, bgb)
            dbg = jnp.zeros((tc, SSD_N), f32)
            dcg = jnp.zeros((tc, SSD_N), f32)
            heads = [g * SSD_HPG + e for e in range(SSD_HPG)]
            hss = [slice(h * SSD_HD, (h + 1) * SSD_HD) for h in heads]
            ohs = [jnp.where(lane1 == h, 1.0, 0.0) for h in heads]
            xss = [xbc[:, hs] for hs in hss]
            dths = [dt[:, h:h + 1] for h in heads]
            cscs = [cs[:, h:h + 1] for h in heads]
            lms = [jnp.where(tril, jnp.exp(jnp.minimum(csc - cst[h:h + 1, :], 0.0)), 0.0) for h, csc in zip(heads, cscs)]
            ms = [cb * lm for lm in lms]
            xdts = [(xs * dth).astype(bf16) for xs, dth in zip(xss, dths)]
            s_hs = [sin_ref[:, hs] for hs in hss]
            s_hbs = [s_h.astype(bf16) for s_h in s_hs]
            dyhs = [dyall[:, hs] for hs in hss]
            dybs = [dyh.astype(bf16) for dyh in dyhs]
            ecss = [jnp.exp(csc) for csc in cscs]
            cs_prods = [_dot(cgb, s_hb) for s_hb in s_hbs]
            dcsbs = [(ecs * dyh).astype(bf16) for ecs, dyh in zip(ecss, dyhs)]
            dcg_a = [_dot_nt(dcsb, s_hb) for dcsb, s_hb in zip(dcsbs, s_hbs)]
            ds_ins = [_dot_tn(cgb, dcsb) for dcsb in dcsbs]
            dms = [_dot_nt(dyb, xdt) for dyb, xdt in zip(dybs, xdts)]
            ws = [dm * m for dm, m in zip(dms, ms)]
            dcbbs = [(dm * lm).astype(bf16) for dm, lm in zip(dms, lms)]
            dcg_b = [_dot(dcbb, bgb) for dcbb in dcbbs]
            dbg_a = [_dot_tn(dcbb, cgb) for dcbb in dcbbs]
            dxdts = [_dot_tn(m.astype(bf16), dyb) for m, dyb in zip(ms, dybs)]
            dsns = [ds_scr[:, hs] for hs in hss]
            dsnbs = [dsn.astype(bf16) for dsn in dsns]
            cls = [cs[tc - 1:tc, h:h + 1] for h in heads]
            decs = [jnp.exp(cl - csc) for cl, csc in zip(cls, cscs)]
            dxdts = [dxdt + _dot((bg * dec).astype(bf16), dsnb) for dxdt, dec, dsnb in zip(dxdts, decs, dsnbs)]
            dbds = [_dot_nt(xdt, dsnb) for xdt, dsnb in zip(xdts, dsnbs)]
            gdecs = [jnp.sum(dbd * bg, -1, keepdims=True) * dec for dbd, dec in zip(dbds, decs)]
            ecls = [jnp.exp(cl) for cl in cls]
            for k, (h, hs) in enumerate(zip(heads, hss)):
                dd_row = dd_row + ohs[k] * jnp.sum(dyhs[k] * xss[k])
                dcg = dcg + dcg_a[k] + dcg_b[k]
                dbg = dbg + dbg_a[k] + dbds[k] * decs[k]
                dcl = jnp.sum(gdecs[k]) + jnp.sum(dsns[k] * s_hs[k]) * ecls[k]
                ds_scr[:, hs] = ecls[k] * dsns[k] + ds_ins[k]
                dcs_h = (jnp.sum(dyhs[k] * ecss[k] * cs_prods[k], -1, keepdims=True)
                         + jnp.sum(ws[k], -1, keepdims=True) - gdecs[k] + jnp.where(rowc == tc - 1, dcl, 0.0))
                dcs = dcs + dcs_h * ohs[k]
                dcsr = jnp.where(sub == h, jnp.sum(ws[k], 0, keepdims=True), dcsr)
                dxbc_scr[:, hs] = hp_ref[2:3, h:h + 1] * dyhs[k] + dxdts[k] * dths[k]
                ddt = ddt + jnp.sum(dxdts[k] * xss[k], -1, keepdims=True) * ohs[k]
            dxbc_scr[:, b0:b0 + SSD_N] = dbg
            dxbc_scr[:, c0:c0 + SSD_N] = dcg
        dcs = dcs - dcsr.T
        r_i = lax.broadcasted_iota(jnp.int32, (tc, tc), 0)
        c_i = lax.broadcasted_iota(jnp.int32, (tc, tc), 1)
        da = _dot_hi(jnp.where(r_i <= c_i, 1.0, 0.0), dcs)
        ddt = ddt + da * a_neg
        acc2_ref[1:2, :] += jnp.sum(da * dt, 0, keepdims=True) * a_neg
        lane = lax.broadcasted_iota(jnp.int32, (tc, 128), 1)
        ddraw = jnp.where(lane < SSD_H, ddt * _sigmoid(raw), 0.0)
        acc2_ref[0:1, :] += jnp.sum(ddraw, 0, keepdims=True)
        acc2_ref[2:3, :] += dd_row
        ddt_ref[...] = ddraw
        dpre = dxbc_scr[...] * (sig * (1.0 + pre * (1.0 - sig)))
        acc1_ref[4:5, :] += jnp.sum(dpre, 0, keepdims=True)
        for k in range(SSD_K):
            acc1_ref[k:k + 1, :] += jnp.sum(dpre * _shift_down(x, xprev, SSD_K - 1 - k), 0, keepdims=True)
        dnext = dnext_scr[...]
        dxbc_ref[...] = (cp_ref[3:4, :] * dpre + cp_ref[2:3, :] * _shift_up(dpre, dnext, 1)
                         + cp_ref[1:2, :] * _shift_up(dpre, dnext, 2)
                         + cp_ref[0:1, :] * _shift_up(dpre, dnext, 3)).astype(bf16)
        dnext_scr[...] = dpre

    rev = lambda w_, col: pl.BlockSpec((tc, w_), lambda i: (nc - 1 - i, col // w_))
    return pl.pallas_call(
        body, name="ssd_bwd", grid=(nc,),
        in_specs=[rev(1024, 0), pl.BlockSpec((tc, 1024), lambda i: (jnp.maximum(nc - 2 - i, 0), 0)),
                  rev(SSD_W, P_Z), rev(128, P_SM),
                  _const_spec((8, 1024)), _const_spec((8, 128)), _const_spec((8, SSD_W)),
                  pl.BlockSpec((SSD_N, SSD_W), lambda i: (nc - 1 - i, 0)), rev(SSD_W, 0), rev(SSD_W, 0)],
        out_specs=[rev(1024, 0), rev(SSD_W, 0), rev(128, 0),
                   pl.BlockSpec((8, 1024), lambda i: (0, 0)), pl.BlockSpec((8, 128), lambda i: (0, 0))],
        out_shape=[jax.ShapeDtypeStruct((L, 1024), bf16), jax.ShapeDtypeStruct((L, SSD_W), bf16),
                   jax.ShapeDtypeStruct((L, 128), f32), jax.ShapeDtypeStruct((8, 1024), f32),
                   jax.ShapeDtypeStruct((8, 128), f32)],
        scratch_shapes=[pltpu.VMEM((SSD_N, SSD_W), f32), pltpu.VMEM((tc, 1024), f32), pltpu.VMEM((tc, 1024), f32)],
        compiler_params=_cp(("arbitrary",)))(proj, proj, proj, proj, cp, hp, ng, sin, ypre, dycat)


def _pack_cols(w):
    pad = jnp.zeros(w.shape[:-1] + (P_W - P_SM - SSD_H - FOX_H,), w.dtype)
    return jnp.concatenate([w[..., 512:1536], w[..., 0:512], w[..., 1544:2312], w[..., 2316:3084],
                            w[..., 1536:1544], w[..., 2312:2316], pad], axis=-1)


def _unpack_cols(g):
    return jnp.concatenate([g[..., 1024:1536], g[..., 0:1024], g[..., 3072:3080], g[..., 1536:2304],
                            g[..., 3080:3084], g[..., 2304:3072]], axis=-1)


def _rows8(*rows):
    width = max(r.shape[-1] for r in rows)
    out = [jnp.pad(r.astype(f32), (0, width - r.shape[-1])) for r in rows]
    out += [jnp.zeros((width,), f32)] * (8 - len(out))
    return jnp.stack(out)


def _local_step(x, tgt, p, weights_of, grads_done):
    gb_in = _rows8(p["ln_in_g"], p["ln_in_b"])
    cst = _fox_consts()
    rows_to_8 = lambda t: jnp.pad(t, [(0, 0)] * (t.ndim - 2) + [(0, 8 - t.shape[-2]), (0, 0)])
    lanes = lambda t, lo: jnp.pad(t, ((0, 0), (lo, 128 - lo - t.shape[-1])))[:, None, :]
    mvs = rows_to_8(jnp.concatenate([p["mod"].reshape(DEPTH, 3, 3, D), p["ln_g"][:, :, None], p["ln_b"][:, :, None]], 2))
    cps = rows_to_8(jnp.concatenate([p["ssd_conv_w"], p["ssd_conv_b"][:, None]], 1))
    hps = rows_to_8(jnp.concatenate([lanes(p["ssd_dt_bias"], 0), lanes(p["ssd_a_log"], 0), lanes(p["ssd_d"], 0),
                                     lanes(p["fox_f_bias"], SM_F)], 1))
    ngs = rows_to_8(p["ssd_norm_g"][:, None])
    scws = rows_to_8(p["sconv_w"])
    x0 = _ln_in_fwd(x, gb_in)
    saved = []
    for l in range(DEPTH):
        mv = [mvs[l, j] for j in range(3)]
        cp, hp, ng, scw = cps[l], hps[l], ngs[l], scws[l]
        w = dict(weights_of(l, "ffn1", x0))
        x1 = _ffn_fwd(x0, mv[0], w["ffn1_w_in"], w["ffn1_w_out"])
        w.update(weights_of(l, "mix", x1))
        proj = _inproj_fwd(x1, mv[1], w["mix_w_in"])
        y_ssd, ypre, sin = _ssd_fwd(proj, cp, hp, ng)
        qa, ka, va, tot = _fox_prep(proj, hp, cst)
        tot = tot[::8, SM_F:SM_F + FOX_H]
        o, o_acc, lse = _fox_attn_fwd(qa, ka, va, tot)
        y_sc = _sconv_fwd(proj, scw)
        ycat = jnp.concatenate([y_ssd, o, y_sc], axis=1)
        x2 = _outproj_fwd(x1, ycat, mv[1], w["mix_w_out"])
        w.update(weights_of(l, "ffn2", x2))
        x3 = _ffn_fwd(x2, mv[2], w["ffn2_w_in"], w["ffn2_w_out"])
        saved.append((x0, x1, x2, mv, cp, hp, ng, scw, proj, ypre, sin, qa, ka, va, tot, o_acc, lse, ycat, w))
        x0 = x3
    dx, loss_acc = _loss_head(x0, tgt)

    g = {k: [None] * DEPTH for k in (
        "mod", "ln_g", "ln_b", "ssd_conv_w", "ssd_conv_b", "ssd_dt_bias", "ssd_a_log", "ssd_d", "ssd_norm_g",
        "fox_f_bias", "sconv_w")}
    def behind(small, tok):
        return small if tok is None else small + tok[0, 0]

    token = None
    for l in reversed(range(DEPTH)):
        x0, x1, x2, mv, cp, hp, ng, scw, proj, ypre, sin, qa, ka, va, tot, o_acc, lse, ycat, w = saved[l]
        dx, h, dat, actt, dyb, a2 = _ffn_bwd(x2, dx, behind(mv[2], token), w["ffn2_w_in"], w["ffn2_w_out"])
        token = grads_done(l, "ffn2_w_out", {"ffn2_w_out": _matmul_tokens(actt, dyb, tn=D, tk=DW_TK, name="dw_ffn_out")})
        token = grads_done(l, "ffn2_w_in", {"ffn2_w_in": _matmul_tokens(dat, h, tn=D, tk=DW_TK, name="dw_ffn_in",
                                                                        after=token)})
        mv1 = behind(mv[1], token)
        dxp, dyb, dycat, ycat_t, a1 = _outproj_bwd(x1, ycat, dx, mv1, w["mix_w_out"])
        gw_mix_out = _matmul_tokens(ycat_t, dyb, tn=D // 2, tk=DW_TK, name="dw_mix_out")
        dxbc, dz, ddt, acc1, acc2 = _ssd_bwd(proj, dycat, ypre, sin, cp, hp, ng)
        qb, doa, qbt, doat = _fox_bprep(qa, dycat, o_acc, lse, cst)
        dq, dkt, dvt = _fox_attn_bwd(ka, va, tot, qb, doa, qbt, doat)
        dq, dk, dv, dsm, accf = _fox_post(dq, dkt, dvt, proj, ddt, hp, cst)
        dsb, dsc, dsx, accs = _sconv_bwd(proj, dycat, scw)
        dproj = jnp.concatenate([dxbc, dz, dq, dk, dv, dsb, dsc, dsx, dsm], axis=1)
        dx, ht, a1b = _inproj_bwd(x1, dxp, dproj, mv1, w["mix_w_in"])
        token = grads_done(l, "mix", {"mix_w_in": _matmul_tokens(ht, dproj, tn=P_W // 5, tk=DW_TK, name="dw_mix_in"),
                                      "mix_w_out": gw_mix_out})
        dx, h, dat, actt, dyb, a0 = _ffn_bwd(x0, dx, behind(mv[0], token), w["ffn1_w_in"], w["ffn1_w_out"])
        token = grads_done(l, "ffn1_w_out", {"ffn1_w_out": _matmul_tokens(actt, dyb, tn=D, tk=DW_TK, name="dw_ffn_out")})
        token = grads_done(l, "ffn1_w_in", {"ffn1_w_in": _matmul_tokens(dat, h, tn=D, tk=DW_TK, name="dw_ffn_in",
                                                                        after=token)})
        g["mod"][l] = jnp.concatenate([a0[0:3], a1b[0:2], a1[2:3], a2[0:3]], axis=0)
        g["ln_g"][l] = jnp.stack([a0[3], a1[3], a2[3]])
        g["ln_b"][l] = jnp.stack([a0[4], a1[4], a2[4]])
        g["ssd_conv_w"][l] = acc1[0:SSD_K]
        g["ssd_conv_b"][l] = acc1[4]
        g["ssd_norm_g"][l] = acc1[5, :SSD_W]
        g["ssd_dt_bias"][l] = acc2[0, :SSD_H]
        g["ssd_a_log"][l] = acc2[1, :SSD_H]
        g["ssd_d"][l] = acc2[2, :SSD_H]
        g["fox_f_bias"][l] = accf[3, SM_F:SM_F + FOX_H]
        g["sconv_w"][l] = accs[0:SC_K]
    grad_x, a_in = _ln_in_bwd(x, dx, behind(gb_in, token))
    g = {k: jnp.stack(v) for k, v in g.items()}
    g["ln_in_g"], g["ln_in_b"] = a_in[0], a_in[1]
    return loss_acc[0, 0], grad_x, g


MESH = pl.DeviceIdType.MESH
ANY = pl.BlockSpec(memory_space=pl.ANY)


def _all_gather(shards, *, in_vmem, name):
    n_arr = len(shards)

    def body(*refs):
        x_refs, out_refs = refs[:n_arr], refs[n_arr:2 * n_arr]
        send_sems, recv_sems, local_sems = refs[2 * n_arr:]
        x, y, c = lax.axis_index("x"), lax.axis_index("y"), lax.axis_index("c")
        me, sibling = (x, y, c), (x, y, 1 - c)
        chips = [(1 - x, y), (x, 1 - y), (1 - x, 1 - y)]

        def copy(a, k, block, to, src=None):
            px, py, pc = block
            slot = out_refs[a].at[4 * px + 2 * py + pc]
            return pltpu.make_async_remote_copy(
                src_ref=slot if src is None else src, dst_ref=slot,
                send_sem=send_sems.at[7 * a + k], recv_sem=recv_sems.at[7 * a + k], device_id=to, device_id_type=MESH)

        mine, first, passed = [], [], []
        for a in range(n_arr):
            mine.append(pltpu.make_async_copy(x_refs[a], out_refs[a].at[4 * x + 2 * y + c], local_sems.at[a]))
            mine[-1].start()
            first.append(copy(a, 0, me, sibling, src=x_refs[a]))
            first += [copy(a, 1 + j, me, (*chip, c), src=x_refs[a]) for j, chip in enumerate(chips)]
        for cp in first:
            cp.start()
        for j, chip in enumerate(chips):
            for a in range(n_arr):
                copy(a, 1 + j, (*chip, c), me).wait_recv()
                passed.append(copy(a, 4 + j, (*chip, c), sibling))
                passed[-1].start()
        for a in range(n_arr):
            copy(a, 0, sibling, me).wait_recv()
            for j, chip in enumerate(chips):
                copy(a, 4 + j, (*chip, 1 - c), me).wait_recv()
        for cp in first + passed:
            cp.wait_send()
        for cp in mine:
            cp.wait()

    spec = pl.BlockSpec(memory_space=pltpu.VMEM) if in_vmem else ANY
    return pl.pallas_call(
        body, name=name, out_shape=[jax.ShapeDtypeStruct((N_DEV,) + s.shape, s.dtype) for s in shards],
        in_specs=[spec] * n_arr, out_specs=[spec] * n_arr,
        scratch_shapes=[pltpu.SemaphoreType.DMA((7 * n_arr,)), pltpu.SemaphoreType.DMA((7 * n_arr,)),
                        pltpu.SemaphoreType.DMA((n_arr,))],
    )(*shards)


HBM = pl.BlockSpec(memory_space=pltpu.HBM)
SEM = pl.BlockSpec(memory_space=pltpu.SEMAPHORE)
EFFECT = pltpu.SideEffectType.DATAFLOW_SIDE_EFFECTING


def _spread_copies(srcs, lands, send_sems, recv_sems, local_sems, scatter):
    x, y, c = lax.axis_index("x"), lax.axis_index("y"), lax.axis_index("c")
    me = 4 * x + 2 * y + c
    local, remote = [], []
    for a in range(len(srcs)):
        own = srcs[a].at[me] if scatter else srcs[a]
        local.append(pltpu.make_async_copy(own, lands[a].at[me], local_sems.at[a]))
        for r in range(1, N_DEV):
            px, py, pc = (1 - x if r & 4 else x), (1 - y if r & 2 else y), (1 - c if r & 1 else c)
            peer = 4 * px + 2 * py + pc
            k = (N_DEV - 1) * a + r - 1
            mk = functools.partial(pltpu.make_async_remote_copy, send_sem=send_sems.at[k], recv_sem=recv_sems.at[k],
                                   device_id=(px, py, pc), device_id_type=MESH)
            remote.append((mk(src_ref=srcs[a].at[peer] if scatter else srcs[a], dst_ref=lands[a].at[me]),
                           mk(src_ref=own, dst_ref=lands[a].at[peer])))
    return local, remote


def _spread_start(srcs, *, scatter, name, after=()):
    n, k = len(srcs), len(after)
    lands = [jax.ShapeDtypeStruct((N_DEV,) + s.shape[-2:], s.dtype) for s in srcs]

    def body(*refs):
        src, land = refs[:n], refs[n:2 * n]
        send_sems, recv_sems, local_sems = refs[2 * n + k:2 * n + k + 3]
        token = refs[-1]
        local, remote = _spread_copies(src, land, send_sems, recv_sems, local_sems, scatter)
        for cp in local:
            cp.start()
        for cp, _ in remote:
            cp.start()
        token[...] = jnp.zeros_like(token)

    nsem = (N_DEV - 1) * n
    out = pl.pallas_call(
        body, name=name,
        out_shape=(pltpu.SemaphoreType.DMA((nsem,)), pltpu.SemaphoreType.DMA((nsem,)), pltpu.SemaphoreType.DMA((n,)),
                   *[pltpu.HBM(s.shape, s.dtype) for s in srcs], *[pltpu.HBM(s.shape, s.dtype) for s in lands],
                   jax.ShapeDtypeStruct((8, 128), f32)),
        in_specs=[HBM] * (2 * n) + [ANY] * k,
        out_specs=(SEM, SEM, SEM, *[HBM] * (2 * n), pl.BlockSpec(memory_space=pltpu.VMEM)),
        input_output_aliases={i: 3 + i for i in range(2 * n)},
        compiler_params=pltpu.CompilerParams(has_side_effects=EFFECT),
    )(*[pltpu.with_memory_space_constraint(s, pltpu.HBM) for s in srcs],
      *[pltpu.with_memory_space_constraint(lax.empty(s.shape, s.dtype), pltpu.HBM) for s in lands], *after)
    return out[:-1], out[-1]


def _spread_wait(state, after, *, scatter, name):
    n = (len(state) - 3) // 2
    sems, thru = state[:3], state[3:]

    def body(*refs):
        src, land = refs[:n], refs[n:2 * n]
        send_sems, recv_sems, local_sems = refs[2 * n:2 * n + 3]
        local, remote = _spread_copies(src, land, send_sems, recv_sems, local_sems, scatter)
        for sent, received in remote:
            sent.wait_send()
            received.wait_recv()
        for cp in local:
            cp.wait()

    out = pl.pallas_call(
        body, name=name,
        out_shape=tuple(pltpu.HBM(t.shape, t.dtype) for t in thru),
        in_specs=[HBM] * (2 * n) + [SEM] * 3 + [ANY], out_specs=tuple([HBM] * (2 * n)),
        input_output_aliases={i: i for i in range(2 * n)},
        compiler_params=pltpu.CompilerParams(has_side_effects=EFFECT),
    )(*thru, *sems, after)
    return list(out[n:])


def _sum_slots(buf, *, tr, name):
    nb, r, n = buf.shape

    def body(b_ref, o_ref):
        acc = b_ref[0].astype(f32)
        for k in range(1, nb):
            acc = acc + b_ref[k].astype(f32)
        o_ref[...] = acc

    return pl.pallas_call(
        body, name=name, grid=(r // tr,),
        in_specs=[pl.BlockSpec((nb, tr, n), lambda i: (0, i, 0))],
        out_specs=pl.BlockSpec((tr, n), lambda i: (i, 0)),
        out_shape=jax.ShapeDtypeStruct((r, n), f32), compiler_params=_cp(("parallel",)))(buf)


def _ada_fwd(c_all, ada_w, ada_b_cols):
    n = ada_w.shape[-1]

    def body(c_ref, w_ref, b_ref, o_ref):
        cv = c_ref[...]
        ca = (cv * _sigmoid(cv)).astype(bf16)
        o_ref[...] = _dot(ca, w_ref[...].astype(bf16)) + b_ref[...]

    return pl.pallas_call(
        body, name="ada_fwd", grid=(DEPTH,),
        in_specs=[_const_spec((N_DEV, D)), pl.BlockSpec((None, D, n), lambda l: (l, 0, 0)),
                  pl.BlockSpec((None, 1, n), lambda l: (l, 0, 0))],
        out_specs=pl.BlockSpec((None, N_DEV, n), lambda l: (l, 0, 0)),
        out_shape=jax.ShapeDtypeStruct((DEPTH, N_DEV, n), f32), compiler_params=_cp(("parallel",)))(c_all, ada_w, ada_b_cols)


def _ada_bwd(c_all, dmod_cols):
    n = dmod_cols.shape[-1]

    def body(c_ref, d_ref, o_ref):
        cv = c_ref[...]
        ca = (cv * _sigmoid(cv)).astype(bf16)
        o_ref[...] = _dot_tn(ca, d_ref[...].astype(bf16))

    return pl.pallas_call(
        body, name="ada_bwd", grid=(DEPTH,),
        in_specs=[_const_spec((N_DEV, D)), pl.BlockSpec((None, N_DEV, n), lambda l: (l, 0, 0))],
        out_specs=pl.BlockSpec((None, D, n), lambda l: (l, 0, 0)),
        out_shape=jax.ShapeDtypeStruct((DEPTH, D, n), f32), compiler_params=_cp(("parallel",)))(c_all, dmod_cols)


def _adamw(w, g, m, v, *, tr, name):
    r, n = w.shape

    def body(w_ref, g_ref, m_ref, v_ref, d_ref, mo_ref, vo_ref):
        g_ = g_ref[...]
        m_ = ADAM_B1 * m_ref[...] + (1.0 - ADAM_B1) * g_
        v_ = ADAM_B2 * v_ref[...] + (1.0 - ADAM_B2) * jnp.square(g_)
        m_hat = m_ / (1.0 - ADAM_B1 ** ADAM_STEP)
        v_hat = v_ / (1.0 - ADAM_B2 ** ADAM_STEP)
        d_ref[...] = -ADAM_LR * (m_hat / (jnp.sqrt(v_hat) + ADAM_EPS) + ADAM_WD * w_ref[...])
        mo_ref[...] = m_
        vo_ref[...] = v_

    blk = pl.BlockSpec((tr, n), lambda i: (i, 0))
    return pl.pallas_call(
        body, name=name, grid=(r // tr,), in_specs=[blk] * 4, out_specs=[blk] * 3,
        out_shape=[jax.ShapeDtypeStruct((r, n), f32)] * 3, compiler_params=_cp(("parallel",)))(w, g, m, v)


def _adamw_landed(w, m, v, land0, land1, *, tr, name):
    _, r, n = w.shape
    nrt = r // tr

    def body(w_ref, m_ref, v_ref, l0_ref, l1_ref, g_ref, d_ref, mo_ref, vo_ref):
        layer = pl.program_id(0)

        def total(ref):
            acc = ref[0].astype(f32)
            for k in range(1, N_DEV):
                acc = acc + ref[k].astype(f32)
            return acc

        @pl.when(layer == 0)
        def _():
            g_ref[...] = total(l0_ref)

        @pl.when(layer == 1)
        def _():
            g_ref[...] = total(l1_ref)
        g_ = g_ref[...]
        m_ = ADAM_B1 * m_ref[...] + (1.0 - ADAM_B1) * g_
        v_ = ADAM_B2 * v_ref[...] + (1.0 - ADAM_B2) * jnp.square(g_)
        m_hat = m_ / (1.0 - ADAM_B1 ** ADAM_STEP)
        v_hat = v_ / (1.0 - ADAM_B2 ** ADAM_STEP)
        d_ref[...] = -ADAM_LR * (m_hat / (jnp.sqrt(v_hat) + ADAM_EPS) + ADAM_WD * w_ref[...])
        mo_ref[...] = m_
        vo_ref[...] = v_

    blk = pl.BlockSpec((None, tr, n), lambda l, i: (l, i, 0))
    land0_spec = pl.BlockSpec((N_DEV, tr, n), lambda l, i: (0, jnp.where(l == 0, i, nrt - 1), 0))
    land1_spec = pl.BlockSpec((N_DEV, tr, n), lambda l, i: (0, jnp.where(l == 1, i, 0), 0))
    return pl.pallas_call(
        body, name=name, grid=(DEPTH, nrt), in_specs=[blk] * 3 + [land0_spec, land1_spec], out_specs=[blk] * 4,
        out_shape=[jax.ShapeDtypeStruct(w.shape, f32)] * 4,
        compiler_params=_cp(("arbitrary", "arbitrary")))(w, m, v, land0, land1)


WEIGHTS = ["ln_in_g", "ln_in_b", "ada_w", "ada_b", "ffn1_w_in", "ffn1_w_out", "mix_w_in", "mix_w_out", "ssd_conv_w",
           "ssd_conv_b", "ssd_dt_bias", "ssd_a_log", "ssd_d", "ssd_norm_g", "fox_f_bias", "sconv_w", "ffn2_w_in",
           "ffn2_w_out", "ln_g", "ln_b"]
BIG = ["ffn1_w_in", "ffn1_w_out", "ffn2_w_in", "ffn2_w_out", "mix_w_in", "mix_w_out"]
GROUPS = {"ffn1": ["ffn1_w_in", "ffn1_w_out"], "mix": ["mix_w_in", "mix_w_out"], "ffn2": ["ffn2_w_in", "ffn2_w_out"]}
COL_SHARDED = ("ffn1_w_in", "ffn2_w_in")
SMALL_SHARDED = {"ssd_conv_w": 128, "sconv_w": 32, "ln_g": 128, "ln_b": 128}
ADAM_TR = {"ada_w": 256, "ffn1_w_in": 512, "ffn2_w_in": 512, "ffn1_w_out": 352, "ffn2_w_out": 352, "mix_w_in": 64,
           "mix_w_out": 256}
LAND_TR = {"ffn1_w_in": 176, "ffn2_w_in": 176, "ffn1_w_out": 176, "ffn2_w_out": 176, "mix_w_in": 32, "mix_w_out": 128}


def _pad_rows(v, mult=128):
    v = v.reshape(-1)
    return jnp.pad(v, (0, (-v.shape[0]) % mult))


def _pack_rows(parts, row_mult=8):
    flat = [_pad_rows(p.astype(f32)) for p in parts]
    offs, o = [], 0
    for f in flat:
        offs.append(o)
        o += f.shape[0] // 128
    buf = jnp.concatenate(flat).reshape(-1, 128)
    return jnp.pad(buf, ((0, (-buf.shape[0]) % row_mult), (0, 0))), offs


def _take(buf, off, shape):
    n = 1
    for s in shape:
        n *= s
    rows = -(-n // 128)
    lead = buf.shape[:-2]
    flat = buf[..., off:off + rows, :].reshape(lead + (rows * 128,))
    return flat[..., :n].reshape(lead + tuple(shape))


def kernel(*args):
    names = (["x", "c"] + WEIGHTS + ["loss_target"] + ["m_" + n for n in WEIGHTS] + ["v_" + n for n in WEIGHTS])
    assert len(args) == len(names)
    a = dict(zip(names, args))
    xi, yi, ci = lax.axis_index("x"), lax.axis_index("y"), lax.axis_index("c")
    me = 4 * xi + 2 * yi + ci

    small_in = [a["c"], a["ln_g"], a["ln_b"], a["ssd_conv_w"], a["sconv_w"]]
    buf, offs = _pack_rows(small_in)
    got, = _all_gather([buf], in_vmem=True, name="gather_small")
    c_all = _take(got, offs[0], (D,))
    full = {}
    for k, n in enumerate(["ln_g", "ln_b", "ssd_conv_w", "sconv_w"]):
        sh = a[n].shape
        t = _take(got, offs[k + 1], sh)
        full[n] = jnp.transpose(t, (1, 2, 0, 3)).reshape(sh[0], sh[1], N_DEV * sh[2])

    ncol = a["ada_w"].shape[-1]
    ada_b_cols = lax.dynamic_slice_in_dim(a["ada_b"], me * ncol, ncol, axis=1)[:, None, :]
    mod_cols = _ada_fwd(c_all, a["ada_w"], ada_b_cols)
    got, = _all_gather([mod_cols.reshape(DEPTH * N_DEV, ncol)], in_vmem=True, name="gather_mod")
    got = got.reshape(N_DEV, DEPTH, N_DEV, ncol)
    mod = lax.dynamic_index_in_dim(got, me, axis=2, keepdims=False)
    mod = jnp.transpose(mod, (1, 0, 2)).reshape(DEPTH, 9, D)

    flip = lambda t: jnp.swapaxes(t, 1, 2)
    shards = {n: (flip(a[n]) if n in COL_SHARDED else a[n]) for n in BIG}
    shards["mix_w_in"] = _pack_cols(a["mix_w_in"])
    def as_weights(names, got):
        return {n: (t if n in COL_SHARDED else t.reshape(-1, t.shape[-1])) for n, t in zip(names, got)}

    def group_shards(l, gn):
        return [shards[n][l].astype(bf16) for n in GROUPS[gn]]

    w_first = _all_gather(group_shards(0, "ffn1"), in_vmem=False, name="gather_weights")
    w_state, after = {}, [w_first[0]]
    for l, gn in [(0, "mix"), (0, "ffn2"), (1, "ffn1"), (1, "mix"), (1, "ffn2")]:
        w_state[l, gn], token = _spread_start(group_shards(l, gn), scatter=False, after=after,
                                              name=f"weights_{l}{gn}_start")
        after = [token]

    def weights_of(l, gn, after):
        if (l, gn) == (0, "ffn1"):
            return as_weights(GROUPS[gn], w_first)
        return as_weights(GROUPS[gn], _spread_wait(w_state[l, gn], after, scatter=False, name=f"weights_{l}{gn}_wait"))

    g_state = {}

    last_send = {}

    def grads_done(l, gn, gw, after=()):
        if (l, gn) == (0, "ffn1_w_in") and not after:
            last_send.update(gw)
            return None
        srcs = [t.reshape((N_DEV,) + shards[n].shape[1:]) for n, t in gw.items()]
        g_state[l, gn], token = _spread_start(srcs, scatter=True, after=after, name=f"grads_{l}{gn}_start")
        return token

    p = {n: a[n] for n in ("ln_in_g", "ln_in_b", "ssd_conv_b", "ssd_dt_bias", "ssd_a_log", "ssd_d", "ssd_norm_g",
                           "fox_f_bias")}
    p.update(full)
    p["mod"] = mod + after[0][0, 0]

    loss_local, grad_x, g = _local_step(a["x"][0], a["loss_target"][0], p, weights_of, grads_done)
    loss = lax.psum(loss_local, ("x", "y", "c"))

    small_names = ["mod", "ln_in_g", "ln_in_b", "ssd_conv_b", "ssd_dt_bias", "ssd_a_log", "ssd_d", "ssd_norm_g",
                   "fox_f_bias", "ln_g", "ln_b", "ssd_conv_w", "sconv_w"]
    buf, offs = _pack_rows([g[n] for n in small_names])
    got, = _all_gather([buf], in_vmem=True, name="gather_small_grads")
    tot = _sum_slots(got, tr=buf.shape[0], name="sum_small_grads")
    grads_done(0, "ffn1_w_in", last_send, after=(tot,))
    grads = {}
    for k, n in enumerate(small_names[1:], start=1):
        t = _take(tot, offs[k], g[n].shape)
        if n in SMALL_SHARDED:
            w_ = SMALL_SHARDED[n]
            t = lax.dynamic_slice_in_dim(t, me * w_, w_, axis=2)
        grads[n] = t
    grads["ada_b"] = _take(tot, offs[0], (DEPTH, 9 * D))
    dmod_all = _take(got, offs[0], (DEPTH, 9 * D))
    dmod_cols = jnp.transpose(lax.dynamic_slice_in_dim(dmod_all, me * ncol, ncol, axis=2), (1, 0, 2))
    grads["ada_w"] = _ada_bwd(c_all, dmod_cols)

    delta, new_m, new_v = {}, {}, {}

    def adamw(n):
        sh = a[n].shape
        two = lambda t: t.reshape(-1, sh[-1])
        outs = _adamw(two(a[n]), two(grads[n]), two(a["m_" + n]), two(a["v_" + n]), tr=ADAM_TR[n], name="adamw_" + n)
        delta[n], new_m[n], new_v[n] = (t.reshape(sh) for t in outs)

    small_params = [n for n in WEIGHTS if n not in ADAM_TR]
    packs = [_pack_rows([src[pre + n] for n in small_params])[0]
             for src, pre in ((a, ""), (grads, ""), (a, "m_"), (a, "v_"))]
    _, offs = _pack_rows([a[n] for n in small_params])
    outs = _adamw(*packs, tr=packs[0].shape[0], name="adamw_small")
    for k, n in enumerate(small_params):
        delta[n], new_m[n], new_v[n] = (_take(t, offs[k], a[n].shape) for t in outs)
    adamw("ada_w")
    after = grad_x
    for gn, names in [("ffn2_w_out", ["ffn2_w_out"]), ("ffn2_w_in", ["ffn2_w_in"]), ("mix", GROUPS["mix"]),
                      ("ffn1_w_out", ["ffn1_w_out"]), ("ffn1_w_in", ["ffn1_w_in"])]:
        lands = [_spread_wait(g_state[l, gn], after if l == 0 else grad_x, scatter=True, name=f"grads_{l}{gn}_wait")
                 for l in range(DEPTH)]
        for k, n in enumerate(names):
            if n == "mix_w_in":
                sums = [_sum_slots(t[k], tr=LAND_TR[n], name="sum_devices") for t in lands]
                grads[n] = _unpack_cols(jnp.stack(sums))
                adamw(n)
            elif n in COL_SHARDED:
                grads[n], delta[n], new_m[n], new_v[n] = (flip(t) for t in _adamw_landed(
                    flip(a[n]), flip(a["m_" + n]), flip(a["v_" + n]), lands[0][k], lands[1][k], tr=LAND_TR[n],
                    name="adamw_" + n))
            else:
                grads[n], delta[n], new_m[n], new_v[n] = _adamw_landed(
                    a[n], a["m_" + n], a["v_" + n], lands[0][k], lands[1][k], tr=LAND_TR[n], name="adamw_" + n)
            after = delta[n]

    return (loss, grad_x[None], *[grads[n] for n in WEIGHTS], *[delta[n] for n in WEIGHTS],
            *[new_m[n] for n in WEIGHTS], *[new_v[n] for n in WEIGHTS])
```
